```python
import jax, jax.numpy as jnp
from jax import lax
import numpy as np

D_MODEL = 1024
BATCH = 16
SEQ = 2048
DEPTH = 2

A_GROUPS = 8
A_GROUP_DIM = D_MODEL // 16
A_DIM = A_GROUPS * A_GROUP_DIM
B_GROUPS = 8
B_GROUP_DIM = D_MODEL // 16
B_DIM = B_GROUPS * B_GROUP_DIM
MIX_DIM = A_DIM + B_DIM
IN_EVEN = 2 * A_DIM + 3 * B_DIM
A_CONV_WIDTH = 31
B_CONV_WIDTH = 3
CHUNK = 128
C_GROUPS = 8
C_GROUP_DIM = D_MODEL // 8
C_DIM = C_GROUPS * C_GROUP_DIM
D_FF = 4 * D_MODEL
N_EVEN = (DEPTH + 1) // 2
N_ODD = DEPTH // 2
RMS_EPS = 1e-6
LN_EPS = 1e-5

kernel_name = "hybrid_conformer_shortconv_gmlp_trunk"


def rms_norm(x, g):
    xf = x.astype(jnp.float32)
    y = xf * lax.rsqrt(jnp.mean(xf * xf, axis=-1, keepdims=True) + RMS_EPS)
    return (y * g.astype(jnp.float32)).astype(x.dtype)


def layer_norm(x, g, b):
    xf = x.astype(jnp.float32)
    mu = jnp.mean(xf, axis=-1, keepdims=True)
    xc = xf - mu
    var = jnp.mean(xc * xc, axis=-1, keepdims=True)
    y = xc * lax.rsqrt(var + LN_EPS) * g.astype(jnp.float32) + b.astype(jnp.float32)
    return y.astype(x.dtype)


def causal_depthwise_conv(x, w):
    k = w.shape[0]
    return lax.conv_general_dilated(
        x, w[:, None, :].astype(x.dtype), window_strides=(1,), padding=[(k - 1, 0)],
        dimension_numbers=("NWC", "WIO", "NWC"), feature_group_count=x.shape[-1])


def conv_mixers(h, w_in, conv_a_w, conv_a_b, ln_a_g, ln_a_b, conv_b_w, w_out):
    z = h @ w_in
    a_val, a_gate, b_gate, c_gate, b_val = jnp.split(
        z, [A_DIM, 2 * A_DIM, 2 * A_DIM + B_DIM, 2 * A_DIM + 2 * B_DIM], axis=-1)
    a = a_val * jax.nn.sigmoid(a_gate)
    a = causal_depthwise_conv(a, conv_a_w) + conv_a_b
    a = jax.nn.silu(layer_norm(a, ln_a_g, ln_a_b))
    bo = b_gate * causal_depthwise_conv(c_gate * b_val, conv_b_w)
    return jnp.concatenate([a, bo], axis=-1) @ w_out


def chunked_spatial_gating(h, w_in, b_in, ln_v_g, ln_v_b, w_s, b_s, w_out):
    z = jax.nn.gelu(h @ w_in + b_in, approximate=False)
    u, v = jnp.split(z, 2, axis=-1)
    v = layer_norm(v, ln_v_g, ln_v_b)
    bsz, s, _ = v.shape
    vc = v.reshape(bsz, s // CHUNK, CHUNK, C_GROUPS, C_GROUP_DIM)
    mask = jnp.tril(jnp.ones((CHUNK, CHUNK), dtype=bool))
    ws = jnp.where(mask[None], w_s, 0.0).astype(v.dtype)
    sv = jnp.einsum("gts,bnsgc->bntgc", ws, vc) + b_s.T[None, None, :, :, None].astype(v.dtype)
    y = u * sv.reshape(bsz, s, C_DIM)
    return y @ w_out


def squared_relu_mlp(h, w1, w2):
    a = jax.nn.relu(h @ w1)
    return (a * a) @ w2


def _fwd_setup_inputs(seed: int = 0) -> dict:
    key = jax.random.key(seed)
    ks = iter(jax.random.split(key, 32))

    def nrm(shape, scale):
        return jax.random.normal(next(ks), shape, jnp.float32) * scale

    def gain(shape):
        return 1.0 + nrm(shape, 0.02)

    d = D_MODEL
    return {
        "x": nrm((BATCH, SEQ, d), 1.0),
        "ev_norm_g": gain((N_EVEN, d)),
        "ev_w_in": nrm((N_EVEN, d, IN_EVEN), d ** -0.5),
        "ev_conv_a_w": nrm((N_EVEN, A_CONV_WIDTH, A_DIM), A_CONV_WIDTH ** -0.5),
        "ev_conv_a_b": nrm((N_EVEN, A_DIM), 0.02),
        "ev_ln_a_g": gain((N_EVEN, A_DIM)),
        "ev_ln_a_b": nrm((N_EVEN, A_DIM), 0.02),
        "ev_conv_b_w": nrm((N_EVEN, B_CONV_WIDTH, B_DIM), B_CONV_WIDTH ** -0.5),
        "ev_w_out": nrm((N_EVEN, MIX_DIM, d), MIX_DIM ** -0.5),
        "od_norm_g": gain((N_ODD, d)),
        "od_w_in": nrm((N_ODD, d, 2 * C_DIM), d ** -0.5),
        "od_b_in": nrm((N_ODD, 2 * C_DIM), 0.02),
        "od_ln_v_g": gain((N_ODD, C_DIM)),
        "od_ln_v_b": nrm((N_ODD, C_DIM), 0.02),
        "od_w_s": nrm((N_ODD, C_GROUPS, CHUNK, CHUNK), CHUNK ** -0.5),
        "od_b_s": gain((N_ODD, C_GROUPS, CHUNK)),
        "od_w_out": nrm((N_ODD, C_DIM, d), C_DIM ** -0.5),
        "mlp_norm_g": gain((DEPTH, d)),
        "mlp_w1": nrm((DEPTH, d, D_FF), d ** -0.5),
        "mlp_w2": nrm((DEPTH, D_FF, d), D_FF ** -0.5),
        "final_norm_g": gain((d,)),
    }


def _fwd_reference(x, ev_norm_g, ev_w_in, ev_conv_a_w, ev_conv_a_b, ev_ln_a_g, ev_ln_a_b,
              ev_conv_b_w, ev_w_out, od_norm_g, od_w_in, od_b_in, od_ln_v_g, od_ln_v_b,
              od_w_s, od_b_s, od_w_out, mlp_norm_g, mlp_w1, mlp_w2, final_norm_g):
    h = x
    for i in range(DEPTH):
        j = i // 2
        if i % 2 == 0:
            h = h + conv_mixers(rms_norm(h, ev_norm_g[j]), ev_w_in[j], ev_conv_a_w[j],
                                ev_conv_a_b[j], ev_ln_a_g[j], ev_ln_a_b[j],
                                ev_conv_b_w[j], ev_w_out[j])
        else:
            h = h + chunked_spatial_gating(rms_norm(h, od_norm_g[j]), od_w_in[j], od_b_in[j],
                                           od_ln_v_g[j], od_ln_v_b[j], od_w_s[j],
                                           od_b_s[j], od_w_out[j])
        h = h + squared_relu_mlp(rms_norm(h, mlp_norm_g[i]), mlp_w1[i], mlp_w2[i])
    return rms_norm(h, final_norm_g)


import jax as _jax
import jax.numpy as _jnp

TWIN_FORMAT = 'train_step'
FWD_PARAMS = ['x', 'ev_norm_g', 'ev_w_in', 'ev_conv_a_w', 'ev_conv_a_b', 'ev_ln_a_g', 'ev_ln_a_b', 'ev_conv_b_w', 'ev_w_out', 'od_norm_g', 'od_w_in', 'od_b_in', 'od_ln_v_g', 'od_ln_v_b', 'od_w_s', 'od_b_s', 'od_w_out', 'mlp_norm_g', 'mlp_w1', 'mlp_w2', 'final_norm_g']
TWIN_WEIGHTS = ['ev_norm_g', 'ev_w_in', 'ev_conv_a_w', 'ev_conv_a_b', 'ev_ln_a_g', 'ev_ln_a_b', 'ev_conv_b_w', 'ev_w_out', 'od_norm_g', 'od_w_in', 'od_b_in', 'od_ln_v_g', 'od_ln_v_b', 'od_w_s', 'od_b_s', 'od_w_out', 'mlp_norm_g', 'mlp_w1', 'mlp_w2', 'final_norm_g']
TWIN_DIFF_INPUT = 'x'
TWIN_INPUTS = ['x', 'ev_norm_g', 'ev_w_in', 'ev_conv_a_w', 'ev_conv_a_b', 'ev_ln_a_g', 'ev_ln_a_b', 'ev_conv_b_w', 'ev_w_out', 'od_norm_g', 'od_w_in', 'od_b_in', 'od_ln_v_g', 'od_ln_v_b', 'od_w_s', 'od_b_s', 'od_w_out', 'mlp_norm_g', 'mlp_w1', 'mlp_w2', 'final_norm_g', 'loss_target', 'm_ev_norm_g', 'm_ev_w_in', 'm_ev_conv_a_w', 'm_ev_conv_a_b', 'm_ev_ln_a_g', 'm_ev_ln_a_b', 'm_ev_conv_b_w', 'm_ev_w_out', 'm_od_norm_g', 'm_od_w_in', 'm_od_b_in', 'm_od_ln_v_g', 'm_od_ln_v_b', 'm_od_w_s', 'm_od_b_s', 'm_od_w_out', 'm_mlp_norm_g', 'm_mlp_w1', 'm_mlp_w2', 'm_final_norm_g', 'v_ev_norm_g', 'v_ev_w_in', 'v_ev_conv_a_w', 'v_ev_conv_a_b', 'v_ev_ln_a_g', 'v_ev_ln_a_b', 'v_ev_conv_b_w', 'v_ev_w_out', 'v_od_norm_g', 'v_od_w_in', 'v_od_b_in', 'v_od_ln_v_g', 'v_od_ln_v_b', 'v_od_w_s', 'v_od_b_s', 'v_od_w_out', 'v_mlp_norm_g', 'v_mlp_w1', 'v_mlp_w2', 'v_final_norm_g']
TWIN_OUTPUTS = ['loss', 'grad_x', 'grad_ev_norm_g', 'grad_ev_w_in', 'grad_ev_conv_a_w', 'grad_ev_conv_a_b', 'grad_ev_ln_a_g', 'grad_ev_ln_a_b', 'grad_ev_conv_b_w', 'grad_ev_w_out', 'grad_od_norm_g', 'grad_od_w_in', 'grad_od_b_in', 'grad_od_ln_v_g', 'grad_od_ln_v_b', 'grad_od_w_s', 'grad_od_b_s', 'grad_od_w_out', 'grad_mlp_norm_g', 'grad_mlp_w1', 'grad_mlp_w2', 'grad_final_norm_g', 'delta_ev_norm_g', 'delta_ev_w_in', 'delta_ev_conv_a_w', 'delta_ev_conv_a_b', 'delta_ev_ln_a_g', 'delta_ev_ln_a_b', 'delta_ev_conv_b_w', 'delta_ev_w_out', 'delta_od_norm_g', 'delta_od_w_in', 'delta_od_b_in', 'delta_od_ln_v_g', 'delta_od_ln_v_b', 'delta_od_w_s', 'delta_od_b_s', 'delta_od_w_out', 'delta_mlp_norm_g', 'delta_mlp_w1', 'delta_mlp_w2', 'delta_final_norm_g', 'new_m_ev_norm_g', 'new_m_ev_w_in', 'new_m_ev_conv_a_w', 'new_m_ev_conv_a_b', 'new_m_ev_ln_a_g', 'new_m_ev_ln_a_b', 'new_m_ev_conv_b_w', 'new_m_ev_w_out', 'new_m_od_norm_g', 'new_m_od_w_in', 'new_m_od_b_in', 'new_m_od_ln_v_g', 'new_m_od_ln_v_b', 'new_m_od_w_s', 'new_m_od_b_s', 'new_m_od_w_out', 'new_m_mlp_norm_g', 'new_m_mlp_w1', 'new_m_mlp_w2', 'new_m_final_norm_g', 'new_v_ev_norm_g', 'new_v_ev_w_in', 'new_v_ev_conv_a_w', 'new_v_ev_conv_a_b', 'new_v_ev_ln_a_g', 'new_v_ev_ln_a_b', 'new_v_ev_conv_b_w', 'new_v_ev_w_out', 'new_v_od_norm_g', 'new_v_od_w_in', 'new_v_od_b_in', 'new_v_od_ln_v_g', 'new_v_od_ln_v_b', 'new_v_od_w_s', 'new_v_od_b_s', 'new_v_od_w_out', 'new_v_mlp_norm_g', 'new_v_mlp_w1', 'new_v_mlp_w2', 'new_v_final_norm_g']
TWIN_LEAF_KINDS = {'loss': 'loss', 'grad_x': 'grad_x', 'grad_ev_norm_g': 'grad_w', 'grad_ev_w_in': 'grad_w', 'grad_ev_conv_a_w': 'grad_w', 'grad_ev_conv_a_b': 'grad_w', 'grad_ev_ln_a_g': 'grad_w', 'grad_ev_ln_a_b': 'grad_w', 'grad_ev_conv_b_w': 'grad_w', 'grad_ev_w_out': 'grad_w', 'grad_od_norm_g': 'grad_w', 'grad_od_w_in': 'grad_w', 'grad_od_b_in': 'grad_w', 'grad_od_ln_v_g': 'grad_w', 'grad_od_ln_v_b': 'grad_w', 'grad_od_w_s': 'grad_w', 'grad_od_b_s': 'grad_w', 'grad_od_w_out': 'grad_w', 'grad_mlp_norm_g': 'grad_w', 'grad_mlp_w1': 'grad_w', 'grad_mlp_w2': 'grad_w', 'grad_final_norm_g': 'grad_w', 'delta_ev_norm_g': 'delta_w', 'delta_ev_w_in': 'delta_w', 'delta_ev_conv_a_w': 'delta_w', 'delta_ev_conv_a_b': 'delta_w', 'delta_ev_ln_a_g': 'delta_w', 'delta_ev_ln_a_b': 'delta_w', 'delta_ev_conv_b_w': 'delta_w', 'delta_ev_w_out': 'delta_w', 'delta_od_norm_g': 'delta_w', 'delta_od_w_in': 'delta_w', 'delta_od_b_in': 'delta_w', 'delta_od_ln_v_g': 'delta_w', 'delta_od_ln_v_b': 'delta_w', 'delta_od_w_s': 'delta_w', 'delta_od_b_s': 'delta_w', 'delta_od_w_out': 'delta_w', 'delta_mlp_norm_g': 'delta_w', 'delta_mlp_w1': 'delta_w', 'delta_mlp_w2': 'delta_w', 'delta_final_norm_g': 'delta_w', 'new_m_ev_norm_g': 'new_m', 'new_m_ev_w_in': 'new_m', 'new_m_ev_conv_a_w': 'new_m', 'new_m_ev_conv_a_b': 'new_m', 'new_m_ev_ln_a_g': 'new_m', 'new_m_ev_ln_a_b': 'new_m', 'new_m_ev_conv_b_w': 'new_m', 'new_m_ev_w_out': 'new_m', 'new_m_od_norm_g': 'new_m', 'new_m_od_w_in': 'new_m', 'new_m_od_b_in': 'new_m', 'new_m_od_ln_v_g': 'new_m', 'new_m_od_ln_v_b': 'new_m', 'new_m_od_w_s': 'new_m', 'new_m_od_b_s': 'new_m', 'new_m_od_w_out': 'new_m', 'new_m_mlp_norm_g': 'new_m', 'new_m_mlp_w1': 'new_m', 'new_m_mlp_w2': 'new_m', 'new_m_final_norm_g': 'new_m', 'new_v_ev_norm_g': 'new_v', 'new_v_ev_w_in': 'new_v', 'new_v_ev_conv_a_w': 'new_v', 'new_v_ev_conv_a_b': 'new_v', 'new_v_ev_ln_a_g': 'new_v', 'new_v_ev_ln_a_b': 'new_v', 'new_v_ev_conv_b_w': 'new_v', 'new_v_ev_w_out': 'new_v', 'new_v_od_norm_g': 'new_v', 'new_v_od_w_in': 'new_v', 'new_v_od_b_in': 'new_v', 'new_v_od_ln_v_g': 'new_v', 'new_v_od_ln_v_b': 'new_v', 'new_v_od_w_s': 'new_v', 'new_v_od_b_s': 'new_v', 'new_v_od_w_out': 'new_v', 'new_v_mlp_norm_g': 'new_v', 'new_v_mlp_w1': 'new_v', 'new_v_mlp_w2': 'new_v', 'new_v_final_norm_g': 'new_v'}


def _forward(args):
    return _fwd_reference(*[args[k] for k in FWD_PARAMS])


def _output_shape():
    out = _jax.eval_shape(lambda: _forward(_fwd_setup_inputs(0)))
    return out.shape, out.dtype

N_MICROBATCH = 1
ADAM_LR = 0.001
ADAM_B1 = 0.9
ADAM_B2 = 0.999
ADAM_EPS = 1e-08
ADAM_WD = 0.01
ADAM_STEP = 10
PER_EXAMPLE_BATCH_AXIS = {'x': 0, 'loss_target': 0}
SHARED_INPUTS = []
_WEIGHT_DTYPES = {'ev_norm_g': _jnp.float32, 'ev_w_in': _jnp.float32, 'ev_conv_a_w': _jnp.float32, 'ev_conv_a_b': _jnp.float32, 'ev_ln_a_g': _jnp.float32, 'ev_ln_a_b': _jnp.float32, 'ev_conv_b_w': _jnp.float32, 'ev_w_out': _jnp.float32, 'od_norm_g': _jnp.float32, 'od_w_in': _jnp.float32, 'od_b_in': _jnp.float32, 'od_ln_v_g': _jnp.float32, 'od_ln_v_b': _jnp.float32, 'od_w_s': _jnp.float32, 'od_b_s': _jnp.float32, 'od_w_out': _jnp.float32, 'mlp_norm_g': _jnp.float32, 'mlp_w1': _jnp.float32, 'mlp_w2': _jnp.float32, 'final_norm_g': _jnp.float32}
MOMENT_SCALE = {'ev_norm_g': 2.176896e-01, 'ev_w_in': 1.331852e-01, 'ev_conv_a_w': 9.980075e-02, 'ev_conv_a_b': 2.202138e-01, 'ev_ln_a_g': 1.210284e-01, 'ev_ln_a_b': 1.351660e-01, 'ev_conv_b_w': 1.660617e-01, 'ev_w_out': 1.331298e-01, 'od_norm_g': 9.603014e-02, 'od_w_in': 6.690391e-02, 'od_b_in': 7.304222e-02, 'od_ln_v_g': 4.769631e-02, 'od_ln_v_b': 4.360570e-02, 'od_w_s': 4.419256e-02, 'od_b_s': 6.133714e-02, 'od_w_out': 8.343056e-02, 'mlp_norm_g': 1.390665e-01, 'mlp_w1': 6.784063e-02, 'mlp_w2': 1.277110e-01, 'final_norm_g': 3.247968e+01}


def _to_microbatches(a, axis):
    t = _jnp.moveaxis(a, axis, 0)
    t = t.reshape((N_MICROBATCH, t.shape[0] // N_MICROBATCH) + t.shape[1:])
    return _jnp.moveaxis(t, 1, axis + 1)


def setup_inputs(seed: int = 0) -> dict:
    inp = _fwd_setup_inputs(seed)
    key = _jax.random.fold_in(_jax.random.key(seed), 7919)
    shape, _ = _output_shape()
    out = dict(inp)
    out["loss_target"] = _jax.random.normal(_jax.random.fold_in(key, 0), shape, _jnp.float32)
    for i, name in enumerate(TWIN_WEIGHTS):
        w = inp[name].astype(_jnp.float32)
        if MOMENT_SCALE is None:
            s = _jnp.sqrt(_jnp.mean(_jnp.square(w)) + 1e-30)
        else:
            s = MOMENT_SCALE[name]
        km, kv = _jax.random.split(_jax.random.fold_in(key, i + 1))
        out[name] = w
        out["m_" + name] = s * _jax.random.normal(km, w.shape, _jnp.float32)
        out["v_" + name] = (s * s) * _jax.random.uniform(kv, w.shape, _jnp.float32, 0.5, 1.5)
    if N_MICROBATCH > 1:
        for name, axis in PER_EXAMPLE_BATCH_AXIS.items():
            out[name] = _to_microbatches(out[name], axis)
    return {'x': out['x'], 'ev_norm_g': out['ev_norm_g'], 'ev_w_in': out['ev_w_in'], 'ev_conv_a_w': out['ev_conv_a_w'], 'ev_conv_a_b': out['ev_conv_a_b'], 'ev_ln_a_g': out['ev_ln_a_g'], 'ev_ln_a_b': out['ev_ln_a_b'], 'ev_conv_b_w': out['ev_conv_b_w'], 'ev_w_out': out['ev_w_out'], 'od_norm_g': out['od_norm_g'], 'od_w_in': out['od_w_in'], 'od_b_in': out['od_b_in'], 'od_ln_v_g': out['od_ln_v_g'], 'od_ln_v_b': out['od_ln_v_b'], 'od_w_s': out['od_w_s'], 'od_b_s': out['od_b_s'], 'od_w_out': out['od_w_out'], 'mlp_norm_g': out['mlp_norm_g'], 'mlp_w1': out['mlp_w1'], 'mlp_w2': out['mlp_w2'], 'final_norm_g': out['final_norm_g'], 'loss_target': out['loss_target'], 'm_ev_norm_g': out['m_ev_norm_g'], 'm_ev_w_in': out['m_ev_w_in'], 'm_ev_conv_a_w': out['m_ev_conv_a_w'], 'm_ev_conv_a_b': out['m_ev_conv_a_b'], 'm_ev_ln_a_g': out['m_ev_ln_a_g'], 'm_ev_ln_a_b': out['m_ev_ln_a_b'], 'm_ev_conv_b_w': out['m_ev_conv_b_w'], 'm_ev_w_out': out['m_ev_w_out'], 'm_od_norm_g': out['m_od_norm_g'], 'm_od_w_in': out['m_od_w_in'], 'm_od_b_in': out['m_od_b_in'], 'm_od_ln_v_g': out['m_od_ln_v_g'], 'm_od_ln_v_b': out['m_od_ln_v_b'], 'm_od_w_s': out['m_od_w_s'], 'm_od_b_s': out['m_od_b_s'], 'm_od_w_out': out['m_od_w_out'], 'm_mlp_norm_g': out['m_mlp_norm_g'], 'm_mlp_w1': out['m_mlp_w1'], 'm_mlp_w2': out['m_mlp_w2'], 'm_final_norm_g': out['m_final_norm_g'], 'v_ev_norm_g': out['v_ev_norm_g'], 'v_ev_w_in': out['v_ev_w_in'], 'v_ev_conv_a_w': out['v_ev_conv_a_w'], 'v_ev_conv_a_b': out['v_ev_conv_a_b'], 'v_ev_ln_a_g': out['v_ev_ln_a_g'], 'v_ev_ln_a_b': out['v_ev_ln_a_b'], 'v_ev_conv_b_w': out['v_ev_conv_b_w'], 'v_ev_w_out': out['v_ev_w_out'], 'v_od_norm_g': out['v_od_norm_g'], 'v_od_w_in': out['v_od_w_in'], 'v_od_b_in': out['v_od_b_in'], 'v_od_ln_v_g': out['v_od_ln_v_g'], 'v_od_ln_v_b': out['v_od_ln_v_b'], 'v_od_w_s': out['v_od_w_s'], 'v_od_b_s': out['v_od_b_s'], 'v_od_w_out': out['v_od_w_out'], 'v_mlp_norm_g': out['v_mlp_norm_g'], 'v_mlp_w1': out['v_mlp_w1'], 'v_mlp_w2': out['v_mlp_w2'], 'v_final_norm_g': out['v_final_norm_g']}


def _loss(weights, diff, rest, loss_target):
    with _jax.named_scope("forward"):
        args = {**rest, TWIN_DIFF_INPUT: diff, **{k: w.astype(_WEIGHT_DTYPES[k]) for k, w in weights.items()}}
        y = _forward(args)
    with _jax.named_scope("loss_head"):
        err = _jnp.square(y.astype(_jnp.float32) - loss_target)
        return 0.5 * _jnp.sum(_jnp.mean(err, axis=-1)) if err.ndim else 0.5 * err


def _adamw(w, g, m, v):
    m = ADAM_B1 * m + (1.0 - ADAM_B1) * g
    v = ADAM_B2 * v + (1.0 - ADAM_B2) * _jnp.square(g)
    m_hat = m / (1.0 - ADAM_B1 ** ADAM_STEP)
    v_hat = v / (1.0 - ADAM_B2 ** ADAM_STEP)
    delta = -ADAM_LR * (m_hat / (_jnp.sqrt(v_hat) + ADAM_EPS) + ADAM_WD * w)
    return delta, m, v


def reference(x, ev_norm_g, ev_w_in, ev_conv_a_w, ev_conv_a_b, ev_ln_a_g, ev_ln_a_b, ev_conv_b_w, ev_w_out, od_norm_g, od_w_in, od_b_in, od_ln_v_g, od_ln_v_b, od_w_s, od_b_s, od_w_out, mlp_norm_g, mlp_w1, mlp_w2, final_norm_g, loss_target, m_ev_norm_g, m_ev_w_in, m_ev_conv_a_w, m_ev_conv_a_b, m_ev_ln_a_g, m_ev_ln_a_b, m_ev_conv_b_w, m_ev_w_out, m_od_norm_g, m_od_w_in, m_od_b_in, m_od_ln_v_g, m_od_ln_v_b, m_od_w_s, m_od_b_s, m_od_w_out, m_mlp_norm_g, m_mlp_w1, m_mlp_w2, m_final_norm_g, v_ev_norm_g, v_ev_w_in, v_ev_conv_a_w, v_ev_conv_a_b, v_ev_ln_a_g, v_ev_ln_a_b, v_ev_conv_b_w, v_ev_w_out, v_od_norm_g, v_od_w_in, v_od_b_in, v_od_ln_v_g, v_od_ln_v_b, v_od_w_s, v_od_b_s, v_od_w_out, v_mlp_norm_g, v_mlp_w1, v_mlp_w2, v_final_norm_g):
    given = dict(x=x, ev_norm_g=ev_norm_g, ev_w_in=ev_w_in, ev_conv_a_w=ev_conv_a_w, ev_conv_a_b=ev_conv_a_b, ev_ln_a_g=ev_ln_a_g, ev_ln_a_b=ev_ln_a_b, ev_conv_b_w=ev_conv_b_w, ev_w_out=ev_w_out, od_norm_g=od_norm_g, od_w_in=od_w_in, od_b_in=od_b_in, od_ln_v_g=od_ln_v_g, od_ln_v_b=od_ln_v_b, od_w_s=od_w_s, od_b_s=od_b_s, od_w_out=od_w_out, mlp_norm_g=mlp_norm_g, mlp_w1=mlp_w1, mlp_w2=mlp_w2, final_norm_g=final_norm_g, loss_target=loss_target, m_ev_norm_g=m_ev_norm_g, m_ev_w_in=m_ev_w_in, m_ev_conv_a_w=m_ev_conv_a_w, m_ev_conv_a_b=m_ev_conv_a_b, m_ev_ln_a_g=m_ev_ln_a_g, m_ev_ln_a_b=m_ev_ln_a_b, m_ev_conv_b_w=m_ev_conv_b_w, m_ev_w_out=m_ev_w_out, m_od_norm_g=m_od_norm_g, m_od_w_in=m_od_w_in, m_od_b_in=m_od_b_in, m_od_ln_v_g=m_od_ln_v_g, m_od_ln_v_b=m_od_ln_v_b, m_od_w_s=m_od_w_s, m_od_b_s=m_od_b_s, m_od_w_out=m_od_w_out, m_mlp_norm_g=m_mlp_norm_g, m_mlp_w1=m_mlp_w1, m_mlp_w2=m_mlp_w2, m_final_norm_g=m_final_norm_g, v_ev_norm_g=v_ev_norm_g, v_ev_w_in=v_ev_w_in, v_ev_conv_a_w=v_ev_conv_a_w, v_ev_conv_a_b=v_ev_conv_a_b, v_ev_ln_a_g=v_ev_ln_a_g, v_ev_ln_a_b=v_ev_ln_a_b, v_ev_conv_b_w=v_ev_conv_b_w, v_ev_w_out=v_ev_w_out, v_od_norm_g=v_od_norm_g, v_od_w_in=v_od_w_in, v_od_b_in=v_od_b_in, v_od_ln_v_g=v_od_ln_v_g, v_od_ln_v_b=v_od_ln_v_b, v_od_w_s=v_od_w_s, v_od_b_s=v_od_b_s, v_od_w_out=v_od_w_out, v_mlp_norm_g=v_mlp_norm_g, v_mlp_w1=v_mlp_w1, v_mlp_w2=v_mlp_w2, v_final_norm_g=v_final_norm_g)
    weights = {n: given[n] for n in TWIN_WEIGHTS}
    shared = {n: given[n] for n in SHARED_INPUTS}
    per_example = {n: given[n] for n in ['x']}
    grad_fn = _jax.value_and_grad(_loss, argnums=(0, 1))

    def one_microbatch(ex, loss_target):
        ex = dict(ex)
        diff = ex.pop(TWIN_DIFF_INPUT)
        return grad_fn(weights, diff, {**shared, **ex}, loss_target)

    if N_MICROBATCH == 1:
        loss, (grad_w, grad_x) = one_microbatch(per_example, given["loss_target"])
    else:
        def body(carry, xs):
            loss_sum, grad_sum = carry
            l_k, (gw_k, gx_k) = one_microbatch(xs[0], xs[1])
            with _jax.named_scope("update"):
                return (loss_sum + l_k, _jax.tree.map(_jnp.add, grad_sum, gw_k)), gx_k

        init = (_jnp.zeros((), _jnp.float32), _jax.tree.map(_jnp.zeros_like, weights))
        (loss, grad_w), grad_x = _jax.lax.scan(body, init, (per_example, given["loss_target"]))
    with _jax.named_scope("update"):
        delta_w, new_m, new_v = {}, {}, {}
        for n in TWIN_WEIGHTS:
            delta_w[n], new_m[n], new_v[n] = _adamw(weights[n], grad_w[n], given["m_" + n], given["v_" + n])
    return (loss, grad_x, *[grad_w[n] for n in TWIN_WEIGHTS], *[delta_w[n] for n in TWIN_WEIGHTS],
            *[new_m[n] for n in TWIN_WEIGHTS], *[new_v[n] for n in TWIN_WEIGHTS])
```

```python
import functools
import math

import jax
import jax.numpy as jnp
from jax import lax
from jax.experimental import pallas as pl
from jax.experimental.pallas import tpu as pltpu

F32 = jnp.float32
BF16 = jnp.bfloat16
MESH = pl.DeviceIdType.MESH

D_MODEL = 1024
A_DIM = 512
B_DIM = 512
IN_EVEN = 2 * A_DIM + 3 * B_DIM
A_TAPS = 31
B_TAPS = 3
CHUNK = 128
C_GROUPS = 8
C_DIM = 1024
D_FF = 4096
RMS_EPS = 1e-6
LN_EPS = 1e-5
N_DEV = 8
N_CHIP = 4

ADAM_LR = 0.001
ADAM_B1 = 0.9
ADAM_B2 = 0.999
ADAM_EPS = 1e-08
ADAM_WD = 0.01
ADAM_STEP = 10

A_HALO = 32
B_HALO = 8
VMEM_LIMIT_BYTES = 56 * 1024 * 1024
INV_SQRT2 = 1.0 / math.sqrt(2.0)
INV_SQRT_2PI = 1.0 / math.sqrt(2.0 * math.pi)


def _params(*sem):
    return pltpu.CompilerParams(dimension_semantics=sem if sem else None, vmem_limit_bytes=VMEM_LIMIT_BYTES)


def _tile(n, want):
    t = min(n, want)
    while n % t:
        t //= 2
    return t


def _sigmoid(x):
    return 1.0 / (1.0 + jnp.exp(-x))


def _gelu(x):
    return 0.5 * x * (1.0 + lax.erf(x * INV_SQRT2))


def _gelu_grad(x):
    return 0.5 * (1.0 + lax.erf(x * INV_SQRT2)) + x * jnp.exp(-0.5 * x * x) * INV_SQRT_2PI


def _colsum(x):
    return jnp.sum(x, axis=0, keepdims=True)


def _matmul(name, a, b, *, kind, m, n, k, a_spec, b_spec, tm, tn, tk, out_shape, out_specs, epilogue,
            extras=(), extra_specs=()):
    dims = {"nn": (((1,), (0,)), ((), ())), "nt": (((1,), (1,)), ((), ())), "tn": (((0,), (0,)), ((), ()))}[kind]
    nk = k // tk
    n_extra = len(extras)
    n_out = len(out_shape)

    def body(a_ref, b_ref, *rest):
        extra_refs = rest[:n_extra]
        out_refs = rest[n_extra:n_extra + n_out]
        acc_ref = rest[n_extra + n_out]
        step = pl.program_id(2)

        @pl.when(step == 0)
        def _():
            acc_ref[...] = jnp.zeros_like(acc_ref)

        acc_ref[...] += lax.dot_general(a_ref[...], b_ref[...], dims, preferred_element_type=F32)

        @pl.when(step == nk - 1)
        def _():
            epilogue(acc_ref[...], extra_refs, out_refs)

    return pl.pallas_call(
        body, name=name, grid=(m // tm, n // tn, nk),
        in_specs=[a_spec, b_spec, *extra_specs], out_specs=out_specs, out_shape=out_shape,
        scratch_shapes=[pltpu.VMEM((tm, tn), F32)],
        compiler_params=_params("parallel", "parallel", "arbitrary"),
    )(a, b, *extras)


def _b_spec_nn(b, tk, tn, lead):
    if b.ndim == 2:
        return pl.BlockSpec((tk, tn), lambda i, j, kk: (kk, j))
    if b.ndim == 3:
        return pl.BlockSpec((None, tk, tn), lambda i, j, kk: (j, kk, 0))
    return pl.BlockSpec((None, None, tk, tn), lambda i, j, kk: (j, lead, kk, 0))


def _b_spec_nt(b, tn, tk, lead):
    if b.ndim == 2:
        return pl.BlockSpec((tn, tk), lambda i, j, kk: (j, kk))
    if b.ndim == 3:
        return pl.BlockSpec((None, tn, tk), lambda i, j, kk: (kk, j, 0))
    return pl.BlockSpec((None, None, tn, tk), lambda i, j, kk: (kk, lead, j, 0))


def _mm_nn(name, a, b, n, epilogue, out_dtypes, *, extras=(), bias=None, lead=0):
    m, k = a.shape
    blocked = b.ndim > 2
    tm = _tile(m, 1024)
    tn = b.shape[-1] if blocked else _tile(n, 1024)
    tk = _tile(k, 512)
    tile = pl.BlockSpec((tm, tn), lambda i, j, kk: (i, j))
    extra_specs = [tile] * len(extras)
    if bias is not None:
        extras = (*extras, bias)
        extra_specs.append(pl.BlockSpec((1, tn), lambda i, j, kk: (0, j)))
    return _matmul(
        name, a, b, kind="nn", m=m, n=n, k=k, tm=tm, tn=tn, tk=tk,
        a_spec=pl.BlockSpec((tm, tk), lambda i, j, kk: (i, kk)), b_spec=_b_spec_nn(b, tk, tn, lead),
        out_shape=[jax.ShapeDtypeStruct((m, n), dt) for dt in out_dtypes], out_specs=[tile] * len(out_dtypes),
        epilogue=epilogue, extras=extras, extra_specs=extra_specs)


def _mm_nt(name, a, b, n, epilogue, out_dtypes, *, extras=(), lead=0):
    m, k = a.shape
    blocked = b.ndim > 2
    tm = _tile(m, 1024)
    tn = _tile(n, 1024)
    tk = b.shape[-1] if blocked else _tile(k, 512)
    tile = pl.BlockSpec((tm, tn), lambda i, j, kk: (i, j))
    return _matmul(
        name, a, b, kind="nt", m=m, n=n, k=k, tm=tm, tn=tn, tk=tk,
        a_spec=pl.BlockSpec((tm, tk), lambda i, j, kk: (i, kk)), b_spec=_b_spec_nt(b, tn, tk, lead),
        out_shape=[jax.ShapeDtypeStruct((m, n), dt) for dt in out_dtypes], out_specs=[tile] * len(out_dtypes),
        epilogue=epilogue, extras=extras, extra_specs=[tile] * len(extras))


def _mm_tn(name, a, b, *, col_blocks=0):
    t, k = a.shape
    n = b.shape[1]
    tm = _tile(k, 1024)
    tn = n // col_blocks if col_blocks else _tile(n, 1024)
    tk = _tile(t, 512)
    if col_blocks:
        out_shape = [jax.ShapeDtypeStruct((col_blocks, k, tn), F32)]
        out_specs = [pl.BlockSpec((None, tm, tn), lambda i, j, kk: (j, i, 0))]
    else:
        out_shape = [jax.ShapeDtypeStruct((k, n), F32)]
        out_specs = [pl.BlockSpec((tm, tn), lambda i, j, kk: (i, j))]

    def epilogue(acc, extra_refs, out_refs):
        out_refs[0][...] = acc

    return _matmul(
        name, a, b, kind="tn", m=k, n=n, k=t, tm=tm, tn=tn, tk=tk,
        a_spec=pl.BlockSpec((tk, tm), lambda i, j, kk: (kk, i)), b_spec=pl.BlockSpec((tk, tn), lambda i, j, kk: (kk, j)),
        out_shape=out_shape, out_specs=out_specs, epilogue=epilogue)[0]


def _ep_store(acc, extra_refs, out_refs):
    out_refs[0][...] = acc.astype(out_refs[0].dtype)


def _ep_residual(acc, extra_refs, out_refs):
    out_refs[0][...] = extra_refs[0][...] + acc


def _ep_bias(acc, extra_refs, out_refs):
    out_refs[0][...] = acc + extra_refs[0][...]


def _ep_relu_sq(acc, extra_refs, out_refs):
    r = jnp.maximum(acc, 0.0)
    out_refs[0][...] = (r * r).astype(BF16)
    out_refs[1][...] = r.astype(BF16)


def _ep_relu_sq_grad(acc, extra_refs, out_refs):
    out_refs[0][...] = (acc * (2.0 * extra_refs[0][...].astype(F32))).astype(BF16)


def _rms_fwd(name, h, g):
    t, d = h.shape
    tt = _tile(t, 512)

    def body(h_ref, g_ref, n_ref):
        x = h_ref[...]
        r = lax.rsqrt(jnp.mean(x * x, axis=-1, keepdims=True) + RMS_EPS)
        n_ref[...] = (x * r * g_ref[...]).astype(BF16)

    return pl.pallas_call(
        body, name=name, grid=(t // tt,),
        in_specs=[pl.BlockSpec((tt, d), lambda i: (i, 0)), pl.BlockSpec((1, d), lambda i: (0, 0))],
        out_specs=pl.BlockSpec((tt, d), lambda i: (i, 0)), out_shape=jax.ShapeDtypeStruct((t, d), BF16),
        compiler_params=_params("parallel"),
    )(h, g)


def _rms_bwd(name, dn, h, g, grad_in):
    t, d = h.shape
    tt = _tile(t, 512)

    def body(dn_ref, h_ref, g_ref, gin_ref, gout_ref, gout16_ref, dg_ref):
        @pl.when(pl.program_id(0) == 0)
        def _():
            dg_ref[...] = jnp.zeros_like(dg_ref)

        x = h_ref[...]
        dnv = dn_ref[...]
        r = lax.rsqrt(jnp.mean(x * x, axis=-1, keepdims=True) + RMS_EPS)
        xr = x * r
        dg_ref[...] += _colsum(dnv * xr)
        dy = dnv * g_ref[...]
        out = gin_ref[...] + r * (dy - xr * jnp.mean(dy * xr, axis=-1, keepdims=True))
        gout_ref[...] = out
        gout16_ref[...] = out.astype(BF16)

    row = pl.BlockSpec((tt, d), lambda i: (i, 0))
    vec = pl.BlockSpec((1, d), lambda i: (0, 0))
    return pl.pallas_call(
        body, name=name, grid=(t // tt,), in_specs=[row, row, vec, row], out_specs=[row, row, vec],
        out_shape=[jax.ShapeDtypeStruct((t, d), F32), jax.ShapeDtypeStruct((t, d), BF16), jax.ShapeDtypeStruct((1, d), F32)],
        compiler_params=_params("arbitrary"),
    )(dn, h, g, grad_in)


def _final_loss(name, h, g, target):
    t, d = h.shape
    tt = _tile(t, 512)

    def body(h_ref, g_ref, tgt_ref, gout_ref, gout16_ref, dg_ref, loss_ref):
        @pl.when(pl.program_id(0) == 0)
        def _():
            dg_ref[...] = jnp.zeros_like(dg_ref)
            loss_ref[...] = jnp.zeros_like(loss_ref)

        x = h_ref[...]
        r = lax.rsqrt(jnp.mean(x * x, axis=-1, keepdims=True) + RMS_EPS)
        xr = x * r
        err = xr * g_ref[...] - tgt_ref[...]
        loss_ref[...] += 0.5 * jnp.sum(jnp.mean(err * err, axis=-1, keepdims=True), axis=0, keepdims=True)
        dout = err * (1.0 / d)
        dg_ref[...] += _colsum(dout * xr)
        dy = dout * g_ref[...]
        out = r * (dy - xr * jnp.mean(dy * xr, axis=-1, keepdims=True))
        gout_ref[...] = out
        gout16_ref[...] = out.astype(BF16)

    row = pl.BlockSpec((tt, d), lambda i: (i, 0))
    vec = pl.BlockSpec((1, d), lambda i: (0, 0))
    one = pl.BlockSpec((1, 1), lambda i: (0, 0))
    return pl.pallas_call(
        body, name=name, grid=(t // tt,), in_specs=[row, vec, row], out_specs=[row, row, vec, one],
        out_shape=[jax.ShapeDtypeStruct((t, d), F32), jax.ShapeDtypeStruct((t, d), BF16),
                   jax.ShapeDtypeStruct((1, d), F32), jax.ShapeDtypeStruct((1, 1), F32)],
        compiler_params=_params("arbitrary"),
    )(h, g, target)


def _mixer_windows(z_ref, zh_ref, first, a1_s, cb_s, tt):
    sig = _sigmoid(z_ref[:, A_DIM:2 * A_DIM])
    a1_s[A_HALO:A_HALO + tt, :] = z_ref[:, 0:A_DIM] * sig
    a1_h = zh_ref[:, 0:A_DIM] * _sigmoid(zh_ref[:, A_DIM:2 * A_DIM])
    a1_s[0:A_HALO, :] = jnp.where(first, 0.0, a1_h)
    cb_s[B_HALO:B_HALO + tt, :] = z_ref[:, 3 * A_DIM:4 * A_DIM] * z_ref[:, 4 * A_DIM:5 * A_DIM]
    cb_h = zh_ref[A_HALO - B_HALO:A_HALO, 3 * A_DIM:4 * A_DIM] * zh_ref[A_HALO - B_HALO:A_HALO, 4 * A_DIM:5 * A_DIM]
    cb_s[0:B_HALO, :] = jnp.where(first, 0.0, cb_h)
    return sig


def _causal_conv(win_s, w_ref, taps, halo, tt):
    base = halo - (taps - 1)
    acc = w_ref[0:1, :] * win_s[pl.ds(base, tt), :]
    for k in range(1, taps):
        acc = acc + w_ref[k:k + 1, :] * win_s[pl.ds(base + k, tt), :]
    return acc


def _layer_norm_stats(x):
    mu = jnp.mean(x, axis=-1, keepdims=True)
    xc = x - mu
    rstd = lax.rsqrt(jnp.mean(xc * xc, axis=-1, keepdims=True) + LN_EPS)
    return xc * rstd, rstd


def _mixer_specs(t, seq, tt):
    tiles_per_seq = seq // tt
    halo_blocks = tt // A_HALO
    z_spec = pl.BlockSpec((tt, IN_EVEN), lambda i: (i, 0))
    zh_spec = pl.BlockSpec((A_HALO, IN_EVEN), lambda i: (jnp.maximum(i * halo_blocks - 1, 0), 0))
    return tiles_per_seq, z_spec, zh_spec


def _vec_spec(rows, cols):
    return pl.BlockSpec((rows, cols), lambda i: (0, 0))


def _mixer_fwd(z, seq, caw, cab, lag, lab, cbw):
    t = z.shape[0]
    tt = _tile(seq, 256)
    tiles_per_seq, z_spec, zh_spec = _mixer_specs(t, seq, tt)

    def body(z_ref, zh_ref, caw_ref, cab_ref, lag_ref, lab_ref, cbw_ref, mix_ref, a1_s, cb_s):
        first = (pl.program_id(0) % tiles_per_seq) == 0
        _mixer_windows(z_ref, zh_ref, first, a1_s, cb_s, tt)
        a2 = _causal_conv(a1_s, caw_ref, A_TAPS, A_HALO, tt) + cab_ref[...]
        xhat, _ = _layer_norm_stats(a2)
        a3 = xhat * lag_ref[...] + lab_ref[...]
        mix_ref[:, 0:A_DIM] = (a3 * _sigmoid(a3)).astype(BF16)
        cv = _causal_conv(cb_s, cbw_ref, B_TAPS, B_HALO, tt)
        mix_ref[:, A_DIM:A_DIM + B_DIM] = (z_ref[:, 2 * A_DIM:3 * A_DIM] * cv).astype(BF16)

    return pl.pallas_call(
        body, name="mixer_fwd", grid=(t // tt,),
        in_specs=[z_spec, zh_spec, _vec_spec(A_TAPS, A_DIM), _vec_spec(1, A_DIM), _vec_spec(1, A_DIM), _vec_spec(1, A_DIM),
                  _vec_spec(B_TAPS, B_DIM)],
        out_specs=pl.BlockSpec((tt, A_DIM + B_DIM), lambda i: (i, 0)),
        out_shape=jax.ShapeDtypeStruct((t, A_DIM + B_DIM), BF16),
        scratch_shapes=[pltpu.VMEM((A_HALO + tt, A_DIM), F32), pltpu.VMEM((B_HALO + tt, B_DIM), F32)],
        compiler_params=_params("parallel"),
    )(z, z, caw, cab, lag, lab, cbw)


def _mixer_bwd_local(z, dmix, seq, caw, cab, lag, lab, cbw):
    t = z.shape[0]
    tt = _tile(seq, 256)
    tiles_per_seq, z_spec, zh_spec = _mixer_specs(t, seq, tt)

    def body(z_ref, zh_ref, dmix_ref, caw_ref, cab_ref, lag_ref, lab_ref, cbw_ref,
             da2_ref, dcv_ref, dcaw_ref, dcab_ref, dlag_ref, dlab_ref, dcbw_ref, a1_s, cb_s):
        @pl.when(pl.program_id(0) == 0)
        def _():
            for ref in (dcaw_ref, dcab_ref, dlag_ref, dlab_ref, dcbw_ref):
                ref[...] = jnp.zeros_like(ref)

        first = (pl.program_id(0) % tiles_per_seq) == 0
        _mixer_windows(z_ref, zh_ref, first, a1_s, cb_s, tt)
        a2 = _causal_conv(a1_s, caw_ref, A_TAPS, A_HALO, tt) + cab_ref[...]
        xhat, rstd = _layer_norm_stats(a2)
        a3 = xhat * lag_ref[...] + lab_ref[...]
        s3 = _sigmoid(a3)
        da3 = dmix_ref[:, 0:A_DIM] * (s3 * (1.0 + a3 * (1.0 - s3)))
        dlag_ref[...] += _colsum(da3 * xhat)
        dlab_ref[...] += _colsum(da3)
        dxh = da3 * lag_ref[...]
        da2 = rstd * (dxh - jnp.mean(dxh, axis=-1, keepdims=True) - xhat * jnp.mean(dxh * xhat, axis=-1, keepdims=True))
        da2_ref[...] = da2
        dcab_ref[...] += _colsum(da2)
        for k in range(A_TAPS):
            dcaw_ref[k:k + 1, :] += _colsum(da2 * a1_s[pl.ds(A_HALO - (A_TAPS - 1) + k, tt), :])
        dcv = dmix_ref[:, A_DIM:A_DIM + B_DIM] * z_ref[:, 2 * A_DIM:3 * A_DIM]
        dcv_ref[...] = dcv
        for k in range(B_TAPS):
            dcbw_ref[k:k + 1, :] += _colsum(dcv * cb_s[pl.ds(B_HALO - (B_TAPS - 1) + k, tt), :])

    half = pl.BlockSpec((tt, A_DIM), lambda i: (i, 0))
    return pl.pallas_call(
        body, name="mixer_bwd_local", grid=(t // tt,),
        in_specs=[z_spec, zh_spec, pl.BlockSpec((tt, A_DIM + B_DIM), lambda i: (i, 0)),
                  _vec_spec(A_TAPS, A_DIM), _vec_spec(1, A_DIM), _vec_spec(1, A_DIM), _vec_spec(1, A_DIM), _vec_spec(B_TAPS, B_DIM)],
        out_specs=[half, half, _vec_spec(A_TAPS, A_DIM), _vec_spec(1, A_DIM), _vec_spec(1, A_DIM), _vec_spec(1, A_DIM),
                   _vec_spec(B_TAPS, B_DIM)],
        out_shape=[jax.ShapeDtypeStruct((t, A_DIM), F32), jax.ShapeDtypeStruct((t, B_DIM), F32),
                   jax.ShapeDtypeStruct((A_TAPS, A_DIM), F32), jax.ShapeDtypeStruct((1, A_DIM), F32),
                   jax.ShapeDtypeStruct((1, A_DIM), F32), jax.ShapeDtypeStruct((1, A_DIM), F32),
                   jax.ShapeDtypeStruct((B_TAPS, B_DIM), F32)],
        scratch_shapes=[pltpu.VMEM((A_HALO + tt, A_DIM), F32), pltpu.VMEM((B_HALO + tt, B_DIM), F32)],
        compiler_params=_params("arbitrary"),
    )(z, z, dmix, caw, cab, lag, lab, cbw)


def _mixer_bwd_input(z, dmix, da2, dcv, seq, caw, cbw):
    t = z.shape[0]
    tt = _tile(seq, 256)
    tiles_per_seq, z_spec, zh_spec = _mixer_specs(t, seq, tt)
    a_blocks = tt // A_HALO
    b_blocks = tt // B_HALO
    last_a = t // A_HALO - 1
    last_b = t // B_HALO - 1

    def body(z_ref, zh_ref, dmix_ref, da2_ref, da2n_ref, dcv_ref, dcvn_ref, caw_ref, cbw_ref, dz_ref, a1_s, cb_s, da2_s, dcv_s):
        pos = pl.program_id(0) % tiles_per_seq
        first = pos == 0
        last = pos == tiles_per_seq - 1
        sig = _mixer_windows(z_ref, zh_ref, first, a1_s, cb_s, tt)
        da2_s[0:tt, :] = da2_ref[...]
        da2_s[tt:tt + A_HALO, :] = jnp.where(last, 0.0, da2n_ref[...])
        dcv_s[0:tt, :] = dcv_ref[...]
        dcv_s[tt:tt + B_HALO, :] = jnp.where(last, 0.0, dcvn_ref[...])
        da1 = caw_ref[0:1, :] * da2_s[pl.ds(A_TAPS - 1, tt), :]
        for k in range(1, A_TAPS):
            da1 = da1 + caw_ref[k:k + 1, :] * da2_s[pl.ds(A_TAPS - 1 - k, tt), :]
        dz_ref[:, 0:A_DIM] = (da1 * sig).astype(BF16)
        dz_ref[:, A_DIM:2 * A_DIM] = (da1 * z_ref[:, 0:A_DIM] * sig * (1.0 - sig)).astype(BF16)
        cv = _causal_conv(cb_s, cbw_ref, B_TAPS, B_HALO, tt)
        dz_ref[:, 2 * A_DIM:3 * A_DIM] = (dmix_ref[:, A_DIM:A_DIM + B_DIM] * cv).astype(BF16)
        dcb = cbw_ref[0:1, :] * dcv_s[pl.ds(B_TAPS - 1, tt), :]
        for k in range(1, B_TAPS):
            dcb = dcb + cbw_ref[k:k + 1, :] * dcv_s[pl.ds(B_TAPS - 1 - k, tt), :]
        dz_ref[:, 3 * A_DIM:4 * A_DIM] = (dcb * z_ref[:, 4 * A_DIM:5 * A_DIM]).astype(BF16)
        dz_ref[:, 4 * A_DIM:5 * A_DIM] = (dcb * z_ref[:, 3 * A_DIM:4 * A_DIM]).astype(BF16)

    half = pl.BlockSpec((tt, A_DIM), lambda i: (i, 0))
    return pl.pallas_call(
        body, name="mixer_bwd_input", grid=(t // tt,),
        in_specs=[z_spec, zh_spec, pl.BlockSpec((tt, A_DIM + B_DIM), lambda i: (i, 0)),
                  half, pl.BlockSpec((A_HALO, A_DIM), lambda i: (jnp.minimum((i + 1) * a_blocks, last_a), 0)),
                  half, pl.BlockSpec((B_HALO, B_DIM), lambda i: (jnp.minimum((i + 1) * b_blocks, last_b), 0)),
                  _vec_spec(A_TAPS, A_DIM), _vec_spec(B_TAPS, B_DIM)],
        out_specs=pl.BlockSpec((tt, IN_EVEN), lambda i: (i, 0)),
        out_shape=jax.ShapeDtypeStruct((t, IN_EVEN), BF16),
        scratch_shapes=[pltpu.VMEM((A_HALO + tt, A_DIM), F32), pltpu.VMEM((B_HALO + tt, B_DIM), F32),
                        pltpu.VMEM((tt + A_HALO, A_DIM), F32), pltpu.VMEM((tt + B_HALO, B_DIM), F32)],
        compiler_params=_params("parallel"),
    )(z, z, dmix, da2, da2, dcv, dcv, caw, cbw)


def _tril_ws(ws_ref, g):
    rows = lax.broadcasted_iota(jnp.int32, (CHUNK, CHUNK), 0)
    cols = lax.broadcasted_iota(jnp.int32, (CHUNK, CHUNK), 1)
    return jnp.where(rows >= cols, ws_ref[g], 0.0).astype(BF16), rows >= cols


def _sgu_fwd(pre, lvg, lvb, ws, bs_b):
    t = pre.shape[0]
    tt = _tile(t, 256)

    def body(pre_ref, lvg_ref, lvb_ref, ws_ref, bsb_ref, y_ref):
        vhat, _ = _layer_norm_stats(_gelu(pre_ref[:, C_DIM:2 * C_DIM]))
        vl = (vhat * lvg_ref[...] + lvb_ref[...]).astype(BF16)
        for g in range(C_GROUPS):
            w, _ = _tril_ws(ws_ref, g)
            cols = slice(g * CHUNK, (g + 1) * CHUNK)
            for ci in range(tt // CHUNK):
                rows = slice(ci * CHUNK, (ci + 1) * CHUNK)
                sv = jnp.dot(w, vl[rows, cols], preferred_element_type=F32) + bsb_ref[g]
                y_ref[rows, cols] = (_gelu(pre_ref[rows, cols]) * sv).astype(BF16)

    return pl.pallas_call(
        body, name="sgu_fwd", grid=(t // tt,),
        in_specs=[pl.BlockSpec((tt, 2 * C_DIM), lambda i: (i, 0)), _vec_spec(1, C_DIM), _vec_spec(1, C_DIM),
                  pl.BlockSpec((C_GROUPS, CHUNK, CHUNK), lambda i: (0, 0, 0)), pl.BlockSpec((C_GROUPS, CHUNK, CHUNK), lambda i: (0, 0, 0))],
        out_specs=pl.BlockSpec((tt, C_DIM), lambda i: (i, 0)), out_shape=jax.ShapeDtypeStruct((t, C_DIM), BF16),
        compiler_params=_params("parallel"),
    )(pre, lvg, lvb, ws, bs_b)


def _sgu_bwd(pre, dy, lvg, lvb, ws, bs_b):
    t = pre.shape[0]
    tt = _tile(t, 256)

    def body(pre_ref, dy_ref, lvg_ref, lvb_ref, ws_ref, bsb_ref, dpre_ref, dws_ref, dbsb_ref, dlvg_ref, dlvb_ref, dbin_ref,
             dvl_s):
        @pl.when(pl.program_id(0) == 0)
        def _():
            for ref in (dws_ref, dbsb_ref, dlvg_ref, dlvb_ref, dbin_ref):
                ref[...] = jnp.zeros_like(ref)

        pre_v = pre_ref[:, C_DIM:2 * C_DIM]
        vhat, rstd = _layer_norm_stats(_gelu(pre_v))
        vl = (vhat * lvg_ref[...] + lvb_ref[...]).astype(BF16)
        for g in range(C_GROUPS):
            w, keep = _tril_ws(ws_ref, g)
            cols = slice(g * CHUNK, (g + 1) * CHUNK)
            dws = jnp.zeros((CHUNK, CHUNK), F32)
            dbs = jnp.zeros((CHUNK, 1), F32)
            for ci in range(tt // CHUNK):
                rows = slice(ci * CHUNK, (ci + 1) * CHUNK)
                vl_g = vl[rows, cols]
                sv = jnp.dot(w, vl_g, preferred_element_type=F32) + bsb_ref[g]
                pre_u = pre_ref[rows, cols]
                dyv = dy_ref[rows, cols]
                du = dyv * sv * _gelu_grad(pre_u)
                dpre_ref[rows, cols] = du.astype(BF16)
                dbin_ref[:, cols] += _colsum(du)
                dsv = dyv * _gelu(pre_u)
                dbs = dbs + jnp.sum(dsv, axis=1, keepdims=True)
                dsv16 = dsv.astype(BF16)
                dws = dws + lax.dot_general(dsv16, vl_g, (((1,), (1,)), ((), ())), preferred_element_type=F32)
                dvl_s[rows, cols] = lax.dot_general(w, dsv16, (((0,), (0,)), ((), ())), preferred_element_type=F32)
            dws_ref[g] += jnp.where(keep, dws, 0.0)
            dbsb_ref[g] += dbs
        dvl = dvl_s[...]
        dlvg_ref[...] += _colsum(dvl * vhat)
        dlvb_ref[...] += _colsum(dvl)
        dxh = dvl * lvg_ref[...]
        dv = rstd * (dxh - jnp.mean(dxh, axis=-1, keepdims=True) - vhat * jnp.mean(dxh * vhat, axis=-1, keepdims=True))
        dpv = dv * _gelu_grad(pre_v)
        dpre_ref[:, C_DIM:2 * C_DIM] = dpv.astype(BF16)
        dbin_ref[:, C_DIM:2 * C_DIM] += _colsum(dpv)

    return pl.pallas_call(
        body, name="sgu_bwd", grid=(t // tt,),
        in_specs=[pl.BlockSpec((tt, 2 * C_DIM), lambda i: (i, 0)), pl.BlockSpec((tt, C_DIM), lambda i: (i, 0)),
                  _vec_spec(1, C_DIM), _vec_spec(1, C_DIM),
                  pl.BlockSpec((C_GROUPS, CHUNK, CHUNK), lambda i: (0, 0, 0)), pl.BlockSpec((C_GROUPS, CHUNK, CHUNK), lambda i: (0, 0, 0))],
        out_specs=[pl.BlockSpec((tt, 2 * C_DIM), lambda i: (i, 0)),
                   pl.BlockSpec((C_GROUPS, CHUNK, CHUNK), lambda i: (0, 0, 0)), pl.BlockSpec((C_GROUPS, CHUNK, CHUNK), lambda i: (0, 0, 0)),
                   _vec_spec(1, C_DIM), _vec_spec(1, C_DIM), _vec_spec(1, 2 * C_DIM)],
        out_shape=[jax.ShapeDtypeStruct((t, 2 * C_DIM), BF16), jax.ShapeDtypeStruct((C_GROUPS, CHUNK, CHUNK), F32),
                   jax.ShapeDtypeStruct((C_GROUPS, CHUNK, CHUNK), F32), jax.ShapeDtypeStruct((1, C_DIM), F32),
                   jax.ShapeDtypeStruct((1, C_DIM), F32), jax.ShapeDtypeStruct((1, 2 * C_DIM), F32)],
        scratch_shapes=[pltpu.VMEM((tt, C_DIM), F32)],
        compiler_params=_params("arbitrary"),
    )(pre, dy, lvg, lvb, ws, bs_b)


HBM_SPEC = pl.BlockSpec(memory_space=pltpu.HBM)


def _place():
    x, y, c = lax.axis_index("x"), lax.axis_index("y"), lax.axis_index("c")
    return x, y, c, [(1 - x, y), (x, 1 - y), (1 - x, 1 - y)]


def _all_gather(name, shards):
    nw = len(shards)

    def body(*refs):
        ins, outs = refs[:nw], refs[nw:2 * nw]
        send_sems, recv_sems, local_sems = refs[2 * nw:]
        x, y, c, chips = _place()
        me, sibling = (x, y, c), (x, y, 1 - c)

        def slot(w, p):
            return outs[w].at[4 * p[0] + 2 * p[1] + p[2]]

        def copy(w, k, block, to, src=None):
            return pltpu.make_async_remote_copy(
                src_ref=slot(w, block) if src is None else src, dst_ref=slot(w, block),
                send_sem=send_sems.at[w, k], recv_sem=recv_sems.at[w, k], device_id=to, device_id_type=MESH)

        started = []
        for w in range(nw):
            mine = pltpu.make_async_copy(ins[w], slot(w, me), local_sems.at[w])
            mine.start()
            started.append(mine)
        sends = []
        for w in range(nw):
            first = [copy(w, 0, me, sibling, src=ins[w])]
            first += [copy(w, 1 + j, me, (*chip, c), src=ins[w]) for j, chip in enumerate(chips)]
            for cp in first:
                cp.start()
            sends += first
        for w in range(nw):
            for j, chip in enumerate(chips):
                copy(w, 1 + j, (*chip, c), me).wait_recv()
                passed = copy(w, 4 + j, (*chip, c), sibling)
                passed.start()
                sends.append(passed)
        for w in range(nw):
            copy(w, 0, sibling, me).wait_recv()
            for j, chip in enumerate(chips):
                copy(w, 4 + j, (*chip, 1 - c), me).wait_recv()
        for cp in sends:
            cp.wait_send()
        for mine in started:
            mine.wait()

    return pl.pallas_call(
        body, name=name,
        in_specs=[HBM_SPEC] * nw, out_specs=[HBM_SPEC] * nw,
        out_shape=[jax.ShapeDtypeStruct((N_DEV, *s.shape), s.dtype) for s in shards],
        scratch_shapes=[pltpu.SemaphoreType.DMA((nw, 7)), pltpu.SemaphoreType.DMA((nw, 7)), pltpu.SemaphoreType.DMA((nw,))],
    )(*shards)


def _pair_exchange(name, parts):
    nw = len(parts)

    def body(*refs):
        ins, outs = refs[:nw], refs[nw:2 * nw]
        send_sems, recv_sems = refs[2 * nw:]
        x, y, c, _ = _place()
        copies = []
        for w in range(nw):
            for q in range(N_CHIP):
                cp = pltpu.make_async_remote_copy(
                    src_ref=ins[w].at[2 * q + 1 - c], dst_ref=outs[w].at[q],
                    send_sem=send_sems.at[w, q], recv_sem=recv_sems.at[w, q], device_id=(x, y, 1 - c), device_id_type=MESH)
                cp.start()
                copies.append(cp)
        for cp in copies:
            cp.wait()

    return pl.pallas_call(
        body, name=name, in_specs=[HBM_SPEC] * nw, out_specs=[HBM_SPEC] * nw,
        out_shape=[jax.ShapeDtypeStruct((N_CHIP, *p.shape[1:]), p.dtype) for p in parts],
        scratch_shapes=[pltpu.SemaphoreType.DMA((nw, N_CHIP)), pltpu.SemaphoreType.DMA((nw, N_CHIP))],
    )(*parts)


def _pair_sum(name, part, got, core):
    _, k, n = part.shape
    tk = _tile(k, 256)

    def body(core_ref, p_ref, s_ref, o_ref):
        o_ref[...] = p_ref[...] + s_ref[...]

    return pl.pallas_call(
        body, name=name,
        grid_spec=pltpu.PrefetchScalarGridSpec(
            num_scalar_prefetch=1, grid=(N_CHIP, k // tk),
            in_specs=[pl.BlockSpec((None, tk, n), lambda q, i, core_ref: (2 * q + core_ref[0], i, 0)),
                      pl.BlockSpec((None, tk, n), lambda q, i, core_ref: (q, i, 0))],
            out_specs=pl.BlockSpec((None, tk, n), lambda q, i, core_ref: (q, i, 0))),
        out_shape=jax.ShapeDtypeStruct((N_CHIP, k, n), F32),
        compiler_params=_params("parallel", "parallel"),
    )(core, part, got)


def _chip_exchange(name, sums):
    nw = len(sums)

    def body(*refs):
        ins, outs = refs[:nw], refs[nw:2 * nw]
        send_sems, recv_sems, local_sems = refs[2 * nw:]
        x, y, c, chips = _place()
        my_chip = 2 * x + y
        copies, locals_ = [], []
        for w in range(nw):
            mine = pltpu.make_async_copy(ins[w].at[my_chip], outs[w].at[my_chip], local_sems.at[w])
            mine.start()
            locals_.append(mine)
            for j, chip in enumerate(chips):
                cp = pltpu.make_async_remote_copy(
                    src_ref=ins[w].at[2 * chip[0] + chip[1]], dst_ref=outs[w].at[my_chip],
                    send_sem=send_sems.at[w, j], recv_sem=recv_sems.at[w, j], device_id=(*chip, c), device_id_type=MESH)
                cp.start()
                copies.append(cp)
        for cp in copies:
            cp.wait()
        for mine in locals_:
            mine.wait()

    return pl.pallas_call(
        body, name=name, in_specs=[HBM_SPEC] * nw, out_specs=[HBM_SPEC] * nw,
        out_shape=[jax.ShapeDtypeStruct(s.shape, s.dtype) for s in sums],
        scratch_shapes=[pltpu.SemaphoreType.DMA((nw, 3)), pltpu.SemaphoreType.DMA((nw, 3)), pltpu.SemaphoreType.DMA((nw,))],
    )(*sums)


def _adamw_math(w, g, m, v):
    m = ADAM_B1 * m + (1.0 - ADAM_B1) * g
    v = ADAM_B2 * v + (1.0 - ADAM_B2) * (g * g)
    m_hat = m / (1.0 - ADAM_B1 ** ADAM_STEP)
    v_hat = v / (1.0 - ADAM_B2 ** ADAM_STEP)
    delta = -ADAM_LR * (m_hat / (jnp.sqrt(v_hat) + ADAM_EPS) + ADAM_WD * w)
    return delta, m, v


def _sum_adamw(name, parts, w, m, v):
    n_parts, k, n = parts.shape
    tk = _tile(k, 128)

    def body(p_ref, w_ref, m_ref, v_ref, g_ref, d_ref, nm_ref, nv_ref):
        g = p_ref[0]
        for q in range(1, n_parts):
            g = g + p_ref[q]
        g_ref[...] = g
        d_ref[...], nm_ref[...], nv_ref[...] = _adamw_math(w_ref[...], g, m_ref[...], v_ref[...])

    blk = pl.BlockSpec((tk, n), lambda i: (i, 0))
    return pl.pallas_call(
        body, name=name, grid=(k // tk,),
        in_specs=[pl.BlockSpec((n_parts, tk, n), lambda i: (0, i, 0)), blk, blk, blk], out_specs=[blk] * 4,
        out_shape=[jax.ShapeDtypeStruct((k, n), F32)] * 4,
        compiler_params=_params("parallel"),
    )(parts, w, m, v)


def _sum_parts(name, parts):
    n_parts, k, n = parts.shape

    def body(p_ref, g_ref):
        g = p_ref[0]
        for q in range(1, n_parts):
            g = g + p_ref[q]
        g_ref[...] = g

    return pl.pallas_call(body, name=name, out_shape=jax.ShapeDtypeStruct((k, n), F32))(parts)


def _adamw(name, w, g, m, v):
    def body(w_ref, g_ref, m_ref, v_ref, d_ref, nm_ref, nv_ref):
        d_ref[...], nm_ref[...], nv_ref[...] = _adamw_math(w_ref[...], g_ref[...], m_ref[...], v_ref[...])

    return pl.pallas_call(body, name=name, out_shape=[jax.ShapeDtypeStruct(w.shape, F32)] * 3)(w, g, m, v)


LANES = 128
PACK_ROWS = 8


def _pack(arrays):
    flat = []
    for a in arrays:
        a = a.reshape(-1)
        flat.append(jnp.pad(a, (0, (-a.shape[0]) % (PACK_ROWS * LANES))))
    return jnp.concatenate(flat).reshape(-1, LANES)


def _unpack(packed, shapes):
    out, row = [], 0
    for shape in shapes:
        size = math.prod(shape)
        rows = -(-size // (PACK_ROWS * LANES)) * PACK_ROWS
        out.append(packed[row:row + rows].reshape(-1)[:size].reshape(shape))
        row += rows
    return out


SMALL = ("ev_norm_g", "ev_conv_a_w", "ev_conv_a_b", "ev_ln_a_g", "ev_ln_a_b", "ev_conv_b_w", "od_norm_g", "od_b_in",
         "od_ln_v_g", "od_ln_v_b", "od_w_s", "od_b_s", "mlp_norm_g", "final_norm_g")
SMALL_SHARDED = ("ev_conv_a_w", "ev_conv_b_w", "od_norm_g", "od_b_in", "od_ln_v_g", "od_ln_v_b")
LARGE = ("ev_w_in", "ev_w_out", "od_w_in", "od_w_out", "mlp_w1", "mlp_w2")
ORDER = ("ev_norm_g", "ev_w_in", "ev_conv_a_w", "ev_conv_a_b", "ev_ln_a_g", "ev_ln_a_b", "ev_conv_b_w", "ev_w_out",
         "od_norm_g", "od_w_in", "od_b_in", "od_ln_v_g", "od_ln_v_b", "od_w_s", "od_b_s", "od_w_out", "mlp_norm_g",
         "mlp_w1", "mlp_w2", "final_norm_g")


def _gather_small_shards(name, shards):
    packed = _pack(shards)
    every = _all_gather(name, [packed])[0]
    per_dev = [_unpack(every[d], [s.shape for s in shards]) for d in range(N_DEV)]
    return [jnp.concatenate([per_dev[d][i] for d in range(N_DEV)], axis=-1) for i in range(len(shards))]


def kernel(x, ev_norm_g, ev_w_in, ev_conv_a_w, ev_conv_a_b, ev_ln_a_g, ev_ln_a_b, ev_conv_b_w, ev_w_out, od_norm_g, od_w_in, od_b_in, od_ln_v_g, od_ln_v_b, od_w_s, od_b_s, od_w_out, mlp_norm_g, mlp_w1, mlp_w2, final_norm_g, loss_target, m_ev_norm_g, m_ev_w_in, m_ev_conv_a_w, m_ev_conv_a_b, m_ev_ln_a_g, m_ev_ln_a_b, m_ev_conv_b_w, m_ev_w_out, m_od_norm_g, m_od_w_in, m_od_b_in, m_od_ln_v_g, m_od_ln_v_b, m_od_w_s, m_od_b_s, m_od_w_out, m_mlp_norm_g, m_mlp_w1, m_mlp_w2, m_final_norm_g, v_ev_norm_g, v_ev_w_in, v_ev_conv_a_w, v_ev_conv_a_b, v_ev_ln_a_g, v_ev_ln_a_b, v_ev_conv_b_w, v_ev_w_out, v_od_norm_g, v_od_w_in, v_od_b_in, v_od_ln_v_g, v_od_ln_v_b, v_od_w_s, v_od_b_s, v_od_w_out, v_mlp_norm_g, v_mlp_w1, v_mlp_w2, v_final_norm_g):
    W = dict(ev_norm_g=ev_norm_g, ev_w_in=ev_w_in, ev_conv_a_w=ev_conv_a_w, ev_conv_a_b=ev_conv_a_b, ev_ln_a_g=ev_ln_a_g,
             ev_ln_a_b=ev_ln_a_b, ev_conv_b_w=ev_conv_b_w, ev_w_out=ev_w_out, od_norm_g=od_norm_g, od_w_in=od_w_in,
             od_b_in=od_b_in, od_ln_v_g=od_ln_v_g, od_ln_v_b=od_ln_v_b, od_w_s=od_w_s, od_b_s=od_b_s, od_w_out=od_w_out,
             mlp_norm_g=mlp_norm_g, mlp_w1=mlp_w1, mlp_w2=mlp_w2, final_norm_g=final_norm_g)
    M = dict(ev_norm_g=m_ev_norm_g, ev_w_in=m_ev_w_in, ev_conv_a_w=m_ev_conv_a_w, ev_conv_a_b=m_ev_conv_a_b,
             ev_ln_a_g=m_ev_ln_a_g, ev_ln_a_b=m_ev_ln_a_b, ev_conv_b_w=m_ev_conv_b_w, ev_w_out=m_ev_w_out,
             od_norm_g=m_od_norm_g, od_w_in=m_od_w_in, od_b_in=m_od_b_in, od_ln_v_g=m_od_ln_v_g, od_ln_v_b=m_od_ln_v_b,
             od_w_s=m_od_w_s, od_b_s=m_od_b_s, od_w_out=m_od_w_out, mlp_norm_g=m_mlp_norm_g, mlp_w1=m_mlp_w1,
             mlp_w2=m_mlp_w2, final_norm_g=m_final_norm_g)
    V = dict(ev_norm_g=v_ev_norm_g, ev_w_in=v_ev_w_in, ev_conv_a_w=v_ev_conv_a_w, ev_conv_a_b=v_ev_conv_a_b,
             ev_ln_a_g=v_ev_ln_a_g, ev_ln_a_b=v_ev_ln_a_b, ev_conv_b_w=v_ev_conv_b_w, ev_w_out=v_ev_w_out,
             od_norm_g=v_od_norm_g, od_w_in=v_od_w_in, od_b_in=v_od_b_in, od_ln_v_g=v_od_ln_v_g, od_ln_v_b=v_od_ln_v_b,
             od_w_s=v_od_w_s, od_b_s=v_od_b_s, od_w_out=v_od_w_out, mlp_norm_g=v_mlp_norm_g, mlp_w1=v_mlp_w1,
             mlp_w2=v_mlp_w2, final_norm_g=v_final_norm_g)

    n_seq, seq, d = x.shape
    t = n_seq * seq
    dev = 4 * lax.axis_index("x") + 2 * lax.axis_index("y") + lax.axis_index("c")
    core = lax.axis_index("c").astype(jnp.int32).reshape(1)

    g_ev_in, g_ev_out, g_od_in, g_od_out, g_w1, g_w2 = _all_gather(
        "gather_weights", [W[n].astype(BF16) for n in LARGE])
    w_ev_in = jnp.transpose(g_ev_in[:, 0], (1, 0, 2)).reshape(D_MODEL, IN_EVEN)
    w_ev_out = g_ev_out.reshape(D_MODEL, D_MODEL)
    w_od_in = g_od_in[:, 0]
    w_od_out = g_od_out.reshape(D_MODEL, D_MODEL)
    w_w2 = [g_w2[:, l].reshape(D_FF, D_MODEL) for l in range(2)]
    caw, cbw, od_g, od_bin, lvg, lvb = _gather_small_shards(
        "gather_small_weights", [W[n][0] for n in SMALL_SHARDED[:2]] + [W[n] for n in SMALL_SHARDED[2:]])
    ev_g, cab, lag, lab = W["ev_norm_g"], W["ev_conv_a_b"], W["ev_ln_a_g"], W["ev_ln_a_b"]
    ws = W["od_w_s"][0]
    bs_b = jnp.broadcast_to(W["od_b_s"][0][:, :, None], (C_GROUPS, CHUNK, CHUNK))
    mlp_g = [W["mlp_norm_g"][l:l + 1] for l in range(2)]
    fin_g = W["final_norm_g"].reshape(1, d)

    def mlp_fwd(l, h):
        n = _rms_fwd(f"mlp{l}_norm", h, mlp_g[l])
        q, r = _mm_nn(f"mlp{l}_up", n, g_w1, D_FF, _ep_relu_sq, [BF16, BF16], lead=l)
        h_out = _mm_nn(f"mlp{l}_down", q, w_w2[l], d, _ep_residual, [F32], extras=(h,))[0]
        return h_out, (n, q, r)

    def mlp_bwd(l, h, saved, grad, grad16):
        n, q, r = saved
        dw2 = _mm_tn(f"mlp{l}_dw2", q, grad16)
        dp = _mm_nt(f"mlp{l}_dq", grad16, w_w2[l], D_FF, _ep_relu_sq_grad, [BF16], extras=(r,))[0]
        dw1 = _mm_tn(f"mlp{l}_dw1", n, dp, col_blocks=N_DEV)
        dn = _mm_nt(f"mlp{l}_dn", dp, g_w1, d, _ep_store, [F32], lead=l)[0]
        g_out, g_out16, dg = _rms_bwd(f"mlp{l}_norm_bwd", dn, h, mlp_g[l], grad)
        return g_out, g_out16, dw1, dw2, dg

    h0 = x.reshape(t, d)
    n0 = _rms_fwd("ev_norm", h0, ev_g)
    z = _mm_nn("ev_in", n0, w_ev_in, IN_EVEN, _ep_store, [F32])[0]
    mix = _mixer_fwd(z, seq, caw, cab, lag, lab, cbw)
    h1 = _mm_nn("ev_out", mix, w_ev_out, d, _ep_residual, [F32], extras=(h0,))[0]
    h2, mlp0_saved = mlp_fwd(0, h1)
    n2 = _rms_fwd("od_norm", h2, od_g)

    pre = _mm_nn("od_in", n2, w_od_in, 2 * C_DIM, _ep_bias, [F32], bias=od_bin)[0]
    y = _sgu_fwd(pre, lvg, lvb, ws, bs_b)
    h3 = _mm_nn("od_out", y, w_od_out, d, _ep_residual, [F32], extras=(h2,))[0]
    h4, mlp1_saved = mlp_fwd(1, h3)
    grad, grad16, d_fin_g, loss_part = _final_loss("final_loss", h4, fin_g, loss_target.reshape(t, d))
    loss = lax.psum(loss_part[0, 0], ("x", "y", "c"))

    grad, grad16, dw1_1, dw2_1, dg_mlp1 = mlp_bwd(1, h3, mlp1_saved, grad, grad16)
    d_od_out = _mm_tn("od_dw_out", y, grad16)
    dy = _mm_nt("od_dy", grad16, w_od_out, C_DIM, _ep_store, [F32])[0]
    dpre, d_ws, d_bsb, d_lvg, d_lvb, d_bin = _sgu_bwd(pre, dy, lvg, lvb, ws, bs_b)
    d_od_in = _mm_tn("od_dw_in", n2, dpre, col_blocks=N_DEV)
    dn2 = _mm_nt("od_dn", dpre, w_od_in, d, _ep_store, [F32])[0]
    grad, grad16, d_od_g = _rms_bwd("od_norm_bwd", dn2, h2, od_g, grad)
    grad, grad16, dw1_0, dw2_0, dg_mlp0 = mlp_bwd(0, h1, mlp0_saved, grad, grad16)
    d_ev_out = _mm_tn("ev_dw_out", mix, grad16)
    dmix = _mm_nt("ev_dmix", grad16, w_ev_out, A_DIM + B_DIM, _ep_store, [F32])[0]
    da2, dcv, d_caw, d_cab, d_lag, d_lab, d_cbw = _mixer_bwd_local(z, dmix, seq, caw, cab, lag, lab, cbw)
    dz = _mixer_bwd_input(z, dmix, da2, dcv, seq, caw, cbw)
    d_ev_in = _mm_tn("ev_dw_in", n0, dz)
    dn0 = _mm_nt("ev_dn", dz, w_ev_in, d, _ep_store, [F32])[0]
    grad_x, _, d_ev_g = _rms_bwd("ev_norm_bwd", dn0, h0, ev_g, grad)

    parts = [
        jnp.transpose(d_ev_in.reshape(D_MODEL, N_DEV, IN_EVEN // N_DEV), (1, 0, 2)),
        d_ev_out.reshape(N_DEV, D_MODEL // N_DEV, D_MODEL),
        d_od_in,
        d_od_out.reshape(N_DEV, D_MODEL // N_DEV, D_MODEL),
        dw1_0, dw1_1,
        dw2_0.reshape(N_DEV, D_FF // N_DEV, D_MODEL), dw2_1.reshape(N_DEV, D_FF // N_DEV, D_MODEL),
    ]
    got = _pair_exchange("grads_pair_exchange", parts)
    sums = [_pair_sum(f"grads_pair_sum{i}", p, s, core) for i, (p, s) in enumerate(zip(parts, got))]
    by_chip = _chip_exchange("grads_chip_exchange", sums)
    shard = {"ev_w_in": (0, None), "ev_w_out": (1, None), "od_w_in": (2, None), "od_w_out": (3, None),
             "mlp_w1": (4, 5), "mlp_w2": (6, 7)}
    out_g, out_d, out_m, out_v = {}, {}, {}, {}
    for name, (i0, i1) in shard.items():
        res = []
        for l, i in enumerate((i0,) if i1 is None else (i0, i1)):
            res.append(_sum_adamw(f"adamw_{name}_{l}", by_chip[i], W[name][l], M[name][l], V[name][l]))
        for store, j in ((out_g, 0), (out_d, 1), (out_m, 2), (out_v, 3)):
            store[name] = jnp.stack([r[j] for r in res])

    small_parts = dict(
        ev_norm_g=d_ev_g, ev_conv_a_w=d_caw[None], ev_conv_a_b=d_cab, ev_ln_a_g=d_lag, ev_ln_a_b=d_lab,
        ev_conv_b_w=d_cbw[None], od_norm_g=d_od_g, od_b_in=d_bin, od_ln_v_g=d_lvg, od_ln_v_b=d_lvb, od_w_s=d_ws[None],
        od_b_s=d_bsb[:, :, 0][None], mlp_norm_g=jnp.concatenate([dg_mlp0, dg_mlp1], axis=0), final_norm_g=d_fin_g.reshape(d))
    full_shapes = [small_parts[n].shape for n in SMALL]
    packed = _pack([small_parts[n] for n in SMALL])
    every = _all_gather("gather_small_grads", [packed])[0]
    total = _unpack(_sum_parts("sum_small_grads", every), full_shapes)
    small_g = {}
    for n, g in zip(SMALL, total):
        if n in SMALL_SHARDED:
            width = W[n].shape[-1]
            g = lax.dynamic_slice_in_dim(g, dev * width, width, axis=g.ndim - 1)
        small_g[n] = g
    own_shapes = [W[n].shape for n in SMALL]
    packed_d, packed_m, packed_v = _adamw(
        "adamw_small", _pack([W[n] for n in SMALL]), _pack([small_g[n] for n in SMALL]),
        _pack([M[n] for n in SMALL]), _pack([V[n] for n in SMALL]))
    for n, dd, mm, vv in zip(SMALL, _unpack(packed_d, own_shapes), _unpack(packed_m, own_shapes), _unpack(packed_v, own_shapes)):
        out_g[n], out_d[n], out_m[n], out_v[n] = small_g[n], dd, mm, vv

    return (loss, grad_x.reshape(n_seq, seq, d), *[out_g[n] for n in ORDER], *[out_d[n] for n in ORDER],
            *[out_m[n] for n in ORDER], *[out_v[n] for n in ORDER])
```

```python
import functools
import math

import jax
import jax.numpy as jnp
from jax import lax
from jax.experimental import pallas as pl
from jax.experimental.pallas import tpu as pltpu

F32 = jnp.float32
BF16 = jnp.bfloat16
MESH = pl.DeviceIdType.MESH

D_MODEL = 1024
A_DIM = 512
B_DIM = 512
IN_EVEN = 2 * A_DIM + 3 * B_DIM
A_TAPS = 31
B_TAPS = 3
CHUNK = 128
C_GROUPS = 8
C_DIM = 1024
D_FF = 4096
RMS_EPS = 1e-6
LN_EPS = 1e-5
N_DEV = 8
N_CHIP = 4

ADAM_LR = 0.001
ADAM_B1 = 0.9
ADAM_B2 = 0.999
ADAM_EPS = 1e-08
ADAM_WD = 0.01
ADAM_STEP = 10

A_HALO = 32
B_HALO = 8
VMEM_LIMIT_BYTES = 56 * 1024 * 1024
INV_SQRT2 = 1.0 / math.sqrt(2.0)
INV_SQRT_2PI = 1.0 / math.sqrt(2.0 * math.pi)


def _params(*sem):
    return pltpu.CompilerParams(dimension_semantics=sem if sem else None, vmem_limit_bytes=VMEM_LIMIT_BYTES)


def _tile(n, want):
    t = min(n, want)
    while n % t:
        t //= 2
    return t


def _sigmoid(x):
    return 1.0 / (1.0 + jnp.exp(-x))


def _gelu(x):
    return 0.5 * x * (1.0 + lax.erf(x * INV_SQRT2))


def _gelu_grad(x):
    return 0.5 * (1.0 + lax.erf(x * INV_SQRT2)) + x * jnp.exp(-0.5 * x * x) * INV_SQRT_2PI


def _colsum(x):
    return jnp.sum(x, axis=0, keepdims=True)


def _matmul(name, a, b, *, kind, m, n, k, a_spec, b_spec, tm, tn, tk, out_shape, out_specs, epilogue,
            extras=(), extra_specs=()):
    dims = {"nn": (((1,), (0,)), ((), ())), "nt": (((1,), (1,)), ((), ())), "tn": (((0,), (0,)), ((), ()))}[kind]
    nk = k // tk
    n_extra = len(extras)
    n_out = len(out_shape)

    def body(a_ref, b_ref, *rest):
        extra_refs = rest[:n_extra]
        out_refs = rest[n_extra:n_extra + n_out]
        part = lax.dot_general(a_ref[...], b_ref[...], dims, preferred_element_type=F32)
        if nk == 1:
            epilogue(part, extra_refs, out_refs)
            return
        acc_ref = rest[n_extra + n_out]
        step = pl.program_id(2)

        @pl.when(step == 0)
        def _():
            acc_ref[...] = part

        @pl.when(jnp.logical_and(step > 0, step < nk - 1))
        def _():
            acc_ref[...] += part

        @pl.when(step == nk - 1)
        def _():
            epilogue(acc_ref[...] + part, extra_refs, out_refs)

    return pl.pallas_call(
        body, name=name, grid=(m // tm, n // tn, nk),
        in_specs=[a_spec, b_spec, *extra_specs], out_specs=out_specs, out_shape=out_shape,
        scratch_shapes=[pltpu.VMEM((tm, tn), F32)] if nk > 1 else [],
        compiler_params=_params("parallel", "parallel", "arbitrary"),
    )(a, b, *extras)


def _b_spec_nn(b, tk, tn, lead):
    if b.ndim == 2:
        return pl.BlockSpec((tk, tn), lambda i, j, kk: (kk, j))
    if b.ndim == 3:
        return pl.BlockSpec((None, tk, tn), lambda i, j, kk: (j, kk, 0))
    return pl.BlockSpec((None, None, tk, tn), lambda i, j, kk: (j, lead, kk, 0))


def _b_spec_nt(b, tn, tk, lead):
    if b.ndim == 2:
        return pl.BlockSpec((tn, tk), lambda i, j, kk: (j, kk))
    if b.ndim == 3:
        return pl.BlockSpec((None, tn, tk), lambda i, j, kk: (kk, j, 0))
    return pl.BlockSpec((None, None, tn, tk), lambda i, j, kk: (kk, lead, j, 0))


def _mm_nn(name, a, b, n, epilogue, out_dtypes, *, extras=(), bias=None, lead=0):
    m, k = a.shape
    blocked = b.ndim > 2
    tm = _tile(m, 1024)
    tn = b.shape[-1] if blocked else _tile(n, 1024)
    tk = _tile(k, 1024)
    tile = pl.BlockSpec((tm, tn), lambda i, j, kk: (i, j))
    extra_specs = [tile] * len(extras)
    if bias is not None:
        extras = (*extras, bias)
        extra_specs.append(pl.BlockSpec((1, tn), lambda i, j, kk: (0, j)))
    return _matmul(
        name, a, b, kind="nn", m=m, n=n, k=k, tm=tm, tn=tn, tk=tk,
        a_spec=pl.BlockSpec((tm, tk), lambda i, j, kk: (i, kk)), b_spec=_b_spec_nn(b, tk, tn, lead),
        out_shape=[jax.ShapeDtypeStruct((m, n), dt) for dt in out_dtypes], out_specs=[tile] * len(out_dtypes),
        epilogue=epilogue, extras=extras, extra_specs=extra_specs)


def _mm_nt(name, a, b, n, epilogue, out_dtypes, *, extras=(), lead=0):
    m, k = a.shape
    blocked = b.ndim > 2
    tm = _tile(m, 1024)
    tn = _tile(n, 1024)
    tk = b.shape[-1] if blocked else _tile(k, 1024)
    tile = pl.BlockSpec((tm, tn), lambda i, j, kk: (i, j))
    return _matmul(
        name, a, b, kind="nt", m=m, n=n, k=k, tm=tm, tn=tn, tk=tk,
        a_spec=pl.BlockSpec((tm, tk), lambda i, j, kk: (i, kk)), b_spec=_b_spec_nt(b, tn, tk, lead),
        out_shape=[jax.ShapeDtypeStruct((m, n), dt) for dt in out_dtypes], out_specs=[tile] * len(out_dtypes),
        epilogue=epilogue, extras=extras, extra_specs=[tile] * len(extras))


def _mm_tn(name, a, b, *, col_blocks=0):
    t, k = a.shape
    n = b.shape[1]
    tm = _tile(k, 1024)
    tn = n // col_blocks if col_blocks else _tile(n, 1024)
    tk = _tile(t, 1024)
    if col_blocks:
        out_shape = [jax.ShapeDtypeStruct((col_blocks, k, tn), F32)]
        out_specs = [pl.BlockSpec((None, tm, tn), lambda i, j, kk: (j, i, 0))]
    else:
        out_shape = [jax.ShapeDtypeStruct((k, n), F32)]
        out_specs = [pl.BlockSpec((tm, tn), lambda i, j, kk: (i, j))]

    def epilogue(acc, extra_refs, out_refs):
        out_refs[0][...] = acc

    return _matmul(
        name, a, b, kind="tn", m=k, n=n, k=t, tm=tm, tn=tn, tk=tk,
        a_spec=pl.BlockSpec((tk, tm), lambda i, j, kk: (kk, i)), b_spec=pl.BlockSpec((tk, tn), lambda i, j, kk: (kk, j)),
        out_shape=out_shape, out_specs=out_specs, epilogue=epilogue)[0]


def _ep_store(acc, extra_refs, out_refs):
    out_refs[0][...] = acc.astype(out_refs[0].dtype)


def _ep_residual(acc, extra_refs, out_refs):
    out_refs[0][...] = extra_refs[0][...] + acc


def _ep_bias(acc, extra_refs, out_refs):
    out_refs[0][...] = acc + extra_refs[0][...]


def _ep_relu_sq(acc, extra_refs, out_refs):
    r = jnp.maximum(acc, 0.0)
    out_refs[0][...] = (r * r).astype(BF16)


def _ep_relu_sq_grad(acc, extra_refs, out_refs):
    out_refs[0][...] = (acc * (2.0 * jnp.sqrt(extra_refs[0][...].astype(F32)))).astype(BF16)


def _rms_fwd(name, h, g):
    t, d = h.shape
    tt = _tile(t, 512)

    def body(h_ref, g_ref, n_ref):
        x = h_ref[...]
        r = lax.rsqrt(jnp.mean(x * x, axis=-1, keepdims=True) + RMS_EPS)
        n_ref[...] = (x * r * g_ref[...]).astype(BF16)

    return pl.pallas_call(
        body, name=name, grid=(t // tt,),
        in_specs=[pl.BlockSpec((tt, d), lambda i: (i, 0)), pl.BlockSpec((1, d), lambda i: (0, 0))],
        out_specs=pl.BlockSpec((tt, d), lambda i: (i, 0)), out_shape=jax.ShapeDtypeStruct((t, d), BF16),
        compiler_params=_params("parallel"),
    )(h, g)


def _rms_bwd(name, dn, h, g, grad_in):
    t, d = h.shape
    tt = _tile(t, 512)

    def body(dn_ref, h_ref, g_ref, gin_ref, gout_ref, gout16_ref, dg_ref):
        @pl.when(pl.program_id(0) == 0)
        def _():
            dg_ref[...] = jnp.zeros_like(dg_ref)

        x = h_ref[...]
        dnv = dn_ref[...]
        r = lax.rsqrt(jnp.mean(x * x, axis=-1, keepdims=True) + RMS_EPS)
        xr = x * r
        dg_ref[...] += _colsum(dnv * xr)
        dy = dnv * g_ref[...]
        out = gin_ref[...] + r * (dy - xr * jnp.mean(dy * xr, axis=-1, keepdims=True))
        gout_ref[...] = out
        gout16_ref[...] = out.astype(BF16)

    row = pl.BlockSpec((tt, d), lambda i: (i, 0))
    vec = pl.BlockSpec((1, d), lambda i: (0, 0))
    return pl.pallas_call(
        body, name=name, grid=(t // tt,), in_specs=[row, row, vec, row], out_specs=[row, row, vec],
        out_shape=[jax.ShapeDtypeStruct((t, d), F32), jax.ShapeDtypeStruct((t, d), BF16), jax.ShapeDtypeStruct((1, d), F32)],
        compiler_params=_params("arbitrary"),
    )(dn, h, g, grad_in)


def _final_loss(name, h, g, target):
    t, d = h.shape
    tt = _tile(t, 512)

    def body(h_ref, g_ref, tgt_ref, gout_ref, gout16_ref, dg_ref, loss_ref):
        @pl.when(pl.program_id(0) == 0)
        def _():
            dg_ref[...] = jnp.zeros_like(dg_ref)
            loss_ref[...] = jnp.zeros_like(loss_ref)

        x = h_ref[...]
        r = lax.rsqrt(jnp.mean(x * x, axis=-1, keepdims=True) + RMS_EPS)
        xr = x * r
        err = xr * g_ref[...] - tgt_ref[...]
        loss_ref[...] += 0.5 * jnp.sum(jnp.mean(err * err, axis=-1, keepdims=True), axis=0, keepdims=True)
        dout = err * (1.0 / d)
        dg_ref[...] += _colsum(dout * xr)
        dy = dout * g_ref[...]
        out = r * (dy - xr * jnp.mean(dy * xr, axis=-1, keepdims=True))
        gout_ref[...] = out
        gout16_ref[...] = out.astype(BF16)

    row = pl.BlockSpec((tt, d), lambda i: (i, 0))
    vec = pl.BlockSpec((1, d), lambda i: (0, 0))
    one = pl.BlockSpec((1, 1), lambda i: (0, 0))
    return pl.pallas_call(
        body, name=name, grid=(t // tt,), in_specs=[row, vec, row], out_specs=[row, row, vec, one],
        out_shape=[jax.ShapeDtypeStruct((t, d), F32), jax.ShapeDtypeStruct((t, d), BF16),
                   jax.ShapeDtypeStruct((1, d), F32), jax.ShapeDtypeStruct((1, 1), F32)],
        compiler_params=_params("arbitrary"),
    )(h, g, target)


def _mixer_windows(z_ref, zh_ref, first, a1_s, cb_s, tt):
    sig = _sigmoid(z_ref[:, A_DIM:2 * A_DIM])
    a1_s[A_HALO:A_HALO + tt, :] = z_ref[:, 0:A_DIM] * sig
    a1_h = zh_ref[:, 0:A_DIM] * _sigmoid(zh_ref[:, A_DIM:2 * A_DIM])
    a1_s[0:A_HALO, :] = jnp.where(first, 0.0, a1_h)
    cb_s[B_HALO:B_HALO + tt, :] = z_ref[:, 3 * A_DIM:4 * A_DIM] * z_ref[:, 4 * A_DIM:5 * A_DIM]
    cb_h = zh_ref[A_HALO - B_HALO:A_HALO, 3 * A_DIM:4 * A_DIM] * zh_ref[A_HALO - B_HALO:A_HALO, 4 * A_DIM:5 * A_DIM]
    cb_s[0:B_HALO, :] = jnp.where(first, 0.0, cb_h)
    return sig


def _causal_conv(win_s, w_ref, taps, halo, tt):
    base = halo - (taps - 1)
    acc = w_ref[0:1, :] * win_s[pl.ds(base, tt), :]
    for k in range(1, taps):
        acc = acc + w_ref[k:k + 1, :] * win_s[pl.ds(base + k, tt), :]
    return acc


def _layer_norm_stats(x):
    mu = jnp.mean(x, axis=-1, keepdims=True)
    xc = x - mu
    rstd = lax.rsqrt(jnp.mean(xc * xc, axis=-1, keepdims=True) + LN_EPS)
    return xc * rstd, rstd


def _mixer_specs(t, seq, tt):
    tiles_per_seq = seq // tt
    halo_blocks = tt // A_HALO
    z_spec = pl.BlockSpec((tt, IN_EVEN), lambda i: (i, 0))
    zh_spec = pl.BlockSpec((A_HALO, IN_EVEN), lambda i: (jnp.maximum(i * halo_blocks - 1, 0), 0))
    return tiles_per_seq, z_spec, zh_spec


def _vec_spec(rows, cols):
    return pl.BlockSpec((rows, cols), lambda i: (0, 0))


def _mixer_fwd(z, seq, caw, cab, lag, lab, cbw):
    t = z.shape[0]
    tt = _tile(seq, 256)
    tiles_per_seq, z_spec, zh_spec = _mixer_specs(t, seq, tt)

    def body(z_ref, zh_ref, caw_ref, cab_ref, lag_ref, lab_ref, cbw_ref, mix_ref, a1_s, cb_s):
        first = (pl.program_id(0) % tiles_per_seq) == 0
        _mixer_windows(z_ref, zh_ref, first, a1_s, cb_s, tt)
        a2 = _causal_conv(a1_s, caw_ref, A_TAPS, A_HALO, tt) + cab_ref[...]
        xhat, _ = _layer_norm_stats(a2)
        a3 = xhat * lag_ref[...] + lab_ref[...]
        mix_ref[:, 0:A_DIM] = (a3 * _sigmoid(a3)).astype(BF16)
        cv = _causal_conv(cb_s, cbw_ref, B_TAPS, B_HALO, tt)
        mix_ref[:, A_DIM:A_DIM + B_DIM] = (z_ref[:, 2 * A_DIM:3 * A_DIM] * cv).astype(BF16)

    return pl.pallas_call(
        body, name="mixer_fwd", grid=(t // tt,),
        in_specs=[z_spec, zh_spec, _vec_spec(A_TAPS, A_DIM), _vec_spec(1, A_DIM), _vec_spec(1, A_DIM), _vec_spec(1, A_DIM),
                  _vec_spec(B_TAPS, B_DIM)],
        out_specs=pl.BlockSpec((tt, A_DIM + B_DIM), lambda i: (i, 0)),
        out_shape=jax.ShapeDtypeStruct((t, A_DIM + B_DIM), BF16),
        scratch_shapes=[pltpu.VMEM((A_HALO + tt, A_DIM), F32), pltpu.VMEM((B_HALO + tt, B_DIM), F32)],
        compiler_params=_params("parallel"),
    )(z, z, caw, cab, lag, lab, cbw)


def _mixer_bwd_local(z, dmix, seq, caw, cab, lag, lab, cbw):
    t = z.shape[0]
    tt = _tile(seq, 256)
    tiles_per_seq, z_spec, zh_spec = _mixer_specs(t, seq, tt)

    def body(z_ref, zh_ref, dmix_ref, caw_ref, cab_ref, lag_ref, lab_ref, cbw_ref,
             da2_ref, dcv_ref, dcaw_ref, dcab_ref, dlag_ref, dlab_ref, dcbw_ref, a1_s, cb_s):
        @pl.when(pl.program_id(0) == 0)
        def _():
            for ref in (dcaw_ref, dcab_ref, dlag_ref, dlab_ref, dcbw_ref):
                ref[...] = jnp.zeros_like(ref)

        first = (pl.program_id(0) % tiles_per_seq) == 0
        _mixer_windows(z_ref, zh_ref, first, a1_s, cb_s, tt)
        a2 = _causal_conv(a1_s, caw_ref, A_TAPS, A_HALO, tt) + cab_ref[...]
        xhat, rstd = _layer_norm_stats(a2)
        a3 = xhat * lag_ref[...] + lab_ref[...]
        s3 = _sigmoid(a3)
        da3 = dmix_ref[:, 0:A_DIM] * (s3 * (1.0 + a3 * (1.0 - s3)))
        dlag_ref[...] += _colsum(da3 * xhat)
        dlab_ref[...] += _colsum(da3)
        dxh = da3 * lag_ref[...]
        da2 = rstd * (dxh - jnp.mean(dxh, axis=-1, keepdims=True) - xhat * jnp.mean(dxh * xhat, axis=-1, keepdims=True))
        da2_ref[...] = da2
        dcab_ref[...] += _colsum(da2)
        for k in range(A_TAPS):
            dcaw_ref[k:k + 1, :] += _colsum(da2 * a1_s[pl.ds(A_HALO - (A_TAPS - 1) + k, tt), :])
        dcv = dmix_ref[:, A_DIM:A_DIM + B_DIM] * z_ref[:, 2 * A_DIM:3 * A_DIM]
        dcv_ref[...] = dcv
        for k in range(B_TAPS):
            dcbw_ref[k:k + 1, :] += _colsum(dcv * cb_s[pl.ds(B_HALO - (B_TAPS - 1) + k, tt), :])

    half = pl.BlockSpec((tt, A_DIM), lambda i: (i, 0))
    return pl.pallas_call(
        body, name="mixer_bwd_local", grid=(t // tt,),
        in_specs=[z_spec, zh_spec, pl.BlockSpec((tt, A_DIM + B_DIM), lambda i: (i, 0)),
                  _vec_spec(A_TAPS, A_DIM), _vec_spec(1, A_DIM), _vec_spec(1, A_DIM), _vec_spec(1, A_DIM), _vec_spec(B_TAPS, B_DIM)],
        out_specs=[half, half, _vec_spec(A_TAPS, A_DIM), _vec_spec(1, A_DIM), _vec_spec(1, A_DIM), _vec_spec(1, A_DIM),
                   _vec_spec(B_TAPS, B_DIM)],
        out_shape=[jax.ShapeDtypeStruct((t, A_DIM), F32), jax.ShapeDtypeStruct((t, B_DIM), F32),
                   jax.ShapeDtypeStruct((A_TAPS, A_DIM), F32), jax.ShapeDtypeStruct((1, A_DIM), F32),
                   jax.ShapeDtypeStruct((1, A_DIM), F32), jax.ShapeDtypeStruct((1, A_DIM), F32),
                   jax.ShapeDtypeStruct((B_TAPS, B_DIM), F32)],
        scratch_shapes=[pltpu.VMEM((A_HALO + tt, A_DIM), F32), pltpu.VMEM((B_HALO + tt, B_DIM), F32)],
        compiler_params=_params("arbitrary"),
    )(z, z, dmix, caw, cab, lag, lab, cbw)


def _mixer_bwd_input(z, dmix, da2, dcv, seq, caw, cbw):
    t = z.shape[0]
    tt = _tile(seq, 256)
    tiles_per_seq, z_spec, zh_spec = _mixer_specs(t, seq, tt)
    a_blocks = tt // A_HALO
    b_blocks = tt // B_HALO
    last_a = t // A_HALO - 1
    last_b = t // B_HALO - 1

    def body(z_ref, zh_ref, dmix_ref, da2_ref, da2n_ref, dcv_ref, dcvn_ref, caw_ref, cbw_ref, dz_ref, a1_s, cb_s, da2_s, dcv_s):
        pos = pl.program_id(0) % tiles_per_seq
        first = pos == 0
        last = pos == tiles_per_seq - 1
        sig = _mixer_windows(z_ref, zh_ref, first, a1_s, cb_s, tt)
        da2_s[0:tt, :] = da2_ref[...]
        da2_s[tt:tt + A_HALO, :] = jnp.where(last, 0.0, da2n_ref[...])
        dcv_s[0:tt, :] = dcv_ref[...]
        dcv_s[tt:tt + B_HALO, :] = jnp.where(last, 0.0, dcvn_ref[...])
        da1 = caw_ref[0:1, :] * da2_s[pl.ds(A_TAPS - 1, tt), :]
        for k in range(1, A_TAPS):
            da1 = da1 + caw_ref[k:k + 1, :] * da2_s[pl.ds(A_TAPS - 1 - k, tt), :]
        dz_ref[:, 0:A_DIM] = (da1 * sig).astype(BF16)
        dz_ref[:, A_DIM:2 * A_DIM] = (da1 * z_ref[:, 0:A_DIM] * sig * (1.0 - sig)).astype(BF16)
        cv = _causal_conv(cb_s, cbw_ref, B_TAPS, B_HALO, tt)
        dz_ref[:, 2 * A_DIM:3 * A_DIM] = (dmix_ref[:, A_DIM:A_DIM + B_DIM] * cv).astype(BF16)
        dcb = cbw_ref[0:1, :] * dcv_s[pl.ds(B_TAPS - 1, tt), :]
        for k in range(1, B_TAPS):
            dcb = dcb + cbw_ref[k:k + 1, :] * dcv_s[pl.ds(B_TAPS - 1 - k, tt), :]
        dz_ref[:, 3 * A_DIM:4 * A_DIM] = (dcb * z_ref[:, 4 * A_DIM:5 * A_DIM]).astype(BF16)
        dz_ref[:, 4 * A_DIM:5 * A_DIM] = (dcb * z_ref[:, 3 * A_DIM:4 * A_DIM]).astype(BF16)

    half = pl.BlockSpec((tt, A_DIM), lambda i: (i, 0))
    return pl.pallas_call(
        body, name="mixer_bwd_input", grid=(t // tt,),
        in_specs=[z_spec, zh_spec, pl.BlockSpec((tt, A_DIM + B_DIM), lambda i: (i, 0)),
                  half, pl.BlockSpec((A_HALO, A_DIM), lambda i: (jnp.minimum((i + 1) * a_blocks, last_a), 0)),
                  half, pl.BlockSpec((B_HALO, B_DIM), lambda i: (jnp.minimum((i + 1) * b_blocks, last_b), 0)),
                  _vec_spec(A_TAPS, A_DIM), _vec_spec(B_TAPS, B_DIM)],
        out_specs=pl.BlockSpec((tt, IN_EVEN), lambda i: (i, 0)),
        out_shape=jax.ShapeDtypeStruct((t, IN_EVEN), BF16),
        scratch_shapes=[pltpu.VMEM((A_HALO + tt, A_DIM), F32), pltpu.VMEM((B_HALO + tt, B_DIM), F32),
                        pltpu.VMEM((tt + A_HALO, A_DIM), F32), pltpu.VMEM((tt + B_HALO, B_DIM), F32)],
        compiler_params=_params("parallel"),
    )(z, z, dmix, da2, da2, dcv, dcv, caw, cbw)


def _tril_ws(ws_ref, g):
    rows = lax.broadcasted_iota(jnp.int32, (CHUNK, CHUNK), 0)
    cols = lax.broadcasted_iota(jnp.int32, (CHUNK, CHUNK), 1)
    return jnp.where(rows >= cols, ws_ref[g], 0.0).astype(BF16), rows >= cols


def _sgu_fwd(pre, lvg, lvb, ws, bs_b):
    t = pre.shape[0]
    tt = _tile(t, 256)

    def body(pre_ref, lvg_ref, lvb_ref, ws_ref, bsb_ref, y_ref):
        vhat, _ = _layer_norm_stats(_gelu(pre_ref[:, C_DIM:2 * C_DIM]))
        vl = (vhat * lvg_ref[...] + lvb_ref[...]).astype(BF16)
        for g in range(C_GROUPS):
            w, _ = _tril_ws(ws_ref, g)
            cols = slice(g * CHUNK, (g + 1) * CHUNK)
            for ci in range(tt // CHUNK):
                rows = slice(ci * CHUNK, (ci + 1) * CHUNK)
                sv = jnp.dot(w, vl[rows, cols], preferred_element_type=F32) + bsb_ref[g]
                y_ref[rows, cols] = (_gelu(pre_ref[rows, cols]) * sv).astype(BF16)

    return pl.pallas_call(
        body, name="sgu_fwd", grid=(t // tt,),
        in_specs=[pl.BlockSpec((tt, 2 * C_DIM), lambda i: (i, 0)), _vec_spec(1, C_DIM), _vec_spec(1, C_DIM),
                  pl.BlockSpec((C_GROUPS, CHUNK, CHUNK), lambda i: (0, 0, 0)), pl.BlockSpec((C_GROUPS, CHUNK, CHUNK), lambda i: (0, 0, 0))],
        out_specs=pl.BlockSpec((tt, C_DIM), lambda i: (i, 0)), out_shape=jax.ShapeDtypeStruct((t, C_DIM), BF16),
        compiler_params=_params("parallel"),
    )(pre, lvg, lvb, ws, bs_b)


def _sgu_bwd(pre, dy, lvg, lvb, ws, bs_b):
    t = pre.shape[0]
    tt = _tile(t, 256)

    def body(pre_ref, dy_ref, lvg_ref, lvb_ref, ws_ref, bsb_ref, dpre_ref, dws_ref, dbsb_ref, dlvg_ref, dlvb_ref, dbin_ref,
             dvl_s):
        @pl.when(pl.program_id(0) == 0)
        def _():
            for ref in (dws_ref, dbsb_ref, dlvg_ref, dlvb_ref, dbin_ref):
                ref[...] = jnp.zeros_like(ref)

        pre_v = pre_ref[:, C_DIM:2 * C_DIM]
        vhat, rstd = _layer_norm_stats(_gelu(pre_v))
        vl = (vhat * lvg_ref[...] + lvb_ref[...]).astype(BF16)
        for g in range(C_GROUPS):
            w, keep = _tril_ws(ws_ref, g)
            cols = slice(g * CHUNK, (g + 1) * CHUNK)
            dws = jnp.zeros((CHUNK, CHUNK), F32)
            dbs = jnp.zeros((CHUNK, 1), F32)
            for ci in range(tt // CHUNK):
                rows = slice(ci * CHUNK, (ci + 1) * CHUNK)
                vl_g = vl[rows, cols]
                sv = jnp.dot(w, vl_g, preferred_element_type=F32) + bsb_ref[g]
                pre_u = pre_ref[rows, cols]
                dyv = dy_ref[rows, cols]
                du = dyv * sv * _gelu_grad(pre_u)
                dpre_ref[rows, cols] = du.astype(BF16)
                dbin_ref[:, cols] += _colsum(du)
                dsv = dyv * _gelu(pre_u)
                dbs = dbs + jnp.sum(dsv, axis=1, keepdims=True)
                dsv16 = dsv.astype(BF16)
                dws = dws + lax.dot_general(dsv16, vl_g, (((1,), (1,)), ((), ())), preferred_element_type=F32)
                dvl_s[rows, cols] = lax.dot_general(w, dsv16, (((0,), (0,)), ((), ())), preferred_element_type=F32)
            dws_ref[g] += jnp.where(keep, dws, 0.0)
            dbsb_ref[g] += dbs
        dvl = dvl_s[...]
        dlvg_ref[...] += _colsum(dvl * vhat)
        dlvb_ref[...] += _colsum(dvl)
        dxh = dvl * lvg_ref[...]
        dv = rstd * (dxh - jnp.mean(dxh, axis=-1, keepdims=True) - vhat * jnp.mean(dxh * vhat, axis=-1, keepdims=True))
        dpv = dv * _gelu_grad(pre_v)
        dpre_ref[:, C_DIM:2 * C_DIM] = dpv.astype(BF16)
        dbin_ref[:, C_DIM:2 * C_DIM] += _colsum(dpv)

    return pl.pallas_call(
        body, name="sgu_bwd", grid=(t // tt,),
        in_specs=[pl.BlockSpec((tt, 2 * C_DIM), lambda i: (i, 0)), pl.BlockSpec((tt, C_DIM), lambda i: (i, 0)),
                  _vec_spec(1, C_DIM), _vec_spec(1, C_DIM),
                  pl.BlockSpec((C_GROUPS, CHUNK, CHUNK), lambda i: (0, 0, 0)), pl.BlockSpec((C_GROUPS, CHUNK, CHUNK), lambda i: (0, 0, 0))],
        out_specs=[pl.BlockSpec((tt, 2 * C_DIM), lambda i: (i, 0)),
                   pl.BlockSpec((C_GROUPS, CHUNK, CHUNK), lambda i: (0, 0, 0)), pl.BlockSpec((C_GROUPS, CHUNK, CHUNK), lambda i: (0, 0, 0)),
                   _vec_spec(1, C_DIM), _vec_spec(1, C_DIM), _vec_spec(1, 2 * C_DIM)],
        out_shape=[jax.ShapeDtypeStruct((t, 2 * C_DIM), BF16), jax.ShapeDtypeStruct((C_GROUPS, CHUNK, CHUNK), F32),
                   jax.ShapeDtypeStruct((C_GROUPS, CHUNK, CHUNK), F32), jax.ShapeDtypeStruct((1, C_DIM), F32),
                   jax.ShapeDtypeStruct((1, C_DIM), F32), jax.ShapeDtypeStruct((1, 2 * C_DIM), F32)],
        scratch_shapes=[pltpu.VMEM((tt, C_DIM), F32)],
        compiler_params=_params("arbitrary"),
    )(pre, dy, lvg, lvb, ws, bs_b)


HBM_SPEC = pl.BlockSpec(memory_space=pltpu.HBM)


def _place():
    x, y, c = lax.axis_index("x"), lax.axis_index("y"), lax.axis_index("c")
    return x, y, c, [(1 - x, y), (x, 1 - y), (1 - x, 1 - y)]


def _all_gather(name, shards):
    nw = len(shards)

    def body(*refs):
        ins, outs = refs[:nw], refs[nw:2 * nw]
        send_sems, recv_sems, local_sems = refs[2 * nw:]
        x, y, c, chips = _place()
        me, sibling = (x, y, c), (x, y, 1 - c)

        def slot(w, p):
            return outs[w].at[4 * p[0] + 2 * p[1] + p[2]]

        def copy(w, k, block, to, src=None):
            return pltpu.make_async_remote_copy(
                src_ref=slot(w, block) if src is None else src, dst_ref=slot(w, block),
                send_sem=send_sems.at[w, k], recv_sem=recv_sems.at[w, k], device_id=to, device_id_type=MESH)

        started = []
        for w in range(nw):
            mine = pltpu.make_async_copy(ins[w], slot(w, me), local_sems.at[w])
            mine.start()
            started.append(mine)
        sends = []
        for w in range(nw):
            first = [copy(w, 0, me, sibling, src=ins[w])]
            first += [copy(w, 1 + j, me, (*chip, c), src=ins[w]) for j, chip in enumerate(chips)]
            for cp in first:
                cp.start()
            sends += first
        for w in range(nw):
            for j, chip in enumerate(chips):
                copy(w, 1 + j, (*chip, c), me).wait_recv()
                passed = copy(w, 4 + j, (*chip, c), sibling)
                passed.start()
                sends.append(passed)
        for w in range(nw):
            copy(w, 0, sibling, me).wait_recv()
            for j, chip in enumerate(chips):
                copy(w, 4 + j, (*chip, 1 - c), me).wait_recv()
        for cp in sends:
            cp.wait_send()
        for mine in started:
            mine.wait()

    return pl.pallas_call(
        body, name=name,
        in_specs=[HBM_SPEC] * nw, out_specs=[HBM_SPEC] * nw,
        out_shape=[jax.ShapeDtypeStruct((N_DEV, *s.shape), s.dtype) for s in shards],
        scratch_shapes=[pltpu.SemaphoreType.DMA((nw, 7)), pltpu.SemaphoreType.DMA((nw, 7)), pltpu.SemaphoreType.DMA((nw,))],
    )(*shards)


def _pair_exchange(name, parts):
    nw = len(parts)

    def body(*refs):
        ins, outs = refs[:nw], refs[nw:2 * nw]
        send_sems, recv_sems = refs[2 * nw:]
        x, y, c, _ = _place()
        copies = []
        for w in range(nw):
            for q in range(N_CHIP):
                cp = pltpu.make_async_remote_copy(
                    src_ref=ins[w].at[2 * q + 1 - c], dst_ref=outs[w].at[q],
                    send_sem=send_sems.at[w, q], recv_sem=recv_sems.at[w, q], device_id=(x, y, 1 - c), device_id_type=MESH)
                cp.start()
                copies.append(cp)
        for cp in copies:
            cp.wait()

    return pl.pallas_call(
        body, name=name, in_specs=[HBM_SPEC] * nw, out_specs=[HBM_SPEC] * nw,
        out_shape=[jax.ShapeDtypeStruct((N_CHIP, *p.shape[1:]), p.dtype) for p in parts],
        scratch_shapes=[pltpu.SemaphoreType.DMA((nw, N_CHIP)), pltpu.SemaphoreType.DMA((nw, N_CHIP))],
    )(*parts)


def _pair_sum(name, part, got, core):
    _, k, n = part.shape
    tk = _tile(k, 256)

    def body(core_ref, p_ref, s_ref, o_ref):
        o_ref[...] = (p_ref[...] + s_ref[...]).astype(BF16)

    return pl.pallas_call(
        body, name=name,
        grid_spec=pltpu.PrefetchScalarGridSpec(
            num_scalar_prefetch=1, grid=(N_CHIP, k // tk),
            in_specs=[pl.BlockSpec((None, tk, n), lambda q, i, core_ref: (2 * q + core_ref[0], i, 0)),
                      pl.BlockSpec((None, tk, n), lambda q, i, core_ref: (q, i, 0))],
            out_specs=pl.BlockSpec((None, tk, n), lambda q, i, core_ref: (q, i, 0))),
        out_shape=jax.ShapeDtypeStruct((N_CHIP, k, n), BF16),
        compiler_params=_params("parallel", "parallel"),
    )(core, part, got)


def _chip_exchange(name, sums):
    nw = len(sums)

    def body(*refs):
        ins, outs = refs[:nw], refs[nw:2 * nw]
        send_sems, recv_sems, local_sems = refs[2 * nw:]
        x, y, c, chips = _place()
        my_chip = 2 * x + y
        copies, locals_ = [], []
        for w in range(nw):
            mine = pltpu.make_async_copy(ins[w].at[my_chip], outs[w].at[my_chip], local_sems.at[w])
            mine.start()
            locals_.append(mine)
            for j, chip in enumerate(chips):
                cp = pltpu.make_async_remote_copy(
                    src_ref=ins[w].at[2 * chip[0] + chip[1]], dst_ref=outs[w].at[my_chip],
                    send_sem=send_sems.at[w, j], recv_sem=recv_sems.at[w, j], device_id=(*chip, c), device_id_type=MESH)
                cp.start()
                copies.append(cp)
        for cp in copies:
            cp.wait()
        for mine in locals_:
            mine.wait()

    return pl.pallas_call(
        body, name=name, in_specs=[HBM_SPEC] * nw, out_specs=[HBM_SPEC] * nw,
        out_shape=[jax.ShapeDtypeStruct(s.shape, s.dtype) for s in sums],
        scratch_shapes=[pltpu.SemaphoreType.DMA((nw, 3)), pltpu.SemaphoreType.DMA((nw, 3)), pltpu.SemaphoreType.DMA((nw,))],
    )(*sums)


def _adamw_math(w, g, m, v):
    m = ADAM_B1 * m + (1.0 - ADAM_B1) * g
    v = ADAM_B2 * v + (1.0 - ADAM_B2) * (g * g)
    m_hat = m / (1.0 - ADAM_B1 ** ADAM_STEP)
    v_hat = v / (1.0 - ADAM_B2 ** ADAM_STEP)
    delta = -ADAM_LR * (m_hat / (jnp.sqrt(v_hat) + ADAM_EPS) + ADAM_WD * w)
    return delta, m, v


def _sum_adamw(name, parts, w, m, v):
    n_parts, k, n = parts.shape
    tk = _tile(k, 128)

    def body(p_ref, w_ref, m_ref, v_ref, g_ref, d_ref, nm_ref, nv_ref):
        g = p_ref[0].astype(F32)
        for q in range(1, n_parts):
            g = g + p_ref[q].astype(F32)
        g_ref[...] = g
        d_ref[...], nm_ref[...], nv_ref[...] = _adamw_math(w_ref[...], g, m_ref[...], v_ref[...])

    blk = pl.BlockSpec((tk, n), lambda i: (i, 0))
    return pl.pallas_call(
        body, name=name, grid=(k // tk,),
        in_specs=[pl.BlockSpec((n_parts, tk, n), lambda i: (0, i, 0)), blk, blk, blk], out_specs=[blk] * 4,
        out_shape=[jax.ShapeDtypeStruct((k, n), F32)] * 4,
        compiler_params=_params("parallel"),
    )(parts, w, m, v)


def _sum_parts(name, parts):
    n_parts, k, n = parts.shape

    def body(p_ref, g_ref):
        g = p_ref[0]
        for q in range(1, n_parts):
            g = g + p_ref[q]
        g_ref[...] = g

    return pl.pallas_call(body, name=name, out_shape=jax.ShapeDtypeStruct((k, n), F32))(parts)


def _adamw(name, w, g, m, v):
    def body(w_ref, g_ref, m_ref, v_ref, d_ref, nm_ref, nv_ref):
        d_ref[...], nm_ref[...], nv_ref[...] = _adamw_math(w_ref[...], g_ref[...], m_ref[...], v_ref[...])

    return pl.pallas_call(body, name=name, out_shape=[jax.ShapeDtypeStruct(w.shape, F32)] * 3)(w, g, m, v)


LANES = 128
PACK_ROWS = 8


def _pack(arrays):
    flat = []
    for a in arrays:
        a = a.reshape(-1)
        flat.append(jnp.pad(a, (0, (-a.shape[0]) % (PACK_ROWS * LANES))))
    return jnp.concatenate(flat).reshape(-1, LANES)


def _unpack(packed, shapes):
    out, row = [], 0
    for shape in shapes:
        size = math.prod(shape)
        rows = -(-size // (PACK_ROWS * LANES)) * PACK_ROWS
        out.append(packed[row:row + rows].reshape(-1)[:size].reshape(shape))
        row += rows
    return out


SMALL = ("ev_norm_g", "ev_conv_a_w", "ev_conv_a_b", "ev_ln_a_g", "ev_ln_a_b", "ev_conv_b_w", "od_norm_g", "od_b_in",
         "od_ln_v_g", "od_ln_v_b", "od_w_s", "od_b_s", "mlp_norm_g", "final_norm_g")
SMALL_SHARDED = ("ev_conv_a_w", "ev_conv_b_w", "od_norm_g", "od_b_in", "od_ln_v_g", "od_ln_v_b")
LARGE = ("ev_w_in", "ev_w_out", "od_w_in", "od_w_out", "mlp_w1", "mlp_w2")
ORDER = ("ev_norm_g", "ev_w_in", "ev_conv_a_w", "ev_conv_a_b", "ev_ln_a_g", "ev_ln_a_b", "ev_conv_b_w", "ev_w_out",
         "od_norm_g", "od_w_in", "od_b_in", "od_ln_v_g", "od_ln_v_b", "od_w_s", "od_b_s", "od_w_out", "mlp_norm_g",
         "mlp_w1", "mlp_w2", "final_norm_g")


def _gather_small_shards(name, shards):
    packed = _pack(shards)
    every = _all_gather(name, [packed])[0]
    per_dev = [_unpack(every[d], [s.shape for s in shards]) for d in range(N_DEV)]
    return [jnp.concatenate([per_dev[d][i] for d in range(N_DEV)], axis=-1) for i in range(len(shards))]


def kernel(x, ev_norm_g, ev_w_in, ev_conv_a_w, ev_conv_a_b, ev_ln_a_g, ev_ln_a_b, ev_conv_b_w, ev_w_out, od_norm_g, od_w_in, od_b_in, od_ln_v_g, od_ln_v_b, od_w_s, od_b_s, od_w_out, mlp_norm_g, mlp_w1, mlp_w2, final_norm_g, loss_target, m_ev_norm_g, m_ev_w_in, m_ev_conv_a_w, m_ev_conv_a_b, m_ev_ln_a_g, m_ev_ln_a_b, m_ev_conv_b_w, m_ev_w_out, m_od_norm_g, m_od_w_in, m_od_b_in, m_od_ln_v_g, m_od_ln_v_b, m_od_w_s, m_od_b_s, m_od_w_out, m_mlp_norm_g, m_mlp_w1, m_mlp_w2, m_final_norm_g, v_ev_norm_g, v_ev_w_in, v_ev_conv_a_w, v_ev_conv_a_b, v_ev_ln_a_g, v_ev_ln_a_b, v_ev_conv_b_w, v_ev_w_out, v_od_norm_g, v_od_w_in, v_od_b_in, v_od_ln_v_g, v_od_ln_v_b, v_od_w_s, v_od_b_s, v_od_w_out, v_mlp_norm_g, v_mlp_w1, v_mlp_w2, v_final_norm_g):
    W = dict(ev_norm_g=ev_norm_g, ev_w_in=ev_w_in, ev_conv_a_w=ev_conv_a_w, ev_conv_a_b=ev_conv_a_b, ev_ln_a_g=ev_ln_a_g,
             ev_ln_a_b=ev_ln_a_b, ev_conv_b_w=ev_conv_b_w, ev_w_out=ev_w_out, od_norm_g=od_norm_g, od_w_in=od_w_in,
             od_b_in=od_b_in, od_ln_v_g=od_ln_v_g, od_ln_v_b=od_ln_v_b, od_w_s=od_w_s, od_b_s=od_b_s, od_w_out=od_w_out,
             mlp_norm_g=mlp_norm_g, mlp_w1=mlp_w1, mlp_w2=mlp_w2, final_norm_g=final_norm_g)
    M = dict(ev_norm_g=m_ev_norm_g, ev_w_in=m_ev_w_in, ev_conv_a_w=m_ev_conv_a_w, ev_conv_a_b=m_ev_conv_a_b,
             ev_ln_a_g=m_ev_ln_a_g, ev_ln_a_b=m_ev_ln_a_b, ev_conv_b_w=m_ev_conv_b_w, ev_w_out=m_ev_w_out,
             od_norm_g=m_od_norm_g, od_w_in=m_od_w_in, od_b_in=m_od_b_in, od_ln_v_g=m_od_ln_v_g, od_ln_v_b=m_od_ln_v_b,
             od_w_s=m_od_w_s, od_b_s=m_od_b_s, od_w_out=m_od_w_out, mlp_norm_g=m_mlp_norm_g, mlp_w1=m_mlp_w1,
             mlp_w2=m_mlp_w2, final_norm_g=m_final_norm_g)
    V = dict(ev_norm_g=v_ev_norm_g, ev_w_in=v_ev_w_in, ev_conv_a_w=v_ev_conv_a_w, ev_conv_a_b=v_ev_conv_a_b,
             ev_ln_a_g=v_ev_ln_a_g, ev_ln_a_b=v_ev_ln_a_b, ev_conv_b_w=v_ev_conv_b_w, ev_w_out=v_ev_w_out,
             od_norm_g=v_od_norm_g, od_w_in=v_od_w_in, od_b_in=v_od_b_in, od_ln_v_g=v_od_ln_v_g, od_ln_v_b=v_od_ln_v_b,
             od_w_s=v_od_w_s, od_b_s=v_od_b_s, od_w_out=v_od_w_out, mlp_norm_g=v_mlp_norm_g, mlp_w1=v_mlp_w1,
             mlp_w2=v_mlp_w2, final_norm_g=v_final_norm_g)

    n_seq, seq, d = x.shape
    t = n_seq * seq
    dev = 4 * lax.axis_index("x") + 2 * lax.axis_index("y") + lax.axis_index("c")
    core = lax.axis_index("c").astype(jnp.int32).reshape(1)

    g_ev_in, g_ev_out, g_od_in, g_od_out, g_w1, g_w2 = _all_gather(
        "gather_weights", [W[n].astype(BF16) for n in LARGE])
    w_ev_in = jnp.transpose(g_ev_in[:, 0], (1, 0, 2)).reshape(D_MODEL, IN_EVEN)
    w_ev_out = g_ev_out.reshape(D_MODEL, D_MODEL)
    w_od_in = g_od_in[:, 0]
    w_od_out = g_od_out.reshape(D_MODEL, D_MODEL)
    w_w2 = [g_w2[:, l].reshape(D_FF, D_MODEL) for l in range(2)]
    caw, cbw, od_g, od_bin, lvg, lvb = _gather_small_shards(
        "gather_small_weights", [W[n][0] for n in SMALL_SHARDED[:2]] + [W[n] for n in SMALL_SHARDED[2:]])
    ev_g, cab, lag, lab = W["ev_norm_g"], W["ev_conv_a_b"], W["ev_ln_a_g"], W["ev_ln_a_b"]
    ws = W["od_w_s"][0]
    bs_b = jnp.broadcast_to(W["od_b_s"][0][:, :, None], (C_GROUPS, CHUNK, CHUNK))
    mlp_g = [W["mlp_norm_g"][l:l + 1] for l in range(2)]
    fin_g = W["final_norm_g"].reshape(1, d)

    def mlp_fwd(l, h):
        n = _rms_fwd(f"mlp{l}_norm", h, mlp_g[l])
        q = _mm_nn(f"mlp{l}_up", n, g_w1, D_FF, _ep_relu_sq, [BF16], lead=l)[0]
        h_out = _mm_nn(f"mlp{l}_down", q, w_w2[l], d, _ep_residual, [F32], extras=(h,))[0]
        return h_out, (n, q)

    def mlp_bwd(l, h, saved, grad, grad16):
        n, q = saved
        dw2 = _mm_tn(f"mlp{l}_dw2", q, grad16)
        dp = _mm_nt(f"mlp{l}_dq", grad16, w_w2[l], D_FF, _ep_relu_sq_grad, [BF16], extras=(q,))[0]
        dw1 = _mm_tn(f"mlp{l}_dw1", n, dp, col_blocks=N_DEV)
        dn = _mm_nt(f"mlp{l}_dn", dp, g_w1, d, _ep_store, [F32], lead=l)[0]
        g_out, g_out16, dg = _rms_bwd(f"mlp{l}_norm_bwd", dn, h, mlp_g[l], grad)
        return g_out, g_out16, dw1, dw2, dg

    h0 = x.reshape(t, d)
    n0 = _rms_fwd("ev_norm", h0, ev_g)
    z = _mm_nn("ev_in", n0, w_ev_in, IN_EVEN, _ep_store, [F32])[0]
    mix = _mixer_fwd(z, seq, caw, cab, lag, lab, cbw)
    h1 = _mm_nn("ev_out", mix, w_ev_out, d, _ep_residual, [F32], extras=(h0,))[0]
    h2, mlp0_saved = mlp_fwd(0, h1)
    n2 = _rms_fwd("od_norm", h2, od_g)

    pre = _mm_nn("od_in", n2, w_od_in, 2 * C_DIM, _ep_bias, [F32], bias=od_bin)[0]
    y = _sgu_fwd(pre, lvg, lvb, ws, bs_b)
    h3 = _mm_nn("od_out", y, w_od_out, d, _ep_residual, [F32], extras=(h2,))[0]
    h4, mlp1_saved = mlp_fwd(1, h3)
    grad, grad16, d_fin_g, loss_part = _final_loss("final_loss", h4, fin_g, loss_target.reshape(t, d))
    loss = lax.psum(loss_part[0, 0], ("x", "y", "c"))

    grad, grad16, dw1_1, dw2_1, dg_mlp1 = mlp_bwd(1, h3, mlp1_saved, grad, grad16)
    d_od_out = _mm_tn("od_dw_out", y, grad16)
    dy = _mm_nt("od_dy", grad16, w_od_out, C_DIM, _ep_store, [F32])[0]
    dpre, d_ws, d_bsb, d_lvg, d_lvb, d_bin = _sgu_bwd(pre, dy, lvg, lvb, ws, bs_b)
    d_od_in = _mm_tn("od_dw_in", n2, dpre, col_blocks=N_DEV)
    dn2 = _mm_nt("od_dn", dpre, w_od_in, d, _ep_store, [F32])[0]
    grad, grad16, d_od_g = _rms_bwd("od_norm_bwd", dn2, h2, od_g, grad)
    grad, grad16, dw1_0, dw2_0, dg_mlp0 = mlp_bwd(0, h1, mlp0_saved, grad, grad16)
    d_ev_out = _mm_tn("ev_dw_out", mix, grad16)
    dmix = _mm_nt("ev_dmix", grad16, w_ev_out, A_DIM + B_DIM, _ep_store, [F32])[0]
    da2, dcv, d_caw, d_cab, d_lag, d_lab, d_cbw = _mixer_bwd_local(z, dmix, seq, caw, cab, lag, lab, cbw)
    dz = _mixer_bwd_input(z, dmix, da2, dcv, seq, caw, cbw)
    d_ev_in = _mm_tn("ev_dw_in", n0, dz)
    dn0 = _mm_nt("ev_dn", dz, w_ev_in, d, _ep_store, [F32])[0]
    grad_x, _, d_ev_g = _rms_bwd("ev_norm_bwd", dn0, h0, ev_g, grad)

    parts = [
        jnp.transpose(d_ev_in.reshape(D_MODEL, N_DEV, IN_EVEN // N_DEV), (1, 0, 2)),
        d_ev_out.reshape(N_DEV, D_MODEL // N_DEV, D_MODEL),
        d_od_in,
        d_od_out.reshape(N_DEV, D_MODEL // N_DEV, D_MODEL),
        dw1_0, dw1_1,
        dw2_0.reshape(N_DEV, D_FF // N_DEV, D_MODEL), dw2_1.reshape(N_DEV, D_FF // N_DEV, D_MODEL),
    ]
    got = _pair_exchange("grads_pair_exchange", parts)
    sums = [_pair_sum(f"grads_pair_sum{i}", p, s, core) for i, (p, s) in enumerate(zip(parts, got))]
    by_chip = _chip_exchange("grads_chip_exchange", sums)
    shard = {"ev_w_in": (0, None), "ev_w_out": (1, None), "od_w_in": (2, None), "od_w_out": (3, None),
             "mlp_w1": (4, 5), "mlp_w2": (6, 7)}
    out_g, out_d, out_m, out_v = {}, {}, {}, {}
    for name, (i0, i1) in shard.items():
        res = []
        for l, i in enumerate((i0,) if i1 is None else (i0, i1)):
            res.append(_sum_adamw(f"adamw_{name}_{l}", by_chip[i], W[name][l], M[name][l], V[name][l]))
        for store, j in ((out_g, 0), (out_d, 1), (out_m, 2), (out_v, 3)):
            store[name] = jnp.stack([r[j] for r in res])

    small_parts = dict(
        ev_norm_g=d_ev_g, ev_conv_a_w=d_caw[None], ev_conv_a_b=d_cab, ev_ln_a_g=d_lag, ev_ln_a_b=d_lab,
        ev_conv_b_w=d_cbw[None], od_norm_g=d_od_g, od_b_in=d_bin, od_ln_v_g=d_lvg, od_ln_v_b=d_lvb, od_w_s=d_ws[None],
        od_b_s=d_bsb[:, :, 0][None], mlp_norm_g=jnp.concatenate([dg_mlp0, dg_mlp1], axis=0), final_norm_g=d_fin_g.reshape(d))
    full_shapes = [small_parts[n].shape for n in SMALL]
    packed = _pack([small_parts[n] for n in SMALL])
    every = _all_gather("gather_small_grads", [packed])[0]
    total = _unpack(_sum_parts("sum_small_grads", every), full_shapes)
    small_g = {}
    for n, g in zip(SMALL, total):
        if n in SMALL_SHARDED:
            width = W[n].shape[-1]
            g = lax.dynamic_slice_in_dim(g, dev * width, width, axis=g.ndim - 1)
        small_g[n] = g
    own_shapes = [W[n].shape for n in SMALL]
    packed_d, packed_m, packed_v = _adamw(
        "adamw_small", _pack([W[n] for n in SMALL]), _pack([small_g[n] for n in SMALL]),
        _pack([M[n] for n in SMALL]), _pack([V[n] for n in SMALL]))
    for n, dd, mm, vv in zip(SMALL, _unpack(packed_d, own_shapes), _unpack(packed_m, own_shapes), _unpack(packed_v, own_shapes)):
        out_g[n], out_d[n], out_m[n], out_v[n] = small_g[n], dd, mm, vv

    return (loss, grad_x.reshape(n_seq, seq, d), *[out_g[n] for n in ORDER], *[out_d[n] for n in ORDER],
            *[out_m[n] for n in ORDER], *[out_v[n] for n in ORDER])
```

```python
import math

import jax
import jax.numpy as jnp
from jax import lax
from jax.experimental import pallas as pl
from jax.experimental.pallas import tpu as pltpu

F32 = jnp.float32
BF16 = jnp.bfloat16
MESH = pl.DeviceIdType.MESH

D_MODEL = 1024
A_DIM = 512
B_DIM = 512
IN_EVEN = 2 * A_DIM + 3 * B_DIM
A_TAPS = 31
B_TAPS = 3
CHUNK = 128
C_GROUPS = 8
C_DIM = 1024
D_FF = 4096
RMS_EPS = 1e-6
LN_EPS = 1e-5
N_DEV = 8
N_CHIP = 4

ADAM_LR = 0.001
ADAM_B1 = 0.9
ADAM_B2 = 0.999
ADAM_EPS = 1e-08
ADAM_WD = 0.01
ADAM_STEP = 10

A_HALO = 32
B_HALO = 8
VMEM_LIMIT_BYTES = 56 * 1024 * 1024
INV_SQRT2 = 1.0 / math.sqrt(2.0)
INV_SQRT_2PI = 1.0 / math.sqrt(2.0 * math.pi)
HBM_SPEC = pl.BlockSpec(memory_space=pltpu.HBM)


def _tile(n, want):
    t = min(n, want)
    while n % t:
        t //= 2
    return t


def _sigmoid(x):
    return 1.0 / (1.0 + jnp.exp(-x))


def _gelu(x):
    return 0.5 * x * (1.0 + lax.erf(x * INV_SQRT2))


def _gelu_grad(x):
    return 0.5 * (1.0 + lax.erf(x * INV_SQRT2)) + x * jnp.exp(-0.5 * x * x) * INV_SQRT_2PI


def _colsum(x):
    return jnp.sum(x, axis=0, keepdims=True)


class _Comm:
    def __init__(self, ins, out_shapes, sem_shapes, start, finish):
        self.ins, self.out_shapes, self.sem_shapes, self.start, self.finish = ins, out_shapes, sem_shapes, start, finish
        self.out = None


def _call(body, *, name, grid, in_specs, out_specs, out_shape, args, scratch_shapes=(), parallel=False, comm=None):
    if comm is None:
        sem = ("parallel" if parallel else "arbitrary",) * len(grid)
        return pl.pallas_call(
            body, name=name, grid=grid, in_specs=list(in_specs), out_specs=list(out_specs), out_shape=list(out_shape),
            scratch_shapes=list(scratch_shapes),
            compiler_params=pltpu.CompilerParams(dimension_semantics=sem, vmem_limit_bytes=VMEM_LIMIT_BYTES),
        )(*args)
    n_in, n_out, n_scr = len(in_specs), len(out_shape), len(scratch_shapes)
    c_in, c_out = len(comm.ins), len(comm.out_shapes)
    steps = grid

    def carrying(*refs):
        pos = 0
        ins = refs[pos:pos + n_in]; pos += n_in
        c_ins = refs[pos:pos + c_in]; pos += c_in
        outs = refs[pos:pos + n_out]; pos += n_out
        c_outs = refs[pos:pos + c_out]; pos += c_out
        scr = refs[pos:pos + n_scr]; pos += n_scr
        c_sems = refs[pos:]
        first = pl.program_id(0) == 0
        last = pl.program_id(0) == steps[0] - 1
        for axis in range(1, len(steps)):
            first = jnp.logical_and(first, pl.program_id(axis) == 0)
            last = jnp.logical_and(last, pl.program_id(axis) == steps[axis] - 1)

        @pl.when(first)
        def _():
            comm.start(c_ins, c_outs, c_sems)

        body(*ins, *outs, *scr)

        @pl.when(last)
        def _():
            comm.finish(c_ins, c_outs, c_sems)

    res = pl.pallas_call(
        carrying, name=name, grid=grid,
        in_specs=[*in_specs, *[HBM_SPEC] * c_in], out_specs=[*out_specs, *[HBM_SPEC] * c_out],
        out_shape=[*out_shape, *comm.out_shapes], scratch_shapes=[*scratch_shapes, *comm.sem_shapes],
        compiler_params=pltpu.CompilerParams(dimension_semantics=("arbitrary",) * len(grid), vmem_limit_bytes=VMEM_LIMIT_BYTES),
    )(*args, *comm.ins)
    comm.out = list(res[n_out:])
    return list(res[:n_out])


def _comm_alone(name, comm):
    c_in, c_out = len(comm.ins), len(comm.out_shapes)

    def body(*refs):
        ins, outs, sems = refs[:c_in], refs[c_in:c_in + c_out], refs[c_in + c_out:]
        comm.start(ins, outs, sems)
        comm.finish(ins, outs, sems)

    comm.out = list(pl.pallas_call(
        body, name=name, in_specs=[HBM_SPEC] * c_in, out_specs=[HBM_SPEC] * c_out, out_shape=list(comm.out_shapes),
        scratch_shapes=list(comm.sem_shapes))(*comm.ins))
    return comm.out


def _place():
    x, y, c = lax.axis_index("x"), lax.axis_index("y"), lax.axis_index("c")
    return x, y, c, [(1 - x, y), (x, 1 - y), (1 - x, 1 - y)]


def _gather_comm(shards):
    nw = len(shards)

    def plan(ins, outs, sems):
        send_sems, recv_sems, local_sems = sems
        x, y, c, chips = _place()
        me, sibling = (x, y, c), (x, y, 1 - c)

        def slot(w, p):
            return outs[w].at[4 * p[0] + 2 * p[1] + p[2]]

        def copy(w, k, block, to, src=None):
            return pltpu.make_async_remote_copy(
                src_ref=slot(w, block) if src is None else src, dst_ref=slot(w, block),
                send_sem=send_sems.at[w, k], recv_sem=recv_sems.at[w, k], device_id=to, device_id_type=MESH)

        local = [pltpu.make_async_copy(ins[w], slot(w, me), local_sems.at[w]) for w in range(nw)]
        first = [[copy(w, 0, me, sibling, src=ins[w])] + [copy(w, 1 + j, me, (*chip, c), src=ins[w]) for j, chip in enumerate(chips)]
                 for w in range(nw)]
        landed = [[copy(w, 1 + j, (*chip, c), me) for j, chip in enumerate(chips)] for w in range(nw)]
        passed = [[copy(w, 4 + j, (*chip, c), sibling) for j, chip in enumerate(chips)] for w in range(nw)]
        from_sibling = [[copy(w, 0, sibling, me)] + [copy(w, 4 + j, (*chip, 1 - c), me) for j, chip in enumerate(chips)]
                        for w in range(nw)]
        return local, first, landed, passed, from_sibling

    def start(ins, outs, sems):
        local, first, _, _, _ = plan(ins, outs, sems)
        for cp in local:
            cp.start()
        for row in first:
            for cp in row:
                cp.start()

    def finish(ins, outs, sems):
        local, first, landed, passed, from_sibling = plan(ins, outs, sems)
        for w in range(nw):
            for j in range(3):
                landed[w][j].wait_recv()
                passed[w][j].start()
        for w in range(nw):
            for cp in from_sibling[w]:
                cp.wait_recv()
        for w in range(nw):
            for cp in first[w] + passed[w]:
                cp.wait_send()
        for cp in local:
            cp.wait()

    return _Comm(list(shards), [jax.ShapeDtypeStruct((N_DEV, *s.shape), s.dtype) for s in shards],
                 [pltpu.SemaphoreType.DMA((nw, 7)), pltpu.SemaphoreType.DMA((nw, 7)), pltpu.SemaphoreType.DMA((nw,))],
                 start, finish)


def _pair_comm(parts):
    nw = len(parts)

    def plan(ins, outs, sems):
        send_sems, recv_sems = sems
        x, y, c, _ = _place()
        return [pltpu.make_async_remote_copy(
            src_ref=ins[w].at[2 * q + 1 - c], dst_ref=outs[w].at[q], send_sem=send_sems.at[w, q], recv_sem=recv_sems.at[w, q],
            device_id=(x, y, 1 - c), device_id_type=MESH) for w in range(nw) for q in range(N_CHIP)]

    def start(ins, outs, sems):
        for cp in plan(ins, outs, sems):
            cp.start()

    def finish(ins, outs, sems):
        for cp in plan(ins, outs, sems):
            cp.wait()

    return _Comm(list(parts), [jax.ShapeDtypeStruct((N_CHIP, *p.shape[1:]), p.dtype) for p in parts],
                 [pltpu.SemaphoreType.DMA((nw, N_CHIP)), pltpu.SemaphoreType.DMA((nw, N_CHIP))], start, finish)


def _chip_comm(sums):
    nw = len(sums)

    def plan(ins, outs, sems):
        send_sems, recv_sems, local_sems = sems
        x, y, c, chips = _place()
        my_chip = 2 * x + y
        local = [pltpu.make_async_copy(ins[w].at[my_chip], outs[w].at[my_chip], local_sems.at[w]) for w in range(nw)]
        remote = [pltpu.make_async_remote_copy(
            src_ref=ins[w].at[2 * chip[0] + chip[1]], dst_ref=outs[w].at[my_chip],
            send_sem=send_sems.at[w, j], recv_sem=recv_sems.at[w, j], device_id=(*chip, c), device_id_type=MESH)
            for w in range(nw) for j, chip in enumerate(chips)]
        return local, remote

    def start(ins, outs, sems):
        local, remote = plan(ins, outs, sems)
        for cp in local + remote:
            cp.start()

    def finish(ins, outs, sems):
        local, remote = plan(ins, outs, sems)
        for cp in remote + local:
            cp.wait()

    return _Comm(list(sums), [jax.ShapeDtypeStruct(s.shape, s.dtype) for s in sums],
                 [pltpu.SemaphoreType.DMA((nw, 3)), pltpu.SemaphoreType.DMA((nw, 3)), pltpu.SemaphoreType.DMA((nw,))],
                 start, finish)


def _matmul(name, a, b, *, kind, m, n, k, a_spec, b_spec, tm, tn, tk, out_shape, out_specs, epilogue,
            extras=(), extra_specs=(), comm=None):
    dims = {"nn": (((1,), (0,)), ((), ())), "nt": (((1,), (1,)), ((), ())), "tn": (((0,), (0,)), ((), ()))}[kind]
    nk = k // tk
    n_extra = len(extras)
    n_out = len(out_shape)

    def body(a_ref, b_ref, *rest):
        extra_refs = rest[:n_extra]
        out_refs = rest[n_extra:n_extra + n_out]
        part = lax.dot_general(a_ref[...], b_ref[...], dims, preferred_element_type=F32)
        if nk == 1:
            epilogue(part, extra_refs, out_refs)
            return
        acc_ref = rest[n_extra + n_out]
        step = pl.program_id(2)

        @pl.when(step == 0)
        def _():
            acc_ref[...] = part

        @pl.when(jnp.logical_and(step > 0, step < nk - 1))
        def _():
            acc_ref[...] += part

        @pl.when(step == nk - 1)
        def _():
            epilogue(acc_ref[...] + part, extra_refs, out_refs)

    return _call(
        body, name=name, grid=(m // tm, n // tn, nk), in_specs=[a_spec, b_spec, *extra_specs], out_specs=out_specs,
        out_shape=out_shape, scratch_shapes=[pltpu.VMEM((tm, tn), F32)] if nk > 1 else [], args=(a, b, *extras), comm=comm)


def _mm_nn(name, a, b, n, epilogue, out_dtypes, *, extras=(), bias=None, comm=None):
    m, k = a.shape
    blocked = b.ndim == 3
    tm = _tile(m, 1024)
    tn = b.shape[-1] if blocked else _tile(n, 1024)
    tk = _tile(k, 1024)
    tile = pl.BlockSpec((tm, tn), lambda i, j, kk: (i, j))
    extra_specs = [tile] * len(extras)
    if bias is not None:
        extras = (*extras, bias)
        extra_specs.append(pl.BlockSpec((1, tn), lambda i, j, kk: (0, j)))
    if blocked:
        b_spec = pl.BlockSpec((None, tk, tn), lambda i, j, kk: (j, kk, 0))
    else:
        b_spec = pl.BlockSpec((tk, tn), lambda i, j, kk: (kk, j))
    return _matmul(
        name, a, b, kind="nn", m=m, n=n, k=k, tm=tm, tn=tn, tk=tk,
        a_spec=pl.BlockSpec((tm, tk), lambda i, j, kk: (i, kk)), b_spec=b_spec,
        out_shape=[jax.ShapeDtypeStruct((m, n), dt) for dt in out_dtypes], out_specs=[tile] * len(out_dtypes),
        epilogue=epilogue, extras=extras, extra_specs=extra_specs, comm=comm)


def _mm_nt(name, a, b, n, epilogue, out_dtypes, *, extras=(), comm=None):
    m, k = a.shape
    blocked = b.ndim == 3
    tm = _tile(m, 1024)
    tn = _tile(n, 1024)
    tk = b.shape[-1] if blocked else _tile(k, 1024)
    tile = pl.BlockSpec((tm, tn), lambda i, j, kk: (i, j))
    if blocked:
        b_spec = pl.BlockSpec((None, tn, tk), lambda i, j, kk: (kk, j, 0))
    else:
        b_spec = pl.BlockSpec((tn, tk), lambda i, j, kk: (j, kk))
    return _matmul(
        name, a, b, kind="nt", m=m, n=n, k=k, tm=tm, tn=tn, tk=tk,
        a_spec=pl.BlockSpec((tm, tk), lambda i, j, kk: (i, kk)), b_spec=b_spec,
        out_shape=[jax.ShapeDtypeStruct((m, n), dt) for dt in out_dtypes], out_specs=[tile] * len(out_dtypes),
        epilogue=epilogue, extras=extras, extra_specs=[tile] * len(extras), comm=comm)


def _mm_tn(name, a, b, *, col_blocks=0, comm=None):
    t, k = a.shape
    n = b.shape[1]
    tm = _tile(k, 1024)
    tn = n // col_blocks if col_blocks else _tile(n, 1024)
    tk = _tile(t, 1024)
    if col_blocks:
        out_shape = [jax.ShapeDtypeStruct((col_blocks, k, tn), F32)]
        out_specs = [pl.BlockSpec((None, tm, tn), lambda i, j, kk: (j, i, 0))]
    else:
        out_shape = [jax.ShapeDtypeStruct((k, n), F32)]
        out_specs = [pl.BlockSpec((tm, tn), lambda i, j, kk: (i, j))]

    def epilogue(acc, extra_refs, out_refs):
        out_refs[0][...] = acc

    return _matmul(
        name, a, b, kind="tn", m=k, n=n, k=t, tm=tm, tn=tn, tk=tk,
        a_spec=pl.BlockSpec((tk, tm), lambda i, j, kk: (kk, i)), b_spec=pl.BlockSpec((tk, tn), lambda i, j, kk: (kk, j)),
        out_shape=out_shape, out_specs=out_specs, epilogue=epilogue, comm=comm)[0]


def _ep_store(acc, extra_refs, out_refs):
    out_refs[0][...] = acc.astype(out_refs[0].dtype)


def _ep_residual(acc, extra_refs, out_refs):
    out_refs[0][...] = extra_refs[0][...] + acc


def _ep_bias(acc, extra_refs, out_refs):
    out_refs[0][...] = acc + extra_refs[0][...]


def _ep_relu_sq(acc, extra_refs, out_refs):
    r = jnp.maximum(acc, 0.0)
    out_refs[0][...] = (r * r).astype(BF16)


def _ep_relu_sq_grad(acc, extra_refs, out_refs):
    out_refs[0][...] = (acc * (2.0 * jnp.sqrt(extra_refs[0][...].astype(F32)))).astype(BF16)


def _rms_fwd(name, h, g, comm=None):
    t, d = h.shape
    tt = _tile(t, 512)

    def body(h_ref, g_ref, n_ref):
        x = h_ref[...]
        r = lax.rsqrt(jnp.mean(x * x, axis=-1, keepdims=True) + RMS_EPS)
        n_ref[...] = (x * r * g_ref[...]).astype(BF16)

    return _call(
        body, name=name, grid=(t // tt,),
        in_specs=[pl.BlockSpec((tt, d), lambda i: (i, 0)), pl.BlockSpec((1, d), lambda i: (0, 0))],
        out_specs=[pl.BlockSpec((tt, d), lambda i: (i, 0))], out_shape=[jax.ShapeDtypeStruct((t, d), BF16)],
        args=(h, g), parallel=True, comm=comm)[0]


def _rms_bwd(name, dn, h, g, grad_in, comm=None):
    t, d = h.shape
    tt = _tile(t, 512)

    def body(dn_ref, h_ref, g_ref, gin_ref, gout_ref, gout16_ref, dg_ref):
        @pl.when(pl.program_id(0) == 0)
        def _():
            dg_ref[...] = jnp.zeros_like(dg_ref)

        x = h_ref[...]
        dnv = dn_ref[...]
        r = lax.rsqrt(jnp.mean(x * x, axis=-1, keepdims=True) + RMS_EPS)
        xr = x * r
        dg_ref[...] += _colsum(dnv * xr)
        dy = dnv * g_ref[...]
        out = gin_ref[...] + r * (dy - xr * jnp.mean(dy * xr, axis=-1, keepdims=True))
        gout_ref[...] = out
        gout16_ref[...] = out.astype(BF16)

    row = pl.BlockSpec((tt, d), lambda i: (i, 0))
    vec = pl.BlockSpec((1, d), lambda i: (0, 0))
    return _call(
        body, name=name, grid=(t // tt,), in_specs=[row, row, vec, row], out_specs=[row, row, vec],
        out_shape=[jax.ShapeDtypeStruct((t, d), F32), jax.ShapeDtypeStruct((t, d), BF16), jax.ShapeDtypeStruct((1, d), F32)],
        args=(dn, h, g, grad_in), comm=comm)


def _final_loss(name, h, g, target):
    t, d = h.shape
    tt = _tile(t, 512)

    def body(h_ref, g_ref, tgt_ref, gout_ref, gout16_ref, dg_ref, loss_ref):
        @pl.when(pl.program_id(0) == 0)
        def _():
            dg_ref[...] = jnp.zeros_like(dg_ref)
            loss_ref[...] = jnp.zeros_like(loss_ref)

        x = h_ref[...]
        r = lax.rsqrt(jnp.mean(x * x, axis=-1, keepdims=True) + RMS_EPS)
        xr = x * r
        err = xr * g_ref[...] - tgt_ref[...]
        loss_ref[...] += 0.5 * jnp.sum(jnp.mean(err * err, axis=-1, keepdims=True), axis=0, keepdims=True)
        dout = err * (1.0 / d)
        dg_ref[...] += _colsum(dout * xr)
        dy = dout * g_ref[...]
        out = r * (dy - xr * jnp.mean(dy * xr, axis=-1, keepdims=True))
        gout_ref[...] = out
        gout16_ref[...] = out.astype(BF16)

    row = pl.BlockSpec((tt, d), lambda i: (i, 0))
    vec = pl.BlockSpec((1, d), lambda i: (0, 0))
    one = pl.BlockSpec((1, 1), lambda i: (0, 0))
    return _call(
        body, name=name, grid=(t // tt,), in_specs=[row, vec, row], out_specs=[row, row, vec, one],
        out_shape=[jax.ShapeDtypeStruct((t, d), F32), jax.ShapeDtypeStruct((t, d), BF16),
                   jax.ShapeDtypeStruct((1, d), F32), jax.ShapeDtypeStruct((1, 1), F32)],
        args=(h, g, target))


def _mixer_windows(z_ref, zh_ref, first, a1_s, cb_s, tt):
    sig = _sigmoid(z_ref[:, A_DIM:2 * A_DIM])
    a1_s[A_HALO:A_HALO + tt, :] = z_ref[:, 0:A_DIM] * sig
    a1_h = zh_ref[:, 0:A_DIM] * _sigmoid(zh_ref[:, A_DIM:2 * A_DIM])
    a1_s[0:A_HALO, :] = jnp.where(first, 0.0, a1_h)
    cb_s[B_HALO:B_HALO + tt, :] = z_ref[:, 3 * A_DIM:4 * A_DIM] * z_ref[:, 4 * A_DIM:5 * A_DIM]
    cb_h = zh_ref[A_HALO - B_HALO:A_HALO, 3 * A_DIM:4 * A_DIM] * zh_ref[A_HALO - B_HALO:A_HALO, 4 * A_DIM:5 * A_DIM]
    cb_s[0:B_HALO, :] = jnp.where(first, 0.0, cb_h)
    return sig


def _causal_conv(win_s, w_ref, taps, halo, tt):
    base = halo - (taps - 1)
    acc = w_ref[0:1, :] * win_s[pl.ds(base, tt), :]
    for k in range(1, taps):
        acc = acc + w_ref[k:k + 1, :] * win_s[pl.ds(base + k, tt), :]
    return acc


def _layer_norm_stats(x):
    mu = jnp.mean(x, axis=-1, keepdims=True)
    xc = x - mu
    rstd = lax.rsqrt(jnp.mean(xc * xc, axis=-1, keepdims=True) + LN_EPS)
    return xc * rstd, rstd


def _mixer_specs(seq, tt):
    tiles_per_seq = seq // tt
    halo_blocks = tt // A_HALO
    z_spec = pl.BlockSpec((tt, IN_EVEN), lambda i: (i, 0))
    zh_spec = pl.BlockSpec((A_HALO, IN_EVEN), lambda i: (jnp.maximum(i * halo_blocks - 1, 0), 0))
    return tiles_per_seq, z_spec, zh_spec


def _vec_spec(rows, cols):
    return pl.BlockSpec((rows, cols), lambda i: (0, 0))


def _mixer_fwd(z, seq, caw, cab, lag, lab, cbw, comm=None):
    t = z.shape[0]
    tt = _tile(seq, 256)
    tiles_per_seq, z_spec, zh_spec = _mixer_specs(seq, tt)

    def body(z_ref, zh_ref, caw_ref, cab_ref, lag_ref, lab_ref, cbw_ref, mix_ref, a1_s, cb_s):
        first = (pl.program_id(0) % tiles_per_seq) == 0
        _mixer_windows(z_ref, zh_ref, first, a1_s, cb_s, tt)
        a2 = _causal_conv(a1_s, caw_ref, A_TAPS, A_HALO, tt) + cab_ref[...]
        xhat, _ = _layer_norm_stats(a2)
        a3 = xhat * lag_ref[...] + lab_ref[...]
        mix_ref[:, 0:A_DIM] = (a3 * _sigmoid(a3)).astype(BF16)
        cv = _causal_conv(cb_s, cbw_ref, B_TAPS, B_HALO, tt)
        mix_ref[:, A_DIM:A_DIM + B_DIM] = (z_ref[:, 2 * A_DIM:3 * A_DIM] * cv).astype(BF16)

    return _call(
        body, name="mixer_fwd", grid=(t // tt,),
        in_specs=[z_spec, zh_spec, _vec_spec(A_TAPS, A_DIM), _vec_spec(1, A_DIM), _vec_spec(1, A_DIM), _vec_spec(1, A_DIM),
                  _vec_spec(B_TAPS, B_DIM)],
        out_specs=[pl.BlockSpec((tt, A_DIM + B_DIM), lambda i: (i, 0))],
        out_shape=[jax.ShapeDtypeStruct((t, A_DIM + B_DIM), BF16)],
        scratch_shapes=[pltpu.VMEM((A_HALO + tt, A_DIM), F32), pltpu.VMEM((B_HALO + tt, B_DIM), F32)],
        args=(z, z, caw, cab, lag, lab, cbw), parallel=True, comm=comm)[0]


def _mixer_bwd_local(z, dmix, seq, caw, cab, lag, lab, cbw, comm=None):
    t = z.shape[0]
    tt = _tile(seq, 256)
    tiles_per_seq, z_spec, zh_spec = _mixer_specs(seq, tt)

    def body(z_ref, zh_ref, dmix_ref, caw_ref, cab_ref, lag_ref, lab_ref, cbw_ref,
             da2_ref, dcv_ref, dcaw_ref, dcab_ref, dlag_ref, dlab_ref, dcbw_ref, a1_s, cb_s):
        @pl.when(pl.program_id(0) == 0)
        def _():
            for ref in (dcaw_ref, dcab_ref, dlag_ref, dlab_ref, dcbw_ref):
                ref[...] = jnp.zeros_like(ref)

        first = (pl.program_id(0) % tiles_per_seq) == 0
        _mixer_windows(z_ref, zh_ref, first, a1_s, cb_s, tt)
        a2 = _causal_conv(a1_s, caw_ref, A_TAPS, A_HALO, tt) + cab_ref[...]
        xhat, rstd = _layer_norm_stats(a2)
        a3 = xhat * lag_ref[...] + lab_ref[...]
        s3 = _sigmoid(a3)
        da3 = dmix_ref[:, 0:A_DIM] * (s3 * (1.0 + a3 * (1.0 - s3)))
        dlag_ref[...] += _colsum(da3 * xhat)
        dlab_ref[...] += _colsum(da3)
        dxh = da3 * lag_ref[...]
        da2 = rstd * (dxh - jnp.mean(dxh, axis=-1, keepdims=True) - xhat * jnp.mean(dxh * xhat, axis=-1, keepdims=True))
        da2_ref[...] = da2
        dcab_ref[...] += _colsum(da2)
        for k in range(A_TAPS):
            dcaw_ref[k:k + 1, :] += _colsum(da2 * a1_s[pl.ds(A_HALO - (A_TAPS - 1) + k, tt), :])
        dcv = dmix_ref[:, A_DIM:A_DIM + B_DIM] * z_ref[:, 2 * A_DIM:3 * A_DIM]
        dcv_ref[...] = dcv
        for k in range(B_TAPS):
            dcbw_ref[k:k + 1, :] += _colsum(dcv * cb_s[pl.ds(B_HALO - (B_TAPS - 1) + k, tt), :])

    half = pl.BlockSpec((tt, A_DIM), lambda i: (i, 0))
    return _call(
        body, name="mixer_bwd_local", grid=(t // tt,),
        in_specs=[z_spec, zh_spec, pl.BlockSpec((tt, A_DIM + B_DIM), lambda i: (i, 0)),
                  _vec_spec(A_TAPS, A_DIM), _vec_spec(1, A_DIM), _vec_spec(1, A_DIM), _vec_spec(1, A_DIM), _vec_spec(B_TAPS, B_DIM)],
        out_specs=[half, half, _vec_spec(A_TAPS, A_DIM), _vec_spec(1, A_DIM), _vec_spec(1, A_DIM), _vec_spec(1, A_DIM),
                   _vec_spec(B_TAPS, B_DIM)],
        out_shape=[jax.ShapeDtypeStruct((t, A_DIM), F32), jax.ShapeDtypeStruct((t, B_DIM), F32),
                   jax.ShapeDtypeStruct((A_TAPS, A_DIM), F32), jax.ShapeDtypeStruct((1, A_DIM), F32),
                   jax.ShapeDtypeStruct((1, A_DIM), F32), jax.ShapeDtypeStruct((1, A_DIM), F32),
                   jax.ShapeDtypeStruct((B_TAPS, B_DIM), F32)],
        scratch_shapes=[pltpu.VMEM((A_HALO + tt, A_DIM), F32), pltpu.VMEM((B_HALO + tt, B_DIM), F32)],
        args=(z, z, dmix, caw, cab, lag, lab, cbw), comm=comm)


def _mixer_bwd_input(z, dmix, da2, dcv, seq, caw, cbw, comm=None):
    t = z.shape[0]
    tt = _tile(seq, 256)
    tiles_per_seq, z_spec, zh_spec = _mixer_specs(seq, tt)
    a_blocks = tt // A_HALO
    b_blocks = tt // B_HALO
    last_a = t // A_HALO - 1
    last_b = t // B_HALO - 1

    def body(z_ref, zh_ref, dmix_ref, da2_ref, da2n_ref, dcv_ref, dcvn_ref, caw_ref, cbw_ref, dz_ref, a1_s, cb_s, da2_s, dcv_s):
        pos = pl.program_id(0) % tiles_per_seq
        first = pos == 0
        last = pos == tiles_per_seq - 1
        sig = _mixer_windows(z_ref, zh_ref, first, a1_s, cb_s, tt)
        da2_s[0:tt, :] = da2_ref[...]
        da2_s[tt:tt + A_HALO, :] = jnp.where(last, 0.0, da2n_ref[...])
        dcv_s[0:tt, :] = dcv_ref[...]
        dcv_s[tt:tt + B_HALO, :] = jnp.where(last, 0.0, dcvn_ref[...])
        da1 = caw_ref[0:1, :] * da2_s[pl.ds(A_TAPS - 1, tt), :]
        for k in range(1, A_TAPS):
            da1 = da1 + caw_ref[k:k + 1, :] * da2_s[pl.ds(A_TAPS - 1 - k, tt), :]
        dz_ref[:, 0:A_DIM] = (da1 * sig).astype(BF16)
        dz_ref[:, A_DIM:2 * A_DIM] = (da1 * z_ref[:, 0:A_DIM] * sig * (1.0 - sig)).astype(BF16)
        cv = _causal_conv(cb_s, cbw_ref, B_TAPS, B_HALO, tt)
        dz_ref[:, 2 * A_DIM:3 * A_DIM] = (dmix_ref[:, A_DIM:A_DIM + B_DIM] * cv).astype(BF16)
        dcb = cbw_ref[0:1, :] * dcv_s[pl.ds(B_TAPS - 1, tt), :]
        for k in range(1, B_TAPS):
            dcb = dcb + cbw_ref[k:k + 1, :] * dcv_s[pl.ds(B_TAPS - 1 - k, tt), :]
        dz_ref[:, 3 * A_DIM:4 * A_DIM] = (dcb * z_ref[:, 4 * A_DIM:5 * A_DIM]).astype(BF16)
        dz_ref[:, 4 * A_DIM:5 * A_DIM] = (dcb * z_ref[:, 3 * A_DIM:4 * A_DIM]).astype(BF16)

    half = pl.BlockSpec((tt, A_DIM), lambda i: (i, 0))
    return _call(
        body, name="mixer_bwd_input", grid=(t // tt,),
        in_specs=[z_spec, zh_spec, pl.BlockSpec((tt, A_DIM + B_DIM), lambda i: (i, 0)),
                  half, pl.BlockSpec((A_HALO, A_DIM), lambda i: (jnp.minimum((i + 1) * a_blocks, last_a), 0)),
                  half, pl.BlockSpec((B_HALO, B_DIM), lambda i: (jnp.minimum((i + 1) * b_blocks, last_b), 0)),
                  _vec_spec(A_TAPS, A_DIM), _vec_spec(B_TAPS, B_DIM)],
        out_specs=[pl.BlockSpec((tt, IN_EVEN), lambda i: (i, 0))],
        out_shape=[jax.ShapeDtypeStruct((t, IN_EVEN), BF16)],
        scratch_shapes=[pltpu.VMEM((A_HALO + tt, A_DIM), F32), pltpu.VMEM((B_HALO + tt, B_DIM), F32),
                        pltpu.VMEM((tt + A_HALO, A_DIM), F32), pltpu.VMEM((tt + B_HALO, B_DIM), F32)],
        args=(z, z, dmix, da2, da2, dcv, dcv, caw, cbw), parallel=True, comm=comm)[0]


def _tril_ws(ws_ref, g):
    rows = lax.broadcasted_iota(jnp.int32, (CHUNK, CHUNK), 0)
    cols = lax.broadcasted_iota(jnp.int32, (CHUNK, CHUNK), 1)
    return jnp.where(rows >= cols, ws_ref[g], 0.0).astype(BF16), rows >= cols


def _sgu_fwd(pre, lvg, lvb, ws, bs_b, comm=None):
    t = pre.shape[0]
    tt = _tile(t, 256)

    def body(pre_ref, lvg_ref, lvb_ref, ws_ref, bsb_ref, y_ref):
        vhat, _ = _layer_norm_stats(_gelu(pre_ref[:, C_DIM:2 * C_DIM]))
        vl = (vhat * lvg_ref[...] + lvb_ref[...]).astype(BF16)
        for g in range(C_GROUPS):
            w, _ = _tril_ws(ws_ref, g)
            cols = slice(g * CHUNK, (g + 1) * CHUNK)
            for ci in range(tt // CHUNK):
                rows = slice(ci * CHUNK, (ci + 1) * CHUNK)
                sv = jnp.dot(w, vl[rows, cols], preferred_element_type=F32) + bsb_ref[g]
                y_ref[rows, cols] = (_gelu(pre_ref[rows, cols]) * sv).astype(BF16)

    group = pl.BlockSpec((C_GROUPS, CHUNK, CHUNK), lambda i: (0, 0, 0))
    return _call(
        body, name="sgu_fwd", grid=(t // tt,),
        in_specs=[pl.BlockSpec((tt, 2 * C_DIM), lambda i: (i, 0)), _vec_spec(1, C_DIM), _vec_spec(1, C_DIM), group, group],
        out_specs=[pl.BlockSpec((tt, C_DIM), lambda i: (i, 0))], out_shape=[jax.ShapeDtypeStruct((t, C_DIM), BF16)],
        args=(pre, lvg, lvb, ws, bs_b), parallel=True, comm=comm)[0]


def _sgu_bwd(pre, dy, lvg, lvb, ws, bs_b, comm=None):
    t = pre.shape[0]
    tt = _tile(t, 256)

    def body(pre_ref, dy_ref, lvg_ref, lvb_ref, ws_ref, bsb_ref, dpre_ref, dws_ref, dbsb_ref, dlvg_ref, dlvb_ref, dbin_ref,
             dvl_s):
        @pl.when(pl.program_id(0) == 0)
        def _():
            for ref in (dws_ref, dbsb_ref, dlvg_ref, dlvb_ref, dbin_ref):
                ref[...] = jnp.zeros_like(ref)

        pre_v = pre_ref[:, C_DIM:2 * C_DIM]
        vhat, rstd = _layer_norm_stats(_gelu(pre_v))
        vl = (vhat * lvg_ref[...] + lvb_ref[...]).astype(BF16)
        for g in range(C_GROUPS):
            w, keep = _tril_ws(ws_ref, g)
            cols = slice(g * CHUNK, (g + 1) * CHUNK)
            dws = jnp.zeros((CHUNK, CHUNK), F32)
            dbs = jnp.zeros((CHUNK, 1), F32)
            for ci in range(tt // CHUNK):
                rows = slice(ci * CHUNK, (ci + 1) * CHUNK)
                vl_g = vl[rows, cols]
                sv = jnp.dot(w, vl_g, preferred_element_type=F32) + bsb_ref[g]
                pre_u = pre_ref[rows, cols]
                dyv = dy_ref[rows, cols]
                du = dyv * sv * _gelu_grad(pre_u)
                dpre_ref[rows, cols] = du.astype(BF16)
                dbin_ref[:, cols] += _colsum(du)
                dsv = dyv * _gelu(pre_u)
                dbs = dbs + jnp.sum(dsv, axis=1, keepdims=True)
                dsv16 = dsv.astype(BF16)
                dws = dws + lax.dot_general(dsv16, vl_g, (((1,), (1,)), ((), ())), preferred_element_type=F32)
                dvl_s[rows, cols] = lax.dot_general(w, dsv16, (((0,), (0,)), ((), ())), preferred_element_type=F32)
            dws_ref[g] += jnp.where(keep, dws, 0.0)
            dbsb_ref[g] += dbs
        dvl = dvl_s[...]
        dlvg_ref[...] += _colsum(dvl * vhat)
        dlvb_ref[...] += _colsum(dvl)
        dxh = dvl * lvg_ref[...]
        dv = rstd * (dxh - jnp.mean(dxh, axis=-1, keepdims=True) - vhat * jnp.mean(dxh * vhat, axis=-1, keepdims=True))
        dpv = dv * _gelu_grad(pre_v)
        dpre_ref[:, C_DIM:2 * C_DIM] = dpv.astype(BF16)
        dbin_ref[:, C_DIM:2 * C_DIM] += _colsum(dpv)

    group = pl.BlockSpec((C_GROUPS, CHUNK, CHUNK), lambda i: (0, 0, 0))
    return _call(
        body, name="sgu_bwd", grid=(t // tt,),
        in_specs=[pl.BlockSpec((tt, 2 * C_DIM), lambda i: (i, 0)), pl.BlockSpec((tt, C_DIM), lambda i: (i, 0)),
                  _vec_spec(1, C_DIM), _vec_spec(1, C_DIM), group, group],
        out_specs=[pl.BlockSpec((tt, 2 * C_DIM), lambda i: (i, 0)), group, group,
                   _vec_spec(1, C_DIM), _vec_spec(1, C_DIM), _vec_spec(1, 2 * C_DIM)],
        out_shape=[jax.ShapeDtypeStruct((t, 2 * C_DIM), BF16), jax.ShapeDtypeStruct((C_GROUPS, CHUNK, CHUNK), F32),
                   jax.ShapeDtypeStruct((C_GROUPS, CHUNK, CHUNK), F32), jax.ShapeDtypeStruct((1, C_DIM), F32),
                   jax.ShapeDtypeStruct((1, C_DIM), F32), jax.ShapeDtypeStruct((1, 2 * C_DIM), F32)],
        scratch_shapes=[pltpu.VMEM((tt, C_DIM), F32)],
        args=(pre, dy, lvg, lvb, ws, bs_b), comm=comm)


def _pair_sum(name, part, got, core):
    _, k, n = part.shape
    tk = _tile(k, 256)

    def body(core_ref, p_ref, s_ref, o_ref):
        o_ref[...] = (p_ref[...] + s_ref[...]).astype(BF16)

    return pl.pallas_call(
        body, name=name,
        grid_spec=pltpu.PrefetchScalarGridSpec(
            num_scalar_prefetch=1, grid=(N_CHIP, k // tk),
            in_specs=[pl.BlockSpec((None, tk, n), lambda q, i, core_ref: (2 * q + core_ref[0], i, 0)),
                      pl.BlockSpec((None, tk, n), lambda q, i, core_ref: (q, i, 0))],
            out_specs=pl.BlockSpec((None, tk, n), lambda q, i, core_ref: (q, i, 0))),
        out_shape=jax.ShapeDtypeStruct((N_CHIP, k, n), BF16),
        compiler_params=pltpu.CompilerParams(dimension_semantics=("parallel", "parallel"), vmem_limit_bytes=VMEM_LIMIT_BYTES),
    )(core, part, got)


def _adamw_math(w, g, m, v):
    m = ADAM_B1 * m + (1.0 - ADAM_B1) * g
    v = ADAM_B2 * v + (1.0 - ADAM_B2) * (g * g)
    m_hat = m / (1.0 - ADAM_B1 ** ADAM_STEP)
    v_hat = v / (1.0 - ADAM_B2 ** ADAM_STEP)
    delta = -ADAM_LR * (m_hat / (jnp.sqrt(v_hat) + ADAM_EPS) + ADAM_WD * w)
    return delta, m, v


def _sum_adamw(name, parts, w, m, v):
    n_parts, k, n = parts.shape
    tk = _tile(k, 128)

    def body(p_ref, w_ref, m_ref, v_ref, g_ref, d_ref, nm_ref, nv_ref):
        g = p_ref[0].astype(F32)
        for q in range(1, n_parts):
            g = g + p_ref[q].astype(F32)
        g_ref[...] = g
        d_ref[...], nm_ref[...], nv_ref[...] = _adamw_math(w_ref[...], g, m_ref[...], v_ref[...])

    blk = pl.BlockSpec((tk, n), lambda i: (i, 0))
    return _call(
        body, name=name, grid=(k // tk,),
        in_specs=[pl.BlockSpec((n_parts, tk, n), lambda i: (0, i, 0)), blk, blk, blk], out_specs=[blk] * 4,
        out_shape=[jax.ShapeDtypeStruct((k, n), F32)] * 4, args=(parts, w, m, v), parallel=True)


def _sum_parts(name, parts):
    n_parts, k, n = parts.shape

    def body(p_ref, g_ref):
        g = p_ref[0]
        for q in range(1, n_parts):
            g = g + p_ref[q]
        g_ref[...] = g

    return pl.pallas_call(body, name=name, out_shape=jax.ShapeDtypeStruct((k, n), F32))(parts)


def _adamw(name, w, g, m, v):
    def body(w_ref, g_ref, m_ref, v_ref, d_ref, nm_ref, nv_ref):
        d_ref[...], nm_ref[...], nv_ref[...] = _adamw_math(w_ref[...], g_ref[...], m_ref[...], v_ref[...])

    return pl.pallas_call(body, name=name, out_shape=[jax.ShapeDtypeStruct(w.shape, F32)] * 3)(w, g, m, v)


LANES = 128
PACK_ROWS = 8


def _pack(arrays):
    flat = []
    for a in arrays:
        a = a.reshape(-1)
        flat.append(jnp.pad(a, (0, (-a.shape[0]) % (PACK_ROWS * LANES))))
    return jnp.concatenate(flat).reshape(-1, LANES)


def _unpack(packed, shapes):
    out, row = [], 0
    for shape in shapes:
        size = math.prod(shape)
        rows = -(-size // (PACK_ROWS * LANES)) * PACK_ROWS
        out.append(packed[row:row + rows].reshape(-1)[:size].reshape(shape))
        row += rows
    return out


SMALL = ("ev_norm_g", "ev_conv_a_w", "ev_conv_a_b", "ev_ln_a_g", "ev_ln_a_b", "ev_conv_b_w", "od_norm_g", "od_b_in",
         "od_ln_v_g", "od_ln_v_b", "od_w_s", "od_b_s", "mlp_norm_g", "final_norm_g")
SMALL_SHARDED = ("ev_conv_a_w", "ev_conv_b_w", "od_norm_g", "od_b_in", "od_ln_v_g", "od_ln_v_b")
ORDER = ("ev_norm_g", "ev_w_in", "ev_conv_a_w", "ev_conv_a_b", "ev_ln_a_g", "ev_ln_a_b", "ev_conv_b_w", "ev_w_out",
         "od_norm_g", "od_w_in", "od_b_in", "od_ln_v_g", "od_ln_v_b", "od_w_s", "od_b_s", "od_w_out", "mlp_norm_g",
         "mlp_w1", "mlp_w2", "final_norm_g")


def kernel(x, ev_norm_g, ev_w_in, ev_conv_a_w, ev_conv_a_b, ev_ln_a_g, ev_ln_a_b, ev_conv_b_w, ev_w_out, od_norm_g, od_w_in, od_b_in, od_ln_v_g, od_ln_v_b, od_w_s, od_b_s, od_w_out, mlp_norm_g, mlp_w1, mlp_w2, final_norm_g, loss_target, m_ev_norm_g, m_ev_w_in, m_ev_conv_a_w, m_ev_conv_a_b, m_ev_ln_a_g, m_ev_ln_a_b, m_ev_conv_b_w, m_ev_w_out, m_od_norm_g, m_od_w_in, m_od_b_in, m_od_ln_v_g, m_od_ln_v_b, m_od_w_s, m_od_b_s, m_od_w_out, m_mlp_norm_g, m_mlp_w1, m_mlp_w2, m_final_norm_g, v_ev_norm_g, v_ev_w_in, v_ev_conv_a_w, v_ev_conv_a_b, v_ev_ln_a_g, v_ev_ln_a_b, v_ev_conv_b_w, v_ev_w_out, v_od_norm_g, v_od_w_in, v_od_b_in, v_od_ln_v_g, v_od_ln_v_b, v_od_w_s, v_od_b_s, v_od_w_out, v_mlp_norm_g, v_mlp_w1, v_mlp_w2, v_final_norm_g):
    W = dict(ev_norm_g=ev_norm_g, ev_w_in=ev_w_in, ev_conv_a_w=ev_conv_a_w, ev_conv_a_b=ev_conv_a_b, ev_ln_a_g=ev_ln_a_g,
             ev_ln_a_b=ev_ln_a_b, ev_conv_b_w=ev_conv_b_w, ev_w_out=ev_w_out, od_norm_g=od_norm_g, od_w_in=od_w_in,
             od_b_in=od_b_in, od_ln_v_g=od_ln_v_g, od_ln_v_b=od_ln_v_b, od_w_s=od_w_s, od_b_s=od_b_s, od_w_out=od_w_out,
             mlp_norm_g=mlp_norm_g, mlp_w1=mlp_w1, mlp_w2=mlp_w2, final_norm_g=final_norm_g)
    M = dict(ev_norm_g=m_ev_norm_g, ev_w_in=m_ev_w_in, ev_conv_a_w=m_ev_conv_a_w, ev_conv_a_b=m_ev_conv_a_b,
             ev_ln_a_g=m_ev_ln_a_g, ev_ln_a_b=m_ev_ln_a_b, ev_conv_b_w=m_ev_conv_b_w, ev_w_out=m_ev_w_out,
             od_norm_g=m_od_norm_g, od_w_in=m_od_w_in, od_b_in=m_od_b_in, od_ln_v_g=m_od_ln_v_g, od_ln_v_b=m_od_ln_v_b,
             od_w_s=m_od_w_s, od_b_s=m_od_b_s, od_w_out=m_od_w_out, mlp_norm_g=m_mlp_norm_g, mlp_w1=m_mlp_w1,
             mlp_w2=m_mlp_w2, final_norm_g=m_final_norm_g)
    V = dict(ev_norm_g=v_ev_norm_g, ev_w_in=v_ev_w_in, ev_conv_a_w=v_ev_conv_a_w, ev_conv_a_b=v_ev_conv_a_b,
             ev_ln_a_g=v_ev_ln_a_g, ev_ln_a_b=v_ev_ln_a_b, ev_conv_b_w=v_ev_conv_b_w, ev_w_out=v_ev_w_out,
             od_norm_g=v_od_norm_g, od_w_in=v_od_w_in, od_b_in=v_od_b_in, od_ln_v_g=v_od_ln_v_g, od_ln_v_b=v_od_ln_v_b,
             od_w_s=v_od_w_s, od_b_s=v_od_b_s, od_w_out=v_od_w_out, mlp_norm_g=v_mlp_norm_g, mlp_w1=v_mlp_w1,
             mlp_w2=v_mlp_w2, final_norm_g=v_final_norm_g)

    n_seq, seq, d = x.shape
    t = n_seq * seq
    dev = 4 * lax.axis_index("x") + 2 * lax.axis_index("y") + lax.axis_index("c")
    core = lax.axis_index("c").astype(jnp.int32).reshape(1)

    ev_g, cab, lag, lab = W["ev_norm_g"], W["ev_conv_a_b"], W["ev_ln_a_g"], W["ev_ln_a_b"]
    ws = W["od_w_s"][0]
    bs_b = jnp.broadcast_to(W["od_b_s"][0][:, :, None], (C_GROUPS, CHUNK, CHUNK))
    mlp_g = [W["mlp_norm_g"][l:l + 1] for l in range(2)]
    fin_g = W["final_norm_g"].reshape(1, d)
    small_shards = [W[n][0] for n in SMALL_SHARDED[:2]] + [W[n] for n in SMALL_SHARDED[2:]]

    def w16(name, l=0):
        return W[name][l].astype(BF16)

    h0 = x.reshape(t, d)
    gather = _gather_comm([w16("ev_w_in"), _pack(small_shards)])
    n0 = _rms_fwd("ev_norm", h0, ev_g, comm=gather)
    g_ev_in, g_small = gather.out
    w_ev_in = jnp.transpose(g_ev_in, (1, 0, 2)).reshape(D_MODEL, IN_EVEN)
    per_dev = [_unpack(g_small[i], [s.shape for s in small_shards]) for i in range(N_DEV)]
    caw, cbw, od_g, od_bin, lvg, lvb = [
        jnp.concatenate([per_dev[i][j] for i in range(N_DEV)], axis=-1) for j in range(len(small_shards))]

    gather = _gather_comm([w16("ev_w_out"), w16("mlp_w1", 0)])
    z = _mm_nn("ev_in", n0, w_ev_in, IN_EVEN, _ep_store, [F32], comm=gather)[0]
    w_ev_out = gather.out[0].reshape(D_MODEL, D_MODEL)
    w1 = [gather.out[1], None]

    gather = _gather_comm([w16("mlp_w2", 0)])
    mix = _mixer_fwd(z, seq, caw, cab, lag, lab, cbw, comm=gather)
    w2 = [gather.out[0].reshape(D_FF, D_MODEL), None]

    gather = _gather_comm([w16("od_w_out")])
    h1 = _mm_nn("ev_out", mix, w_ev_out, d, _ep_residual, [F32], extras=(h0,), comm=gather)[0]
    w_od_out = gather.out[0].reshape(D_MODEL, D_MODEL)

    n1 = _rms_fwd("mlp0_norm", h1, mlp_g[0])
    gather = _gather_comm([w16("od_w_in")])
    q0 = _mm_nn("mlp0_up", n1, w1[0], D_FF, _ep_relu_sq, [BF16], comm=gather)[0]
    w_od_in = gather.out[0]

    gather = _gather_comm([w16("mlp_w1", 1)])
    h2 = _mm_nn("mlp0_down", q0, w2[0], d, _ep_residual, [F32], extras=(h1,), comm=gather)[0]
    w1[1] = gather.out[0]

    n2 = _rms_fwd("od_norm", h2, od_g)
    gather = _gather_comm([w16("mlp_w2", 1)])
    pre = _mm_nn("od_in", n2, w_od_in, 2 * C_DIM, _ep_bias, [F32], bias=od_bin, comm=gather)[0]
    w2[1] = gather.out[0].reshape(D_FF, D_MODEL)

    y = _sgu_fwd(pre, lvg, lvb, ws, bs_b)
    h3 = _mm_nn("od_out", y, w_od_out, d, _ep_residual, [F32], extras=(h2,))[0]
    n3 = _rms_fwd("mlp1_norm", h3, mlp_g[1])
    q1 = _mm_nn("mlp1_up", n3, w1[1], D_FF, _ep_relu_sq, [BF16])[0]
    h4 = _mm_nn("mlp1_down", q1, w2[1], d, _ep_residual, [F32], extras=(h3,))[0]
    grad, grad16, d_fin_g, loss_part = _final_loss("final_loss", h4, fin_g, loss_target.reshape(t, d))
    loss = lax.psum(loss_part[0, 0], ("x", "y", "c"))

    by_chip = {}

    def swap(name, parts):
        comm = _pair_comm([parts])
        comm.parts, comm.weight = parts, name
        return comm

    def exchange(swapped):
        sums = _pair_sum(f"pair_sum_{swapped.weight}", swapped.parts, swapped.out[0], core)
        comm = _chip_comm([sums])
        comm.weight = swapped.weight
        return comm

    def done(comm):
        by_chip[comm.weight] = comm.out[0]

    dw2_1 = _mm_tn("mlp1_dw2", q1, grad16).reshape(N_DEV, D_FF // N_DEV, D_MODEL)
    s_a = swap("w2_1", dw2_1)
    dp = _mm_nt("mlp1_dq", grad16, w2[1], D_FF, _ep_relu_sq_grad, [BF16], extras=(q1,), comm=s_a)[0]
    c_a = exchange(s_a)
    dw1_1 = _mm_tn("mlp1_dw1", n3, dp, col_blocks=N_DEV, comm=c_a)
    done(c_a)
    s_b = swap("w1_1", dw1_1)
    dn = _mm_nt("mlp1_dn", dp, w1[1], d, _ep_store, [F32], comm=s_b)[0]
    c_b = exchange(s_b)
    grad, grad16, dg_mlp1 = _rms_bwd("mlp1_norm_bwd", dn, h3, mlp_g[1], grad)
    d_od_out = _mm_tn("od_dw_out", y, grad16).reshape(N_DEV, D_MODEL // N_DEV, D_MODEL)
    s_c = swap("od_out", d_od_out)
    dy = _mm_nt("od_dy", grad16, w_od_out, C_DIM, _ep_store, [F32], comm=s_c)[0]
    c_c = exchange(s_c)
    dpre, d_ws, d_bsb, d_lvg, d_lvb, d_bin = _sgu_bwd(pre, dy, lvg, lvb, ws, bs_b, comm=c_b)
    done(c_b)
    d_od_in = _mm_tn("od_dw_in", n2, dpre, col_blocks=N_DEV, comm=c_c)
    done(c_c)
    s_d = swap("od_in", d_od_in)
    dn2 = _mm_nt("od_dn", dpre, w_od_in, d, _ep_store, [F32], comm=s_d)[0]
    c_d = exchange(s_d)
    grad, grad16, d_od_g = _rms_bwd("od_norm_bwd", dn2, h2, od_g, grad)
    dw2_0 = _mm_tn("mlp0_dw2", q0, grad16, comm=c_d).reshape(N_DEV, D_FF // N_DEV, D_MODEL)
    done(c_d)
    s_e = swap("w2_0", dw2_0)
    dp = _mm_nt("mlp0_dq", grad16, w2[0], D_FF, _ep_relu_sq_grad, [BF16], extras=(q0,), comm=s_e)[0]
    c_e = exchange(s_e)
    dw1_0 = _mm_tn("mlp0_dw1", n1, dp, col_blocks=N_DEV, comm=c_e)
    done(c_e)
    s_f = swap("w1_0", dw1_0)
    dn = _mm_nt("mlp0_dn", dp, w1[0], d, _ep_store, [F32], comm=s_f)[0]
    c_f = exchange(s_f)
    grad, grad16, dg_mlp0 = _rms_bwd("mlp0_norm_bwd", dn, h1, mlp_g[0], grad)
    d_ev_out = _mm_tn("ev_dw_out", mix, grad16).reshape(N_DEV, D_MODEL // N_DEV, D_MODEL)
    s_g = swap("ev_out", d_ev_out)
    dmix = _mm_nt("ev_dmix", grad16, w_ev_out, A_DIM + B_DIM, _ep_store, [F32], comm=s_g)[0]
    c_g = exchange(s_g)
    da2, dcv, d_caw, d_cab, d_lag, d_lab, d_cbw = _mixer_bwd_local(z, dmix, seq, caw, cab, lag, lab, cbw, comm=c_f)
    done(c_f)
    dz = _mixer_bwd_input(z, dmix, da2, dcv, seq, caw, cbw, comm=c_g)
    done(c_g)
    d_ev_in = _mm_tn("ev_dw_in", n0, dz)
    d_ev_in = jnp.transpose(d_ev_in.reshape(D_MODEL, N_DEV, IN_EVEN // N_DEV), (1, 0, 2))
    s_h = swap("ev_in", d_ev_in)
    dn0 = _mm_nt("ev_dn", dz, w_ev_in, d, _ep_store, [F32], comm=s_h)[0]
    c_h = exchange(s_h)
    grad_x, _, d_ev_g = _rms_bwd("ev_norm_bwd", dn0, h0, ev_g, grad, comm=c_h)
    done(c_h)

    shard = {"ev_w_in": ("ev_in",), "ev_w_out": ("ev_out",), "od_w_in": ("od_in",), "od_w_out": ("od_out",),
             "mlp_w1": ("w1_0", "w1_1"), "mlp_w2": ("w2_0", "w2_1")}
    out_g, out_d, out_m, out_v = {}, {}, {}, {}
    for name, keys in shard.items():
        res = [_sum_adamw(f"adamw_{name}_{l}", by_chip[key], W[name][l], M[name][l], V[name][l]) for l, key in enumerate(keys)]
        for store, j in ((out_g, 0), (out_d, 1), (out_m, 2), (out_v, 3)):
            store[name] = jnp.stack([r[j] for r in res])

    small_parts = dict(
        ev_norm_g=d_ev_g, ev_conv_a_w=d_caw[None], ev_conv_a_b=d_cab, ev_ln_a_g=d_lag, ev_ln_a_b=d_lab,
        ev_conv_b_w=d_cbw[None], od_norm_g=d_od_g, od_b_in=d_bin, od_ln_v_g=d_lvg, od_ln_v_b=d_lvb, od_w_s=d_ws[None],
        od_b_s=d_bsb[:, :, 0][None], mlp_norm_g=jnp.concatenate([dg_mlp0, dg_mlp1], axis=0), final_norm_g=d_fin_g.reshape(d))
    full_shapes = [small_parts[n].shape for n in SMALL]
    every = _comm_alone("gather_small_grads", _gather_comm([_pack([small_parts[n] for n in SMALL])]))[0]
    total = _unpack(_sum_parts("sum_small_grads", every), full_shapes)
    small_g = {}
    for n, g in zip(SMALL, total):
        if n in SMALL_SHARDED:
            width = W[n].shape[-1]
            g = lax.dynamic_slice_in_dim(g, dev * width, width, axis=g.ndim - 1)
        small_g[n] = g
    own_shapes = [W[n].shape for n in SMALL]
    packed_d, packed_m, packed_v = _adamw(
        "adamw_small", _pack([W[n] for n in SMALL]), _pack([small_g[n] for n in SMALL]),
        _pack([M[n] for n in SMALL]), _pack([V[n] for n in SMALL]))
    for n, dd, mm, vv in zip(SMALL, _unpack(packed_d, own_shapes), _unpack(packed_m, own_shapes), _unpack(packed_v, own_shapes)):
        out_g[n], out_d[n], out_m[n], out_v[n] = small_g[n], dd, mm, vv

    return (loss, grad_x.reshape(n_seq, seq, d), *[out_g[n] for n in ORDER], *[out_d[n] for n in ORDER],
            *[out_m[n] for n in ORDER], *[out_v[n] for n in ORDER])
```

```python
import math

import jax
import jax.numpy as jnp
from jax import lax
from jax.experimental import pallas as pl
from jax.experimental.pallas import tpu as pltpu

F32 = jnp.float32
BF16 = jnp.bfloat16
MESH = pl.DeviceIdType.MESH

D_MODEL = 1024
A_DIM = 512
B_DIM = 512
IN_EVEN = 2 * A_DIM + 3 * B_DIM
A_TAPS = 31
B_TAPS = 3
CHUNK = 128
C_GROUPS = 8
C_DIM = 1024
D_FF = 4096
RMS_EPS = 1e-6
LN_EPS = 1e-5
N_DEV = 8
N_CHIP = 4

ADAM_LR = 0.001
ADAM_B1 = 0.9
ADAM_B2 = 0.999
ADAM_EPS = 1e-08
ADAM_WD = 0.01
ADAM_STEP = 10

A_HALO = 32
B_HALO = 8
VMEM_LIMIT_BYTES = 56 * 1024 * 1024
INV_SQRT2 = 1.0 / math.sqrt(2.0)
INV_SQRT_2PI = 1.0 / math.sqrt(2.0 * math.pi)
HBM_SPEC = pl.BlockSpec(memory_space=pltpu.HBM)


def _tile(n, want):
    t = min(n, want)
    while n % t:
        t //= 2
    return t


def _sigmoid(x):
    return 1.0 / (1.0 + jnp.exp(-x))


def _gelu(x):
    return 0.5 * x * (1.0 + lax.erf(x * INV_SQRT2))


def _gelu_grad(x):
    return 0.5 * (1.0 + lax.erf(x * INV_SQRT2)) + x * jnp.exp(-0.5 * x * x) * INV_SQRT_2PI


def _colsum(x):
    return jnp.sum(x, axis=0, keepdims=True)


class _Comm:
    def __init__(self, ins, out_shapes, sem_shapes, start, finish, middle=None):
        self.ins, self.out_shapes, self.sem_shapes, self.start, self.finish = ins, out_shapes, sem_shapes, start, finish
        self.middle = middle
        self.out = None


MIDDLE_AT = 0.75


def _call(body, *, name, grid, in_specs, out_specs, out_shape, args, scratch_shapes=(), parallel=False, comm=None):
    if comm is None:
        sem = ("parallel" if parallel else "arbitrary",) * len(grid)
        return pl.pallas_call(
            body, name=name, grid=grid, in_specs=list(in_specs), out_specs=list(out_specs), out_shape=list(out_shape),
            scratch_shapes=list(scratch_shapes),
            compiler_params=pltpu.CompilerParams(dimension_semantics=sem, vmem_limit_bytes=VMEM_LIMIT_BYTES),
        )(*args)
    n_in, n_out, n_scr = len(in_specs), len(out_shape), len(scratch_shapes)
    c_in, c_out = len(comm.ins), len(comm.out_shapes)
    steps = grid
    total = math.prod(steps)
    first_step = (0,) * len(steps)
    last_step = tuple(s - 1 for s in steps)
    middle_step = None
    if comm.middle is not None and 0 < int(MIDDLE_AT * total) < total - 1:
        rest, idx = int(MIDDLE_AT * total), []
        for s in reversed(steps):
            idx.append(rest % s)
            rest //= s
        middle_step = tuple(reversed(idx))

    def carrying(*refs):
        pos = 0
        ins = refs[pos:pos + n_in]; pos += n_in
        c_ins = refs[pos:pos + c_in]; pos += c_in
        outs = refs[pos:pos + n_out]; pos += n_out
        c_outs = refs[pos:pos + c_out]; pos += c_out
        scr = refs[pos:pos + n_scr]; pos += n_scr
        c_sems = refs[pos:]
        def at(step):
            hit = pl.program_id(0) == step[0]
            for axis in range(1, len(steps)):
                hit = jnp.logical_and(hit, pl.program_id(axis) == step[axis])
            return hit

        @pl.when(at(first_step))
        def _():
            comm.start(c_ins, c_outs, c_sems)

        if middle_step is not None:
            @pl.when(at(middle_step))
            def _():
                comm.middle(c_ins, c_outs, c_sems)

        body(*ins, *outs, *scr)

        @pl.when(at(last_step))
        def _():
            if comm.middle is not None and middle_step is None:
                comm.middle(c_ins, c_outs, c_sems)
            comm.finish(c_ins, c_outs, c_sems)

    res = pl.pallas_call(
        carrying, name=name, grid=grid,
        in_specs=[*in_specs, *[HBM_SPEC] * c_in], out_specs=[*out_specs, *[HBM_SPEC] * c_out],
        out_shape=[*out_shape, *comm.out_shapes], scratch_shapes=[*scratch_shapes, *comm.sem_shapes],
        compiler_params=pltpu.CompilerParams(dimension_semantics=("arbitrary",) * len(grid), vmem_limit_bytes=VMEM_LIMIT_BYTES),
    )(*args, *comm.ins)
    comm.out = list(res[n_out:])
    return list(res[:n_out])


def _comm_alone(name, comm):
    c_in, c_out = len(comm.ins), len(comm.out_shapes)

    def body(*refs):
        ins, outs, sems = refs[:c_in], refs[c_in:c_in + c_out], refs[c_in + c_out:]
        comm.start(ins, outs, sems)
        if comm.middle is not None:
            comm.middle(ins, outs, sems)
        comm.finish(ins, outs, sems)

    comm.out = list(pl.pallas_call(
        body, name=name, in_specs=[HBM_SPEC] * c_in, out_specs=[HBM_SPEC] * c_out, out_shape=list(comm.out_shapes),
        scratch_shapes=list(comm.sem_shapes))(*comm.ins))
    return comm.out


def _place():
    x, y, c = lax.axis_index("x"), lax.axis_index("y"), lax.axis_index("c")
    return x, y, c, [(1 - x, y), (x, 1 - y), (1 - x, 1 - y)]


def _gather_comm(shards):
    nw = len(shards)

    def plan(ins, outs, sems):
        send_sems, recv_sems, local_sems = sems
        x, y, c, chips = _place()
        me, sibling = (x, y, c), (x, y, 1 - c)

        def slot(w, p):
            return outs[w].at[4 * p[0] + 2 * p[1] + p[2]]

        def copy(w, k, block, to, src=None):
            return pltpu.make_async_remote_copy(
                src_ref=slot(w, block) if src is None else src, dst_ref=slot(w, block),
                send_sem=send_sems.at[w, k], recv_sem=recv_sems.at[w, k], device_id=to, device_id_type=MESH)

        local = [pltpu.make_async_copy(ins[w], slot(w, me), local_sems.at[w]) for w in range(nw)]
        first = [[copy(w, 0, me, sibling, src=ins[w])] + [copy(w, 1 + j, me, (*chip, c), src=ins[w]) for j, chip in enumerate(chips)]
                 for w in range(nw)]
        landed = [[copy(w, 1 + j, (*chip, c), me) for j, chip in enumerate(chips)] for w in range(nw)]
        passed = [[copy(w, 4 + j, (*chip, c), sibling) for j, chip in enumerate(chips)] for w in range(nw)]
        from_sibling = [[copy(w, 0, sibling, me)] + [copy(w, 4 + j, (*chip, 1 - c), me) for j, chip in enumerate(chips)]
                        for w in range(nw)]
        return local, first, landed, passed, from_sibling

    def start(ins, outs, sems):
        local, first, _, _, _ = plan(ins, outs, sems)
        for cp in local:
            cp.start()
        for row in first:
            for cp in row:
                cp.start()

    def middle(ins, outs, sems):
        _, _, landed, passed, _ = plan(ins, outs, sems)
        for w in range(nw):
            for j in range(3):
                landed[w][j].wait_recv()
                passed[w][j].start()

    def finish(ins, outs, sems):
        local, first, landed, passed, from_sibling = plan(ins, outs, sems)
        for w in range(nw):
            for cp in from_sibling[w]:
                cp.wait_recv()
        for w in range(nw):
            for cp in first[w] + passed[w]:
                cp.wait_send()
        for cp in local:
            cp.wait()

    return _Comm(list(shards), [jax.ShapeDtypeStruct((N_DEV, *s.shape), s.dtype) for s in shards],
                 [pltpu.SemaphoreType.DMA((nw, 7)), pltpu.SemaphoreType.DMA((nw, 7)), pltpu.SemaphoreType.DMA((nw,))],
                 start, finish, middle)


def _pair_comm(parts):
    nw = len(parts)

    def plan(ins, outs, sems):
        send_sems, recv_sems = sems
        x, y, c, _ = _place()
        return [pltpu.make_async_remote_copy(
            src_ref=ins[w].at[2 * q + 1 - c], dst_ref=outs[w].at[q], send_sem=send_sems.at[w, q], recv_sem=recv_sems.at[w, q],
            device_id=(x, y, 1 - c), device_id_type=MESH) for w in range(nw) for q in range(N_CHIP)]

    def start(ins, outs, sems):
        for cp in plan(ins, outs, sems):
            cp.start()

    def finish(ins, outs, sems):
        for cp in plan(ins, outs, sems):
            cp.wait()

    return _Comm(list(parts), [jax.ShapeDtypeStruct((N_CHIP, *p.shape[1:]), p.dtype) for p in parts],
                 [pltpu.SemaphoreType.DMA((nw, N_CHIP)), pltpu.SemaphoreType.DMA((nw, N_CHIP))], start, finish)


def _chip_comm(sums):
    nw = len(sums)

    def plan(ins, outs, sems):
        send_sems, recv_sems, local_sems = sems
        x, y, c, chips = _place()
        my_chip = 2 * x + y
        local = [pltpu.make_async_copy(ins[w].at[my_chip], outs[w].at[my_chip], local_sems.at[w]) for w in range(nw)]
        remote = [pltpu.make_async_remote_copy(
            src_ref=ins[w].at[2 * chip[0] + chip[1]], dst_ref=outs[w].at[my_chip],
            send_sem=send_sems.at[w, j], recv_sem=recv_sems.at[w, j], device_id=(*chip, c), device_id_type=MESH)
            for w in range(nw) for j, chip in enumerate(chips)]
        return local, remote

    def start(ins, outs, sems):
        local, remote = plan(ins, outs, sems)
        for cp in local + remote:
            cp.start()

    def finish(ins, outs, sems):
        local, remote = plan(ins, outs, sems)
        for cp in remote + local:
            cp.wait()

    return _Comm(list(sums), [jax.ShapeDtypeStruct(s.shape, s.dtype) for s in sums],
                 [pltpu.SemaphoreType.DMA((nw, 3)), pltpu.SemaphoreType.DMA((nw, 3)), pltpu.SemaphoreType.DMA((nw,))],
                 start, finish)


def _matmul(name, a, b, *, kind, m, n, k, a_spec, b_spec, tm, tn, tk, out_shape, out_specs, epilogue,
            extras=(), extra_specs=(), comm=None):
    dims = {"nn": (((1,), (0,)), ((), ())), "nt": (((1,), (1,)), ((), ())), "tn": (((0,), (0,)), ((), ()))}[kind]
    nk = k // tk
    n_extra = len(extras)
    n_out = len(out_shape)

    def body(a_ref, b_ref, *rest):
        extra_refs = rest[:n_extra]
        out_refs = rest[n_extra:n_extra + n_out]
        part = lax.dot_general(a_ref[...], b_ref[...], dims, preferred_element_type=F32)
        if nk == 1:
            epilogue(part, extra_refs, out_refs)
            return
        acc_ref = rest[n_extra + n_out]
        step = pl.program_id(2)

        @pl.when(step == 0)
        def _():
            acc_ref[...] = part

        @pl.when(jnp.logical_and(step > 0, step < nk - 1))
        def _():
            acc_ref[...] += part

        @pl.when(step == nk - 1)
        def _():
            epilogue(acc_ref[...] + part, extra_refs, out_refs)

    return _call(
        body, name=name, grid=(m // tm, n // tn, nk), in_specs=[a_spec, b_spec, *extra_specs], out_specs=out_specs,
        out_shape=out_shape, scratch_shapes=[pltpu.VMEM((tm, tn), F32)] if nk > 1 else [], args=(a, b, *extras), comm=comm)


def _mm_nn(name, a, b, n, epilogue, out_dtypes, *, extras=(), bias=None, comm=None):
    m, k = a.shape
    blocked = b.ndim == 3
    tm = _tile(m, 1024)
    tn = b.shape[-1] if blocked else _tile(n, 1024)
    tk = _tile(k, 1024)
    tile = pl.BlockSpec((tm, tn), lambda i, j, kk: (i, j))
    extra_specs = [tile] * len(extras)
    if bias is not None:
        extras = (*extras, bias)
        extra_specs.append(pl.BlockSpec((1, tn), lambda i, j, kk: (0, j)))
    if blocked:
        b_spec = pl.BlockSpec((None, tk, tn), lambda i, j, kk: (j, kk, 0))
    else:
        b_spec = pl.BlockSpec((tk, tn), lambda i, j, kk: (kk, j))
    return _matmul(
        name, a, b, kind="nn", m=m, n=n, k=k, tm=tm, tn=tn, tk=tk,
        a_spec=pl.BlockSpec((tm, tk), lambda i, j, kk: (i, kk)), b_spec=b_spec,
        out_shape=[jax.ShapeDtypeStruct((m, n), dt) for dt in out_dtypes], out_specs=[tile] * len(out_dtypes),
        epilogue=epilogue, extras=extras, extra_specs=extra_specs, comm=comm)


def _mm_nt(name, a, b, n, epilogue, out_dtypes, *, extras=(), comm=None):
    m, k = a.shape
    blocked = b.ndim == 3
    tm = _tile(m, 1024)
    tn = _tile(n, 1024)
    tk = b.shape[-1] if blocked else _tile(k, 1024)
    tile = pl.BlockSpec((tm, tn), lambda i, j, kk: (i, j))
    if blocked:
        b_spec = pl.BlockSpec((None, tn, tk), lambda i, j, kk: (kk, j, 0))
    else:
        b_spec = pl.BlockSpec((tn, tk), lambda i, j, kk: (j, kk))
    return _matmul(
        name, a, b, kind="nt", m=m, n=n, k=k, tm=tm, tn=tn, tk=tk,
        a_spec=pl.BlockSpec((tm, tk), lambda i, j, kk: (i, kk)), b_spec=b_spec,
        out_shape=[jax.ShapeDtypeStruct((m, n), dt) for dt in out_dtypes], out_specs=[tile] * len(out_dtypes),
        epilogue=epilogue, extras=extras, extra_specs=[tile] * len(extras), comm=comm)


def _mm_tn(name, a, b, *, col_blocks=0, comm=None):
    t, k = a.shape
    n = b.shape[1]
    tm = _tile(k, 1024)
    tn = n // col_blocks if col_blocks else _tile(n, 1024)
    tk = _tile(t, 1024)
    if col_blocks:
        out_shape = [jax.ShapeDtypeStruct((col_blocks, k, tn), F32)]
        out_specs = [pl.BlockSpec((None, tm, tn), lambda i, j, kk: (j, i, 0))]
    else:
        out_shape = [jax.ShapeDtypeStruct((k, n), F32)]
        out_specs = [pl.BlockSpec((tm, tn), lambda i, j, kk: (i, j))]

    def epilogue(acc, extra_refs, out_refs):
        out_refs[0][...] = acc

    return _matmul(
        name, a, b, kind="tn", m=k, n=n, k=t, tm=tm, tn=tn, tk=tk,
        a_spec=pl.BlockSpec((tk, tm), lambda i, j, kk: (kk, i)), b_spec=pl.BlockSpec((tk, tn), lambda i, j, kk: (kk, j)),
        out_shape=out_shape, out_specs=out_specs, epilogue=epilogue, comm=comm)[0]


def _ep_store(acc, extra_refs, out_refs):
    out_refs[0][...] = acc.astype(out_refs[0].dtype)


def _ep_residual(acc, extra_refs, out_refs):
    out_refs[0][...] = extra_refs[0][...] + acc


def _ep_bias(acc, extra_refs, out_refs):
    out_refs[0][...] = acc + extra_refs[0][...]


def _ep_relu_sq(acc, extra_refs, out_refs):
    r = jnp.maximum(acc, 0.0)
    out_refs[0][...] = (r * r).astype(BF16)


def _ep_relu_sq_grad(acc, extra_refs, out_refs):
    out_refs[0][...] = (acc * (2.0 * jnp.sqrt(extra_refs[0][...].astype(F32)))).astype(BF16)


def _rms_fwd(name, h, g, comm=None):
    t, d = h.shape
    tt = _tile(t, 512)

    def body(h_ref, g_ref, n_ref):
        x = h_ref[...]
        r = lax.rsqrt(jnp.mean(x * x, axis=-1, keepdims=True) + RMS_EPS)
        n_ref[...] = (x * r * g_ref[...]).astype(BF16)

    return _call(
        body, name=name, grid=(t // tt,),
        in_specs=[pl.BlockSpec((tt, d), lambda i: (i, 0)), pl.BlockSpec((1, d), lambda i: (0, 0))],
        out_specs=[pl.BlockSpec((tt, d), lambda i: (i, 0))], out_shape=[jax.ShapeDtypeStruct((t, d), BF16)],
        args=(h, g), parallel=True, comm=comm)[0]


def _rms_bwd(name, dn, h, g, grad_in, comm=None):
    t, d = h.shape
    tt = _tile(t, 512)

    def body(dn_ref, h_ref, g_ref, gin_ref, gout_ref, gout16_ref, dg_ref):
        @pl.when(pl.program_id(0) == 0)
        def _():
            dg_ref[...] = jnp.zeros_like(dg_ref)

        x = h_ref[...]
        dnv = dn_ref[...]
        r = lax.rsqrt(jnp.mean(x * x, axis=-1, keepdims=True) + RMS_EPS)
        xr = x * r
        dg_ref[...] += _colsum(dnv * xr)
        dy = dnv * g_ref[...]
        out = gin_ref[...] + r * (dy - xr * jnp.mean(dy * xr, axis=-1, keepdims=True))
        gout_ref[...] = out
        gout16_ref[...] = out.astype(BF16)

    row = pl.BlockSpec((tt, d), lambda i: (i, 0))
    vec = pl.BlockSpec((1, d), lambda i: (0, 0))
    return _call(
        body, name=name, grid=(t // tt,), in_specs=[row, row, vec, row], out_specs=[row, row, vec],
        out_shape=[jax.ShapeDtypeStruct((t, d), F32), jax.ShapeDtypeStruct((t, d), BF16), jax.ShapeDtypeStruct((1, d), F32)],
        args=(dn, h, g, grad_in), comm=comm)


def _final_loss(name, h, g, target):
    t, d = h.shape
    tt = _tile(t, 512)

    def body(h_ref, g_ref, tgt_ref, gout_ref, gout16_ref, dg_ref, loss_ref):
        @pl.when(pl.program_id(0) == 0)
        def _():
            dg_ref[...] = jnp.zeros_like(dg_ref)
            loss_ref[...] = jnp.zeros_like(loss_ref)

        x = h_ref[...]
        r = lax.rsqrt(jnp.mean(x * x, axis=-1, keepdims=True) + RMS_EPS)
        xr = x * r
        err = xr * g_ref[...] - tgt_ref[...]
        loss_ref[...] += 0.5 * jnp.sum(jnp.mean(err * err, axis=-1, keepdims=True), axis=0, keepdims=True)
        dout = err * (1.0 / d)
        dg_ref[...] += _colsum(dout * xr)
        dy = dout * g_ref[...]
        out = r * (dy - xr * jnp.mean(dy * xr, axis=-1, keepdims=True))
        gout_ref[...] = out
        gout16_ref[...] = out.astype(BF16)

    row = pl.BlockSpec((tt, d), lambda i: (i, 0))
    vec = pl.BlockSpec((1, d), lambda i: (0, 0))
    one = pl.BlockSpec((1, 1), lambda i: (0, 0))
    return _call(
        body, name=name, grid=(t // tt,), in_specs=[row, vec, row], out_specs=[row, row, vec, one],
        out_shape=[jax.ShapeDtypeStruct((t, d), F32), jax.ShapeDtypeStruct((t, d), BF16),
                   jax.ShapeDtypeStruct((1, d), F32), jax.ShapeDtypeStruct((1, 1), F32)],
        args=(h, g, target))


def _mixer_windows(z_ref, zh_ref, first, a1_s, cb_s, tt):
    sig = _sigmoid(z_ref[:, A_DIM:2 * A_DIM])
    a1_s[A_HALO:A_HALO + tt, :] = z_ref[:, 0:A_DIM] * sig
    a1_h = zh_ref[:, 0:A_DIM] * _sigmoid(zh_ref[:, A_DIM:2 * A_DIM])
    a1_s[0:A_HALO, :] = jnp.where(first, 0.0, a1_h)
    cb_s[B_HALO:B_HALO + tt, :] = z_ref[:, 3 * A_DIM:4 * A_DIM] * z_ref[:, 4 * A_DIM:5 * A_DIM]
    cb_h = zh_ref[A_HALO - B_HALO:A_HALO, 3 * A_DIM:4 * A_DIM] * zh_ref[A_HALO - B_HALO:A_HALO, 4 * A_DIM:5 * A_DIM]
    cb_s[0:B_HALO, :] = jnp.where(first, 0.0, cb_h)
    return sig


def _causal_conv(win_s, w_ref, taps, halo, tt):
    base = halo - (taps - 1)
    acc = w_ref[0:1, :] * win_s[pl.ds(base, tt), :]
    for k in range(1, taps):
        acc = acc + w_ref[k:k + 1, :] * win_s[pl.ds(base + k, tt), :]
    return acc


SUBLANES = 8
LANE_BLOCK = 128
ROW_BLOCK = 128
SHIFT_ROWS = A_HALO - SUBLANES


def _shifted_copies(win_s, sh_s, tt):
    for b in range(1, SUBLANES):
        sh_s[b - 1] = win_s[pl.ds(b, tt + SHIFT_ROWS), :]


def _window_rows(win_s, sh_s, offset, rows, cols):
    b = offset % SUBLANES
    if b == 0:
        return win_s[pl.ds(offset, rows), cols]
    return sh_s[b - 1, pl.ds(offset - b, rows), cols]


def _blocks(tt):
    rb = min(tt, ROW_BLOCK)
    return rb, [(r, slice(lb * LANE_BLOCK, (lb + 1) * LANE_BLOCK))
                for lb in range(A_DIM // LANE_BLOCK) for r in range(0, tt, rb)]


def _conv_taps(win_s, sh_s, w_ref, offsets, out_s, tt, bias_ref=None):
    rb, blocks = _blocks(tt)
    for r, cols in blocks:
        acc = w_ref[0:1, cols] * _window_rows(win_s, sh_s, r + offsets[0], rb, cols)
        for k in range(1, len(offsets)):
            acc = acc + w_ref[k:k + 1, cols] * _window_rows(win_s, sh_s, r + offsets[k], rb, cols)
        out_s[r:r + rb, cols] = acc if bias_ref is None else acc + bias_ref[:, cols]


A_CAUSAL = [A_HALO - (A_TAPS - 1) + k for k in range(A_TAPS)]
A_ANTICAUSAL = [A_TAPS - 1 - k for k in range(A_TAPS)]


def _layer_norm_stats(x):
    mu = jnp.mean(x, axis=-1, keepdims=True)
    xc = x - mu
    rstd = lax.rsqrt(jnp.mean(xc * xc, axis=-1, keepdims=True) + LN_EPS)
    return xc * rstd, rstd


def _mixer_specs(seq, tt):
    tiles_per_seq = seq // tt
    halo_blocks = tt // A_HALO
    z_spec = pl.BlockSpec((tt, IN_EVEN), lambda i: (i, 0))
    zh_spec = pl.BlockSpec((A_HALO, IN_EVEN), lambda i: (jnp.maximum(i * halo_blocks - 1, 0), 0))
    return tiles_per_seq, z_spec, zh_spec


def _vec_spec(rows, cols):
    return pl.BlockSpec((rows, cols), lambda i: (0, 0))


def _mixer_fwd(z, seq, caw, cab, lag, lab, cbw, comm=None):
    t = z.shape[0]
    tt = _tile(seq, 256)
    tiles_per_seq, z_spec, zh_spec = _mixer_specs(seq, tt)

    def body(z_ref, zh_ref, caw_ref, cab_ref, lag_ref, lab_ref, cbw_ref, mix_ref, a1_s, cb_s, sh_s, a2_s):
        first = (pl.program_id(0) % tiles_per_seq) == 0
        _mixer_windows(z_ref, zh_ref, first, a1_s, cb_s, tt)
        _shifted_copies(a1_s, sh_s, tt)
        _conv_taps(a1_s, sh_s, caw_ref, A_CAUSAL, a2_s, tt, bias_ref=cab_ref)
        xhat, _ = _layer_norm_stats(a2_s[...])
        a3 = xhat * lag_ref[...] + lab_ref[...]
        mix_ref[:, 0:A_DIM] = (a3 * _sigmoid(a3)).astype(BF16)
        cv = _causal_conv(cb_s, cbw_ref, B_TAPS, B_HALO, tt)
        mix_ref[:, A_DIM:A_DIM + B_DIM] = (z_ref[:, 2 * A_DIM:3 * A_DIM] * cv).astype(BF16)

    return _call(
        body, name="mixer_fwd", grid=(t // tt,),
        in_specs=[z_spec, zh_spec, _vec_spec(A_TAPS, A_DIM), _vec_spec(1, A_DIM), _vec_spec(1, A_DIM), _vec_spec(1, A_DIM),
                  _vec_spec(B_TAPS, B_DIM)],
        out_specs=[pl.BlockSpec((tt, A_DIM + B_DIM), lambda i: (i, 0))],
        out_shape=[jax.ShapeDtypeStruct((t, A_DIM + B_DIM), BF16)],
        scratch_shapes=[pltpu.VMEM((A_HALO + tt, A_DIM), F32), pltpu.VMEM((B_HALO + tt, B_DIM), F32),
                        pltpu.VMEM((SUBLANES - 1, tt + SHIFT_ROWS, A_DIM), F32), pltpu.VMEM((tt, A_DIM), F32)],
        args=(z, z, caw, cab, lag, lab, cbw), parallel=True, comm=comm)[0]


def _mixer_bwd_local(z, dmix, seq, caw, cab, lag, lab, cbw, comm=None):
    t = z.shape[0]
    tt = _tile(seq, 256)
    tiles_per_seq, z_spec, zh_spec = _mixer_specs(seq, tt)

    def body(z_ref, zh_ref, dmix_ref, caw_ref, cab_ref, lag_ref, lab_ref, cbw_ref,
             da2_ref, dcv_ref, dcaw_ref, dcab_ref, dlag_ref, dlab_ref, dcbw_ref, a1_s, cb_s, sh_s, a2_s):
        @pl.when(pl.program_id(0) == 0)
        def _():
            for ref in (dcaw_ref, dcab_ref, dlag_ref, dlab_ref, dcbw_ref):
                ref[...] = jnp.zeros_like(ref)

        first = (pl.program_id(0) % tiles_per_seq) == 0
        _mixer_windows(z_ref, zh_ref, first, a1_s, cb_s, tt)
        _shifted_copies(a1_s, sh_s, tt)
        _conv_taps(a1_s, sh_s, caw_ref, A_CAUSAL, a2_s, tt, bias_ref=cab_ref)
        xhat, rstd = _layer_norm_stats(a2_s[...])
        a3 = xhat * lag_ref[...] + lab_ref[...]
        s3 = _sigmoid(a3)
        da3 = dmix_ref[:, 0:A_DIM] * (s3 * (1.0 + a3 * (1.0 - s3)))
        dlag_ref[...] += _colsum(da3 * xhat)
        dlab_ref[...] += _colsum(da3)
        dxh = da3 * lag_ref[...]
        da2 = rstd * (dxh - jnp.mean(dxh, axis=-1, keepdims=True) - xhat * jnp.mean(dxh * xhat, axis=-1, keepdims=True))
        da2_ref[...] = da2
        dcab_ref[...] += _colsum(da2)
        rb, blocks = _blocks(tt)
        for r, cols in blocks:
            da2_b = da2_ref[r:r + rb, cols]
            for k in range(A_TAPS):
                dcaw_ref[k:k + 1, cols] += _colsum(da2_b * _window_rows(a1_s, sh_s, r + A_CAUSAL[k], rb, cols))
        dcv =dmix_ref[:, A_DIM:A_DIM + B_DIM] * z_ref[:, 2 * A_DIM:3 * A_DIM]
        dcv_ref[...] = dcv
        for k in range(B_TAPS):
            dcbw_ref[k:k + 1, :] += _colsum(dcv * cb_s[pl.ds(B_HALO - (B_TAPS - 1) + k, tt), :])

    half = pl.BlockSpec((tt, A_DIM), lambda i: (i, 0))
    return _call(
        body, name="mixer_bwd_local", grid=(t // tt,),
        in_specs=[z_spec, zh_spec, pl.BlockSpec((tt, A_DIM + B_DIM), lambda i: (i, 0)),
                  _vec_spec(A_TAPS, A_DIM), _vec_spec(1, A_DIM), _vec_spec(1, A_DIM), _vec_spec(1, A_DIM), _vec_spec(B_TAPS, B_DIM)],
        out_specs=[half, half, _vec_spec(A_TAPS, A_DIM), _vec_spec(1, A_DIM), _vec_spec(1, A_DIM), _vec_spec(1, A_DIM),
                   _vec_spec(B_TAPS, B_DIM)],
        out_shape=[jax.ShapeDtypeStruct((t, A_DIM), F32), jax.ShapeDtypeStruct((t, B_DIM), F32),
                   jax.ShapeDtypeStruct((A_TAPS, A_DIM), F32), jax.ShapeDtypeStruct((1, A_DIM), F32),
                   jax.ShapeDtypeStruct((1, A_DIM), F32), jax.ShapeDtypeStruct((1, A_DIM), F32),
                   jax.ShapeDtypeStruct((B_TAPS, B_DIM), F32)],
        scratch_shapes=[pltpu.VMEM((A_HALO + tt, A_DIM), F32), pltpu.VMEM((B_HALO + tt, B_DIM), F32),
                        pltpu.VMEM((SUBLANES - 1, tt + SHIFT_ROWS, A_DIM), F32), pltpu.VMEM((tt, A_DIM), F32)],
        args=(z, z, dmix, caw, cab, lag, lab, cbw), comm=comm)


def _mixer_bwd_input(z, dmix, da2, dcv, seq, caw, cbw, comm=None):
    t = z.shape[0]
    tt = _tile(seq, 256)
    tiles_per_seq, z_spec, zh_spec = _mixer_specs(seq, tt)
    a_blocks = tt // A_HALO
    b_blocks = tt // B_HALO
    last_a = t // A_HALO - 1
    last_b = t // B_HALO - 1

    def body(z_ref, zh_ref, dmix_ref, da2_ref, da2n_ref, dcv_ref, dcvn_ref, caw_ref, cbw_ref, dz_ref, a1_s, cb_s, da2_s, dcv_s,
             sh_s, da1_s):
        pos = pl.program_id(0) % tiles_per_seq
        first = pos == 0
        last = pos == tiles_per_seq - 1
        sig = _mixer_windows(z_ref, zh_ref, first, a1_s, cb_s, tt)
        da2_s[0:tt, :] = da2_ref[...]
        da2_s[tt:tt + A_HALO, :] = jnp.where(last, 0.0, da2n_ref[...])
        dcv_s[0:tt, :] = dcv_ref[...]
        dcv_s[tt:tt + B_HALO, :] = jnp.where(last, 0.0, dcvn_ref[...])
        _shifted_copies(da2_s, sh_s, tt)
        _conv_taps(da2_s, sh_s, caw_ref, A_ANTICAUSAL, da1_s, tt)
        da1 = da1_s[...]
        dz_ref[:, 0:A_DIM] = (da1 * sig).astype(BF16)
        dz_ref[:, A_DIM:2 * A_DIM] = (da1 * z_ref[:, 0:A_DIM] * sig * (1.0 - sig)).astype(BF16)
        cv = _causal_conv(cb_s, cbw_ref, B_TAPS, B_HALO, tt)
        dz_ref[:, 2 * A_DIM:3 * A_DIM] = (dmix_ref[:, A_DIM:A_DIM + B_DIM] * cv).astype(BF16)
        dcb = cbw_ref[0:1, :] * dcv_s[pl.ds(B_TAPS - 1, tt), :]
        for k in range(1, B_TAPS):
            dcb = dcb + cbw_ref[k:k + 1, :] * dcv_s[pl.ds(B_TAPS - 1 - k, tt), :]
        dz_ref[:, 3 * A_DIM:4 * A_DIM] = (dcb * z_ref[:, 4 * A_DIM:5 * A_DIM]).astype(BF16)
        dz_ref[:, 4 * A_DIM:5 * A_DIM] = (dcb * z_ref[:, 3 * A_DIM:4 * A_DIM]).astype(BF16)

    half = pl.BlockSpec((tt, A_DIM), lambda i: (i, 0))
    return _call(
        body, name="mixer_bwd_input", grid=(t // tt,),
        in_specs=[z_spec, zh_spec, pl.BlockSpec((tt, A_DIM + B_DIM), lambda i: (i, 0)),
                  half, pl.BlockSpec((A_HALO, A_DIM), lambda i: (jnp.minimum((i + 1) * a_blocks, last_a), 0)),
                  half, pl.BlockSpec((B_HALO, B_DIM), lambda i: (jnp.minimum((i + 1) * b_blocks, last_b), 0)),
                  _vec_spec(A_TAPS, A_DIM), _vec_spec(B_TAPS, B_DIM)],
        out_specs=[pl.BlockSpec((tt, IN_EVEN), lambda i: (i, 0))],
        out_shape=[jax.ShapeDtypeStruct((t, IN_EVEN), BF16)],
        scratch_shapes=[pltpu.VMEM((A_HALO + tt, A_DIM), F32), pltpu.VMEM((B_HALO + tt, B_DIM), F32),
                        pltpu.VMEM((tt + A_HALO, A_DIM), F32), pltpu.VMEM((tt + B_HALO, B_DIM), F32),
                        pltpu.VMEM((SUBLANES - 1, tt + SHIFT_ROWS, A_DIM), F32), pltpu.VMEM((tt, A_DIM), F32)],
        args=(z, z, dmix, da2, da2, dcv, dcv, caw, cbw), parallel=True, comm=comm)[0]


def _tril_ws(ws_ref, g):
    rows = lax.broadcasted_iota(jnp.int32, (CHUNK, CHUNK), 0)
    cols = lax.broadcasted_iota(jnp.int32, (CHUNK, CHUNK), 1)
    return jnp.where(rows >= cols, ws_ref[g], 0.0).astype(BF16), rows >= cols


def _sgu_fwd(pre, lvg, lvb, ws, bs_b, comm=None):
    t = pre.shape[0]
    tt = _tile(t, 256)

    def body(pre_ref, lvg_ref, lvb_ref, ws_ref, bsb_ref, y_ref):
        vhat, _ = _layer_norm_stats(_gelu(pre_ref[:, C_DIM:2 * C_DIM]))
        vl = (vhat * lvg_ref[...] + lvb_ref[...]).astype(BF16)
        for g in range(C_GROUPS):
            w, _ = _tril_ws(ws_ref, g)
            cols = slice(g * CHUNK, (g + 1) * CHUNK)
            for ci in range(tt // CHUNK):
                rows = slice(ci * CHUNK, (ci + 1) * CHUNK)
                sv = jnp.dot(w, vl[rows, cols], preferred_element_type=F32) + bsb_ref[g]
                y_ref[rows, cols] = (_gelu(pre_ref[rows, cols]) * sv).astype(BF16)

    group = pl.BlockSpec((C_GROUPS, CHUNK, CHUNK), lambda i: (0, 0, 0))
    return _call(
        body, name="sgu_fwd", grid=(t // tt,),
        in_specs=[pl.BlockSpec((tt, 2 * C_DIM), lambda i: (i, 0)), _vec_spec(1, C_DIM), _vec_spec(1, C_DIM), group, group],
        out_specs=[pl.BlockSpec((tt, C_DIM), lambda i: (i, 0))], out_shape=[jax.ShapeDtypeStruct((t, C_DIM), BF16)],
        args=(pre, lvg, lvb, ws, bs_b), parallel=True, comm=comm)[0]


def _sgu_bwd(pre, dy, lvg, lvb, ws, bs_b, comm=None):
    t = pre.shape[0]
    tt = _tile(t, 256)

    def body(pre_ref, dy_ref, lvg_ref, lvb_ref, ws_ref, bsb_ref, dpre_ref, dws_ref, dbsb_ref, dlvg_ref, dlvb_ref, dbin_ref,
             dvl_s):
        @pl.when(pl.program_id(0) == 0)
        def _():
            for ref in (dws_ref, dbsb_ref, dlvg_ref, dlvb_ref, dbin_ref):
                ref[...] = jnp.zeros_like(ref)

        pre_v = pre_ref[:, C_DIM:2 * C_DIM]
        vhat, rstd = _layer_norm_stats(_gelu(pre_v))
        vl = (vhat * lvg_ref[...] + lvb_ref[...]).astype(BF16)
        for g in range(C_GROUPS):
            w, keep = _tril_ws(ws_ref, g)
            cols = slice(g * CHUNK, (g + 1) * CHUNK)
            dws = jnp.zeros((CHUNK, CHUNK), F32)
            dbs = jnp.zeros((CHUNK, 1), F32)
            for ci in range(tt // CHUNK):
                rows = slice(ci * CHUNK, (ci + 1) * CHUNK)
                vl_g = vl[rows, cols]
                sv = jnp.dot(w, vl_g, preferred_element_type=F32) + bsb_ref[g]
                pre_u = pre_ref[rows, cols]
                dyv = dy_ref[rows, cols]
                du = dyv * sv * _gelu_grad(pre_u)
                dpre_ref[rows, cols] = du.astype(BF16)
                dbin_ref[:, cols] += _colsum(du)
                dsv = dyv * _gelu(pre_u)
                dbs = dbs + jnp.sum(dsv, axis=1, keepdims=True)
                dsv16 = dsv.astype(BF16)
                dws = dws + lax.dot_general(dsv16, vl_g, (((1,), (1,)), ((), ())), preferred_element_type=F32)
                dvl_s[rows, cols] = lax.dot_general(w, dsv16, (((0,), (0,)), ((), ())), preferred_element_type=F32)
            dws_ref[g] += jnp.where(keep, dws, 0.0)
            dbsb_ref[g] += dbs
        dvl = dvl_s[...]
        dlvg_ref[...] += _colsum(dvl * vhat)
        dlvb_ref[...] += _colsum(dvl)
        dxh = dvl * lvg_ref[...]
        dv = rstd * (dxh - jnp.mean(dxh, axis=-1, keepdims=True) - vhat * jnp.mean(dxh * vhat, axis=-1, keepdims=True))
        dpv = dv * _gelu_grad(pre_v)
        dpre_ref[:, C_DIM:2 * C_DIM] = dpv.astype(BF16)
        dbin_ref[:, C_DIM:2 * C_DIM] += _colsum(dpv)

    group = pl.BlockSpec((C_GROUPS, CHUNK, CHUNK), lambda i: (0, 0, 0))
    return _call(
        body, name="sgu_bwd", grid=(t // tt,),
        in_specs=[pl.BlockSpec((tt, 2 * C_DIM), lambda i: (i, 0)), pl.BlockSpec((tt, C_DIM), lambda i: (i, 0)),
                  _vec_spec(1, C_DIM), _vec_spec(1, C_DIM), group, group],
        out_specs=[pl.BlockSpec((tt, 2 * C_DIM), lambda i: (i, 0)), group, group,
                   _vec_spec(1, C_DIM), _vec_spec(1, C_DIM), _vec_spec(1, 2 * C_DIM)],
        out_shape=[jax.ShapeDtypeStruct((t, 2 * C_DIM), BF16), jax.ShapeDtypeStruct((C_GROUPS, CHUNK, CHUNK), F32),
                   jax.ShapeDtypeStruct((C_GROUPS, CHUNK, CHUNK), F32), jax.ShapeDtypeStruct((1, C_DIM), F32),
                   jax.ShapeDtypeStruct((1, C_DIM), F32), jax.ShapeDtypeStruct((1, 2 * C_DIM), F32)],
        scratch_shapes=[pltpu.VMEM((tt, C_DIM), F32)],
        args=(pre, dy, lvg, lvb, ws, bs_b), comm=comm)


def _pair_sum(name, part, got, core):
    _, k, n = part.shape
    tk = _tile(k, 256)

    def body(core_ref, p_ref, s_ref, o_ref):
        o_ref[...] = (p_ref[...] + s_ref[...]).astype(BF16)

    return pl.pallas_call(
        body, name=name,
        grid_spec=pltpu.PrefetchScalarGridSpec(
            num_scalar_prefetch=1, grid=(N_CHIP, k // tk),
            in_specs=[pl.BlockSpec((None, tk, n), lambda q, i, core_ref: (2 * q + core_ref[0], i, 0)),
                      pl.BlockSpec((None, tk, n), lambda q, i, core_ref: (q, i, 0))],
            out_specs=pl.BlockSpec((None, tk, n), lambda q, i, core_ref: (q, i, 0))),
        out_shape=jax.ShapeDtypeStruct((N_CHIP, k, n), BF16),
        compiler_params=pltpu.CompilerParams(dimension_semantics=("parallel", "parallel"), vmem_limit_bytes=VMEM_LIMIT_BYTES),
    )(core, part, got)


def _adamw_math(w, g, m, v):
    m = ADAM_B1 * m + (1.0 - ADAM_B1) * g
    v = ADAM_B2 * v + (1.0 - ADAM_B2) * (g * g)
    m_hat = m / (1.0 - ADAM_B1 ** ADAM_STEP)
    v_hat = v / (1.0 - ADAM_B2 ** ADAM_STEP)
    delta = -ADAM_LR * (m_hat / (jnp.sqrt(v_hat) + ADAM_EPS) + ADAM_WD * w)
    return delta, m, v


def _sum_adamw(name, parts, w, m, v):
    layers = len(parts)
    n_parts, k, n = parts[0].shape
    tk = _tile(k, 128)

    def body(*refs):
        p_refs = refs[:layers]
        w_ref, m_ref, v_ref, g_ref, d_ref, nm_ref, nv_ref = refs[layers:]

        def total(p_ref):
            g = p_ref[0].astype(F32)
            for q in range(1, n_parts):
                g = g + p_ref[q].astype(F32)
            return g

        g = total(p_refs[0])
        for l in range(1, layers):
            g = jnp.where(pl.program_id(0) == l, total(p_refs[l]), g)
        g_ref[...] = g
        d_ref[...], nm_ref[...], nv_ref[...] = _adamw_math(w_ref[...], g, m_ref[...], v_ref[...])

    blk = pl.BlockSpec((None, tk, n), lambda l, i: (l, i, 0))
    return _call(
        body, name=name, grid=(layers, k // tk),
        in_specs=[pl.BlockSpec((n_parts, tk, n), lambda l, i: (0, i, 0))] * layers + [blk, blk, blk], out_specs=[blk] * 4,
        out_shape=[jax.ShapeDtypeStruct((layers, k, n), F32)] * 4, args=(*parts, w, m, v), parallel=True)


def _small_adamw(name, parts, w, m, v):
    count = len(parts)

    def body(*refs):
        p_refs, w_refs, m_refs, v_refs = (refs[j * count:(j + 1) * count] for j in range(4))
        g_refs, d_refs, nm_refs, nv_refs = (refs[(4 + j) * count:(5 + j) * count] for j in range(4))
        for i in range(count):
            g = p_refs[i][0]
            for dev in range(1, N_DEV):
                g = g + p_refs[i][dev]
            g_refs[i][...] = g
            d_refs[i][...], nm_refs[i][...], nv_refs[i][...] = _adamw_math(w_refs[i][...], g, m_refs[i][...], v_refs[i][...])

    res = pl.pallas_call(
        body, name=name, out_shape=[jax.ShapeDtypeStruct(a.shape, F32) for a in w] * 4,
        compiler_params=pltpu.CompilerParams(vmem_limit_bytes=VMEM_LIMIT_BYTES))(*parts, *w, *m, *v)
    return [res[j * count:(j + 1) * count] for j in range(4)]


def _rows(a):
    return a.reshape(-1, a.shape[-1])


def _whole(gathered):
    return jnp.transpose(gathered, (1, 0, 2)).reshape(gathered.shape[1], -1)


SMALL =("ev_norm_g", "ev_conv_a_w", "ev_conv_a_b", "ev_ln_a_g", "ev_ln_a_b", "ev_conv_b_w", "od_norm_g", "od_b_in",
         "od_ln_v_g", "od_ln_v_b", "od_w_s", "od_b_s", "mlp_norm_g", "final_norm_g")
SMALL_SHARDED = ("ev_conv_a_w", "ev_conv_b_w", "od_norm_g", "od_b_in", "od_ln_v_g", "od_ln_v_b")
ORDER = ("ev_norm_g", "ev_w_in", "ev_conv_a_w", "ev_conv_a_b", "ev_ln_a_g", "ev_ln_a_b", "ev_conv_b_w", "ev_w_out",
         "od_norm_g", "od_w_in", "od_b_in", "od_ln_v_g", "od_ln_v_b", "od_w_s", "od_b_s", "od_w_out", "mlp_norm_g",
         "mlp_w1", "mlp_w2", "final_norm_g")


def kernel(x, ev_norm_g, ev_w_in, ev_conv_a_w, ev_conv_a_b, ev_ln_a_g, ev_ln_a_b, ev_conv_b_w, ev_w_out, od_norm_g, od_w_in, od_b_in, od_ln_v_g, od_ln_v_b, od_w_s, od_b_s, od_w_out, mlp_norm_g, mlp_w1, mlp_w2, final_norm_g, loss_target, m_ev_norm_g, m_ev_w_in, m_ev_conv_a_w, m_ev_conv_a_b, m_ev_ln_a_g, m_ev_ln_a_b, m_ev_conv_b_w, m_ev_w_out, m_od_norm_g, m_od_w_in, m_od_b_in, m_od_ln_v_g, m_od_ln_v_b, m_od_w_s, m_od_b_s, m_od_w_out, m_mlp_norm_g, m_mlp_w1, m_mlp_w2, m_final_norm_g, v_ev_norm_g, v_ev_w_in, v_ev_conv_a_w, v_ev_conv_a_b, v_ev_ln_a_g, v_ev_ln_a_b, v_ev_conv_b_w, v_ev_w_out, v_od_norm_g, v_od_w_in, v_od_b_in, v_od_ln_v_g, v_od_ln_v_b, v_od_w_s, v_od_b_s, v_od_w_out, v_mlp_norm_g, v_mlp_w1, v_mlp_w2, v_final_norm_g):
    W = dict(ev_norm_g=ev_norm_g, ev_w_in=ev_w_in, ev_conv_a_w=ev_conv_a_w, ev_conv_a_b=ev_conv_a_b, ev_ln_a_g=ev_ln_a_g,
             ev_ln_a_b=ev_ln_a_b, ev_conv_b_w=ev_conv_b_w, ev_w_out=ev_w_out, od_norm_g=od_norm_g, od_w_in=od_w_in,
             od_b_in=od_b_in, od_ln_v_g=od_ln_v_g, od_ln_v_b=od_ln_v_b, od_w_s=od_w_s, od_b_s=od_b_s, od_w_out=od_w_out,
             mlp_norm_g=mlp_norm_g, mlp_w1=mlp_w1, mlp_w2=mlp_w2, final_norm_g=final_norm_g)
    M = dict(ev_norm_g=m_ev_norm_g, ev_w_in=m_ev_w_in, ev_conv_a_w=m_ev_conv_a_w, ev_conv_a_b=m_ev_conv_a_b,
             ev_ln_a_g=m_ev_ln_a_g, ev_ln_a_b=m_ev_ln_a_b, ev_conv_b_w=m_ev_conv_b_w, ev_w_out=m_ev_w_out,
             od_norm_g=m_od_norm_g, od_w_in=m_od_w_in, od_b_in=m_od_b_in, od_ln_v_g=m_od_ln_v_g, od_ln_v_b=m_od_ln_v_b,
             od_w_s=m_od_w_s, od_b_s=m_od_b_s, od_w_out=m_od_w_out, mlp_norm_g=m_mlp_norm_g, mlp_w1=m_mlp_w1,
             mlp_w2=m_mlp_w2, final_norm_g=m_final_norm_g)
    V = dict(ev_norm_g=v_ev_norm_g, ev_w_in=v_ev_w_in, ev_conv_a_w=v_ev_conv_a_w, ev_conv_a_b=v_ev_conv_a_b,
             ev_ln_a_g=v_ev_ln_a_g, ev_ln_a_b=v_ev_ln_a_b, ev_conv_b_w=v_ev_conv_b_w, ev_w_out=v_ev_w_out,
             od_norm_g=v_od_norm_g, od_w_in=v_od_w_in, od_b_in=v_od_b_in, od_ln_v_g=v_od_ln_v_g, od_ln_v_b=v_od_ln_v_b,
             od_w_s=v_od_w_s, od_b_s=v_od_b_s, od_w_out=v_od_w_out, mlp_norm_g=v_mlp_norm_g, mlp_w1=v_mlp_w1,
             mlp_w2=v_mlp_w2, final_norm_g=v_final_norm_g)

    n_seq, seq, d = x.shape
    t = n_seq * seq
    dev = 4 * lax.axis_index("x") + 2 * lax.axis_index("y") + lax.axis_index("c")
    core = lax.axis_index("c").astype(jnp.int32).reshape(1)

    ev_g, cab, lag, lab = W["ev_norm_g"], W["ev_conv_a_b"], W["ev_ln_a_g"], W["ev_ln_a_b"]
    ws = W["od_w_s"][0]
    bs_b = jnp.broadcast_to(W["od_b_s"][0][:, :, None], (C_GROUPS, CHUNK, CHUNK))
    mlp_g = [W["mlp_norm_g"][l:l + 1] for l in range(2)]
    fin_g = W["final_norm_g"].reshape(1, d)

    def w16(name, l=0):
        return W[name][l].astype(BF16)

    h0 = x.reshape(t, d)
    gather = _gather_comm([w16("ev_w_in")] + [_rows(W[n]) for n in SMALL_SHARDED])
    n0 = _rms_fwd("ev_norm", h0, ev_g, comm=gather)
    w_ev_in, caw, cbw, od_g, od_bin, lvg, lvb = [_whole(g) for g in gather.out]

    gather = _gather_comm([w16("ev_w_out"), w16("mlp_w1", 0)])
    z = _mm_nn("ev_in", n0, w_ev_in, IN_EVEN, _ep_store, [F32], comm=gather)[0]
    w_ev_out = gather.out[0].reshape(D_MODEL, D_MODEL)
    w1 = [gather.out[1], None]

    gather = _gather_comm([w16("mlp_w2", 0)])
    mix = _mixer_fwd(z, seq, caw, cab, lag, lab, cbw, comm=gather)
    w2 = [gather.out[0].reshape(D_FF, D_MODEL), None]

    gather = _gather_comm([w16("od_w_out")])
    h1 = _mm_nn("ev_out", mix, w_ev_out, d, _ep_residual, [F32], extras=(h0,), comm=gather)[0]
    w_od_out = gather.out[0].reshape(D_MODEL, D_MODEL)

    n1 = _rms_fwd("mlp0_norm", h1, mlp_g[0])
    gather = _gather_comm([w16("od_w_in")])
    q0 = _mm_nn("mlp0_up", n1, w1[0], D_FF, _ep_relu_sq, [BF16], comm=gather)[0]
    w_od_in = gather.out[0]

    gather = _gather_comm([w16("mlp_w1", 1)])
    h2 = _mm_nn("mlp0_down", q0, w2[0], d, _ep_residual, [F32], extras=(h1,), comm=gather)[0]
    w1[1] = gather.out[0]

    n2 = _rms_fwd("od_norm", h2, od_g)
    gather = _gather_comm([w16("mlp_w2", 1)])
    pre = _mm_nn("od_in", n2, w_od_in, 2 * C_DIM, _ep_bias, [F32], bias=od_bin, comm=gather)[0]
    w2[1] = gather.out[0].reshape(D_FF, D_MODEL)

    y = _sgu_fwd(pre, lvg, lvb, ws, bs_b)
    h3 = _mm_nn("od_out", y, w_od_out, d, _ep_residual, [F32], extras=(h2,))[0]
    n3 = _rms_fwd("mlp1_norm", h3, mlp_g[1])
    q1 = _mm_nn("mlp1_up", n3, w1[1], D_FF, _ep_relu_sq, [BF16])[0]
    h4 = _mm_nn("mlp1_down", q1, w2[1], d, _ep_residual, [F32], extras=(h3,))[0]
    grad, grad16, d_fin_g, loss_part = _final_loss("final_loss", h4, fin_g, loss_target.reshape(t, d))
    loss = lax.psum(loss_part[0, 0], ("x", "y", "c"))

    by_chip = {}

    def swap(name, parts):
        comm = _pair_comm([parts])
        comm.parts, comm.weight = parts, name
        return comm

    def exchange(swapped):
        sums = _pair_sum(f"pair_sum_{swapped.weight}", swapped.parts, swapped.out[0], core)
        comm = _chip_comm([sums])
        comm.weight = swapped.weight
        return comm

    def done(comm):
        by_chip[comm.weight] = comm.out[0]

    dw2_1 = _mm_tn("mlp1_dw2", q1, grad16).reshape(N_DEV, D_FF // N_DEV, D_MODEL)
    s_a = swap("w2_1", dw2_1)
    dp = _mm_nt("mlp1_dq", grad16, w2[1], D_FF, _ep_relu_sq_grad, [BF16], extras=(q1,), comm=s_a)[0]
    c_a = exchange(s_a)
    dw1_1 = _mm_tn("mlp1_dw1", n3, dp, col_blocks=N_DEV, comm=c_a)
    done(c_a)
    s_b = swap("w1_1", dw1_1)
    dn = _mm_nt("mlp1_dn", dp, w1[1], d, _ep_store, [F32], comm=s_b)[0]
    c_b = exchange(s_b)
    grad, grad16, dg_mlp1 = _rms_bwd("mlp1_norm_bwd", dn, h3, mlp_g[1], grad)
    d_od_out = _mm_tn("od_dw_out", y, grad16).reshape(N_DEV, D_MODEL // N_DEV, D_MODEL)
    s_c = swap("od_out", d_od_out)
    dy = _mm_nt("od_dy", grad16, w_od_out, C_DIM, _ep_store, [F32], comm=s_c)[0]
    c_c = exchange(s_c)
    dpre, d_ws, d_bsb, d_lvg, d_lvb, d_bin = _sgu_bwd(pre, dy, lvg, lvb, ws, bs_b, comm=c_b)
    done(c_b)
    d_od_in = _mm_tn("od_dw_in", n2, dpre, col_blocks=N_DEV, comm=c_c)
    done(c_c)
    s_d = swap("od_in", d_od_in)
    dn2 = _mm_nt("od_dn", dpre, w_od_in, d, _ep_store, [F32], comm=s_d)[0]
    c_d = exchange(s_d)
    grad, grad16, d_od_g = _rms_bwd("od_norm_bwd", dn2, h2, od_g, grad)
    dw2_0 = _mm_tn("mlp0_dw2", q0, grad16, comm=c_d).reshape(N_DEV, D_FF // N_DEV, D_MODEL)
    done(c_d)
    s_e = swap("w2_0", dw2_0)
    dp = _mm_nt("mlp0_dq", grad16, w2[0], D_FF, _ep_relu_sq_grad, [BF16], extras=(q0,), comm=s_e)[0]
    c_e = exchange(s_e)
    dw1_0 = _mm_tn("mlp0_dw1", n1, dp, col_blocks=N_DEV, comm=c_e)
    done(c_e)
    s_f = swap("w1_0", dw1_0)
    dn = _mm_nt("mlp0_dn", dp, w1[0], d, _ep_store, [F32], comm=s_f)[0]
    c_f = exchange(s_f)
    grad, grad16, dg_mlp0 = _rms_bwd("mlp0_norm_bwd", dn, h1, mlp_g[0], grad)
    d_ev_out = _mm_tn("ev_dw_out", mix, grad16).reshape(N_DEV, D_MODEL // N_DEV, D_MODEL)
    s_g = swap("ev_out", d_ev_out)
    dmix = _mm_nt("ev_dmix", grad16, w_ev_out, A_DIM + B_DIM, _ep_store, [F32], comm=s_g)[0]
    c_g = exchange(s_g)
    da2, dcv, d_caw, d_cab, d_lag, d_lab, d_cbw = _mixer_bwd_local(z, dmix, seq, caw, cab, lag, lab, cbw, comm=c_f)
    done(c_f)
    dz = _mixer_bwd_input(z, dmix, da2, dcv, seq, caw, cbw, comm=c_g)
    done(c_g)
    d_ev_in = _mm_tn("ev_dw_in", n0, dz)
    d_ev_in = jnp.transpose(d_ev_in.reshape(D_MODEL, N_DEV, IN_EVEN // N_DEV), (1, 0, 2))
    s_h = swap("ev_in", d_ev_in)
    dn0 = _mm_nt("ev_dn", dz, w_ev_in, d, _ep_store, [F32], comm=s_h)[0]
    c_h = exchange(s_h)
    grad_x, _, d_ev_g = _rms_bwd("ev_norm_bwd", dn0, h0, ev_g, grad, comm=c_h)
    done(c_h)

    shard = {"ev_w_in": ("ev_in",), "ev_w_out": ("ev_out",), "od_w_in": ("od_in",), "od_w_out": ("od_out",),
             "mlp_w1": ("w1_0", "w1_1"), "mlp_w2": ("w2_0", "w2_1")}
    out_g, out_d, out_m, out_v = {}, {}, {}, {}
    for name, keys in shard.items():
        out_g[name], out_d[name], out_m[name], out_v[name] = _sum_adamw(
            f"adamw_{name}", [by_chip[key] for key in keys], W[name], M[name], V[name])

    small_parts = dict(
        ev_norm_g=d_ev_g, ev_conv_a_w=d_caw, ev_conv_a_b=d_cab, ev_ln_a_g=d_lag, ev_ln_a_b=d_lab,
        ev_conv_b_w=d_cbw, od_norm_g=d_od_g, od_b_in=d_bin, od_ln_v_g=d_lvg, od_ln_v_b=d_lvb,
        od_w_s=d_ws.reshape(C_GROUPS * CHUNK, CHUNK), od_b_s=d_bsb[:, :, 0],
        mlp_norm_g=jnp.concatenate([dg_mlp0, dg_mlp1], axis=0), final_norm_g=d_fin_g)
    every = _comm_alone("gather_small_grads", _gather_comm([small_parts[n] for n in SMALL]))
    mine = []
    for n, g in zip(SMALL, every):
        if n in SMALL_SHARDED:
            width = W[n].shape[-1]
            g = lax.dynamic_slice_in_dim(g, dev * width, width, axis=2)
        mine.append(g)
    res = _small_adamw("adamw_small", mine, [_rows(W[n]) for n in SMALL], [_rows(M[n]) for n in SMALL],
                       [_rows(V[n]) for n in SMALL])
    for store, values in zip((out_g, out_d, out_m, out_v), res):
        for n, value in zip(SMALL, values):
            store[n] = value.reshape(W[n].shape)

    return (loss, grad_x.reshape(n_seq, seq, d), *[out_g[n] for n in ORDER], *[out_d[n] for n in ORDER],
            *[out_m[n] for n in ORDER], *[out_v[n] for n in ORDER])
```

```python
import math

import jax
import jax.numpy as jnp
from jax import lax
from jax.experimental import pallas as pl
from jax.experimental.pallas import tpu as pltpu

F32 = jnp.float32
BF16 = jnp.bfloat16
MESH = pl.DeviceIdType.MESH

D_MODEL = 1024
A_DIM = 512
B_DIM = 512
IN_EVEN = 2 * A_DIM + 3 * B_DIM
A_TAPS = 31
B_TAPS = 3
CHUNK = 128
C_GROUPS = 8
C_DIM = 1024
D_FF = 4096
RMS_EPS = 1e-6
LN_EPS = 1e-5
N_DEV = 8
N_CHIP = 4

ADAM_LR = 0.001
ADAM_B1 = 0.9
ADAM_B2 = 0.999
ADAM_EPS = 1e-08
ADAM_WD = 0.01
ADAM_STEP = 10

A_HALO = 32
B_HALO = 8
VMEM_LIMIT_BYTES = 56 * 1024 * 1024
INV_SQRT2 = 1.0 / math.sqrt(2.0)
INV_SQRT_2PI = 1.0 / math.sqrt(2.0 * math.pi)
HBM_SPEC = pl.BlockSpec(memory_space=pltpu.HBM)


def _tile(n, want):
    t = min(n, want)
    while n % t:
        t //= 2
    return t


def _sigmoid(x):
    return 1.0 / (1.0 + jnp.exp(-x))


def _gelu(x):
    return 0.5 * x * (1.0 + lax.erf(x * INV_SQRT2))


def _gelu_grad(x):
    return 0.5 * (1.0 + lax.erf(x * INV_SQRT2)) + x * jnp.exp(-0.5 * x * x) * INV_SQRT_2PI


def _colsum(x):
    return jnp.sum(x, axis=0, keepdims=True)


class _Comm:
    def __init__(self, ins, out_shapes, sem_shapes, start, finish, middle=None):
        self.ins, self.out_shapes, self.sem_shapes, self.start, self.finish = ins, out_shapes, sem_shapes, start, finish
        self.middle = middle
        self.out = None


MIDDLE_AT = 0.75
FUSED_ROWS = 512


def _call(body, *, name, grid, in_specs, out_specs, out_shape, args, scratch_shapes=(), parallel=False, comm=None):
    if comm is None:
        sem = ("parallel" if parallel else "arbitrary",) * len(grid)
        return pl.pallas_call(
            body, name=name, grid=grid, in_specs=list(in_specs), out_specs=list(out_specs), out_shape=list(out_shape),
            scratch_shapes=list(scratch_shapes),
            compiler_params=pltpu.CompilerParams(dimension_semantics=sem, vmem_limit_bytes=VMEM_LIMIT_BYTES),
        )(*args)
    n_in, n_out, n_scr = len(in_specs), len(out_shape), len(scratch_shapes)
    c_in, c_out = len(comm.ins), len(comm.out_shapes)
    steps = grid
    total = math.prod(steps)
    first_step = (0,) * len(steps)
    last_step = tuple(s - 1 for s in steps)
    middle_step = None
    if comm.middle is not None and 0 < int(MIDDLE_AT * total) < total - 1:
        rest, idx = int(MIDDLE_AT * total), []
        for s in reversed(steps):
            idx.append(rest % s)
            rest //= s
        middle_step = tuple(reversed(idx))

    def carrying(*refs):
        pos = 0
        ins = refs[pos:pos + n_in]; pos += n_in
        c_ins = refs[pos:pos + c_in]; pos += c_in
        outs = refs[pos:pos + n_out]; pos += n_out
        c_outs = refs[pos:pos + c_out]; pos += c_out
        scr = refs[pos:pos + n_scr]; pos += n_scr
        c_sems = refs[pos:]
        def at(step):
            hit = pl.program_id(0) == step[0]
            for axis in range(1, len(steps)):
                hit = jnp.logical_and(hit, pl.program_id(axis) == step[axis])
            return hit

        @pl.when(at(first_step))
        def _():
            comm.start(c_ins, c_outs, c_sems)

        if middle_step is not None:
            @pl.when(at(middle_step))
            def _():
                comm.middle(c_ins, c_outs, c_sems)

        body(*ins, *outs, *scr)

        @pl.when(at(last_step))
        def _():
            if comm.middle is not None and middle_step is None:
                comm.middle(c_ins, c_outs, c_sems)
            comm.finish(c_ins, c_outs, c_sems)

    res = pl.pallas_call(
        carrying, name=name, grid=grid,
        in_specs=[*in_specs, *[HBM_SPEC] * c_in], out_specs=[*out_specs, *[HBM_SPEC] * c_out],
        out_shape=[*out_shape, *comm.out_shapes], scratch_shapes=[*scratch_shapes, *comm.sem_shapes],
        compiler_params=pltpu.CompilerParams(dimension_semantics=("arbitrary",) * len(grid), vmem_limit_bytes=VMEM_LIMIT_BYTES),
    )(*args, *comm.ins)
    comm.out = list(res[n_out:])
    return list(res[:n_out])


def _comm_alone(name, comm):
    c_in, c_out = len(comm.ins), len(comm.out_shapes)

    def body(*refs):
        ins, outs, sems = refs[:c_in], refs[c_in:c_in + c_out], refs[c_in + c_out:]
        comm.start(ins, outs, sems)
        if comm.middle is not None:
            comm.middle(ins, outs, sems)
        comm.finish(ins, outs, sems)

    comm.out = list(pl.pallas_call(
        body, name=name, in_specs=[HBM_SPEC] * c_in, out_specs=[HBM_SPEC] * c_out, out_shape=list(comm.out_shapes),
        scratch_shapes=list(comm.sem_shapes))(*comm.ins))
    return comm.out


def _place():
    x, y, c = lax.axis_index("x"), lax.axis_index("y"), lax.axis_index("c")
    return x, y, c, [(1 - x, y), (x, 1 - y), (1 - x, 1 - y)]


def _gather_comm(shards):
    nw = len(shards)

    def plan(ins, outs, sems):
        send_sems, recv_sems, local_sems = sems
        x, y, c, chips = _place()
        me, sibling = (x, y, c), (x, y, 1 - c)

        def slot(w, p):
            return outs[w].at[4 * p[0] + 2 * p[1] + p[2]]

        def copy(w, k, block, to, src=None):
            return pltpu.make_async_remote_copy(
                src_ref=slot(w, block) if src is None else src, dst_ref=slot(w, block),
                send_sem=send_sems.at[w, k], recv_sem=recv_sems.at[w, k], device_id=to, device_id_type=MESH)

        local = [pltpu.make_async_copy(ins[w], slot(w, me), local_sems.at[w]) for w in range(nw)]
        first = [[copy(w, 0, me, sibling, src=ins[w])] + [copy(w, 1 + j, me, (*chip, c), src=ins[w]) for j, chip in enumerate(chips)]
                 for w in range(nw)]
        landed = [[copy(w, 1 + j, (*chip, c), me) for j, chip in enumerate(chips)] for w in range(nw)]
        passed = [[copy(w, 4 + j, (*chip, c), sibling) for j, chip in enumerate(chips)] for w in range(nw)]
        from_sibling = [[copy(w, 0, sibling, me)] + [copy(w, 4 + j, (*chip, 1 - c), me) for j, chip in enumerate(chips)]
                        for w in range(nw)]
        return local, first, landed, passed, from_sibling

    def start(ins, outs, sems):
        local, first, _, _, _ = plan(ins, outs, sems)
        for cp in local:
            cp.start()
        for row in first:
            for cp in row:
                cp.start()

    def middle(ins, outs, sems):
        _, _, landed, passed, _ = plan(ins, outs, sems)
        for w in range(nw):
            for j in range(3):
                landed[w][j].wait_recv()
                passed[w][j].start()

    def finish(ins, outs, sems):
        local, first, landed, passed, from_sibling = plan(ins, outs, sems)
        for w in range(nw):
            for cp in from_sibling[w]:
                cp.wait_recv()
        for w in range(nw):
            for cp in first[w] + passed[w]:
                cp.wait_send()
        for cp in local:
            cp.wait()

    return _Comm(list(shards), [jax.ShapeDtypeStruct((N_DEV, *s.shape), s.dtype) for s in shards],
                 [pltpu.SemaphoreType.DMA((nw, 7)), pltpu.SemaphoreType.DMA((nw, 7)), pltpu.SemaphoreType.DMA((nw,))],
                 start, finish, middle)


def _pair_comm(parts):
    nw = len(parts)

    def plan(ins, outs, sems):
        send_sems, recv_sems = sems
        x, y, c, _ = _place()
        return [pltpu.make_async_remote_copy(
            src_ref=ins[w].at[2 * q + 1 - c], dst_ref=outs[w].at[q], send_sem=send_sems.at[w, q], recv_sem=recv_sems.at[w, q],
            device_id=(x, y, 1 - c), device_id_type=MESH) for w in range(nw) for q in range(N_CHIP)]

    def start(ins, outs, sems):
        for cp in plan(ins, outs, sems):
            cp.start()

    def finish(ins, outs, sems):
        for cp in plan(ins, outs, sems):
            cp.wait()

    return _Comm(list(parts), [jax.ShapeDtypeStruct((N_CHIP, *p.shape[1:]), p.dtype) for p in parts],
                 [pltpu.SemaphoreType.DMA((nw, N_CHIP)), pltpu.SemaphoreType.DMA((nw, N_CHIP))], start, finish)


def _chip_comm(sums):
    nw = len(sums)

    def plan(ins, outs, sems):
        send_sems, recv_sems, local_sems = sems
        x, y, c, chips = _place()
        my_chip = 2 * x + y
        local = [pltpu.make_async_copy(ins[w].at[my_chip], outs[w].at[my_chip], local_sems.at[w]) for w in range(nw)]
        remote = [pltpu.make_async_remote_copy(
            src_ref=ins[w].at[2 * chip[0] + chip[1]], dst_ref=outs[w].at[my_chip],
            send_sem=send_sems.at[w, j], recv_sem=recv_sems.at[w, j], device_id=(*chip, c), device_id_type=MESH)
            for w in range(nw) for j, chip in enumerate(chips)]
        return local, remote

    def start(ins, outs, sems):
        local, remote = plan(ins, outs, sems)
        for cp in local + remote:
            cp.start()

    def finish(ins, outs, sems):
        local, remote = plan(ins, outs, sems)
        for cp in remote + local:
            cp.wait()

    return _Comm(list(sums), [jax.ShapeDtypeStruct(s.shape, s.dtype) for s in sums],
                 [pltpu.SemaphoreType.DMA((nw, 3)), pltpu.SemaphoreType.DMA((nw, 3)), pltpu.SemaphoreType.DMA((nw,))],
                 start, finish)


def _matmul(name, a, b, *, kind, m, n, k, a_spec, b_spec, tm, tn, tk, out_shape, out_specs, epilogue,
            extras=(), extra_specs=(), comm=None):
    dims = {"nn": (((1,), (0,)), ((), ())), "nt": (((1,), (1,)), ((), ())), "tn": (((0,), (0,)), ((), ()))}[kind]
    nk = k // tk
    n_extra = len(extras)
    n_out = len(out_shape)

    def body(a_ref, b_ref, *rest):
        extra_refs = rest[:n_extra]
        out_refs = rest[n_extra:n_extra + n_out]
        part = lax.dot_general(a_ref[...], b_ref[...], dims, preferred_element_type=F32)
        if nk == 1:
            epilogue(part, extra_refs, out_refs)
            return
        acc_ref = rest[n_extra + n_out]
        step = pl.program_id(2)

        @pl.when(step == 0)
        def _():
            acc_ref[...] = part

        @pl.when(jnp.logical_and(step > 0, step < nk - 1))
        def _():
            acc_ref[...] += part

        @pl.when(step == nk - 1)
        def _():
            epilogue(acc_ref[...] + part, extra_refs, out_refs)

    return _call(
        body, name=name, grid=(m // tm, n // tn, nk), in_specs=[a_spec, b_spec, *extra_specs], out_specs=out_specs,
        out_shape=out_shape, scratch_shapes=[pltpu.VMEM((tm, tn), F32)] if nk > 1 else [], args=(a, b, *extras), comm=comm)


def _mm_operands(m, n, tm, tn, out_dtypes, extras, rows, sums):
    tile = pl.BlockSpec((tm, tn), lambda i, j, kk: (i, j))
    row = pl.BlockSpec((1, tn), lambda i, j, kk: (0, j))
    one = pl.BlockSpec((1, 1), lambda i, j, kk: (0, 0))
    out_shape = [jax.ShapeDtypeStruct((m, n), dt) for dt in out_dtypes]
    out_shape += [jax.ShapeDtypeStruct((1, n if wide else 1), F32) for wide in sums]
    out_specs = [tile] * len(out_dtypes) + [row if wide else one for wide in sums]
    return (*extras, *rows), [tile] * len(extras) + [row] * len(rows), out_shape, out_specs


def _mm_nn(name, a, b, n, epilogue, out_dtypes, *, extras=(), rows=(), sums=(), tm_want=1024, comm=None):
    m, k = a.shape
    blocked = b.ndim == 3
    tm = _tile(m, tm_want)
    tn = b.shape[-1] if blocked else _tile(n, 1024)
    tk = _tile(k, 1024)
    extras, extra_specs, out_shape, out_specs = _mm_operands(m, n, tm, tn, out_dtypes, extras, rows, sums)
    if blocked:
        b_spec = pl.BlockSpec((None, tk, tn), lambda i, j, kk: (j, kk, 0))
    else:
        b_spec = pl.BlockSpec((tk, tn), lambda i, j, kk: (kk, j))
    return _matmul(
        name, a, b, kind="nn", m=m, n=n, k=k, tm=tm, tn=tn, tk=tk,
        a_spec=pl.BlockSpec((tm, tk), lambda i, j, kk: (i, kk)), b_spec=b_spec, out_shape=out_shape, out_specs=out_specs,
        epilogue=epilogue, extras=extras, extra_specs=extra_specs, comm=comm)


def _mm_nt(name, a, b, n, epilogue, out_dtypes, *, extras=(), rows=(), sums=(), tm_want=1024, comm=None):
    m, k = a.shape
    blocked = b.ndim == 3
    tm = _tile(m, tm_want)
    tn = _tile(n, 1024)
    tk = b.shape[-1] if blocked else _tile(k, 1024)
    extras, extra_specs, out_shape, out_specs = _mm_operands(m, n, tm, tn, out_dtypes, extras, rows, sums)
    if blocked:
        b_spec = pl.BlockSpec((None, tn, tk), lambda i, j, kk: (kk, j, 0))
    else:
        b_spec = pl.BlockSpec((tn, tk), lambda i, j, kk: (j, kk))
    return _matmul(
        name, a, b, kind="nt", m=m, n=n, k=k, tm=tm, tn=tn, tk=tk,
        a_spec=pl.BlockSpec((tm, tk), lambda i, j, kk: (i, kk)), b_spec=b_spec, out_shape=out_shape, out_specs=out_specs,
        epilogue=epilogue, extras=extras, extra_specs=extra_specs, comm=comm)


def _mm_tn(name, a, b, *, col_blocks=0, comm=None):
    t, k = a.shape
    n = b.shape[1]
    tm = _tile(k, 1024)
    tn = n // col_blocks if col_blocks else _tile(n, 1024)
    tk = _tile(t, 1024)
    if col_blocks:
        out_shape = [jax.ShapeDtypeStruct((col_blocks, k, tn), F32)]
        out_specs = [pl.BlockSpec((None, tm, tn), lambda i, j, kk: (j, i, 0))]
    else:
        out_shape = [jax.ShapeDtypeStruct((k, n), F32)]
        out_specs = [pl.BlockSpec((tm, tn), lambda i, j, kk: (i, j))]

    def epilogue(acc, extra_refs, out_refs):
        out_refs[0][...] = acc

    return _matmul(
        name, a, b, kind="tn", m=k, n=n, k=t, tm=tm, tn=tn, tk=tk,
        a_spec=pl.BlockSpec((tk, tm), lambda i, j, kk: (kk, i)), b_spec=pl.BlockSpec((tk, tn), lambda i, j, kk: (kk, j)),
        out_shape=out_shape, out_specs=out_specs, epilogue=epilogue, comm=comm)[0]


def _ep_store(acc, extra_refs, out_refs):
    out_refs[0][...] = acc.astype(out_refs[0].dtype)


def _ep_bias(acc, extra_refs, out_refs):
    out_refs[0][...] = acc + extra_refs[0][...]


def _rms(x):
    r = lax.rsqrt(jnp.mean(x * x, axis=-1, keepdims=True) + RMS_EPS)
    return r, x * r


def _rms_grad(dn, r, xr, g):
    dy = dn * g
    return r * (dy - xr * jnp.mean(dy * xr, axis=-1, keepdims=True))


def _ep_residual_norm(acc, extra_refs, out_refs):
    h = extra_refs[0][...] + acc
    out_refs[0][...] = h
    _, xr = _rms(h)
    out_refs[1][...] = (xr * extra_refs[1][...]).astype(BF16)


def _ep_norm_grad(acc, extra_refs, out_refs):
    @pl.when(pl.program_id(0) == 0)
    def _():
        out_refs[2][...] = jnp.zeros_like(out_refs[2])

    r, xr = _rms(extra_refs[0][...])
    out_refs[2][...] += _colsum(acc * xr)
    out = extra_refs[1][...] + _rms_grad(acc, r, xr, extra_refs[2][...])
    out_refs[0][...] = out
    out_refs[1][...] = out.astype(BF16)


def _ep_final_loss(acc, extra_refs, out_refs):
    @pl.when(pl.program_id(0) == 0)
    def _():
        out_refs[2][...] = jnp.zeros_like(out_refs[2])
        out_refs[3][...] = jnp.zeros_like(out_refs[3])

    h = extra_refs[0][...] + acc
    g = extra_refs[2][...]
    r, xr = _rms(h)
    err = xr * g - extra_refs[1][...]
    out_refs[3][...] += 0.5 * jnp.sum(jnp.mean(err * err, axis=-1, keepdims=True), axis=0, keepdims=True)
    dout = err * (1.0 / h.shape[-1])
    out_refs[2][...] += _colsum(dout * xr)
    out = _rms_grad(dout, r, xr, g)
    out_refs[0][...] = out
    out_refs[1][...] = out.astype(BF16)


def _ep_relu_sq(acc, extra_refs, out_refs):
    r = jnp.maximum(acc, 0.0)
    out_refs[0][...] = (r * r).astype(BF16)


def _ep_relu_sq_grad(acc, extra_refs, out_refs):
    out_refs[0][...] = (acc * (2.0 * jnp.sqrt(extra_refs[0][...].astype(F32)))).astype(BF16)


def _rms_fwd(name, h, g, comm=None):
    t, d = h.shape
    tt = _tile(t, 512)

    def body(h_ref, g_ref, n_ref):
        x = h_ref[...]
        r = lax.rsqrt(jnp.mean(x * x, axis=-1, keepdims=True) + RMS_EPS)
        n_ref[...] = (x * r * g_ref[...]).astype(BF16)

    return _call(
        body, name=name, grid=(t // tt,),
        in_specs=[pl.BlockSpec((tt, d), lambda i: (i, 0)), pl.BlockSpec((1, d), lambda i: (0, 0))],
        out_specs=[pl.BlockSpec((tt, d), lambda i: (i, 0))], out_shape=[jax.ShapeDtypeStruct((t, d), BF16)],
        args=(h, g), parallel=True, comm=comm)[0]


def _mixer_windows(z_ref, zh_ref, first, a1_s, cb_s, tt):
    sig = _sigmoid(z_ref[:, A_DIM:2 * A_DIM])
    a1_s[A_HALO:A_HALO + tt, :] = z_ref[:, 0:A_DIM] * sig
    a1_h = zh_ref[:, 0:A_DIM] * _sigmoid(zh_ref[:, A_DIM:2 * A_DIM])
    a1_s[0:A_HALO, :] = jnp.where(first, 0.0, a1_h)
    cb_s[B_HALO:B_HALO + tt, :] = z_ref[:, 3 * A_DIM:4 * A_DIM] * z_ref[:, 4 * A_DIM:5 * A_DIM]
    cb_h = zh_ref[A_HALO - B_HALO:A_HALO, 3 * A_DIM:4 * A_DIM] * zh_ref[A_HALO - B_HALO:A_HALO, 4 * A_DIM:5 * A_DIM]
    cb_s[0:B_HALO, :] = jnp.where(first, 0.0, cb_h)
    return sig


def _causal_conv(win_s, w_ref, taps, halo, tt):
    base = halo - (taps - 1)
    acc = w_ref[0:1, :] * win_s[pl.ds(base, tt), :]
    for k in range(1, taps):
        acc = acc + w_ref[k:k + 1, :] * win_s[pl.ds(base + k, tt), :]
    return acc


SUBLANES = 8
LANE_BLOCK = 128
ROW_BLOCK = 128
SHIFT_ROWS = A_HALO - SUBLANES


def _shifted_copies(win_s, sh_s, tt):
    for b in range(1, SUBLANES):
        sh_s[b - 1] = win_s[pl.ds(b, tt + SHIFT_ROWS), :]


def _window_rows(win_s, sh_s, offset, rows, cols):
    b = offset % SUBLANES
    if b == 0:
        return win_s[pl.ds(offset, rows), cols]
    return sh_s[b - 1, pl.ds(offset - b, rows), cols]


def _blocks(tt):
    rb = min(tt, ROW_BLOCK)
    return rb, [(r, slice(lb * LANE_BLOCK, (lb + 1) * LANE_BLOCK))
                for lb in range(A_DIM // LANE_BLOCK) for r in range(0, tt, rb)]


def _conv_taps(win_s, sh_s, w_ref, offsets, out_s, tt, bias_ref=None):
    rb, blocks = _blocks(tt)
    for r, cols in blocks:
        acc = w_ref[0:1, cols] * _window_rows(win_s, sh_s, r + offsets[0], rb, cols)
        for k in range(1, len(offsets)):
            acc = acc + w_ref[k:k + 1, cols] * _window_rows(win_s, sh_s, r + offsets[k], rb, cols)
        out_s[r:r + rb, cols] = acc if bias_ref is None else acc + bias_ref[:, cols]


A_CAUSAL = [A_HALO - (A_TAPS - 1) + k for k in range(A_TAPS)]
A_ANTICAUSAL = [A_TAPS - 1 - k for k in range(A_TAPS)]


def _layer_norm_stats(x):
    mu = jnp.mean(x, axis=-1, keepdims=True)
    xc = x - mu
    rstd = lax.rsqrt(jnp.mean(xc * xc, axis=-1, keepdims=True) + LN_EPS)
    return xc * rstd, rstd


def _mixer_specs(seq, tt):
    tiles_per_seq = seq // tt
    halo_blocks = tt // A_HALO
    z_spec = pl.BlockSpec((tt, IN_EVEN), lambda i: (i, 0))
    zh_spec = pl.BlockSpec((A_HALO, IN_EVEN), lambda i: (jnp.maximum(i * halo_blocks - 1, 0), 0))
    return tiles_per_seq, z_spec, zh_spec


def _vec_spec(rows, cols):
    return pl.BlockSpec((rows, cols), lambda i: (0, 0))


def _mixer_fwd(z, seq, caw, cab, lag, lab, cbw, comm=None):
    t = z.shape[0]
    tt = _tile(seq, 256)
    tiles_per_seq, z_spec, zh_spec = _mixer_specs(seq, tt)

    def body(z_ref, zh_ref, caw_ref, cab_ref, lag_ref, lab_ref, cbw_ref, mix_ref, a1_s, cb_s, sh_s, a2_s):
        first = (pl.program_id(0) % tiles_per_seq) == 0
        _mixer_windows(z_ref, zh_ref, first, a1_s, cb_s, tt)
        _shifted_copies(a1_s, sh_s, tt)
        _conv_taps(a1_s, sh_s, caw_ref, A_CAUSAL, a2_s, tt, bias_ref=cab_ref)
        xhat, _ = _layer_norm_stats(a2_s[...])
        a3 = xhat * lag_ref[...] + lab_ref[...]
        mix_ref[:, 0:A_DIM] = (a3 * _sigmoid(a3)).astype(BF16)
        cv = _causal_conv(cb_s, cbw_ref, B_TAPS, B_HALO, tt)
        mix_ref[:, A_DIM:A_DIM + B_DIM] = (z_ref[:, 2 * A_DIM:3 * A_DIM] * cv).astype(BF16)

    return _call(
        body, name="mixer_fwd", grid=(t // tt,),
        in_specs=[z_spec, zh_spec, _vec_spec(A_TAPS, A_DIM), _vec_spec(1, A_DIM), _vec_spec(1, A_DIM), _vec_spec(1, A_DIM),
                  _vec_spec(B_TAPS, B_DIM)],
        out_specs=[pl.BlockSpec((tt, A_DIM + B_DIM), lambda i: (i, 0))],
        out_shape=[jax.ShapeDtypeStruct((t, A_DIM + B_DIM), BF16)],
        scratch_shapes=[pltpu.VMEM((A_HALO + tt, A_DIM), F32), pltpu.VMEM((B_HALO + tt, B_DIM), F32),
                        pltpu.VMEM((SUBLANES - 1, tt + SHIFT_ROWS, A_DIM), F32), pltpu.VMEM((tt, A_DIM), F32)],
        args=(z, z, caw, cab, lag, lab, cbw), parallel=True, comm=comm)[0]


def _mixer_bwd_local(z, dmix, seq, caw, cab, lag, lab, cbw, comm=None):
    t = z.shape[0]
    tt = _tile(seq, 256)
    tiles_per_seq, z_spec, zh_spec = _mixer_specs(seq, tt)

    def body(z_ref, zh_ref, dmix_ref, caw_ref, cab_ref, lag_ref, lab_ref, cbw_ref,
             da2_ref, dcv_ref, dcaw_ref, dcab_ref, dlag_ref, dlab_ref, dcbw_ref, a1_s, cb_s, sh_s, a2_s):
        @pl.when(pl.program_id(0) == 0)
        def _():
            for ref in (dcaw_ref, dcab_ref, dlag_ref, dlab_ref, dcbw_ref):
                ref[...] = jnp.zeros_like(ref)

        first = (pl.program_id(0) % tiles_per_seq) == 0
        _mixer_windows(z_ref, zh_ref, first, a1_s, cb_s, tt)
        _shifted_copies(a1_s, sh_s, tt)
        _conv_taps(a1_s, sh_s, caw_ref, A_CAUSAL, a2_s, tt, bias_ref=cab_ref)
        xhat, rstd = _layer_norm_stats(a2_s[...])
        a3 = xhat * lag_ref[...] + lab_ref[...]
        s3 = _sigmoid(a3)
        da3 = dmix_ref[:, 0:A_DIM] * (s3 * (1.0 + a3 * (1.0 - s3)))
        dlag_ref[...] += _colsum(da3 * xhat)
        dlab_ref[...] += _colsum(da3)
        dxh = da3 * lag_ref[...]
        da2 = rstd * (dxh - jnp.mean(dxh, axis=-1, keepdims=True) - xhat * jnp.mean(dxh * xhat, axis=-1, keepdims=True))
        da2_ref[...] = da2
        dcab_ref[...] += _colsum(da2)
        rb, blocks = _blocks(tt)
        for r, cols in blocks:
            da2_b = da2_ref[r:r + rb, cols]
            for k in range(A_TAPS):
                dcaw_ref[k:k + 1, cols] += _colsum(da2_b * _window_rows(a1_s, sh_s, r + A_CAUSAL[k], rb, cols))
        dcv =dmix_ref[:, A_DIM:A_DIM + B_DIM] * z_ref[:, 2 * A_DIM:3 * A_DIM]
        dcv_ref[...] = dcv
        for k in range(B_TAPS):
            dcbw_ref[k:k + 1, :] += _colsum(dcv * cb_s[pl.ds(B_HALO - (B_TAPS - 1) + k, tt), :])

    half = pl.BlockSpec((tt, A_DIM), lambda i: (i, 0))
    return _call(
        body, name="mixer_bwd_local", grid=(t // tt,),
        in_specs=[z_spec, zh_spec, pl.BlockSpec((tt, A_DIM + B_DIM), lambda i: (i, 0)),
                  _vec_spec(A_TAPS, A_DIM), _vec_spec(1, A_DIM), _vec_spec(1, A_DIM), _vec_spec(1, A_DIM), _vec_spec(B_TAPS, B_DIM)],
        out_specs=[half, half, _vec_spec(A_TAPS, A_DIM), _vec_spec(1, A_DIM), _vec_spec(1, A_DIM), _vec_spec(1, A_DIM),
                   _vec_spec(B_TAPS, B_DIM)],
        out_shape=[jax.ShapeDtypeStruct((t, A_DIM), F32), jax.ShapeDtypeStruct((t, B_DIM), F32),
                   jax.ShapeDtypeStruct((A_TAPS, A_DIM), F32), jax.ShapeDtypeStruct((1, A_DIM), F32),
                   jax.ShapeDtypeStruct((1, A_DIM), F32), jax.ShapeDtypeStruct((1, A_DIM), F32),
                   jax.ShapeDtypeStruct((B_TAPS, B_DIM), F32)],
        scratch_shapes=[pltpu.VMEM((A_HALO + tt, A_DIM), F32), pltpu.VMEM((B_HALO + tt, B_DIM), F32),
                        pltpu.VMEM((SUBLANES - 1, tt + SHIFT_ROWS, A_DIM), F32), pltpu.VMEM((tt, A_DIM), F32)],
        args=(z, z, dmix, caw, cab, lag, lab, cbw), comm=comm)


def _mixer_bwd_input(z, dmix, da2, dcv, seq, caw, cbw, comm=None):
    t = z.shape[0]
    tt = _tile(seq, 256)
    tiles_per_seq, z_spec, zh_spec = _mixer_specs(seq, tt)
    a_blocks = tt // A_HALO
    b_blocks = tt // B_HALO
    last_a = t // A_HALO - 1
    last_b = t // B_HALO - 1

    def body(z_ref, zh_ref, dmix_ref, da2_ref, da2n_ref, dcv_ref, dcvn_ref, caw_ref, cbw_ref, dz_ref, a1_s, cb_s, da2_s, dcv_s,
             sh_s, da1_s):
        pos = pl.program_id(0) % tiles_per_seq
        first = pos == 0
        last = pos == tiles_per_seq - 1
        sig = _mixer_windows(z_ref, zh_ref, first, a1_s, cb_s, tt)
        da2_s[0:tt, :] = da2_ref[...]
        da2_s[tt:tt + A_HALO, :] = jnp.where(last, 0.0, da2n_ref[...])
        dcv_s[0:tt, :] = dcv_ref[...]
        dcv_s[tt:tt + B_HALO, :] = jnp.where(last, 0.0, dcvn_ref[...])
        _shifted_copies(da2_s, sh_s, tt)
        _conv_taps(da2_s, sh_s, caw_ref, A_ANTICAUSAL, da1_s, tt)
        da1 = da1_s[...]
        dz_ref[:, 0:A_DIM] = (da1 * sig).astype(BF16)
        dz_ref[:, A_DIM:2 * A_DIM] = (da1 * z_ref[:, 0:A_DIM] * sig * (1.0 - sig)).astype(BF16)
        cv = _causal_conv(cb_s, cbw_ref, B_TAPS, B_HALO, tt)
        dz_ref[:, 2 * A_DIM:3 * A_DIM] = (dmix_ref[:, A_DIM:A_DIM + B_DIM] * cv).astype(BF16)
        dcb = cbw_ref[0:1, :] * dcv_s[pl.ds(B_TAPS - 1, tt), :]
        for k in range(1, B_TAPS):
            dcb = dcb + cbw_ref[k:k + 1, :] * dcv_s[pl.ds(B_TAPS - 1 - k, tt), :]
        dz_ref[:, 3 * A_DIM:4 * A_DIM] = (dcb * z_ref[:, 4 * A_DIM:5 * A_DIM]).astype(BF16)
        dz_ref[:, 4 * A_DIM:5 * A_DIM] = (dcb * z_ref[:, 3 * A_DIM:4 * A_DIM]).astype(BF16)

    half = pl.BlockSpec((tt, A_DIM), lambda i: (i, 0))
    return _call(
        body, name="mixer_bwd_input", grid=(t // tt,),
        in_specs=[z_spec, zh_spec, pl.BlockSpec((tt, A_DIM + B_DIM), lambda i: (i, 0)),
                  half, pl.BlockSpec((A_HALO, A_DIM), lambda i: (jnp.minimum((i + 1) * a_blocks, last_a), 0)),
                  half, pl.BlockSpec((B_HALO, B_DIM), lambda i: (jnp.minimum((i + 1) * b_blocks, last_b), 0)),
                  _vec_spec(A_TAPS, A_DIM), _vec_spec(B_TAPS, B_DIM)],
        out_specs=[pl.BlockSpec((tt, IN_EVEN), lambda i: (i, 0))],
        out_shape=[jax.ShapeDtypeStruct((t, IN_EVEN), BF16)],
        scratch_shapes=[pltpu.VMEM((A_HALO + tt, A_DIM), F32), pltpu.VMEM((B_HALO + tt, B_DIM), F32),
                        pltpu.VMEM((tt + A_HALO, A_DIM), F32), pltpu.VMEM((tt + B_HALO, B_DIM), F32),
                        pltpu.VMEM((SUBLANES - 1, tt + SHIFT_ROWS, A_DIM), F32), pltpu.VMEM((tt, A_DIM), F32)],
        args=(z, z, dmix, da2, da2, dcv, dcv, caw, cbw), parallel=True, comm=comm)[0]


def _tril_ws(ws_ref, g):
    rows = lax.broadcasted_iota(jnp.int32, (CHUNK, CHUNK), 0)
    cols = lax.broadcasted_iota(jnp.int32, (CHUNK, CHUNK), 1)
    return jnp.where(rows >= cols, ws_ref[g], 0.0).astype(BF16), rows >= cols


def _sgu_fwd(pre, lvg, lvb, ws, bs_b, comm=None):
    t = pre.shape[0]
    tt = _tile(t, 256)

    def body(pre_ref, lvg_ref, lvb_ref, ws_ref, bsb_ref, y_ref):
        vhat, _ = _layer_norm_stats(_gelu(pre_ref[:, C_DIM:2 * C_DIM]))
        vl = (vhat * lvg_ref[...] + lvb_ref[...]).astype(BF16)
        for g in range(C_GROUPS):
            w, _ = _tril_ws(ws_ref, g)
            cols = slice(g * CHUNK, (g + 1) * CHUNK)
            for ci in range(tt // CHUNK):
                rows = slice(ci * CHUNK, (ci + 1) * CHUNK)
                sv = jnp.dot(w, vl[rows, cols], preferred_element_type=F32) + bsb_ref[g]
                y_ref[rows, cols] = (_gelu(pre_ref[rows, cols]) * sv).astype(BF16)

    group = pl.BlockSpec((C_GROUPS, CHUNK, CHUNK), lambda i: (0, 0, 0))
    return _call(
        body, name="sgu_fwd", grid=(t // tt,),
        in_specs=[pl.BlockSpec((tt, 2 * C_DIM), lambda i: (i, 0)), _vec_spec(1, C_DIM), _vec_spec(1, C_DIM), group, group],
        out_specs=[pl.BlockSpec((tt, C_DIM), lambda i: (i, 0))], out_shape=[jax.ShapeDtypeStruct((t, C_DIM), BF16)],
        args=(pre, lvg, lvb, ws, bs_b), parallel=True, comm=comm)[0]


def _sgu_bwd(pre, dy, lvg, lvb, ws, bs_b, comm=None):
    t = pre.shape[0]
    tt = _tile(t, 256)

    def body(pre_ref, dy_ref, lvg_ref, lvb_ref, ws_ref, bsb_ref, dpre_ref, dws_ref, dbsb_ref, dlvg_ref, dlvb_ref, dbin_ref,
             dvl_s):
        @pl.when(pl.program_id(0) == 0)
        def _():
            for ref in (dws_ref, dbsb_ref, dlvg_ref, dlvb_ref, dbin_ref):
                ref[...] = jnp.zeros_like(ref)

        pre_v = pre_ref[:, C_DIM:2 * C_DIM]
        vhat, rstd = _layer_norm_stats(_gelu(pre_v))
        vl = (vhat * lvg_ref[...] + lvb_ref[...]).astype(BF16)
        for g in range(C_GROUPS):
            w, keep = _tril_ws(ws_ref, g)
            cols = slice(g * CHUNK, (g + 1) * CHUNK)
            dws = jnp.zeros((CHUNK, CHUNK), F32)
            dbs = jnp.zeros((CHUNK, 1), F32)
            for ci in range(tt // CHUNK):
                rows = slice(ci * CHUNK, (ci + 1) * CHUNK)
                vl_g = vl[rows, cols]
                sv = jnp.dot(w, vl_g, preferred_element_type=F32) + bsb_ref[g]
                pre_u = pre_ref[rows, cols]
                dyv = dy_ref[rows, cols]
                du = dyv * sv * _gelu_grad(pre_u)
                dpre_ref[rows, cols] = du.astype(BF16)
                dbin_ref[:, cols] += _colsum(du)
                dsv = dyv * _gelu(pre_u)
                dbs = dbs + jnp.sum(dsv, axis=1, keepdims=True)
                dsv16 = dsv.astype(BF16)
                dws = dws + lax.dot_general(dsv16, vl_g, (((1,), (1,)), ((), ())), preferred_element_type=F32)
                dvl_s[rows, cols] = lax.dot_general(w, dsv16, (((0,), (0,)), ((), ())), preferred_element_type=F32)
            dws_ref[g] += jnp.where(keep, dws, 0.0)
            dbsb_ref[g] += dbs
        dvl = dvl_s[...]
        dlvg_ref[...] += _colsum(dvl * vhat)
        dlvb_ref[...] += _colsum(dvl)
        dxh = dvl * lvg_ref[...]
        dv = rstd * (dxh - jnp.mean(dxh, axis=-1, keepdims=True) - vhat * jnp.mean(dxh * vhat, axis=-1, keepdims=True))
        dpv = dv * _gelu_grad(pre_v)
        dpre_ref[:, C_DIM:2 * C_DIM] = dpv.astype(BF16)
        dbin_ref[:, C_DIM:2 * C_DIM] += _colsum(dpv)

    group = pl.BlockSpec((C_GROUPS, CHUNK, CHUNK), lambda i: (0, 0, 0))
    return _call(
        body, name="sgu_bwd", grid=(t // tt,),
        in_specs=[pl.BlockSpec((tt, 2 * C_DIM), lambda i: (i, 0)), pl.BlockSpec((tt, C_DIM), lambda i: (i, 0)),
                  _vec_spec(1, C_DIM), _vec_spec(1, C_DIM), group, group],
        out_specs=[pl.BlockSpec((tt, 2 * C_DIM), lambda i: (i, 0)), group, group,
                   _vec_spec(1, C_DIM), _vec_spec(1, C_DIM), _vec_spec(1, 2 * C_DIM)],
        out_shape=[jax.ShapeDtypeStruct((t, 2 * C_DIM), BF16), jax.ShapeDtypeStruct((C_GROUPS, CHUNK, CHUNK), F32),
                   jax.ShapeDtypeStruct((C_GROUPS, CHUNK, CHUNK), F32), jax.ShapeDtypeStruct((1, C_DIM), F32),
                   jax.ShapeDtypeStruct((1, C_DIM), F32), jax.ShapeDtypeStruct((1, 2 * C_DIM), F32)],
        scratch_shapes=[pltpu.VMEM((tt, C_DIM), F32)],
        args=(pre, dy, lvg, lvb, ws, bs_b), comm=comm)


def _pair_sum(name, part, got, core):
    _, k, n = part.shape
    tk = _tile(k, 256)

    def body(core_ref, p_ref, s_ref, o_ref):
        o_ref[...] = (p_ref[...] + s_ref[...]).astype(BF16)

    return pl.pallas_call(
        body, name=name,
        grid_spec=pltpu.PrefetchScalarGridSpec(
            num_scalar_prefetch=1, grid=(N_CHIP, k // tk),
            in_specs=[pl.BlockSpec((None, tk, n), lambda q, i, core_ref: (2 * q + core_ref[0], i, 0)),
                      pl.BlockSpec((None, tk, n), lambda q, i, core_ref: (q, i, 0))],
            out_specs=pl.BlockSpec((None, tk, n), lambda q, i, core_ref: (q, i, 0))),
        out_shape=jax.ShapeDtypeStruct((N_CHIP, k, n), BF16),
        compiler_params=pltpu.CompilerParams(dimension_semantics=("parallel", "parallel"), vmem_limit_bytes=VMEM_LIMIT_BYTES),
    )(core, part, got)


def _adamw_math(w, g, m, v):
    m = ADAM_B1 * m + (1.0 - ADAM_B1) * g
    v = ADAM_B2 * v + (1.0 - ADAM_B2) * (g * g)
    m_hat = m / (1.0 - ADAM_B1 ** ADAM_STEP)
    v_hat = v / (1.0 - ADAM_B2 ** ADAM_STEP)
    delta = -ADAM_LR * (m_hat / (jnp.sqrt(v_hat) + ADAM_EPS) + ADAM_WD * w)
    return delta, m, v


def _sum_adamw(name, parts, w, m, v, comm=None):
    layers = len(parts)
    n_parts, k, n = parts[0].shape
    tk = _tile(k, 128)

    def body(*refs):
        p_refs = refs[:layers]
        w_ref, m_ref, v_ref, g_ref, d_ref, nm_ref, nv_ref = refs[layers:]

        def total(p_ref):
            g = p_ref[0].astype(F32)
            for q in range(1, n_parts):
                g = g + p_ref[q].astype(F32)
            return g

        g = total(p_refs[0])
        for l in range(1, layers):
            g = jnp.where(pl.program_id(0) == l, total(p_refs[l]), g)
        g_ref[...] = g
        d_ref[...], nm_ref[...], nv_ref[...] = _adamw_math(w_ref[...], g, m_ref[...], v_ref[...])

    blk = pl.BlockSpec((None, tk, n), lambda l, i: (l, i, 0))
    return _call(
        body, name=name, grid=(layers, k // tk),
        in_specs=[pl.BlockSpec((n_parts, tk, n), lambda l, i: (0, i, 0))] * layers + [blk, blk, blk], out_specs=[blk] * 4,
        out_shape=[jax.ShapeDtypeStruct((layers, k, n), F32)] * 4, args=(*parts, w, m, v), parallel=True, comm=comm)


def _small_adamw(name, parts, w, m, v):
    count = len(parts)

    def body(*refs):
        p_refs, w_refs, m_refs, v_refs = (refs[j * count:(j + 1) * count] for j in range(4))
        g_refs, d_refs, nm_refs, nv_refs = (refs[(4 + j) * count:(5 + j) * count] for j in range(4))
        for i in range(count):
            g = p_refs[i][0]
            for dev in range(1, N_DEV):
                g = g + p_refs[i][dev]
            g_refs[i][...] = g
            d_refs[i][...], nm_refs[i][...], nv_refs[i][...] = _adamw_math(w_refs[i][...], g, m_refs[i][...], v_refs[i][...])

    res = pl.pallas_call(
        body, name=name, out_shape=[jax.ShapeDtypeStruct(a.shape, F32) for a in w] * 4,
        compiler_params=pltpu.CompilerParams(vmem_limit_bytes=VMEM_LIMIT_BYTES))(*parts, *w, *m, *v)
    return [res[j * count:(j + 1) * count] for j in range(4)]


def _rows(a):
    return a.reshape(-1, a.shape[-1])


def _whole(gathered):
    return jnp.transpose(gathered, (1, 0, 2)).reshape(gathered.shape[1], -1)


SMALL =("ev_norm_g", "ev_conv_a_w", "ev_conv_a_b", "ev_ln_a_g", "ev_ln_a_b", "ev_conv_b_w", "od_norm_g", "od_b_in",
         "od_ln_v_g", "od_ln_v_b", "od_w_s", "od_b_s", "mlp_norm_g", "final_norm_g")
SMALL_SHARDED = ("ev_conv_a_w", "ev_conv_b_w", "od_norm_g", "od_b_in", "od_ln_v_g", "od_ln_v_b")
ORDER = ("ev_norm_g", "ev_w_in", "ev_conv_a_w", "ev_conv_a_b", "ev_ln_a_g", "ev_ln_a_b", "ev_conv_b_w", "ev_w_out",
         "od_norm_g", "od_w_in", "od_b_in", "od_ln_v_g", "od_ln_v_b", "od_w_s", "od_b_s", "od_w_out", "mlp_norm_g",
         "mlp_w1", "mlp_w2", "final_norm_g")


def kernel(x, ev_norm_g, ev_w_in, ev_conv_a_w, ev_conv_a_b, ev_ln_a_g, ev_ln_a_b, ev_conv_b_w, ev_w_out, od_norm_g, od_w_in, od_b_in, od_ln_v_g, od_ln_v_b, od_w_s, od_b_s, od_w_out, mlp_norm_g, mlp_w1, mlp_w2, final_norm_g, loss_target, m_ev_norm_g, m_ev_w_in, m_ev_conv_a_w, m_ev_conv_a_b, m_ev_ln_a_g, m_ev_ln_a_b, m_ev_conv_b_w, m_ev_w_out, m_od_norm_g, m_od_w_in, m_od_b_in, m_od_ln_v_g, m_od_ln_v_b, m_od_w_s, m_od_b_s, m_od_w_out, m_mlp_norm_g, m_mlp_w1, m_mlp_w2, m_final_norm_g, v_ev_norm_g, v_ev_w_in, v_ev_conv_a_w, v_ev_conv_a_b, v_ev_ln_a_g, v_ev_ln_a_b, v_ev_conv_b_w, v_ev_w_out, v_od_norm_g, v_od_w_in, v_od_b_in, v_od_ln_v_g, v_od_ln_v_b, v_od_w_s, v_od_b_s, v_od_w_out, v_mlp_norm_g, v_mlp_w1, v_mlp_w2, v_final_norm_g):
    W = dict(ev_norm_g=ev_norm_g, ev_w_in=ev_w_in, ev_conv_a_w=ev_conv_a_w, ev_conv_a_b=ev_conv_a_b, ev_ln_a_g=ev_ln_a_g,
             ev_ln_a_b=ev_ln_a_b, ev_conv_b_w=ev_conv_b_w, ev_w_out=ev_w_out, od_norm_g=od_norm_g, od_w_in=od_w_in,
             od_b_in=od_b_in, od_ln_v_g=od_ln_v_g, od_ln_v_b=od_ln_v_b, od_w_s=od_w_s, od_b_s=od_b_s, od_w_out=od_w_out,
             mlp_norm_g=mlp_norm_g, mlp_w1=mlp_w1, mlp_w2=mlp_w2, final_norm_g=final_norm_g)
    M = dict(ev_norm_g=m_ev_norm_g, ev_w_in=m_ev_w_in, ev_conv_a_w=m_ev_conv_a_w, ev_conv_a_b=m_ev_conv_a_b,
             ev_ln_a_g=m_ev_ln_a_g, ev_ln_a_b=m_ev_ln_a_b, ev_conv_b_w=m_ev_conv_b_w, ev_w_out=m_ev_w_out,
             od_norm_g=m_od_norm_g, od_w_in=m_od_w_in, od_b_in=m_od_b_in, od_ln_v_g=m_od_ln_v_g, od_ln_v_b=m_od_ln_v_b,
             od_w_s=m_od_w_s, od_b_s=m_od_b_s, od_w_out=m_od_w_out, mlp_norm_g=m_mlp_norm_g, mlp_w1=m_mlp_w1,
             mlp_w2=m_mlp_w2, final_norm_g=m_final_norm_g)
    V = dict(ev_norm_g=v_ev_norm_g, ev_w_in=v_ev_w_in, ev_conv_a_w=v_ev_conv_a_w, ev_conv_a_b=v_ev_conv_a_b,
             ev_ln_a_g=v_ev_ln_a_g, ev_ln_a_b=v_ev_ln_a_b, ev_conv_b_w=v_ev_conv_b_w, ev_w_out=v_ev_w_out,
             od_norm_g=v_od_norm_g, od_w_in=v_od_w_in, od_b_in=v_od_b_in, od_ln_v_g=v_od_ln_v_g, od_ln_v_b=v_od_ln_v_b,
             od_w_s=v_od_w_s, od_b_s=v_od_b_s, od_w_out=v_od_w_out, mlp_norm_g=v_mlp_norm_g, mlp_w1=v_mlp_w1,
             mlp_w2=v_mlp_w2, final_norm_g=v_final_norm_g)

    n_seq, seq, d = x.shape
    t = n_seq * seq
    dev = 4 * lax.axis_index("x") + 2 * lax.axis_index("y") + lax.axis_index("c")
    core = lax.axis_index("c").astype(jnp.int32).reshape(1)

    ev_g, cab, lag, lab = W["ev_norm_g"], W["ev_conv_a_b"], W["ev_ln_a_g"], W["ev_ln_a_b"]
    ws = W["od_w_s"][0]
    bs_b = jnp.broadcast_to(W["od_b_s"][0][:, :, None], (C_GROUPS, CHUNK, CHUNK))
    mlp_g = [W["mlp_norm_g"][l:l + 1] for l in range(2)]
    fin_g = W["final_norm_g"].reshape(1, d)

    def w16(name, l=0):
        return W[name][l].astype(BF16)

    h0 = x.reshape(t, d)
    gather = _gather_comm([w16("ev_w_in")] + [_rows(W[n]) for n in SMALL_SHARDED])
    n0 = _rms_fwd("ev_norm", h0, ev_g, comm=gather)
    w_ev_in, caw, cbw, od_g, od_bin, lvg, lvb = [_whole(g) for g in gather.out]

    gather = _gather_comm([w16("ev_w_out"), w16("od_w_in")])
    z = _mm_nn("ev_in", n0, w_ev_in, IN_EVEN, _ep_store, [F32], comm=gather)[0]
    w_ev_out = gather.out[0].reshape(D_MODEL, D_MODEL)
    w_od_in = gather.out[1]

    gather = _gather_comm([w16("mlp_w1", 0)])
    mix = _mixer_fwd(z, seq, caw, cab, lag, lab, cbw, comm=gather)
    w1 = [gather.out[0], None]

    gather = _gather_comm([w16("od_w_out")])
    h1, n1 = _mm_nn("ev_out", mix, w_ev_out, d, _ep_residual_norm, [F32, BF16], extras=(h0,), rows=(mlp_g[0],), comm=gather)
    w_od_out = gather.out[0].reshape(D_MODEL, D_MODEL)

    gather = _gather_comm([w16("mlp_w2", 0)])
    q0 = _mm_nn("mlp0_up", n1, w1[0], D_FF, _ep_relu_sq, [BF16], comm=gather)[0]
    w2 = [gather.out[0].reshape(D_FF, D_MODEL), None]

    gather = _gather_comm([w16("mlp_w1", 1)])
    h2, n2 = _mm_nn("mlp0_down", q0, w2[0], d, _ep_residual_norm, [F32, BF16], extras=(h1,), rows=(od_g,), comm=gather)
    w1[1] = gather.out[0]

    pre = _mm_nn("od_in", n2, w_od_in, 2 * C_DIM, _ep_bias, [F32], rows=(od_bin,))[0]
    y = _sgu_fwd(pre, lvg, lvb, ws, bs_b)
    h3, n3 = _mm_nn("od_out", y, w_od_out, d, _ep_residual_norm, [F32, BF16], extras=(h2,), rows=(mlp_g[1],))
    gather = _gather_comm([w16("mlp_w2", 1)])
    q1 = _mm_nn("mlp1_up", n3, w1[1], D_FF, _ep_relu_sq, [BF16], comm=gather)[0]
    w2[1] = gather.out[0].reshape(D_FF, D_MODEL)
    grad, grad16, d_fin_g, loss_part = _mm_nn(
        "mlp1_down", q1, w2[1], d, _ep_final_loss, [F32, BF16], extras=(h3, loss_target.reshape(t, d)), rows=(fin_g,),
        sums=(True, False), tm_want=FUSED_ROWS)
    loss = lax.psum(loss_part[0, 0], ("x", "y", "c"))

    by_chip = {}

    def swap(name, parts):
        comm = _pair_comm([parts])
        comm.parts, comm.weight = parts, name
        return comm

    def exchange(swapped):
        sums = _pair_sum(f"pair_sum_{swapped.weight}", swapped.parts, swapped.out[0], core)
        comm = _chip_comm([sums])
        comm.weight = swapped.weight
        return comm

    def done(comm):
        by_chip[comm.weight] = comm.out[0]

    def norm_grad(name, dz_, w, h, residual_grad, g, comm):
        return _mm_nt(name, dz_, w, d, _ep_norm_grad, [F32, BF16], extras=(h, residual_grad), rows=(g,), sums=(True,),
                      tm_want=FUSED_ROWS, comm=comm)

    dw2_1 = _mm_tn("mlp1_dw2", q1, grad16).reshape(N_DEV, D_FF // N_DEV, D_MODEL)
    s_a = swap("w2_1", dw2_1)
    dp = _mm_nt("mlp1_dq", grad16, w2[1], D_FF, _ep_relu_sq_grad, [BF16], extras=(q1,), comm=s_a)[0]
    c_a = exchange(s_a)
    dw1_1 = _mm_tn("mlp1_dw1", n3, dp, col_blocks=N_DEV, comm=c_a)
    done(c_a)
    s_b = swap("w1_1", dw1_1)
    grad, grad16, dg_mlp1 = norm_grad("mlp1_dn", dp, w1[1], h3, grad, mlp_g[1], s_b)
    c_b = exchange(s_b)
    d_od_out = _mm_tn("od_dw_out", y, grad16).reshape(N_DEV, D_MODEL // N_DEV, D_MODEL)
    s_c = swap("od_out", d_od_out)
    dy = _mm_nt("od_dy", grad16, w_od_out, C_DIM, _ep_store, [F32], comm=s_c)[0]
    c_c = exchange(s_c)
    dpre, d_ws, d_bsb, d_lvg, d_lvb, d_bin = _sgu_bwd(pre, dy, lvg, lvb, ws, bs_b, comm=c_c)
    done(c_c)
    d_od_in = _mm_tn("od_dw_in", n2, dpre, col_blocks=N_DEV, comm=c_b)
    done(c_b)
    s_d = swap("od_in", d_od_in)
    grad, grad16, d_od_g = norm_grad("od_dn", dpre, w_od_in, h2, grad, od_g, s_d)
    c_d = exchange(s_d)
    dw2_0 = _mm_tn("mlp0_dw2", q0, grad16, comm=c_d).reshape(N_DEV, D_FF // N_DEV, D_MODEL)
    done(c_d)
    s_e = swap("w2_0", dw2_0)
    dp = _mm_nt("mlp0_dq", grad16, w2[0], D_FF, _ep_relu_sq_grad, [BF16], extras=(q0,), comm=s_e)[0]
    c_e = exchange(s_e)
    dw1_0 = _mm_tn("mlp0_dw1", n1, dp, col_blocks=N_DEV, comm=c_e)
    done(c_e)
    s_f = swap("w1_0", dw1_0)
    grad, grad16, dg_mlp0 = norm_grad("mlp0_dn", dp, w1[0], h1, grad, mlp_g[0], s_f)
    c_f = exchange(s_f)
    d_ev_out = _mm_tn("ev_dw_out", mix, grad16).reshape(N_DEV, D_MODEL // N_DEV, D_MODEL)
    s_g = swap("ev_out", d_ev_out)
    dmix = _mm_nt("ev_dmix", grad16, w_ev_out, A_DIM + B_DIM, _ep_store, [F32], comm=s_g)[0]
    c_g = exchange(s_g)
    da2, dcv, d_caw, d_cab, d_lag, d_lab, d_cbw = _mixer_bwd_local(z, dmix, seq, caw, cab, lag, lab, cbw, comm=c_f)
    done(c_f)
    dz = _mixer_bwd_input(z, dmix, da2, dcv, seq, caw, cbw, comm=c_g)
    done(c_g)
    d_ev_in = _mm_tn("ev_dw_in", n0, dz)
    d_ev_in = jnp.transpose(d_ev_in.reshape(D_MODEL, N_DEV, IN_EVEN // N_DEV), (1, 0, 2))
    s_h = swap("ev_in", d_ev_in)
    grad_x, _, d_ev_g = norm_grad("ev_dn", dz, w_ev_in, h0, grad, ev_g, s_h)
    c_h = exchange(s_h)

    small_parts = dict(
        ev_norm_g=d_ev_g, ev_conv_a_w=d_caw, ev_conv_a_b=d_cab, ev_ln_a_g=d_lag, ev_ln_a_b=d_lab,
        ev_conv_b_w=d_cbw, od_norm_g=d_od_g, od_b_in=d_bin, od_ln_v_g=d_lvg, od_ln_v_b=d_lvb,
        od_w_s=d_ws.reshape(C_GROUPS * CHUNK, CHUNK), od_b_s=d_bsb[:, :, 0],
        mlp_norm_g=jnp.concatenate([dg_mlp0, dg_mlp1], axis=0), final_norm_g=d_fin_g)
    small_gather = _gather_comm([small_parts[n] for n in SMALL])
    shard = {"mlp_w1": ("w1_0", "w1_1"), "mlp_w2": ("w2_0", "w2_1"), "od_w_in": ("od_in",), "od_w_out": ("od_out",),
             "ev_w_out": ("ev_out",), "ev_w_in": ("ev_in",)}
    carried = {"mlp_w1": c_h, "mlp_w2": small_gather}
    out_g, out_d, out_m, out_v = {}, {}, {}, {}
    for name, keys in shard.items():
        out_g[name], out_d[name], out_m[name], out_v[name] = _sum_adamw(
            f"adamw_{name}", [by_chip[key] for key in keys], W[name], M[name], V[name], comm=carried.get(name))
        if name == "mlp_w1":
            done(c_h)

    mine = []
    for n, g in zip(SMALL, small_gather.out):
        if n in SMALL_SHARDED:
            width = W[n].shape[-1]
            g = lax.dynamic_slice_in_dim(g, dev * width, width, axis=2)
        mine.append(g)
    res = _small_adamw("adamw_small", mine, [_rows(W[n]) for n in SMALL], [_rows(M[n]) for n in SMALL],
                       [_rows(V[n]) for n in SMALL])
    for store, values in zip((out_g, out_d, out_m, out_v), res):
        for n, value in zip(SMALL, values):
            store[n] = value.reshape(W[n].shape)

    return (loss, grad_x.reshape(n_seq, seq, d), *[out_g[n] for n in ORDER], *[out_d[n] for n in ORDER],
            *[out_m[n] for n in ORDER], *[out_v[n] for n in ORDER])
```

```python
import math

import jax
import jax.numpy as jnp
from jax import lax
from jax.experimental import pallas as pl
from jax.experimental.pallas import tpu as pltpu

F32 = jnp.float32
BF16 = jnp.bfloat16
MESH = pl.DeviceIdType.MESH

D_MODEL = 1024
A_DIM = 512
B_DIM = 512
IN_EVEN = 2 * A_DIM + 3 * B_DIM
A_TAPS = 31
B_TAPS = 3
CHUNK = 128
C_GROUPS = 8
C_DIM = 1024
D_FF = 4096
RMS_EPS = 1e-6
LN_EPS = 1e-5
N_DEV = 8
N_CHIP = 4

ADAM_LR = 0.001
ADAM_B1 = 0.9
ADAM_B2 = 0.999
ADAM_EPS = 1e-08
ADAM_WD = 0.01
ADAM_STEP = 10

A_HALO = 32
B_HALO = 8
VMEM_LIMIT_BYTES = 56 * 1024 * 1024
INV_SQRT2 = 1.0 / math.sqrt(2.0)
INV_SQRT_2PI = 1.0 / math.sqrt(2.0 * math.pi)
HBM_SPEC = pl.BlockSpec(memory_space=pltpu.HBM)


def _tile(n, want):
    t = min(n, want)
    while n % t:
        t //= 2
    return t


def _sigmoid(x):
    return 1.0 / (1.0 + jnp.exp(-x))


def _gelu(x):
    return 0.5 * x * (1.0 + lax.erf(x * INV_SQRT2))


def _gelu_grad(x):
    return 0.5 * (1.0 + lax.erf(x * INV_SQRT2)) + x * jnp.exp(-0.5 * x * x) * INV_SQRT_2PI


def _colsum(x):
    return jnp.sum(x, axis=0, keepdims=True)


class _Comm:
    def __init__(self, ins, out_shapes, sem_shapes, start, finish, middle=None):
        self.ins, self.out_shapes, self.sem_shapes, self.start, self.finish = ins, out_shapes, sem_shapes, start, finish
        self.middle = middle
        self.out = None


MIDDLE_AT = 0.75
FUSED_ROWS = 512


def _call(body, *, name, grid, in_specs, out_specs, out_shape, args, scratch_shapes=(), parallel=False, comm=None):
    if comm is None:
        sem = ("parallel" if parallel else "arbitrary",) * len(grid)
        return pl.pallas_call(
            body, name=name, grid=grid, in_specs=list(in_specs), out_specs=list(out_specs), out_shape=list(out_shape),
            scratch_shapes=list(scratch_shapes),
            compiler_params=pltpu.CompilerParams(dimension_semantics=sem, vmem_limit_bytes=VMEM_LIMIT_BYTES),
        )(*args)
    n_in, n_out, n_scr = len(in_specs), len(out_shape), len(scratch_shapes)
    c_in, c_out = len(comm.ins), len(comm.out_shapes)
    steps = grid
    total = math.prod(steps)
    first_step = (0,) * len(steps)
    last_step = tuple(s - 1 for s in steps)
    middle_step = None
    if comm.middle is not None and 0 < int(MIDDLE_AT * total) < total - 1:
        rest, idx = int(MIDDLE_AT * total), []
        for s in reversed(steps):
            idx.append(rest % s)
            rest //= s
        middle_step = tuple(reversed(idx))

    def carrying(*refs):
        pos = 0
        ins = refs[pos:pos + n_in]; pos += n_in
        c_ins = refs[pos:pos + c_in]; pos += c_in
        outs = refs[pos:pos + n_out]; pos += n_out
        c_outs = refs[pos:pos + c_out]; pos += c_out
        scr = refs[pos:pos + n_scr]; pos += n_scr
        c_sems = refs[pos:]
        def at(step):
            hit = pl.program_id(0) == step[0]
            for axis in range(1, len(steps)):
                hit = jnp.logical_and(hit, pl.program_id(axis) == step[axis])
            return hit

        @pl.when(at(first_step))
        def _():
            comm.start(c_ins, c_outs, c_sems)

        if middle_step is not None:
            @pl.when(at(middle_step))
            def _():
                comm.middle(c_ins, c_outs, c_sems)

        body(*ins, *outs, *scr)

        @pl.when(at(last_step))
        def _():
            if comm.middle is not None and middle_step is None:
                comm.middle(c_ins, c_outs, c_sems)
            comm.finish(c_ins, c_outs, c_sems)

    res = pl.pallas_call(
        carrying, name=name, grid=grid,
        in_specs=[*in_specs, *[HBM_SPEC] * c_in], out_specs=[*out_specs, *[HBM_SPEC] * c_out],
        out_shape=[*out_shape, *comm.out_shapes], scratch_shapes=[*scratch_shapes, *comm.sem_shapes],
        compiler_params=pltpu.CompilerParams(dimension_semantics=("arbitrary",) * len(grid), vmem_limit_bytes=VMEM_LIMIT_BYTES),
    )(*args, *comm.ins)
    comm.out = list(res[n_out:])
    return list(res[:n_out])


def _comm_alone(name, comm):
    c_in, c_out = len(comm.ins), len(comm.out_shapes)

    def body(*refs):
        ins, outs, sems = refs[:c_in], refs[c_in:c_in + c_out], refs[c_in + c_out:]
        comm.start(ins, outs, sems)
        if comm.middle is not None:
            comm.middle(ins, outs, sems)
        comm.finish(ins, outs, sems)

    comm.out = list(pl.pallas_call(
        body, name=name, in_specs=[HBM_SPEC] * c_in, out_specs=[HBM_SPEC] * c_out, out_shape=list(comm.out_shapes),
        scratch_shapes=list(comm.sem_shapes))(*comm.ins))
    return comm.out


def _place():
    x, y, c = lax.axis_index("x"), lax.axis_index("y"), lax.axis_index("c")
    return x, y, c, [(1 - x, y), (x, 1 - y), (1 - x, 1 - y)]


def _gather_comm(shards):
    nw = len(shards)

    def plan(ins, outs, sems):
        send_sems, recv_sems, local_sems = sems
        x, y, c, chips = _place()
        me, sibling = (x, y, c), (x, y, 1 - c)

        def slot(w, p):
            return outs[w].at[4 * p[0] + 2 * p[1] + p[2]]

        def copy(w, k, block, to, src=None):
            return pltpu.make_async_remote_copy(
                src_ref=slot(w, block) if src is None else src, dst_ref=slot(w, block),
                send_sem=send_sems.at[w, k], recv_sem=recv_sems.at[w, k], device_id=to, device_id_type=MESH)

        local = [pltpu.make_async_copy(ins[w], slot(w, me), local_sems.at[w]) for w in range(nw)]
        first = [[copy(w, 0, me, sibling, src=ins[w])] + [copy(w, 1 + j, me, (*chip, c), src=ins[w]) for j, chip in enumerate(chips)]
                 for w in range(nw)]
        landed = [[copy(w, 1 + j, (*chip, c), me) for j, chip in enumerate(chips)] for w in range(nw)]
        passed = [[copy(w, 4 + j, (*chip, c), sibling) for j, chip in enumerate(chips)] for w in range(nw)]
        from_sibling = [[copy(w, 0, sibling, me)] + [copy(w, 4 + j, (*chip, 1 - c), me) for j, chip in enumerate(chips)]
                        for w in range(nw)]
        return local, first, landed, passed, from_sibling

    def start(ins, outs, sems):
        local, first, _, _, _ = plan(ins, outs, sems)
        for cp in local:
            cp.start()
        for row in first:
            for cp in row:
                cp.start()

    def middle(ins, outs, sems):
        _, _, landed, passed, _ = plan(ins, outs, sems)
        for w in range(nw):
            for j in range(3):
                landed[w][j].wait_recv()
                passed[w][j].start()

    def finish(ins, outs, sems):
        local, first, landed, passed, from_sibling = plan(ins, outs, sems)
        for w in range(nw):
            for cp in from_sibling[w]:
                cp.wait_recv()
        for w in range(nw):
            for cp in first[w] + passed[w]:
                cp.wait_send()
        for cp in local:
            cp.wait()

    return _Comm(list(shards), [jax.ShapeDtypeStruct((N_DEV, *s.shape), s.dtype) for s in shards],
                 [pltpu.SemaphoreType.DMA((nw, 7)), pltpu.SemaphoreType.DMA((nw, 7)), pltpu.SemaphoreType.DMA((nw,))],
                 start, finish, middle)


def _pair_comm(parts):
    nw = len(parts)

    def plan(ins, outs, sems):
        send_sems, recv_sems = sems
        x, y, c, _ = _place()
        return [pltpu.make_async_remote_copy(
            src_ref=ins[w].at[2 * q + 1 - c], dst_ref=outs[w].at[q], send_sem=send_sems.at[w, q], recv_sem=recv_sems.at[w, q],
            device_id=(x, y, 1 - c), device_id_type=MESH) for w in range(nw) for q in range(N_CHIP)]

    def start(ins, outs, sems):
        for cp in plan(ins, outs, sems):
            cp.start()

    def finish(ins, outs, sems):
        for cp in plan(ins, outs, sems):
            cp.wait()

    return _Comm(list(parts), [jax.ShapeDtypeStruct((N_CHIP, *p.shape[1:]), p.dtype) for p in parts],
                 [pltpu.SemaphoreType.DMA((nw, N_CHIP)), pltpu.SemaphoreType.DMA((nw, N_CHIP))], start, finish)


def _chip_comm(sums):
    nw = len(sums)

    def plan(ins, outs, sems):
        send_sems, recv_sems, local_sems = sems
        x, y, c, chips = _place()
        my_chip = 2 * x + y
        local = [pltpu.make_async_copy(ins[w].at[my_chip], outs[w].at[my_chip], local_sems.at[w]) for w in range(nw)]
        remote = [pltpu.make_async_remote_copy(
            src_ref=ins[w].at[2 * chip[0] + chip[1]], dst_ref=outs[w].at[my_chip],
            send_sem=send_sems.at[w, j], recv_sem=recv_sems.at[w, j], device_id=(*chip, c), device_id_type=MESH)
            for w in range(nw) for j, chip in enumerate(chips)]
        return local, remote

    def start(ins, outs, sems):
        local, remote = plan(ins, outs, sems)
        for cp in local + remote:
            cp.start()

    def finish(ins, outs, sems):
        local, remote = plan(ins, outs, sems)
        for cp in remote + local:
            cp.wait()

    return _Comm(list(sums), [jax.ShapeDtypeStruct(s.shape, s.dtype) for s in sums],
                 [pltpu.SemaphoreType.DMA((nw, 3)), pltpu.SemaphoreType.DMA((nw, 3)), pltpu.SemaphoreType.DMA((nw,))],
                 start, finish)


def _matmul(name, a, b, *, kind, m, n, k, a_spec, b_spec, tm, tn, tk, out_shape, out_specs, epilogue,
            extras=(), extra_specs=(), comm=None):
    dims = {"nn": (((1,), (0,)), ((), ())), "nt": (((1,), (1,)), ((), ())), "tn": (((0,), (0,)), ((), ()))}[kind]
    nk = k // tk
    n_extra = len(extras)
    n_out = len(out_shape)

    def body(a_ref, b_ref, *rest):
        extra_refs = rest[:n_extra]
        out_refs = rest[n_extra:n_extra + n_out]
        part = lax.dot_general(a_ref[...], b_ref[...], dims, preferred_element_type=F32)
        if nk == 1:
            epilogue(part, extra_refs, out_refs)
            return
        acc_ref = rest[n_extra + n_out]
        step = pl.program_id(2)

        @pl.when(step == 0)
        def _():
            acc_ref[...] = part

        @pl.when(jnp.logical_and(step > 0, step < nk - 1))
        def _():
            acc_ref[...] += part

        @pl.when(step == nk - 1)
        def _():
            epilogue(acc_ref[...] + part, extra_refs, out_refs)

    return _call(
        body, name=name, grid=(m // tm, n // tn, nk), in_specs=[a_spec, b_spec, *extra_specs], out_specs=out_specs,
        out_shape=out_shape, scratch_shapes=[pltpu.VMEM((tm, tn), F32)] if nk > 1 else [], args=(a, b, *extras), comm=comm)


def _mm_operands(m, n, tm, tn, out_dtypes, extras, rows, sums):
    tile = pl.BlockSpec((tm, tn), lambda i, j, kk: (i, j))
    row = pl.BlockSpec((1, tn), lambda i, j, kk: (0, j))
    one = pl.BlockSpec((1, 1), lambda i, j, kk: (0, 0))
    out_shape = [jax.ShapeDtypeStruct((m, n), dt) for dt in out_dtypes]
    out_shape += [jax.ShapeDtypeStruct((1, n if wide else 1), F32) for wide in sums]
    out_specs = [tile] * len(out_dtypes) + [row if wide else one for wide in sums]
    return (*extras, *rows), [tile] * len(extras) + [row] * len(rows), out_shape, out_specs


def _row_tile(m, k, tm_want):
    return _tile(m, tm_want or (1024 if k <= 1024 else 512))


def _mm_nn(name, a, b, n, epilogue, out_dtypes, *, extras=(), rows=(), sums=(), tm_want=None, comm=None):
    m, k = a.shape
    blocked = b.ndim == 3
    tm = _row_tile(m, k, tm_want)
    tn = b.shape[-1] if blocked else _tile(n, 1024)
    tk = k
    extras, extra_specs, out_shape, out_specs = _mm_operands(m, n, tm, tn, out_dtypes, extras, rows, sums)
    if blocked:
        b_spec = pl.BlockSpec((None, tk, tn), lambda i, j, kk: (j, kk, 0))
    else:
        b_spec = pl.BlockSpec((tk, tn), lambda i, j, kk: (kk, j))
    return _matmul(
        name, a, b, kind="nn", m=m, n=n, k=k, tm=tm, tn=tn, tk=tk,
        a_spec=pl.BlockSpec((tm, tk), lambda i, j, kk: (i, kk)), b_spec=b_spec, out_shape=out_shape, out_specs=out_specs,
        epilogue=epilogue, extras=extras, extra_specs=extra_specs, comm=comm)


def _mm_nt(name, a, b, n, epilogue, out_dtypes, *, extras=(), rows=(), sums=(), tm_want=None, comm=None):
    m, k = a.shape
    blocked = b.ndim == 3
    tm = _row_tile(m, k, tm_want)
    tn = _tile(n, 1024)
    tk = b.shape[-1] if blocked else k
    extras, extra_specs, out_shape, out_specs = _mm_operands(m, n, tm, tn, out_dtypes, extras, rows, sums)
    if blocked:
        b_spec = pl.BlockSpec((None, tn, tk), lambda i, j, kk: (kk, j, 0))
    else:
        b_spec = pl.BlockSpec((tn, tk), lambda i, j, kk: (j, kk))
    return _matmul(
        name, a, b, kind="nt", m=m, n=n, k=k, tm=tm, tn=tn, tk=tk,
        a_spec=pl.BlockSpec((tm, tk), lambda i, j, kk: (i, kk)), b_spec=b_spec, out_shape=out_shape, out_specs=out_specs,
        epilogue=epilogue, extras=extras, extra_specs=extra_specs, comm=comm)


def _mm_tn(name, a, b, *, col_blocks=0, comm=None):
    t, k = a.shape
    n = b.shape[1]
    tm = _row_tile(k, t, None)
    tn = n // col_blocks if col_blocks else _tile(n, 1024)
    tk = t
    if col_blocks:
        out_shape = [jax.ShapeDtypeStruct((col_blocks, k, tn), F32)]
        out_specs = [pl.BlockSpec((None, tm, tn), lambda i, j, kk: (j, i, 0))]
    else:
        out_shape = [jax.ShapeDtypeStruct((k, n), F32)]
        out_specs = [pl.BlockSpec((tm, tn), lambda i, j, kk: (i, j))]

    def epilogue(acc, extra_refs, out_refs):
        out_refs[0][...] = acc

    return _matmul(
        name, a, b, kind="tn", m=k, n=n, k=t, tm=tm, tn=tn, tk=tk,
        a_spec=pl.BlockSpec((tk, tm), lambda i, j, kk: (kk, i)), b_spec=pl.BlockSpec((tk, tn), lambda i, j, kk: (kk, j)),
        out_shape=out_shape, out_specs=out_specs, epilogue=epilogue, comm=comm)[0]


def _ep_store(acc, extra_refs, out_refs):
    out_refs[0][...] = acc.astype(out_refs[0].dtype)


def _ep_bias(acc, extra_refs, out_refs):
    out_refs[0][...] = acc + extra_refs[0][...]


def _rms(x):
    r = lax.rsqrt(jnp.mean(x * x, axis=-1, keepdims=True) + RMS_EPS)
    return r, x * r


def _rms_grad(dn, r, xr, g):
    dy = dn * g
    return r * (dy - xr * jnp.mean(dy * xr, axis=-1, keepdims=True))


def _ep_residual_norm(acc, extra_refs, out_refs):
    h = extra_refs[0][...] + acc
    out_refs[0][...] = h
    _, xr = _rms(h)
    out_refs[1][...] = (xr * extra_refs[1][...]).astype(BF16)


def _ep_final_loss(acc, extra_refs, out_refs):
    @pl.when(pl.program_id(0) == 0)
    def _():
        out_refs[2][...] = jnp.zeros_like(out_refs[2])
        out_refs[3][...] = jnp.zeros_like(out_refs[3])

    h = extra_refs[0][...] + acc
    g = extra_refs[2][...]
    r, xr = _rms(h)
    err = xr * g - extra_refs[1][...]
    out_refs[3][...] += 0.5 * jnp.sum(jnp.mean(err * err, axis=-1, keepdims=True), axis=0, keepdims=True)
    dout = err * (1.0 / h.shape[-1])
    out_refs[2][...] += _colsum(dout * xr)
    out = _rms_grad(dout, r, xr, g)
    out_refs[0][...] = out
    out_refs[1][...] = out.astype(BF16)


def _ep_relu_sq(acc, extra_refs, out_refs):
    r = jnp.maximum(acc, 0.0)
    out_refs[0][...] = (r * r).astype(BF16)


def _ep_relu_sq_grad(acc, extra_refs, out_refs):
    out_refs[0][...] = (acc * (2.0 * jnp.sqrt(extra_refs[0][...].astype(F32)))).astype(BF16)


def _rms_fwd(name, h, g, comm=None):
    t, d = h.shape
    tt = _tile(t, 512)

    def body(h_ref, g_ref, n_ref):
        x = h_ref[...]
        r = lax.rsqrt(jnp.mean(x * x, axis=-1, keepdims=True) + RMS_EPS)
        n_ref[...] = (x * r * g_ref[...]).astype(BF16)

    return _call(
        body, name=name, grid=(t // tt,),
        in_specs=[pl.BlockSpec((tt, d), lambda i: (i, 0)), pl.BlockSpec((1, d), lambda i: (0, 0))],
        out_specs=[pl.BlockSpec((tt, d), lambda i: (i, 0))], out_shape=[jax.ShapeDtypeStruct((t, d), BF16)],
        args=(h, g), parallel=True, comm=comm)[0]


def _rms_bwd(name, dn, h, g, grad_in, comm=None):
    t, d = h.shape
    tt = _tile(t, 512)

    def body(dn_ref, h_ref, g_ref, gin_ref, gout_ref, gout16_ref, dg_ref):
        @pl.when(pl.program_id(0) == 0)
        def _():
            dg_ref[...] = jnp.zeros_like(dg_ref)

        dnv = dn_ref[...]
        r, xr = _rms(h_ref[...])
        dg_ref[...] += _colsum(dnv * xr)
        out = gin_ref[...] + _rms_grad(dnv, r, xr, g_ref[...])
        gout_ref[...] = out
        gout16_ref[...] = out.astype(BF16)

    row = pl.BlockSpec((tt, d), lambda i: (i, 0))
    vec = pl.BlockSpec((1, d), lambda i: (0, 0))
    return _call(
        body, name=name, grid=(t // tt,), in_specs=[row, row, vec, row], out_specs=[row, row, vec],
        out_shape=[jax.ShapeDtypeStruct((t, d), F32), jax.ShapeDtypeStruct((t, d), BF16), jax.ShapeDtypeStruct((1, d), F32)],
        args=(dn, h, g, grad_in), comm=comm)


def _mixer_windows(z_ref, zh_ref, first, a1_s, cb_s, tt):
    sig = _sigmoid(z_ref[:, A_DIM:2 * A_DIM])
    a1_s[A_HALO:A_HALO + tt, :] = z_ref[:, 0:A_DIM] * sig
    a1_h = zh_ref[:, 0:A_DIM] * _sigmoid(zh_ref[:, A_DIM:2 * A_DIM])
    a1_s[0:A_HALO, :] = jnp.where(first, 0.0, a1_h)
    cb_s[B_HALO:B_HALO + tt, :] = z_ref[:, 3 * A_DIM:4 * A_DIM] * z_ref[:, 4 * A_DIM:5 * A_DIM]
    cb_h = zh_ref[A_HALO - B_HALO:A_HALO, 3 * A_DIM:4 * A_DIM] * zh_ref[A_HALO - B_HALO:A_HALO, 4 * A_DIM:5 * A_DIM]
    cb_s[0:B_HALO, :] = jnp.where(first, 0.0, cb_h)
    return sig


def _causal_conv(win_s, w_ref, taps, halo, tt):
    base = halo - (taps - 1)
    acc = w_ref[0:1, :] * win_s[pl.ds(base, tt), :]
    for k in range(1, taps):
        acc = acc + w_ref[k:k + 1, :] * win_s[pl.ds(base + k, tt), :]
    return acc


SUBLANES = 8
LANE_BLOCK = 128
ROW_BLOCK = 128
SHIFT_ROWS = A_HALO - SUBLANES


def _shifted_copies(win_s, sh_s, tt):
    for b in range(1, SUBLANES):
        sh_s[b - 1] = win_s[pl.ds(b, tt + SHIFT_ROWS), :]


def _window_rows(win_s, sh_s, offset, rows, cols):
    b = offset % SUBLANES
    if b == 0:
        return win_s[pl.ds(offset, rows), cols]
    return sh_s[b - 1, pl.ds(offset - b, rows), cols]


def _blocks(tt):
    rb = min(tt, ROW_BLOCK)
    return rb, [(r, slice(lb * LANE_BLOCK, (lb + 1) * LANE_BLOCK))
                for lb in range(A_DIM // LANE_BLOCK) for r in range(0, tt, rb)]


def _conv_taps(win_s, sh_s, w_ref, offsets, out_s, tt, bias_ref=None):
    rb, blocks = _blocks(tt)
    for r, cols in blocks:
        acc = w_ref[0:1, cols] * _window_rows(win_s, sh_s, r + offsets[0], rb, cols)
        for k in range(1, len(offsets)):
            acc = acc + w_ref[k:k + 1, cols] * _window_rows(win_s, sh_s, r + offsets[k], rb, cols)
        out_s[r:r + rb, cols] = acc if bias_ref is None else acc + bias_ref[:, cols]


A_CAUSAL = [A_HALO - (A_TAPS - 1) + k for k in range(A_TAPS)]
A_ANTICAUSAL = [A_TAPS - 1 - k for k in range(A_TAPS)]


def _layer_norm_stats(x):
    mu = jnp.mean(x, axis=-1, keepdims=True)
    xc = x - mu
    rstd = lax.rsqrt(jnp.mean(xc * xc, axis=-1, keepdims=True) + LN_EPS)
    return xc * rstd, rstd


def _mixer_specs(seq, tt):
    tiles_per_seq = seq // tt
    halo_blocks = tt // A_HALO
    z_spec = pl.BlockSpec((tt, IN_EVEN), lambda i: (i, 0))
    zh_spec = pl.BlockSpec((A_HALO, IN_EVEN), lambda i: (jnp.maximum(i * halo_blocks - 1, 0), 0))
    return tiles_per_seq, z_spec, zh_spec


def _vec_spec(rows, cols):
    return pl.BlockSpec((rows, cols), lambda i: (0, 0))


def _mixer_fwd(z, seq, caw, cab, lag, lab, cbw, comm=None):
    t = z.shape[0]
    tt = _tile(seq, 256)
    tiles_per_seq, z_spec, zh_spec = _mixer_specs(seq, tt)

    def body(z_ref, zh_ref, caw_ref, cab_ref, lag_ref, lab_ref, cbw_ref, mix_ref, a1_s, cb_s, sh_s, a2_s):
        first = (pl.program_id(0) % tiles_per_seq) == 0
        _mixer_windows(z_ref, zh_ref, first, a1_s, cb_s, tt)
        _shifted_copies(a1_s, sh_s, tt)
        _conv_taps(a1_s, sh_s, caw_ref, A_CAUSAL, a2_s, tt, bias_ref=cab_ref)
        xhat, _ = _layer_norm_stats(a2_s[...])
        a3 = xhat * lag_ref[...] + lab_ref[...]
        mix_ref[:, 0:A_DIM] = (a3 * _sigmoid(a3)).astype(BF16)
        cv = _causal_conv(cb_s, cbw_ref, B_TAPS, B_HALO, tt)
        mix_ref[:, A_DIM:A_DIM + B_DIM] = (z_ref[:, 2 * A_DIM:3 * A_DIM] * cv).astype(BF16)

    return _call(
        body, name="mixer_fwd", grid=(t // tt,),
        in_specs=[z_spec, zh_spec, _vec_spec(A_TAPS, A_DIM), _vec_spec(1, A_DIM), _vec_spec(1, A_DIM), _vec_spec(1, A_DIM),
                  _vec_spec(B_TAPS, B_DIM)],
        out_specs=[pl.BlockSpec((tt, A_DIM + B_DIM), lambda i: (i, 0))],
        out_shape=[jax.ShapeDtypeStruct((t, A_DIM + B_DIM), BF16)],
        scratch_shapes=[pltpu.VMEM((A_HALO + tt, A_DIM), F32), pltpu.VMEM((B_HALO + tt, B_DIM), F32),
                        pltpu.VMEM((SUBLANES - 1, tt + SHIFT_ROWS, A_DIM), F32), pltpu.VMEM((tt, A_DIM), F32)],
        args=(z, z, caw, cab, lag, lab, cbw), parallel=True, comm=comm)[0]


def _mixer_bwd_local(z, dmix, seq, caw, cab, lag, lab, cbw, comm=None):
    t = z.shape[0]
    tt = _tile(seq, 256)
    tiles_per_seq, z_spec, zh_spec = _mixer_specs(seq, tt)

    def body(z_ref, zh_ref, dmix_ref, caw_ref, cab_ref, lag_ref, lab_ref, cbw_ref,
             da2_ref, dcv_ref, dcaw_ref, dcab_ref, dlag_ref, dlab_ref, dcbw_ref, a1_s, cb_s, sh_s, a2_s):
        @pl.when(pl.program_id(0) == 0)
        def _():
            for ref in (dcaw_ref, dcab_ref, dlag_ref, dlab_ref, dcbw_ref):
                ref[...] = jnp.zeros_like(ref)

        first = (pl.program_id(0) % tiles_per_seq) == 0
        _mixer_windows(z_ref, zh_ref, first, a1_s, cb_s, tt)
        _shifted_copies(a1_s, sh_s, tt)
        _conv_taps(a1_s, sh_s, caw_ref, A_CAUSAL, a2_s, tt, bias_ref=cab_ref)
        xhat, rstd = _layer_norm_stats(a2_s[...])
        a3 = xhat * lag_ref[...] + lab_ref[...]
        s3 = _sigmoid(a3)
        da3 = dmix_ref[:, 0:A_DIM] * (s3 * (1.0 + a3 * (1.0 - s3)))
        dlag_ref[...] += _colsum(da3 * xhat)
        dlab_ref[...] += _colsum(da3)
        dxh = da3 * lag_ref[...]
        da2 = rstd * (dxh - jnp.mean(dxh, axis=-1, keepdims=True) - xhat * jnp.mean(dxh * xhat, axis=-1, keepdims=True))
        da2_ref[...] = da2
        dcab_ref[...] += _colsum(da2)
        rb, blocks = _blocks(tt)
        for r, cols in blocks:
            da2_b = da2_ref[r:r + rb, cols]
            for k in range(A_TAPS):
                dcaw_ref[k:k + 1, cols] += _colsum(da2_b * _window_rows(a1_s, sh_s, r + A_CAUSAL[k], rb, cols))
        dcv =dmix_ref[:, A_DIM:A_DIM + B_DIM] * z_ref[:, 2 * A_DIM:3 * A_DIM]
        dcv_ref[...] = dcv
        for k in range(B_TAPS):
            dcbw_ref[k:k + 1, :] += _colsum(dcv * cb_s[pl.ds(B_HALO - (B_TAPS - 1) + k, tt), :])

    half = pl.BlockSpec((tt, A_DIM), lambda i: (i, 0))
    return _call(
        body, name="mixer_bwd_local", grid=(t // tt,),
        in_specs=[z_spec, zh_spec, pl.BlockSpec((tt, A_DIM + B_DIM), lambda i: (i, 0)),
                  _vec_spec(A_TAPS, A_DIM), _vec_spec(1, A_DIM), _vec_spec(1, A_DIM), _vec_spec(1, A_DIM), _vec_spec(B_TAPS, B_DIM)],
        out_specs=[half, half, _vec_spec(A_TAPS, A_DIM), _vec_spec(1, A_DIM), _vec_spec(1, A_DIM), _vec_spec(1, A_DIM),
                   _vec_spec(B_TAPS, B_DIM)],
        out_shape=[jax.ShapeDtypeStruct((t, A_DIM), F32), jax.ShapeDtypeStruct((t, B_DIM), F32),
                   jax.ShapeDtypeStruct((A_TAPS, A_DIM), F32), jax.ShapeDtypeStruct((1, A_DIM), F32),
                   jax.ShapeDtypeStruct((1, A_DIM), F32), jax.ShapeDtypeStruct((1, A_DIM), F32),
                   jax.ShapeDtypeStruct((B_TAPS, B_DIM), F32)],
        scratch_shapes=[pltpu.VMEM((A_HALO + tt, A_DIM), F32), pltpu.VMEM((B_HALO + tt, B_DIM), F32),
                        pltpu.VMEM((SUBLANES - 1, tt + SHIFT_ROWS, A_DIM), F32), pltpu.VMEM((tt, A_DIM), F32)],
        args=(z, z, dmix, caw, cab, lag, lab, cbw), comm=comm)


def _mixer_bwd_input(z, dmix, da2, dcv, seq, caw, cbw, comm=None):
    t = z.shape[0]
    tt = _tile(seq, 256)
    tiles_per_seq, z_spec, zh_spec = _mixer_specs(seq, tt)
    a_blocks = tt // A_HALO
    b_blocks = tt // B_HALO
    last_a = t // A_HALO - 1
    last_b = t // B_HALO - 1

    def body(z_ref, zh_ref, dmix_ref, da2_ref, da2n_ref, dcv_ref, dcvn_ref, caw_ref, cbw_ref, dz_ref, a1_s, cb_s, da2_s, dcv_s,
             sh_s, da1_s):
        pos = pl.program_id(0) % tiles_per_seq
        first = pos == 0
        last = pos == tiles_per_seq - 1
        sig = _mixer_windows(z_ref, zh_ref, first, a1_s, cb_s, tt)
        da2_s[0:tt, :] = da2_ref[...]
        da2_s[tt:tt + A_HALO, :] = jnp.where(last, 0.0, da2n_ref[...])
        dcv_s[0:tt, :] = dcv_ref[...]
        dcv_s[tt:tt + B_HALO, :] = jnp.where(last, 0.0, dcvn_ref[...])
        _shifted_copies(da2_s, sh_s, tt)
        _conv_taps(da2_s, sh_s, caw_ref, A_ANTICAUSAL, da1_s, tt)
        da1 = da1_s[...]
        dz_ref[:, 0:A_DIM] = (da1 * sig).astype(BF16)
        dz_ref[:, A_DIM:2 * A_DIM] = (da1 * z_ref[:, 0:A_DIM] * sig * (1.0 - sig)).astype(BF16)
        cv = _causal_conv(cb_s, cbw_ref, B_TAPS, B_HALO, tt)
        dz_ref[:, 2 * A_DIM:3 * A_DIM] = (dmix_ref[:, A_DIM:A_DIM + B_DIM] * cv).astype(BF16)
        dcb = cbw_ref[0:1, :] * dcv_s[pl.ds(B_TAPS - 1, tt), :]
        for k in range(1, B_TAPS):
            dcb = dcb + cbw_ref[k:k + 1, :] * dcv_s[pl.ds(B_TAPS - 1 - k, tt), :]
        dz_ref[:, 3 * A_DIM:4 * A_DIM] = (dcb * z_ref[:, 4 * A_DIM:5 * A_DIM]).astype(BF16)
        dz_ref[:, 4 * A_DIM:5 * A_DIM] = (dcb * z_ref[:, 3 * A_DIM:4 * A_DIM]).astype(BF16)

    half = pl.BlockSpec((tt, A_DIM), lambda i: (i, 0))
    return _call(
        body, name="mixer_bwd_input", grid=(t // tt,),
        in_specs=[z_spec, zh_spec, pl.BlockSpec((tt, A_DIM + B_DIM), lambda i: (i, 0)),
                  half, pl.BlockSpec((A_HALO, A_DIM), lambda i: (jnp.minimum((i + 1) * a_blocks, last_a), 0)),
                  half, pl.BlockSpec((B_HALO, B_DIM), lambda i: (jnp.minimum((i + 1) * b_blocks, last_b), 0)),
                  _vec_spec(A_TAPS, A_DIM), _vec_spec(B_TAPS, B_DIM)],
        out_specs=[pl.BlockSpec((tt, IN_EVEN), lambda i: (i, 0))],
        out_shape=[jax.ShapeDtypeStruct((t, IN_EVEN), BF16)],
        scratch_shapes=[pltpu.VMEM((A_HALO + tt, A_DIM), F32), pltpu.VMEM((B_HALO + tt, B_DIM), F32),
                        pltpu.VMEM((tt + A_HALO, A_DIM), F32), pltpu.VMEM((tt + B_HALO, B_DIM), F32),
                        pltpu.VMEM((SUBLANES - 1, tt + SHIFT_ROWS, A_DIM), F32), pltpu.VMEM((tt, A_DIM), F32)],
        args=(z, z, dmix, da2, da2, dcv, dcv, caw, cbw), parallel=True, comm=comm)[0]


def _tril_ws(ws_ref, g):
    rows = lax.broadcasted_iota(jnp.int32, (CHUNK, CHUNK), 0)
    cols = lax.broadcasted_iota(jnp.int32, (CHUNK, CHUNK), 1)
    return jnp.where(rows >= cols, ws_ref[g], 0.0).astype(BF16), rows >= cols


def _sgu_fwd(pre, lvg, lvb, ws, bs_b, comm=None):
    t = pre.shape[0]
    tt = _tile(t, 256)

    def body(pre_ref, lvg_ref, lvb_ref, ws_ref, bsb_ref, y_ref):
        vhat, _ = _layer_norm_stats(_gelu(pre_ref[:, C_DIM:2 * C_DIM]))
        vl = (vhat * lvg_ref[...] + lvb_ref[...]).astype(BF16)
        for g in range(C_GROUPS):
            w, _ = _tril_ws(ws_ref, g)
            cols = slice(g * CHUNK, (g + 1) * CHUNK)
            for ci in range(tt // CHUNK):
                rows = slice(ci * CHUNK, (ci + 1) * CHUNK)
                sv = jnp.dot(w, vl[rows, cols], preferred_element_type=F32) + bsb_ref[g]
                y_ref[rows, cols] = (_gelu(pre_ref[rows, cols]) * sv).astype(BF16)

    group = pl.BlockSpec((C_GROUPS, CHUNK, CHUNK), lambda i: (0, 0, 0))
    return _call(
        body, name="sgu_fwd", grid=(t // tt,),
        in_specs=[pl.BlockSpec((tt, 2 * C_DIM), lambda i: (i, 0)), _vec_spec(1, C_DIM), _vec_spec(1, C_DIM), group, group],
        out_specs=[pl.BlockSpec((tt, C_DIM), lambda i: (i, 0))], out_shape=[jax.ShapeDtypeStruct((t, C_DIM), BF16)],
        args=(pre, lvg, lvb, ws, bs_b), parallel=True, comm=comm)[0]


def _sgu_bwd(pre, dy, lvg, lvb, ws, bs_b, comm=None):
    t = pre.shape[0]
    tt = _tile(t, 256)

    def body(pre_ref, dy_ref, lvg_ref, lvb_ref, ws_ref, bsb_ref, dpre_ref, dws_ref, dbsb_ref, dlvg_ref, dlvb_ref, dbin_ref,
             dvl_s):
        @pl.when(pl.program_id(0) == 0)
        def _():
            for ref in (dws_ref, dbsb_ref, dlvg_ref, dlvb_ref, dbin_ref):
                ref[...] = jnp.zeros_like(ref)

        pre_v = pre_ref[:, C_DIM:2 * C_DIM]
        vhat, rstd = _layer_norm_stats(_gelu(pre_v))
        vl = (vhat * lvg_ref[...] + lvb_ref[...]).astype(BF16)
        for g in range(C_GROUPS):
            w, keep = _tril_ws(ws_ref, g)
            cols = slice(g * CHUNK, (g + 1) * CHUNK)
            dws = jnp.zeros((CHUNK, CHUNK), F32)
            dbs = jnp.zeros((CHUNK, 1), F32)
            for ci in range(tt // CHUNK):
                rows = slice(ci * CHUNK, (ci + 1) * CHUNK)
                vl_g = vl[rows, cols]
                sv = jnp.dot(w, vl_g, preferred_element_type=F32) + bsb_ref[g]
                pre_u = pre_ref[rows, cols]
                dyv = dy_ref[rows, cols]
                du = dyv * sv * _gelu_grad(pre_u)
                dpre_ref[rows, cols] = du.astype(BF16)
                dbin_ref[:, cols] += _colsum(du)
                dsv = dyv * _gelu(pre_u)
                dbs = dbs + jnp.sum(dsv, axis=1, keepdims=True)
                dsv16 = dsv.astype(BF16)
                dws = dws + lax.dot_general(dsv16, vl_g, (((1,), (1,)), ((), ())), preferred_element_type=F32)
                dvl_s[rows, cols] = lax.dot_general(w, dsv16, (((0,), (0,)), ((), ())), preferred_element_type=F32)
            dws_ref[g] += jnp.where(keep, dws, 0.0)
            dbsb_ref[g] += dbs
        dvl = dvl_s[...]
        dlvg_ref[...] += _colsum(dvl * vhat)
        dlvb_ref[...] += _colsum(dvl)
        dxh = dvl * lvg_ref[...]
        dv = rstd * (dxh - jnp.mean(dxh, axis=-1, keepdims=True) - vhat * jnp.mean(dxh * vhat, axis=-1, keepdims=True))
        dpv = dv * _gelu_grad(pre_v)
        dpre_ref[:, C_DIM:2 * C_DIM] = dpv.astype(BF16)
        dbin_ref[:, C_DIM:2 * C_DIM] += _colsum(dpv)

    group = pl.BlockSpec((C_GROUPS, CHUNK, CHUNK), lambda i: (0, 0, 0))
    return _call(
        body, name="sgu_bwd", grid=(t // tt,),
        in_specs=[pl.BlockSpec((tt, 2 * C_DIM), lambda i: (i, 0)), pl.BlockSpec((tt, C_DIM), lambda i: (i, 0)),
                  _vec_spec(1, C_DIM), _vec_spec(1, C_DIM), group, group],
        out_specs=[pl.BlockSpec((tt, 2 * C_DIM), lambda i: (i, 0)), group, group,
                   _vec_spec(1, C_DIM), _vec_spec(1, C_DIM), _vec_spec(1, 2 * C_DIM)],
        out_shape=[jax.ShapeDtypeStruct((t, 2 * C_DIM), BF16), jax.ShapeDtypeStruct((C_GROUPS, CHUNK, CHUNK), F32),
                   jax.ShapeDtypeStruct((C_GROUPS, CHUNK, CHUNK), F32), jax.ShapeDtypeStruct((1, C_DIM), F32),
                   jax.ShapeDtypeStruct((1, C_DIM), F32), jax.ShapeDtypeStruct((1, 2 * C_DIM), F32)],
        scratch_shapes=[pltpu.VMEM((tt, C_DIM), F32)],
        args=(pre, dy, lvg, lvb, ws, bs_b), comm=comm)


def _pair_sum(name, part, got, core):
    _, k, n = part.shape
    tk = _tile(k, 256)

    def body(core_ref, p_ref, s_ref, o_ref):
        o_ref[...] = (p_ref[...] + s_ref[...]).astype(BF16)

    return pl.pallas_call(
        body, name=name,
        grid_spec=pltpu.PrefetchScalarGridSpec(
            num_scalar_prefetch=1, grid=(N_CHIP, k // tk),
            in_specs=[pl.BlockSpec((None, tk, n), lambda q, i, core_ref: (2 * q + core_ref[0], i, 0)),
                      pl.BlockSpec((None, tk, n), lambda q, i, core_ref: (q, i, 0))],
            out_specs=pl.BlockSpec((None, tk, n), lambda q, i, core_ref: (q, i, 0))),
        out_shape=jax.ShapeDtypeStruct((N_CHIP, k, n), BF16),
        compiler_params=pltpu.CompilerParams(dimension_semantics=("parallel", "parallel"), vmem_limit_bytes=VMEM_LIMIT_BYTES),
    )(core, part, got)


def _adamw_math(w, g, m, v):
    m = ADAM_B1 * m + (1.0 - ADAM_B1) * g
    v = ADAM_B2 * v + (1.0 - ADAM_B2) * (g * g)
    m_hat = m / (1.0 - ADAM_B1 ** ADAM_STEP)
    v_hat = v / (1.0 - ADAM_B2 ** ADAM_STEP)
    delta = -ADAM_LR * (m_hat / (jnp.sqrt(v_hat) + ADAM_EPS) + ADAM_WD * w)
    return delta, m, v


def _sum_adamw(name, parts, w, m, v, comm=None):
    layers = len(parts)
    n_parts, k, n = parts[0].shape
    tk = _tile(k, 128)

    def body(*refs):
        p_refs = refs[:layers]
        w_ref, m_ref, v_ref, g_ref, d_ref, nm_ref, nv_ref = refs[layers:]

        def total(p_ref):
            g = p_ref[0].astype(F32)
            for q in range(1, n_parts):
                g = g + p_ref[q].astype(F32)
            return g

        g = total(p_refs[0])
        for l in range(1, layers):
            g = jnp.where(pl.program_id(0) == l, total(p_refs[l]), g)
        g_ref[...] = g
        d_ref[...], nm_ref[...], nv_ref[...] = _adamw_math(w_ref[...], g, m_ref[...], v_ref[...])

    blk = pl.BlockSpec((None, tk, n), lambda l, i: (l, i, 0))
    return _call(
        body, name=name, grid=(layers, k // tk),
        in_specs=[pl.BlockSpec((n_parts, tk, n), lambda l, i: (0, i, 0))] * layers + [blk, blk, blk], out_specs=[blk] * 4,
        out_shape=[jax.ShapeDtypeStruct((layers, k, n), F32)] * 4, args=(*parts, w, m, v), parallel=True, comm=comm)


def _small_adamw(name, parts, w, m, v):
    count = len(parts)

    def body(*refs):
        p_refs, w_refs, m_refs, v_refs = (refs[j * count:(j + 1) * count] for j in range(4))
        g_refs, d_refs, nm_refs, nv_refs = (refs[(4 + j) * count:(5 + j) * count] for j in range(4))
        for i in range(count):
            g = p_refs[i][0]
            for dev in range(1, N_DEV):
                g = g + p_refs[i][dev]
            g_refs[i][...] = g
            d_refs[i][...], nm_refs[i][...], nv_refs[i][...] = _adamw_math(w_refs[i][...], g, m_refs[i][...], v_refs[i][...])

    res = pl.pallas_call(
        body, name=name, out_shape=[jax.ShapeDtypeStruct(a.shape, F32) for a in w] * 4,
        compiler_params=pltpu.CompilerParams(vmem_limit_bytes=VMEM_LIMIT_BYTES))(*parts, *w, *m, *v)
    return [res[j * count:(j + 1) * count] for j in range(4)]


def _rows(a):
    return a.reshape(-1, a.shape[-1])


def _whole(gathered):
    return jnp.transpose(gathered, (1, 0, 2)).reshape(gathered.shape[1], -1)


SMALL =("ev_norm_g", "ev_conv_a_w", "ev_conv_a_b", "ev_ln_a_g", "ev_ln_a_b", "ev_conv_b_w", "od_norm_g", "od_b_in",
         "od_ln_v_g", "od_ln_v_b", "od_w_s", "od_b_s", "mlp_norm_g", "final_norm_g")
SMALL_SHARDED = ("ev_conv_a_w", "ev_conv_b_w", "od_norm_g", "od_b_in", "od_ln_v_g", "od_ln_v_b")
ORDER = ("ev_norm_g", "ev_w_in", "ev_conv_a_w", "ev_conv_a_b", "ev_ln_a_g", "ev_ln_a_b", "ev_conv_b_w", "ev_w_out",
         "od_norm_g", "od_w_in", "od_b_in", "od_ln_v_g", "od_ln_v_b", "od_w_s", "od_b_s", "od_w_out", "mlp_norm_g",
         "mlp_w1", "mlp_w2", "final_norm_g")


def kernel(x, ev_norm_g, ev_w_in, ev_conv_a_w, ev_conv_a_b, ev_ln_a_g, ev_ln_a_b, ev_conv_b_w, ev_w_out, od_norm_g, od_w_in, od_b_in, od_ln_v_g, od_ln_v_b, od_w_s, od_b_s, od_w_out, mlp_norm_g, mlp_w1, mlp_w2, final_norm_g, loss_target, m_ev_norm_g, m_ev_w_in, m_ev_conv_a_w, m_ev_conv_a_b, m_ev_ln_a_g, m_ev_ln_a_b, m_ev_conv_b_w, m_ev_w_out, m_od_norm_g, m_od_w_in, m_od_b_in, m_od_ln_v_g, m_od_ln_v_b, m_od_w_s, m_od_b_s, m_od_w_out, m_mlp_norm_g, m_mlp_w1, m_mlp_w2, m_final_norm_g, v_ev_norm_g, v_ev_w_in, v_ev_conv_a_w, v_ev_conv_a_b, v_ev_ln_a_g, v_ev_ln_a_b, v_ev_conv_b_w, v_ev_w_out, v_od_norm_g, v_od_w_in, v_od_b_in, v_od_ln_v_g, v_od_ln_v_b, v_od_w_s, v_od_b_s, v_od_w_out, v_mlp_norm_g, v_mlp_w1, v_mlp_w2, v_final_norm_g):
    W = dict(ev_norm_g=ev_norm_g, ev_w_in=ev_w_in, ev_conv_a_w=ev_conv_a_w, ev_conv_a_b=ev_conv_a_b, ev_ln_a_g=ev_ln_a_g,
             ev_ln_a_b=ev_ln_a_b, ev_conv_b_w=ev_conv_b_w, ev_w_out=ev_w_out, od_norm_g=od_norm_g, od_w_in=od_w_in,
             od_b_in=od_b_in, od_ln_v_g=od_ln_v_g, od_ln_v_b=od_ln_v_b, od_w_s=od_w_s, od_b_s=od_b_s, od_w_out=od_w_out,
             mlp_norm_g=mlp_norm_g, mlp_w1=mlp_w1, mlp_w2=mlp_w2, final_norm_g=final_norm_g)
    M = dict(ev_norm_g=m_ev_norm_g, ev_w_in=m_ev_w_in, ev_conv_a_w=m_ev_conv_a_w, ev_conv_a_b=m_ev_conv_a_b,
             ev_ln_a_g=m_ev_ln_a_g, ev_ln_a_b=m_ev_ln_a_b, ev_conv_b_w=m_ev_conv_b_w, ev_w_out=m_ev_w_out,
             od_norm_g=m_od_norm_g, od_w_in=m_od_w_in, od_b_in=m_od_b_in, od_ln_v_g=m_od_ln_v_g, od_ln_v_b=m_od_ln_v_b,
             od_w_s=m_od_w_s, od_b_s=m_od_b_s, od_w_out=m_od_w_out, mlp_norm_g=m_mlp_norm_g, mlp_w1=m_mlp_w1,
             mlp_w2=m_mlp_w2, final_norm_g=m_final_norm_g)
    V = dict(ev_norm_g=v_ev_norm_g, ev_w_in=v_ev_w_in, ev_conv_a_w=v_ev_conv_a_w, ev_conv_a_b=v_ev_conv_a_b,
             ev_ln_a_g=v_ev_ln_a_g, ev_ln_a_b=v_ev_ln_a_b, ev_conv_b_w=v_ev_conv_b_w, ev_w_out=v_ev_w_out,
             od_norm_g=v_od_norm_g, od_w_in=v_od_w_in, od_b_in=v_od_b_in, od_ln_v_g=v_od_ln_v_g, od_ln_v_b=v_od_ln_v_b,
             od_w_s=v_od_w_s, od_b_s=v_od_b_s, od_w_out=v_od_w_out, mlp_norm_g=v_mlp_norm_g, mlp_w1=v_mlp_w1,
             mlp_w2=v_mlp_w2, final_norm_g=v_final_norm_g)

    n_seq, seq, d = x.shape
    t = n_seq * seq
    dev = 4 * lax.axis_index("x") + 2 * lax.axis_index("y") + lax.axis_index("c")
    core = lax.axis_index("c").astype(jnp.int32).reshape(1)

    ev_g, cab, lag, lab = W["ev_norm_g"], W["ev_conv_a_b"], W["ev_ln_a_g"], W["ev_ln_a_b"]
    ws = W["od_w_s"][0]
    bs_b = jnp.broadcast_to(W["od_b_s"][0][:, :, None], (C_GROUPS, CHUNK, CHUNK))
    mlp_g = [W["mlp_norm_g"][l:l + 1] for l in range(2)]
    fin_g = W["final_norm_g"].reshape(1, d)

    def w16(name, l=0):
        return W[name][l].astype(BF16)

    h0 = x.reshape(t, d)
    gather = _gather_comm([w16("ev_w_in")] + [_rows(W[n]) for n in SMALL_SHARDED])
    n0 = _rms_fwd("ev_norm", h0, ev_g, comm=gather)
    w_ev_in, caw, cbw, od_g, od_bin, lvg, lvb = [_whole(g) for g in gather.out]

    gather = _gather_comm([w16("ev_w_out"), w16("od_w_in")])
    z = _mm_nn("ev_in", n0, w_ev_in, IN_EVEN, _ep_store, [F32], comm=gather)[0]
    w_ev_out = gather.out[0].reshape(D_MODEL, D_MODEL)
    w_od_in = _whole(gather.out[1])

    gather = _gather_comm([w16("mlp_w1", 0)])
    mix = _mixer_fwd(z, seq, caw, cab, lag, lab, cbw, comm=gather)
    w1 = [_whole(gather.out[0]), None]

    gather = _gather_comm([w16("od_w_out")])
    h1, n1 = _mm_nn("ev_out", mix, w_ev_out, d, _ep_residual_norm, [F32, BF16], extras=(h0,), rows=(mlp_g[0],), comm=gather)
    w_od_out = gather.out[0].reshape(D_MODEL, D_MODEL)

    gather = _gather_comm([w16("mlp_w2", 0)])
    q0 = _mm_nn("mlp0_up", n1, w1[0], D_FF, _ep_relu_sq, [BF16], comm=gather)[0]
    w2 = [gather.out[0].reshape(D_FF, D_MODEL), None]

    gather = _gather_comm([w16("mlp_w1", 1)])
    h2, n2 = _mm_nn("mlp0_down", q0, w2[0], d, _ep_residual_norm, [F32, BF16], extras=(h1,), rows=(od_g,), comm=gather)
    w1[1] = _whole(gather.out[0])

    pre = _mm_nn("od_in", n2, w_od_in, 2 * C_DIM, _ep_bias, [F32], rows=(od_bin,))[0]
    y = _sgu_fwd(pre, lvg, lvb, ws, bs_b)
    h3, n3 = _mm_nn("od_out", y, w_od_out, d, _ep_residual_norm, [F32, BF16], extras=(h2,), rows=(mlp_g[1],))
    gather = _gather_comm([w16("mlp_w2", 1)])
    q1 = _mm_nn("mlp1_up", n3, w1[1], D_FF, _ep_relu_sq, [BF16], comm=gather)[0]
    w2[1] = gather.out[0].reshape(D_FF, D_MODEL)
    grad, grad16, d_fin_g, loss_part = _mm_nn(
        "mlp1_down", q1, w2[1], d, _ep_final_loss, [F32, BF16], extras=(h3, loss_target.reshape(t, d)), rows=(fin_g,),
        sums=(True, False), tm_want=FUSED_ROWS)
    loss = lax.psum(loss_part[0, 0], ("x", "y", "c"))

    by_chip = {}

    def swap(name, parts):
        comm = _pair_comm([parts])
        comm.parts, comm.weight = parts, name
        return comm

    def exchange(swapped):
        sums = _pair_sum(f"pair_sum_{swapped.weight}", swapped.parts, swapped.out[0], core)
        comm = _chip_comm([sums])
        comm.weight = swapped.weight
        return comm

    def done(comm):
        by_chip[comm.weight] = comm.out[0]

    def norm_grad(name, dz_, w, h, residual_grad, g, comm):
        dn = _mm_nt(name, dz_, w, d, _ep_store, [F32], comm=comm)[0]
        return _rms_bwd(name + "_norm", dn, h, g, residual_grad)

    dw2_1 = _mm_tn("mlp1_dw2", q1, grad16).reshape(N_DEV, D_FF // N_DEV, D_MODEL)
    s_a = swap("w2_1", dw2_1)
    dp = _mm_nt("mlp1_dq", grad16, w2[1], D_FF, _ep_relu_sq_grad, [BF16], extras=(q1,), comm=s_a)[0]
    c_a = exchange(s_a)
    dw1_1 = _mm_tn("mlp1_dw1", n3, dp, col_blocks=N_DEV, comm=c_a)
    done(c_a)
    s_b = swap("w1_1", dw1_1)
    grad, grad16, dg_mlp1 = norm_grad("mlp1_dn", dp, w1[1], h3, grad, mlp_g[1], s_b)
    c_b = exchange(s_b)
    d_od_out = _mm_tn("od_dw_out", y, grad16).reshape(N_DEV, D_MODEL // N_DEV, D_MODEL)
    s_c = swap("od_out", d_od_out)
    dy = _mm_nt("od_dy", grad16, w_od_out, C_DIM, _ep_store, [F32], comm=s_c)[0]
    c_c = exchange(s_c)
    dpre, d_ws, d_bsb, d_lvg, d_lvb, d_bin = _sgu_bwd(pre, dy, lvg, lvb, ws, bs_b, comm=c_c)
    done(c_c)
    d_od_in = _mm_tn("od_dw_in", n2, dpre, col_blocks=N_DEV, comm=c_b)
    done(c_b)
    s_d = swap("od_in", d_od_in)
    grad, grad16, d_od_g = norm_grad("od_dn", dpre, w_od_in, h2, grad, od_g, s_d)
    c_d = exchange(s_d)
    dw2_0 = _mm_tn("mlp0_dw2", q0, grad16, comm=c_d).reshape(N_DEV, D_FF // N_DEV, D_MODEL)
    done(c_d)
    s_e = swap("w2_0", dw2_0)
    dp = _mm_nt("mlp0_dq", grad16, w2[0], D_FF, _ep_relu_sq_grad, [BF16], extras=(q0,), comm=s_e)[0]
    c_e = exchange(s_e)
    dw1_0 = _mm_tn("mlp0_dw1", n1, dp, col_blocks=N_DEV, comm=c_e)
    done(c_e)
    s_f = swap("w1_0", dw1_0)
    grad, grad16, dg_mlp0 = norm_grad("mlp0_dn", dp, w1[0], h1, grad, mlp_g[0], s_f)
    c_f = exchange(s_f)
    d_ev_out = _mm_tn("ev_dw_out", mix, grad16).reshape(N_DEV, D_MODEL // N_DEV, D_MODEL)
    s_g = swap("ev_out", d_ev_out)
    dmix = _mm_nt("ev_dmix", grad16, w_ev_out, A_DIM + B_DIM, _ep_store, [F32], comm=s_g)[0]
    c_g = exchange(s_g)
    da2, dcv, d_caw, d_cab, d_lag, d_lab, d_cbw = _mixer_bwd_local(z, dmix, seq, caw, cab, lag, lab, cbw, comm=c_f)
    done(c_f)
    dz = _mixer_bwd_input(z, dmix, da2, dcv, seq, caw, cbw, comm=c_g)
    done(c_g)
    d_ev_in = _mm_tn("ev_dw_in", n0, dz)
    d_ev_in = jnp.transpose(d_ev_in.reshape(D_MODEL, N_DEV, IN_EVEN // N_DEV), (1, 0, 2))
    s_h = swap("ev_in", d_ev_in)
    grad_x, _, d_ev_g = norm_grad("ev_dn", dz, w_ev_in, h0, grad, ev_g, s_h)
    c_h = exchange(s_h)

    small_parts = dict(
        ev_norm_g=d_ev_g, ev_conv_a_w=d_caw, ev_conv_a_b=d_cab, ev_ln_a_g=d_lag, ev_ln_a_b=d_lab,
        ev_conv_b_w=d_cbw, od_norm_g=d_od_g, od_b_in=d_bin, od_ln_v_g=d_lvg, od_ln_v_b=d_lvb,
        od_w_s=d_ws.reshape(C_GROUPS * CHUNK, CHUNK), od_b_s=d_bsb[:, :, 0],
        mlp_norm_g=jnp.concatenate([dg_mlp0, dg_mlp1], axis=0), final_norm_g=d_fin_g)
    small_gather = _gather_comm([small_parts[n] for n in SMALL])
    shard = {"mlp_w1": ("w1_0", "w1_1"), "mlp_w2": ("w2_0", "w2_1"), "od_w_in": ("od_in",), "od_w_out": ("od_out",),
             "ev_w_out": ("ev_out",), "ev_w_in": ("ev_in",)}
    carried = {"mlp_w1": c_h, "mlp_w2": small_gather}
    out_g, out_d, out_m, out_v = {}, {}, {}, {}
    for name, keys in shard.items():
        out_g[name], out_d[name], out_m[name], out_v[name] = _sum_adamw(
            f"adamw_{name}", [by_chip[key] for key in keys], W[name], M[name], V[name], comm=carried.get(name))
        if name == "mlp_w1":
            done(c_h)

    mine = []
    for n, g in zip(SMALL, small_gather.out):
        if n in SMALL_SHARDED:
            width = W[n].shape[-1]
            g = lax.dynamic_slice_in_dim(g, dev * width, width, axis=2)
        mine.append(g)
    res = _small_adamw("adamw_small", mine, [_rows(W[n]) for n in SMALL], [_rows(M[n]) for n in SMALL],
                       [_rows(V[n]) for n in SMALL])
    for store, values in zip((out_g, out_d, out_m, out_v), res):
        for n, value in zip(SMALL, values):
            store[n] = value.reshape(W[n].shape)

    return (loss, grad_x.reshape(n_seq, seq, d), *[out_g[n] for n in ORDER], *[out_d[n] for n in ORDER],
            *[out_m[n] for n in ORDER], *[out_v[n] for n in ORDER])
```

```python
import math

import jax
import jax.numpy as jnp
from jax import lax
from jax.experimental import pallas as pl
from jax.experimental.pallas import tpu as pltpu

F32 = jnp.float32
BF16 = jnp.bfloat16
MESH = pl.DeviceIdType.MESH

D_MODEL = 1024
A_DIM = 512
B_DIM = 512
IN_EVEN = 2 * A_DIM + 3 * B_DIM
A_TAPS = 31
B_TAPS = 3
CHUNK = 128
C_GROUPS = 8
C_DIM = 1024
D_FF = 4096
RMS_EPS = 1e-6
LN_EPS = 1e-5
N_DEV = 8
N_CHIP = 4

ADAM_LR = 0.001
ADAM_B1 = 0.9
ADAM_B2 = 0.999
ADAM_EPS = 1e-08
ADAM_WD = 0.01
ADAM_STEP = 10

A_HALO = 32
B_HALO = 8
VMEM_LIMIT_BYTES = 56 * 1024 * 1024
INV_SQRT2 = 1.0 / math.sqrt(2.0)
INV_SQRT_2PI = 1.0 / math.sqrt(2.0 * math.pi)
HBM_SPEC = pl.BlockSpec(memory_space=pltpu.HBM)


def _tile(n, want):
    t = min(n, want)
    while n % t:
        t //= 2
    return t


def _sigmoid(x):
    return 1.0 / (1.0 + jnp.exp(-x))


def _gelu(x):
    return 0.5 * x * (1.0 + lax.erf(x * INV_SQRT2))


def _gelu_grad(x):
    return 0.5 * (1.0 + lax.erf(x * INV_SQRT2)) + x * jnp.exp(-0.5 * x * x) * INV_SQRT_2PI


def _colsum(x):
    return jnp.sum(x, axis=0, keepdims=True)


class _Comm:
    def __init__(self, ins, out_shapes, sem_shapes, start, finish, middle=None, into=None):
        self.ins, self.out_shapes, self.sem_shapes, self.start, self.finish = ins, out_shapes, sem_shapes, start, finish
        self.middle = middle
        self.into = list(into) if into is not None else []
        self.out = None


def _piece_rows(rows, piece):
    if piece is None:
        return 0, rows
    i, n = piece
    return i * (rows // n), rows // n


MIDDLE_AT = 0.75
FUSED_ROWS = 512


def _call(body, *, name, grid, in_specs, out_specs, out_shape, args, scratch_shapes=(), parallel=False, comm=None):
    comms = [] if comm is None else (list(comm) if isinstance(comm, (list, tuple)) else [comm])
    if not comms:
        sem = ("parallel" if parallel else "arbitrary",) * len(grid)
        return pl.pallas_call(
            body, name=name, grid=grid, in_specs=list(in_specs), out_specs=list(out_specs), out_shape=list(out_shape),
            scratch_shapes=list(scratch_shapes),
            compiler_params=pltpu.CompilerParams(dimension_semantics=sem, vmem_limit_bytes=VMEM_LIMIT_BYTES),
        )(*args)
    n_in, n_out, n_scr = len(in_specs), len(out_shape), len(scratch_shapes)
    c_ins_all = [a for cm in comms for a in cm.ins]
    c_into_all = [a for cm in comms for a in cm.into]
    c_out_shapes = [s for cm in comms for s in cm.out_shapes]
    c_sem_shapes = [s for cm in comms for s in cm.sem_shapes]
    aliases, in_pos, out_pos = {}, n_in + len(c_ins_all), n_out
    for cm in comms:
        for j in range(len(cm.into)):
            aliases[in_pos + j] = out_pos + j
        in_pos += len(cm.into)
        out_pos += len(cm.out_shapes)
    steps = grid
    total = math.prod(steps)
    first_step = (0,) * len(steps)
    last_step = tuple(s - 1 for s in steps)
    middle_step = None
    if 0 < int(MIDDLE_AT * total) < total - 1:
        rest, idx = int(MIDDLE_AT * total), []
        for s in reversed(steps):
            idx.append(rest % s)
            rest //= s
        middle_step = tuple(reversed(idx))

    def carrying(*refs):
        pos = 0
        ins = refs[pos:pos + n_in]; pos += n_in
        c_ins = refs[pos:pos + len(c_ins_all)]; pos += len(c_ins_all) + len(c_into_all)
        outs = refs[pos:pos + n_out]; pos += n_out
        c_outs = refs[pos:pos + len(c_out_shapes)]; pos += len(c_out_shapes)
        scr = refs[pos:pos + n_scr]; pos += n_scr
        c_sems = refs[pos:]
        views, i0, o0, s0 = [], 0, 0, 0
        for cm in comms:
            views.append((c_ins[i0:i0 + len(cm.ins)], c_outs[o0:o0 + len(cm.out_shapes)], c_sems[s0:s0 + len(cm.sem_shapes)]))
            i0, o0, s0 = i0 + len(cm.ins), o0 + len(cm.out_shapes), s0 + len(cm.sem_shapes)

        def at(step):
            hit = pl.program_id(0) == step[0]
            for axis in range(1, len(steps)):
                hit = jnp.logical_and(hit, pl.program_id(axis) == step[axis])
            return hit

        @pl.when(at(first_step))
        def _():
            for cm, view in zip(comms, views):
                cm.start(*view)

        if middle_step is not None:
            @pl.when(at(middle_step))
            def _():
                for cm, view in zip(comms, views):
                    if cm.middle is not None:
                        cm.middle(*view)

        body(*ins, *outs, *scr)

        @pl.when(at(last_step))
        def _():
            for cm, view in zip(comms, views):
                if cm.middle is not None and middle_step is None:
                    cm.middle(*view)
            for cm, view in zip(comms, views):
                cm.finish(*view)

    res = pl.pallas_call(
        carrying, name=name, grid=grid,
        in_specs=[*in_specs, *[HBM_SPEC] * (len(c_ins_all) + len(c_into_all))],
        out_specs=[*out_specs, *[HBM_SPEC] * len(c_out_shapes)],
        out_shape=[*out_shape, *c_out_shapes], scratch_shapes=[*scratch_shapes, *c_sem_shapes],
        input_output_aliases=aliases,
        compiler_params=pltpu.CompilerParams(dimension_semantics=("arbitrary",) * len(grid), vmem_limit_bytes=VMEM_LIMIT_BYTES),
    )(*args, *c_ins_all, *c_into_all)
    pos = n_out
    for cm in comms:
        cm.out = list(res[pos:pos + len(cm.out_shapes)])
        pos += len(cm.out_shapes)
    return list(res[:n_out])


def _place():
    x, y, c = lax.axis_index("x"), lax.axis_index("y"), lax.axis_index("c")
    return x, y, c, [(1 - x, y), (x, 1 - y), (1 - x, 1 - y)]


def _gather_comm(shards, piece=None, into=None):
    nw = len(shards)
    spans = [_piece_rows(s.shape[0], piece) for s in shards]

    def plan(ins, outs, sems):
        send_sems, recv_sems, local_sems = sems
        x, y, c, chips = _place()
        me, sibling = (x, y, c), (x, y, 1 - c)

        def slot(w, p):
            return outs[w].at[4 * p[0] + 2 * p[1] + p[2], pl.ds(*spans[w])]

        def mine(w):
            return ins[w].at[pl.ds(*spans[w])]

        def copy(w, k, block, to, src=None):
            return pltpu.make_async_remote_copy(
                src_ref=slot(w, block) if src is None else src, dst_ref=slot(w, block),
                send_sem=send_sems.at[w, k], recv_sem=recv_sems.at[w, k], device_id=to, device_id_type=MESH)

        local = [pltpu.make_async_copy(mine(w), slot(w, me), local_sems.at[w]) for w in range(nw)]
        first = [[copy(w, 0, me, sibling, src=mine(w))] + [copy(w, 1 + j, me, (*chip, c), src=mine(w)) for j, chip in enumerate(chips)]
                 for w in range(nw)]
        landed = [[copy(w, 1 + j, (*chip, c), me) for j, chip in enumerate(chips)] for w in range(nw)]
        passed = [[copy(w, 4 + j, (*chip, c), sibling) for j, chip in enumerate(chips)] for w in range(nw)]
        from_sibling = [[copy(w, 0, sibling, me)] + [copy(w, 4 + j, (*chip, 1 - c), me) for j, chip in enumerate(chips)]
                        for w in range(nw)]
        return local, first, landed, passed, from_sibling

    def start(ins, outs, sems):
        local, first, _, _, _ = plan(ins, outs, sems)
        for cp in local:
            cp.start()
        for row in first:
            for cp in row:
                cp.start()

    def middle(ins, outs, sems):
        _, _, landed, passed, _ = plan(ins, outs, sems)
        for w in range(nw):
            for j in range(3):
                landed[w][j].wait_recv()
                passed[w][j].start()

    def finish(ins, outs, sems):
        local, first, landed, passed, from_sibling = plan(ins, outs, sems)
        for w in range(nw):
            for cp in from_sibling[w]:
                cp.wait_recv()
        for w in range(nw):
            for cp in first[w] + passed[w]:
                cp.wait_send()
        for cp in local:
            cp.wait()

    return _Comm(list(shards), [jax.ShapeDtypeStruct((N_DEV, *s.shape), s.dtype) for s in shards],
                 [pltpu.SemaphoreType.DMA((nw, 7)), pltpu.SemaphoreType.DMA((nw, 7)), pltpu.SemaphoreType.DMA((nw,))],
                 start, finish, middle, into=into)


def _pair_comm(parts):
    nw = len(parts)

    def plan(ins, outs, sems):
        send_sems, recv_sems = sems
        x, y, c, _ = _place()
        return [pltpu.make_async_remote_copy(
            src_ref=ins[w].at[2 * q + 1 - c], dst_ref=outs[w].at[q], send_sem=send_sems.at[w, q], recv_sem=recv_sems.at[w, q],
            device_id=(x, y, 1 - c), device_id_type=MESH) for w in range(nw) for q in range(N_CHIP)]

    def start(ins, outs, sems):
        for cp in plan(ins, outs, sems):
            cp.start()

    def finish(ins, outs, sems):
        for cp in plan(ins, outs, sems):
            cp.wait()

    return _Comm(list(parts), [jax.ShapeDtypeStruct((N_CHIP, *p.shape[1:]), p.dtype) for p in parts],
                 [pltpu.SemaphoreType.DMA((nw, N_CHIP)), pltpu.SemaphoreType.DMA((nw, N_CHIP))], start, finish)


def _chip_comm(sums, piece=None, into=None):
    nw = len(sums)
    spans = [_piece_rows(s.shape[1], piece) for s in sums]

    def plan(ins, outs, sems):
        send_sems, recv_sems, local_sems = sems
        x, y, c, chips = _place()
        my_chip = 2 * x + y
        local = [pltpu.make_async_copy(ins[w].at[my_chip, pl.ds(*spans[w])], outs[w].at[my_chip, pl.ds(*spans[w])],
                                       local_sems.at[w]) for w in range(nw)]
        remote = [pltpu.make_async_remote_copy(
            src_ref=ins[w].at[2 * chip[0] + chip[1], pl.ds(*spans[w])], dst_ref=outs[w].at[my_chip, pl.ds(*spans[w])],
            send_sem=send_sems.at[w, j], recv_sem=recv_sems.at[w, j], device_id=(*chip, c), device_id_type=MESH)
            for w in range(nw) for j, chip in enumerate(chips)]
        return local, remote

    def start(ins, outs, sems):
        local, remote = plan(ins, outs, sems)
        for cp in local + remote:
            cp.start()

    def finish(ins, outs, sems):
        local, remote = plan(ins, outs, sems)
        for cp in remote + local:
            cp.wait()

    return _Comm(list(sums), [jax.ShapeDtypeStruct(s.shape, s.dtype) for s in sums],
                 [pltpu.SemaphoreType.DMA((nw, 3)), pltpu.SemaphoreType.DMA((nw, 3)), pltpu.SemaphoreType.DMA((nw,))],
                 start, finish, into=into)


def _matmul(name, a, b, *, kind, m, n, k, a_spec, b_spec, tm, tn, tk, out_shape, out_specs, epilogue,
            extras=(), extra_specs=(), comm=None):
    dims = {"nn": (((1,), (0,)), ((), ())), "nt": (((1,), (1,)), ((), ())), "tn": (((0,), (0,)), ((), ()))}[kind]
    nk = k // tk
    n_extra = len(extras)
    n_out = len(out_shape)

    def body(a_ref, b_ref, *rest):
        extra_refs = rest[:n_extra]
        out_refs = rest[n_extra:n_extra + n_out]
        part = lax.dot_general(a_ref[...], b_ref[...], dims, preferred_element_type=F32)
        if nk == 1:
            epilogue(part, extra_refs, out_refs)
            return
        acc_ref = rest[n_extra + n_out]
        step = pl.program_id(2)

        @pl.when(step == 0)
        def _():
            acc_ref[...] = part

        @pl.when(jnp.logical_and(step > 0, step < nk - 1))
        def _():
            acc_ref[...] += part

        @pl.when(step == nk - 1)
        def _():
            epilogue(acc_ref[...] + part, extra_refs, out_refs)

    return _call(
        body, name=name, grid=(m // tm, n // tn, nk), in_specs=[a_spec, b_spec, *extra_specs], out_specs=out_specs,
        out_shape=out_shape, scratch_shapes=[pltpu.VMEM((tm, tn), F32)] if nk > 1 else [], args=(a, b, *extras), comm=comm)


def _mm_operands(m, n, tm, tn, out_dtypes, extras, rows, sums):
    tile = pl.BlockSpec((tm, tn), lambda i, j, kk: (i, j))
    row = pl.BlockSpec((1, tn), lambda i, j, kk: (0, j))
    one = pl.BlockSpec((1, 1), lambda i, j, kk: (0, 0))
    out_shape = [jax.ShapeDtypeStruct((m, n), dt) for dt in out_dtypes]
    out_shape += [jax.ShapeDtypeStruct((1, n if wide else 1), F32) for wide in sums]
    out_specs = [tile] * len(out_dtypes) + [row if wide else one for wide in sums]
    return (*extras, *rows), [tile] * len(extras) + [row] * len(rows), out_shape, out_specs


def _row_tile(m, k, tm_want):
    return _tile(m, tm_want or (1024 if k <= 1024 else 512))


def _mm_nn(name, a, b, n, epilogue, out_dtypes, *, extras=(), rows=(), sums=(), tm_want=None, comm=None):
    m, k = a.shape
    blocked = b.ndim == 3
    tm = _row_tile(m, k, tm_want)
    tn = b.shape[-1] if blocked else _tile(n, 1024)
    tk = k
    extras, extra_specs, out_shape, out_specs = _mm_operands(m, n, tm, tn, out_dtypes, extras, rows, sums)
    if blocked:
        b_spec = pl.BlockSpec((None, tk, tn), lambda i, j, kk: (j, kk, 0))
    else:
        b_spec = pl.BlockSpec((tk, tn), lambda i, j, kk: (kk, j))
    return _matmul(
        name, a, b, kind="nn", m=m, n=n, k=k, tm=tm, tn=tn, tk=tk,
        a_spec=pl.BlockSpec((tm, tk), lambda i, j, kk: (i, kk)), b_spec=b_spec, out_shape=out_shape, out_specs=out_specs,
        epilogue=epilogue, extras=extras, extra_specs=extra_specs, comm=comm)


def _mm_nt(name, a, b, n, epilogue, out_dtypes, *, extras=(), rows=(), sums=(), tm_want=None, comm=None):
    m, k = a.shape
    blocked = b.ndim == 3
    tm = _row_tile(m, k, tm_want)
    tn = _tile(n, 1024)
    tk = b.shape[-1] if blocked else k
    extras, extra_specs, out_shape, out_specs = _mm_operands(m, n, tm, tn, out_dtypes, extras, rows, sums)
    if blocked:
        b_spec = pl.BlockSpec((None, tn, tk), lambda i, j, kk: (kk, j, 0))
    else:
        b_spec = pl.BlockSpec((tn, tk), lambda i, j, kk: (j, kk))
    return _matmul(
        name, a, b, kind="nt", m=m, n=n, k=k, tm=tm, tn=tn, tk=tk,
        a_spec=pl.BlockSpec((tm, tk), lambda i, j, kk: (i, kk)), b_spec=b_spec, out_shape=out_shape, out_specs=out_specs,
        epilogue=epilogue, extras=extras, extra_specs=extra_specs, comm=comm)


def _mm_tn(name, a, b, *, col_blocks=0, comm=None):
    t, k = a.shape
    n = b.shape[1]
    tm = _row_tile(k, t, None)
    tn = n // col_blocks if col_blocks else _tile(n, 1024)
    tk = t
    if col_blocks:
        out_shape = [jax.ShapeDtypeStruct((col_blocks, k, tn), F32)]
        out_specs = [pl.BlockSpec((None, tm, tn), lambda i, j, kk: (j, i, 0))]
    else:
        out_shape = [jax.ShapeDtypeStruct((k, n), F32)]
        out_specs = [pl.BlockSpec((tm, tn), lambda i, j, kk: (i, j))]

    def epilogue(acc, extra_refs, out_refs):
        out_refs[0][...] = acc

    return _matmul(
        name, a, b, kind="tn", m=k, n=n, k=t, tm=tm, tn=tn, tk=tk,
        a_spec=pl.BlockSpec((tk, tm), lambda i, j, kk: (kk, i)), b_spec=pl.BlockSpec((tk, tn), lambda i, j, kk: (kk, j)),
        out_shape=out_shape, out_specs=out_specs, epilogue=epilogue, comm=comm)[0]


def _ep_store(acc, extra_refs, out_refs):
    out_refs[0][...] = acc.astype(out_refs[0].dtype)


def _ep_bias(acc, extra_refs, out_refs):
    out_refs[0][...] = acc + extra_refs[0][...]


def _rms(x):
    r = lax.rsqrt(jnp.mean(x * x, axis=-1, keepdims=True) + RMS_EPS)
    return r, x * r


def _rms_grad(dn, r, xr, g):
    dy = dn * g
    return r * (dy - xr * jnp.mean(dy * xr, axis=-1, keepdims=True))


def _ep_residual_norm(acc, extra_refs, out_refs):
    h = extra_refs[0][...] + acc
    out_refs[0][...] = h
    _, xr = _rms(h)
    out_refs[1][...] = (xr * extra_refs[1][...]).astype(BF16)


def _ep_final_loss(acc, extra_refs, out_refs):
    @pl.when(pl.program_id(0) == 0)
    def _():
        out_refs[2][...] = jnp.zeros_like(out_refs[2])
        out_refs[3][...] = jnp.zeros_like(out_refs[3])

    h = extra_refs[0][...] + acc
    g = extra_refs[2][...]
    r, xr = _rms(h)
    err = xr * g - extra_refs[1][...]
    out_refs[3][...] += 0.5 * jnp.sum(jnp.mean(err * err, axis=-1, keepdims=True), axis=0, keepdims=True)
    dout = err * (1.0 / h.shape[-1])
    out_refs[2][...] += _colsum(dout * xr)
    out = _rms_grad(dout, r, xr, g)
    out_refs[0][...] = out
    out_refs[1][...] = out.astype(BF16)


def _ep_relu_sq(acc, extra_refs, out_refs):
    r = jnp.maximum(acc, 0.0)
    out_refs[0][...] = (r * r).astype(BF16)


def _ep_relu_sq_grad(acc, extra_refs, out_refs):
    out_refs[0][...] = (acc * (2.0 * jnp.sqrt(extra_refs[0][...].astype(F32)))).astype(BF16)


def _rms_fwd(name, h, g, comm=None):
    t, d = h.shape
    tt = _tile(t, 512)

    def body(h_ref, g_ref, n_ref):
        x = h_ref[...]
        r = lax.rsqrt(jnp.mean(x * x, axis=-1, keepdims=True) + RMS_EPS)
        n_ref[...] = (x * r * g_ref[...]).astype(BF16)

    return _call(
        body, name=name, grid=(t // tt,),
        in_specs=[pl.BlockSpec((tt, d), lambda i: (i, 0)), pl.BlockSpec((1, d), lambda i: (0, 0))],
        out_specs=[pl.BlockSpec((tt, d), lambda i: (i, 0))], out_shape=[jax.ShapeDtypeStruct((t, d), BF16)],
        args=(h, g), parallel=True, comm=comm)[0]


def _rms_bwd(name, dn, h, g, grad_in, comm=None):
    t, d = h.shape
    tt = _tile(t, 512)

    def body(dn_ref, h_ref, g_ref, gin_ref, gout_ref, gout16_ref, dg_ref):
        @pl.when(pl.program_id(0) == 0)
        def _():
            dg_ref[...] = jnp.zeros_like(dg_ref)

        dnv = dn_ref[...]
        r, xr = _rms(h_ref[...])
        dg_ref[...] += _colsum(dnv * xr)
        out = gin_ref[...] + _rms_grad(dnv, r, xr, g_ref[...])
        gout_ref[...] = out
        gout16_ref[...] = out.astype(BF16)

    row = pl.BlockSpec((tt, d), lambda i: (i, 0))
    vec = pl.BlockSpec((1, d), lambda i: (0, 0))
    return _call(
        body, name=name, grid=(t // tt,), in_specs=[row, row, vec, row], out_specs=[row, row, vec],
        out_shape=[jax.ShapeDtypeStruct((t, d), F32), jax.ShapeDtypeStruct((t, d), BF16), jax.ShapeDtypeStruct((1, d), F32)],
        args=(dn, h, g, grad_in), comm=comm)


def _mixer_windows(z_ref, zh_ref, first, a1_s, cb_s, tt):
    sig = _sigmoid(z_ref[:, A_DIM:2 * A_DIM])
    a1_s[A_HALO:A_HALO + tt, :] = z_ref[:, 0:A_DIM] * sig
    a1_h = zh_ref[:, 0:A_DIM] * _sigmoid(zh_ref[:, A_DIM:2 * A_DIM])
    a1_s[0:A_HALO, :] = jnp.where(first, 0.0, a1_h)
    cb_s[B_HALO:B_HALO + tt, :] = z_ref[:, 3 * A_DIM:4 * A_DIM] * z_ref[:, 4 * A_DIM:5 * A_DIM]
    cb_h = zh_ref[A_HALO - B_HALO:A_HALO, 3 * A_DIM:4 * A_DIM] * zh_ref[A_HALO - B_HALO:A_HALO, 4 * A_DIM:5 * A_DIM]
    cb_s[0:B_HALO, :] = jnp.where(first, 0.0, cb_h)
    return sig


def _causal_conv(win_s, w_ref, taps, halo, tt):
    base = halo - (taps - 1)
    acc = w_ref[0:1, :] * win_s[pl.ds(base, tt), :]
    for k in range(1, taps):
        acc = acc + w_ref[k:k + 1, :] * win_s[pl.ds(base + k, tt), :]
    return acc


SUBLANES = 8
LANE_BLOCK = 128
ROW_BLOCK = 128
SHIFT_ROWS = A_HALO - SUBLANES


def _shifted_copies(win_s, sh_s, tt):
    for b in range(1, SUBLANES):
        sh_s[b - 1] = win_s[pl.ds(b, tt + SHIFT_ROWS), :]


def _window_rows(win_s, sh_s, offset, rows, cols):
    b = offset % SUBLANES
    if b == 0:
        return win_s[pl.ds(offset, rows), cols]
    return sh_s[b - 1, pl.ds(offset - b, rows), cols]


def _blocks(tt):
    rb = min(tt, ROW_BLOCK)
    return rb, [(r, slice(lb * LANE_BLOCK, (lb + 1) * LANE_BLOCK))
                for lb in range(A_DIM // LANE_BLOCK) for r in range(0, tt, rb)]


def _conv_taps(win_s, sh_s, w_ref, offsets, out_s, tt, bias_ref=None):
    rb, blocks = _blocks(tt)
    for r, cols in blocks:
        acc = w_ref[0:1, cols] * _window_rows(win_s, sh_s, r + offsets[0], rb, cols)
        for k in range(1, len(offsets)):
            acc = acc + w_ref[k:k + 1, cols] * _window_rows(win_s, sh_s, r + offsets[k], rb, cols)
        out_s[r:r + rb, cols] = acc if bias_ref is None else acc + bias_ref[:, cols]


A_CAUSAL = [A_HALO - (A_TAPS - 1) + k for k in range(A_TAPS)]
A_ANTICAUSAL = [A_TAPS - 1 - k for k in range(A_TAPS)]


def _layer_norm_stats(x):
    mu = jnp.mean(x, axis=-1, keepdims=True)
    xc = x - mu
    rstd = lax.rsqrt(jnp.mean(xc * xc, axis=-1, keepdims=True) + LN_EPS)
    return xc * rstd, rstd


def _mixer_specs(seq, tt):
    tiles_per_seq = seq // tt
    halo_blocks = tt // A_HALO
    z_spec = pl.BlockSpec((tt, IN_EVEN), lambda i: (i, 0))
    zh_spec = pl.BlockSpec((A_HALO, IN_EVEN), lambda i: (jnp.maximum(i * halo_blocks - 1, 0), 0))
    return tiles_per_seq, z_spec, zh_spec


def _vec_spec(rows, cols):
    return pl.BlockSpec((rows, cols), lambda i: (0, 0))


def _mixer_fwd(z, seq, caw, cab, lag, lab, cbw, comm=None):
    t = z.shape[0]
    tt = _tile(seq, 256)
    tiles_per_seq, z_spec, zh_spec = _mixer_specs(seq, tt)

    def body(z_ref, zh_ref, caw_ref, cab_ref, lag_ref, lab_ref, cbw_ref, mix_ref, a1_s, cb_s, sh_s, a2_s):
        first = (pl.program_id(0) % tiles_per_seq) == 0
        _mixer_windows(z_ref, zh_ref, first, a1_s, cb_s, tt)
        _shifted_copies(a1_s, sh_s, tt)
        _conv_taps(a1_s, sh_s, caw_ref, A_CAUSAL, a2_s, tt, bias_ref=cab_ref)
        xhat, _ = _layer_norm_stats(a2_s[...])
        a3 = xhat * lag_ref[...] + lab_ref[...]
        mix_ref[:, 0:A_DIM] = (a3 * _sigmoid(a3)).astype(BF16)
        cv = _causal_conv(cb_s, cbw_ref, B_TAPS, B_HALO, tt)
        mix_ref[:, A_DIM:A_DIM + B_DIM] = (z_ref[:, 2 * A_DIM:3 * A_DIM] * cv).astype(BF16)

    return _call(
        body, name="mixer_fwd", grid=(t // tt,),
        in_specs=[z_spec, zh_spec, _vec_spec(A_TAPS, A_DIM), _vec_spec(1, A_DIM), _vec_spec(1, A_DIM), _vec_spec(1, A_DIM),
                  _vec_spec(B_TAPS, B_DIM)],
        out_specs=[pl.BlockSpec((tt, A_DIM + B_DIM), lambda i: (i, 0))],
        out_shape=[jax.ShapeDtypeStruct((t, A_DIM + B_DIM), BF16)],
        scratch_shapes=[pltpu.VMEM((A_HALO + tt, A_DIM), F32), pltpu.VMEM((B_HALO + tt, B_DIM), F32),
                        pltpu.VMEM((SUBLANES - 1, tt + SHIFT_ROWS, A_DIM), F32), pltpu.VMEM((tt, A_DIM), F32)],
        args=(z, z, caw, cab, lag, lab, cbw), parallel=True, comm=comm)[0]


def _mixer_bwd_local(z, dmix, seq, caw, cab, lag, lab, cbw, comm=None):
    t = z.shape[0]
    tt = _tile(seq, 256)
    tiles_per_seq, z_spec, zh_spec = _mixer_specs(seq, tt)

    def body(z_ref, zh_ref, dmix_ref, caw_ref, cab_ref, lag_ref, lab_ref, cbw_ref,
             da2_ref, dcv_ref, dcaw_ref, dcab_ref, dlag_ref, dlab_ref, dcbw_ref, a1_s, cb_s, sh_s, a2_s):
        @pl.when(pl.program_id(0) == 0)
        def _():
            for ref in (dcaw_ref, dcab_ref, dlag_ref, dlab_ref, dcbw_ref):
                ref[...] = jnp.zeros_like(ref)

        first = (pl.program_id(0) % tiles_per_seq) == 0
        _mixer_windows(z_ref, zh_ref, first, a1_s, cb_s, tt)
        _shifted_copies(a1_s, sh_s, tt)
        _conv_taps(a1_s, sh_s, caw_ref, A_CAUSAL, a2_s, tt, bias_ref=cab_ref)
        xhat, rstd = _layer_norm_stats(a2_s[...])
        a3 = xhat * lag_ref[...] + lab_ref[...]
        s3 = _sigmoid(a3)
        da3 = dmix_ref[:, 0:A_DIM] * (s3 * (1.0 + a3 * (1.0 - s3)))
        dlag_ref[...] += _colsum(da3 * xhat)
        dlab_ref[...] += _colsum(da3)
        dxh = da3 * lag_ref[...]
        da2 = rstd * (dxh - jnp.mean(dxh, axis=-1, keepdims=True) - xhat * jnp.mean(dxh * xhat, axis=-1, keepdims=True))
        da2_ref[...] = da2
        dcab_ref[...] += _colsum(da2)
        rb, blocks = _blocks(tt)
        for r, cols in blocks:
            da2_b = da2_ref[r:r + rb, cols]
            for k in range(A_TAPS):
                dcaw_ref[k:k + 1, cols] += _colsum(da2_b * _window_rows(a1_s, sh_s, r + A_CAUSAL[k], rb, cols))
        dcv =dmix_ref[:, A_DIM:A_DIM + B_DIM] * z_ref[:, 2 * A_DIM:3 * A_DIM]
        dcv_ref[...] = dcv
        for k in range(B_TAPS):
            dcbw_ref[k:k + 1, :] += _colsum(dcv * cb_s[pl.ds(B_HALO - (B_TAPS - 1) + k, tt), :])

    half = pl.BlockSpec((tt, A_DIM), lambda i: (i, 0))
    return _call(
        body, name="mixer_bwd_local", grid=(t // tt,),
        in_specs=[z_spec, zh_spec, pl.BlockSpec((tt, A_DIM + B_DIM), lambda i: (i, 0)),
                  _vec_spec(A_TAPS, A_DIM), _vec_spec(1, A_DIM), _vec_spec(1, A_DIM), _vec_spec(1, A_DIM), _vec_spec(B_TAPS, B_DIM)],
        out_specs=[half, half, _vec_spec(A_TAPS, A_DIM), _vec_spec(1, A_DIM), _vec_spec(1, A_DIM), _vec_spec(1, A_DIM),
                   _vec_spec(B_TAPS, B_DIM)],
        out_shape=[jax.ShapeDtypeStruct((t, A_DIM), F32), jax.ShapeDtypeStruct((t, B_DIM), F32),
                   jax.ShapeDtypeStruct((A_TAPS, A_DIM), F32), jax.ShapeDtypeStruct((1, A_DIM), F32),
                   jax.ShapeDtypeStruct((1, A_DIM), F32), jax.ShapeDtypeStruct((1, A_DIM), F32),
                   jax.ShapeDtypeStruct((B_TAPS, B_DIM), F32)],
        scratch_shapes=[pltpu.VMEM((A_HALO + tt, A_DIM), F32), pltpu.VMEM((B_HALO + tt, B_DIM), F32),
                        pltpu.VMEM((SUBLANES - 1, tt + SHIFT_ROWS, A_DIM), F32), pltpu.VMEM((tt, A_DIM), F32)],
        args=(z, z, dmix, caw, cab, lag, lab, cbw), comm=comm)


def _mixer_bwd_input(z, dmix, da2, dcv, seq, caw, cbw, comm=None):
    t = z.shape[0]
    tt = _tile(seq, 256)
    tiles_per_seq, z_spec, zh_spec = _mixer_specs(seq, tt)
    a_blocks = tt // A_HALO
    b_blocks = tt // B_HALO
    last_a = t // A_HALO - 1
    last_b = t // B_HALO - 1

    def body(z_ref, zh_ref, dmix_ref, da2_ref, da2n_ref, dcv_ref, dcvn_ref, caw_ref, cbw_ref, dz_ref, a1_s, cb_s, da2_s, dcv_s,
             sh_s, da1_s):
        pos = pl.program_id(0) % tiles_per_seq
        first = pos == 0
        last = pos == tiles_per_seq - 1
        sig = _mixer_windows(z_ref, zh_ref, first, a1_s, cb_s, tt)
        da2_s[0:tt, :] = da2_ref[...]
        da2_s[tt:tt + A_HALO, :] = jnp.where(last, 0.0, da2n_ref[...])
        dcv_s[0:tt, :] = dcv_ref[...]
        dcv_s[tt:tt + B_HALO, :] = jnp.where(last, 0.0, dcvn_ref[...])
        _shifted_copies(da2_s, sh_s, tt)
        _conv_taps(da2_s, sh_s, caw_ref, A_ANTICAUSAL, da1_s, tt)
        da1 = da1_s[...]
        dz_ref[:, 0:A_DIM] = (da1 * sig).astype(BF16)
        dz_ref[:, A_DIM:2 * A_DIM] = (da1 * z_ref[:, 0:A_DIM] * sig * (1.0 - sig)).astype(BF16)
        cv = _causal_conv(cb_s, cbw_ref, B_TAPS, B_HALO, tt)
        dz_ref[:, 2 * A_DIM:3 * A_DIM] = (dmix_ref[:, A_DIM:A_DIM + B_DIM] * cv).astype(BF16)
        dcb = cbw_ref[0:1, :] * dcv_s[pl.ds(B_TAPS - 1, tt), :]
        for k in range(1, B_TAPS):
            dcb = dcb + cbw_ref[k:k + 1, :] * dcv_s[pl.ds(B_TAPS - 1 - k, tt), :]
        dz_ref[:, 3 * A_DIM:4 * A_DIM] = (dcb * z_ref[:, 4 * A_DIM:5 * A_DIM]).astype(BF16)
        dz_ref[:, 4 * A_DIM:5 * A_DIM] = (dcb * z_ref[:, 3 * A_DIM:4 * A_DIM]).astype(BF16)

    half = pl.BlockSpec((tt, A_DIM), lambda i: (i, 0))
    return _call(
        body, name="mixer_bwd_input", grid=(t // tt,),
        in_specs=[z_spec, zh_spec, pl.BlockSpec((tt, A_DIM + B_DIM), lambda i: (i, 0)),
                  half, pl.BlockSpec((A_HALO, A_DIM), lambda i: (jnp.minimum((i + 1) * a_blocks, last_a), 0)),
                  half, pl.BlockSpec((B_HALO, B_DIM), lambda i: (jnp.minimum((i + 1) * b_blocks, last_b), 0)),
                  _vec_spec(A_TAPS, A_DIM), _vec_spec(B_TAPS, B_DIM)],
        out_specs=[pl.BlockSpec((tt, IN_EVEN), lambda i: (i, 0))],
        out_shape=[jax.ShapeDtypeStruct((t, IN_EVEN), BF16)],
        scratch_shapes=[pltpu.VMEM((A_HALO + tt, A_DIM), F32), pltpu.VMEM((B_HALO + tt, B_DIM), F32),
                        pltpu.VMEM((tt + A_HALO, A_DIM), F32), pltpu.VMEM((tt + B_HALO, B_DIM), F32),
                        pltpu.VMEM((SUBLANES - 1, tt + SHIFT_ROWS, A_DIM), F32), pltpu.VMEM((tt, A_DIM), F32)],
        args=(z, z, dmix, da2, da2, dcv, dcv, caw, cbw), parallel=True, comm=comm)[0]


def _tril_ws(ws_ref, g):
    rows = lax.broadcasted_iota(jnp.int32, (CHUNK, CHUNK), 0)
    cols = lax.broadcasted_iota(jnp.int32, (CHUNK, CHUNK), 1)
    return jnp.where(rows >= cols, ws_ref[g], 0.0).astype(BF16), rows >= cols


def _sgu_fwd(pre, lvg, lvb, ws, bs_b, comm=None):
    t = pre.shape[0]
    tt = _tile(t, 256)

    def body(pre_ref, lvg_ref, lvb_ref, ws_ref, bsb_ref, y_ref):
        vhat, _ = _layer_norm_stats(_gelu(pre_ref[:, C_DIM:2 * C_DIM]))
        vl = (vhat * lvg_ref[...] + lvb_ref[...]).astype(BF16)
        for g in range(C_GROUPS):
            w, _ = _tril_ws(ws_ref, g)
            cols = slice(g * CHUNK, (g + 1) * CHUNK)
            for ci in range(tt // CHUNK):
                rows = slice(ci * CHUNK, (ci + 1) * CHUNK)
                sv = jnp.dot(w, vl[rows, cols], preferred_element_type=F32) + bsb_ref[g]
                y_ref[rows, cols] = (_gelu(pre_ref[rows, cols]) * sv).astype(BF16)

    group = pl.BlockSpec((C_GROUPS, CHUNK, CHUNK), lambda i: (0, 0, 0))
    return _call(
        body, name="sgu_fwd", grid=(t // tt,),
        in_specs=[pl.BlockSpec((tt, 2 * C_DIM), lambda i: (i, 0)), _vec_spec(1, C_DIM), _vec_spec(1, C_DIM), group, group],
        out_specs=[pl.BlockSpec((tt, C_DIM), lambda i: (i, 0))], out_shape=[jax.ShapeDtypeStruct((t, C_DIM), BF16)],
        args=(pre, lvg, lvb, ws, bs_b), parallel=True, comm=comm)[0]


def _sgu_bwd(pre, dy, lvg, lvb, ws, bs_b, comm=None):
    t = pre.shape[0]
    tt = _tile(t, 256)

    def body(pre_ref, dy_ref, lvg_ref, lvb_ref, ws_ref, bsb_ref, dpre_ref, dws_ref, dbsb_ref, dlvg_ref, dlvb_ref, dbin_ref,
             dvl_s):
        @pl.when(pl.program_id(0) == 0)
        def _():
            for ref in (dws_ref, dbsb_ref, dlvg_ref, dlvb_ref, dbin_ref):
                ref[...] = jnp.zeros_like(ref)

        pre_v = pre_ref[:, C_DIM:2 * C_DIM]
        vhat, rstd = _layer_norm_stats(_gelu(pre_v))
        vl = (vhat * lvg_ref[...] + lvb_ref[...]).astype(BF16)
        for g in range(C_GROUPS):
            w, keep = _tril_ws(ws_ref, g)
            cols = slice(g * CHUNK, (g + 1) * CHUNK)
            dws = jnp.zeros((CHUNK, CHUNK), F32)
            dbs = jnp.zeros((CHUNK, 1), F32)
            for ci in range(tt // CHUNK):
                rows = slice(ci * CHUNK, (ci + 1) * CHUNK)
                vl_g = vl[rows, cols]
                sv = jnp.dot(w, vl_g, preferred_element_type=F32) + bsb_ref[g]
                pre_u = pre_ref[rows, cols]
                dyv = dy_ref[rows, cols]
                du = dyv * sv * _gelu_grad(pre_u)
                dpre_ref[rows, cols] = du.astype(BF16)
                dbin_ref[:, cols] += _colsum(du)
                dsv = dyv * _gelu(pre_u)
                dbs = dbs + jnp.sum(dsv, axis=1, keepdims=True)
                dsv16 = dsv.astype(BF16)
                dws = dws + lax.dot_general(dsv16, vl_g, (((1,), (1,)), ((), ())), preferred_element_type=F32)
                dvl_s[rows, cols] = lax.dot_general(w, dsv16, (((0,), (0,)), ((), ())), preferred_element_type=F32)
            dws_ref[g] += jnp.where(keep, dws, 0.0)
            dbsb_ref[g] += dbs
        dvl = dvl_s[...]
        dlvg_ref[...] += _colsum(dvl * vhat)
        dlvb_ref[...] += _colsum(dvl)
        dxh = dvl * lvg_ref[...]
        dv = rstd * (dxh - jnp.mean(dxh, axis=-1, keepdims=True) - vhat * jnp.mean(dxh * vhat, axis=-1, keepdims=True))
        dpv = dv * _gelu_grad(pre_v)
        dpre_ref[:, C_DIM:2 * C_DIM] = dpv.astype(BF16)
        dbin_ref[:, C_DIM:2 * C_DIM] += _colsum(dpv)

    group = pl.BlockSpec((C_GROUPS, CHUNK, CHUNK), lambda i: (0, 0, 0))
    return _call(
        body, name="sgu_bwd", grid=(t // tt,),
        in_specs=[pl.BlockSpec((tt, 2 * C_DIM), lambda i: (i, 0)), pl.BlockSpec((tt, C_DIM), lambda i: (i, 0)),
                  _vec_spec(1, C_DIM), _vec_spec(1, C_DIM), group, group],
        out_specs=[pl.BlockSpec((tt, 2 * C_DIM), lambda i: (i, 0)), group, group,
                   _vec_spec(1, C_DIM), _vec_spec(1, C_DIM), _vec_spec(1, 2 * C_DIM)],
        out_shape=[jax.ShapeDtypeStruct((t, 2 * C_DIM), BF16), jax.ShapeDtypeStruct((C_GROUPS, CHUNK, CHUNK), F32),
                   jax.ShapeDtypeStruct((C_GROUPS, CHUNK, CHUNK), F32), jax.ShapeDtypeStruct((1, C_DIM), F32),
                   jax.ShapeDtypeStruct((1, C_DIM), F32), jax.ShapeDtypeStruct((1, 2 * C_DIM), F32)],
        scratch_shapes=[pltpu.VMEM((tt, C_DIM), F32)],
        args=(pre, dy, lvg, lvb, ws, bs_b), comm=comm)


def _pair_sum(name, part, got, core):
    _, k, n = part.shape
    tk = _tile(k, 256)

    def body(core_ref, p_ref, s_ref, o_ref):
        o_ref[...] = (p_ref[...] + s_ref[...]).astype(BF16)

    return pl.pallas_call(
        body, name=name,
        grid_spec=pltpu.PrefetchScalarGridSpec(
            num_scalar_prefetch=1, grid=(N_CHIP, k // tk),
            in_specs=[pl.BlockSpec((None, tk, n), lambda q, i, core_ref: (2 * q + core_ref[0], i, 0)),
                      pl.BlockSpec((None, tk, n), lambda q, i, core_ref: (q, i, 0))],
            out_specs=pl.BlockSpec((None, tk, n), lambda q, i, core_ref: (q, i, 0))),
        out_shape=jax.ShapeDtypeStruct((N_CHIP, k, n), BF16),
        compiler_params=pltpu.CompilerParams(dimension_semantics=("parallel", "parallel"), vmem_limit_bytes=VMEM_LIMIT_BYTES),
    )(core, part, got)


def _adamw_math(w, g, m, v):
    m = ADAM_B1 * m + (1.0 - ADAM_B1) * g
    v = ADAM_B2 * v + (1.0 - ADAM_B2) * (g * g)
    m_hat = m / (1.0 - ADAM_B1 ** ADAM_STEP)
    v_hat = v / (1.0 - ADAM_B2 ** ADAM_STEP)
    delta = -ADAM_LR * (m_hat / (jnp.sqrt(v_hat) + ADAM_EPS) + ADAM_WD * w)
    return delta, m, v


def _sum_adamw(name, parts, w, m, v, comm=None):
    layers = len(parts)
    n_parts, k, n = parts[0].shape
    tk = _tile(k, 128)

    def body(*refs):
        p_refs = refs[:layers]
        w_ref, m_ref, v_ref, g_ref, d_ref, nm_ref, nv_ref = refs[layers:]

        def total(p_ref):
            g = p_ref[0].astype(F32)
            for q in range(1, n_parts):
                g = g + p_ref[q].astype(F32)
            return g

        g = total(p_refs[0])
        for l in range(1, layers):
            g = jnp.where(pl.program_id(0) == l, total(p_refs[l]), g)
        g_ref[...] = g
        d_ref[...], nm_ref[...], nv_ref[...] = _adamw_math(w_ref[...], g, m_ref[...], v_ref[...])

    blk = pl.BlockSpec((None, tk, n), lambda l, i: (l, i, 0))
    return _call(
        body, name=name, grid=(layers, k // tk),
        in_specs=[pl.BlockSpec((n_parts, tk, n), lambda l, i: (0, i, 0))] * layers + [blk, blk, blk], out_specs=[blk] * 4,
        out_shape=[jax.ShapeDtypeStruct((layers, k, n), F32)] * 4, args=(*parts, w, m, v), parallel=True, comm=comm)


def _small_adamw(name, parts, w, m, v):
    count = len(parts)

    def body(*refs):
        p_refs, w_refs, m_refs, v_refs = (refs[j * count:(j + 1) * count] for j in range(4))
        g_refs, d_refs, nm_refs, nv_refs = (refs[(4 + j) * count:(5 + j) * count] for j in range(4))
        for i in range(count):
            g = p_refs[i][0]
            for dev in range(1, N_DEV):
                g = g + p_refs[i][dev]
            g_refs[i][...] = g
            d_refs[i][...], nm_refs[i][...], nv_refs[i][...] = _adamw_math(w_refs[i][...], g, m_refs[i][...], v_refs[i][...])

    res = pl.pallas_call(
        body, name=name, out_shape=[jax.ShapeDtypeStruct(a.shape, F32) for a in w] * 4,
        compiler_params=pltpu.CompilerParams(vmem_limit_bytes=VMEM_LIMIT_BYTES))(*parts, *w, *m, *v)
    return [res[j * count:(j + 1) * count] for j in range(4)]


def _rows(a):
    return a.reshape(-1, a.shape[-1])


def _whole(gathered):
    return jnp.transpose(gathered, (1, 0, 2)).reshape(gathered.shape[1], -1)


SMALL =("ev_norm_g", "ev_conv_a_w", "ev_conv_a_b", "ev_ln_a_g", "ev_ln_a_b", "ev_conv_b_w", "od_norm_g", "od_b_in",
         "od_ln_v_g", "od_ln_v_b", "od_w_s", "od_b_s", "mlp_norm_g", "final_norm_g")
SMALL_SHARDED = ("ev_conv_a_w", "ev_conv_b_w", "od_norm_g", "od_b_in", "od_ln_v_g", "od_ln_v_b")
ORDER = ("ev_norm_g", "ev_w_in", "ev_conv_a_w", "ev_conv_a_b", "ev_ln_a_g", "ev_ln_a_b", "ev_conv_b_w", "ev_w_out",
         "od_norm_g", "od_w_in", "od_b_in", "od_ln_v_g", "od_ln_v_b", "od_w_s", "od_b_s", "od_w_out", "mlp_norm_g",
         "mlp_w1", "mlp_w2", "final_norm_g")


def kernel(x, ev_norm_g, ev_w_in, ev_conv_a_w, ev_conv_a_b, ev_ln_a_g, ev_ln_a_b, ev_conv_b_w, ev_w_out, od_norm_g, od_w_in, od_b_in, od_ln_v_g, od_ln_v_b, od_w_s, od_b_s, od_w_out, mlp_norm_g, mlp_w1, mlp_w2, final_norm_g, loss_target, m_ev_norm_g, m_ev_w_in, m_ev_conv_a_w, m_ev_conv_a_b, m_ev_ln_a_g, m_ev_ln_a_b, m_ev_conv_b_w, m_ev_w_out, m_od_norm_g, m_od_w_in, m_od_b_in, m_od_ln_v_g, m_od_ln_v_b, m_od_w_s, m_od_b_s, m_od_w_out, m_mlp_norm_g, m_mlp_w1, m_mlp_w2, m_final_norm_g, v_ev_norm_g, v_ev_w_in, v_ev_conv_a_w, v_ev_conv_a_b, v_ev_ln_a_g, v_ev_ln_a_b, v_ev_conv_b_w, v_ev_w_out, v_od_norm_g, v_od_w_in, v_od_b_in, v_od_ln_v_g, v_od_ln_v_b, v_od_w_s, v_od_b_s, v_od_w_out, v_mlp_norm_g, v_mlp_w1, v_mlp_w2, v_final_norm_g):
    W = dict(ev_norm_g=ev_norm_g, ev_w_in=ev_w_in, ev_conv_a_w=ev_conv_a_w, ev_conv_a_b=ev_conv_a_b, ev_ln_a_g=ev_ln_a_g,
             ev_ln_a_b=ev_ln_a_b, ev_conv_b_w=ev_conv_b_w, ev_w_out=ev_w_out, od_norm_g=od_norm_g, od_w_in=od_w_in,
             od_b_in=od_b_in, od_ln_v_g=od_ln_v_g, od_ln_v_b=od_ln_v_b, od_w_s=od_w_s, od_b_s=od_b_s, od_w_out=od_w_out,
             mlp_norm_g=mlp_norm_g, mlp_w1=mlp_w1, mlp_w2=mlp_w2, final_norm_g=final_norm_g)
    M = dict(ev_norm_g=m_ev_norm_g, ev_w_in=m_ev_w_in, ev_conv_a_w=m_ev_conv_a_w, ev_conv_a_b=m_ev_conv_a_b,
             ev_ln_a_g=m_ev_ln_a_g, ev_ln_a_b=m_ev_ln_a_b, ev_conv_b_w=m_ev_conv_b_w, ev_w_out=m_ev_w_out,
             od_norm_g=m_od_norm_g, od_w_in=m_od_w_in, od_b_in=m_od_b_in, od_ln_v_g=m_od_ln_v_g, od_ln_v_b=m_od_ln_v_b,
             od_w_s=m_od_w_s, od_b_s=m_od_b_s, od_w_out=m_od_w_out, mlp_norm_g=m_mlp_norm_g, mlp_w1=m_mlp_w1,
             mlp_w2=m_mlp_w2, final_norm_g=m_final_norm_g)
    V = dict(ev_norm_g=v_ev_norm_g, ev_w_in=v_ev_w_in, ev_conv_a_w=v_ev_conv_a_w, ev_conv_a_b=v_ev_conv_a_b,
             ev_ln_a_g=v_ev_ln_a_g, ev_ln_a_b=v_ev_ln_a_b, ev_conv_b_w=v_ev_conv_b_w, ev_w_out=v_ev_w_out,
             od_norm_g=v_od_norm_g, od_w_in=v_od_w_in, od_b_in=v_od_b_in, od_ln_v_g=v_od_ln_v_g, od_ln_v_b=v_od_ln_v_b,
             od_w_s=v_od_w_s, od_b_s=v_od_b_s, od_w_out=v_od_w_out, mlp_norm_g=v_mlp_norm_g, mlp_w1=v_mlp_w1,
             mlp_w2=v_mlp_w2, final_norm_g=v_final_norm_g)

    n_seq, seq, d = x.shape
    t = n_seq * seq
    dev = 4 * lax.axis_index("x") + 2 * lax.axis_index("y") + lax.axis_index("c")
    core = lax.axis_index("c").astype(jnp.int32).reshape(1)

    ev_g, cab, lag, lab = W["ev_norm_g"], W["ev_conv_a_b"], W["ev_ln_a_g"], W["ev_ln_a_b"]
    ws = W["od_w_s"][0]
    bs_b = jnp.broadcast_to(W["od_b_s"][0][:, :, None], (C_GROUPS, CHUNK, CHUNK))
    mlp_g = [W["mlp_norm_g"][l:l + 1] for l in range(2)]
    fin_g = W["final_norm_g"].reshape(1, d)

    def w16(name, l=0):
        return W[name][l].astype(BF16)

    h0 = x.reshape(t, d)
    gather = _gather_comm([w16("ev_w_in")] + [_rows(W[n]) for n in SMALL_SHARDED])
    n0 = _rms_fwd("ev_norm", h0, ev_g, comm=gather)
    w_ev_in, caw, cbw, od_g, od_bin, lvg, lvb = [_whole(g) for g in gather.out]

    first, second = (0, 2), (1, 2)
    w1_0, w2_0, w1_1, w2_1 = w16("mlp_w1", 0), w16("mlp_w2", 0), w16("mlp_w1", 1), w16("mlp_w2", 1)

    g_a, g_b = _gather_comm([w16("ev_w_out")]), _gather_comm([w1_0], first)
    z = _mm_nn("ev_in", n0, w_ev_in, IN_EVEN, _ep_store, [F32], comm=[g_a, g_b])[0]
    w_ev_out = g_a.out[0].reshape(D_MODEL, D_MODEL)

    g_c, g_d = _gather_comm([w1_0], second, into=g_b.out), _gather_comm([w2_0], first)
    mix = _mixer_fwd(z, seq, caw, cab, lag, lab, cbw, comm=[g_c, g_d])
    w1 = [_whole(g_c.out[0]), None]

    g_e = _gather_comm([w2_0], second, into=g_d.out)
    h1, n1 = _mm_nn("ev_out", mix, w_ev_out, d, _ep_residual_norm, [F32, BF16], extras=(h0,), rows=(mlp_g[0],), comm=g_e)
    w2 = [g_e.out[0].reshape(D_FF, D_MODEL), None]

    g_f = _gather_comm([w16("od_w_in"), w16("od_w_out")])
    q0 = _mm_nn("mlp0_up", n1, w1[0], D_FF, _ep_relu_sq, [BF16], comm=g_f)[0]
    w_od_in = _whole(g_f.out[0])
    w_od_out = g_f.out[1].reshape(D_MODEL, D_MODEL)

    g_g = _gather_comm([w1_1], first)
    h2, n2 = _mm_nn("mlp0_down", q0, w2[0], d, _ep_residual_norm, [F32, BF16], extras=(h1,), rows=(od_g,), comm=g_g)
    g_h = _gather_comm([w1_1], second, into=g_g.out)
    pre = _mm_nn("od_in", n2, w_od_in, 2 * C_DIM, _ep_bias, [F32], rows=(od_bin,), comm=g_h)[0]
    w1[1] = _whole(g_h.out[0])
    g_i = _gather_comm([w2_1], first)
    y = _sgu_fwd(pre, lvg, lvb, ws, bs_b, comm=g_i)
    h3, n3 = _mm_nn("od_out", y, w_od_out, d, _ep_residual_norm, [F32, BF16], extras=(h2,), rows=(mlp_g[1],))
    g_j = _gather_comm([w2_1], second, into=g_i.out)
    q1 = _mm_nn("mlp1_up", n3, w1[1], D_FF, _ep_relu_sq, [BF16], comm=g_j)[0]
    w2[1] = g_j.out[0].reshape(D_FF, D_MODEL)
    grad, grad16, d_fin_g, loss_part = _mm_nn(
        "mlp1_down", q1, w2[1], d, _ep_final_loss, [F32, BF16], extras=(h3, loss_target.reshape(t, d)), rows=(fin_g,),
        sums=(True, False), tm_want=FUSED_ROWS)
    loss = lax.psum(loss_part[0, 0], ("x", "y", "c"))

    by_chip = {}

    def swap(name, parts):
        comm = _pair_comm([parts])
        comm.parts, comm.weight = parts, name
        return comm

    def exchange(swapped, halves=False):
        sums = _pair_sum(f"pair_sum_{swapped.weight}", swapped.parts, swapped.out[0], core)
        if not halves:
            comm = _chip_comm([sums])
            comm.weight = swapped.weight
            return comm
        comm = _chip_comm([sums], first)
        comm.sums, comm.weight = sums, swapped.weight
        return comm

    def rest(comm):
        other = _chip_comm([comm.sums], second, into=comm.out)
        other.weight = comm.weight
        return other

    def done(comm):
        by_chip[comm.weight] = comm.out[0]

    dw2_1 = _mm_tn("mlp1_dw2", q1, grad16).reshape(N_DEV, D_FF // N_DEV, D_MODEL)
    s_a = swap("w2_1", dw2_1)
    dp = _mm_nt("mlp1_dq", grad16, w2[1], D_FF, _ep_relu_sq_grad, [BF16], extras=(q1,), comm=s_a)[0]
    c_a = exchange(s_a, halves=True)
    dw1_1 = _mm_tn("mlp1_dw1", n3, dp, col_blocks=N_DEV, comm=c_a)
    c_a2, s_b = rest(c_a), swap("w1_1", dw1_1)
    dn = _mm_nt("mlp1_dn", dp, w1[1], d, _ep_store, [F32], comm=[c_a2, s_b])[0]
    done(c_a2)
    c_b = exchange(s_b, halves=True)
    grad, grad16, dg_mlp1 = _rms_bwd("mlp1_dn_norm", dn, h3, mlp_g[1], grad)
    d_od_out = _mm_tn("od_dw_out", y, grad16).reshape(N_DEV, D_MODEL // N_DEV, D_MODEL)
    s_c = swap("od_out", d_od_out)
    dy = _mm_nt("od_dy", grad16, w_od_out, C_DIM, _ep_store, [F32], comm=s_c)[0]
    c_c = exchange(s_c)
    dpre, d_ws, d_bsb, d_lvg, d_lvb, d_bin = _sgu_bwd(pre, dy, lvg, lvb, ws, bs_b, comm=[c_b, c_c])
    done(c_c)
    c_b2 = rest(c_b)
    d_od_in = _mm_tn("od_dw_in", n2, dpre, col_blocks=N_DEV, comm=c_b2)
    done(c_b2)
    s_d = swap("od_in", d_od_in)
    dn = _mm_nt("od_dn", dpre, w_od_in, d, _ep_store, [F32], comm=s_d)[0]
    c_d = exchange(s_d)
    grad, grad16, d_od_g = _rms_bwd("od_dn_norm", dn, h2, od_g, grad)
    dw2_0 = _mm_tn("mlp0_dw2", q0, grad16, comm=c_d).reshape(N_DEV, D_FF // N_DEV, D_MODEL)
    done(c_d)
    s_e = swap("w2_0", dw2_0)
    dp = _mm_nt("mlp0_dq", grad16, w2[0], D_FF, _ep_relu_sq_grad, [BF16], extras=(q0,), comm=s_e)[0]
    c_e = exchange(s_e, halves=True)
    dw1_0 = _mm_tn("mlp0_dw1", n1, dp, col_blocks=N_DEV, comm=c_e)
    c_e2, s_f = rest(c_e), swap("w1_0", dw1_0)
    dn = _mm_nt("mlp0_dn", dp, w1[0], d, _ep_store, [F32], comm=[c_e2, s_f])[0]
    done(c_e2)
    c_f = exchange(s_f)
    grad, grad16, dg_mlp0 = _rms_bwd("mlp0_dn_norm", dn, h1, mlp_g[0], grad)
    d_ev_out = _mm_tn("ev_dw_out", mix, grad16).reshape(N_DEV, D_MODEL // N_DEV, D_MODEL)
    s_g = swap("ev_out", d_ev_out)
    dmix = _mm_nt("ev_dmix", grad16, w_ev_out, A_DIM + B_DIM, _ep_store, [F32], comm=s_g)[0]
    c_g = exchange(s_g)
    da2, dcv, d_caw, d_cab, d_lag, d_lab, d_cbw = _mixer_bwd_local(z, dmix, seq, caw, cab, lag, lab, cbw, comm=[c_f, c_g])
    done(c_f)
    done(c_g)
    dz = _mixer_bwd_input(z, dmix, da2, dcv, seq, caw, cbw)
    d_ev_in = _mm_tn("ev_dw_in", n0, dz)
    d_ev_in = jnp.transpose(d_ev_in.reshape(D_MODEL, N_DEV, IN_EVEN // N_DEV), (1, 0, 2))
    s_h = swap("ev_in", d_ev_in)
    dn = _mm_nt("ev_dn", dz, w_ev_in, d, _ep_store, [F32], comm=s_h)[0]
    c_h = exchange(s_h, halves=True)
    grad_x, _, d_ev_g = _rms_bwd("ev_dn_norm", dn, h0, ev_g, grad, comm=c_h)
    c_h = rest(c_h)

    small_parts = dict(
        ev_norm_g=d_ev_g, ev_conv_a_w=d_caw, ev_conv_a_b=d_cab, ev_ln_a_g=d_lag, ev_ln_a_b=d_lab,
        ev_conv_b_w=d_cbw, od_norm_g=d_od_g, od_b_in=d_bin, od_ln_v_g=d_lvg, od_ln_v_b=d_lvb,
        od_w_s=d_ws.reshape(C_GROUPS * CHUNK, CHUNK), od_b_s=d_bsb[:, :, 0],
        mlp_norm_g=jnp.concatenate([dg_mlp0, dg_mlp1], axis=0), final_norm_g=d_fin_g)
    small_gather = _gather_comm([small_parts[n] for n in SMALL])
    shard = {"mlp_w1": ("w1_0", "w1_1"), "mlp_w2": ("w2_0", "w2_1"), "od_w_in": ("od_in",), "od_w_out": ("od_out",),
             "ev_w_out": ("ev_out",), "ev_w_in": ("ev_in",)}
    carried = {"mlp_w1": c_h, "mlp_w2": small_gather}
    out_g, out_d, out_m, out_v = {}, {}, {}, {}
    for name, keys in shard.items():
        out_g[name], out_d[name], out_m[name], out_v[name] = _sum_adamw(
            f"adamw_{name}", [by_chip[key] for key in keys], W[name], M[name], V[name], comm=carried.get(name))
        if name == "mlp_w1":
            done(c_h)

    mine = []
    for n, g in zip(SMALL, small_gather.out):
        if n in SMALL_SHARDED:
            width = W[n].shape[-1]
            g = lax.dynamic_slice_in_dim(g, dev * width, width, axis=2)
        mine.append(g)
    res = _small_adamw("adamw_small", mine, [_rows(W[n]) for n in SMALL], [_rows(M[n]) for n in SMALL],
                       [_rows(V[n]) for n in SMALL])
    for store, values in zip((out_g, out_d, out_m, out_v), res):
        for n, value in zip(SMALL, values):
            store[n] = value.reshape(W[n].shape)

    return (loss, grad_x.reshape(n_seq, seq, d), *[out_g[n] for n in ORDER], *[out_d[n] for n in ORDER],
            *[out_m[n] for n in ORDER], *[out_v[n] for n in ORDER])
```

```python
import math

import jax
import jax.numpy as jnp
from jax import lax
from jax.experimental import pallas as pl
from jax.experimental.pallas import tpu as pltpu

F32 = jnp.float32
BF16 = jnp.bfloat16
MESH = pl.DeviceIdType.MESH

D_MODEL = 1024
A_DIM = 512
B_DIM = 512
IN_EVEN = 2 * A_DIM + 3 * B_DIM
A_TAPS = 31
B_TAPS = 3
CHUNK = 128
C_GROUPS = 8
C_DIM = 1024
D_FF = 4096
RMS_EPS = 1e-6
LN_EPS = 1e-5
N_DEV = 8
N_CHIP = 4

ADAM_LR = 0.001
ADAM_B1 = 0.9
ADAM_B2 = 0.999
ADAM_EPS = 1e-08
ADAM_WD = 0.01
ADAM_STEP = 10

A_HALO = 32
B_HALO = 8
VMEM_LIMIT_BYTES = 56 * 1024 * 1024
INV_SQRT2 = 1.0 / math.sqrt(2.0)
INV_SQRT_2PI = 1.0 / math.sqrt(2.0 * math.pi)
HBM_SPEC = pl.BlockSpec(memory_space=pltpu.HBM)


def _tile(n, want):
    t = min(n, want)
    while n % t:
        t //= 2
    return t


def _sigmoid(x):
    return 1.0 / (1.0 + jnp.exp(-x))


def _gelu(x):
    return 0.5 * x * (1.0 + lax.erf(x * INV_SQRT2))


def _gelu_grad(x):
    return 0.5 * (1.0 + lax.erf(x * INV_SQRT2)) + x * jnp.exp(-0.5 * x * x) * INV_SQRT_2PI


def _colsum(x):
    return jnp.sum(x, axis=0, keepdims=True)


class _Comm:
    def __init__(self, ins, out_shapes, sem_shapes, start, finish, middle=None, into=None):
        self.ins, self.out_shapes, self.sem_shapes, self.start, self.finish = ins, out_shapes, sem_shapes, start, finish
        self.middle = middle
        self.into = list(into) if into is not None else []
        self.out = None


def _piece_rows(rows, piece):
    if piece is None:
        return 0, rows
    i, n = piece
    return i * (rows // n), rows // n


MIDDLE_AT = 0.75
FUSED_ROWS = 512


def _call(body, *, name, grid, in_specs, out_specs, out_shape, args, scratch_shapes=(), parallel=False, comm=None):
    comms = [] if comm is None else (list(comm) if isinstance(comm, (list, tuple)) else [comm])
    if not comms:
        sem = ("parallel" if parallel else "arbitrary",) * len(grid)
        return pl.pallas_call(
            body, name=name, grid=grid, in_specs=list(in_specs), out_specs=list(out_specs), out_shape=list(out_shape),
            scratch_shapes=list(scratch_shapes),
            compiler_params=pltpu.CompilerParams(dimension_semantics=sem, vmem_limit_bytes=VMEM_LIMIT_BYTES),
        )(*args)
    n_in, n_out, n_scr = len(in_specs), len(out_shape), len(scratch_shapes)
    c_ins_all = [a for cm in comms for a in cm.ins]
    c_into_all = [a for cm in comms for a in cm.into]
    c_out_shapes = [s for cm in comms for s in cm.out_shapes]
    c_sem_shapes = [s for cm in comms for s in cm.sem_shapes]
    aliases, in_pos, out_pos = {}, n_in + len(c_ins_all), n_out
    for cm in comms:
        for j in range(len(cm.into)):
            aliases[in_pos + j] = out_pos + j
        in_pos += len(cm.into)
        out_pos += len(cm.out_shapes)
    steps = grid
    total = math.prod(steps)
    first_step = (0,) * len(steps)
    last_step = tuple(s - 1 for s in steps)
    middle_step = None
    if 0 < int(MIDDLE_AT * total) < total - 1:
        rest, idx = int(MIDDLE_AT * total), []
        for s in reversed(steps):
            idx.append(rest % s)
            rest //= s
        middle_step = tuple(reversed(idx))

    def carrying(*refs):
        pos = 0
        ins = refs[pos:pos + n_in]; pos += n_in
        c_ins = refs[pos:pos + len(c_ins_all)]; pos += len(c_ins_all) + len(c_into_all)
        outs = refs[pos:pos + n_out]; pos += n_out
        c_outs = refs[pos:pos + len(c_out_shapes)]; pos += len(c_out_shapes)
        scr = refs[pos:pos + n_scr]; pos += n_scr
        c_sems = refs[pos:]
        views, i0, o0, s0 = [], 0, 0, 0
        for cm in comms:
            views.append((c_ins[i0:i0 + len(cm.ins)], c_outs[o0:o0 + len(cm.out_shapes)], c_sems[s0:s0 + len(cm.sem_shapes)]))
            i0, o0, s0 = i0 + len(cm.ins), o0 + len(cm.out_shapes), s0 + len(cm.sem_shapes)

        def at(step):
            hit = pl.program_id(0) == step[0]
            for axis in range(1, len(steps)):
                hit = jnp.logical_and(hit, pl.program_id(axis) == step[axis])
            return hit

        @pl.when(at(first_step))
        def _():
            for cm, view in zip(comms, views):
                cm.start(*view)

        if middle_step is not None:
            @pl.when(at(middle_step))
            def _():
                for cm, view in zip(comms, views):
                    if cm.middle is not None:
                        cm.middle(*view)

        body(*ins, *outs, *scr)

        @pl.when(at(last_step))
        def _():
            for cm, view in zip(comms, views):
                if cm.middle is not None and middle_step is None:
                    cm.middle(*view)
            for cm, view in zip(comms, views):
                cm.finish(*view)

    res = pl.pallas_call(
        carrying, name=name, grid=grid,
        in_specs=[*in_specs, *[HBM_SPEC] * (len(c_ins_all) + len(c_into_all))],
        out_specs=[*out_specs, *[HBM_SPEC] * len(c_out_shapes)],
        out_shape=[*out_shape, *c_out_shapes], scratch_shapes=[*scratch_shapes, *c_sem_shapes],
        input_output_aliases=aliases,
        compiler_params=pltpu.CompilerParams(dimension_semantics=("arbitrary",) * len(grid), vmem_limit_bytes=VMEM_LIMIT_BYTES),
    )(*args, *c_ins_all, *c_into_all)
    pos = n_out
    for cm in comms:
        cm.out = list(res[pos:pos + len(cm.out_shapes)])
        pos += len(cm.out_shapes)
    return list(res[:n_out])


def _place():
    x, y, c = lax.axis_index("x"), lax.axis_index("y"), lax.axis_index("c")
    return x, y, c, [(1 - x, y), (x, 1 - y), (1 - x, 1 - y)]


def _gather_comm(shards, piece=None, into=None):
    nw = len(shards)
    spans = [_piece_rows(s.shape[0], piece) for s in shards]

    def plan(ins, outs, sems):
        send_sems, recv_sems, local_sems = sems
        x, y, c, chips = _place()
        me, sibling = (x, y, c), (x, y, 1 - c)

        def slot(w, p):
            return outs[w].at[4 * p[0] + 2 * p[1] + p[2], pl.ds(*spans[w])]

        def mine(w):
            return ins[w].at[pl.ds(*spans[w])]

        def copy(w, k, block, to, src=None):
            return pltpu.make_async_remote_copy(
                src_ref=slot(w, block) if src is None else src, dst_ref=slot(w, block),
                send_sem=send_sems.at[w, k], recv_sem=recv_sems.at[w, k], device_id=to, device_id_type=MESH)

        local = [pltpu.make_async_copy(mine(w), slot(w, me), local_sems.at[w]) for w in range(nw)]
        first = [[copy(w, 0, me, sibling, src=mine(w))] + [copy(w, 1 + j, me, (*chip, c), src=mine(w)) for j, chip in enumerate(chips)]
                 for w in range(nw)]
        landed = [[copy(w, 1 + j, (*chip, c), me) for j, chip in enumerate(chips)] for w in range(nw)]
        passed = [[copy(w, 4 + j, (*chip, c), sibling) for j, chip in enumerate(chips)] for w in range(nw)]
        from_sibling = [[copy(w, 0, sibling, me)] + [copy(w, 4 + j, (*chip, 1 - c), me) for j, chip in enumerate(chips)]
                        for w in range(nw)]
        return local, first, landed, passed, from_sibling

    def start(ins, outs, sems):
        local, first, _, _, _ = plan(ins, outs, sems)
        for cp in local:
            cp.start()
        for row in first:
            for cp in row:
                cp.start()

    def middle(ins, outs, sems):
        _, _, landed, passed, _ = plan(ins, outs, sems)
        for w in range(nw):
            for j in range(3):
                landed[w][j].wait_recv()
                passed[w][j].start()

    def finish(ins, outs, sems):
        local, first, landed, passed, from_sibling = plan(ins, outs, sems)
        for w in range(nw):
            for cp in from_sibling[w]:
                cp.wait_recv()
        for w in range(nw):
            for cp in first[w] + passed[w]:
                cp.wait_send()
        for cp in local:
            cp.wait()

    return _Comm(list(shards), [jax.ShapeDtypeStruct((N_DEV, *s.shape), s.dtype) for s in shards],
                 [pltpu.SemaphoreType.DMA((nw, 7)), pltpu.SemaphoreType.DMA((nw, 7)), pltpu.SemaphoreType.DMA((nw,))],
                 start, finish, middle, into=into)


def _pair_comm(parts):
    nw = len(parts)

    def plan(ins, outs, sems):
        send_sems, recv_sems = sems
        x, y, c, _ = _place()
        return [pltpu.make_async_remote_copy(
            src_ref=ins[w].at[2 * q + 1 - c], dst_ref=outs[w].at[q], send_sem=send_sems.at[w, q], recv_sem=recv_sems.at[w, q],
            device_id=(x, y, 1 - c), device_id_type=MESH) for w in range(nw) for q in range(N_CHIP)]

    def start(ins, outs, sems):
        for cp in plan(ins, outs, sems):
            cp.start()

    def finish(ins, outs, sems):
        for cp in plan(ins, outs, sems):
            cp.wait()

    return _Comm(list(parts), [jax.ShapeDtypeStruct((N_CHIP, *p.shape[1:]), p.dtype) for p in parts],
                 [pltpu.SemaphoreType.DMA((nw, N_CHIP)), pltpu.SemaphoreType.DMA((nw, N_CHIP))], start, finish)


def _chip_comm(sums, piece=None, into=None):
    nw = len(sums)
    spans = [_piece_rows(s.shape[1], piece) for s in sums]

    def plan(ins, outs, sems):
        send_sems, recv_sems, local_sems = sems
        x, y, c, chips = _place()
        my_chip = 2 * x + y
        local = [pltpu.make_async_copy(ins[w].at[my_chip, pl.ds(*spans[w])], outs[w].at[my_chip, pl.ds(*spans[w])],
                                       local_sems.at[w]) for w in range(nw)]
        remote = [pltpu.make_async_remote_copy(
            src_ref=ins[w].at[2 * chip[0] + chip[1], pl.ds(*spans[w])], dst_ref=outs[w].at[my_chip, pl.ds(*spans[w])],
            send_sem=send_sems.at[w, j], recv_sem=recv_sems.at[w, j], device_id=(*chip, c), device_id_type=MESH)
            for w in range(nw) for j, chip in enumerate(chips)]
        return local, remote

    def start(ins, outs, sems):
        local, remote = plan(ins, outs, sems)
        for cp in local + remote:
            cp.start()

    def finish(ins, outs, sems):
        local, remote = plan(ins, outs, sems)
        for cp in remote + local:
            cp.wait()

    return _Comm(list(sums), [jax.ShapeDtypeStruct(s.shape, s.dtype) for s in sums],
                 [pltpu.SemaphoreType.DMA((nw, 3)), pltpu.SemaphoreType.DMA((nw, 3)), pltpu.SemaphoreType.DMA((nw,))],
                 start, finish, into=into)


def _matmul(name, a, b, *, kind, m, n, k, a_spec, b_spec, tm, tn, tk, out_shape, out_specs, epilogue,
            extras=(), extra_specs=(), comm=None):
    dims = {"nn": (((1,), (0,)), ((), ())), "nt": (((1,), (1,)), ((), ())), "tn": (((0,), (0,)), ((), ()))}[kind]
    nk = k // tk
    n_extra = len(extras)
    n_out = len(out_shape)

    def body(a_ref, b_ref, *rest):
        extra_refs = rest[:n_extra]
        out_refs = rest[n_extra:n_extra + n_out]
        part = lax.dot_general(a_ref[...], b_ref[...], dims, preferred_element_type=F32)
        if nk == 1:
            epilogue(part, extra_refs, out_refs)
            return
        acc_ref = rest[n_extra + n_out]
        step = pl.program_id(2)

        @pl.when(step == 0)
        def _():
            acc_ref[...] = part

        @pl.when(jnp.logical_and(step > 0, step < nk - 1))
        def _():
            acc_ref[...] += part

        @pl.when(step == nk - 1)
        def _():
            epilogue(acc_ref[...] + part, extra_refs, out_refs)

    return _call(
        body, name=name, grid=(m // tm, n // tn, nk), in_specs=[a_spec, b_spec, *extra_specs], out_specs=out_specs,
        out_shape=out_shape, scratch_shapes=[pltpu.VMEM((tm, tn), F32)] if nk > 1 else [], args=(a, b, *extras), comm=comm)


def _mm_operands(m, n, tm, tn, out_dtypes, extras, rows, sums):
    tile = pl.BlockSpec((tm, tn), lambda i, j, kk: (i, j))
    row = pl.BlockSpec((1, tn), lambda i, j, kk: (0, j))
    one = pl.BlockSpec((1, 1), lambda i, j, kk: (0, 0))
    out_shape = [jax.ShapeDtypeStruct((m, n), dt) for dt in out_dtypes]
    out_shape += [jax.ShapeDtypeStruct((1, n if wide else 1), F32) for wide in sums]
    out_specs = [tile] * len(out_dtypes) + [row if wide else one for wide in sums]
    return (*extras, *rows), [tile] * len(extras) + [row] * len(rows), out_shape, out_specs


def _row_tile(m, k, tm_want):
    return _tile(m, tm_want or (1024 if k <= 1024 else 512))


def _mm_nn(name, a, b, n, epilogue, out_dtypes, *, extras=(), rows=(), sums=(), tm_want=None, comm=None):
    m, k = a.shape
    blocked = b.ndim == 3
    tm = _row_tile(m, k, tm_want)
    tn = b.shape[-1] if blocked else _tile(n, 1024)
    tk = k
    extras, extra_specs, out_shape, out_specs = _mm_operands(m, n, tm, tn, out_dtypes, extras, rows, sums)
    if blocked:
        b_spec = pl.BlockSpec((None, tk, tn), lambda i, j, kk: (j, kk, 0))
    else:
        b_spec = pl.BlockSpec((tk, tn), lambda i, j, kk: (kk, j))
    return _matmul(
        name, a, b, kind="nn", m=m, n=n, k=k, tm=tm, tn=tn, tk=tk,
        a_spec=pl.BlockSpec((tm, tk), lambda i, j, kk: (i, kk)), b_spec=b_spec, out_shape=out_shape, out_specs=out_specs,
        epilogue=epilogue, extras=extras, extra_specs=extra_specs, comm=comm)


def _mm_nt(name, a, b, n, epilogue, out_dtypes, *, extras=(), rows=(), sums=(), tm_want=None, comm=None):
    m, k = a.shape
    blocked = b.ndim == 3
    tm = _row_tile(m, k, tm_want)
    tn = _tile(n, 1024)
    tk = b.shape[-1] if blocked else k
    extras, extra_specs, out_shape, out_specs = _mm_operands(m, n, tm, tn, out_dtypes, extras, rows, sums)
    if blocked:
        b_spec = pl.BlockSpec((None, tn, tk), lambda i, j, kk: (kk, j, 0))
    else:
        b_spec = pl.BlockSpec((tn, tk), lambda i, j, kk: (j, kk))
    return _matmul(
        name, a, b, kind="nt", m=m, n=n, k=k, tm=tm, tn=tn, tk=tk,
        a_spec=pl.BlockSpec((tm, tk), lambda i, j, kk: (i, kk)), b_spec=b_spec, out_shape=out_shape, out_specs=out_specs,
        epilogue=epilogue, extras=extras, extra_specs=extra_specs, comm=comm)


def _mm_tn(name, a, b, *, col_blocks=0, comm=None):
    t, k = a.shape
    n = b.shape[1]
    tm = _row_tile(k, t, None)
    tn = n // col_blocks if col_blocks else _tile(n, 1024)
    tk = t
    if col_blocks:
        out_shape = [jax.ShapeDtypeStruct((col_blocks, k, tn), F32)]
        out_specs = [pl.BlockSpec((None, tm, tn), lambda i, j, kk: (j, i, 0))]
    else:
        out_shape = [jax.ShapeDtypeStruct((k, n), F32)]
        out_specs = [pl.BlockSpec((tm, tn), lambda i, j, kk: (i, j))]

    def epilogue(acc, extra_refs, out_refs):
        out_refs[0][...] = acc

    return _matmul(
        name, a, b, kind="tn", m=k, n=n, k=t, tm=tm, tn=tn, tk=tk,
        a_spec=pl.BlockSpec((tk, tm), lambda i, j, kk: (kk, i)), b_spec=pl.BlockSpec((tk, tn), lambda i, j, kk: (kk, j)),
        out_shape=out_shape, out_specs=out_specs, epilogue=epilogue, comm=comm)[0]


def _ep_store(acc, extra_refs, out_refs):
    out_refs[0][...] = acc.astype(out_refs[0].dtype)


def _ep_bias(acc, extra_refs, out_refs):
    out_refs[0][...] = acc + extra_refs[0][...]


def _rms(x):
    r = lax.rsqrt(jnp.mean(x * x, axis=-1, keepdims=True) + RMS_EPS)
    return r, x * r


def _rms_grad(dn, r, xr, g):
    dy = dn * g
    return r * (dy - xr * jnp.mean(dy * xr, axis=-1, keepdims=True))


def _ep_residual_norm(acc, extra_refs, out_refs):
    h = extra_refs[0][...] + acc
    out_refs[0][...] = h
    _, xr = _rms(h)
    out_refs[1][...] = (xr * extra_refs[1][...]).astype(BF16)


def _ep_final_loss(acc, extra_refs, out_refs):
    @pl.when(pl.program_id(0) == 0)
    def _():
        out_refs[2][...] = jnp.zeros_like(out_refs[2])
        out_refs[3][...] = jnp.zeros_like(out_refs[3])

    h = extra_refs[0][...] + acc
    g = extra_refs[2][...]
    r, xr = _rms(h)
    err = xr * g - extra_refs[1][...]
    out_refs[3][...] += 0.5 * jnp.sum(jnp.mean(err * err, axis=-1, keepdims=True), axis=0, keepdims=True)
    dout = err * (1.0 / h.shape[-1])
    out_refs[2][...] += _colsum(dout * xr)
    out = _rms_grad(dout, r, xr, g)
    out_refs[0][...] = out
    out_refs[1][...] = out.astype(BF16)


def _ep_relu_sq(acc, extra_refs, out_refs):
    r = jnp.maximum(acc, 0.0)
    out_refs[0][...] = (r * r).astype(BF16)


def _ep_relu_sq_grad(acc, extra_refs, out_refs):
    out_refs[0][...] = (acc * (2.0 * jnp.sqrt(extra_refs[0][...].astype(F32)))).astype(BF16)


def _rms_fwd(name, h, g, comm=None):
    t, d = h.shape
    tt = _tile(t, 512)

    def body(h_ref, g_ref, n_ref):
        x = h_ref[...]
        r = lax.rsqrt(jnp.mean(x * x, axis=-1, keepdims=True) + RMS_EPS)
        n_ref[...] = (x * r * g_ref[...]).astype(BF16)

    return _call(
        body, name=name, grid=(t // tt,),
        in_specs=[pl.BlockSpec((tt, d), lambda i: (i, 0)), pl.BlockSpec((1, d), lambda i: (0, 0))],
        out_specs=[pl.BlockSpec((tt, d), lambda i: (i, 0))], out_shape=[jax.ShapeDtypeStruct((t, d), BF16)],
        args=(h, g), parallel=True, comm=comm)[0]


def _rms_bwd(name, dn, h, g, grad_in, comm=None):
    t, d = h.shape
    tt = _tile(t, 512)

    def body(dn_ref, h_ref, g_ref, gin_ref, gout_ref, gout16_ref, dg_ref):
        @pl.when(pl.program_id(0) == 0)
        def _():
            dg_ref[...] = jnp.zeros_like(dg_ref)

        dnv = dn_ref[...]
        r, xr = _rms(h_ref[...])
        dg_ref[...] += _colsum(dnv * xr)
        out = gin_ref[...] + _rms_grad(dnv, r, xr, g_ref[...])
        gout_ref[...] = out
        gout16_ref[...] = out.astype(BF16)

    row = pl.BlockSpec((tt, d), lambda i: (i, 0))
    vec = pl.BlockSpec((1, d), lambda i: (0, 0))
    return _call(
        body, name=name, grid=(t // tt,), in_specs=[row, row, vec, row], out_specs=[row, row, vec],
        out_shape=[jax.ShapeDtypeStruct((t, d), F32), jax.ShapeDtypeStruct((t, d), BF16), jax.ShapeDtypeStruct((1, d), F32)],
        args=(dn, h, g, grad_in), comm=comm)


def _mixer_windows(z_ref, zh_ref, first, a1_s, cb_s, tt):
    sig = _sigmoid(z_ref[:, A_DIM:2 * A_DIM])
    a1_s[A_HALO:A_HALO + tt, :] = z_ref[:, 0:A_DIM] * sig
    a1_h = zh_ref[:, 0:A_DIM] * _sigmoid(zh_ref[:, A_DIM:2 * A_DIM])
    a1_s[0:A_HALO, :] = jnp.where(first, 0.0, a1_h)
    cb_s[B_HALO:B_HALO + tt, :] = z_ref[:, 3 * A_DIM:4 * A_DIM] * z_ref[:, 4 * A_DIM:5 * A_DIM]
    cb_h = zh_ref[A_HALO - B_HALO:A_HALO, 3 * A_DIM:4 * A_DIM] * zh_ref[A_HALO - B_HALO:A_HALO, 4 * A_DIM:5 * A_DIM]
    cb_s[0:B_HALO, :] = jnp.where(first, 0.0, cb_h)
    return sig


def _causal_conv(win_s, w_ref, taps, halo, tt):
    base = halo - (taps - 1)
    acc = w_ref[0:1, :] * win_s[pl.ds(base, tt), :]
    for k in range(1, taps):
        acc = acc + w_ref[k:k + 1, :] * win_s[pl.ds(base + k, tt), :]
    return acc


SUBLANES = 8
LANE_BLOCK = 128
ROW_BLOCK = 128
SHIFT_ROWS = A_HALO - SUBLANES


def _shifted_copies(win_s, sh_s, tt):
    for b in range(1, SUBLANES):
        sh_s[b - 1] = win_s[pl.ds(b, tt + SHIFT_ROWS), :]


def _window_rows(win_s, sh_s, offset, rows, cols):
    b = offset % SUBLANES
    if b == 0:
        return win_s[pl.ds(offset, rows), cols]
    return sh_s[b - 1, pl.ds(offset - b, rows), cols]


def _blocks(tt):
    rb = min(tt, ROW_BLOCK)
    return rb, [(r, slice(lb * LANE_BLOCK, (lb + 1) * LANE_BLOCK))
                for lb in range(A_DIM // LANE_BLOCK) for r in range(0, tt, rb)]


def _conv_taps(win_s, sh_s, w_ref, offsets, out_s, tt, bias_ref=None):
    rb, blocks = _blocks(tt)
    for r, cols in blocks:
        acc = w_ref[0:1, cols] * _window_rows(win_s, sh_s, r + offsets[0], rb, cols)
        for k in range(1, len(offsets)):
            acc = acc + w_ref[k:k + 1, cols] * _window_rows(win_s, sh_s, r + offsets[k], rb, cols)
        out_s[r:r + rb, cols] = acc if bias_ref is None else acc + bias_ref[:, cols]


A_CAUSAL = [A_HALO - (A_TAPS - 1) + k for k in range(A_TAPS)]
A_ANTICAUSAL = [A_TAPS - 1 - k for k in range(A_TAPS)]


def _layer_norm_stats(x):
    mu = jnp.mean(x, axis=-1, keepdims=True)
    xc = x - mu
    rstd = lax.rsqrt(jnp.mean(xc * xc, axis=-1, keepdims=True) + LN_EPS)
    return xc * rstd, rstd


def _mixer_specs(seq, tt):
    tiles_per_seq = seq // tt
    halo_blocks = tt // A_HALO
    z_spec = pl.BlockSpec((tt, IN_EVEN), lambda i: (i, 0))
    zh_spec = pl.BlockSpec((A_HALO, IN_EVEN), lambda i: (jnp.maximum(i * halo_blocks - 1, 0), 0))
    return tiles_per_seq, z_spec, zh_spec


def _vec_spec(rows, cols):
    return pl.BlockSpec((rows, cols), lambda i: (0, 0))


def _mixer_fwd(z, seq, caw, cab, lag, lab, cbw, comm=None):
    t = z.shape[0]
    tt = _tile(seq, 256)
    tiles_per_seq, z_spec, zh_spec = _mixer_specs(seq, tt)

    def body(z_ref, zh_ref, caw_ref, cab_ref, lag_ref, lab_ref, cbw_ref, mix_ref, a1_s, cb_s, sh_s, a2_s):
        first = (pl.program_id(0) % tiles_per_seq) == 0
        _mixer_windows(z_ref, zh_ref, first, a1_s, cb_s, tt)
        _shifted_copies(a1_s, sh_s, tt)
        _conv_taps(a1_s, sh_s, caw_ref, A_CAUSAL, a2_s, tt, bias_ref=cab_ref)
        xhat, _ = _layer_norm_stats(a2_s[...])
        a3 = xhat * lag_ref[...] + lab_ref[...]
        mix_ref[:, 0:A_DIM] = (a3 * _sigmoid(a3)).astype(BF16)
        cv = _causal_conv(cb_s, cbw_ref, B_TAPS, B_HALO, tt)
        mix_ref[:, A_DIM:A_DIM + B_DIM] = (z_ref[:, 2 * A_DIM:3 * A_DIM] * cv).astype(BF16)

    return _call(
        body, name="mixer_fwd", grid=(t // tt,),
        in_specs=[z_spec, zh_spec, _vec_spec(A_TAPS, A_DIM), _vec_spec(1, A_DIM), _vec_spec(1, A_DIM), _vec_spec(1, A_DIM),
                  _vec_spec(B_TAPS, B_DIM)],
        out_specs=[pl.BlockSpec((tt, A_DIM + B_DIM), lambda i: (i, 0))],
        out_shape=[jax.ShapeDtypeStruct((t, A_DIM + B_DIM), BF16)],
        scratch_shapes=[pltpu.VMEM((A_HALO + tt, A_DIM), F32), pltpu.VMEM((B_HALO + tt, B_DIM), F32),
                        pltpu.VMEM((SUBLANES - 1, tt + SHIFT_ROWS, A_DIM), F32), pltpu.VMEM((tt, A_DIM), F32)],
        args=(z, z, caw, cab, lag, lab, cbw), parallel=True, comm=comm)[0]


def _mixer_bwd_local(z, dmix, seq, caw, cab, lag, lab, cbw, comm=None):
    t = z.shape[0]
    tt = _tile(seq, 256)
    tiles_per_seq, z_spec, zh_spec = _mixer_specs(seq, tt)

    def body(z_ref, zh_ref, dmix_ref, caw_ref, cab_ref, lag_ref, lab_ref, cbw_ref,
             da2_ref, dcv_ref, dcaw_ref, dcab_ref, dlag_ref, dlab_ref, dcbw_ref, a1_s, cb_s, sh_s, a2_s):
        @pl.when(pl.program_id(0) == 0)
        def _():
            for ref in (dcaw_ref, dcab_ref, dlag_ref, dlab_ref, dcbw_ref):
                ref[...] = jnp.zeros_like(ref)

        first = (pl.program_id(0) % tiles_per_seq) == 0
        _mixer_windows(z_ref, zh_ref, first, a1_s, cb_s, tt)
        _shifted_copies(a1_s, sh_s, tt)
        _conv_taps(a1_s, sh_s, caw_ref, A_CAUSAL, a2_s, tt, bias_ref=cab_ref)
        xhat, rstd = _layer_norm_stats(a2_s[...])
        a3 = xhat * lag_ref[...] + lab_ref[...]
        s3 = _sigmoid(a3)
        da3 = dmix_ref[:, 0:A_DIM] * (s3 * (1.0 + a3 * (1.0 - s3)))
        dlag_ref[...] += _colsum(da3 * xhat)
        dlab_ref[...] += _colsum(da3)
        dxh = da3 * lag_ref[...]
        da2 = rstd * (dxh - jnp.mean(dxh, axis=-1, keepdims=True) - xhat * jnp.mean(dxh * xhat, axis=-1, keepdims=True))
        da2_ref[...] = da2
        dcab_ref[...] += _colsum(da2)
        rb, blocks = _blocks(tt)
        for r, cols in blocks:
            da2_b = da2_ref[r:r + rb, cols]
            for k in range(A_TAPS):
                dcaw_ref[k:k + 1, cols] += _colsum(da2_b * _window_rows(a1_s, sh_s, r + A_CAUSAL[k], rb, cols))
        dcv =dmix_ref[:, A_DIM:A_DIM + B_DIM] * z_ref[:, 2 * A_DIM:3 * A_DIM]
        dcv_ref[...] = dcv
        for k in range(B_TAPS):
            dcbw_ref[k:k + 1, :] += _colsum(dcv * cb_s[pl.ds(B_HALO - (B_TAPS - 1) + k, tt), :])

    half = pl.BlockSpec((tt, A_DIM), lambda i: (i, 0))
    return _call(
        body, name="mixer_bwd_local", grid=(t // tt,),
        in_specs=[z_spec, zh_spec, pl.BlockSpec((tt, A_DIM + B_DIM), lambda i: (i, 0)),
                  _vec_spec(A_TAPS, A_DIM), _vec_spec(1, A_DIM), _vec_spec(1, A_DIM), _vec_spec(1, A_DIM), _vec_spec(B_TAPS, B_DIM)],
        out_specs=[half, half, _vec_spec(A_TAPS, A_DIM), _vec_spec(1, A_DIM), _vec_spec(1, A_DIM), _vec_spec(1, A_DIM),
                   _vec_spec(B_TAPS, B_DIM)],
        out_shape=[jax.ShapeDtypeStruct((t, A_DIM), F32), jax.ShapeDtypeStruct((t, B_DIM), F32),
                   jax.ShapeDtypeStruct((A_TAPS, A_DIM), F32), jax.ShapeDtypeStruct((1, A_DIM), F32),
                   jax.ShapeDtypeStruct((1, A_DIM), F32), jax.ShapeDtypeStruct((1, A_DIM), F32),
                   jax.ShapeDtypeStruct((B_TAPS, B_DIM), F32)],
        scratch_shapes=[pltpu.VMEM((A_HALO + tt, A_DIM), F32), pltpu.VMEM((B_HALO + tt, B_DIM), F32),
                        pltpu.VMEM((SUBLANES - 1, tt + SHIFT_ROWS, A_DIM), F32), pltpu.VMEM((tt, A_DIM), F32)],
        args=(z, z, dmix, caw, cab, lag, lab, cbw), comm=comm)


def _mixer_bwd_input(z, dmix, da2, dcv, seq, caw, cbw, comm=None):
    t = z.shape[0]
    tt = _tile(seq, 256)
    tiles_per_seq, z_spec, zh_spec = _mixer_specs(seq, tt)
    a_blocks = tt // A_HALO
    b_blocks = tt // B_HALO
    last_a = t // A_HALO - 1
    last_b = t // B_HALO - 1

    def body(z_ref, zh_ref, dmix_ref, da2_ref, da2n_ref, dcv_ref, dcvn_ref, caw_ref, cbw_ref, dz_ref, a1_s, cb_s, da2_s, dcv_s,
             sh_s, da1_s):
        pos = pl.program_id(0) % tiles_per_seq
        first = pos == 0
        last = pos == tiles_per_seq - 1
        sig = _mixer_windows(z_ref, zh_ref, first, a1_s, cb_s, tt)
        da2_s[0:tt, :] = da2_ref[...]
        da2_s[tt:tt + A_HALO, :] = jnp.where(last, 0.0, da2n_ref[...])
        dcv_s[0:tt, :] = dcv_ref[...]
        dcv_s[tt:tt + B_HALO, :] = jnp.where(last, 0.0, dcvn_ref[...])
        _shifted_copies(da2_s, sh_s, tt)
        _conv_taps(da2_s, sh_s, caw_ref, A_ANTICAUSAL, da1_s, tt)
        da1 = da1_s[...]
        dz_ref[:, 0:A_DIM] = (da1 * sig).astype(BF16)
        dz_ref[:, A_DIM:2 * A_DIM] = (da1 * z_ref[:, 0:A_DIM] * sig * (1.0 - sig)).astype(BF16)
        cv = _causal_conv(cb_s, cbw_ref, B_TAPS, B_HALO, tt)
        dz_ref[:, 2 * A_DIM:3 * A_DIM] = (dmix_ref[:, A_DIM:A_DIM + B_DIM] * cv).astype(BF16)
        dcb = cbw_ref[0:1, :] * dcv_s[pl.ds(B_TAPS - 1, tt), :]
        for k in range(1, B_TAPS):
            dcb = dcb + cbw_ref[k:k + 1, :] * dcv_s[pl.ds(B_TAPS - 1 - k, tt), :]
        dz_ref[:, 3 * A_DIM:4 * A_DIM] = (dcb * z_ref[:, 4 * A_DIM:5 * A_DIM]).astype(BF16)
        dz_ref[:, 4 * A_DIM:5 * A_DIM] = (dcb * z_ref[:, 3 * A_DIM:4 * A_DIM]).astype(BF16)

    half = pl.BlockSpec((tt, A_DIM), lambda i: (i, 0))
    return _call(
        body, name="mixer_bwd_input", grid=(t // tt,),
        in_specs=[z_spec, zh_spec, pl.BlockSpec((tt, A_DIM + B_DIM), lambda i: (i, 0)),
                  half, pl.BlockSpec((A_HALO, A_DIM), lambda i: (jnp.minimum((i + 1) * a_blocks, last_a), 0)),
                  half, pl.BlockSpec((B_HALO, B_DIM), lambda i: (jnp.minimum((i + 1) * b_blocks, last_b), 0)),
                  _vec_spec(A_TAPS, A_DIM), _vec_spec(B_TAPS, B_DIM)],
        out_specs=[pl.BlockSpec((tt, IN_EVEN), lambda i: (i, 0))],
        out_shape=[jax.ShapeDtypeStruct((t, IN_EVEN), BF16)],
        scratch_shapes=[pltpu.VMEM((A_HALO + tt, A_DIM), F32), pltpu.VMEM((B_HALO + tt, B_DIM), F32),
                        pltpu.VMEM((tt + A_HALO, A_DIM), F32), pltpu.VMEM((tt + B_HALO, B_DIM), F32),
                        pltpu.VMEM((SUBLANES - 1, tt + SHIFT_ROWS, A_DIM), F32), pltpu.VMEM((tt, A_DIM), F32)],
        args=(z, z, dmix, da2, da2, dcv, dcv, caw, cbw), parallel=True, comm=comm)[0]


def _tril_ws(ws_ref, g):
    rows = lax.broadcasted_iota(jnp.int32, (CHUNK, CHUNK), 0)
    cols = lax.broadcasted_iota(jnp.int32, (CHUNK, CHUNK), 1)
    return jnp.where(rows >= cols, ws_ref[g], 0.0).astype(BF16), rows >= cols


def _sgu_fwd(pre, lvg, lvb, ws, bs_b, comm=None):
    t = pre.shape[0]
    tt = _tile(t, 256)

    def body(pre_ref, lvg_ref, lvb_ref, ws_ref, bsb_ref, y_ref):
        vhat, _ = _layer_norm_stats(_gelu(pre_ref[:, C_DIM:2 * C_DIM]))
        vl = (vhat * lvg_ref[...] + lvb_ref[...]).astype(BF16)
        for g in range(C_GROUPS):
            w, _ = _tril_ws(ws_ref, g)
            cols = slice(g * CHUNK, (g + 1) * CHUNK)
            for ci in range(tt // CHUNK):
                rows = slice(ci * CHUNK, (ci + 1) * CHUNK)
                sv = jnp.dot(w, vl[rows, cols], preferred_element_type=F32) + bsb_ref[g]
                y_ref[rows, cols] = (_gelu(pre_ref[rows, cols]) * sv).astype(BF16)

    group = pl.BlockSpec((C_GROUPS, CHUNK, CHUNK), lambda i: (0, 0, 0))
    return _call(
        body, name="sgu_fwd", grid=(t // tt,),
        in_specs=[pl.BlockSpec((tt, 2 * C_DIM), lambda i: (i, 0)), _vec_spec(1, C_DIM), _vec_spec(1, C_DIM), group, group],
        out_specs=[pl.BlockSpec((tt, C_DIM), lambda i: (i, 0))], out_shape=[jax.ShapeDtypeStruct((t, C_DIM), BF16)],
        args=(pre, lvg, lvb, ws, bs_b), parallel=True, comm=comm)[0]


def _sgu_bwd(pre, dy, lvg, lvb, ws, bs_b, comm=None):
    t = pre.shape[0]
    tt = _tile(t, 256)

    def body(pre_ref, dy_ref, lvg_ref, lvb_ref, ws_ref, bsb_ref, dpre_ref, dws_ref, dbsb_ref, dlvg_ref, dlvb_ref, dbin_ref,
             dvl_s):
        @pl.when(pl.program_id(0) == 0)
        def _():
            for ref in (dws_ref, dbsb_ref, dlvg_ref, dlvb_ref, dbin_ref):
                ref[...] = jnp.zeros_like(ref)

        pre_v = pre_ref[:, C_DIM:2 * C_DIM]
        vhat, rstd = _layer_norm_stats(_gelu(pre_v))
        vl = (vhat * lvg_ref[...] + lvb_ref[...]).astype(BF16)
        for g in range(C_GROUPS):
            w, keep = _tril_ws(ws_ref, g)
            cols = slice(g * CHUNK, (g + 1) * CHUNK)
            dws = jnp.zeros((CHUNK, CHUNK), F32)
            dbs = jnp.zeros((CHUNK, 1), F32)
            for ci in range(tt // CHUNK):
                rows = slice(ci * CHUNK, (ci + 1) * CHUNK)
                vl_g = vl[rows, cols]
                sv = jnp.dot(w, vl_g, preferred_element_type=F32) + bsb_ref[g]
                pre_u = pre_ref[rows, cols]
                dyv = dy_ref[rows, cols]
                du = dyv * sv * _gelu_grad(pre_u)
                dpre_ref[rows, cols] = du.astype(BF16)
                dbin_ref[:, cols] += _colsum(du)
                dsv = dyv * _gelu(pre_u)
                dbs = dbs + jnp.sum(dsv, axis=1, keepdims=True)
                dsv16 = dsv.astype(BF16)
                dws = dws + lax.dot_general(dsv16, vl_g, (((1,), (1,)), ((), ())), preferred_element_type=F32)
                dvl_s[rows, cols] = lax.dot_general(w, dsv16, (((0,), (0,)), ((), ())), preferred_element_type=F32)
            dws_ref[g] += jnp.where(keep, dws, 0.0)
            dbsb_ref[g] += dbs
        dvl = dvl_s[...]
        dlvg_ref[...] += _colsum(dvl * vhat)
        dlvb_ref[...] += _colsum(dvl)
        dxh = dvl * lvg_ref[...]
        dv = rstd * (dxh - jnp.mean(dxh, axis=-1, keepdims=True) - vhat * jnp.mean(dxh * vhat, axis=-1, keepdims=True))
        dpv = dv * _gelu_grad(pre_v)
        dpre_ref[:, C_DIM:2 * C_DIM] = dpv.astype(BF16)
        dbin_ref[:, C_DIM:2 * C_DIM] += _colsum(dpv)

    group = pl.BlockSpec((C_GROUPS, CHUNK, CHUNK), lambda i: (0, 0, 0))
    return _call(
        body, name="sgu_bwd", grid=(t // tt,),
        in_specs=[pl.BlockSpec((tt, 2 * C_DIM), lambda i: (i, 0)), pl.BlockSpec((tt, C_DIM), lambda i: (i, 0)),
                  _vec_spec(1, C_DIM), _vec_spec(1, C_DIM), group, group],
        out_specs=[pl.BlockSpec((tt, 2 * C_DIM), lambda i: (i, 0)), group, group,
                   _vec_spec(1, C_DIM), _vec_spec(1, C_DIM), _vec_spec(1, 2 * C_DIM)],
        out_shape=[jax.ShapeDtypeStruct((t, 2 * C_DIM), BF16), jax.ShapeDtypeStruct((C_GROUPS, CHUNK, CHUNK), F32),
                   jax.ShapeDtypeStruct((C_GROUPS, CHUNK, CHUNK), F32), jax.ShapeDtypeStruct((1, C_DIM), F32),
                   jax.ShapeDtypeStruct((1, C_DIM), F32), jax.ShapeDtypeStruct((1, 2 * C_DIM), F32)],
        scratch_shapes=[pltpu.VMEM((tt, C_DIM), F32)],
        args=(pre, dy, lvg, lvb, ws, bs_b), comm=comm)


def _pair_sum(name, part, got, core):
    _, k, n = part.shape
    tk = _tile(k, 256)

    def body(core_ref, p_ref, s_ref, o_ref):
        o_ref[...] = (p_ref[...] + s_ref[...]).astype(BF16)

    return pl.pallas_call(
        body, name=name,
        grid_spec=pltpu.PrefetchScalarGridSpec(
            num_scalar_prefetch=1, grid=(N_CHIP, k // tk),
            in_specs=[pl.BlockSpec((None, tk, n), lambda q, i, core_ref: (2 * q + core_ref[0], i, 0)),
                      pl.BlockSpec((None, tk, n), lambda q, i, core_ref: (q, i, 0))],
            out_specs=pl.BlockSpec((None, tk, n), lambda q, i, core_ref: (q, i, 0))),
        out_shape=jax.ShapeDtypeStruct((N_CHIP, k, n), BF16),
        compiler_params=pltpu.CompilerParams(dimension_semantics=("parallel", "parallel"), vmem_limit_bytes=VMEM_LIMIT_BYTES),
    )(core, part, got)


def _adamw_math(w, g, m, v):
    m = ADAM_B1 * m + (1.0 - ADAM_B1) * g
    v = ADAM_B2 * v + (1.0 - ADAM_B2) * (g * g)
    m_hat = m / (1.0 - ADAM_B1 ** ADAM_STEP)
    v_hat = v / (1.0 - ADAM_B2 ** ADAM_STEP)
    delta = -ADAM_LR * (m_hat / (jnp.sqrt(v_hat) + ADAM_EPS) + ADAM_WD * w)
    return delta, m, v


def _sum_adamw(name, parts, w, m, v, comm=None):
    layers = len(parts)
    n_parts, k, n = parts[0].shape
    tk = _tile(k, 256)

    def body(*refs):
        p_refs = refs[:layers]
        w_ref, m_ref, v_ref, g_ref, d_ref, nm_ref, nv_ref = refs[layers:]

        def total(p_ref):
            g = p_ref[0].astype(F32)
            for q in range(1, n_parts):
                g = g + p_ref[q].astype(F32)
            return g

        g = total(p_refs[0])
        for l in range(1, layers):
            g = jnp.where(pl.program_id(0) == l, total(p_refs[l]), g)
        g_ref[...] = g
        d_ref[...], nm_ref[...], nv_ref[...] = _adamw_math(w_ref[...], g, m_ref[...], v_ref[...])

    blk = pl.BlockSpec((None, tk, n), lambda l, i: (l, i, 0))
    return _call(
        body, name=name, grid=(layers, k // tk),
        in_specs=[pl.BlockSpec((n_parts, tk, n), lambda l, i: (0, i, 0))] * layers + [blk, blk, blk], out_specs=[blk] * 4,
        out_shape=[jax.ShapeDtypeStruct((layers, k, n), F32)] * 4, args=(*parts, w, m, v), parallel=True, comm=comm)


def _small_adamw(name, parts, w, m, v, losses):
    count = len(parts)

    def in_order(ref):
        total = ref[0]
        for dev in range(1, N_DEV):
            total = total + ref[dev]
        return total

    def body(*refs):
        p_refs, w_refs, m_refs, v_refs = (refs[j * count:(j + 1) * count] for j in range(4))
        losses_ref = refs[4 * count]
        g_refs, d_refs, nm_refs, nv_refs = (refs[4 * count + 1 + j * count:4 * count + 1 + (j + 1) * count] for j in range(4))
        loss_ref = refs[8 * count + 1]
        for i in range(count):
            g = in_order(p_refs[i])
            g_refs[i][...] = g
            d_refs[i][...], nm_refs[i][...], nv_refs[i][...] = _adamw_math(w_refs[i][...], g, m_refs[i][...], v_refs[i][...])
        loss_ref[...] = in_order(losses_ref)

    res = pl.pallas_call(
        body, name=name,
        out_shape=[jax.ShapeDtypeStruct(a.shape, F32) for a in w] * 4 + [jax.ShapeDtypeStruct(losses.shape[1:], F32)],
        compiler_params=pltpu.CompilerParams(vmem_limit_bytes=VMEM_LIMIT_BYTES))(*parts, *w, *m, *v, losses)
    return [res[j * count:(j + 1) * count] for j in range(4)], res[4 * count]


def _rows(a):
    return a.reshape(-1, a.shape[-1])


def _whole(gathered):
    return jnp.transpose(gathered, (1, 0, 2)).reshape(gathered.shape[1], -1)


SMALL =("ev_norm_g", "ev_conv_a_w", "ev_conv_a_b", "ev_ln_a_g", "ev_ln_a_b", "ev_conv_b_w", "od_norm_g", "od_b_in",
         "od_ln_v_g", "od_ln_v_b", "od_w_s", "od_b_s", "mlp_norm_g", "final_norm_g")
SMALL_SHARDED = ("ev_conv_a_w", "ev_conv_b_w", "od_norm_g", "od_b_in", "od_ln_v_g", "od_ln_v_b")
ORDER = ("ev_norm_g", "ev_w_in", "ev_conv_a_w", "ev_conv_a_b", "ev_ln_a_g", "ev_ln_a_b", "ev_conv_b_w", "ev_w_out",
         "od_norm_g", "od_w_in", "od_b_in", "od_ln_v_g", "od_ln_v_b", "od_w_s", "od_b_s", "od_w_out", "mlp_norm_g",
         "mlp_w1", "mlp_w2", "final_norm_g")


def kernel(x, ev_norm_g, ev_w_in, ev_conv_a_w, ev_conv_a_b, ev_ln_a_g, ev_ln_a_b, ev_conv_b_w, ev_w_out, od_norm_g, od_w_in, od_b_in, od_ln_v_g, od_ln_v_b, od_w_s, od_b_s, od_w_out, mlp_norm_g, mlp_w1, mlp_w2, final_norm_g, loss_target, m_ev_norm_g, m_ev_w_in, m_ev_conv_a_w, m_ev_conv_a_b, m_ev_ln_a_g, m_ev_ln_a_b, m_ev_conv_b_w, m_ev_w_out, m_od_norm_g, m_od_w_in, m_od_b_in, m_od_ln_v_g, m_od_ln_v_b, m_od_w_s, m_od_b_s, m_od_w_out, m_mlp_norm_g, m_mlp_w1, m_mlp_w2, m_final_norm_g, v_ev_norm_g, v_ev_w_in, v_ev_conv_a_w, v_ev_conv_a_b, v_ev_ln_a_g, v_ev_ln_a_b, v_ev_conv_b_w, v_ev_w_out, v_od_norm_g, v_od_w_in, v_od_b_in, v_od_ln_v_g, v_od_ln_v_b, v_od_w_s, v_od_b_s, v_od_w_out, v_mlp_norm_g, v_mlp_w1, v_mlp_w2, v_final_norm_g):
    W = dict(ev_norm_g=ev_norm_g, ev_w_in=ev_w_in, ev_conv_a_w=ev_conv_a_w, ev_conv_a_b=ev_conv_a_b, ev_ln_a_g=ev_ln_a_g,
             ev_ln_a_b=ev_ln_a_b, ev_conv_b_w=ev_conv_b_w, ev_w_out=ev_w_out, od_norm_g=od_norm_g, od_w_in=od_w_in,
             od_b_in=od_b_in, od_ln_v_g=od_ln_v_g, od_ln_v_b=od_ln_v_b, od_w_s=od_w_s, od_b_s=od_b_s, od_w_out=od_w_out,
             mlp_norm_g=mlp_norm_g, mlp_w1=mlp_w1, mlp_w2=mlp_w2, final_norm_g=final_norm_g)
    M = dict(ev_norm_g=m_ev_norm_g, ev_w_in=m_ev_w_in, ev_conv_a_w=m_ev_conv_a_w, ev_conv_a_b=m_ev_conv_a_b,
             ev_ln_a_g=m_ev_ln_a_g, ev_ln_a_b=m_ev_ln_a_b, ev_conv_b_w=m_ev_conv_b_w, ev_w_out=m_ev_w_out,
             od_norm_g=m_od_norm_g, od_w_in=m_od_w_in, od_b_in=m_od_b_in, od_ln_v_g=m_od_ln_v_g, od_ln_v_b=m_od_ln_v_b,
             od_w_s=m_od_w_s, od_b_s=m_od_b_s, od_w_out=m_od_w_out, mlp_norm_g=m_mlp_norm_g, mlp_w1=m_mlp_w1,
             mlp_w2=m_mlp_w2, final_norm_g=m_final_norm_g)
    V = dict(ev_norm_g=v_ev_norm_g, ev_w_in=v_ev_w_in, ev_conv_a_w=v_ev_conv_a_w, ev_conv_a_b=v_ev_conv_a_b,
             ev_ln_a_g=v_ev_ln_a_g, ev_ln_a_b=v_ev_ln_a_b, ev_conv_b_w=v_ev_conv_b_w, ev_w_out=v_ev_w_out,
             od_norm_g=v_od_norm_g, od_w_in=v_od_w_in, od_b_in=v_od_b_in, od_ln_v_g=v_od_ln_v_g, od_ln_v_b=v_od_ln_v_b,
             od_w_s=v_od_w_s, od_b_s=v_od_b_s, od_w_out=v_od_w_out, mlp_norm_g=v_mlp_norm_g, mlp_w1=v_mlp_w1,
             mlp_w2=v_mlp_w2, final_norm_g=v_final_norm_g)

    n_seq, seq, d = x.shape
    t = n_seq * seq
    dev = 4 * lax.axis_index("x") + 2 * lax.axis_index("y") + lax.axis_index("c")
    core = lax.axis_index("c").astype(jnp.int32).reshape(1)

    ev_g, cab, lag, lab = W["ev_norm_g"], W["ev_conv_a_b"], W["ev_ln_a_g"], W["ev_ln_a_b"]
    ws = W["od_w_s"][0]
    bs_b = jnp.broadcast_to(W["od_b_s"][0][:, :, None], (C_GROUPS, CHUNK, CHUNK))
    mlp_g = [W["mlp_norm_g"][l:l + 1] for l in range(2)]
    fin_g = W["final_norm_g"].reshape(1, d)

    def w16(name, l=0):
        return W[name][l].astype(BF16)

    h0 = x.reshape(t, d)
    gather = _gather_comm([w16("ev_w_in")] + [_rows(W[n]) for n in SMALL_SHARDED])
    n0 = _rms_fwd("ev_norm", h0, ev_g, comm=gather)
    w_ev_in, caw, cbw, od_g, od_bin, lvg, lvb = [_whole(g) for g in gather.out]

    first, second = (0, 2), (1, 2)
    w1_0, w2_0, w1_1, w2_1 = w16("mlp_w1", 0), w16("mlp_w2", 0), w16("mlp_w1", 1), w16("mlp_w2", 1)

    g_a, g_b = _gather_comm([w16("ev_w_out")]), _gather_comm([w1_0], first)
    z = _mm_nn("ev_in", n0, w_ev_in, IN_EVEN, _ep_store, [F32], comm=[g_a, g_b])[0]
    w_ev_out = g_a.out[0].reshape(D_MODEL, D_MODEL)

    g_c, g_d = _gather_comm([w1_0], second, into=g_b.out), _gather_comm([w2_0], first)
    mix = _mixer_fwd(z, seq, caw, cab, lag, lab, cbw, comm=[g_c, g_d])
    w1 = [_whole(g_c.out[0]), None]

    g_e = _gather_comm([w2_0], second, into=g_d.out)
    h1, n1 = _mm_nn("ev_out", mix, w_ev_out, d, _ep_residual_norm, [F32, BF16], extras=(h0,), rows=(mlp_g[0],), comm=g_e)
    w2 = [g_e.out[0].reshape(D_FF, D_MODEL), None]

    g_f = _gather_comm([w16("od_w_in"), w16("od_w_out")])
    q0 = _mm_nn("mlp0_up", n1, w1[0], D_FF, _ep_relu_sq, [BF16], comm=g_f)[0]
    w_od_in = _whole(g_f.out[0])
    w_od_out = g_f.out[1].reshape(D_MODEL, D_MODEL)

    g_g = _gather_comm([w1_1], first)
    h2, n2 = _mm_nn("mlp0_down", q0, w2[0], d, _ep_residual_norm, [F32, BF16], extras=(h1,), rows=(od_g,), comm=g_g)
    g_h = _gather_comm([w1_1], second, into=g_g.out)
    pre = _mm_nn("od_in", n2, w_od_in, 2 * C_DIM, _ep_bias, [F32], rows=(od_bin,), comm=g_h)[0]
    w1[1] = _whole(g_h.out[0])
    g_i = _gather_comm([w2_1], first)
    y = _sgu_fwd(pre, lvg, lvb, ws, bs_b, comm=g_i)
    h3, n3 = _mm_nn("od_out", y, w_od_out, d, _ep_residual_norm, [F32, BF16], extras=(h2,), rows=(mlp_g[1],))
    g_j = _gather_comm([w2_1], second, into=g_i.out)
    q1 = _mm_nn("mlp1_up", n3, w1[1], D_FF, _ep_relu_sq, [BF16], comm=g_j)[0]
    w2[1] = g_j.out[0].reshape(D_FF, D_MODEL)
    grad, grad16, d_fin_g, loss_part = _mm_nn(
        "mlp1_down", q1, w2[1], d, _ep_final_loss, [F32, BF16], extras=(h3, loss_target.reshape(t, d)), rows=(fin_g,),
        sums=(True, False), tm_want=FUSED_ROWS)

    by_chip = {}

    def swap(name, parts):
        comm = _pair_comm([parts])
        comm.parts, comm.weight = parts, name
        return comm

    def exchange(swapped, halves=False):
        sums = _pair_sum(f"pair_sum_{swapped.weight}", swapped.parts, swapped.out[0], core)
        if not halves:
            comm = _chip_comm([sums])
            comm.weight = swapped.weight
            return comm
        comm = _chip_comm([sums], first)
        comm.sums, comm.weight = sums, swapped.weight
        return comm

    def rest(comm):
        other = _chip_comm([comm.sums], second, into=comm.out)
        other.weight = comm.weight
        return other

    def done(comm):
        by_chip[comm.weight] = comm.out[0]

    dw2_1 = _mm_tn("mlp1_dw2", q1, grad16).reshape(N_DEV, D_FF // N_DEV, D_MODEL)
    s_a = swap("w2_1", dw2_1)
    dp = _mm_nt("mlp1_dq", grad16, w2[1], D_FF, _ep_relu_sq_grad, [BF16], extras=(q1,), comm=s_a)[0]
    c_a = exchange(s_a, halves=True)
    dw1_1 = _mm_tn("mlp1_dw1", n3, dp, col_blocks=N_DEV, comm=c_a)
    c_a2, s_b = rest(c_a), swap("w1_1", dw1_1)
    dn = _mm_nt("mlp1_dn", dp, w1[1], d, _ep_store, [F32], comm=[c_a2, s_b])[0]
    done(c_a2)
    c_b = exchange(s_b, halves=True)
    grad, grad16, dg_mlp1 = _rms_bwd("mlp1_dn_norm", dn, h3, mlp_g[1], grad)
    d_od_out = _mm_tn("od_dw_out", y, grad16).reshape(N_DEV, D_MODEL // N_DEV, D_MODEL)
    s_c = swap("od_out", d_od_out)
    dy = _mm_nt("od_dy", grad16, w_od_out, C_DIM, _ep_store, [F32], comm=s_c)[0]
    c_c = exchange(s_c)
    dpre, d_ws, d_bsb, d_lvg, d_lvb, d_bin = _sgu_bwd(pre, dy, lvg, lvb, ws, bs_b, comm=[c_b, c_c])
    done(c_c)
    c_b2 = rest(c_b)
    d_od_in = _mm_tn("od_dw_in", n2, dpre, col_blocks=N_DEV, comm=c_b2)
    done(c_b2)
    s_d = swap("od_in", d_od_in)
    dn = _mm_nt("od_dn", dpre, w_od_in, d, _ep_store, [F32], comm=s_d)[0]
    c_d = exchange(s_d)
    grad, grad16, d_od_g = _rms_bwd("od_dn_norm", dn, h2, od_g, grad)
    dw2_0 = _mm_tn("mlp0_dw2", q0, grad16, comm=c_d).reshape(N_DEV, D_FF // N_DEV, D_MODEL)
    done(c_d)
    s_e = swap("w2_0", dw2_0)
    dp = _mm_nt("mlp0_dq", grad16, w2[0], D_FF, _ep_relu_sq_grad, [BF16], extras=(q0,), comm=s_e)[0]
    c_e = exchange(s_e, halves=True)
    dw1_0 = _mm_tn("mlp0_dw1", n1, dp, col_blocks=N_DEV, comm=c_e)
    c_e2, s_f = rest(c_e), swap("w1_0", dw1_0)
    dn = _mm_nt("mlp0_dn", dp, w1[0], d, _ep_store, [F32], comm=[c_e2, s_f])[0]
    done(c_e2)
    c_f = exchange(s_f)
    grad, grad16, dg_mlp0 = _rms_bwd("mlp0_dn_norm", dn, h1, mlp_g[0], grad)
    d_ev_out = _mm_tn("ev_dw_out", mix, grad16).reshape(N_DEV, D_MODEL // N_DEV, D_MODEL)
    s_g = swap("ev_out", d_ev_out)
    dmix = _mm_nt("ev_dmix", grad16, w_ev_out, A_DIM + B_DIM, _ep_store, [F32], comm=s_g)[0]
    c_g = exchange(s_g)
    da2, dcv, d_caw, d_cab, d_lag, d_lab, d_cbw = _mixer_bwd_local(z, dmix, seq, caw, cab, lag, lab, cbw, comm=[c_f, c_g])
    done(c_f)
    done(c_g)
    dz = _mixer_bwd_input(z, dmix, da2, dcv, seq, caw, cbw)
    small_parts = dict(
        ev_conv_a_w=d_caw, ev_conv_a_b=d_cab, ev_ln_a_g=d_lag, ev_ln_a_b=d_lab,
        ev_conv_b_w=d_cbw, od_norm_g=d_od_g, od_b_in=d_bin, od_ln_v_g=d_lvg, od_ln_v_b=d_lvb,
        od_w_s=d_ws.reshape(C_GROUPS * CHUNK, CHUNK), od_b_s=d_bsb[:, :, 0],
        mlp_norm_g=jnp.concatenate([dg_mlp0, dg_mlp1], axis=0), final_norm_g=d_fin_g)
    early = [n for n in SMALL if n != "ev_norm_g"]
    small_gather = _gather_comm([small_parts[n] for n in early] + [jnp.broadcast_to(loss_part, (SUBLANES, LANE_BLOCK))])
    d_ev_in = _mm_tn("ev_dw_in", n0, dz, comm=small_gather)
    d_ev_in = jnp.transpose(d_ev_in.reshape(D_MODEL, N_DEV, IN_EVEN // N_DEV), (1, 0, 2))
    s_h = swap("ev_in", d_ev_in)
    dn = _mm_nt("ev_dn", dz, w_ev_in, d, _ep_store, [F32], comm=s_h)[0]
    c_h = exchange(s_h)
    grad_x, _, d_ev_g = _rms_bwd("ev_dn_norm", dn, h0, ev_g, grad)

    last_gather = _gather_comm([d_ev_g])
    shard = {"od_w_out": ("od_out",), "mlp_w1": ("w1_0", "w1_1"), "mlp_w2": ("w2_0", "w2_1"), "od_w_in": ("od_in",),
             "ev_w_out": ("ev_out",), "ev_w_in": ("ev_in",)}
    carried = {"od_w_out": [c_h, last_gather]}
    out_g, out_d, out_m, out_v = {}, {}, {}, {}
    for name, keys in shard.items():
        out_g[name], out_d[name], out_m[name], out_v[name] = _sum_adamw(
            f"adamw_{name}", [by_chip[key] for key in keys], W[name], M[name], V[name], comm=carried.get(name))
        if name == "od_w_out":
            done(c_h)

    gathered = dict(zip(early, small_gather.out[:-1]), ev_norm_g=last_gather.out[0])
    mine = []
    for n in SMALL:
        g = gathered[n]
        if n in SMALL_SHARDED:
            width = W[n].shape[-1]
            g = lax.dynamic_slice_in_dim(g, dev * width, width, axis=2)
        mine.append(g)
    res, loss_tile = _small_adamw("adamw_small", mine, [_rows(W[n]) for n in SMALL], [_rows(M[n]) for n in SMALL],
                                  [_rows(V[n]) for n in SMALL], small_gather.out[-1])
    loss = loss_tile[0, 0]
    for store, values in zip((out_g, out_d, out_m, out_v), res):
        for n, value in zip(SMALL, values):
            store[n] = value.reshape(W[n].shape)

    return (loss, grad_x.reshape(n_seq, seq, d), *[out_g[n] for n in ORDER], *[out_d[n] for n in ORDER],
            *[out_m[n] for n in ORDER], *[out_v[n] for n in ORDER])
```

```python
import math

import jax
import jax.numpy as jnp
from jax import lax
from jax.experimental import pallas as pl
from jax.experimental.pallas import tpu as pltpu

F32 = jnp.float32
BF16 = jnp.bfloat16
MESH = pl.DeviceIdType.MESH

D_MODEL = 1024
A_DIM = 512
B_DIM = 512
IN_EVEN = 2 * A_DIM + 3 * B_DIM
A_TAPS = 31
B_TAPS = 3
CHUNK = 128
C_GROUPS = 8
C_DIM = 1024
D_FF = 4096
RMS_EPS = 1e-6
LN_EPS = 1e-5
N_DEV = 8
N_CHIP = 4

ADAM_LR = 0.001
ADAM_B1 = 0.9
ADAM_B2 = 0.999
ADAM_EPS = 1e-08
ADAM_WD = 0.01
ADAM_STEP = 10

A_HALO = 32
B_HALO = 8
VMEM_LIMIT_BYTES = 56 * 1024 * 1024
INV_SQRT2 = 1.0 / math.sqrt(2.0)
INV_SQRT_2PI = 1.0 / math.sqrt(2.0 * math.pi)
HBM_SPEC = pl.BlockSpec(memory_space=pltpu.HBM)


def _tile(n, want):
    t = min(n, want)
    while n % t:
        t //= 2
    return t


def _sigmoid(x):
    return 1.0 / (1.0 + jnp.exp(-x))


def _gelu(x):
    return 0.5 * x * (1.0 + lax.erf(x * INV_SQRT2))


def _gelu_grad(x):
    return 0.5 * (1.0 + lax.erf(x * INV_SQRT2)) + x * jnp.exp(-0.5 * x * x) * INV_SQRT_2PI


def _colsum(x):
    return jnp.sum(x, axis=0, keepdims=True)


class _Comm:
    def __init__(self, ins, out_shapes, sem_shapes, start, finish, middle=None, into=None, sibling=False, chips=False):
        self.ins, self.out_shapes, self.sem_shapes, self.start, self.finish = ins, out_shapes, sem_shapes, start, finish
        self.middle = middle
        self.sibling, self.chips = sibling, chips
        self.into = list(into) if into is not None else []
        self.out = None


def _piece_rows(rows, piece):
    if piece is None:
        return 0, rows
    i, n = piece
    return i * (rows // n), rows // n


MIDDLE_AT = 0.75
BARRIER_IDS = {(True, True): 0, (True, False): 1, (False, True): 2}
FUSED_ROWS = 512


def _call(body, *, name, grid, in_specs, out_specs, out_shape, args, scratch_shapes=(), parallel=False, comm=None):
    comms = [] if comm is None else (list(comm) if isinstance(comm, (list, tuple)) else [comm])
    if not comms:
        sem = ("parallel" if parallel else "arbitrary",) * len(grid)
        return pl.pallas_call(
            body, name=name, grid=grid, in_specs=list(in_specs), out_specs=list(out_specs), out_shape=list(out_shape),
            scratch_shapes=list(scratch_shapes),
            compiler_params=pltpu.CompilerParams(dimension_semantics=sem, vmem_limit_bytes=VMEM_LIMIT_BYTES),
        )(*args)
    n_in, n_out, n_scr = len(in_specs), len(out_shape), len(scratch_shapes)
    c_ins_all = [a for cm in comms for a in cm.ins]
    c_into_all = [a for cm in comms for a in cm.into]
    c_out_shapes = [s for cm in comms for s in cm.out_shapes]
    c_sem_shapes = [s for cm in comms for s in cm.sem_shapes]
    aliases, in_pos, out_pos = {}, n_in + len(c_ins_all), n_out
    for cm in comms:
        for j in range(len(cm.into)):
            aliases[in_pos + j] = out_pos + j
        in_pos += len(cm.into)
        out_pos += len(cm.out_shapes)
    to_sibling = any(cm.sibling for cm in comms)
    to_chips = any(cm.chips for cm in comms)
    steps = grid
    total = math.prod(steps)
    first_step = (0,) * len(steps)
    last_step = tuple(s - 1 for s in steps)
    middle_step = None
    if 0 < int(MIDDLE_AT * total) < total - 1:
        rest, idx = int(MIDDLE_AT * total), []
        for s in reversed(steps):
            idx.append(rest % s)
            rest //= s
        middle_step = tuple(reversed(idx))

    def carrying(*refs):
        pos = 0
        ins = refs[pos:pos + n_in]; pos += n_in
        c_ins = refs[pos:pos + len(c_ins_all)]; pos += len(c_ins_all) + len(c_into_all)
        outs = refs[pos:pos + n_out]; pos += n_out
        c_outs = refs[pos:pos + len(c_out_shapes)]; pos += len(c_out_shapes)
        scr = refs[pos:pos + n_scr]; pos += n_scr
        c_sems = refs[pos:]
        views, i0, o0, s0 = [], 0, 0, 0
        for cm in comms:
            views.append((c_ins[i0:i0 + len(cm.ins)], c_outs[o0:o0 + len(cm.out_shapes)], c_sems[s0:s0 + len(cm.sem_shapes)]))
            i0, o0, s0 = i0 + len(cm.ins), o0 + len(cm.out_shapes), s0 + len(cm.sem_shapes)

        def at(step):
            hit = pl.program_id(0) == step[0]
            for axis in range(1, len(steps)):
                hit = jnp.logical_and(hit, pl.program_id(axis) == step[axis])
            return hit

        @pl.when(at(first_step))
        def _():
            x, y, c, chips = _place()
            peers = ([(x, y, 1 - c)] if to_sibling else []) + ([(*chip, c) for chip in chips] if to_chips else [])
            barrier = pltpu.get_barrier_semaphore()
            for peer in peers:
                pl.semaphore_signal(barrier, inc=1, device_id=peer, device_id_type=MESH)
            pl.semaphore_wait(barrier, len(peers))
            for cm, view in zip(comms, views):
                cm.start(*view)

        if middle_step is not None:
            @pl.when(at(middle_step))
            def _():
                for cm, view in zip(comms, views):
                    if cm.middle is not None:
                        cm.middle(*view)

        body(*ins, *outs, *scr)

        @pl.when(at(last_step))
        def _():
            for cm, view in zip(comms, views):
                if cm.middle is not None and middle_step is None:
                    cm.middle(*view)
            for cm, view in zip(comms, views):
                cm.finish(*view)

    res = pl.pallas_call(
        carrying, name=name, grid=grid,
        in_specs=[*in_specs, *[HBM_SPEC] * (len(c_ins_all) + len(c_into_all))],
        out_specs=[*out_specs, *[HBM_SPEC] * len(c_out_shapes)],
        out_shape=[*out_shape, *c_out_shapes], scratch_shapes=[*scratch_shapes, *c_sem_shapes],
        input_output_aliases=aliases,
        compiler_params=pltpu.CompilerParams(dimension_semantics=("arbitrary",) * len(grid), vmem_limit_bytes=VMEM_LIMIT_BYTES,
                                             collective_id=BARRIER_IDS[to_sibling, to_chips]),
    )(*args, *c_ins_all, *c_into_all)
    pos = n_out
    for cm in comms:
        cm.out = list(res[pos:pos + len(cm.out_shapes)])
        pos += len(cm.out_shapes)
    return list(res[:n_out])


def _place():
    x, y, c = lax.axis_index("x"), lax.axis_index("y"), lax.axis_index("c")
    return x, y, c, [(1 - x, y), (x, 1 - y), (1 - x, 1 - y)]


def _gather_comm(shards, piece=None, into=None):
    nw = len(shards)
    spans = [_piece_rows(s.shape[0], piece) for s in shards]

    def plan(ins, outs, sems):
        send_sems, recv_sems, local_sems = sems
        x, y, c, chips = _place()
        me, sibling = (x, y, c), (x, y, 1 - c)

        def slot(w, p):
            return outs[w].at[4 * p[0] + 2 * p[1] + p[2], pl.ds(*spans[w])]

        def mine(w):
            return ins[w].at[pl.ds(*spans[w])]

        def copy(w, k, block, to, src=None):
            return pltpu.make_async_remote_copy(
                src_ref=slot(w, block) if src is None else src, dst_ref=slot(w, block),
                send_sem=send_sems.at[w, k], recv_sem=recv_sems.at[w, k], device_id=to, device_id_type=MESH)

        local = [pltpu.make_async_copy(mine(w), slot(w, me), local_sems.at[w]) for w in range(nw)]
        first = [[copy(w, 0, me, sibling, src=mine(w))] + [copy(w, 1 + j, me, (*chip, c), src=mine(w)) for j, chip in enumerate(chips)]
                 for w in range(nw)]
        landed = [[copy(w, 1 + j, (*chip, c), me) for j, chip in enumerate(chips)] for w in range(nw)]
        passed = [[copy(w, 4 + j, (*chip, c), sibling) for j, chip in enumerate(chips)] for w in range(nw)]
        from_sibling = [[copy(w, 0, sibling, me)] + [copy(w, 4 + j, (*chip, 1 - c), me) for j, chip in enumerate(chips)]
                        for w in range(nw)]
        return local, first, landed, passed, from_sibling

    def start(ins, outs, sems):
        local, first, _, _, _ = plan(ins, outs, sems)
        for cp in local:
            cp.start()
        for row in first:
            for cp in row:
                cp.start()

    def middle(ins, outs, sems):
        _, _, landed, passed, _ = plan(ins, outs, sems)
        for w in range(nw):
            for j in range(3):
                landed[w][j].wait_recv()
                passed[w][j].start()

    def finish(ins, outs, sems):
        local, first, landed, passed, from_sibling = plan(ins, outs, sems)
        for w in range(nw):
            for cp in from_sibling[w]:
                cp.wait_recv()
        for w in range(nw):
            for cp in first[w] + passed[w]:
                cp.wait_send()
        for cp in local:
            cp.wait()

    return _Comm(list(shards), [jax.ShapeDtypeStruct((N_DEV, *s.shape), s.dtype) for s in shards],
                 [pltpu.SemaphoreType.DMA((nw, 7)), pltpu.SemaphoreType.DMA((nw, 7)), pltpu.SemaphoreType.DMA((nw,))],
                 start, finish, middle, into=into, sibling=True, chips=True)


def _pair_comm(parts):
    nw = len(parts)

    def plan(ins, outs, sems):
        send_sems, recv_sems = sems
        x, y, c, _ = _place()
        return [pltpu.make_async_remote_copy(
            src_ref=ins[w].at[2 * q + 1 - c], dst_ref=outs[w].at[q], send_sem=send_sems.at[w, q], recv_sem=recv_sems.at[w, q],
            device_id=(x, y, 1 - c), device_id_type=MESH) for w in range(nw) for q in range(N_CHIP)]

    def start(ins, outs, sems):
        for cp in plan(ins, outs, sems):
            cp.start()

    def finish(ins, outs, sems):
        for cp in plan(ins, outs, sems):
            cp.wait()

    return _Comm(list(parts), [jax.ShapeDtypeStruct((N_CHIP, *p.shape[1:]), p.dtype) for p in parts],
                 [pltpu.SemaphoreType.DMA((nw, N_CHIP)), pltpu.SemaphoreType.DMA((nw, N_CHIP))], start, finish, sibling=True)


def _chip_comm(sums, piece=None, into=None):
    nw = len(sums)
    spans = [_piece_rows(s.shape[1], piece) for s in sums]

    def plan(ins, outs, sems):
        send_sems, recv_sems, local_sems = sems
        x, y, c, chips = _place()
        my_chip = 2 * x + y
        local = [pltpu.make_async_copy(ins[w].at[my_chip, pl.ds(*spans[w])], outs[w].at[my_chip, pl.ds(*spans[w])],
                                       local_sems.at[w]) for w in range(nw)]
        remote = [pltpu.make_async_remote_copy(
            src_ref=ins[w].at[2 * chip[0] + chip[1], pl.ds(*spans[w])], dst_ref=outs[w].at[my_chip, pl.ds(*spans[w])],
            send_sem=send_sems.at[w, j], recv_sem=recv_sems.at[w, j], device_id=(*chip, c), device_id_type=MESH)
            for w in range(nw) for j, chip in enumerate(chips)]
        return local, remote

    def start(ins, outs, sems):
        local, remote = plan(ins, outs, sems)
        for cp in local + remote:
            cp.start()

    def finish(ins, outs, sems):
        local, remote = plan(ins, outs, sems)
        for cp in remote + local:
            cp.wait()

    return _Comm(list(sums), [jax.ShapeDtypeStruct(s.shape, s.dtype) for s in sums],
                 [pltpu.SemaphoreType.DMA((nw, 3)), pltpu.SemaphoreType.DMA((nw, 3)), pltpu.SemaphoreType.DMA((nw,))],
                 start, finish, into=into, chips=True)


def _matmul(name, a, b, *, kind, m, n, k, a_spec, b_spec, tm, tn, tk, out_shape, out_specs, epilogue,
            extras=(), extra_specs=(), comm=None):
    dims = {"nn": (((1,), (0,)), ((), ())), "nt": (((1,), (1,)), ((), ())), "tn": (((0,), (0,)), ((), ()))}[kind]
    nk = k // tk
    n_extra = len(extras)
    n_out = len(out_shape)

    def body(a_ref, b_ref, *rest):
        extra_refs = rest[:n_extra]
        out_refs = rest[n_extra:n_extra + n_out]
        part = lax.dot_general(a_ref[...], b_ref[...], dims, preferred_element_type=F32)
        if nk == 1:
            epilogue(part, extra_refs, out_refs)
            return
        acc_ref = rest[n_extra + n_out]
        step = pl.program_id(2)

        @pl.when(step == 0)
        def _():
            acc_ref[...] = part

        @pl.when(jnp.logical_and(step > 0, step < nk - 1))
        def _():
            acc_ref[...] += part

        @pl.when(step == nk - 1)
        def _():
            epilogue(acc_ref[...] + part, extra_refs, out_refs)

    return _call(
        body, name=name, grid=(m // tm, n // tn, nk), in_specs=[a_spec, b_spec, *extra_specs], out_specs=out_specs,
        out_shape=out_shape, scratch_shapes=[pltpu.VMEM((tm, tn), F32)] if nk > 1 else [], args=(a, b, *extras), comm=comm)


def _mm_operands(m, n, tm, tn, out_dtypes, extras, rows, sums):
    tile = pl.BlockSpec((tm, tn), lambda i, j, kk: (i, j))
    row = pl.BlockSpec((1, tn), lambda i, j, kk: (0, j))
    one = pl.BlockSpec((1, 1), lambda i, j, kk: (0, 0))
    out_shape = [jax.ShapeDtypeStruct((m, n), dt) for dt in out_dtypes]
    out_shape += [jax.ShapeDtypeStruct((1, n if wide else 1), F32) for wide in sums]
    out_specs = [tile] * len(out_dtypes) + [row if wide else one for wide in sums]
    return (*extras, *rows), [tile] * len(extras) + [row] * len(rows), out_shape, out_specs


def _row_tile(m, k, tm_want):
    return _tile(m, tm_want or (1024 if k <= 1024 else 512))


def _mm_nn(name, a, b, n, epilogue, out_dtypes, *, extras=(), rows=(), sums=(), tm_want=None, comm=None):
    m, k = a.shape
    blocked = b.ndim == 3
    tm = _row_tile(m, k, tm_want)
    tn = b.shape[-1] if blocked else _tile(n, 1024)
    tk = k
    extras, extra_specs, out_shape, out_specs = _mm_operands(m, n, tm, tn, out_dtypes, extras, rows, sums)
    if blocked:
        b_spec = pl.BlockSpec((None, tk, tn), lambda i, j, kk: (j, kk, 0))
    else:
        b_spec = pl.BlockSpec((tk, tn), lambda i, j, kk: (kk, j))
    return _matmul(
        name, a, b, kind="nn", m=m, n=n, k=k, tm=tm, tn=tn, tk=tk,
        a_spec=pl.BlockSpec((tm, tk), lambda i, j, kk: (i, kk)), b_spec=b_spec, out_shape=out_shape, out_specs=out_specs,
        epilogue=epilogue, extras=extras, extra_specs=extra_specs, comm=comm)


def _mm_nt(name, a, b, n, epilogue, out_dtypes, *, extras=(), rows=(), sums=(), tm_want=None, comm=None):
    m, k = a.shape
    blocked = b.ndim == 3
    tm = _row_tile(m, k, tm_want)
    tn = _tile(n, 1024)
    tk = b.shape[-1] if blocked else k
    extras, extra_specs, out_shape, out_specs = _mm_operands(m, n, tm, tn, out_dtypes, extras, rows, sums)
    if blocked:
        b_spec = pl.BlockSpec((None, tn, tk), lambda i, j, kk: (kk, j, 0))
    else:
        b_spec = pl.BlockSpec((tn, tk), lambda i, j, kk: (j, kk))
    return _matmul(
        name, a, b, kind="nt", m=m, n=n, k=k, tm=tm, tn=tn, tk=tk,
        a_spec=pl.BlockSpec((tm, tk), lambda i, j, kk: (i, kk)), b_spec=b_spec, out_shape=out_shape, out_specs=out_specs,
        epilogue=epilogue, extras=extras, extra_specs=extra_specs, comm=comm)


def _mm_tn(name, a, b, *, col_blocks=0, comm=None):
    t, k = a.shape
    n = b.shape[1]
    tm = _row_tile(k, t, None)
    tn = n // col_blocks if col_blocks else _tile(n, 1024)
    tk = t
    if col_blocks:
        out_shape = [jax.ShapeDtypeStruct((col_blocks, k, tn), F32)]
        out_specs = [pl.BlockSpec((None, tm, tn), lambda i, j, kk: (j, i, 0))]
    else:
        out_shape = [jax.ShapeDtypeStruct((k, n), F32)]
        out_specs = [pl.BlockSpec((tm, tn), lambda i, j, kk: (i, j))]

    def epilogue(acc, extra_refs, out_refs):
        out_refs[0][...] = acc

    return _matmul(
        name, a, b, kind="tn", m=k, n=n, k=t, tm=tm, tn=tn, tk=tk,
        a_spec=pl.BlockSpec((tk, tm), lambda i, j, kk: (kk, i)), b_spec=pl.BlockSpec((tk, tn), lambda i, j, kk: (kk, j)),
        out_shape=out_shape, out_specs=out_specs, epilogue=epilogue, comm=comm)[0]


def _ep_store(acc, extra_refs, out_refs):
    out_refs[0][...] = acc.astype(out_refs[0].dtype)


def _ep_bias(acc, extra_refs, out_refs):
    out_refs[0][...] = acc + extra_refs[0][...]


def _rms(x):
    r = lax.rsqrt(jnp.mean(x * x, axis=-1, keepdims=True) + RMS_EPS)
    return r, x * r


def _rms_grad(dn, r, xr, g):
    dy = dn * g
    return r * (dy - xr * jnp.mean(dy * xr, axis=-1, keepdims=True))


def _ep_residual_norm(acc, extra_refs, out_refs):
    h = extra_refs[0][...] + acc
    out_refs[0][...] = h
    _, xr = _rms(h)
    out_refs[1][...] = (xr * extra_refs[1][...]).astype(BF16)


def _ep_final_loss(acc, extra_refs, out_refs):
    @pl.when(pl.program_id(0) == 0)
    def _():
        out_refs[2][...] = jnp.zeros_like(out_refs[2])
        out_refs[3][...] = jnp.zeros_like(out_refs[3])

    h = extra_refs[0][...] + acc
    g = extra_refs[2][...]
    r, xr = _rms(h)
    err = xr * g - extra_refs[1][...]
    out_refs[3][...] += 0.5 * jnp.sum(jnp.mean(err * err, axis=-1, keepdims=True), axis=0, keepdims=True)
    dout = err * (1.0 / h.shape[-1])
    out_refs[2][...] += _colsum(dout * xr)
    out = _rms_grad(dout, r, xr, g)
    out_refs[0][...] = out
    out_refs[1][...] = out.astype(BF16)


def _ep_relu_sq(acc, extra_refs, out_refs):
    r = jnp.maximum(acc, 0.0)
    out_refs[0][...] = (r * r).astype(BF16)


def _ep_relu_sq_grad(acc, extra_refs, out_refs):
    out_refs[0][...] = (acc * (2.0 * jnp.sqrt(extra_refs[0][...].astype(F32)))).astype(BF16)


def _rms_fwd(name, h, g, comm=None):
    t, d = h.shape
    tt = _tile(t, 512)

    def body(h_ref, g_ref, n_ref):
        x = h_ref[...]
        r = lax.rsqrt(jnp.mean(x * x, axis=-1, keepdims=True) + RMS_EPS)
        n_ref[...] = (x * r * g_ref[...]).astype(BF16)

    return _call(
        body, name=name, grid=(t // tt,),
        in_specs=[pl.BlockSpec((tt, d), lambda i: (i, 0)), pl.BlockSpec((1, d), lambda i: (0, 0))],
        out_specs=[pl.BlockSpec((tt, d), lambda i: (i, 0))], out_shape=[jax.ShapeDtypeStruct((t, d), BF16)],
        args=(h, g), parallel=True, comm=comm)[0]


def _rms_bwd(name, dn, h, g, grad_in, comm=None):
    t, d = h.shape
    tt = _tile(t, 512)

    def body(dn_ref, h_ref, g_ref, gin_ref, gout_ref, gout16_ref, dg_ref):
        @pl.when(pl.program_id(0) == 0)
        def _():
            dg_ref[...] = jnp.zeros_like(dg_ref)

        dnv = dn_ref[...]
        r, xr = _rms(h_ref[...])
        dg_ref[...] += _colsum(dnv * xr)
        out = gin_ref[...] + _rms_grad(dnv, r, xr, g_ref[...])
        gout_ref[...] = out
        gout16_ref[...] = out.astype(BF16)

    row = pl.BlockSpec((tt, d), lambda i: (i, 0))
    vec = pl.BlockSpec((1, d), lambda i: (0, 0))
    return _call(
        body, name=name, grid=(t // tt,), in_specs=[row, row, vec, row], out_specs=[row, row, vec],
        out_shape=[jax.ShapeDtypeStruct((t, d), F32), jax.ShapeDtypeStruct((t, d), BF16), jax.ShapeDtypeStruct((1, d), F32)],
        args=(dn, h, g, grad_in), comm=comm)


def _mixer_windows(z_ref, zh_ref, first, a1_s, cb_s, tt):
    sig = _sigmoid(z_ref[:, A_DIM:2 * A_DIM])
    a1_s[A_HALO:A_HALO + tt, :] = z_ref[:, 0:A_DIM] * sig
    a1_h = zh_ref[:, 0:A_DIM] * _sigmoid(zh_ref[:, A_DIM:2 * A_DIM])
    a1_s[0:A_HALO, :] = jnp.where(first, 0.0, a1_h)
    cb_s[B_HALO:B_HALO + tt, :] = z_ref[:, 3 * A_DIM:4 * A_DIM] * z_ref[:, 4 * A_DIM:5 * A_DIM]
    cb_h = zh_ref[A_HALO - B_HALO:A_HALO, 3 * A_DIM:4 * A_DIM] * zh_ref[A_HALO - B_HALO:A_HALO, 4 * A_DIM:5 * A_DIM]
    cb_s[0:B_HALO, :] = jnp.where(first, 0.0, cb_h)
    return sig


def _causal_conv(win_s, w_ref, taps, halo, tt):
    base = halo - (taps - 1)
    acc = w_ref[0:1, :] * win_s[pl.ds(base, tt), :]
    for k in range(1, taps):
        acc = acc + w_ref[k:k + 1, :] * win_s[pl.ds(base + k, tt), :]
    return acc


SUBLANES = 8
LANE_BLOCK = 128
ROW_BLOCK = 128
SHIFT_ROWS = A_HALO - SUBLANES


def _shifted_copies(win_s, sh_s, tt):
    for b in range(1, SUBLANES):
        sh_s[b - 1] = win_s[pl.ds(b, tt + SHIFT_ROWS), :]


def _window_rows(win_s, sh_s, offset, rows, cols):
    b = offset % SUBLANES
    if b == 0:
        return win_s[pl.ds(offset, rows), cols]
    return sh_s[b - 1, pl.ds(offset - b, rows), cols]


def _blocks(tt):
    rb = min(tt, ROW_BLOCK)
    return rb, [(r, slice(lb * LANE_BLOCK, (lb + 1) * LANE_BLOCK))
                for lb in range(A_DIM // LANE_BLOCK) for r in range(0, tt, rb)]


def _conv_taps(win_s, sh_s, w_ref, offsets, out_s, tt, bias_ref=None):
    rb, blocks = _blocks(tt)
    for r, cols in blocks:
        acc = w_ref[0:1, cols] * _window_rows(win_s, sh_s, r + offsets[0], rb, cols)
        for k in range(1, len(offsets)):
            acc = acc + w_ref[k:k + 1, cols] * _window_rows(win_s, sh_s, r + offsets[k], rb, cols)
        out_s[r:r + rb, cols] = acc if bias_ref is None else acc + bias_ref[:, cols]


A_CAUSAL = [A_HALO - (A_TAPS - 1) + k for k in range(A_TAPS)]
A_ANTICAUSAL = [A_TAPS - 1 - k for k in range(A_TAPS)]


def _layer_norm_stats(x):
    mu = jnp.mean(x, axis=-1, keepdims=True)
    xc = x - mu
    rstd = lax.rsqrt(jnp.mean(xc * xc, axis=-1, keepdims=True) + LN_EPS)
    return xc * rstd, rstd


def _mixer_specs(seq, tt):
    tiles_per_seq = seq // tt
    halo_blocks = tt // A_HALO
    z_spec = pl.BlockSpec((tt, IN_EVEN), lambda i: (i, 0))
    zh_spec = pl.BlockSpec((A_HALO, IN_EVEN), lambda i: (jnp.maximum(i * halo_blocks - 1, 0), 0))
    return tiles_per_seq, z_spec, zh_spec


def _vec_spec(rows, cols):
    return pl.BlockSpec((rows, cols), lambda i: (0, 0))


def _mixer_fwd(z, seq, caw, cab, lag, lab, cbw, comm=None):
    t = z.shape[0]
    tt = _tile(seq, 256)
    tiles_per_seq, z_spec, zh_spec = _mixer_specs(seq, tt)

    def body(z_ref, zh_ref, caw_ref, cab_ref, lag_ref, lab_ref, cbw_ref, mix_ref, a1_s, cb_s, sh_s, a2_s):
        first = (pl.program_id(0) % tiles_per_seq) == 0
        _mixer_windows(z_ref, zh_ref, first, a1_s, cb_s, tt)
        _shifted_copies(a1_s, sh_s, tt)
        _conv_taps(a1_s, sh_s, caw_ref, A_CAUSAL, a2_s, tt, bias_ref=cab_ref)
        xhat, _ = _layer_norm_stats(a2_s[...])
        a3 = xhat * lag_ref[...] + lab_ref[...]
        mix_ref[:, 0:A_DIM] = (a3 * _sigmoid(a3)).astype(BF16)
        cv = _causal_conv(cb_s, cbw_ref, B_TAPS, B_HALO, tt)
        mix_ref[:, A_DIM:A_DIM + B_DIM] = (z_ref[:, 2 * A_DIM:3 * A_DIM] * cv).astype(BF16)

    return _call(
        body, name="mixer_fwd", grid=(t // tt,),
        in_specs=[z_spec, zh_spec, _vec_spec(A_TAPS, A_DIM), _vec_spec(1, A_DIM), _vec_spec(1, A_DIM), _vec_spec(1, A_DIM),
                  _vec_spec(B_TAPS, B_DIM)],
        out_specs=[pl.BlockSpec((tt, A_DIM + B_DIM), lambda i: (i, 0))],
        out_shape=[jax.ShapeDtypeStruct((t, A_DIM + B_DIM), BF16)],
        scratch_shapes=[pltpu.VMEM((A_HALO + tt, A_DIM), F32), pltpu.VMEM((B_HALO + tt, B_DIM), F32),
                        pltpu.VMEM((SUBLANES - 1, tt + SHIFT_ROWS, A_DIM), F32), pltpu.VMEM((tt, A_DIM), F32)],
        args=(z, z, caw, cab, lag, lab, cbw), parallel=True, comm=comm)[0]


def _mixer_bwd_local(z, dmix, seq, caw, cab, lag, lab, cbw, comm=None):
    t = z.shape[0]
    tt = _tile(seq, 256)
    tiles_per_seq, z_spec, zh_spec = _mixer_specs(seq, tt)

    def body(z_ref, zh_ref, dmix_ref, caw_ref, cab_ref, lag_ref, lab_ref, cbw_ref,
             da2_ref, dcv_ref, dcaw_ref, dcab_ref, dlag_ref, dlab_ref, dcbw_ref, a1_s, cb_s, sh_s, a2_s):
        @pl.when(pl.program_id(0) == 0)
        def _():
            for ref in (dcaw_ref, dcab_ref, dlag_ref, dlab_ref, dcbw_ref):
                ref[...] = jnp.zeros_like(ref)

        first = (pl.program_id(0) % tiles_per_seq) == 0
        _mixer_windows(z_ref, zh_ref, first, a1_s, cb_s, tt)
        _shifted_copies(a1_s, sh_s, tt)
        _conv_taps(a1_s, sh_s, caw_ref, A_CAUSAL, a2_s, tt, bias_ref=cab_ref)
        xhat, rstd = _layer_norm_stats(a2_s[...])
        a3 = xhat * lag_ref[...] + lab_ref[...]
        s3 = _sigmoid(a3)
        da3 = dmix_ref[:, 0:A_DIM] * (s3 * (1.0 + a3 * (1.0 - s3)))
        dlag_ref[...] += _colsum(da3 * xhat)
        dlab_ref[...] += _colsum(da3)
        dxh = da3 * lag_ref[...]
        da2 = rstd * (dxh - jnp.mean(dxh, axis=-1, keepdims=True) - xhat * jnp.mean(dxh * xhat, axis=-1, keepdims=True))
        da2_ref[...] = da2
        dcab_ref[...] += _colsum(da2)
        rb, blocks = _blocks(tt)
        for r, cols in blocks:
            da2_b = da2_ref[r:r + rb, cols]
            for k in range(A_TAPS):
                dcaw_ref[k:k + 1, cols] += _colsum(da2_b * _window_rows(a1_s, sh_s, r + A_CAUSAL[k], rb, cols))
        dcv =dmix_ref[:, A_DIM:A_DIM + B_DIM] * z_ref[:, 2 * A_DIM:3 * A_DIM]
        dcv_ref[...] = dcv
        for k in range(B_TAPS):
            dcbw_ref[k:k + 1, :] += _colsum(dcv * cb_s[pl.ds(B_HALO - (B_TAPS - 1) + k, tt), :])

    half = pl.BlockSpec((tt, A_DIM), lambda i: (i, 0))
    return _call(
        body, name="mixer_bwd_local", grid=(t // tt,),
        in_specs=[z_spec, zh_spec, pl.BlockSpec((tt, A_DIM + B_DIM), lambda i: (i, 0)),
                  _vec_spec(A_TAPS, A_DIM), _vec_spec(1, A_DIM), _vec_spec(1, A_DIM), _vec_spec(1, A_DIM), _vec_spec(B_TAPS, B_DIM)],
        out_specs=[half, half, _vec_spec(A_TAPS, A_DIM), _vec_spec(1, A_DIM), _vec_spec(1, A_DIM), _vec_spec(1, A_DIM),
                   _vec_spec(B_TAPS, B_DIM)],
        out_shape=[jax.ShapeDtypeStruct((t, A_DIM), F32), jax.ShapeDtypeStruct((t, B_DIM), F32),
                   jax.ShapeDtypeStruct((A_TAPS, A_DIM), F32), jax.ShapeDtypeStruct((1, A_DIM), F32),
                   jax.ShapeDtypeStruct((1, A_DIM), F32), jax.ShapeDtypeStruct((1, A_DIM), F32),
                   jax.ShapeDtypeStruct((B_TAPS, B_DIM), F32)],
        scratch_shapes=[pltpu.VMEM((A_HALO + tt, A_DIM), F32), pltpu.VMEM((B_HALO + tt, B_DIM), F32),
                        pltpu.VMEM((SUBLANES - 1, tt + SHIFT_ROWS, A_DIM), F32), pltpu.VMEM((tt, A_DIM), F32)],
        args=(z, z, dmix, caw, cab, lag, lab, cbw), comm=comm)


def _mixer_bwd_input(z, dmix, da2, dcv, seq, caw, cbw, comm=None):
    t = z.shape[0]
    tt = _tile(seq, 256)
    tiles_per_seq, z_spec, zh_spec = _mixer_specs(seq, tt)
    a_blocks = tt // A_HALO
    b_blocks = tt // B_HALO
    last_a = t // A_HALO - 1
    last_b = t // B_HALO - 1

    def body(z_ref, zh_ref, dmix_ref, da2_ref, da2n_ref, dcv_ref, dcvn_ref, caw_ref, cbw_ref, dz_ref, a1_s, cb_s, da2_s, dcv_s,
             sh_s, da1_s):
        pos = pl.program_id(0) % tiles_per_seq
        first = pos == 0
        last = pos == tiles_per_seq - 1
        sig = _mixer_windows(z_ref, zh_ref, first, a1_s, cb_s, tt)
        da2_s[0:tt, :] = da2_ref[...]
        da2_s[tt:tt + A_HALO, :] = jnp.where(last, 0.0, da2n_ref[...])
        dcv_s[0:tt, :] = dcv_ref[...]
        dcv_s[tt:tt + B_HALO, :] = jnp.where(last, 0.0, dcvn_ref[...])
        _shifted_copies(da2_s, sh_s, tt)
        _conv_taps(da2_s, sh_s, caw_ref, A_ANTICAUSAL, da1_s, tt)
        da1 = da1_s[...]
        dz_ref[:, 0:A_DIM] = (da1 * sig).astype(BF16)
        dz_ref[:, A_DIM:2 * A_DIM] = (da1 * z_ref[:, 0:A_DIM] * sig * (1.0 - sig)).astype(BF16)
        cv = _causal_conv(cb_s, cbw_ref, B_TAPS, B_HALO, tt)
        dz_ref[:, 2 * A_DIM:3 * A_DIM] = (dmix_ref[:, A_DIM:A_DIM + B_DIM] * cv).astype(BF16)
        dcb = cbw_ref[0:1, :] * dcv_s[pl.ds(B_TAPS - 1, tt), :]
        for k in range(1, B_TAPS):
            dcb = dcb + cbw_ref[k:k + 1, :] * dcv_s[pl.ds(B_TAPS - 1 - k, tt), :]
        dz_ref[:, 3 * A_DIM:4 * A_DIM] = (dcb * z_ref[:, 4 * A_DIM:5 * A_DIM]).astype(BF16)
        dz_ref[:, 4 * A_DIM:5 * A_DIM] = (dcb * z_ref[:, 3 * A_DIM:4 * A_DIM]).astype(BF16)

    half = pl.BlockSpec((tt, A_DIM), lambda i: (i, 0))
    return _call(
        body, name="mixer_bwd_input", grid=(t // tt,),
        in_specs=[z_spec, zh_spec, pl.BlockSpec((tt, A_DIM + B_DIM), lambda i: (i, 0)),
                  half, pl.BlockSpec((A_HALO, A_DIM), lambda i: (jnp.minimum((i + 1) * a_blocks, last_a), 0)),
                  half, pl.BlockSpec((B_HALO, B_DIM), lambda i: (jnp.minimum((i + 1) * b_blocks, last_b), 0)),
                  _vec_spec(A_TAPS, A_DIM), _vec_spec(B_TAPS, B_DIM)],
        out_specs=[pl.BlockSpec((tt, IN_EVEN), lambda i: (i, 0))],
        out_shape=[jax.ShapeDtypeStruct((t, IN_EVEN), BF16)],
        scratch_shapes=[pltpu.VMEM((A_HALO + tt, A_DIM), F32), pltpu.VMEM((B_HALO + tt, B_DIM), F32),
                        pltpu.VMEM((tt + A_HALO, A_DIM), F32), pltpu.VMEM((tt + B_HALO, B_DIM), F32),
                        pltpu.VMEM((SUBLANES - 1, tt + SHIFT_ROWS, A_DIM), F32), pltpu.VMEM((tt, A_DIM), F32)],
        args=(z, z, dmix, da2, da2, dcv, dcv, caw, cbw), parallel=True, comm=comm)[0]


def _tril_ws(ws_ref, g):
    rows = lax.broadcasted_iota(jnp.int32, (CHUNK, CHUNK), 0)
    cols = lax.broadcasted_iota(jnp.int32, (CHUNK, CHUNK), 1)
    return jnp.where(rows >= cols, ws_ref[g], 0.0).astype(BF16), rows >= cols


def _sgu_fwd(pre, lvg, lvb, ws, bs_b, comm=None):
    t = pre.shape[0]
    tt = _tile(t, 256)

    def body(pre_ref, lvg_ref, lvb_ref, ws_ref, bsb_ref, y_ref):
        vhat, _ = _layer_norm_stats(_gelu(pre_ref[:, C_DIM:2 * C_DIM]))
        vl = (vhat * lvg_ref[...] + lvb_ref[...]).astype(BF16)
        for g in range(C_GROUPS):
            w, _ = _tril_ws(ws_ref, g)
            cols = slice(g * CHUNK, (g + 1) * CHUNK)
            for ci in range(tt // CHUNK):
                rows = slice(ci * CHUNK, (ci + 1) * CHUNK)
                sv = jnp.dot(w, vl[rows, cols], preferred_element_type=F32) + bsb_ref[g]
                y_ref[rows, cols] = (_gelu(pre_ref[rows, cols]) * sv).astype(BF16)

    group = pl.BlockSpec((C_GROUPS, CHUNK, CHUNK), lambda i: (0, 0, 0))
    return _call(
        body, name="sgu_fwd", grid=(t // tt,),
        in_specs=[pl.BlockSpec((tt, 2 * C_DIM), lambda i: (i, 0)), _vec_spec(1, C_DIM), _vec_spec(1, C_DIM), group, group],
        out_specs=[pl.BlockSpec((tt, C_DIM), lambda i: (i, 0))], out_shape=[jax.ShapeDtypeStruct((t, C_DIM), BF16)],
        args=(pre, lvg, lvb, ws, bs_b), parallel=True, comm=comm)[0]


def _sgu_bwd(pre, dy, lvg, lvb, ws, bs_b, comm=None):
    t = pre.shape[0]
    tt = _tile(t, 256)

    def body(pre_ref, dy_ref, lvg_ref, lvb_ref, ws_ref, bsb_ref, dpre_ref, dws_ref, dbsb_ref, dlvg_ref, dlvb_ref, dbin_ref,
             dvl_s):
        @pl.when(pl.program_id(0) == 0)
        def _():
            for ref in (dws_ref, dbsb_ref, dlvg_ref, dlvb_ref, dbin_ref):
                ref[...] = jnp.zeros_like(ref)

        pre_v = pre_ref[:, C_DIM:2 * C_DIM]
        vhat, rstd = _layer_norm_stats(_gelu(pre_v))
        vl = (vhat * lvg_ref[...] + lvb_ref[...]).astype(BF16)
        for g in range(C_GROUPS):
            w, keep = _tril_ws(ws_ref, g)
            cols = slice(g * CHUNK, (g + 1) * CHUNK)
            dws = jnp.zeros((CHUNK, CHUNK), F32)
            dbs = jnp.zeros((CHUNK, 1), F32)
            for ci in range(tt // CHUNK):
                rows = slice(ci * CHUNK, (ci + 1) * CHUNK)
                vl_g = vl[rows, cols]
                sv = jnp.dot(w, vl_g, preferred_element_type=F32) + bsb_ref[g]
                pre_u = pre_ref[rows, cols]
                dyv = dy_ref[rows, cols]
                du = dyv * sv * _gelu_grad(pre_u)
                dpre_ref[rows, cols] = du.astype(BF16)
                dbin_ref[:, cols] += _colsum(du)
                dsv = dyv * _gelu(pre_u)
                dbs = dbs + jnp.sum(dsv, axis=1, keepdims=True)
                dsv16 = dsv.astype(BF16)
                dws = dws + lax.dot_general(dsv16, vl_g, (((1,), (1,)), ((), ())), preferred_element_type=F32)
                dvl_s[rows, cols] = lax.dot_general(w, dsv16, (((0,), (0,)), ((), ())), preferred_element_type=F32)
            dws_ref[g] += jnp.where(keep, dws, 0.0)
            dbsb_ref[g] += dbs
        dvl = dvl_s[...]
        dlvg_ref[...] += _colsum(dvl * vhat)
        dlvb_ref[...] += _colsum(dvl)
        dxh = dvl * lvg_ref[...]
        dv = rstd * (dxh - jnp.mean(dxh, axis=-1, keepdims=True) - vhat * jnp.mean(dxh * vhat, axis=-1, keepdims=True))
        dpv = dv * _gelu_grad(pre_v)
        dpre_ref[:, C_DIM:2 * C_DIM] = dpv.astype(BF16)
        dbin_ref[:, C_DIM:2 * C_DIM] += _colsum(dpv)

    group = pl.BlockSpec((C_GROUPS, CHUNK, CHUNK), lambda i: (0, 0, 0))
    return _call(
        body, name="sgu_bwd", grid=(t // tt,),
        in_specs=[pl.BlockSpec((tt, 2 * C_DIM), lambda i: (i, 0)), pl.BlockSpec((tt, C_DIM), lambda i: (i, 0)),
                  _vec_spec(1, C_DIM), _vec_spec(1, C_DIM), group, group],
        out_specs=[pl.BlockSpec((tt, 2 * C_DIM), lambda i: (i, 0)), group, group,
                   _vec_spec(1, C_DIM), _vec_spec(1, C_DIM), _vec_spec(1, 2 * C_DIM)],
        out_shape=[jax.ShapeDtypeStruct((t, 2 * C_DIM), BF16), jax.ShapeDtypeStruct((C_GROUPS, CHUNK, CHUNK), F32),
                   jax.ShapeDtypeStruct((C_GROUPS, CHUNK, CHUNK), F32), jax.ShapeDtypeStruct((1, C_DIM), F32),
                   jax.ShapeDtypeStruct((1, C_DIM), F32), jax.ShapeDtypeStruct((1, 2 * C_DIM), F32)],
        scratch_shapes=[pltpu.VMEM((tt, C_DIM), F32)],
        args=(pre, dy, lvg, lvb, ws, bs_b), comm=comm)


def _pair_sum(name, part, got, core):
    _, k, n = part.shape
    tk = _tile(k, 256)

    def body(core_ref, p_ref, s_ref, o_ref):
        o_ref[...] = (p_ref[...] + s_ref[...]).astype(BF16)

    return pl.pallas_call(
        body, name=name,
        grid_spec=pltpu.PrefetchScalarGridSpec(
            num_scalar_prefetch=1, grid=(N_CHIP, k // tk),
            in_specs=[pl.BlockSpec((None, tk, n), lambda q, i, core_ref: (2 * q + core_ref[0], i, 0)),
                      pl.BlockSpec((None, tk, n), lambda q, i, core_ref: (q, i, 0))],
            out_specs=pl.BlockSpec((None, tk, n), lambda q, i, core_ref: (q, i, 0))),
        out_shape=jax.ShapeDtypeStruct((N_CHIP, k, n), BF16),
        compiler_params=pltpu.CompilerParams(dimension_semantics=("parallel", "parallel"), vmem_limit_bytes=VMEM_LIMIT_BYTES),
    )(core, part, got)


def _adamw_math(w, g, m, v):
    m = ADAM_B1 * m + (1.0 - ADAM_B1) * g
    v = ADAM_B2 * v + (1.0 - ADAM_B2) * (g * g)
    m_hat = m / (1.0 - ADAM_B1 ** ADAM_STEP)
    v_hat = v / (1.0 - ADAM_B2 ** ADAM_STEP)
    delta = -ADAM_LR * (m_hat / (jnp.sqrt(v_hat) + ADAM_EPS) + ADAM_WD * w)
    return delta, m, v


def _sum_adamw(name, parts, w, m, v, comm=None):
    layers = len(parts)
    n_parts, k, n = parts[0].shape
    tk = _tile(k, 256)

    def body(*refs):
        p_refs = refs[:layers]
        w_ref, m_ref, v_ref, g_ref, d_ref, nm_ref, nv_ref = refs[layers:]

        def total(p_ref):
            g = p_ref[0].astype(F32)
            for q in range(1, n_parts):
                g = g + p_ref[q].astype(F32)
            return g

        g = total(p_refs[0])
        for l in range(1, layers):
            g = jnp.where(pl.program_id(0) == l, total(p_refs[l]), g)
        g_ref[...] = g
        d_ref[...], nm_ref[...], nv_ref[...] = _adamw_math(w_ref[...], g, m_ref[...], v_ref[...])

    blk = pl.BlockSpec((None, tk, n), lambda l, i: (l, i, 0))
    return _call(
        body, name=name, grid=(layers, k // tk),
        in_specs=[pl.BlockSpec((n_parts, tk, n), lambda l, i: (0, i, 0))] * layers + [blk, blk, blk], out_specs=[blk] * 4,
        out_shape=[jax.ShapeDtypeStruct((layers, k, n), F32)] * 4, args=(*parts, w, m, v), parallel=True, comm=comm)


def _small_adamw(name, parts, w, m, v, losses):
    count = len(parts)

    def in_order(ref):
        total = ref[0]
        for dev in range(1, N_DEV):
            total = total + ref[dev]
        return total

    def body(*refs):
        p_refs, w_refs, m_refs, v_refs = (refs[j * count:(j + 1) * count] for j in range(4))
        losses_ref = refs[4 * count]
        g_refs, d_refs, nm_refs, nv_refs = (refs[4 * count + 1 + j * count:4 * count + 1 + (j + 1) * count] for j in range(4))
        loss_ref = refs[8 * count + 1]
        for i in range(count):
            g = in_order(p_refs[i])
            g_refs[i][...] = g
            d_refs[i][...], nm_refs[i][...], nv_refs[i][...] = _adamw_math(w_refs[i][...], g, m_refs[i][...], v_refs[i][...])
        loss_ref[...] = in_order(losses_ref)

    res = pl.pallas_call(
        body, name=name,
        out_shape=[jax.ShapeDtypeStruct(a.shape, F32) for a in w] * 4 + [jax.ShapeDtypeStruct(losses.shape[1:], F32)],
        compiler_params=pltpu.CompilerParams(vmem_limit_bytes=VMEM_LIMIT_BYTES))(*parts, *w, *m, *v, losses)
    return [res[j * count:(j + 1) * count] for j in range(4)], res[4 * count]


def _rows(a):
    return a.reshape(-1, a.shape[-1])


def _whole(gathered):
    return jnp.transpose(gathered, (1, 0, 2)).reshape(gathered.shape[1], -1)


SMALL =("ev_norm_g", "ev_conv_a_w", "ev_conv_a_b", "ev_ln_a_g", "ev_ln_a_b", "ev_conv_b_w", "od_norm_g", "od_b_in",
         "od_ln_v_g", "od_ln_v_b", "od_w_s", "od_b_s", "mlp_norm_g", "final_norm_g")
SMALL_SHARDED = ("ev_conv_a_w", "ev_conv_b_w", "od_norm_g", "od_b_in", "od_ln_v_g", "od_ln_v_b")
ORDER = ("ev_norm_g", "ev_w_in", "ev_conv_a_w", "ev_conv_a_b", "ev_ln_a_g", "ev_ln_a_b", "ev_conv_b_w", "ev_w_out",
         "od_norm_g", "od_w_in", "od_b_in", "od_ln_v_g", "od_ln_v_b", "od_w_s", "od_b_s", "od_w_out", "mlp_norm_g",
         "mlp_w1", "mlp_w2", "final_norm_g")


def kernel(x, ev_norm_g, ev_w_in, ev_conv_a_w, ev_conv_a_b, ev_ln_a_g, ev_ln_a_b, ev_conv_b_w, ev_w_out, od_norm_g, od_w_in, od_b_in, od_ln_v_g, od_ln_v_b, od_w_s, od_b_s, od_w_out, mlp_norm_g, mlp_w1, mlp_w2, final_norm_g, loss_target, m_ev_norm_g, m_ev_w_in, m_ev_conv_a_w, m_ev_conv_a_b, m_ev_ln_a_g, m_ev_ln_a_b, m_ev_conv_b_w, m_ev_w_out, m_od_norm_g, m_od_w_in, m_od_b_in, m_od_ln_v_g, m_od_ln_v_b, m_od_w_s, m_od_b_s, m_od_w_out, m_mlp_norm_g, m_mlp_w1, m_mlp_w2, m_final_norm_g, v_ev_norm_g, v_ev_w_in, v_ev_conv_a_w, v_ev_conv_a_b, v_ev_ln_a_g, v_ev_ln_a_b, v_ev_conv_b_w, v_ev_w_out, v_od_norm_g, v_od_w_in, v_od_b_in, v_od_ln_v_g, v_od_ln_v_b, v_od_w_s, v_od_b_s, v_od_w_out, v_mlp_norm_g, v_mlp_w1, v_mlp_w2, v_final_norm_g):
    W = dict(ev_norm_g=ev_norm_g, ev_w_in=ev_w_in, ev_conv_a_w=ev_conv_a_w, ev_conv_a_b=ev_conv_a_b, ev_ln_a_g=ev_ln_a_g,
             ev_ln_a_b=ev_ln_a_b, ev_conv_b_w=ev_conv_b_w, ev_w_out=ev_w_out, od_norm_g=od_norm_g, od_w_in=od_w_in,
             od_b_in=od_b_in, od_ln_v_g=od_ln_v_g, od_ln_v_b=od_ln_v_b, od_w_s=od_w_s, od_b_s=od_b_s, od_w_out=od_w_out,
             mlp_norm_g=mlp_norm_g, mlp_w1=mlp_w1, mlp_w2=mlp_w2, final_norm_g=final_norm_g)
    M = dict(ev_norm_g=m_ev_norm_g, ev_w_in=m_ev_w_in, ev_conv_a_w=m_ev_conv_a_w, ev_conv_a_b=m_ev_conv_a_b,
             ev_ln_a_g=m_ev_ln_a_g, ev_ln_a_b=m_ev_ln_a_b, ev_conv_b_w=m_ev_conv_b_w, ev_w_out=m_ev_w_out,
             od_norm_g=m_od_norm_g, od_w_in=m_od_w_in, od_b_in=m_od_b_in, od_ln_v_g=m_od_ln_v_g, od_ln_v_b=m_od_ln_v_b,
             od_w_s=m_od_w_s, od_b_s=m_od_b_s, od_w_out=m_od_w_out, mlp_norm_g=m_mlp_norm_g, mlp_w1=m_mlp_w1,
             mlp_w2=m_mlp_w2, final_norm_g=m_final_norm_g)
    V = dict(ev_norm_g=v_ev_norm_g, ev_w_in=v_ev_w_in, ev_conv_a_w=v_ev_conv_a_w, ev_conv_a_b=v_ev_conv_a_b,
             ev_ln_a_g=v_ev_ln_a_g, ev_ln_a_b=v_ev_ln_a_b, ev_conv_b_w=v_ev_conv_b_w, ev_w_out=v_ev_w_out,
             od_norm_g=v_od_norm_g, od_w_in=v_od_w_in, od_b_in=v_od_b_in, od_ln_v_g=v_od_ln_v_g, od_ln_v_b=v_od_ln_v_b,
             od_w_s=v_od_w_s, od_b_s=v_od_b_s, od_w_out=v_od_w_out, mlp_norm_g=v_mlp_norm_g, mlp_w1=v_mlp_w1,
             mlp_w2=v_mlp_w2, final_norm_g=v_final_norm_g)

    n_seq, seq, d = x.shape
    t = n_seq * seq
    dev = 4 * lax.axis_index("x") + 2 * lax.axis_index("y") + lax.axis_index("c")
    core = lax.axis_index("c").astype(jnp.int32).reshape(1)

    ev_g, cab, lag, lab = W["ev_norm_g"], W["ev_conv_a_b"], W["ev_ln_a_g"], W["ev_ln_a_b"]
    ws = W["od_w_s"][0]
    bs_b = jnp.broadcast_to(W["od_b_s"][0][:, :, None], (C_GROUPS, CHUNK, CHUNK))
    mlp_g = [W["mlp_norm_g"][l:l + 1] for l in range(2)]
    fin_g = W["final_norm_g"].reshape(1, d)

    def w16(name, l=0):
        return W[name][l].astype(BF16)

    h0 = x.reshape(t, d)
    gather = _gather_comm([w16("ev_w_in")] + [_rows(W[n]) for n in SMALL_SHARDED])
    n0 = _rms_fwd("ev_norm", h0, ev_g, comm=gather)
    w_ev_in, caw, cbw, od_g, od_bin, lvg, lvb = [_whole(g) for g in gather.out]

    first, second = (0, 2), (1, 2)
    w1_0, w2_0, w1_1, w2_1 = w16("mlp_w1", 0), w16("mlp_w2", 0), w16("mlp_w1", 1), w16("mlp_w2", 1)

    g_a, g_b = _gather_comm([w16("ev_w_out")]), _gather_comm([w1_0], first)
    z = _mm_nn("ev_in", n0, w_ev_in, IN_EVEN, _ep_store, [F32], comm=[g_a, g_b])[0]
    w_ev_out = g_a.out[0].reshape(D_MODEL, D_MODEL)

    g_c, g_d = _gather_comm([w1_0], second, into=g_b.out), _gather_comm([w2_0], first)
    mix = _mixer_fwd(z, seq, caw, cab, lag, lab, cbw, comm=[g_c, g_d])
    w1 = [_whole(g_c.out[0]), None]

    g_e = _gather_comm([w2_0], second, into=g_d.out)
    h1, n1 = _mm_nn("ev_out", mix, w_ev_out, d, _ep_residual_norm, [F32, BF16], extras=(h0,), rows=(mlp_g[0],), comm=g_e)
    w2 = [g_e.out[0].reshape(D_FF, D_MODEL), None]

    g_f = _gather_comm([w16("od_w_in"), w16("od_w_out")])
    q0 = _mm_nn("mlp0_up", n1, w1[0], D_FF, _ep_relu_sq, [BF16], comm=g_f)[0]
    w_od_in = _whole(g_f.out[0])
    w_od_out = g_f.out[1].reshape(D_MODEL, D_MODEL)

    g_g = _gather_comm([w1_1], first)
    h2, n2 = _mm_nn("mlp0_down", q0, w2[0], d, _ep_residual_norm, [F32, BF16], extras=(h1,), rows=(od_g,), comm=g_g)
    g_h = _gather_comm([w1_1], second, into=g_g.out)
    pre = _mm_nn("od_in", n2, w_od_in, 2 * C_DIM, _ep_bias, [F32], rows=(od_bin,), comm=g_h)[0]
    w1[1] = _whole(g_h.out[0])
    g_i = _gather_comm([w2_1], first)
    y = _sgu_fwd(pre, lvg, lvb, ws, bs_b, comm=g_i)
    h3, n3 = _mm_nn("od_out", y, w_od_out, d, _ep_residual_norm, [F32, BF16], extras=(h2,), rows=(mlp_g[1],))
    g_j = _gather_comm([w2_1], second, into=g_i.out)
    q1 = _mm_nn("mlp1_up", n3, w1[1], D_FF, _ep_relu_sq, [BF16], comm=g_j)[0]
    w2[1] = g_j.out[0].reshape(D_FF, D_MODEL)
    grad, grad16, d_fin_g, loss_part = _mm_nn(
        "mlp1_down", q1, w2[1], d, _ep_final_loss, [F32, BF16], extras=(h3, loss_target.reshape(t, d)), rows=(fin_g,),
        sums=(True, False), tm_want=FUSED_ROWS)

    by_chip = {}

    def swap(name, parts):
        comm = _pair_comm([parts])
        comm.parts, comm.weight = parts, name
        return comm

    def exchange(swapped, halves=False):
        sums = _pair_sum(f"pair_sum_{swapped.weight}", swapped.parts, swapped.out[0], core)
        if not halves:
            comm = _chip_comm([sums])
            comm.weight = swapped.weight
            return comm
        comm = _chip_comm([sums], first)
        comm.sums, comm.weight = sums, swapped.weight
        return comm

    def rest(comm):
        other = _chip_comm([comm.sums], second, into=comm.out)
        other.weight = comm.weight
        return other

    def done(comm):
        by_chip[comm.weight] = comm.out[0]

    dw2_1 = _mm_tn("mlp1_dw2", q1, grad16).reshape(N_DEV, D_FF // N_DEV, D_MODEL)
    s_a = swap("w2_1", dw2_1)
    dp = _mm_nt("mlp1_dq", grad16, w2[1], D_FF, _ep_relu_sq_grad, [BF16], extras=(q1,), comm=s_a)[0]
    c_a = exchange(s_a, halves=True)
    dw1_1 = _mm_tn("mlp1_dw1", n3, dp, col_blocks=N_DEV, comm=c_a)
    c_a2, s_b = rest(c_a), swap("w1_1", dw1_1)
    dn = _mm_nt("mlp1_dn", dp, w1[1], d, _ep_store, [F32], comm=[c_a2, s_b])[0]
    done(c_a2)
    c_b = exchange(s_b, halves=True)
    grad, grad16, dg_mlp1 = _rms_bwd("mlp1_dn_norm", dn, h3, mlp_g[1], grad)
    d_od_out = _mm_tn("od_dw_out", y, grad16).reshape(N_DEV, D_MODEL // N_DEV, D_MODEL)
    s_c = swap("od_out", d_od_out)
    dy = _mm_nt("od_dy", grad16, w_od_out, C_DIM, _ep_store, [F32], comm=s_c)[0]
    c_c = exchange(s_c)
    dpre, d_ws, d_bsb, d_lvg, d_lvb, d_bin = _sgu_bwd(pre, dy, lvg, lvb, ws, bs_b, comm=[c_b, c_c])
    done(c_c)
    c_b2 = rest(c_b)
    d_od_in = _mm_tn("od_dw_in", n2, dpre, col_blocks=N_DEV, comm=c_b2)
    done(c_b2)
    s_d = swap("od_in", d_od_in)
    dn = _mm_nt("od_dn", dpre, w_od_in, d, _ep_store, [F32], comm=s_d)[0]
    c_d = exchange(s_d)
    grad, grad16, d_od_g = _rms_bwd("od_dn_norm", dn, h2, od_g, grad)
    dw2_0 = _mm_tn("mlp0_dw2", q0, grad16, comm=c_d).reshape(N_DEV, D_FF // N_DEV, D_MODEL)
    done(c_d)
    s_e = swap("w2_0", dw2_0)
    dp = _mm_nt("mlp0_dq", grad16, w2[0], D_FF, _ep_relu_sq_grad, [BF16], extras=(q0,), comm=s_e)[0]
    c_e = exchange(s_e, halves=True)
    dw1_0 = _mm_tn("mlp0_dw1", n1, dp, col_blocks=N_DEV, comm=c_e)
    c_e2, s_f = rest(c_e), swap("w1_0", dw1_0)
    dn = _mm_nt("mlp0_dn", dp, w1[0], d, _ep_store, [F32], comm=[c_e2, s_f])[0]
    done(c_e2)
    c_f = exchange(s_f)
    grad, grad16, dg_mlp0 = _rms_bwd("mlp0_dn_norm", dn, h1, mlp_g[0], grad)
    d_ev_out = _mm_tn("ev_dw_out", mix, grad16).reshape(N_DEV, D_MODEL // N_DEV, D_MODEL)
    s_g = swap("ev_out", d_ev_out)
    dmix = _mm_nt("ev_dmix", grad16, w_ev_out, A_DIM + B_DIM, _ep_store, [F32], comm=s_g)[0]
    c_g = exchange(s_g)
    da2, dcv, d_caw, d_cab, d_lag, d_lab, d_cbw = _mixer_bwd_local(z, dmix, seq, caw, cab, lag, lab, cbw, comm=[c_f, c_g])
    done(c_f)
    done(c_g)
    dz = _mixer_bwd_input(z, dmix, da2, dcv, seq, caw, cbw)
    small_parts = dict(
        ev_conv_a_w=d_caw, ev_conv_a_b=d_cab, ev_ln_a_g=d_lag, ev_ln_a_b=d_lab,
        ev_conv_b_w=d_cbw, od_norm_g=d_od_g, od_b_in=d_bin, od_ln_v_g=d_lvg, od_ln_v_b=d_lvb,
        od_w_s=d_ws.reshape(C_GROUPS * CHUNK, CHUNK), od_b_s=d_bsb[:, :, 0],
        mlp_norm_g=jnp.concatenate([dg_mlp0, dg_mlp1], axis=0), final_norm_g=d_fin_g)
    early = [n for n in SMALL if n != "ev_norm_g"]
    small_gather = _gather_comm([small_parts[n] for n in early] + [jnp.broadcast_to(loss_part, (SUBLANES, LANE_BLOCK))])
    d_ev_in = _mm_tn("ev_dw_in", n0, dz, comm=small_gather)
    d_ev_in = jnp.transpose(d_ev_in.reshape(D_MODEL, N_DEV, IN_EVEN // N_DEV), (1, 0, 2))
    s_h = swap("ev_in", d_ev_in)
    dn = _mm_nt("ev_dn", dz, w_ev_in, d, _ep_store, [F32], comm=s_h)[0]
    c_h = exchange(s_h)
    grad_x, _, d_ev_g = _rms_bwd("ev_dn_norm", dn, h0, ev_g, grad)

    last_gather = _gather_comm([d_ev_g])
    shard = {"od_w_out": ("od_out",), "mlp_w1": ("w1_0", "w1_1"), "mlp_w2": ("w2_0", "w2_1"), "od_w_in": ("od_in",),
             "ev_w_out": ("ev_out",), "ev_w_in": ("ev_in",)}
    carried = {"od_w_out": [c_h, last_gather]}
    out_g, out_d, out_m, out_v = {}, {}, {}, {}
    for name, keys in shard.items():
        out_g[name], out_d[name], out_m[name], out_v[name] = _sum_adamw(
            f"adamw_{name}", [by_chip[key] for key in keys], W[name], M[name], V[name], comm=carried.get(name))
        if name == "od_w_out":
            done(c_h)

    gathered = dict(zip(early, small_gather.out[:-1]), ev_norm_g=last_gather.out[0])
    mine = []
    for n in SMALL:
        g = gathered[n]
        if n in SMALL_SHARDED:
            width = W[n].shape[-1]
            g = lax.dynamic_slice_in_dim(g, dev * width, width, axis=2)
        mine.append(g)
    res, loss_tile = _small_adamw("adamw_small", mine, [_rows(W[n]) for n in SMALL], [_rows(M[n]) for n in SMALL],
                                  [_rows(V[n]) for n in SMALL], small_gather.out[-1])
    loss = loss_tile[0, 0]
    for store, values in zip((out_g, out_d, out_m, out_v), res):
        for n, value in zip(SMALL, values):
            store[n] = value.reshape(W[n].shape)

    return (loss, grad_x.reshape(n_seq, seq, d), *[out_g[n] for n in ORDER], *[out_d[n] for n in ORDER],
            *[out_m[n] for n in ORDER], *[out_v[n] for n in ORDER])
```

```python
import math

import jax
import jax.numpy as jnp
from jax import lax
from jax.experimental import pallas as pl
from jax.experimental.pallas import tpu as pltpu

F32 = jnp.float32
BF16 = jnp.bfloat16
MESH = pl.DeviceIdType.MESH

D_MODEL = 1024
A_DIM = 512
B_DIM = 512
IN_EVEN = 2 * A_DIM + 3 * B_DIM
A_TAPS = 31
B_TAPS = 3
CHUNK = 128
C_GROUPS = 8
C_DIM = 1024
D_FF = 4096
RMS_EPS = 1e-6
LN_EPS = 1e-5
N_DEV = 8
N_CHIP = 4

ADAM_LR = 0.001
ADAM_B1 = 0.9
ADAM_B2 = 0.999
ADAM_EPS = 1e-08
ADAM_WD = 0.01
ADAM_STEP = 10

A_HALO = 32
B_HALO = 8
VMEM_LIMIT_BYTES = 56 * 1024 * 1024
INV_SQRT2 = 1.0 / math.sqrt(2.0)
INV_SQRT_2PI = 1.0 / math.sqrt(2.0 * math.pi)
HBM_SPEC = pl.BlockSpec(memory_space=pltpu.HBM)


def _tile(n, want):
    t = min(n, want)
    while n % t:
        t //= 2
    return t


def _sigmoid(x):
    return 1.0 / (1.0 + jnp.exp(-x))


def _gelu(x):
    return 0.5 * x * (1.0 + lax.erf(x * INV_SQRT2))


def _gelu_and_grad(x):
    cdf = 0.5 * (1.0 + lax.erf(x * INV_SQRT2))
    return x * cdf, cdf + x * jnp.exp(-0.5 * x * x) * INV_SQRT_2PI


def _colsum(x):
    return jnp.sum(x, axis=0, keepdims=True)


class _Comm:
    def __init__(self, ins, out_shapes, sem_shapes, start, finish, middle=None, into=None, sibling=False, chips=False):
        self.ins, self.out_shapes, self.sem_shapes, self.start, self.finish = ins, out_shapes, sem_shapes, start, finish
        self.middle = middle
        self.sibling, self.chips = sibling, chips
        self.into = list(into) if into is not None else []
        self.out = None


def _piece_rows(rows, piece):
    if piece is None:
        return 0, rows
    i, n = piece
    return i * (rows // n), rows // n


MIDDLE_AT = 0.75
BARRIER_IDS = {(True, True): 0, (True, False): 1, (False, True): 2}
FUSED_ROWS = 512


def _call(body, *, name, grid, in_specs, out_specs, out_shape, args, scratch_shapes=(), parallel=False, comm=None):
    comms = [] if comm is None else (list(comm) if isinstance(comm, (list, tuple)) else [comm])
    if not comms:
        sem = ("parallel" if parallel else "arbitrary",) * len(grid)
        return pl.pallas_call(
            body, name=name, grid=grid, in_specs=list(in_specs), out_specs=list(out_specs), out_shape=list(out_shape),
            scratch_shapes=list(scratch_shapes),
            compiler_params=pltpu.CompilerParams(dimension_semantics=sem, vmem_limit_bytes=VMEM_LIMIT_BYTES),
        )(*args)
    n_in, n_out, n_scr = len(in_specs), len(out_shape), len(scratch_shapes)
    c_ins_all = [a for cm in comms for a in cm.ins]
    c_into_all = [a for cm in comms for a in cm.into]
    c_out_shapes = [s for cm in comms for s in cm.out_shapes]
    c_sem_shapes = [s for cm in comms for s in cm.sem_shapes]
    aliases, in_pos, out_pos = {}, n_in + len(c_ins_all), n_out
    for cm in comms:
        for j in range(len(cm.into)):
            aliases[in_pos + j] = out_pos + j
        in_pos += len(cm.into)
        out_pos += len(cm.out_shapes)
    to_sibling = any(cm.sibling for cm in comms)
    to_chips = any(cm.chips for cm in comms)
    steps = grid
    total = math.prod(steps)
    first_step = (0,) * len(steps)
    last_step = tuple(s - 1 for s in steps)
    middle_step = None
    if 0 < int(MIDDLE_AT * total) < total - 1:
        rest, idx = int(MIDDLE_AT * total), []
        for s in reversed(steps):
            idx.append(rest % s)
            rest //= s
        middle_step = tuple(reversed(idx))

    def carrying(*refs):
        pos = 0
        ins = refs[pos:pos + n_in]; pos += n_in
        c_ins = refs[pos:pos + len(c_ins_all)]; pos += len(c_ins_all) + len(c_into_all)
        outs = refs[pos:pos + n_out]; pos += n_out
        c_outs = refs[pos:pos + len(c_out_shapes)]; pos += len(c_out_shapes)
        scr = refs[pos:pos + n_scr]; pos += n_scr
        c_sems = refs[pos:]
        views, i0, o0, s0 = [], 0, 0, 0
        for cm in comms:
            views.append((c_ins[i0:i0 + len(cm.ins)], c_outs[o0:o0 + len(cm.out_shapes)], c_sems[s0:s0 + len(cm.sem_shapes)]))
            i0, o0, s0 = i0 + len(cm.ins), o0 + len(cm.out_shapes), s0 + len(cm.sem_shapes)

        def at(step):
            hit = pl.program_id(0) == step[0]
            for axis in range(1, len(steps)):
                hit = jnp.logical_and(hit, pl.program_id(axis) == step[axis])
            return hit

        @pl.when(at(first_step))
        def _():
            x, y, c, chips = _place()
            peers = ([(x, y, 1 - c)] if to_sibling else []) + ([(*chip, c) for chip in chips] if to_chips else [])
            barrier = pltpu.get_barrier_semaphore()
            for peer in peers:
                pl.semaphore_signal(barrier, inc=1, device_id=peer, device_id_type=MESH)
            pl.semaphore_wait(barrier, len(peers))
            for cm, view in zip(comms, views):
                cm.start(*view)

        if middle_step is not None:
            @pl.when(at(middle_step))
            def _():
                for cm, view in zip(comms, views):
                    if cm.middle is not None:
                        cm.middle(*view)

        body(*ins, *outs, *scr)

        @pl.when(at(last_step))
        def _():
            for cm, view in zip(comms, views):
                if cm.middle is not None and middle_step is None:
                    cm.middle(*view)
            for cm, view in zip(comms, views):
                cm.finish(*view)

    res = pl.pallas_call(
        carrying, name=name, grid=grid,
        in_specs=[*in_specs, *[HBM_SPEC] * (len(c_ins_all) + len(c_into_all))],
        out_specs=[*out_specs, *[HBM_SPEC] * len(c_out_shapes)],
        out_shape=[*out_shape, *c_out_shapes], scratch_shapes=[*scratch_shapes, *c_sem_shapes],
        input_output_aliases=aliases,
        compiler_params=pltpu.CompilerParams(dimension_semantics=("arbitrary",) * len(grid), vmem_limit_bytes=VMEM_LIMIT_BYTES,
                                             collective_id=BARRIER_IDS[to_sibling, to_chips]),
    )(*args, *c_ins_all, *c_into_all)
    pos = n_out
    for cm in comms:
        cm.out = list(res[pos:pos + len(cm.out_shapes)])
        pos += len(cm.out_shapes)
    return list(res[:n_out])


def _place():
    x, y, c = lax.axis_index("x"), lax.axis_index("y"), lax.axis_index("c")
    return x, y, c, [(1 - x, y), (x, 1 - y), (1 - x, 1 - y)]


def _gather_comm(shards, piece=None, into=None):
    nw = len(shards)
    spans = [_piece_rows(s.shape[0], piece) for s in shards]

    def plan(ins, outs, sems):
        send_sems, recv_sems, local_sems = sems
        x, y, c, chips = _place()
        me, sibling = (x, y, c), (x, y, 1 - c)

        def slot(w, p):
            return outs[w].at[4 * p[0] + 2 * p[1] + p[2], pl.ds(*spans[w])]

        def mine(w):
            return ins[w].at[pl.ds(*spans[w])]

        def copy(w, k, block, to, src=None):
            return pltpu.make_async_remote_copy(
                src_ref=slot(w, block) if src is None else src, dst_ref=slot(w, block),
                send_sem=send_sems.at[w, k], recv_sem=recv_sems.at[w, k], device_id=to, device_id_type=MESH)

        local = [pltpu.make_async_copy(mine(w), slot(w, me), local_sems.at[w]) for w in range(nw)]
        first = [[copy(w, 0, me, sibling, src=mine(w))] + [copy(w, 1 + j, me, (*chip, c), src=mine(w)) for j, chip in enumerate(chips)]
                 for w in range(nw)]
        landed = [[copy(w, 1 + j, (*chip, c), me) for j, chip in enumerate(chips)] for w in range(nw)]
        passed = [[copy(w, 4 + j, (*chip, c), sibling) for j, chip in enumerate(chips)] for w in range(nw)]
        from_sibling = [[copy(w, 0, sibling, me)] + [copy(w, 4 + j, (*chip, 1 - c), me) for j, chip in enumerate(chips)]
                        for w in range(nw)]
        return local, first, landed, passed, from_sibling

    def start(ins, outs, sems):
        local, first, _, _, _ = plan(ins, outs, sems)
        for cp in local:
            cp.start()
        for row in first:
            for cp in row:
                cp.start()

    def middle(ins, outs, sems):
        _, _, landed, passed, _ = plan(ins, outs, sems)
        for w in range(nw):
            for j in range(3):
                landed[w][j].wait_recv()
                passed[w][j].start()

    def finish(ins, outs, sems):
        local, first, landed, passed, from_sibling = plan(ins, outs, sems)
        for w in range(nw):
            for cp in from_sibling[w]:
                cp.wait_recv()
        for w in range(nw):
            for cp in first[w] + passed[w]:
                cp.wait_send()
        for cp in local:
            cp.wait()

    return _Comm(list(shards), [jax.ShapeDtypeStruct((N_DEV, *s.shape), s.dtype) for s in shards],
                 [pltpu.SemaphoreType.DMA((nw, 7)), pltpu.SemaphoreType.DMA((nw, 7)), pltpu.SemaphoreType.DMA((nw,))],
                 start, finish, middle, into=into, sibling=True, chips=True)


def _pair_comm(parts):
    nw = len(parts)

    def plan(ins, outs, sems):
        send_sems, recv_sems = sems
        x, y, c, _ = _place()
        return [pltpu.make_async_remote_copy(
            src_ref=ins[w].at[2 * q + 1 - c], dst_ref=outs[w].at[q], send_sem=send_sems.at[w, q], recv_sem=recv_sems.at[w, q],
            device_id=(x, y, 1 - c), device_id_type=MESH) for w in range(nw) for q in range(N_CHIP)]

    def start(ins, outs, sems):
        for cp in plan(ins, outs, sems):
            cp.start()

    def finish(ins, outs, sems):
        for cp in plan(ins, outs, sems):
            cp.wait()

    return _Comm(list(parts), [jax.ShapeDtypeStruct((N_CHIP, *p.shape[1:]), p.dtype) for p in parts],
                 [pltpu.SemaphoreType.DMA((nw, N_CHIP)), pltpu.SemaphoreType.DMA((nw, N_CHIP))], start, finish, sibling=True)


def _chip_comm(sums, piece=None, into=None):
    nw = len(sums)
    spans = [_piece_rows(s.shape[1], piece) for s in sums]

    def plan(ins, outs, sems):
        send_sems, recv_sems, local_sems = sems
        x, y, c, chips = _place()
        my_chip = 2 * x + y
        local = [pltpu.make_async_copy(ins[w].at[my_chip, pl.ds(*spans[w])], outs[w].at[my_chip, pl.ds(*spans[w])],
                                       local_sems.at[w]) for w in range(nw)]
        remote = [pltpu.make_async_remote_copy(
            src_ref=ins[w].at[2 * chip[0] + chip[1], pl.ds(*spans[w])], dst_ref=outs[w].at[my_chip, pl.ds(*spans[w])],
            send_sem=send_sems.at[w, j], recv_sem=recv_sems.at[w, j], device_id=(*chip, c), device_id_type=MESH)
            for w in range(nw) for j, chip in enumerate(chips)]
        return local, remote

    def start(ins, outs, sems):
        local, remote = plan(ins, outs, sems)
        for cp in local + remote:
            cp.start()

    def finish(ins, outs, sems):
        local, remote = plan(ins, outs, sems)
        for cp in remote + local:
            cp.wait()

    return _Comm(list(sums), [jax.ShapeDtypeStruct(s.shape, s.dtype) for s in sums],
                 [pltpu.SemaphoreType.DMA((nw, 3)), pltpu.SemaphoreType.DMA((nw, 3)), pltpu.SemaphoreType.DMA((nw,))],
                 start, finish, into=into, chips=True)


def _matmul(name, a, b, *, kind, m, n, k, a_spec, b_spec, tm, tn, tk, out_shape, out_specs, epilogue,
            extras=(), extra_specs=(), comm=None):
    dims = {"nn": (((1,), (0,)), ((), ())), "nt": (((1,), (1,)), ((), ())), "tn": (((0,), (0,)), ((), ()))}[kind]
    nk = k // tk
    n_extra = len(extras)
    n_out = len(out_shape)

    def body(a_ref, b_ref, *rest):
        extra_refs = rest[:n_extra]
        out_refs = rest[n_extra:n_extra + n_out]
        part = lax.dot_general(a_ref[...], b_ref[...], dims, preferred_element_type=F32)
        if nk == 1:
            epilogue(part, extra_refs, out_refs)
            return
        acc_ref = rest[n_extra + n_out]
        step = pl.program_id(2)

        @pl.when(step == 0)
        def _():
            acc_ref[...] = part

        @pl.when(jnp.logical_and(step > 0, step < nk - 1))
        def _():
            acc_ref[...] += part

        @pl.when(step == nk - 1)
        def _():
            epilogue(acc_ref[...] + part, extra_refs, out_refs)

    return _call(
        body, name=name, grid=(m // tm, n // tn, nk), in_specs=[a_spec, b_spec, *extra_specs], out_specs=out_specs,
        out_shape=out_shape, scratch_shapes=[pltpu.VMEM((tm, tn), F32)] if nk > 1 else [], args=(a, b, *extras), comm=comm)


def _mm_operands(m, n, tm, tn, out_dtypes, extras, rows, sums):
    tile = pl.BlockSpec((tm, tn), lambda i, j, kk: (i, j))
    row = pl.BlockSpec((1, tn), lambda i, j, kk: (0, j))
    one = pl.BlockSpec((1, 1), lambda i, j, kk: (0, 0))
    out_shape = [jax.ShapeDtypeStruct((m, n), dt) for dt in out_dtypes]
    out_shape += [jax.ShapeDtypeStruct((1, n if wide else 1), F32) for wide in sums]
    out_specs = [tile] * len(out_dtypes) + [row if wide else one for wide in sums]
    return (*extras, *rows), [tile] * len(extras) + [row] * len(rows), out_shape, out_specs


def _row_tile(m, k, tm_want):
    return _tile(m, tm_want or (1024 if k <= 1024 else 512))


def _mm_nn(name, a, b, n, epilogue, out_dtypes, *, extras=(), rows=(), sums=(), tm_want=None, comm=None):
    m, k = a.shape
    blocked = b.ndim == 3
    tm = _row_tile(m, k, tm_want)
    tn = b.shape[-1] if blocked else _tile(n, 1024)
    tk = k
    extras, extra_specs, out_shape, out_specs = _mm_operands(m, n, tm, tn, out_dtypes, extras, rows, sums)
    if blocked:
        b_spec = pl.BlockSpec((None, tk, tn), lambda i, j, kk: (j, kk, 0))
    else:
        b_spec = pl.BlockSpec((tk, tn), lambda i, j, kk: (kk, j))
    return _matmul(
        name, a, b, kind="nn", m=m, n=n, k=k, tm=tm, tn=tn, tk=tk,
        a_spec=pl.BlockSpec((tm, tk), lambda i, j, kk: (i, kk)), b_spec=b_spec, out_shape=out_shape, out_specs=out_specs,
        epilogue=epilogue, extras=extras, extra_specs=extra_specs, comm=comm)


def _mm_nt(name, a, b, n, epilogue, out_dtypes, *, extras=(), rows=(), sums=(), tm_want=None, comm=None):
    m, k = a.shape
    blocked = b.ndim == 3
    tm = _row_tile(m, k, tm_want)
    tn = _tile(n, 1024)
    tk = b.shape[-1] if blocked else k
    extras, extra_specs, out_shape, out_specs = _mm_operands(m, n, tm, tn, out_dtypes, extras, rows, sums)
    if blocked:
        b_spec = pl.BlockSpec((None, tn, tk), lambda i, j, kk: (kk, j, 0))
    else:
        b_spec = pl.BlockSpec((tn, tk), lambda i, j, kk: (j, kk))
    return _matmul(
        name, a, b, kind="nt", m=m, n=n, k=k, tm=tm, tn=tn, tk=tk,
        a_spec=pl.BlockSpec((tm, tk), lambda i, j, kk: (i, kk)), b_spec=b_spec, out_shape=out_shape, out_specs=out_specs,
        epilogue=epilogue, extras=extras, extra_specs=extra_specs, comm=comm)


def _mm_tn(name, a, b, *, col_blocks=0, comm=None):
    t, k = a.shape
    n = b.shape[1]
    tm = _row_tile(k, t, None)
    tn = n // col_blocks if col_blocks else _tile(n, 1024)
    tk = t
    if col_blocks:
        out_shape = [jax.ShapeDtypeStruct((col_blocks, k, tn), F32)]
        out_specs = [pl.BlockSpec((None, tm, tn), lambda i, j, kk: (j, i, 0))]
    else:
        out_shape = [jax.ShapeDtypeStruct((k, n), F32)]
        out_specs = [pl.BlockSpec((tm, tn), lambda i, j, kk: (i, j))]

    def epilogue(acc, extra_refs, out_refs):
        out_refs[0][...] = acc

    return _matmul(
        name, a, b, kind="tn", m=k, n=n, k=t, tm=tm, tn=tn, tk=tk,
        a_spec=pl.BlockSpec((tk, tm), lambda i, j, kk: (kk, i)), b_spec=pl.BlockSpec((tk, tn), lambda i, j, kk: (kk, j)),
        out_shape=out_shape, out_specs=out_specs, epilogue=epilogue, comm=comm)[0]


def _ep_store(acc, extra_refs, out_refs):
    out_refs[0][...] = acc.astype(out_refs[0].dtype)


def _ep_bias(acc, extra_refs, out_refs):
    out_refs[0][...] = acc + extra_refs[0][...]


def _rms(x):
    r = lax.rsqrt(jnp.mean(x * x, axis=-1, keepdims=True) + RMS_EPS)
    return r, x * r


def _rms_grad(dn, r, xr, g):
    dy = dn * g
    return r * (dy - xr * jnp.mean(dy * xr, axis=-1, keepdims=True))


def _ep_residual_norm(acc, extra_refs, out_refs):
    h = extra_refs[0][...] + acc
    out_refs[0][...] = h
    _, xr = _rms(h)
    out_refs[1][...] = (xr * extra_refs[1][...]).astype(BF16)


def _ep_final_loss(acc, extra_refs, out_refs):
    @pl.when(pl.program_id(0) == 0)
    def _():
        out_refs[2][...] = jnp.zeros_like(out_refs[2])
        out_refs[3][...] = jnp.zeros_like(out_refs[3])

    h = extra_refs[0][...] + acc
    g = extra_refs[2][...]
    r, xr = _rms(h)
    err = xr * g - extra_refs[1][...]
    out_refs[3][...] += 0.5 * jnp.sum(jnp.mean(err * err, axis=-1, keepdims=True), axis=0, keepdims=True)
    dout = err * (1.0 / h.shape[-1])
    out_refs[2][...] += _colsum(dout * xr)
    out = _rms_grad(dout, r, xr, g)
    out_refs[0][...] = out
    out_refs[1][...] = out.astype(BF16)


def _ep_relu_sq(acc, extra_refs, out_refs):
    r = jnp.maximum(acc, 0.0)
    out_refs[0][...] = (r * r).astype(BF16)


def _ep_relu_sq_grad(acc, extra_refs, out_refs):
    out_refs[0][...] = (acc * (2.0 * jnp.sqrt(extra_refs[0][...].astype(F32)))).astype(BF16)


def _rms_fwd(name, h, g, comm=None):
    t, d = h.shape
    tt = _tile(t, 512)

    def body(h_ref, g_ref, n_ref):
        x = h_ref[...]
        r = lax.rsqrt(jnp.mean(x * x, axis=-1, keepdims=True) + RMS_EPS)
        n_ref[...] = (x * r * g_ref[...]).astype(BF16)

    return _call(
        body, name=name, grid=(t // tt,),
        in_specs=[pl.BlockSpec((tt, d), lambda i: (i, 0)), pl.BlockSpec((1, d), lambda i: (0, 0))],
        out_specs=[pl.BlockSpec((tt, d), lambda i: (i, 0))], out_shape=[jax.ShapeDtypeStruct((t, d), BF16)],
        args=(h, g), parallel=True, comm=comm)[0]


def _rms_bwd(name, dn, h, g, grad_in, comm=None):
    t, d = h.shape
    tt = _tile(t, 512)

    def body(dn_ref, h_ref, g_ref, gin_ref, gout_ref, gout16_ref, dg_ref):
        @pl.when(pl.program_id(0) == 0)
        def _():
            dg_ref[...] = jnp.zeros_like(dg_ref)

        dnv = dn_ref[...]
        r, xr = _rms(h_ref[...])
        dg_ref[...] += _colsum(dnv * xr)
        out = gin_ref[...] + _rms_grad(dnv, r, xr, g_ref[...])
        gout_ref[...] = out
        gout16_ref[...] = out.astype(BF16)

    row = pl.BlockSpec((tt, d), lambda i: (i, 0))
    vec = pl.BlockSpec((1, d), lambda i: (0, 0))
    return _call(
        body, name=name, grid=(t // tt,), in_specs=[row, row, vec, row], out_specs=[row, row, vec],
        out_shape=[jax.ShapeDtypeStruct((t, d), F32), jax.ShapeDtypeStruct((t, d), BF16), jax.ShapeDtypeStruct((1, d), F32)],
        args=(dn, h, g, grad_in), comm=comm)


def _mixer_windows(z_ref, zh_ref, first, a1_s, cb_s, tt):
    sig = _sigmoid(z_ref[:, A_DIM:2 * A_DIM])
    a1_s[A_HALO:A_HALO + tt, :] = z_ref[:, 0:A_DIM] * sig
    a1_h = zh_ref[:, 0:A_DIM] * _sigmoid(zh_ref[:, A_DIM:2 * A_DIM])
    a1_s[0:A_HALO, :] = jnp.where(first, 0.0, a1_h)
    cb_s[B_HALO:B_HALO + tt, :] = z_ref[:, 3 * A_DIM:4 * A_DIM] * z_ref[:, 4 * A_DIM:5 * A_DIM]
    cb_h = zh_ref[A_HALO - B_HALO:A_HALO, 3 * A_DIM:4 * A_DIM] * zh_ref[A_HALO - B_HALO:A_HALO, 4 * A_DIM:5 * A_DIM]
    cb_s[0:B_HALO, :] = jnp.where(first, 0.0, cb_h)
    return sig


def _causal_conv(win_s, w_ref, taps, halo, tt):
    base = halo - (taps - 1)
    acc = w_ref[0:1, :] * win_s[pl.ds(base, tt), :]
    for k in range(1, taps):
        acc = acc + w_ref[k:k + 1, :] * win_s[pl.ds(base + k, tt), :]
    return acc


SUBLANES = 8
LANE_BLOCK = 128
ROW_BLOCK = 128
SHIFT_ROWS = A_HALO - SUBLANES


def _shifted_copies(win_s, sh_s, tt):
    for b in range(1, SUBLANES):
        sh_s[b - 1] = win_s[pl.ds(b, tt + SHIFT_ROWS), :]


def _window_rows(win_s, sh_s, offset, rows, cols):
    b = offset % SUBLANES
    if b == 0:
        return win_s[pl.ds(offset, rows), cols]
    return sh_s[b - 1, pl.ds(offset - b, rows), cols]


def _blocks(tt):
    rb = min(tt, ROW_BLOCK)
    return rb, [(r, slice(lb * LANE_BLOCK, (lb + 1) * LANE_BLOCK))
                for lb in range(A_DIM // LANE_BLOCK) for r in range(0, tt, rb)]


def _conv_taps(win_s, sh_s, w_ref, offsets, out_s, tt, bias_ref=None):
    rb, blocks = _blocks(tt)
    for r, cols in blocks:
        acc = w_ref[0:1, cols] * _window_rows(win_s, sh_s, r + offsets[0], rb, cols)
        for k in range(1, len(offsets)):
            acc = acc + w_ref[k:k + 1, cols] * _window_rows(win_s, sh_s, r + offsets[k], rb, cols)
        out_s[r:r + rb, cols] = acc if bias_ref is None else acc + bias_ref[:, cols]


A_CAUSAL = [A_HALO - (A_TAPS - 1) + k for k in range(A_TAPS)]
A_ANTICAUSAL = [A_TAPS - 1 - k for k in range(A_TAPS)]


def _layer_norm_stats(x):
    mu = jnp.mean(x, axis=-1, keepdims=True)
    xc = x - mu
    rstd = lax.rsqrt(jnp.mean(xc * xc, axis=-1, keepdims=True) + LN_EPS)
    return xc * rstd, rstd


def _mixer_specs(seq, tt):
    tiles_per_seq = seq // tt
    halo_blocks = tt // A_HALO
    z_spec = pl.BlockSpec((tt, IN_EVEN), lambda i: (i, 0))
    zh_spec = pl.BlockSpec((A_HALO, IN_EVEN), lambda i: (jnp.maximum(i * halo_blocks - 1, 0), 0))
    return tiles_per_seq, z_spec, zh_spec


def _vec_spec(rows, cols):
    return pl.BlockSpec((rows, cols), lambda i: (0, 0))


def _mixer_fwd(z, seq, caw, cab, lag, lab, cbw, comm=None):
    t = z.shape[0]
    tt = _tile(seq, 256)
    tiles_per_seq, z_spec, zh_spec = _mixer_specs(seq, tt)

    def body(z_ref, zh_ref, caw_ref, cab_ref, lag_ref, lab_ref, cbw_ref, mix_ref, a1_s, cb_s, sh_s, a2_s):
        first = (pl.program_id(0) % tiles_per_seq) == 0
        _mixer_windows(z_ref, zh_ref, first, a1_s, cb_s, tt)
        _shifted_copies(a1_s, sh_s, tt)
        _conv_taps(a1_s, sh_s, caw_ref, A_CAUSAL, a2_s, tt, bias_ref=cab_ref)
        xhat, _ = _layer_norm_stats(a2_s[...])
        a3 = xhat * lag_ref[...] + lab_ref[...]
        mix_ref[:, 0:A_DIM] = (a3 * _sigmoid(a3)).astype(BF16)
        cv = _causal_conv(cb_s, cbw_ref, B_TAPS, B_HALO, tt)
        mix_ref[:, A_DIM:A_DIM + B_DIM] = (z_ref[:, 2 * A_DIM:3 * A_DIM] * cv).astype(BF16)

    return _call(
        body, name="mixer_fwd", grid=(t // tt,),
        in_specs=[z_spec, zh_spec, _vec_spec(A_TAPS, A_DIM), _vec_spec(1, A_DIM), _vec_spec(1, A_DIM), _vec_spec(1, A_DIM),
                  _vec_spec(B_TAPS, B_DIM)],
        out_specs=[pl.BlockSpec((tt, A_DIM + B_DIM), lambda i: (i, 0))],
        out_shape=[jax.ShapeDtypeStruct((t, A_DIM + B_DIM), BF16)],
        scratch_shapes=[pltpu.VMEM((A_HALO + tt, A_DIM), F32), pltpu.VMEM((B_HALO + tt, B_DIM), F32),
                        pltpu.VMEM((SUBLANES - 1, tt + SHIFT_ROWS, A_DIM), F32), pltpu.VMEM((tt, A_DIM), F32)],
        args=(z, z, caw, cab, lag, lab, cbw), parallel=True, comm=comm)[0]


def _mixer_bwd_local(z, dmix, seq, caw, cab, lag, lab, cbw, comm=None):
    t = z.shape[0]
    tt = _tile(seq, 256)
    tiles_per_seq, z_spec, zh_spec = _mixer_specs(seq, tt)

    def body(z_ref, zh_ref, dmix_ref, caw_ref, cab_ref, lag_ref, lab_ref, cbw_ref,
             da2_ref, dcv_ref, dcaw_ref, dcab_ref, dlag_ref, dlab_ref, dcbw_ref, a1_s, cb_s, sh_s, a2_s):
        @pl.when(pl.program_id(0) == 0)
        def _():
            for ref in (dcaw_ref, dcab_ref, dlag_ref, dlab_ref, dcbw_ref):
                ref[...] = jnp.zeros_like(ref)

        first = (pl.program_id(0) % tiles_per_seq) == 0
        _mixer_windows(z_ref, zh_ref, first, a1_s, cb_s, tt)
        _shifted_copies(a1_s, sh_s, tt)
        _conv_taps(a1_s, sh_s, caw_ref, A_CAUSAL, a2_s, tt, bias_ref=cab_ref)
        xhat, rstd = _layer_norm_stats(a2_s[...])
        a3 = xhat * lag_ref[...] + lab_ref[...]
        s3 = _sigmoid(a3)
        da3 = dmix_ref[:, 0:A_DIM] * (s3 * (1.0 + a3 * (1.0 - s3)))
        dlag_ref[...] += _colsum(da3 * xhat)
        dlab_ref[...] += _colsum(da3)
        dxh = da3 * lag_ref[...]
        da2 = rstd * (dxh - jnp.mean(dxh, axis=-1, keepdims=True) - xhat * jnp.mean(dxh * xhat, axis=-1, keepdims=True))
        da2_ref[...] = da2
        dcab_ref[...] += _colsum(da2)
        rb, blocks = _blocks(tt)
        for r, cols in blocks:
            da2_b = da2_ref[r:r + rb, cols]
            for k in range(A_TAPS):
                dcaw_ref[k:k + 1, cols] += _colsum(da2_b * _window_rows(a1_s, sh_s, r + A_CAUSAL[k], rb, cols))
        dcv =dmix_ref[:, A_DIM:A_DIM + B_DIM] * z_ref[:, 2 * A_DIM:3 * A_DIM]
        dcv_ref[...] = dcv
        for k in range(B_TAPS):
            dcbw_ref[k:k + 1, :] += _colsum(dcv * cb_s[pl.ds(B_HALO - (B_TAPS - 1) + k, tt), :])

    half = pl.BlockSpec((tt, A_DIM), lambda i: (i, 0))
    return _call(
        body, name="mixer_bwd_local", grid=(t // tt,),
        in_specs=[z_spec, zh_spec, pl.BlockSpec((tt, A_DIM + B_DIM), lambda i: (i, 0)),
                  _vec_spec(A_TAPS, A_DIM), _vec_spec(1, A_DIM), _vec_spec(1, A_DIM), _vec_spec(1, A_DIM), _vec_spec(B_TAPS, B_DIM)],
        out_specs=[half, half, _vec_spec(A_TAPS, A_DIM), _vec_spec(1, A_DIM), _vec_spec(1, A_DIM), _vec_spec(1, A_DIM),
                   _vec_spec(B_TAPS, B_DIM)],
        out_shape=[jax.ShapeDtypeStruct((t, A_DIM), F32), jax.ShapeDtypeStruct((t, B_DIM), F32),
                   jax.ShapeDtypeStruct((A_TAPS, A_DIM), F32), jax.ShapeDtypeStruct((1, A_DIM), F32),
                   jax.ShapeDtypeStruct((1, A_DIM), F32), jax.ShapeDtypeStruct((1, A_DIM), F32),
                   jax.ShapeDtypeStruct((B_TAPS, B_DIM), F32)],
        scratch_shapes=[pltpu.VMEM((A_HALO + tt, A_DIM), F32), pltpu.VMEM((B_HALO + tt, B_DIM), F32),
                        pltpu.VMEM((SUBLANES - 1, tt + SHIFT_ROWS, A_DIM), F32), pltpu.VMEM((tt, A_DIM), F32)],
        args=(z, z, dmix, caw, cab, lag, lab, cbw), comm=comm)


def _mixer_bwd_input(z, dmix, da2, dcv, seq, caw, cbw, comm=None):
    t = z.shape[0]
    tt = _tile(seq, 256)
    tiles_per_seq, z_spec, zh_spec = _mixer_specs(seq, tt)
    a_blocks = tt // A_HALO
    b_blocks = tt // B_HALO
    last_a = t // A_HALO - 1
    last_b = t // B_HALO - 1

    def body(z_ref, zh_ref, dmix_ref, da2_ref, da2n_ref, dcv_ref, dcvn_ref, caw_ref, cbw_ref, dz_ref, a1_s, cb_s, da2_s, dcv_s,
             sh_s, da1_s):
        pos = pl.program_id(0) % tiles_per_seq
        first = pos == 0
        last = pos == tiles_per_seq - 1
        sig = _mixer_windows(z_ref, zh_ref, first, a1_s, cb_s, tt)
        da2_s[0:tt, :] = da2_ref[...]
        da2_s[tt:tt + A_HALO, :] = jnp.where(last, 0.0, da2n_ref[...])
        dcv_s[0:tt, :] = dcv_ref[...]
        dcv_s[tt:tt + B_HALO, :] = jnp.where(last, 0.0, dcvn_ref[...])
        _shifted_copies(da2_s, sh_s, tt)
        _conv_taps(da2_s, sh_s, caw_ref, A_ANTICAUSAL, da1_s, tt)
        da1 = da1_s[...]
        dz_ref[:, 0:A_DIM] = (da1 * sig).astype(BF16)
        dz_ref[:, A_DIM:2 * A_DIM] = (da1 * z_ref[:, 0:A_DIM] * sig * (1.0 - sig)).astype(BF16)
        cv = _causal_conv(cb_s, cbw_ref, B_TAPS, B_HALO, tt)
        dz_ref[:, 2 * A_DIM:3 * A_DIM] = (dmix_ref[:, A_DIM:A_DIM + B_DIM] * cv).astype(BF16)
        dcb = cbw_ref[0:1, :] * dcv_s[pl.ds(B_TAPS - 1, tt), :]
        for k in range(1, B_TAPS):
            dcb = dcb + cbw_ref[k:k + 1, :] * dcv_s[pl.ds(B_TAPS - 1 - k, tt), :]
        dz_ref[:, 3 * A_DIM:4 * A_DIM] = (dcb * z_ref[:, 4 * A_DIM:5 * A_DIM]).astype(BF16)
        dz_ref[:, 4 * A_DIM:5 * A_DIM] = (dcb * z_ref[:, 3 * A_DIM:4 * A_DIM]).astype(BF16)

    half = pl.BlockSpec((tt, A_DIM), lambda i: (i, 0))
    return _call(
        body, name="mixer_bwd_input", grid=(t // tt,),
        in_specs=[z_spec, zh_spec, pl.BlockSpec((tt, A_DIM + B_DIM), lambda i: (i, 0)),
                  half, pl.BlockSpec((A_HALO, A_DIM), lambda i: (jnp.minimum((i + 1) * a_blocks, last_a), 0)),
                  half, pl.BlockSpec((B_HALO, B_DIM), lambda i: (jnp.minimum((i + 1) * b_blocks, last_b), 0)),
                  _vec_spec(A_TAPS, A_DIM), _vec_spec(B_TAPS, B_DIM)],
        out_specs=[pl.BlockSpec((tt, IN_EVEN), lambda i: (i, 0))],
        out_shape=[jax.ShapeDtypeStruct((t, IN_EVEN), BF16)],
        scratch_shapes=[pltpu.VMEM((A_HALO + tt, A_DIM), F32), pltpu.VMEM((B_HALO + tt, B_DIM), F32),
                        pltpu.VMEM((tt + A_HALO, A_DIM), F32), pltpu.VMEM((tt + B_HALO, B_DIM), F32),
                        pltpu.VMEM((SUBLANES - 1, tt + SHIFT_ROWS, A_DIM), F32), pltpu.VMEM((tt, A_DIM), F32)],
        args=(z, z, dmix, da2, da2, dcv, dcv, caw, cbw), parallel=True, comm=comm)[0]


def _tril_ws(ws_ref, g):
    rows = lax.broadcasted_iota(jnp.int32, (CHUNK, CHUNK), 0)
    cols = lax.broadcasted_iota(jnp.int32, (CHUNK, CHUNK), 1)
    return jnp.where(rows >= cols, ws_ref[g], 0.0).astype(BF16), rows >= cols


def _sgu_fwd(pre, lvg, lvb, ws, bs_b, comm=None):
    t = pre.shape[0]
    tt = _tile(t, 256)

    def body(pre_ref, lvg_ref, lvb_ref, ws_ref, bsb_ref, y_ref):
        vhat, _ = _layer_norm_stats(_gelu(pre_ref[:, C_DIM:2 * C_DIM]))
        vl = (vhat * lvg_ref[...] + lvb_ref[...]).astype(BF16)
        for g in range(C_GROUPS):
            w, _ = _tril_ws(ws_ref, g)
            cols = slice(g * CHUNK, (g + 1) * CHUNK)
            for ci in range(tt // CHUNK):
                rows = slice(ci * CHUNK, (ci + 1) * CHUNK)
                sv = jnp.dot(w, vl[rows, cols], preferred_element_type=F32) + bsb_ref[g]
                y_ref[rows, cols] = (_gelu(pre_ref[rows, cols]) * sv).astype(BF16)

    group = pl.BlockSpec((C_GROUPS, CHUNK, CHUNK), lambda i: (0, 0, 0))
    return _call(
        body, name="sgu_fwd", grid=(t // tt,),
        in_specs=[pl.BlockSpec((tt, 2 * C_DIM), lambda i: (i, 0)), _vec_spec(1, C_DIM), _vec_spec(1, C_DIM), group, group],
        out_specs=[pl.BlockSpec((tt, C_DIM), lambda i: (i, 0))], out_shape=[jax.ShapeDtypeStruct((t, C_DIM), BF16)],
        args=(pre, lvg, lvb, ws, bs_b), parallel=True, comm=comm)[0]


def _sgu_bwd(pre, dy, lvg, lvb, ws, bs_b, comm=None):
    t = pre.shape[0]
    tt = _tile(t, 256)

    def body(pre_ref, dy_ref, lvg_ref, lvb_ref, ws_ref, bsb_ref, dpre_ref, dws_ref, dbsb_ref, dlvg_ref, dlvb_ref, dbin_ref,
             dvl_s):
        @pl.when(pl.program_id(0) == 0)
        def _():
            for ref in (dws_ref, dbsb_ref, dlvg_ref, dlvb_ref, dbin_ref):
                ref[...] = jnp.zeros_like(ref)

        v, v_grad = _gelu_and_grad(pre_ref[:, C_DIM:2 * C_DIM])
        vhat, rstd = _layer_norm_stats(v)
        vl = (vhat * lvg_ref[...] + lvb_ref[...]).astype(BF16)
        for g in range(C_GROUPS):
            w, keep = _tril_ws(ws_ref, g)
            cols = slice(g * CHUNK, (g + 1) * CHUNK)
            dws = jnp.zeros((CHUNK, CHUNK), F32)
            dbs = jnp.zeros((CHUNK, 1), F32)
            for ci in range(tt // CHUNK):
                rows = slice(ci * CHUNK, (ci + 1) * CHUNK)
                vl_g = vl[rows, cols]
                sv = jnp.dot(w, vl_g, preferred_element_type=F32) + bsb_ref[g]
                u, u_grad = _gelu_and_grad(pre_ref[rows, cols])
                dyv = dy_ref[rows, cols]
                du = dyv * sv * u_grad
                dpre_ref[rows, cols] = du.astype(BF16)
                dbin_ref[:, cols] += _colsum(du)
                dsv = dyv * u
                dbs = dbs + jnp.sum(dsv, axis=1, keepdims=True)
                dsv16 = dsv.astype(BF16)
                dws = dws + lax.dot_general(dsv16, vl_g, (((1,), (1,)), ((), ())), preferred_element_type=F32)
                dvl_s[rows, cols] = lax.dot_general(w, dsv16, (((0,), (0,)), ((), ())), preferred_element_type=F32)
            dws_ref[g] += jnp.where(keep, dws, 0.0)
            dbsb_ref[g] += dbs
        dvl = dvl_s[...]
        dlvg_ref[...] += _colsum(dvl * vhat)
        dlvb_ref[...] += _colsum(dvl)
        dxh = dvl * lvg_ref[...]
        dv = rstd * (dxh - jnp.mean(dxh, axis=-1, keepdims=True) - vhat * jnp.mean(dxh * vhat, axis=-1, keepdims=True))
        dpv = dv * v_grad
        dpre_ref[:, C_DIM:2 * C_DIM] = dpv.astype(BF16)
        dbin_ref[:, C_DIM:2 * C_DIM] += _colsum(dpv)

    group = pl.BlockSpec((C_GROUPS, CHUNK, CHUNK), lambda i: (0, 0, 0))
    return _call(
        body, name="sgu_bwd", grid=(t // tt,),
        in_specs=[pl.BlockSpec((tt, 2 * C_DIM), lambda i: (i, 0)), pl.BlockSpec((tt, C_DIM), lambda i: (i, 0)),
                  _vec_spec(1, C_DIM), _vec_spec(1, C_DIM), group, group],
        out_specs=[pl.BlockSpec((tt, 2 * C_DIM), lambda i: (i, 0)), group, group,
                   _vec_spec(1, C_DIM), _vec_spec(1, C_DIM), _vec_spec(1, 2 * C_DIM)],
        out_shape=[jax.ShapeDtypeStruct((t, 2 * C_DIM), BF16), jax.ShapeDtypeStruct((C_GROUPS, CHUNK, CHUNK), F32),
                   jax.ShapeDtypeStruct((C_GROUPS, CHUNK, CHUNK), F32), jax.ShapeDtypeStruct((1, C_DIM), F32),
                   jax.ShapeDtypeStruct((1, C_DIM), F32), jax.ShapeDtypeStruct((1, 2 * C_DIM), F32)],
        scratch_shapes=[pltpu.VMEM((tt, C_DIM), F32)],
        args=(pre, dy, lvg, lvb, ws, bs_b), comm=comm)


def _pair_sum(name, part, got, core):
    _, k, n = part.shape
    tk = _tile(k, 1024)

    def body(core_ref, p_ref, s_ref, o_ref):
        o_ref[...] = (p_ref[...] + s_ref[...]).astype(BF16)

    return pl.pallas_call(
        body, name=name,
        grid_spec=pltpu.PrefetchScalarGridSpec(
            num_scalar_prefetch=1, grid=(N_CHIP, k // tk),
            in_specs=[pl.BlockSpec((None, tk, n), lambda q, i, core_ref: (2 * q + core_ref[0], i, 0)),
                      pl.BlockSpec((None, tk, n), lambda q, i, core_ref: (q, i, 0))],
            out_specs=pl.BlockSpec((None, tk, n), lambda q, i, core_ref: (q, i, 0))),
        out_shape=jax.ShapeDtypeStruct((N_CHIP, k, n), BF16),
        compiler_params=pltpu.CompilerParams(dimension_semantics=("parallel", "parallel"), vmem_limit_bytes=VMEM_LIMIT_BYTES),
    )(core, part, got)


def _adamw_math(w, g, m, v):
    m = ADAM_B1 * m + (1.0 - ADAM_B1) * g
    v = ADAM_B2 * v + (1.0 - ADAM_B2) * (g * g)
    m_hat = m / (1.0 - ADAM_B1 ** ADAM_STEP)
    v_hat = v / (1.0 - ADAM_B2 ** ADAM_STEP)
    delta = -ADAM_LR * (m_hat / (jnp.sqrt(v_hat) + ADAM_EPS) + ADAM_WD * w)
    return delta, m, v


def _sum_adamw(name, parts, w, m, v, comm=None):
    layers = len(parts)
    n_parts, k, n = parts[0].shape
    tk = _tile(k, 256)

    def body(*refs):
        p_refs = refs[:layers]
        w_ref, m_ref, v_ref, g_ref, d_ref, nm_ref, nv_ref = refs[layers:]

        def total(p_ref):
            g = p_ref[0].astype(F32)
            for q in range(1, n_parts):
                g = g + p_ref[q].astype(F32)
            return g

        g = total(p_refs[0])
        for l in range(1, layers):
            g = jnp.where(pl.program_id(0) == l, total(p_refs[l]), g)
        g_ref[...] = g
        d_ref[...], nm_ref[...], nv_ref[...] = _adamw_math(w_ref[...], g, m_ref[...], v_ref[...])

    blk = pl.BlockSpec((None, tk, n), lambda l, i: (l, i, 0))
    return _call(
        body, name=name, grid=(layers, k // tk),
        in_specs=[pl.BlockSpec((n_parts, tk, n), lambda l, i: (0, i, 0))] * layers + [blk, blk, blk], out_specs=[blk] * 4,
        out_shape=[jax.ShapeDtypeStruct((layers, k, n), F32)] * 4, args=(*parts, w, m, v), parallel=True, comm=comm)


def _small_adamw(name, parts, w, m, v, losses):
    count = len(parts)

    def in_order(ref):
        total = ref[0]
        for dev in range(1, N_DEV):
            total = total + ref[dev]
        return total

    def body(*refs):
        p_refs, w_refs, m_refs, v_refs = (refs[j * count:(j + 1) * count] for j in range(4))
        losses_ref = refs[4 * count]
        g_refs, d_refs, nm_refs, nv_refs = (refs[4 * count + 1 + j * count:4 * count + 1 + (j + 1) * count] for j in range(4))
        loss_ref = refs[8 * count + 1]
        for i in range(count):
            g = in_order(p_refs[i])
            g_refs[i][...] = g
            d_refs[i][...], nm_refs[i][...], nv_refs[i][...] = _adamw_math(w_refs[i][...], g, m_refs[i][...], v_refs[i][...])
        loss_ref[...] = in_order(losses_ref)

    res = pl.pallas_call(
        body, name=name,
        out_shape=[jax.ShapeDtypeStruct(a.shape, F32) for a in w] * 4 + [jax.ShapeDtypeStruct(losses.shape[1:], F32)],
        compiler_params=pltpu.CompilerParams(vmem_limit_bytes=VMEM_LIMIT_BYTES))(*parts, *w, *m, *v, losses)
    return [res[j * count:(j + 1) * count] for j in range(4)], res[4 * count]


def _rows(a):
    return a.reshape(-1, a.shape[-1])


def _whole(gathered):
    return jnp.transpose(gathered, (1, 0, 2)).reshape(gathered.shape[1], -1)


SMALL =("ev_norm_g", "ev_conv_a_w", "ev_conv_a_b", "ev_ln_a_g", "ev_ln_a_b", "ev_conv_b_w", "od_norm_g", "od_b_in",
         "od_ln_v_g", "od_ln_v_b", "od_w_s", "od_b_s", "mlp_norm_g", "final_norm_g")
SMALL_SHARDED = ("ev_conv_a_w", "ev_conv_b_w", "od_norm_g", "od_b_in", "od_ln_v_g", "od_ln_v_b")
ORDER = ("ev_norm_g", "ev_w_in", "ev_conv_a_w", "ev_conv_a_b", "ev_ln_a_g", "ev_ln_a_b", "ev_conv_b_w", "ev_w_out",
         "od_norm_g", "od_w_in", "od_b_in", "od_ln_v_g", "od_ln_v_b", "od_w_s", "od_b_s", "od_w_out", "mlp_norm_g",
         "mlp_w1", "mlp_w2", "final_norm_g")


def kernel(x, ev_norm_g, ev_w_in, ev_conv_a_w, ev_conv_a_b, ev_ln_a_g, ev_ln_a_b, ev_conv_b_w, ev_w_out, od_norm_g, od_w_in, od_b_in, od_ln_v_g, od_ln_v_b, od_w_s, od_b_s, od_w_out, mlp_norm_g, mlp_w1, mlp_w2, final_norm_g, loss_target, m_ev_norm_g, m_ev_w_in, m_ev_conv_a_w, m_ev_conv_a_b, m_ev_ln_a_g, m_ev_ln_a_b, m_ev_conv_b_w, m_ev_w_out, m_od_norm_g, m_od_w_in, m_od_b_in, m_od_ln_v_g, m_od_ln_v_b, m_od_w_s, m_od_b_s, m_od_w_out, m_mlp_norm_g, m_mlp_w1, m_mlp_w2, m_final_norm_g, v_ev_norm_g, v_ev_w_in, v_ev_conv_a_w, v_ev_conv_a_b, v_ev_ln_a_g, v_ev_ln_a_b, v_ev_conv_b_w, v_ev_w_out, v_od_norm_g, v_od_w_in, v_od_b_in, v_od_ln_v_g, v_od_ln_v_b, v_od_w_s, v_od_b_s, v_od_w_out, v_mlp_norm_g, v_mlp_w1, v_mlp_w2, v_final_norm_g):
    W = dict(ev_norm_g=ev_norm_g, ev_w_in=ev_w_in, ev_conv_a_w=ev_conv_a_w, ev_conv_a_b=ev_conv_a_b, ev_ln_a_g=ev_ln_a_g,
             ev_ln_a_b=ev_ln_a_b, ev_conv_b_w=ev_conv_b_w, ev_w_out=ev_w_out, od_norm_g=od_norm_g, od_w_in=od_w_in,
             od_b_in=od_b_in, od_ln_v_g=od_ln_v_g, od_ln_v_b=od_ln_v_b, od_w_s=od_w_s, od_b_s=od_b_s, od_w_out=od_w_out,
             mlp_norm_g=mlp_norm_g, mlp_w1=mlp_w1, mlp_w2=mlp_w2, final_norm_g=final_norm_g)
    M = dict(ev_norm_g=m_ev_norm_g, ev_w_in=m_ev_w_in, ev_conv_a_w=m_ev_conv_a_w, ev_conv_a_b=m_ev_conv_a_b,
             ev_ln_a_g=m_ev_ln_a_g, ev_ln_a_b=m_ev_ln_a_b, ev_conv_b_w=m_ev_conv_b_w, ev_w_out=m_ev_w_out,
             od_norm_g=m_od_norm_g, od_w_in=m_od_w_in, od_b_in=m_od_b_in, od_ln_v_g=m_od_ln_v_g, od_ln_v_b=m_od_ln_v_b,
             od_w_s=m_od_w_s, od_b_s=m_od_b_s, od_w_out=m_od_w_out, mlp_norm_g=m_mlp_norm_g, mlp_w1=m_mlp_w1,
             mlp_w2=m_mlp_w2, final_norm_g=m_final_norm_g)
    V = dict(ev_norm_g=v_ev_norm_g, ev_w_in=v_ev_w_in, ev_conv_a_w=v_ev_conv_a_w, ev_conv_a_b=v_ev_conv_a_b,
             ev_ln_a_g=v_ev_ln_a_g, ev_ln_a_b=v_ev_ln_a_b, ev_conv_b_w=v_ev_conv_b_w, ev_w_out=v_ev_w_out,
             od_norm_g=v_od_norm_g, od_w_in=v_od_w_in, od_b_in=v_od_b_in, od_ln_v_g=v_od_ln_v_g, od_ln_v_b=v_od_ln_v_b,
             od_w_s=v_od_w_s, od_b_s=v_od_b_s, od_w_out=v_od_w_out, mlp_norm_g=v_mlp_norm_g, mlp_w1=v_mlp_w1,
             mlp_w2=v_mlp_w2, final_norm_g=v_final_norm_g)

    n_seq, seq, d = x.shape
    t = n_seq * seq
    dev = 4 * lax.axis_index("x") + 2 * lax.axis_index("y") + lax.axis_index("c")
    core = lax.axis_index("c").astype(jnp.int32).reshape(1)

    ev_g, cab, lag, lab = W["ev_norm_g"], W["ev_conv_a_b"], W["ev_ln_a_g"], W["ev_ln_a_b"]
    ws = W["od_w_s"][0]
    bs_b = jnp.broadcast_to(W["od_b_s"][0][:, :, None], (C_GROUPS, CHUNK, CHUNK))
    mlp_g = [W["mlp_norm_g"][l:l + 1] for l in range(2)]
    fin_g = W["final_norm_g"].reshape(1, d)

    def w16(name, l=0):
        return W[name][l].astype(BF16)

    h0 = x.reshape(t, d)
    gather = _gather_comm([w16("ev_w_in")] + [_rows(W[n]) for n in SMALL_SHARDED])
    n0 = _rms_fwd("ev_norm", h0, ev_g, comm=gather)
    w_ev_in, caw, cbw, od_g, od_bin, lvg, lvb = [_whole(g) for g in gather.out]

    first, second = (0, 2), (1, 2)
    w1_0, w2_0, w1_1, w2_1 = w16("mlp_w1", 0), w16("mlp_w2", 0), w16("mlp_w1", 1), w16("mlp_w2", 1)

    g_a, g_b = _gather_comm([w16("ev_w_out")]), _gather_comm([w1_0], first)
    z = _mm_nn("ev_in", n0, w_ev_in, IN_EVEN, _ep_store, [F32], comm=[g_a, g_b])[0]
    w_ev_out = g_a.out[0].reshape(D_MODEL, D_MODEL)

    g_c, g_d = _gather_comm([w1_0], second, into=g_b.out), _gather_comm([w2_0], first)
    mix = _mixer_fwd(z, seq, caw, cab, lag, lab, cbw, comm=[g_c, g_d])
    w1 = [_whole(g_c.out[0]), None]

    g_e = _gather_comm([w2_0], second, into=g_d.out)
    h1, n1 = _mm_nn("ev_out", mix, w_ev_out, d, _ep_residual_norm, [F32, BF16], extras=(h0,), rows=(mlp_g[0],), comm=g_e)
    w2 = [g_e.out[0].reshape(D_FF, D_MODEL), None]

    g_f = _gather_comm([w16("od_w_in"), w16("od_w_out")])
    q0 = _mm_nn("mlp0_up", n1, w1[0], D_FF, _ep_relu_sq, [BF16], comm=g_f)[0]
    w_od_in = _whole(g_f.out[0])
    w_od_out = g_f.out[1].reshape(D_MODEL, D_MODEL)

    g_g = _gather_comm([w1_1], first)
    h2, n2 = _mm_nn("mlp0_down", q0, w2[0], d, _ep_residual_norm, [F32, BF16], extras=(h1,), rows=(od_g,), comm=g_g)
    g_h = _gather_comm([w1_1], second, into=g_g.out)
    pre = _mm_nn("od_in", n2, w_od_in, 2 * C_DIM, _ep_bias, [F32], rows=(od_bin,), comm=g_h)[0]
    w1[1] = _whole(g_h.out[0])
    g_i = _gather_comm([w2_1], first)
    y = _sgu_fwd(pre, lvg, lvb, ws, bs_b, comm=g_i)
    h3, n3 = _mm_nn("od_out", y, w_od_out, d, _ep_residual_norm, [F32, BF16], extras=(h2,), rows=(mlp_g[1],))
    g_j = _gather_comm([w2_1], second, into=g_i.out)
    q1 = _mm_nn("mlp1_up", n3, w1[1], D_FF, _ep_relu_sq, [BF16], comm=g_j)[0]
    w2[1] = g_j.out[0].reshape(D_FF, D_MODEL)
    grad, grad16, d_fin_g, loss_part = _mm_nn(
        "mlp1_down", q1, w2[1], d, _ep_final_loss, [F32, BF16], extras=(h3, loss_target.reshape(t, d)), rows=(fin_g,),
        sums=(True, False), tm_want=FUSED_ROWS)

    by_chip = {}

    def swap(name, parts):
        comm = _pair_comm([parts])
        comm.parts, comm.weight = parts, name
        return comm

    def exchange(swapped, halves=False):
        sums = _pair_sum(f"pair_sum_{swapped.weight}", swapped.parts, swapped.out[0], core)
        if not halves:
            comm = _chip_comm([sums])
            comm.weight = swapped.weight
            return comm
        comm = _chip_comm([sums], first)
        comm.sums, comm.weight = sums, swapped.weight
        return comm

    def rest(comm):
        other = _chip_comm([comm.sums], second, into=comm.out)
        other.weight = comm.weight
        return other

    def done(comm):
        by_chip[comm.weight] = comm.out[0]

    dw2_1 = _mm_tn("mlp1_dw2", q1, grad16).reshape(N_DEV, D_FF // N_DEV, D_MODEL)
    s_a = swap("w2_1", dw2_1)
    dp = _mm_nt("mlp1_dq", grad16, w2[1], D_FF, _ep_relu_sq_grad, [BF16], extras=(q1,), comm=s_a)[0]
    c_a = exchange(s_a, halves=True)
    dw1_1 = _mm_tn("mlp1_dw1", n3, dp, col_blocks=N_DEV, comm=c_a)
    c_a2, s_b = rest(c_a), swap("w1_1", dw1_1)
    dn = _mm_nt("mlp1_dn", dp, w1[1], d, _ep_store, [F32], comm=[c_a2, s_b])[0]
    done(c_a2)
    c_b = exchange(s_b, halves=True)
    grad, grad16, dg_mlp1 = _rms_bwd("mlp1_dn_norm", dn, h3, mlp_g[1], grad)
    d_od_out = _mm_tn("od_dw_out", y, grad16).reshape(N_DEV, D_MODEL // N_DEV, D_MODEL)
    s_c = swap("od_out", d_od_out)
    dy = _mm_nt("od_dy", grad16, w_od_out, C_DIM, _ep_store, [F32], comm=s_c)[0]
    c_c = exchange(s_c)
    dpre, d_ws, d_bsb, d_lvg, d_lvb, d_bin = _sgu_bwd(pre, dy, lvg, lvb, ws, bs_b, comm=[c_b, c_c])
    done(c_c)
    c_b2 = rest(c_b)
    d_od_in = _mm_tn("od_dw_in", n2, dpre, col_blocks=N_DEV, comm=c_b2)
    done(c_b2)
    s_d = swap("od_in", d_od_in)
    dn = _mm_nt("od_dn", dpre, w_od_in, d, _ep_store, [F32], comm=s_d)[0]
    c_d = exchange(s_d)
    grad, grad16, d_od_g = _rms_bwd("od_dn_norm", dn, h2, od_g, grad)
    dw2_0 = _mm_tn("mlp0_dw2", q0, grad16, comm=c_d).reshape(N_DEV, D_FF // N_DEV, D_MODEL)
    done(c_d)
    s_e = swap("w2_0", dw2_0)
    dp = _mm_nt("mlp0_dq", grad16, w2[0], D_FF, _ep_relu_sq_grad, [BF16], extras=(q0,), comm=s_e)[0]
    c_e = exchange(s_e, halves=True)
    dw1_0 = _mm_tn("mlp0_dw1", n1, dp, col_blocks=N_DEV, comm=c_e)
    c_e2, s_f = rest(c_e), swap("w1_0", dw1_0)
    dn = _mm_nt("mlp0_dn", dp, w1[0], d, _ep_store, [F32], comm=[c_e2, s_f])[0]
    done(c_e2)
    c_f = exchange(s_f)
    grad, grad16, dg_mlp0 = _rms_bwd("mlp0_dn_norm", dn, h1, mlp_g[0], grad)
    d_ev_out = _mm_tn("ev_dw_out", mix, grad16).reshape(N_DEV, D_MODEL // N_DEV, D_MODEL)
    s_g = swap("ev_out", d_ev_out)
    dmix = _mm_nt("ev_dmix", grad16, w_ev_out, A_DIM + B_DIM, _ep_store, [F32], comm=s_g)[0]
    c_g = exchange(s_g)
    da2, dcv, d_caw, d_cab, d_lag, d_lab, d_cbw = _mixer_bwd_local(z, dmix, seq, caw, cab, lag, lab, cbw, comm=[c_f, c_g])
    done(c_f)
    done(c_g)
    dz = _mixer_bwd_input(z, dmix, da2, dcv, seq, caw, cbw)
    small_parts = dict(
        ev_conv_a_w=d_caw, ev_conv_a_b=d_cab, ev_ln_a_g=d_lag, ev_ln_a_b=d_lab,
        ev_conv_b_w=d_cbw, od_norm_g=d_od_g, od_b_in=d_bin, od_ln_v_g=d_lvg, od_ln_v_b=d_lvb,
        od_w_s=d_ws.reshape(C_GROUPS * CHUNK, CHUNK), od_b_s=d_bsb[:, :, 0],
        mlp_norm_g=jnp.concatenate([dg_mlp0, dg_mlp1], axis=0), final_norm_g=d_fin_g)
    early = [n for n in SMALL if n != "ev_norm_g"]
    small_gather = _gather_comm([small_parts[n] for n in early] + [jnp.broadcast_to(loss_part, (SUBLANES, LANE_BLOCK))])
    d_ev_in = _mm_tn("ev_dw_in", n0, dz, comm=small_gather)
    d_ev_in = jnp.transpose(d_ev_in.reshape(D_MODEL, N_DEV, IN_EVEN // N_DEV), (1, 0, 2))
    s_h = swap("ev_in", d_ev_in)
    dn = _mm_nt("ev_dn", dz, w_ev_in, d, _ep_store, [F32], comm=s_h)[0]
    c_h = exchange(s_h)
    grad_x, _, d_ev_g = _rms_bwd("ev_dn_norm", dn, h0, ev_g, grad, comm=c_h)
    done(c_h)

    last_gather = _gather_comm([d_ev_g])
    shard = {"od_w_out": ("od_out",), "mlp_w1": ("w1_0", "w1_1"), "mlp_w2": ("w2_0", "w2_1"), "od_w_in": ("od_in",),
             "ev_w_out": ("ev_out",), "ev_w_in": ("ev_in",)}
    carried = {"od_w_out": last_gather}
    out_g, out_d, out_m, out_v = {}, {}, {}, {}
    for name, keys in shard.items():
        out_g[name], out_d[name], out_m[name], out_v[name] = _sum_adamw(
            f"adamw_{name}", [by_chip[key] for key in keys], W[name], M[name], V[name], comm=carried.get(name))

    gathered = dict(zip(early, small_gather.out[:-1]), ev_norm_g=last_gather.out[0])
    mine = []
    for n in SMALL:
        g = gathered[n]
        if n in SMALL_SHARDED:
            width = W[n].shape[-1]
            g = lax.dynamic_slice_in_dim(g, dev * width, width, axis=2)
        mine.append(g)
    res, loss_tile = _small_adamw("adamw_small", mine, [_rows(W[n]) for n in SMALL], [_rows(M[n]) for n in SMALL],
                                  [_rows(V[n]) for n in SMALL], small_gather.out[-1])
    loss = loss_tile[0, 0]
    for store, values in zip((out_g, out_d, out_m, out_v), res):
        for n, value in zip(SMALL, values):
            store[n] = value.reshape(W[n].shape)

    return (loss, grad_x.reshape(n_seq, seq, d), *[out_g[n] for n in ORDER], *[out_d[n] for n in ORDER],
            *[out_m[n] for n in ORDER], *[out_v[n] for n in ORDER])
```

```python
import math

import jax
import jax.numpy as jnp
from jax import lax
from jax.experimental import pallas as pl
from jax.experimental.pallas import tpu as pltpu

F32 = jnp.float32
BF16 = jnp.bfloat16
MESH = pl.DeviceIdType.MESH

D_MODEL = 1024
A_DIM = 512
B_DIM = 512
IN_EVEN = 2 * A_DIM + 3 * B_DIM
A_TAPS = 31
B_TAPS = 3
CHUNK = 128
C_GROUPS = 8
C_DIM = 1024
D_FF = 4096
RMS_EPS = 1e-6
LN_EPS = 1e-5
N_DEV = 8
N_CHIP = 4

ADAM_LR = 0.001
ADAM_B1 = 0.9
ADAM_B2 = 0.999
ADAM_EPS = 1e-08
ADAM_WD = 0.01
ADAM_STEP = 10

A_HALO = 32
B_HALO = 8
VMEM_LIMIT_BYTES = 56 * 1024 * 1024
TINY = 1.1754944e-38
INV_SQRT2 = 1.0 / math.sqrt(2.0)
INV_SQRT_2PI = 1.0 / math.sqrt(2.0 * math.pi)
HBM_SPEC = pl.BlockSpec(memory_space=pltpu.HBM)


def _tile(n, want):
    t = min(n, want)
    while n % t:
        t //= 2
    return t


def _sigmoid(x):
    return 1.0 / (1.0 + jnp.exp(-x))


def _gelu(x):
    return 0.5 * x * (1.0 + lax.erf(x * INV_SQRT2))


def _gelu_and_grad(x):
    cdf = 0.5 * (1.0 + lax.erf(x * INV_SQRT2))
    return x * cdf, cdf + x * jnp.exp(-0.5 * x * x) * INV_SQRT_2PI


def _colsum(x):
    return jnp.sum(x, axis=0, keepdims=True)


class _Comm:
    def __init__(self, ins, out_shapes, sem_shapes, start, finish, middle=None, into=None, sibling=False, chips=False):
        self.ins, self.out_shapes, self.sem_shapes, self.start, self.finish = ins, out_shapes, sem_shapes, start, finish
        self.middle = middle
        self.sibling, self.chips = sibling, chips
        self.into = list(into) if into is not None else []
        self.out = None


def _piece_rows(rows, piece):
    if piece is None:
        return 0, rows
    i, n = piece
    return i * (rows // n), rows // n


MIDDLE_AT = 0.75
BARRIER_IDS = {(True, True): 0, (True, False): 1, (False, True): 2}
FUSED_ROWS = 512


def _call(body, *, name, grid, in_specs, out_specs, out_shape, args, scratch_shapes=(), parallel=False, comm=None):
    comms = [] if comm is None else (list(comm) if isinstance(comm, (list, tuple)) else [comm])
    if not comms:
        sem = ("parallel" if parallel else "arbitrary",) * len(grid)
        return pl.pallas_call(
            body, name=name, grid=grid, in_specs=list(in_specs), out_specs=list(out_specs), out_shape=list(out_shape),
            scratch_shapes=list(scratch_shapes),
            compiler_params=pltpu.CompilerParams(dimension_semantics=sem, vmem_limit_bytes=VMEM_LIMIT_BYTES),
        )(*args)
    n_in, n_out, n_scr = len(in_specs), len(out_shape), len(scratch_shapes)
    c_ins_all = [a for cm in comms for a in cm.ins]
    c_into_all = [a for cm in comms for a in cm.into]
    c_out_shapes = [s for cm in comms for s in cm.out_shapes]
    c_sem_shapes = [s for cm in comms for s in cm.sem_shapes]
    aliases, in_pos, out_pos = {}, n_in + len(c_ins_all), n_out
    for cm in comms:
        for j in range(len(cm.into)):
            aliases[in_pos + j] = out_pos + j
        in_pos += len(cm.into)
        out_pos += len(cm.out_shapes)
    to_sibling = any(cm.sibling for cm in comms)
    to_chips = any(cm.chips for cm in comms)
    steps = grid
    total = math.prod(steps)
    first_step = (0,) * len(steps)
    last_step = tuple(s - 1 for s in steps)
    middle_step = None
    if 0 < int(MIDDLE_AT * total) < total - 1:
        rest, idx = int(MIDDLE_AT * total), []
        for s in reversed(steps):
            idx.append(rest % s)
            rest //= s
        middle_step = tuple(reversed(idx))

    def carrying(*refs):
        pos = 0
        ins = refs[pos:pos + n_in]; pos += n_in
        c_ins = refs[pos:pos + len(c_ins_all)]; pos += len(c_ins_all) + len(c_into_all)
        outs = refs[pos:pos + n_out]; pos += n_out
        c_outs = refs[pos:pos + len(c_out_shapes)]; pos += len(c_out_shapes)
        scr = refs[pos:pos + n_scr]; pos += n_scr
        c_sems = refs[pos:]
        views, i0, o0, s0 = [], 0, 0, 0
        for cm in comms:
            views.append((c_ins[i0:i0 + len(cm.ins)], c_outs[o0:o0 + len(cm.out_shapes)], c_sems[s0:s0 + len(cm.sem_shapes)]))
            i0, o0, s0 = i0 + len(cm.ins), o0 + len(cm.out_shapes), s0 + len(cm.sem_shapes)

        def at(step):
            hit = pl.program_id(0) == step[0]
            for axis in range(1, len(steps)):
                hit = jnp.logical_and(hit, pl.program_id(axis) == step[axis])
            return hit

        @pl.when(at(first_step))
        def _():
            x, y, c, chips = _place()
            peers = ([(x, y, 1 - c)] if to_sibling else []) + ([(*chip, c) for chip in chips] if to_chips else [])
            barrier = pltpu.get_barrier_semaphore()
            for peer in peers:
                pl.semaphore_signal(barrier, inc=1, device_id=peer, device_id_type=MESH)
            pl.semaphore_wait(barrier, len(peers))
            for cm, view in zip(comms, views):
                cm.start(*view)

        if middle_step is not None:
            @pl.when(at(middle_step))
            def _():
                for cm, view in zip(comms, views):
                    if cm.middle is not None:
                        cm.middle(*view)

        body(*ins, *outs, *scr)

        @pl.when(at(last_step))
        def _():
            for cm, view in zip(comms, views):
                if cm.middle is not None and middle_step is None:
                    cm.middle(*view)
            for cm, view in zip(comms, views):
                cm.finish(*view)

    res = pl.pallas_call(
        carrying, name=name, grid=grid,
        in_specs=[*in_specs, *[HBM_SPEC] * (len(c_ins_all) + len(c_into_all))],
        out_specs=[*out_specs, *[HBM_SPEC] * len(c_out_shapes)],
        out_shape=[*out_shape, *c_out_shapes], scratch_shapes=[*scratch_shapes, *c_sem_shapes],
        input_output_aliases=aliases,
        compiler_params=pltpu.CompilerParams(dimension_semantics=("arbitrary",) * len(grid), vmem_limit_bytes=VMEM_LIMIT_BYTES,
                                             collective_id=BARRIER_IDS[to_sibling, to_chips]),
    )(*args, *c_ins_all, *c_into_all)
    pos = n_out
    for cm in comms:
        cm.out = list(res[pos:pos + len(cm.out_shapes)])
        pos += len(cm.out_shapes)
    return list(res[:n_out])


def _place():
    x, y, c = lax.axis_index("x"), lax.axis_index("y"), lax.axis_index("c")
    return x, y, c, [(1 - x, y), (x, 1 - y), (1 - x, 1 - y)]


def _gather_comm(shards, piece=None, into=None):
    nw = len(shards)
    spans = [_piece_rows(s.shape[0], piece) for s in shards]

    def plan(ins, outs, sems):
        send_sems, recv_sems, local_sems = sems
        x, y, c, chips = _place()
        me, sibling = (x, y, c), (x, y, 1 - c)

        def slot(w, p):
            return outs[w].at[4 * p[0] + 2 * p[1] + p[2], pl.ds(*spans[w])]

        def mine(w):
            return ins[w].at[pl.ds(*spans[w])]

        def copy(w, k, block, to, src=None):
            return pltpu.make_async_remote_copy(
                src_ref=slot(w, block) if src is None else src, dst_ref=slot(w, block),
                send_sem=send_sems.at[w, k], recv_sem=recv_sems.at[w, k], device_id=to, device_id_type=MESH)

        local = [pltpu.make_async_copy(mine(w), slot(w, me), local_sems.at[w]) for w in range(nw)]
        first = [[copy(w, 0, me, sibling, src=mine(w))] + [copy(w, 1 + j, me, (*chip, c), src=mine(w)) for j, chip in enumerate(chips)]
                 for w in range(nw)]
        landed = [[copy(w, 1 + j, (*chip, c), me) for j, chip in enumerate(chips)] for w in range(nw)]
        passed = [[copy(w, 4 + j, (*chip, c), sibling) for j, chip in enumerate(chips)] for w in range(nw)]
        from_sibling = [[copy(w, 0, sibling, me)] + [copy(w, 4 + j, (*chip, 1 - c), me) for j, chip in enumerate(chips)]
                        for w in range(nw)]
        return local, first, landed, passed, from_sibling

    def start(ins, outs, sems):
        local, first, _, _, _ = plan(ins, outs, sems)
        for cp in local:
            cp.start()
        for row in first:
            for cp in row:
                cp.start()

    def middle(ins, outs, sems):
        _, _, landed, passed, _ = plan(ins, outs, sems)
        for w in range(nw):
            for j in range(3):
                landed[w][j].wait_recv()
                passed[w][j].start()

    def finish(ins, outs, sems):
        local, first, landed, passed, from_sibling = plan(ins, outs, sems)
        for w in range(nw):
            for cp in from_sibling[w]:
                cp.wait_recv()
        for w in range(nw):
            for cp in first[w] + passed[w]:
                cp.wait_send()
        for cp in local:
            cp.wait()

    return _Comm(list(shards), [jax.ShapeDtypeStruct((N_DEV, *s.shape), s.dtype) for s in shards],
                 [pltpu.SemaphoreType.DMA((nw, 7)), pltpu.SemaphoreType.DMA((nw, 7)), pltpu.SemaphoreType.DMA((nw,))],
                 start, finish, middle, into=into, sibling=True, chips=True)


def _pair_comm(parts):
    nw = len(parts)

    def plan(ins, outs, sems):
        send_sems, recv_sems = sems
        x, y, c, _ = _place()
        return [pltpu.make_async_remote_copy(
            src_ref=ins[w].at[2 * q + 1 - c], dst_ref=outs[w].at[q], send_sem=send_sems.at[w, q], recv_sem=recv_sems.at[w, q],
            device_id=(x, y, 1 - c), device_id_type=MESH) for w in range(nw) for q in range(N_CHIP)]

    def start(ins, outs, sems):
        for cp in plan(ins, outs, sems):
            cp.start()

    def finish(ins, outs, sems):
        for cp in plan(ins, outs, sems):
            cp.wait()

    return _Comm(list(parts), [jax.ShapeDtypeStruct((N_CHIP, *p.shape[1:]), p.dtype) for p in parts],
                 [pltpu.SemaphoreType.DMA((nw, N_CHIP)), pltpu.SemaphoreType.DMA((nw, N_CHIP))], start, finish, sibling=True)


def _chip_comm(sums, piece=None, into=None):
    nw = len(sums)
    spans = [_piece_rows(s.shape[1], piece) for s in sums]

    def plan(ins, outs, sems):
        send_sems, recv_sems, local_sems = sems
        x, y, c, chips = _place()
        my_chip = 2 * x + y
        local = [pltpu.make_async_copy(ins[w].at[my_chip, pl.ds(*spans[w])], outs[w].at[my_chip, pl.ds(*spans[w])],
                                       local_sems.at[w]) for w in range(nw)]
        remote = [pltpu.make_async_remote_copy(
            src_ref=ins[w].at[2 * chip[0] + chip[1], pl.ds(*spans[w])], dst_ref=outs[w].at[my_chip, pl.ds(*spans[w])],
            send_sem=send_sems.at[w, j], recv_sem=recv_sems.at[w, j], device_id=(*chip, c), device_id_type=MESH)
            for w in range(nw) for j, chip in enumerate(chips)]
        return local, remote

    def start(ins, outs, sems):
        local, remote = plan(ins, outs, sems)
        for cp in local + remote:
            cp.start()

    def finish(ins, outs, sems):
        local, remote = plan(ins, outs, sems)
        for cp in remote + local:
            cp.wait()

    return _Comm(list(sums), [jax.ShapeDtypeStruct(s.shape, s.dtype) for s in sums],
                 [pltpu.SemaphoreType.DMA((nw, 3)), pltpu.SemaphoreType.DMA((nw, 3)), pltpu.SemaphoreType.DMA((nw,))],
                 start, finish, into=into, chips=True)


def _matmul(name, a, b, *, kind, m, n, k, a_spec, b_spec, tm, tn, tk, out_shape, out_specs, epilogue,
            extras=(), extra_specs=(), comm=None):
    dims = {"nn": (((1,), (0,)), ((), ())), "nt": (((1,), (1,)), ((), ())), "tn": (((0,), (0,)), ((), ()))}[kind]
    nk = k // tk
    n_extra = len(extras)
    n_out = len(out_shape)

    def body(a_ref, b_ref, *rest):
        extra_refs = rest[:n_extra]
        out_refs = rest[n_extra:n_extra + n_out]
        part = lax.dot_general(a_ref[...], b_ref[...], dims, preferred_element_type=F32)
        if nk == 1:
            epilogue(part, extra_refs, out_refs)
            return
        acc_ref = rest[n_extra + n_out]
        step = pl.program_id(2)

        @pl.when(step == 0)
        def _():
            acc_ref[...] = part

        @pl.when(jnp.logical_and(step > 0, step < nk - 1))
        def _():
            acc_ref[...] += part

        @pl.when(step == nk - 1)
        def _():
            epilogue(acc_ref[...] + part, extra_refs, out_refs)

    return _call(
        body, name=name, grid=(m // tm, n // tn, nk), in_specs=[a_spec, b_spec, *extra_specs], out_specs=out_specs,
        out_shape=out_shape, scratch_shapes=[pltpu.VMEM((tm, tn), F32)] if nk > 1 else [], args=(a, b, *extras), comm=comm)


def _mm_operands(m, n, tm, tn, out_dtypes, extras, rows, sums):
    tile = pl.BlockSpec((tm, tn), lambda i, j, kk: (i, j))
    row = pl.BlockSpec((1, tn), lambda i, j, kk: (0, j))
    one = pl.BlockSpec((1, 1), lambda i, j, kk: (0, 0))
    out_shape = [jax.ShapeDtypeStruct((m, n), dt) for dt in out_dtypes]
    out_shape += [jax.ShapeDtypeStruct((1, n if wide else 1), F32) for wide in sums]
    out_specs = [tile] * len(out_dtypes) + [row if wide else one for wide in sums]
    return (*extras, *rows), [tile] * len(extras) + [row] * len(rows), out_shape, out_specs


def _row_tile(m, k, tm_want):
    return _tile(m, tm_want or (1024 if k <= 1024 else 512))


def _mm_nn(name, a, b, n, epilogue, out_dtypes, *, extras=(), rows=(), sums=(), tm_want=None, comm=None):
    m, k = a.shape
    blocked = b.ndim == 3
    tm = _row_tile(m, k, tm_want)
    tn = b.shape[-1] if blocked else _tile(n, 1024)
    tk = k
    extras, extra_specs, out_shape, out_specs = _mm_operands(m, n, tm, tn, out_dtypes, extras, rows, sums)
    if blocked:
        b_spec = pl.BlockSpec((None, tk, tn), lambda i, j, kk: (j, kk, 0))
    else:
        b_spec = pl.BlockSpec((tk, tn), lambda i, j, kk: (kk, j))
    return _matmul(
        name, a, b, kind="nn", m=m, n=n, k=k, tm=tm, tn=tn, tk=tk,
        a_spec=pl.BlockSpec((tm, tk), lambda i, j, kk: (i, kk)), b_spec=b_spec, out_shape=out_shape, out_specs=out_specs,
        epilogue=epilogue, extras=extras, extra_specs=extra_specs, comm=comm)


def _mm_nt(name, a, b, n, epilogue, out_dtypes, *, extras=(), rows=(), sums=(), tm_want=None, comm=None):
    m, k = a.shape
    blocked = b.ndim == 3
    tm = _row_tile(m, k, tm_want)
    tn = _tile(n, 1024)
    tk = b.shape[-1] if blocked else k
    extras, extra_specs, out_shape, out_specs = _mm_operands(m, n, tm, tn, out_dtypes, extras, rows, sums)
    if blocked:
        b_spec = pl.BlockSpec((None, tn, tk), lambda i, j, kk: (kk, j, 0))
    else:
        b_spec = pl.BlockSpec((tn, tk), lambda i, j, kk: (j, kk))
    return _matmul(
        name, a, b, kind="nt", m=m, n=n, k=k, tm=tm, tn=tn, tk=tk,
        a_spec=pl.BlockSpec((tm, tk), lambda i, j, kk: (i, kk)), b_spec=b_spec, out_shape=out_shape, out_specs=out_specs,
        epilogue=epilogue, extras=extras, extra_specs=extra_specs, comm=comm)


def _mm_tn(name, a, b, *, col_blocks=0, comm=None):
    t, k = a.shape
    n = b.shape[1]
    tm = _row_tile(k, t, None)
    tn = n // col_blocks if col_blocks else _tile(n, 1024)
    tk = t
    if col_blocks:
        out_shape = [jax.ShapeDtypeStruct((col_blocks, k, tn), F32)]
        out_specs = [pl.BlockSpec((None, tm, tn), lambda i, j, kk: (j, i, 0))]
    else:
        out_shape = [jax.ShapeDtypeStruct((k, n), F32)]
        out_specs = [pl.BlockSpec((tm, tn), lambda i, j, kk: (i, j))]

    def epilogue(acc, extra_refs, out_refs):
        out_refs[0][...] = acc

    return _matmul(
        name, a, b, kind="tn", m=k, n=n, k=t, tm=tm, tn=tn, tk=tk,
        a_spec=pl.BlockSpec((tk, tm), lambda i, j, kk: (kk, i)), b_spec=pl.BlockSpec((tk, tn), lambda i, j, kk: (kk, j)),
        out_shape=out_shape, out_specs=out_specs, epilogue=epilogue, comm=comm)[0]


def _ep_store(acc, extra_refs, out_refs):
    out_refs[0][...] = acc.astype(out_refs[0].dtype)


def _ep_bias(acc, extra_refs, out_refs):
    out_refs[0][...] = acc + extra_refs[0][...]


def _rms(x):
    r = lax.rsqrt(jnp.mean(x * x, axis=-1, keepdims=True) + RMS_EPS)
    return r, x * r


def _rms_grad(dn, r, xr, g):
    dy = dn * g
    return r * (dy - xr * jnp.mean(dy * xr, axis=-1, keepdims=True))


def _ep_residual_norm(acc, extra_refs, out_refs):
    h = extra_refs[0][...] + acc
    out_refs[0][...] = h
    _, xr = _rms(h)
    out_refs[1][...] = (xr * extra_refs[1][...]).astype(BF16)


def _ep_final_loss(acc, extra_refs, out_refs):
    @pl.when(pl.program_id(0) == 0)
    def _():
        out_refs[2][...] = jnp.zeros_like(out_refs[2])
        out_refs[3][...] = jnp.zeros_like(out_refs[3])

    h = extra_refs[0][...] + acc
    g = extra_refs[2][...]
    r, xr = _rms(h)
    err = xr * g - extra_refs[1][...]
    out_refs[3][...] += 0.5 * jnp.sum(jnp.mean(err * err, axis=-1, keepdims=True), axis=0, keepdims=True)
    dout = err * (1.0 / h.shape[-1])
    out_refs[2][...] += _colsum(dout * xr)
    out = _rms_grad(dout, r, xr, g)
    out_refs[0][...] = out
    out_refs[1][...] = out.astype(BF16)


def _ep_relu_sq(acc, extra_refs, out_refs):
    r = jnp.maximum(acc, 0.0)
    out_refs[0][...] = (r * r).astype(BF16)


def _ep_relu_sq_grad(acc, extra_refs, out_refs):
    q = extra_refs[0][...].astype(F32)
    out_refs[0][...] = (acc * (2.0 * q * lax.rsqrt(jnp.maximum(q, TINY)))).astype(BF16)


def _rms_fwd(name, h, g, comm=None):
    t, d = h.shape
    tt = _tile(t, 512)

    def body(h_ref, g_ref, n_ref):
        x = h_ref[...]
        r = lax.rsqrt(jnp.mean(x * x, axis=-1, keepdims=True) + RMS_EPS)
        n_ref[...] = (x * r * g_ref[...]).astype(BF16)

    return _call(
        body, name=name, grid=(t // tt,),
        in_specs=[pl.BlockSpec((tt, d), lambda i: (i, 0)), pl.BlockSpec((1, d), lambda i: (0, 0))],
        out_specs=[pl.BlockSpec((tt, d), lambda i: (i, 0))], out_shape=[jax.ShapeDtypeStruct((t, d), BF16)],
        args=(h, g), parallel=True, comm=comm)[0]


def _rms_bwd(name, dn, h, g, grad_in, comm=None):
    t, d = h.shape
    tt = _tile(t, 512)

    def body(dn_ref, h_ref, g_ref, gin_ref, gout_ref, gout16_ref, dg_ref):
        @pl.when(pl.program_id(0) == 0)
        def _():
            dg_ref[...] = jnp.zeros_like(dg_ref)

        dnv = dn_ref[...]
        r, xr = _rms(h_ref[...])
        dg_ref[...] += _colsum(dnv * xr)
        out = gin_ref[...] + _rms_grad(dnv, r, xr, g_ref[...])
        gout_ref[...] = out
        gout16_ref[...] = out.astype(BF16)

    row = pl.BlockSpec((tt, d), lambda i: (i, 0))
    vec = pl.BlockSpec((1, d), lambda i: (0, 0))
    return _call(
        body, name=name, grid=(t // tt,), in_specs=[row, row, vec, row], out_specs=[row, row, vec],
        out_shape=[jax.ShapeDtypeStruct((t, d), F32), jax.ShapeDtypeStruct((t, d), BF16), jax.ShapeDtypeStruct((1, d), F32)],
        args=(dn, h, g, grad_in), comm=comm)


def _mixer_windows(z_ref, zh_ref, first, a1_s, cb_s, tt):
    sig = _sigmoid(z_ref[:, A_DIM:2 * A_DIM])
    a1_s[A_HALO:A_HALO + tt, :] = z_ref[:, 0:A_DIM] * sig
    a1_h = zh_ref[:, 0:A_DIM] * _sigmoid(zh_ref[:, A_DIM:2 * A_DIM])
    a1_s[0:A_HALO, :] = jnp.where(first, 0.0, a1_h)
    cb_s[B_HALO:B_HALO + tt, :] = z_ref[:, 3 * A_DIM:4 * A_DIM] * z_ref[:, 4 * A_DIM:5 * A_DIM]
    cb_h = zh_ref[A_HALO - B_HALO:A_HALO, 3 * A_DIM:4 * A_DIM] * zh_ref[A_HALO - B_HALO:A_HALO, 4 * A_DIM:5 * A_DIM]
    cb_s[0:B_HALO, :] = jnp.where(first, 0.0, cb_h)
    return sig


def _causal_conv(win_s, w_ref, taps, halo, tt):
    base = halo - (taps - 1)
    acc = w_ref[0:1, :] * win_s[pl.ds(base, tt), :]
    for k in range(1, taps):
        acc = acc + w_ref[k:k + 1, :] * win_s[pl.ds(base + k, tt), :]
    return acc


SUBLANES = 8
LANE_BLOCK = 128
ROW_BLOCK = 128
SHIFT_ROWS = A_HALO - SUBLANES


def _shifted_copies(win_s, sh_s, tt):
    for b in range(1, SUBLANES):
        sh_s[b - 1] = win_s[pl.ds(b, tt + SHIFT_ROWS), :]


def _window_rows(win_s, sh_s, offset, rows, cols):
    b = offset % SUBLANES
    if b == 0:
        return win_s[pl.ds(offset, rows), cols]
    return sh_s[b - 1, pl.ds(offset - b, rows), cols]


def _blocks(tt):
    rb = min(tt, ROW_BLOCK)
    return rb, [(r, slice(lb * LANE_BLOCK, (lb + 1) * LANE_BLOCK))
                for lb in range(A_DIM // LANE_BLOCK) for r in range(0, tt, rb)]


def _conv_taps(win_s, sh_s, w_ref, offsets, out_s, tt, bias_ref=None):
    rb, blocks = _blocks(tt)
    for r, cols in blocks:
        acc = w_ref[0:1, cols] * _window_rows(win_s, sh_s, r + offsets[0], rb, cols)
        for k in range(1, len(offsets)):
            acc = acc + w_ref[k:k + 1, cols] * _window_rows(win_s, sh_s, r + offsets[k], rb, cols)
        out_s[r:r + rb, cols] = acc if bias_ref is None else acc + bias_ref[:, cols]


A_CAUSAL = [A_HALO - (A_TAPS - 1) + k for k in range(A_TAPS)]
A_ANTICAUSAL = [A_TAPS - 1 - k for k in range(A_TAPS)]


def _layer_norm_stats(x):
    mu = jnp.mean(x, axis=-1, keepdims=True)
    xc = x - mu
    rstd = lax.rsqrt(jnp.mean(xc * xc, axis=-1, keepdims=True) + LN_EPS)
    return xc * rstd, rstd


def _mixer_specs(seq, tt):
    tiles_per_seq = seq // tt
    halo_blocks = tt // A_HALO
    z_spec = pl.BlockSpec((tt, IN_EVEN), lambda i: (i, 0))
    zh_spec = pl.BlockSpec((A_HALO, IN_EVEN), lambda i: (jnp.maximum(i * halo_blocks - 1, 0), 0))
    return tiles_per_seq, z_spec, zh_spec


def _vec_spec(rows, cols):
    return pl.BlockSpec((rows, cols), lambda i: (0, 0))


def _mixer_fwd(z, seq, caw, cab, lag, lab, cbw, comm=None):
    t = z.shape[0]
    tt = _tile(seq, 256)
    tiles_per_seq, z_spec, zh_spec = _mixer_specs(seq, tt)

    def body(z_ref, zh_ref, caw_ref, cab_ref, lag_ref, lab_ref, cbw_ref, mix_ref, a2_ref, a1_s, cb_s, sh_s):
        first = (pl.program_id(0) % tiles_per_seq) == 0
        _mixer_windows(z_ref, zh_ref, first, a1_s, cb_s, tt)
        _shifted_copies(a1_s, sh_s, tt)
        _conv_taps(a1_s, sh_s, caw_ref, A_CAUSAL, a2_ref, tt, bias_ref=cab_ref)
        xhat, _ = _layer_norm_stats(a2_ref[...])
        a3 = xhat * lag_ref[...] + lab_ref[...]
        mix_ref[:, 0:A_DIM] = (a3 * _sigmoid(a3)).astype(BF16)
        cv = _causal_conv(cb_s, cbw_ref, B_TAPS, B_HALO, tt)
        mix_ref[:, A_DIM:A_DIM + B_DIM] = (z_ref[:, 2 * A_DIM:3 * A_DIM] * cv).astype(BF16)

    return _call(
        body, name="mixer_fwd", grid=(t // tt,),
        in_specs=[z_spec, zh_spec, _vec_spec(A_TAPS, A_DIM), _vec_spec(1, A_DIM), _vec_spec(1, A_DIM), _vec_spec(1, A_DIM),
                  _vec_spec(B_TAPS, B_DIM)],
        out_specs=[pl.BlockSpec((tt, A_DIM + B_DIM), lambda i: (i, 0)), pl.BlockSpec((tt, A_DIM), lambda i: (i, 0))],
        out_shape=[jax.ShapeDtypeStruct((t, A_DIM + B_DIM), BF16), jax.ShapeDtypeStruct((t, A_DIM), F32)],
        scratch_shapes=[pltpu.VMEM((A_HALO + tt, A_DIM), F32), pltpu.VMEM((B_HALO + tt, B_DIM), F32),
                        pltpu.VMEM((SUBLANES - 1, tt + SHIFT_ROWS, A_DIM), F32)],
        args=(z, z, caw, cab, lag, lab, cbw), parallel=True, comm=comm)


def _mixer_bwd_local(z, a2, dmix, seq, lag, lab, comm=None):
    t = z.shape[0]
    tt = _tile(seq, 256)
    tiles_per_seq, z_spec, zh_spec = _mixer_specs(seq, tt)

    def body(z_ref, zh_ref, a2_ref, dmix_ref, lag_ref, lab_ref,
             da2_ref, dcv_ref, dcaw_ref, dcab_ref, dlag_ref, dlab_ref, dcbw_ref, a1_s, cb_s, sh_s):
        @pl.when(pl.program_id(0) == 0)
        def _():
            for ref in (dcaw_ref, dcab_ref, dlag_ref, dlab_ref, dcbw_ref):
                ref[...] = jnp.zeros_like(ref)

        first = (pl.program_id(0) % tiles_per_seq) == 0
        _mixer_windows(z_ref, zh_ref, first, a1_s, cb_s, tt)
        _shifted_copies(a1_s, sh_s, tt)
        xhat, rstd = _layer_norm_stats(a2_ref[...])
        a3 = xhat * lag_ref[...] + lab_ref[...]
        s3 = _sigmoid(a3)
        da3 = dmix_ref[:, 0:A_DIM] * (s3 * (1.0 + a3 * (1.0 - s3)))
        dlag_ref[...] += _colsum(da3 * xhat)
        dlab_ref[...] += _colsum(da3)
        dxh = da3 * lag_ref[...]
        da2 = rstd * (dxh - jnp.mean(dxh, axis=-1, keepdims=True) - xhat * jnp.mean(dxh * xhat, axis=-1, keepdims=True))
        da2_ref[...] = da2
        dcab_ref[...] += _colsum(da2)
        rb, blocks = _blocks(tt)
        for r, cols in blocks:
            da2_b = da2_ref[r:r + rb, cols]
            for k in range(A_TAPS):
                dcaw_ref[k:k + 1, cols] += _colsum(da2_b * _window_rows(a1_s, sh_s, r + A_CAUSAL[k], rb, cols))
        dcv =dmix_ref[:, A_DIM:A_DIM + B_DIM] * z_ref[:, 2 * A_DIM:3 * A_DIM]
        dcv_ref[...] = dcv
        for k in range(B_TAPS):
            dcbw_ref[k:k + 1, :] += _colsum(dcv * cb_s[pl.ds(B_HALO - (B_TAPS - 1) + k, tt), :])

    half = pl.BlockSpec((tt, A_DIM), lambda i: (i, 0))
    return _call(
        body, name="mixer_bwd_local", grid=(t // tt,),
        in_specs=[z_spec, zh_spec, half, pl.BlockSpec((tt, A_DIM + B_DIM), lambda i: (i, 0)),
                  _vec_spec(1, A_DIM), _vec_spec(1, A_DIM)],
        out_specs=[half, half, _vec_spec(A_TAPS, A_DIM), _vec_spec(1, A_DIM), _vec_spec(1, A_DIM), _vec_spec(1, A_DIM),
                   _vec_spec(B_TAPS, B_DIM)],
        out_shape=[jax.ShapeDtypeStruct((t, A_DIM), F32), jax.ShapeDtypeStruct((t, B_DIM), F32),
                   jax.ShapeDtypeStruct((A_TAPS, A_DIM), F32), jax.ShapeDtypeStruct((1, A_DIM), F32),
                   jax.ShapeDtypeStruct((1, A_DIM), F32), jax.ShapeDtypeStruct((1, A_DIM), F32),
                   jax.ShapeDtypeStruct((B_TAPS, B_DIM), F32)],
        scratch_shapes=[pltpu.VMEM((A_HALO + tt, A_DIM), F32), pltpu.VMEM((B_HALO + tt, B_DIM), F32),
                        pltpu.VMEM((SUBLANES - 1, tt + SHIFT_ROWS, A_DIM), F32)],
        args=(z, z, a2, dmix, lag, lab), comm=comm)


def _mixer_bwd_input(z, dmix, da2, dcv, seq, caw, cbw, comm=None):
    t = z.shape[0]
    tt = _tile(seq, 256)
    tiles_per_seq, z_spec, zh_spec = _mixer_specs(seq, tt)
    a_blocks = tt // A_HALO
    b_blocks = tt // B_HALO
    last_a = t // A_HALO - 1
    last_b = t // B_HALO - 1

    def body(z_ref, zh_ref, dmix_ref, da2_ref, da2n_ref, dcv_ref, dcvn_ref, caw_ref, cbw_ref, dz_ref, a1_s, cb_s, da2_s, dcv_s,
             sh_s, da1_s):
        pos = pl.program_id(0) % tiles_per_seq
        first = pos == 0
        last = pos == tiles_per_seq - 1
        sig = _mixer_windows(z_ref, zh_ref, first, a1_s, cb_s, tt)
        da2_s[0:tt, :] = da2_ref[...]
        da2_s[tt:tt + A_HALO, :] = jnp.where(last, 0.0, da2n_ref[...])
        dcv_s[0:tt, :] = dcv_ref[...]
        dcv_s[tt:tt + B_HALO, :] = jnp.where(last, 0.0, dcvn_ref[...])
        _shifted_copies(da2_s, sh_s, tt)
        _conv_taps(da2_s, sh_s, caw_ref, A_ANTICAUSAL, da1_s, tt)
        da1 = da1_s[...]
        dz_ref[:, 0:A_DIM] = (da1 * sig).astype(BF16)
        dz_ref[:, A_DIM:2 * A_DIM] = (da1 * z_ref[:, 0:A_DIM] * sig * (1.0 - sig)).astype(BF16)
        cv = _causal_conv(cb_s, cbw_ref, B_TAPS, B_HALO, tt)
        dz_ref[:, 2 * A_DIM:3 * A_DIM] = (dmix_ref[:, A_DIM:A_DIM + B_DIM] * cv).astype(BF16)
        dcb = cbw_ref[0:1, :] * dcv_s[pl.ds(B_TAPS - 1, tt), :]
        for k in range(1, B_TAPS):
            dcb = dcb + cbw_ref[k:k + 1, :] * dcv_s[pl.ds(B_TAPS - 1 - k, tt), :]
        dz_ref[:, 3 * A_DIM:4 * A_DIM] = (dcb * z_ref[:, 4 * A_DIM:5 * A_DIM]).astype(BF16)
        dz_ref[:, 4 * A_DIM:5 * A_DIM] = (dcb * z_ref[:, 3 * A_DIM:4 * A_DIM]).astype(BF16)

    half = pl.BlockSpec((tt, A_DIM), lambda i: (i, 0))
    return _call(
        body, name="mixer_bwd_input", grid=(t // tt,),
        in_specs=[z_spec, zh_spec, pl.BlockSpec((tt, A_DIM + B_DIM), lambda i: (i, 0)),
                  half, pl.BlockSpec((A_HALO, A_DIM), lambda i: (jnp.minimum((i + 1) * a_blocks, last_a), 0)),
                  half, pl.BlockSpec((B_HALO, B_DIM), lambda i: (jnp.minimum((i + 1) * b_blocks, last_b), 0)),
                  _vec_spec(A_TAPS, A_DIM), _vec_spec(B_TAPS, B_DIM)],
        out_specs=[pl.BlockSpec((tt, IN_EVEN), lambda i: (i, 0))],
        out_shape=[jax.ShapeDtypeStruct((t, IN_EVEN), BF16)],
        scratch_shapes=[pltpu.VMEM((A_HALO + tt, A_DIM), F32), pltpu.VMEM((B_HALO + tt, B_DIM), F32),
                        pltpu.VMEM((tt + A_HALO, A_DIM), F32), pltpu.VMEM((tt + B_HALO, B_DIM), F32),
                        pltpu.VMEM((SUBLANES - 1, tt + SHIFT_ROWS, A_DIM), F32), pltpu.VMEM((tt, A_DIM), F32)],
        args=(z, z, dmix, da2, da2, dcv, dcv, caw, cbw), parallel=True, comm=comm)[0]


def _tril_ws(ws_ref, g):
    rows = lax.broadcasted_iota(jnp.int32, (CHUNK, CHUNK), 0)
    cols = lax.broadcasted_iota(jnp.int32, (CHUNK, CHUNK), 1)
    return jnp.where(rows >= cols, ws_ref[g], 0.0).astype(BF16), rows >= cols


def _sgu_fwd(pre, lvg, lvb, ws, bs_b, comm=None):
    t = pre.shape[0]
    tt = _tile(t, 256)

    def body(pre_ref, lvg_ref, lvb_ref, ws_ref, bsb_ref, y_ref):
        vhat, _ = _layer_norm_stats(_gelu(pre_ref[:, C_DIM:2 * C_DIM]))
        vl = (vhat * lvg_ref[...] + lvb_ref[...]).astype(BF16)
        for g in range(C_GROUPS):
            w, _ = _tril_ws(ws_ref, g)
            cols = slice(g * CHUNK, (g + 1) * CHUNK)
            for ci in range(tt // CHUNK):
                rows = slice(ci * CHUNK, (ci + 1) * CHUNK)
                sv = jnp.dot(w, vl[rows, cols], preferred_element_type=F32) + bsb_ref[g]
                y_ref[rows, cols] = (_gelu(pre_ref[rows, cols]) * sv).astype(BF16)

    group = pl.BlockSpec((C_GROUPS, CHUNK, CHUNK), lambda i: (0, 0, 0))
    return _call(
        body, name="sgu_fwd", grid=(t // tt,),
        in_specs=[pl.BlockSpec((tt, 2 * C_DIM), lambda i: (i, 0)), _vec_spec(1, C_DIM), _vec_spec(1, C_DIM), group, group],
        out_specs=[pl.BlockSpec((tt, C_DIM), lambda i: (i, 0))], out_shape=[jax.ShapeDtypeStruct((t, C_DIM), BF16)],
        args=(pre, lvg, lvb, ws, bs_b), parallel=True, comm=comm)[0]


def _sgu_bwd(pre, dy, lvg, lvb, ws, bs_b, comm=None):
    t = pre.shape[0]
    tt = _tile(t, 256)

    def body(pre_ref, dy_ref, lvg_ref, lvb_ref, ws_ref, bsb_ref, dpre_ref, dws_ref, dbsb_ref, dlvg_ref, dlvb_ref, dbin_ref,
             dvl_s):
        @pl.when(pl.program_id(0) == 0)
        def _():
            for ref in (dws_ref, dbsb_ref, dlvg_ref, dlvb_ref, dbin_ref):
                ref[...] = jnp.zeros_like(ref)

        v, v_grad = _gelu_and_grad(pre_ref[:, C_DIM:2 * C_DIM])
        vhat, rstd = _layer_norm_stats(v)
        vl = (vhat * lvg_ref[...] + lvb_ref[...]).astype(BF16)
        for g in range(C_GROUPS):
            w, keep = _tril_ws(ws_ref, g)
            cols = slice(g * CHUNK, (g + 1) * CHUNK)
            dws = jnp.zeros((CHUNK, CHUNK), F32)
            dbs = jnp.zeros((CHUNK, 1), F32)
            for ci in range(tt // CHUNK):
                rows = slice(ci * CHUNK, (ci + 1) * CHUNK)
                vl_g = vl[rows, cols]
                sv = jnp.dot(w, vl_g, preferred_element_type=F32) + bsb_ref[g]
                u, u_grad = _gelu_and_grad(pre_ref[rows, cols])
                dyv = dy_ref[rows, cols]
                du = dyv * sv * u_grad
                dpre_ref[rows, cols] = du.astype(BF16)
                dbin_ref[:, cols] += _colsum(du)
                dsv = dyv * u
                dbs = dbs + jnp.sum(dsv, axis=1, keepdims=True)
                dsv16 = dsv.astype(BF16)
                dws = dws + lax.dot_general(dsv16, vl_g, (((1,), (1,)), ((), ())), preferred_element_type=F32)
                dvl_s[rows, cols] = lax.dot_general(w, dsv16, (((0,), (0,)), ((), ())), preferred_element_type=F32)
            dws_ref[g] += jnp.where(keep, dws, 0.0)
            dbsb_ref[g] += dbs
        dvl = dvl_s[...]
        dlvg_ref[...] += _colsum(dvl * vhat)
        dlvb_ref[...] += _colsum(dvl)
        dxh = dvl * lvg_ref[...]
        dv = rstd * (dxh - jnp.mean(dxh, axis=-1, keepdims=True) - vhat * jnp.mean(dxh * vhat, axis=-1, keepdims=True))
        dpv = dv * v_grad
        dpre_ref[:, C_DIM:2 * C_DIM] = dpv.astype(BF16)
        dbin_ref[:, C_DIM:2 * C_DIM] += _colsum(dpv)

    group = pl.BlockSpec((C_GROUPS, CHUNK, CHUNK), lambda i: (0, 0, 0))
    return _call(
        body, name="sgu_bwd", grid=(t // tt,),
        in_specs=[pl.BlockSpec((tt, 2 * C_DIM), lambda i: (i, 0)), pl.BlockSpec((tt, C_DIM), lambda i: (i, 0)),
                  _vec_spec(1, C_DIM), _vec_spec(1, C_DIM), group, group],
        out_specs=[pl.BlockSpec((tt, 2 * C_DIM), lambda i: (i, 0)), group, group,
                   _vec_spec(1, C_DIM), _vec_spec(1, C_DIM), _vec_spec(1, 2 * C_DIM)],
        out_shape=[jax.ShapeDtypeStruct((t, 2 * C_DIM), BF16), jax.ShapeDtypeStruct((C_GROUPS, CHUNK, CHUNK), F32),
                   jax.ShapeDtypeStruct((C_GROUPS, CHUNK, CHUNK), F32), jax.ShapeDtypeStruct((1, C_DIM), F32),
                   jax.ShapeDtypeStruct((1, C_DIM), F32), jax.ShapeDtypeStruct((1, 2 * C_DIM), F32)],
        scratch_shapes=[pltpu.VMEM((tt, C_DIM), F32)],
        args=(pre, dy, lvg, lvb, ws, bs_b), comm=comm)


def _pair_sum(name, part, got, core):
    _, k, n = part.shape
    tk = _tile(k, 1024)

    def body(core_ref, p_ref, s_ref, o_ref):
        o_ref[...] = (p_ref[...] + s_ref[...]).astype(BF16)

    return pl.pallas_call(
        body, name=name,
        grid_spec=pltpu.PrefetchScalarGridSpec(
            num_scalar_prefetch=1, grid=(N_CHIP, k // tk),
            in_specs=[pl.BlockSpec((None, tk, n), lambda q, i, core_ref: (2 * q + core_ref[0], i, 0)),
                      pl.BlockSpec((None, tk, n), lambda q, i, core_ref: (q, i, 0))],
            out_specs=pl.BlockSpec((None, tk, n), lambda q, i, core_ref: (q, i, 0))),
        out_shape=jax.ShapeDtypeStruct((N_CHIP, k, n), BF16),
        compiler_params=pltpu.CompilerParams(dimension_semantics=("parallel", "parallel"), vmem_limit_bytes=VMEM_LIMIT_BYTES),
    )(core, part, got)


def _adamw_math(w, g, m, v):
    m = ADAM_B1 * m + (1.0 - ADAM_B1) * g
    v = ADAM_B2 * v + (1.0 - ADAM_B2) * (g * g)
    m_hat = m / (1.0 - ADAM_B1 ** ADAM_STEP)
    v_hat = v / (1.0 - ADAM_B2 ** ADAM_STEP)
    delta = -ADAM_LR * (m_hat / (jnp.sqrt(v_hat) + ADAM_EPS) + ADAM_WD * w)
    return delta, m, v


def _sum_adamw(name, parts, w, m, v, comm=None):
    layers = len(parts)
    n_parts, k, n = parts[0].shape
    tk = _tile(k, 256)

    def body(*refs):
        p_refs = refs[:layers]
        w_ref, m_ref, v_ref, g_ref, d_ref, nm_ref, nv_ref = refs[layers:]

        def total(p_ref):
            g = p_ref[0].astype(F32)
            for q in range(1, n_parts):
                g = g + p_ref[q].astype(F32)
            return g

        g = total(p_refs[0])
        for l in range(1, layers):
            g = jnp.where(pl.program_id(0) == l, total(p_refs[l]), g)
        g_ref[...] = g
        d_ref[...], nm_ref[...], nv_ref[...] = _adamw_math(w_ref[...], g, m_ref[...], v_ref[...])

    blk = pl.BlockSpec((None, tk, n), lambda l, i: (l, i, 0))
    return _call(
        body, name=name, grid=(layers, k // tk),
        in_specs=[pl.BlockSpec((n_parts, tk, n), lambda l, i: (0, i, 0))] * layers + [blk, blk, blk], out_specs=[blk] * 4,
        out_shape=[jax.ShapeDtypeStruct((layers, k, n), F32)] * 4, args=(*parts, w, m, v), parallel=True, comm=comm)


def _small_adamw(name, parts, w, m, v, losses):
    count = len(parts)

    def in_order(ref):
        total = ref[0]
        for dev in range(1, N_DEV):
            total = total + ref[dev]
        return total

    def body(*refs):
        p_refs, w_refs, m_refs, v_refs = (refs[j * count:(j + 1) * count] for j in range(4))
        losses_ref = refs[4 * count]
        g_refs, d_refs, nm_refs, nv_refs = (refs[4 * count + 1 + j * count:4 * count + 1 + (j + 1) * count] for j in range(4))
        loss_ref = refs[8 * count + 1]
        for i in range(count):
            g = in_order(p_refs[i])
            g_refs[i][...] = g
            d_refs[i][...], nm_refs[i][...], nv_refs[i][...] = _adamw_math(w_refs[i][...], g, m_refs[i][...], v_refs[i][...])
        loss_ref[...] = in_order(losses_ref)

    res = pl.pallas_call(
        body, name=name,
        out_shape=[jax.ShapeDtypeStruct(a.shape, F32) for a in w] * 4 + [jax.ShapeDtypeStruct(losses.shape[1:], F32)],
        compiler_params=pltpu.CompilerParams(vmem_limit_bytes=VMEM_LIMIT_BYTES))(*parts, *w, *m, *v, losses)
    return [res[j * count:(j + 1) * count] for j in range(4)], res[4 * count]


def _rows(a):
    return a.reshape(-1, a.shape[-1])


def _whole(gathered):
    return jnp.transpose(gathered, (1, 0, 2)).reshape(gathered.shape[1], -1)


SMALL =("ev_norm_g", "ev_conv_a_w", "ev_conv_a_b", "ev_ln_a_g", "ev_ln_a_b", "ev_conv_b_w", "od_norm_g", "od_b_in",
         "od_ln_v_g", "od_ln_v_b", "od_w_s", "od_b_s", "mlp_norm_g", "final_norm_g")
SMALL_SHARDED = ("ev_conv_a_w", "ev_conv_b_w", "od_norm_g", "od_b_in", "od_ln_v_g", "od_ln_v_b")
ORDER = ("ev_norm_g", "ev_w_in", "ev_conv_a_w", "ev_conv_a_b", "ev_ln_a_g", "ev_ln_a_b", "ev_conv_b_w", "ev_w_out",
         "od_norm_g", "od_w_in", "od_b_in", "od_ln_v_g", "od_ln_v_b", "od_w_s", "od_b_s", "od_w_out", "mlp_norm_g",
         "mlp_w1", "mlp_w2", "final_norm_g")


def kernel(x, ev_norm_g, ev_w_in, ev_conv_a_w, ev_conv_a_b, ev_ln_a_g, ev_ln_a_b, ev_conv_b_w, ev_w_out, od_norm_g, od_w_in, od_b_in, od_ln_v_g, od_ln_v_b, od_w_s, od_b_s, od_w_out, mlp_norm_g, mlp_w1, mlp_w2, final_norm_g, loss_target, m_ev_norm_g, m_ev_w_in, m_ev_conv_a_w, m_ev_conv_a_b, m_ev_ln_a_g, m_ev_ln_a_b, m_ev_conv_b_w, m_ev_w_out, m_od_norm_g, m_od_w_in, m_od_b_in, m_od_ln_v_g, m_od_ln_v_b, m_od_w_s, m_od_b_s, m_od_w_out, m_mlp_norm_g, m_mlp_w1, m_mlp_w2, m_final_norm_g, v_ev_norm_g, v_ev_w_in, v_ev_conv_a_w, v_ev_conv_a_b, v_ev_ln_a_g, v_ev_ln_a_b, v_ev_conv_b_w, v_ev_w_out, v_od_norm_g, v_od_w_in, v_od_b_in, v_od_ln_v_g, v_od_ln_v_b, v_od_w_s, v_od_b_s, v_od_w_out, v_mlp_norm_g, v_mlp_w1, v_mlp_w2, v_final_norm_g):
    W = dict(ev_norm_g=ev_norm_g, ev_w_in=ev_w_in, ev_conv_a_w=ev_conv_a_w, ev_conv_a_b=ev_conv_a_b, ev_ln_a_g=ev_ln_a_g,
             ev_ln_a_b=ev_ln_a_b, ev_conv_b_w=ev_conv_b_w, ev_w_out=ev_w_out, od_norm_g=od_norm_g, od_w_in=od_w_in,
             od_b_in=od_b_in, od_ln_v_g=od_ln_v_g, od_ln_v_b=od_ln_v_b, od_w_s=od_w_s, od_b_s=od_b_s, od_w_out=od_w_out,
             mlp_norm_g=mlp_norm_g, mlp_w1=mlp_w1, mlp_w2=mlp_w2, final_norm_g=final_norm_g)
    M = dict(ev_norm_g=m_ev_norm_g, ev_w_in=m_ev_w_in, ev_conv_a_w=m_ev_conv_a_w, ev_conv_a_b=m_ev_conv_a_b,
             ev_ln_a_g=m_ev_ln_a_g, ev_ln_a_b=m_ev_ln_a_b, ev_conv_b_w=m_ev_conv_b_w, ev_w_out=m_ev_w_out,
             od_norm_g=m_od_norm_g, od_w_in=m_od_w_in, od_b_in=m_od_b_in, od_ln_v_g=m_od_ln_v_g, od_ln_v_b=m_od_ln_v_b,
             od_w_s=m_od_w_s, od_b_s=m_od_b_s, od_w_out=m_od_w_out, mlp_norm_g=m_mlp_norm_g, mlp_w1=m_mlp_w1,
             mlp_w2=m_mlp_w2, final_norm_g=m_final_norm_g)
    V = dict(ev_norm_g=v_ev_norm_g, ev_w_in=v_ev_w_in, ev_conv_a_w=v_ev_conv_a_w, ev_conv_a_b=v_ev_conv_a_b,
             ev_ln_a_g=v_ev_ln_a_g, ev_ln_a_b=v_ev_ln_a_b, ev_conv_b_w=v_ev_conv_b_w, ev_w_out=v_ev_w_out,
             od_norm_g=v_od_norm_g, od_w_in=v_od_w_in, od_b_in=v_od_b_in, od_ln_v_g=v_od_ln_v_g, od_ln_v_b=v_od_ln_v_b,
             od_w_s=v_od_w_s, od_b_s=v_od_b_s, od_w_out=v_od_w_out, mlp_norm_g=v_mlp_norm_g, mlp_w1=v_mlp_w1,
             mlp_w2=v_mlp_w2, final_norm_g=v_final_norm_g)

    n_seq, seq, d = x.shape
    t = n_seq * seq
    dev = 4 * lax.axis_index("x") + 2 * lax.axis_index("y") + lax.axis_index("c")
    core = lax.axis_index("c").astype(jnp.int32).reshape(1)

    ev_g, cab, lag, lab = W["ev_norm_g"], W["ev_conv_a_b"], W["ev_ln_a_g"], W["ev_ln_a_b"]
    ws = W["od_w_s"][0]
    bs_b = jnp.broadcast_to(W["od_b_s"][0][:, :, None], (C_GROUPS, CHUNK, CHUNK))
    mlp_g = [W["mlp_norm_g"][l:l + 1] for l in range(2)]
    fin_g = W["final_norm_g"].reshape(1, d)

    def w16(name, l=0):
        return W[name][l].astype(BF16)

    h0 = x.reshape(t, d)
    gather = _gather_comm([w16("ev_w_in")] + [_rows(W[n]) for n in SMALL_SHARDED])
    n0 = _rms_fwd("ev_norm", h0, ev_g, comm=gather)
    w_ev_in, caw, cbw, od_g, od_bin, lvg, lvb = [_whole(g) for g in gather.out]

    first, second = (0, 2), (1, 2)
    w1_0, w2_0, w1_1, w2_1 = w16("mlp_w1", 0), w16("mlp_w2", 0), w16("mlp_w1", 1), w16("mlp_w2", 1)

    g_a, g_b = _gather_comm([w16("ev_w_out")]), _gather_comm([w1_0], first)
    z = _mm_nn("ev_in", n0, w_ev_in, IN_EVEN, _ep_store, [F32], comm=[g_a, g_b])[0]
    w_ev_out = g_a.out[0].reshape(D_MODEL, D_MODEL)

    g_c, g_d = _gather_comm([w1_0], second, into=g_b.out), _gather_comm([w2_0], first)
    mix, a2 = _mixer_fwd(z, seq, caw, cab, lag, lab, cbw, comm=[g_c, g_d])
    w1 = [_whole(g_c.out[0]), None]

    g_e = _gather_comm([w2_0], second, into=g_d.out)
    h1, n1 = _mm_nn("ev_out", mix, w_ev_out, d, _ep_residual_norm, [F32, BF16], extras=(h0,), rows=(mlp_g[0],), comm=g_e)
    w2 = [g_e.out[0].reshape(D_FF, D_MODEL), None]

    g_f = _gather_comm([w16("od_w_in"), w16("od_w_out")])
    q0 = _mm_nn("mlp0_up", n1, w1[0], D_FF, _ep_relu_sq, [BF16], comm=g_f)[0]
    w_od_in = _whole(g_f.out[0])
    w_od_out = g_f.out[1].reshape(D_MODEL, D_MODEL)

    g_g = _gather_comm([w1_1], first)
    h2, n2 = _mm_nn("mlp0_down", q0, w2[0], d, _ep_residual_norm, [F32, BF16], extras=(h1,), rows=(od_g,), comm=g_g)
    g_h = _gather_comm([w1_1], second, into=g_g.out)
    pre = _mm_nn("od_in", n2, w_od_in, 2 * C_DIM, _ep_bias, [F32], rows=(od_bin,), comm=g_h)[0]
    w1[1] = _whole(g_h.out[0])
    g_i = _gather_comm([w2_1], first)
    y = _sgu_fwd(pre, lvg, lvb, ws, bs_b, comm=g_i)
    h3, n3 = _mm_nn("od_out", y, w_od_out, d, _ep_residual_norm, [F32, BF16], extras=(h2,), rows=(mlp_g[1],))
    g_j = _gather_comm([w2_1], second, into=g_i.out)
    q1 = _mm_nn("mlp1_up", n3, w1[1], D_FF, _ep_relu_sq, [BF16], comm=g_j)[0]
    w2[1] = g_j.out[0].reshape(D_FF, D_MODEL)
    grad, grad16, d_fin_g, loss_part = _mm_nn(
        "mlp1_down", q1, w2[1], d, _ep_final_loss, [F32, BF16], extras=(h3, loss_target.reshape(t, d)), rows=(fin_g,),
        sums=(True, False), tm_want=FUSED_ROWS)

    by_chip = {}

    def swap(name, parts):
        comm = _pair_comm([parts])
        comm.parts, comm.weight = parts, name
        return comm

    def exchange(swapped, halves=False):
        sums = _pair_sum(f"pair_sum_{swapped.weight}", swapped.parts, swapped.out[0], core)
        if not halves:
            comm = _chip_comm([sums])
            comm.weight = swapped.weight
            return comm
        comm = _chip_comm([sums], first)
        comm.sums, comm.weight = sums, swapped.weight
        return comm

    def rest(comm):
        other = _chip_comm([comm.sums], second, into=comm.out)
        other.weight = comm.weight
        return other

    def done(comm):
        by_chip[comm.weight] = comm.out[0]

    dw2_1 = _mm_tn("mlp1_dw2", q1, grad16).reshape(N_DEV, D_FF // N_DEV, D_MODEL)
    s_a = swap("w2_1", dw2_1)
    dp = _mm_nt("mlp1_dq", grad16, w2[1], D_FF, _ep_relu_sq_grad, [BF16], extras=(q1,), comm=s_a)[0]
    c_a = exchange(s_a, halves=True)
    dw1_1 = _mm_tn("mlp1_dw1", n3, dp, col_blocks=N_DEV, comm=c_a)
    c_a2, s_b = rest(c_a), swap("w1_1", dw1_1)
    dn = _mm_nt("mlp1_dn", dp, w1[1], d, _ep_store, [F32], comm=[c_a2, s_b])[0]
    done(c_a2)
    c_b = exchange(s_b, halves=True)
    grad, grad16, dg_mlp1 = _rms_bwd("mlp1_dn_norm", dn, h3, mlp_g[1], grad)
    d_od_out = _mm_tn("od_dw_out", y, grad16).reshape(N_DEV, D_MODEL // N_DEV, D_MODEL)
    s_c = swap("od_out", d_od_out)
    dy = _mm_nt("od_dy", grad16, w_od_out, C_DIM, _ep_store, [F32], comm=s_c)[0]
    c_c = exchange(s_c)
    dpre, d_ws, d_bsb, d_lvg, d_lvb, d_bin = _sgu_bwd(pre, dy, lvg, lvb, ws, bs_b, comm=[c_b, c_c])
    done(c_c)
    c_b2 = rest(c_b)
    d_od_in = _mm_tn("od_dw_in", n2, dpre, col_blocks=N_DEV, comm=c_b2)
    done(c_b2)
    s_d = swap("od_in", d_od_in)
    dn = _mm_nt("od_dn", dpre, w_od_in, d, _ep_store, [F32], comm=s_d)[0]
    c_d = exchange(s_d)
    grad, grad16, d_od_g = _rms_bwd("od_dn_norm", dn, h2, od_g, grad)
    dw2_0 = _mm_tn("mlp0_dw2", q0, grad16, comm=c_d).reshape(N_DEV, D_FF // N_DEV, D_MODEL)
    done(c_d)
    s_e = swap("w2_0", dw2_0)
    dp = _mm_nt("mlp0_dq", grad16, w2[0], D_FF, _ep_relu_sq_grad, [BF16], extras=(q0,), comm=s_e)[0]
    c_e = exchange(s_e, halves=True)
    dw1_0 = _mm_tn("mlp0_dw1", n1, dp, col_blocks=N_DEV, comm=c_e)
    c_e2, s_f = rest(c_e), swap("w1_0", dw1_0)
    dn = _mm_nt("mlp0_dn", dp, w1[0], d, _ep_store, [F32], comm=[c_e2, s_f])[0]
    done(c_e2)
    c_f = exchange(s_f, halves=True)
    grad, grad16, dg_mlp0 = _rms_bwd("mlp0_dn_norm", dn, h1, mlp_g[0], grad)
    d_ev_out = _mm_tn("ev_dw_out", mix, grad16).reshape(N_DEV, D_MODEL // N_DEV, D_MODEL)
    s_g = swap("ev_out", d_ev_out)
    dmix = _mm_nt("ev_dmix", grad16, w_ev_out, A_DIM + B_DIM, _ep_store, [F32], comm=s_g)[0]
    c_g = exchange(s_g)
    da2, dcv, d_caw, d_cab, d_lag, d_lab, d_cbw = _mixer_bwd_local(z, a2, dmix, seq, lag, lab, comm=[c_f, c_g])
    done(c_g)
    c_f2 = rest(c_f)
    dz = _mixer_bwd_input(z, dmix, da2, dcv, seq, caw, cbw, comm=c_f2)
    done(c_f2)
    small_parts = dict(
        ev_conv_a_w=d_caw, ev_conv_a_b=d_cab, ev_ln_a_g=d_lag, ev_ln_a_b=d_lab,
        ev_conv_b_w=d_cbw, od_norm_g=d_od_g, od_b_in=d_bin, od_ln_v_g=d_lvg, od_ln_v_b=d_lvb,
        od_w_s=d_ws.reshape(C_GROUPS * CHUNK, CHUNK), od_b_s=d_bsb[:, :, 0],
        mlp_norm_g=jnp.concatenate([dg_mlp0, dg_mlp1], axis=0), final_norm_g=d_fin_g)
    early = [n for n in SMALL if n != "ev_norm_g"]
    small_gather = _gather_comm([small_parts[n] for n in early] + [jnp.broadcast_to(loss_part, (SUBLANES, LANE_BLOCK))])
    d_ev_in = _mm_tn("ev_dw_in", n0, dz, comm=small_gather)
    d_ev_in = jnp.transpose(d_ev_in.reshape(D_MODEL, N_DEV, IN_EVEN // N_DEV), (1, 0, 2))
    s_h = swap("ev_in", d_ev_in)
    dn = _mm_nt("ev_dn", dz, w_ev_in, d, _ep_store, [F32], comm=s_h)[0]
    c_h = exchange(s_h)
    grad_x, _, d_ev_g = _rms_bwd("ev_dn_norm", dn, h0, ev_g, grad, comm=c_h)
    done(c_h)

    last_gather = _gather_comm([d_ev_g])
    shard = {"od_w_out": ("od_out",), "mlp_w1": ("w1_0", "w1_1"), "mlp_w2": ("w2_0", "w2_1"), "od_w_in": ("od_in",),
             "ev_w_out": ("ev_out",), "ev_w_in": ("ev_in",)}
    carried = {"od_w_out": last_gather}
    out_g, out_d, out_m, out_v = {}, {}, {}, {}
    for name, keys in shard.items():
        out_g[name], out_d[name], out_m[name], out_v[name] = _sum_adamw(
            f"adamw_{name}", [by_chip[key] for key in keys], W[name], M[name], V[name], comm=carried.get(name))

    gathered = dict(zip(early, small_gather.out[:-1]), ev_norm_g=last_gather.out[0])
    mine = []
    for n in SMALL:
        g = gathered[n]
        if n in SMALL_SHARDED:
            width = W[n].shape[-1]
            g = lax.dynamic_slice_in_dim(g, dev * width, width, axis=2)
        mine.append(g)
    res, loss_tile = _small_adamw("adamw_small", mine, [_rows(W[n]) for n in SMALL], [_rows(M[n]) for n in SMALL],
                                  [_rows(V[n]) for n in SMALL], small_gather.out[-1])
    loss = loss_tile[0, 0]
    for store, values in zip((out_g, out_d, out_m, out_v), res):
        for n, value in zip(SMALL, values):
            store[n] = value.reshape(W[n].shape)

    return (loss, grad_x.reshape(n_seq, seq, d), *[out_g[n] for n in ORDER], *[out_d[n] for n in ORDER],
            *[out_m[n] for n in ORDER], *[out_v[n] for n in ORDER])
```

```python
import math

import jax
import jax.numpy as jnp
from jax import lax
from jax.experimental import pallas as pl
from jax.experimental.pallas import tpu as pltpu

F32 = jnp.float32
BF16 = jnp.bfloat16
MESH = pl.DeviceIdType.MESH

D_MODEL = 1024
A_DIM = 512
B_DIM = 512
IN_EVEN = 2 * A_DIM + 3 * B_DIM
A_TAPS = 31
B_TAPS = 3
CHUNK = 128
C_GROUPS = 8
C_DIM = 1024
D_FF = 4096
RMS_EPS = 1e-6
LN_EPS = 1e-5
N_DEV = 8
N_CHIP = 4

ADAM_LR = 0.001
ADAM_B1 = 0.9
ADAM_B2 = 0.999
ADAM_EPS = 1e-08
ADAM_WD = 0.01
ADAM_STEP = 10

A_HALO = 32
B_HALO = 8
VMEM_LIMIT_BYTES = 56 * 1024 * 1024
TINY = 1.1754944e-38
INV_SQRT2 = 1.0 / math.sqrt(2.0)
INV_SQRT_2PI = 1.0 / math.sqrt(2.0 * math.pi)
HBM_SPEC = pl.BlockSpec(memory_space=pltpu.HBM)


def _tile(n, want):
    t = min(n, want)
    while n % t:
        t //= 2
    return t


def _sigmoid(x):
    return 1.0 / (1.0 + jnp.exp(-x))


def _gelu(x):
    return 0.5 * x * (1.0 + lax.erf(x * INV_SQRT2))


def _gelu_and_grad(x):
    cdf = 0.5 * (1.0 + lax.erf(x * INV_SQRT2))
    return x * cdf, cdf + x * jnp.exp(-0.5 * x * x) * INV_SQRT_2PI


def _colsum(x):
    return jnp.sum(x, axis=0, keepdims=True)


class _Comm:
    def __init__(self, ins, out_shapes, sem_shapes, start, finish, middle=None, into=None, sibling=False, chips=False):
        self.ins, self.out_shapes, self.sem_shapes, self.start, self.finish = ins, out_shapes, sem_shapes, start, finish
        self.middle = middle
        self.sibling, self.chips = sibling, chips
        self.into = list(into) if into is not None else []
        self.out = None


def _piece_rows(rows, piece):
    if piece is None:
        return 0, rows
    i, n = piece
    return i * (rows // n), rows // n


MIDDLE_AT = 0.75
BARRIER_IDS = {(True, True): 0, (True, False): 1, (False, True): 2}
FUSED_ROWS = 512


def _call(body, *, name, grid, in_specs, out_specs, out_shape, args, scratch_shapes=(), parallel=False, comm=None):
    comms = [] if comm is None else (list(comm) if isinstance(comm, (list, tuple)) else [comm])
    if not comms:
        sem = ("parallel" if parallel else "arbitrary",) * len(grid)
        return pl.pallas_call(
            body, name=name, grid=grid, in_specs=list(in_specs), out_specs=list(out_specs), out_shape=list(out_shape),
            scratch_shapes=list(scratch_shapes),
            compiler_params=pltpu.CompilerParams(dimension_semantics=sem, vmem_limit_bytes=VMEM_LIMIT_BYTES),
        )(*args)
    n_in, n_out, n_scr = len(in_specs), len(out_shape), len(scratch_shapes)
    c_ins_all = [a for cm in comms for a in cm.ins]
    c_into_all = [a for cm in comms for a in cm.into]
    c_out_shapes = [s for cm in comms for s in cm.out_shapes]
    c_sem_shapes = [s for cm in comms for s in cm.sem_shapes]
    aliases, in_pos, out_pos = {}, n_in + len(c_ins_all), n_out
    for cm in comms:
        for j in range(len(cm.into)):
            aliases[in_pos + j] = out_pos + j
        in_pos += len(cm.into)
        out_pos += len(cm.out_shapes)
    to_sibling = any(cm.sibling for cm in comms)
    to_chips = any(cm.chips for cm in comms)
    steps = grid
    total = math.prod(steps)
    first_step = (0,) * len(steps)
    last_step = tuple(s - 1 for s in steps)
    middle_step = None
    if 0 < int(MIDDLE_AT * total) < total - 1:
        rest, idx = int(MIDDLE_AT * total), []
        for s in reversed(steps):
            idx.append(rest % s)
            rest //= s
        middle_step = tuple(reversed(idx))

    def carrying(*refs):
        pos = 0
        ins = refs[pos:pos + n_in]; pos += n_in
        c_ins = refs[pos:pos + len(c_ins_all)]; pos += len(c_ins_all) + len(c_into_all)
        outs = refs[pos:pos + n_out]; pos += n_out
        c_outs = refs[pos:pos + len(c_out_shapes)]; pos += len(c_out_shapes)
        scr = refs[pos:pos + n_scr]; pos += n_scr
        c_sems = refs[pos:]
        views, i0, o0, s0 = [], 0, 0, 0
        for cm in comms:
            views.append((c_ins[i0:i0 + len(cm.ins)], c_outs[o0:o0 + len(cm.out_shapes)], c_sems[s0:s0 + len(cm.sem_shapes)]))
            i0, o0, s0 = i0 + len(cm.ins), o0 + len(cm.out_shapes), s0 + len(cm.sem_shapes)

        def at(step):
            hit = pl.program_id(0) == step[0]
            for axis in range(1, len(steps)):
                hit = jnp.logical_and(hit, pl.program_id(axis) == step[axis])
            return hit

        @pl.when(at(first_step))
        def _():
            x, y, c, chips = _place()
            peers = ([(x, y, 1 - c)] if to_sibling else []) + ([(*chip, c) for chip in chips] if to_chips else [])
            barrier = pltpu.get_barrier_semaphore()
            for peer in peers:
                pl.semaphore_signal(barrier, inc=1, device_id=peer, device_id_type=MESH)
            pl.semaphore_wait(barrier, len(peers))
            for cm, view in zip(comms, views):
                cm.start(*view)

        if middle_step is not None:
            @pl.when(at(middle_step))
            def _():
                for cm, view in zip(comms, views):
                    if cm.middle is not None:
                        cm.middle(*view)

        body(*ins, *outs, *scr)

        @pl.when(at(last_step))
        def _():
            for cm, view in zip(comms, views):
                if cm.middle is not None and middle_step is None:
                    cm.middle(*view)
            for cm, view in zip(comms, views):
                cm.finish(*view)

    res = pl.pallas_call(
        carrying, name=name, grid=grid,
        in_specs=[*in_specs, *[HBM_SPEC] * (len(c_ins_all) + len(c_into_all))],
        out_specs=[*out_specs, *[HBM_SPEC] * len(c_out_shapes)],
        out_shape=[*out_shape, *c_out_shapes], scratch_shapes=[*scratch_shapes, *c_sem_shapes],
        input_output_aliases=aliases,
        compiler_params=pltpu.CompilerParams(dimension_semantics=("arbitrary",) * len(grid), vmem_limit_bytes=VMEM_LIMIT_BYTES,
                                             collective_id=BARRIER_IDS[to_sibling, to_chips]),
    )(*args, *c_ins_all, *c_into_all)
    pos = n_out
    for cm in comms:
        cm.out = list(res[pos:pos + len(cm.out_shapes)])
        pos += len(cm.out_shapes)
    return list(res[:n_out])


def _place():
    x, y, c = lax.axis_index("x"), lax.axis_index("y"), lax.axis_index("c")
    return x, y, c, [(1 - x, y), (x, 1 - y), (1 - x, 1 - y)]


def _gather_comm(shards, piece=None, into=None):
    nw = len(shards)
    spans = [_piece_rows(s.shape[0], piece) for s in shards]

    def plan(ins, outs, sems):
        send_sems, recv_sems, local_sems = sems
        x, y, c, chips = _place()
        me, sibling = (x, y, c), (x, y, 1 - c)

        def slot(w, p):
            return outs[w].at[4 * p[0] + 2 * p[1] + p[2], pl.ds(*spans[w])]

        def mine(w):
            return ins[w].at[pl.ds(*spans[w])]

        def copy(w, k, block, to, src=None):
            return pltpu.make_async_remote_copy(
                src_ref=slot(w, block) if src is None else src, dst_ref=slot(w, block),
                send_sem=send_sems.at[w, k], recv_sem=recv_sems.at[w, k], device_id=to, device_id_type=MESH)

        local = [pltpu.make_async_copy(mine(w), slot(w, me), local_sems.at[w]) for w in range(nw)]
        first = [[copy(w, 0, me, sibling, src=mine(w))] + [copy(w, 1 + j, me, (*chip, c), src=mine(w)) for j, chip in enumerate(chips)]
                 for w in range(nw)]
        landed = [[copy(w, 1 + j, (*chip, c), me) for j, chip in enumerate(chips)] for w in range(nw)]
        passed = [[copy(w, 4 + j, (*chip, c), sibling) for j, chip in enumerate(chips)] for w in range(nw)]
        from_sibling = [[copy(w, 0, sibling, me)] + [copy(w, 4 + j, (*chip, 1 - c), me) for j, chip in enumerate(chips)]
                        for w in range(nw)]
        return local, first, landed, passed, from_sibling

    def start(ins, outs, sems):
        local, first, _, _, _ = plan(ins, outs, sems)
        for cp in local:
            cp.start()
        for row in first:
            for cp in row:
                cp.start()

    def middle(ins, outs, sems):
        _, _, landed, passed, _ = plan(ins, outs, sems)
        for w in range(nw):
            for j in range(3):
                landed[w][j].wait_recv()
                passed[w][j].start()

    def finish(ins, outs, sems):
        local, first, landed, passed, from_sibling = plan(ins, outs, sems)
        for w in range(nw):
            for cp in from_sibling[w]:
                cp.wait_recv()
        for w in range(nw):
            for cp in first[w] + passed[w]:
                cp.wait_send()
        for cp in local:
            cp.wait()

    return _Comm(list(shards), [jax.ShapeDtypeStruct((N_DEV, *s.shape), s.dtype) for s in shards],
                 [pltpu.SemaphoreType.DMA((nw, 7)), pltpu.SemaphoreType.DMA((nw, 7)), pltpu.SemaphoreType.DMA((nw,))],
                 start, finish, middle, into=into, sibling=True, chips=True)


def _pair_comm(parts):
    nw = len(parts)

    def plan(ins, outs, sems):
        send_sems, recv_sems = sems
        x, y, c, _ = _place()
        return [pltpu.make_async_remote_copy(
            src_ref=ins[w].at[2 * q + 1 - c], dst_ref=outs[w].at[q], send_sem=send_sems.at[w, q], recv_sem=recv_sems.at[w, q],
            device_id=(x, y, 1 - c), device_id_type=MESH) for w in range(nw) for q in range(N_CHIP)]

    def start(ins, outs, sems):
        for cp in plan(ins, outs, sems):
            cp.start()

    def finish(ins, outs, sems):
        for cp in plan(ins, outs, sems):
            cp.wait()

    return _Comm(list(parts), [jax.ShapeDtypeStruct((N_CHIP, *p.shape[1:]), p.dtype) for p in parts],
                 [pltpu.SemaphoreType.DMA((nw, N_CHIP)), pltpu.SemaphoreType.DMA((nw, N_CHIP))], start, finish, sibling=True)


def _chip_comm(sums, piece=None, into=None):
    nw = len(sums)
    spans = [_piece_rows(s.shape[1], piece) for s in sums]

    def plan(ins, outs, sems):
        send_sems, recv_sems, local_sems = sems
        x, y, c, chips = _place()
        my_chip = 2 * x + y
        local = [pltpu.make_async_copy(ins[w].at[my_chip, pl.ds(*spans[w])], outs[w].at[my_chip, pl.ds(*spans[w])],
                                       local_sems.at[w]) for w in range(nw)]
        remote = [pltpu.make_async_remote_copy(
            src_ref=ins[w].at[2 * chip[0] + chip[1], pl.ds(*spans[w])], dst_ref=outs[w].at[my_chip, pl.ds(*spans[w])],
            send_sem=send_sems.at[w, j], recv_sem=recv_sems.at[w, j], device_id=(*chip, c), device_id_type=MESH)
            for w in range(nw) for j, chip in enumerate(chips)]
        return local, remote

    def start(ins, outs, sems):
        local, remote = plan(ins, outs, sems)
        for cp in local + remote:
            cp.start()

    def finish(ins, outs, sems):
        local, remote = plan(ins, outs, sems)
        for cp in remote + local:
            cp.wait()

    return _Comm(list(sums), [jax.ShapeDtypeStruct(s.shape, s.dtype) for s in sums],
                 [pltpu.SemaphoreType.DMA((nw, 3)), pltpu.SemaphoreType.DMA((nw, 3)), pltpu.SemaphoreType.DMA((nw,))],
                 start, finish, into=into, chips=True)


def _matmul(name, a, b, *, kind, m, n, k, a_spec, b_spec, tm, tn, tk, out_shape, out_specs, epilogue,
            extras=(), extra_specs=(), comm=None):
    dims = {"nn": (((1,), (0,)), ((), ())), "nt": (((1,), (1,)), ((), ())), "tn": (((0,), (0,)), ((), ()))}[kind]
    nk = k // tk
    n_extra = len(extras)
    n_out = len(out_shape)

    def body(a_ref, b_ref, *rest):
        extra_refs = rest[:n_extra]
        out_refs = rest[n_extra:n_extra + n_out]
        part = lax.dot_general(a_ref[...], b_ref[...], dims, preferred_element_type=F32)
        if nk == 1:
            epilogue(part, extra_refs, out_refs)
            return
        acc_ref = rest[n_extra + n_out]
        step = pl.program_id(2)

        @pl.when(step == 0)
        def _():
            acc_ref[...] = part

        @pl.when(jnp.logical_and(step > 0, step < nk - 1))
        def _():
            acc_ref[...] += part

        @pl.when(step == nk - 1)
        def _():
            epilogue(acc_ref[...] + part, extra_refs, out_refs)

    return _call(
        body, name=name, grid=(m // tm, n // tn, nk), in_specs=[a_spec, b_spec, *extra_specs], out_specs=out_specs,
        out_shape=out_shape, scratch_shapes=[pltpu.VMEM((tm, tn), F32)] if nk > 1 else [], args=(a, b, *extras), comm=comm)


def _mm_operands(m, n, tm, tn, out_dtypes, extras, rows, sums):
    tile = pl.BlockSpec((tm, tn), lambda i, j, kk: (i, j))
    row = pl.BlockSpec((1, tn), lambda i, j, kk: (0, j))
    one = pl.BlockSpec((1, 1), lambda i, j, kk: (0, 0))
    out_shape = [jax.ShapeDtypeStruct((m, n), dt) for dt in out_dtypes]
    out_shape += [jax.ShapeDtypeStruct((1, n if wide else 1), F32) for wide in sums]
    out_specs = [tile] * len(out_dtypes) + [row if wide else one for wide in sums]
    return (*extras, *rows), [tile] * len(extras) + [row] * len(rows), out_shape, out_specs


def _row_tile(m, k, tm_want):
    return _tile(m, tm_want or (1024 if k <= 1024 else 512))


def _mm_nn(name, a, b, n, epilogue, out_dtypes, *, extras=(), rows=(), sums=(), tm_want=None, comm=None):
    m, k = a.shape
    blocked = b.ndim == 3
    tm = _row_tile(m, k, tm_want)
    tn = b.shape[-1] if blocked else _tile(n, 1024)
    tk = k
    extras, extra_specs, out_shape, out_specs = _mm_operands(m, n, tm, tn, out_dtypes, extras, rows, sums)
    if blocked:
        b_spec = pl.BlockSpec((None, tk, tn), lambda i, j, kk: (j, kk, 0))
    else:
        b_spec = pl.BlockSpec((tk, tn), lambda i, j, kk: (kk, j))
    return _matmul(
        name, a, b, kind="nn", m=m, n=n, k=k, tm=tm, tn=tn, tk=tk,
        a_spec=pl.BlockSpec((tm, tk), lambda i, j, kk: (i, kk)), b_spec=b_spec, out_shape=out_shape, out_specs=out_specs,
        epilogue=epilogue, extras=extras, extra_specs=extra_specs, comm=comm)


def _mm_nt(name, a, b, n, epilogue, out_dtypes, *, extras=(), rows=(), sums=(), tm_want=None, comm=None):
    m, k = a.shape
    blocked = b.ndim == 3
    tm = _row_tile(m, k, tm_want)
    tn = _tile(n, 1024)
    tk = b.shape[-1] if blocked else k
    extras, extra_specs, out_shape, out_specs = _mm_operands(m, n, tm, tn, out_dtypes, extras, rows, sums)
    if blocked:
        b_spec = pl.BlockSpec((None, tn, tk), lambda i, j, kk: (kk, j, 0))
    else:
        b_spec = pl.BlockSpec((tn, tk), lambda i, j, kk: (j, kk))
    return _matmul(
        name, a, b, kind="nt", m=m, n=n, k=k, tm=tm, tn=tn, tk=tk,
        a_spec=pl.BlockSpec((tm, tk), lambda i, j, kk: (i, kk)), b_spec=b_spec, out_shape=out_shape, out_specs=out_specs,
        epilogue=epilogue, extras=extras, extra_specs=extra_specs, comm=comm)


def _mm_tn(name, a, b, *, col_blocks=0, comm=None):
    t, k = a.shape
    n = b.shape[1]
    tm = _row_tile(k, t, None)
    tn = n // col_blocks if col_blocks else _tile(n, 1024)
    tk = t
    if col_blocks:
        out_shape = [jax.ShapeDtypeStruct((col_blocks, k, tn), F32)]
        out_specs = [pl.BlockSpec((None, tm, tn), lambda i, j, kk: (j, i, 0))]
    else:
        out_shape = [jax.ShapeDtypeStruct((k, n), F32)]
        out_specs = [pl.BlockSpec((tm, tn), lambda i, j, kk: (i, j))]

    def epilogue(acc, extra_refs, out_refs):
        out_refs[0][...] = acc

    return _matmul(
        name, a, b, kind="tn", m=k, n=n, k=t, tm=tm, tn=tn, tk=tk,
        a_spec=pl.BlockSpec((tk, tm), lambda i, j, kk: (kk, i)), b_spec=pl.BlockSpec((tk, tn), lambda i, j, kk: (kk, j)),
        out_shape=out_shape, out_specs=out_specs, epilogue=epilogue, comm=comm)[0]


def _ep_store(acc, extra_refs, out_refs):
    out_refs[0][...] = acc.astype(out_refs[0].dtype)


def _ep_bias(acc, extra_refs, out_refs):
    out_refs[0][...] = acc + extra_refs[0][...]


def _rms(x):
    r = lax.rsqrt(jnp.mean(x * x, axis=-1, keepdims=True) + RMS_EPS)
    return r, x * r


def _rms_grad(dn, r, xr, g):
    dy = dn * g
    return r * (dy - xr * jnp.mean(dy * xr, axis=-1, keepdims=True))


def _ep_residual_norm(acc, extra_refs, out_refs):
    h = extra_refs[0][...] + acc
    out_refs[0][...] = h
    _, xr = _rms(h)
    out_refs[1][...] = (xr * extra_refs[1][...]).astype(BF16)


def _ep_final_loss(acc, extra_refs, out_refs):
    @pl.when(pl.program_id(0) == 0)
    def _():
        out_refs[2][...] = jnp.zeros_like(out_refs[2])
        out_refs[3][...] = jnp.zeros_like(out_refs[3])

    h = extra_refs[0][...] + acc
    g = extra_refs[2][...]
    r, xr = _rms(h)
    err = xr * g - extra_refs[1][...]
    out_refs[3][...] += 0.5 * jnp.sum(jnp.mean(err * err, axis=-1, keepdims=True), axis=0, keepdims=True)
    dout = err * (1.0 / h.shape[-1])
    out_refs[2][...] += _colsum(dout * xr)
    out = _rms_grad(dout, r, xr, g)
    out_refs[0][...] = out
    out_refs[1][...] = out.astype(BF16)


def _ep_relu_sq(acc, extra_refs, out_refs):
    r = jnp.maximum(acc, 0.0)
    out_refs[0][...] = (r * r).astype(BF16)


def _ep_relu_sq_grad(acc, extra_refs, out_refs):
    q = extra_refs[0][...].astype(F32)
    out_refs[0][...] = (acc * (2.0 * q * lax.rsqrt(jnp.maximum(q, TINY)))).astype(BF16)


def _rms_fwd(name, h, g, comm=None):
    t, d = h.shape
    tt = _tile(t, 512)

    def body(h_ref, g_ref, n_ref):
        x = h_ref[...]
        r = lax.rsqrt(jnp.mean(x * x, axis=-1, keepdims=True) + RMS_EPS)
        n_ref[...] = (x * r * g_ref[...]).astype(BF16)

    return _call(
        body, name=name, grid=(t // tt,),
        in_specs=[pl.BlockSpec((tt, d), lambda i: (i, 0)), pl.BlockSpec((1, d), lambda i: (0, 0))],
        out_specs=[pl.BlockSpec((tt, d), lambda i: (i, 0))], out_shape=[jax.ShapeDtypeStruct((t, d), BF16)],
        args=(h, g), parallel=True, comm=comm)[0]


def _rms_bwd(name, dn, h, g, grad_in, comm=None):
    t, d = h.shape
    tt = _tile(t, 512)

    def body(dn_ref, h_ref, g_ref, gin_ref, gout_ref, gout16_ref, dg_ref):
        @pl.when(pl.program_id(0) == 0)
        def _():
            dg_ref[...] = jnp.zeros_like(dg_ref)

        dnv = dn_ref[...]
        r, xr = _rms(h_ref[...])
        dg_ref[...] += _colsum(dnv * xr)
        out = gin_ref[...] + _rms_grad(dnv, r, xr, g_ref[...])
        gout_ref[...] = out
        gout16_ref[...] = out.astype(BF16)

    row = pl.BlockSpec((tt, d), lambda i: (i, 0))
    vec = pl.BlockSpec((1, d), lambda i: (0, 0))
    return _call(
        body, name=name, grid=(t // tt,), in_specs=[row, row, vec, row], out_specs=[row, row, vec],
        out_shape=[jax.ShapeDtypeStruct((t, d), F32), jax.ShapeDtypeStruct((t, d), BF16), jax.ShapeDtypeStruct((1, d), F32)],
        args=(dn, h, g, grad_in), comm=comm)


def _mixer_windows(z_ref, zh_ref, first, a1_s, cb_s, tt):
    sig = _sigmoid(z_ref[:, A_DIM:2 * A_DIM])
    a1_s[A_HALO:A_HALO + tt, :] = z_ref[:, 0:A_DIM] * sig
    a1_h = zh_ref[:, 0:A_DIM] * _sigmoid(zh_ref[:, A_DIM:2 * A_DIM])
    a1_s[0:A_HALO, :] = jnp.where(first, 0.0, a1_h)
    cb_s[B_HALO:B_HALO + tt, :] = z_ref[:, 3 * A_DIM:4 * A_DIM] * z_ref[:, 4 * A_DIM:5 * A_DIM]
    cb_h = zh_ref[A_HALO - B_HALO:A_HALO, 3 * A_DIM:4 * A_DIM] * zh_ref[A_HALO - B_HALO:A_HALO, 4 * A_DIM:5 * A_DIM]
    cb_s[0:B_HALO, :] = jnp.where(first, 0.0, cb_h)
    return sig


def _causal_conv(win_s, w_ref, taps, halo, tt):
    base = halo - (taps - 1)
    acc = w_ref[0:1, :] * win_s[pl.ds(base, tt), :]
    for k in range(1, taps):
        acc = acc + w_ref[k:k + 1, :] * win_s[pl.ds(base + k, tt), :]
    return acc


SUBLANES = 8
LANE_BLOCK = 128
ROW_BLOCK = 128
SHIFT_ROWS = A_HALO - SUBLANES


def _shifted_copies(win_s, sh_s, tt):
    for b in range(1, SUBLANES):
        sh_s[b - 1] = win_s[pl.ds(b, tt + SHIFT_ROWS), :]


def _window_rows(win_s, sh_s, offset, rows, cols):
    b = offset % SUBLANES
    if b == 0:
        return win_s[pl.ds(offset, rows), cols]
    return sh_s[b - 1, pl.ds(offset - b, rows), cols]


def _blocks(tt):
    rb = min(tt, ROW_BLOCK)
    return rb, [(r, slice(lb * LANE_BLOCK, (lb + 1) * LANE_BLOCK))
                for lb in range(A_DIM // LANE_BLOCK) for r in range(0, tt, rb)]


def _conv_taps(win_s, sh_s, w_ref, offsets, out_s, tt, bias_ref=None):
    rb, blocks = _blocks(tt)
    for r, cols in blocks:
        acc = w_ref[0:1, cols] * _window_rows(win_s, sh_s, r + offsets[0], rb, cols)
        for k in range(1, len(offsets)):
            acc = acc + w_ref[k:k + 1, cols] * _window_rows(win_s, sh_s, r + offsets[k], rb, cols)
        out_s[r:r + rb, cols] = acc if bias_ref is None else acc + bias_ref[:, cols]


A_CAUSAL = [A_HALO - (A_TAPS - 1) + k for k in range(A_TAPS)]
A_ANTICAUSAL = [A_TAPS - 1 - k for k in range(A_TAPS)]


def _layer_norm_stats(x):
    mu = jnp.mean(x, axis=-1, keepdims=True)
    xc = x - mu
    rstd = lax.rsqrt(jnp.mean(xc * xc, axis=-1, keepdims=True) + LN_EPS)
    return xc * rstd, rstd


def _mixer_specs(seq, tt):
    tiles_per_seq = seq // tt
    halo_blocks = tt // A_HALO
    z_spec = pl.BlockSpec((tt, IN_EVEN), lambda i: (i, 0))
    zh_spec = pl.BlockSpec((A_HALO, IN_EVEN), lambda i: (jnp.maximum(i * halo_blocks - 1, 0), 0))
    return tiles_per_seq, z_spec, zh_spec


def _vec_spec(rows, cols):
    return pl.BlockSpec((rows, cols), lambda i: (0, 0))


def _mixer_fwd(z, seq, caw, cab, lag, lab, cbw, comm=None):
    t = z.shape[0]
    tt = _tile(seq, 256)
    tiles_per_seq, z_spec, zh_spec = _mixer_specs(seq, tt)

    def body(z_ref, zh_ref, caw_ref, cab_ref, lag_ref, lab_ref, cbw_ref, mix_ref, a2_ref, a1_s, cb_s, sh_s):
        first = (pl.program_id(0) % tiles_per_seq) == 0
        _mixer_windows(z_ref, zh_ref, first, a1_s, cb_s, tt)
        _shifted_copies(a1_s, sh_s, tt)
        _conv_taps(a1_s, sh_s, caw_ref, A_CAUSAL, a2_ref, tt, bias_ref=cab_ref)
        xhat, _ = _layer_norm_stats(a2_ref[...])
        a3 = xhat * lag_ref[...] + lab_ref[...]
        mix_ref[:, 0:A_DIM] = (a3 * _sigmoid(a3)).astype(BF16)
        cv = _causal_conv(cb_s, cbw_ref, B_TAPS, B_HALO, tt)
        mix_ref[:, A_DIM:A_DIM + B_DIM] = (z_ref[:, 2 * A_DIM:3 * A_DIM] * cv).astype(BF16)

    return _call(
        body, name="mixer_fwd", grid=(t // tt,),
        in_specs=[z_spec, zh_spec, _vec_spec(A_TAPS, A_DIM), _vec_spec(1, A_DIM), _vec_spec(1, A_DIM), _vec_spec(1, A_DIM),
                  _vec_spec(B_TAPS, B_DIM)],
        out_specs=[pl.BlockSpec((tt, A_DIM + B_DIM), lambda i: (i, 0)), pl.BlockSpec((tt, A_DIM), lambda i: (i, 0))],
        out_shape=[jax.ShapeDtypeStruct((t, A_DIM + B_DIM), BF16), jax.ShapeDtypeStruct((t, A_DIM), F32)],
        scratch_shapes=[pltpu.VMEM((A_HALO + tt, A_DIM), F32), pltpu.VMEM((B_HALO + tt, B_DIM), F32),
                        pltpu.VMEM((SUBLANES - 1, tt + SHIFT_ROWS, A_DIM), F32)],
        args=(z, z, caw, cab, lag, lab, cbw), parallel=True, comm=comm)


def _mixer_bwd_local(z, a2, dmix, seq, lag, lab, comm=None):
    t = z.shape[0]
    tt = _tile(seq, 256)
    tiles_per_seq, z_spec, zh_spec = _mixer_specs(seq, tt)

    def body(z_ref, zh_ref, a2_ref, dmix_ref, lag_ref, lab_ref,
             da2_ref, dcv_ref, dcaw_ref, dcab_ref, dlag_ref, dlab_ref, dcbw_ref, a1_s, cb_s, sh_s):
        @pl.when(pl.program_id(0) == 0)
        def _():
            for ref in (dcaw_ref, dcab_ref, dlag_ref, dlab_ref, dcbw_ref):
                ref[...] = jnp.zeros_like(ref)

        first = (pl.program_id(0) % tiles_per_seq) == 0
        _mixer_windows(z_ref, zh_ref, first, a1_s, cb_s, tt)
        _shifted_copies(a1_s, sh_s, tt)
        xhat, rstd = _layer_norm_stats(a2_ref[...])
        a3 = xhat * lag_ref[...] + lab_ref[...]
        s3 = _sigmoid(a3)
        da3 = dmix_ref[:, 0:A_DIM] * (s3 * (1.0 + a3 * (1.0 - s3)))
        dlag_ref[...] += _colsum(da3 * xhat)
        dlab_ref[...] += _colsum(da3)
        dxh = da3 * lag_ref[...]
        da2 = rstd * (dxh - jnp.mean(dxh, axis=-1, keepdims=True) - xhat * jnp.mean(dxh * xhat, axis=-1, keepdims=True))
        da2_ref[...] = da2
        dcab_ref[...] += _colsum(da2)
        rb, blocks = _blocks(tt)
        for r, cols in blocks:
            da2_b = da2_ref[r:r + rb, cols]
            for k in range(A_TAPS):
                dcaw_ref[k:k + 1, cols] += _colsum(da2_b * _window_rows(a1_s, sh_s, r + A_CAUSAL[k], rb, cols))
        dcv =dmix_ref[:, A_DIM:A_DIM + B_DIM] * z_ref[:, 2 * A_DIM:3 * A_DIM]
        dcv_ref[...] = dcv
        for k in range(B_TAPS):
            dcbw_ref[k:k + 1, :] += _colsum(dcv * cb_s[pl.ds(B_HALO - (B_TAPS - 1) + k, tt), :])

    half = pl.BlockSpec((tt, A_DIM), lambda i: (i, 0))
    return _call(
        body, name="mixer_bwd_local", grid=(t // tt,),
        in_specs=[z_spec, zh_spec, half, pl.BlockSpec((tt, A_DIM + B_DIM), lambda i: (i, 0)),
                  _vec_spec(1, A_DIM), _vec_spec(1, A_DIM)],
        out_specs=[half, half, _vec_spec(A_TAPS, A_DIM), _vec_spec(1, A_DIM), _vec_spec(1, A_DIM), _vec_spec(1, A_DIM),
                   _vec_spec(B_TAPS, B_DIM)],
        out_shape=[jax.ShapeDtypeStruct((t, A_DIM), F32), jax.ShapeDtypeStruct((t, B_DIM), F32),
                   jax.ShapeDtypeStruct((A_TAPS, A_DIM), F32), jax.ShapeDtypeStruct((1, A_DIM), F32),
                   jax.ShapeDtypeStruct((1, A_DIM), F32), jax.ShapeDtypeStruct((1, A_DIM), F32),
                   jax.ShapeDtypeStruct((B_TAPS, B_DIM), F32)],
        scratch_shapes=[pltpu.VMEM((A_HALO + tt, A_DIM), F32), pltpu.VMEM((B_HALO + tt, B_DIM), F32),
                        pltpu.VMEM((SUBLANES - 1, tt + SHIFT_ROWS, A_DIM), F32)],
        args=(z, z, a2, dmix, lag, lab), comm=comm)


def _mixer_bwd_input(z, dmix, da2, dcv, seq, caw, cbw, comm=None):
    t = z.shape[0]
    tt = _tile(seq, 256)
    tiles_per_seq, z_spec, zh_spec = _mixer_specs(seq, tt)
    a_blocks = tt // A_HALO
    b_blocks = tt // B_HALO
    last_a = t // A_HALO - 1
    last_b = t // B_HALO - 1

    def body(z_ref, zh_ref, dmix_ref, da2_ref, da2n_ref, dcv_ref, dcvn_ref, caw_ref, cbw_ref, dz_ref, a1_s, cb_s, da2_s, dcv_s,
             sh_s, da1_s):
        pos = pl.program_id(0) % tiles_per_seq
        first = pos == 0
        last = pos == tiles_per_seq - 1
        sig = _mixer_windows(z_ref, zh_ref, first, a1_s, cb_s, tt)
        da2_s[0:tt, :] = da2_ref[...]
        da2_s[tt:tt + A_HALO, :] = jnp.where(last, 0.0, da2n_ref[...])
        dcv_s[0:tt, :] = dcv_ref[...]
        dcv_s[tt:tt + B_HALO, :] = jnp.where(last, 0.0, dcvn_ref[...])
        _shifted_copies(da2_s, sh_s, tt)
        _conv_taps(da2_s, sh_s, caw_ref, A_ANTICAUSAL, da1_s, tt)
        da1 = da1_s[...]
        dz_ref[:, 0:A_DIM] = (da1 * sig).astype(BF16)
        dz_ref[:, A_DIM:2 * A_DIM] = (da1 * z_ref[:, 0:A_DIM] * sig * (1.0 - sig)).astype(BF16)
        cv = _causal_conv(cb_s, cbw_ref, B_TAPS, B_HALO, tt)
        dz_ref[:, 2 * A_DIM:3 * A_DIM] = (dmix_ref[:, A_DIM:A_DIM + B_DIM] * cv).astype(BF16)
        dcb = cbw_ref[0:1, :] * dcv_s[pl.ds(B_TAPS - 1, tt), :]
        for k in range(1, B_TAPS):
            dcb = dcb + cbw_ref[k:k + 1, :] * dcv_s[pl.ds(B_TAPS - 1 - k, tt), :]
        dz_ref[:, 3 * A_DIM:4 * A_DIM] = (dcb * z_ref[:, 4 * A_DIM:5 * A_DIM]).astype(BF16)
        dz_ref[:, 4 * A_DIM:5 * A_DIM] = (dcb * z_ref[:, 3 * A_DIM:4 * A_DIM]).astype(BF16)

    half = pl.BlockSpec((tt, A_DIM), lambda i: (i, 0))
    return _call(
        body, name="mixer_bwd_input", grid=(t // tt,),
        in_specs=[z_spec, zh_spec, pl.BlockSpec((tt, A_DIM + B_DIM), lambda i: (i, 0)),
                  half, pl.BlockSpec((A_HALO, A_DIM), lambda i: (jnp.minimum((i + 1) * a_blocks, last_a), 0)),
                  half, pl.BlockSpec((B_HALO, B_DIM), lambda i: (jnp.minimum((i + 1) * b_blocks, last_b), 0)),
                  _vec_spec(A_TAPS, A_DIM), _vec_spec(B_TAPS, B_DIM)],
        out_specs=[pl.BlockSpec((tt, IN_EVEN), lambda i: (i, 0))],
        out_shape=[jax.ShapeDtypeStruct((t, IN_EVEN), BF16)],
        scratch_shapes=[pltpu.VMEM((A_HALO + tt, A_DIM), F32), pltpu.VMEM((B_HALO + tt, B_DIM), F32),
                        pltpu.VMEM((tt + A_HALO, A_DIM), F32), pltpu.VMEM((tt + B_HALO, B_DIM), F32),
                        pltpu.VMEM((SUBLANES - 1, tt + SHIFT_ROWS, A_DIM), F32), pltpu.VMEM((tt, A_DIM), F32)],
        args=(z, z, dmix, da2, da2, dcv, dcv, caw, cbw), parallel=True, comm=comm)[0]


def _tril_ws(ws_ref, g):
    rows = lax.broadcasted_iota(jnp.int32, (CHUNK, CHUNK), 0)
    cols = lax.broadcasted_iota(jnp.int32, (CHUNK, CHUNK), 1)
    return jnp.where(rows >= cols, ws_ref[g], 0.0).astype(BF16), rows >= cols


def _sgu_fwd(pre, lvg, lvb, ws, bs_b, comm=None):
    t = pre.shape[0]
    tt = _tile(t, 256)

    def body(pre_ref, lvg_ref, lvb_ref, ws_ref, bsb_ref, y_ref):
        vhat, _ = _layer_norm_stats(_gelu(pre_ref[:, C_DIM:2 * C_DIM]))
        vl = (vhat * lvg_ref[...] + lvb_ref[...]).astype(BF16)
        for g in range(C_GROUPS):
            w, _ = _tril_ws(ws_ref, g)
            cols = slice(g * CHUNK, (g + 1) * CHUNK)
            for ci in range(tt // CHUNK):
                rows = slice(ci * CHUNK, (ci + 1) * CHUNK)
                sv = jnp.dot(w, vl[rows, cols], preferred_element_type=F32) + bsb_ref[g]
                y_ref[rows, cols] = (_gelu(pre_ref[rows, cols]) * sv).astype(BF16)

    group = pl.BlockSpec((C_GROUPS, CHUNK, CHUNK), lambda i: (0, 0, 0))
    return _call(
        body, name="sgu_fwd", grid=(t // tt,),
        in_specs=[pl.BlockSpec((tt, 2 * C_DIM), lambda i: (i, 0)), _vec_spec(1, C_DIM), _vec_spec(1, C_DIM), group, group],
        out_specs=[pl.BlockSpec((tt, C_DIM), lambda i: (i, 0))], out_shape=[jax.ShapeDtypeStruct((t, C_DIM), BF16)],
        args=(pre, lvg, lvb, ws, bs_b), parallel=True, comm=comm)[0]


def _sgu_bwd(pre, dy, lvg, lvb, ws, bs_b, comm=None):
    t = pre.shape[0]
    tt = _tile(t, 256)

    def body(pre_ref, dy_ref, lvg_ref, lvb_ref, ws_ref, bsb_ref, dpre_ref, dws_ref, dbsb_ref, dlvg_ref, dlvb_ref, dbin_ref,
             dvl_s):
        @pl.when(pl.program_id(0) == 0)
        def _():
            for ref in (dws_ref, dbsb_ref, dlvg_ref, dlvb_ref, dbin_ref):
                ref[...] = jnp.zeros_like(ref)

        v, v_grad = _gelu_and_grad(pre_ref[:, C_DIM:2 * C_DIM])
        vhat, rstd = _layer_norm_stats(v)
        vl = (vhat * lvg_ref[...] + lvb_ref[...]).astype(BF16)
        for g in range(C_GROUPS):
            w, keep = _tril_ws(ws_ref, g)
            cols = slice(g * CHUNK, (g + 1) * CHUNK)
            dws = jnp.zeros((CHUNK, CHUNK), F32)
            dbs = jnp.zeros((CHUNK, 1), F32)
            for ci in range(tt // CHUNK):
                rows = slice(ci * CHUNK, (ci + 1) * CHUNK)
                vl_g = vl[rows, cols]
                sv = jnp.dot(w, vl_g, preferred_element_type=F32) + bsb_ref[g]
                u, u_grad = _gelu_and_grad(pre_ref[rows, cols])
                dyv = dy_ref[rows, cols]
                du = dyv * sv * u_grad
                dpre_ref[rows, cols] = du.astype(BF16)
                dbin_ref[:, cols] += _colsum(du)
                dsv = dyv * u
                dbs = dbs + jnp.sum(dsv, axis=1, keepdims=True)
                dsv16 = dsv.astype(BF16)
                dws = dws + lax.dot_general(dsv16, vl_g, (((1,), (1,)), ((), ())), preferred_element_type=F32)
                dvl_s[rows, cols] = lax.dot_general(w, dsv16, (((0,), (0,)), ((), ())), preferred_element_type=F32)
            dws_ref[g] += jnp.where(keep, dws, 0.0)
            dbsb_ref[g] += dbs
        dvl = dvl_s[...]
        dlvg_ref[...] += _colsum(dvl * vhat)
        dlvb_ref[...] += _colsum(dvl)
        dxh = dvl * lvg_ref[...]
        dv = rstd * (dxh - jnp.mean(dxh, axis=-1, keepdims=True) - vhat * jnp.mean(dxh * vhat, axis=-1, keepdims=True))
        dpv = dv * v_grad
        dpre_ref[:, C_DIM:2 * C_DIM] = dpv.astype(BF16)
        dbin_ref[:, C_DIM:2 * C_DIM] += _colsum(dpv)

    group = pl.BlockSpec((C_GROUPS, CHUNK, CHUNK), lambda i: (0, 0, 0))
    return _call(
        body, name="sgu_bwd", grid=(t // tt,),
        in_specs=[pl.BlockSpec((tt, 2 * C_DIM), lambda i: (i, 0)), pl.BlockSpec((tt, C_DIM), lambda i: (i, 0)),
                  _vec_spec(1, C_DIM), _vec_spec(1, C_DIM), group, group],
        out_specs=[pl.BlockSpec((tt, 2 * C_DIM), lambda i: (i, 0)), group, group,
                   _vec_spec(1, C_DIM), _vec_spec(1, C_DIM), _vec_spec(1, 2 * C_DIM)],
        out_shape=[jax.ShapeDtypeStruct((t, 2 * C_DIM), BF16), jax.ShapeDtypeStruct((C_GROUPS, CHUNK, CHUNK), F32),
                   jax.ShapeDtypeStruct((C_GROUPS, CHUNK, CHUNK), F32), jax.ShapeDtypeStruct((1, C_DIM), F32),
                   jax.ShapeDtypeStruct((1, C_DIM), F32), jax.ShapeDtypeStruct((1, 2 * C_DIM), F32)],
        scratch_shapes=[pltpu.VMEM((tt, C_DIM), F32)],
        args=(pre, dy, lvg, lvb, ws, bs_b), comm=comm)


def _pair_sum(name, part, got, core):
    _, k, n = part.shape
    tk = _tile(k, 1024)

    def body(core_ref, p_ref, s_ref, o_ref):
        o_ref[...] = (p_ref[...] + s_ref[...]).astype(BF16)

    return pl.pallas_call(
        body, name=name,
        grid_spec=pltpu.PrefetchScalarGridSpec(
            num_scalar_prefetch=1, grid=(N_CHIP, k // tk),
            in_specs=[pl.BlockSpec((None, tk, n), lambda q, i, core_ref: (2 * q + core_ref[0], i, 0)),
                      pl.BlockSpec((None, tk, n), lambda q, i, core_ref: (q, i, 0))],
            out_specs=pl.BlockSpec((None, tk, n), lambda q, i, core_ref: (q, i, 0))),
        out_shape=jax.ShapeDtypeStruct((N_CHIP, k, n), BF16),
        compiler_params=pltpu.CompilerParams(dimension_semantics=("parallel", "parallel"), vmem_limit_bytes=VMEM_LIMIT_BYTES),
    )(core, part, got)


def _adamw_math(w, g, m, v):
    m = ADAM_B1 * m + (1.0 - ADAM_B1) * g
    v = ADAM_B2 * v + (1.0 - ADAM_B2) * (g * g)
    m_hat = m / (1.0 - ADAM_B1 ** ADAM_STEP)
    v_hat = v / (1.0 - ADAM_B2 ** ADAM_STEP)
    delta = -ADAM_LR * (m_hat / (jnp.sqrt(v_hat) + ADAM_EPS) + ADAM_WD * w)
    return delta, m, v


def _sum_adamw(name, parts, w, m, v, comm=None):
    layers = len(parts)
    n_parts, k, n = parts[0].shape
    tk = _tile(k, 256)

    def body(*refs):
        p_refs = refs[:layers]
        w_ref, m_ref, v_ref, g_ref, d_ref, nm_ref, nv_ref = refs[layers:]

        def total(p_ref):
            g = p_ref[0].astype(F32)
            for q in range(1, n_parts):
                g = g + p_ref[q].astype(F32)
            return g

        g = total(p_refs[0])
        for l in range(1, layers):
            g = jnp.where(pl.program_id(0) == l, total(p_refs[l]), g)
        g_ref[...] = g
        d_ref[...], nm_ref[...], nv_ref[...] = _adamw_math(w_ref[...], g, m_ref[...], v_ref[...])

    blk = pl.BlockSpec((None, tk, n), lambda l, i: (l, i, 0))
    return _call(
        body, name=name, grid=(layers, k // tk),
        in_specs=[pl.BlockSpec((n_parts, tk, n), lambda l, i: (0, i, 0))] * layers + [blk, blk, blk], out_specs=[blk] * 4,
        out_shape=[jax.ShapeDtypeStruct((layers, k, n), F32)] * 4, args=(*parts, w, m, v), parallel=True, comm=comm)


def _small_adamw(name, parts, w, m, v, losses):
    count = len(parts)

    def in_order(ref):
        total = ref[0]
        for dev in range(1, N_DEV):
            total = total + ref[dev]
        return total

    def body(*refs):
        p_refs, w_refs, m_refs, v_refs = (refs[j * count:(j + 1) * count] for j in range(4))
        losses_ref = refs[4 * count]
        g_refs, d_refs, nm_refs, nv_refs = (refs[4 * count + 1 + j * count:4 * count + 1 + (j + 1) * count] for j in range(4))
        loss_ref = refs[8 * count + 1]
        for i in range(count):
            g = in_order(p_refs[i])
            g_refs[i][...] = g
            d_refs[i][...], nm_refs[i][...], nv_refs[i][...] = _adamw_math(w_refs[i][...], g, m_refs[i][...], v_refs[i][...])
        loss_ref[...] = in_order(losses_ref)

    res = pl.pallas_call(
        body, name=name,
        out_shape=[jax.ShapeDtypeStruct(a.shape, F32) for a in w] * 4 + [jax.ShapeDtypeStruct(losses.shape[1:], F32)],
        compiler_params=pltpu.CompilerParams(vmem_limit_bytes=VMEM_LIMIT_BYTES))(*parts, *w, *m, *v, losses)
    return [res[j * count:(j + 1) * count] for j in range(4)], res[4 * count]


def _rows(a):
    return a.reshape(-1, a.shape[-1])


def _whole(gathered):
    return jnp.transpose(gathered, (1, 0, 2)).reshape(gathered.shape[1], -1)


SMALL =("ev_norm_g", "ev_conv_a_w", "ev_conv_a_b", "ev_ln_a_g", "ev_ln_a_b", "ev_conv_b_w", "od_norm_g", "od_b_in",
         "od_ln_v_g", "od_ln_v_b", "od_w_s", "od_b_s", "mlp_norm_g", "final_norm_g")
SMALL_SHARDED = ("ev_conv_a_w", "ev_conv_b_w", "od_norm_g", "od_b_in", "od_ln_v_g", "od_ln_v_b")
ORDER = ("ev_norm_g", "ev_w_in", "ev_conv_a_w", "ev_conv_a_b", "ev_ln_a_g", "ev_ln_a_b", "ev_conv_b_w", "ev_w_out",
         "od_norm_g", "od_w_in", "od_b_in", "od_ln_v_g", "od_ln_v_b", "od_w_s", "od_b_s", "od_w_out", "mlp_norm_g",
         "mlp_w1", "mlp_w2", "final_norm_g")


def kernel(x, ev_norm_g, ev_w_in, ev_conv_a_w, ev_conv_a_b, ev_ln_a_g, ev_ln_a_b, ev_conv_b_w, ev_w_out, od_norm_g, od_w_in, od_b_in, od_ln_v_g, od_ln_v_b, od_w_s, od_b_s, od_w_out, mlp_norm_g, mlp_w1, mlp_w2, final_norm_g, loss_target, m_ev_norm_g, m_ev_w_in, m_ev_conv_a_w, m_ev_conv_a_b, m_ev_ln_a_g, m_ev_ln_a_b, m_ev_conv_b_w, m_ev_w_out, m_od_norm_g, m_od_w_in, m_od_b_in, m_od_ln_v_g, m_od_ln_v_b, m_od_w_s, m_od_b_s, m_od_w_out, m_mlp_norm_g, m_mlp_w1, m_mlp_w2, m_final_norm_g, v_ev_norm_g, v_ev_w_in, v_ev_conv_a_w, v_ev_conv_a_b, v_ev_ln_a_g, v_ev_ln_a_b, v_ev_conv_b_w, v_ev_w_out, v_od_norm_g, v_od_w_in, v_od_b_in, v_od_ln_v_g, v_od_ln_v_b, v_od_w_s, v_od_b_s, v_od_w_out, v_mlp_norm_g, v_mlp_w1, v_mlp_w2, v_final_norm_g):
    W = dict(ev_norm_g=ev_norm_g, ev_w_in=ev_w_in, ev_conv_a_w=ev_conv_a_w, ev_conv_a_b=ev_conv_a_b, ev_ln_a_g=ev_ln_a_g,
             ev_ln_a_b=ev_ln_a_b, ev_conv_b_w=ev_conv_b_w, ev_w_out=ev_w_out, od_norm_g=od_norm_g, od_w_in=od_w_in,
             od_b_in=od_b_in, od_ln_v_g=od_ln_v_g, od_ln_v_b=od_ln_v_b, od_w_s=od_w_s, od_b_s=od_b_s, od_w_out=od_w_out,
             mlp_norm_g=mlp_norm_g, mlp_w1=mlp_w1, mlp_w2=mlp_w2, final_norm_g=final_norm_g)
    M = dict(ev_norm_g=m_ev_norm_g, ev_w_in=m_ev_w_in, ev_conv_a_w=m_ev_conv_a_w, ev_conv_a_b=m_ev_conv_a_b,
             ev_ln_a_g=m_ev_ln_a_g, ev_ln_a_b=m_ev_ln_a_b, ev_conv_b_w=m_ev_conv_b_w, ev_w_out=m_ev_w_out,
             od_norm_g=m_od_norm_g, od_w_in=m_od_w_in, od_b_in=m_od_b_in, od_ln_v_g=m_od_ln_v_g, od_ln_v_b=m_od_ln_v_b,
             od_w_s=m_od_w_s, od_b_s=m_od_b_s, od_w_out=m_od_w_out, mlp_norm_g=m_mlp_norm_g, mlp_w1=m_mlp_w1,
             mlp_w2=m_mlp_w2, final_norm_g=m_final_norm_g)
    V = dict(ev_norm_g=v_ev_norm_g, ev_w_in=v_ev_w_in, ev_conv_a_w=v_ev_conv_a_w, ev_conv_a_b=v_ev_conv_a_b,
             ev_ln_a_g=v_ev_ln_a_g, ev_ln_a_b=v_ev_ln_a_b, ev_conv_b_w=v_ev_conv_b_w, ev_w_out=v_ev_w_out,
             od_norm_g=v_od_norm_g, od_w_in=v_od_w_in, od_b_in=v_od_b_in, od_ln_v_g=v_od_ln_v_g, od_ln_v_b=v_od_ln_v_b,
             od_w_s=v_od_w_s, od_b_s=v_od_b_s, od_w_out=v_od_w_out, mlp_norm_g=v_mlp_norm_g, mlp_w1=v_mlp_w1,
             mlp_w2=v_mlp_w2, final_norm_g=v_final_norm_g)

    n_seq, seq, d = x.shape
    t = n_seq * seq
    dev = 4 * lax.axis_index("x") + 2 * lax.axis_index("y") + lax.axis_index("c")
    core = lax.axis_index("c").astype(jnp.int32).reshape(1)

    ev_g, cab, lag, lab = W["ev_norm_g"], W["ev_conv_a_b"], W["ev_ln_a_g"], W["ev_ln_a_b"]
    ws = W["od_w_s"][0]
    bs_b = jnp.broadcast_to(W["od_b_s"][0][:, :, None], (C_GROUPS, CHUNK, CHUNK))
    mlp_g = [W["mlp_norm_g"][l:l + 1] for l in range(2)]
    fin_g = W["final_norm_g"].reshape(1, d)

    def w16(name, l=0):
        return W[name][l].astype(BF16)

    h0 = x.reshape(t, d)
    gather = _gather_comm([w16("ev_w_in")] + [_rows(W[n]) for n in SMALL_SHARDED])
    n0 = _rms_fwd("ev_norm", h0, ev_g, comm=gather)
    w_ev_in, caw, cbw, od_g, od_bin, lvg, lvb = [_whole(g) for g in gather.out]

    first, second = (0, 2), (1, 2)
    w1_0, w2_0, w1_1, w2_1 = w16("mlp_w1", 0), w16("mlp_w2", 0), w16("mlp_w1", 1), w16("mlp_w2", 1)

    g_a, g_b = _gather_comm([w16("ev_w_out")]), _gather_comm([w1_0], first)
    z = _mm_nn("ev_in", n0, w_ev_in, IN_EVEN, _ep_store, [F32], comm=[g_a, g_b])[0]
    w_ev_out = g_a.out[0].reshape(D_MODEL, D_MODEL)

    g_c, g_d = _gather_comm([w1_0], second, into=g_b.out), _gather_comm([w2_0], first)
    mix, a2 = _mixer_fwd(z, seq, caw, cab, lag, lab, cbw, comm=[g_c, g_d])
    w1 = [_whole(g_c.out[0]), None]

    w_od_in16 = w16("od_w_in")
    g_e = _gather_comm([w_od_in16], first)
    h1, n1 = _mm_nn("ev_out", mix, w_ev_out, d, _ep_residual_norm, [F32, BF16], extras=(h0,), rows=(mlp_g[0],), comm=g_e)

    g_f, g_f2 = _gather_comm([w2_0], second, into=g_d.out), _gather_comm([w_od_in16], second, into=g_e.out)
    q0 = _mm_nn("mlp0_up", n1, w1[0], D_FF, _ep_relu_sq, [BF16], comm=[g_f, g_f2])[0]
    w2 = [g_f.out[0].reshape(D_FF, D_MODEL), None]
    w_od_in = _whole(g_f2.out[0])

    g_g, g_g2 = _gather_comm([w1_1], first), _gather_comm([w16("od_w_out")])
    h2, n2 = _mm_nn("mlp0_down", q0, w2[0], d, _ep_residual_norm, [F32, BF16], extras=(h1,), rows=(od_g,), comm=[g_g, g_g2])
    w_od_out = g_g2.out[0].reshape(D_MODEL, D_MODEL)
    g_h = _gather_comm([w1_1], second, into=g_g.out)
    pre = _mm_nn("od_in", n2, w_od_in, 2 * C_DIM, _ep_bias, [F32], rows=(od_bin,), comm=g_h)[0]
    w1[1] = _whole(g_h.out[0])
    g_i = _gather_comm([w2_1], (0, 4))
    y = _sgu_fwd(pre, lvg, lvb, ws, bs_b, comm=g_i)
    g_i2 = _gather_comm([w2_1], (1, 4), into=g_i.out)
    h3, n3 = _mm_nn("od_out", y, w_od_out, d, _ep_residual_norm, [F32, BF16], extras=(h2,), rows=(mlp_g[1],), comm=g_i2)
    g_j = _gather_comm([w2_1], second, into=g_i2.out)
    q1 = _mm_nn("mlp1_up", n3, w1[1], D_FF, _ep_relu_sq, [BF16], comm=g_j)[0]
    w2[1] = g_j.out[0].reshape(D_FF, D_MODEL)
    grad, grad16, d_fin_g, loss_part = _mm_nn(
        "mlp1_down", q1, w2[1], d, _ep_final_loss, [F32, BF16], extras=(h3, loss_target.reshape(t, d)), rows=(fin_g,),
        sums=(True, False), tm_want=FUSED_ROWS)

    by_chip = {}

    def swap(name, parts):
        comm = _pair_comm([parts])
        comm.parts, comm.weight = parts, name
        return comm

    def exchange(swapped, halves=False):
        sums = _pair_sum(f"pair_sum_{swapped.weight}", swapped.parts, swapped.out[0], core)
        if not halves:
            comm = _chip_comm([sums])
            comm.weight = swapped.weight
            return comm
        comm = _chip_comm([sums], first)
        comm.sums, comm.weight = sums, swapped.weight
        return comm

    def rest(comm):
        other = _chip_comm([comm.sums], second, into=comm.out)
        other.weight = comm.weight
        return other

    def done(comm):
        by_chip[comm.weight] = comm.out[0]

    dw2_1 = _mm_tn("mlp1_dw2", q1, grad16).reshape(N_DEV, D_FF // N_DEV, D_MODEL)
    s_a = swap("w2_1", dw2_1)
    dp = _mm_nt("mlp1_dq", grad16, w2[1], D_FF, _ep_relu_sq_grad, [BF16], extras=(q1,), comm=s_a)[0]
    c_a = exchange(s_a, halves=True)
    dw1_1 = _mm_tn("mlp1_dw1", n3, dp, col_blocks=N_DEV, comm=c_a)
    c_a2, s_b = rest(c_a), swap("w1_1", dw1_1)
    dn = _mm_nt("mlp1_dn", dp, w1[1], d, _ep_store, [F32], comm=[c_a2, s_b])[0]
    done(c_a2)
    c_b = exchange(s_b, halves=True)
    grad, grad16, dg_mlp1 = _rms_bwd("mlp1_dn_norm", dn, h3, mlp_g[1], grad)
    d_od_out = _mm_tn("od_dw_out", y, grad16).reshape(N_DEV, D_MODEL // N_DEV, D_MODEL)
    s_c = swap("od_out", d_od_out)
    dy = _mm_nt("od_dy", grad16, w_od_out, C_DIM, _ep_store, [F32], comm=s_c)[0]
    c_c = exchange(s_c)
    dpre, d_ws, d_bsb, d_lvg, d_lvb, d_bin = _sgu_bwd(pre, dy, lvg, lvb, ws, bs_b, comm=[c_b, c_c])
    done(c_c)
    c_b2 = rest(c_b)
    d_od_in = _mm_tn("od_dw_in", n2, dpre, col_blocks=N_DEV, comm=c_b2)
    done(c_b2)
    s_d = swap("od_in", d_od_in)
    dn = _mm_nt("od_dn", dpre, w_od_in, d, _ep_store, [F32], comm=s_d)[0]
    c_d = exchange(s_d)
    grad, grad16, d_od_g = _rms_bwd("od_dn_norm", dn, h2, od_g, grad)
    group_1 = dict(od_norm_g=d_od_g, od_b_in=d_bin, od_ln_v_g=d_lvg, od_ln_v_b=d_lvb,
                   od_w_s=d_ws.reshape(C_GROUPS * CHUNK, CHUNK), od_b_s=d_bsb[:, :, 0], final_norm_g=d_fin_g,
                   mlp_norm_g_1=dg_mlp1, loss=jnp.broadcast_to(loss_part, (SUBLANES, LANE_BLOCK)))
    gather_1 = _gather_comm(list(group_1.values()))
    dw2_0 = _mm_tn("mlp0_dw2", q0, grad16, comm=[c_d, gather_1]).reshape(N_DEV, D_FF // N_DEV, D_MODEL)
    done(c_d)
    s_e = swap("w2_0", dw2_0)
    dp = _mm_nt("mlp0_dq", grad16, w2[0], D_FF, _ep_relu_sq_grad, [BF16], extras=(q0,), comm=s_e)[0]
    c_e = exchange(s_e, halves=True)
    dw1_0 = _mm_tn("mlp0_dw1", n1, dp, col_blocks=N_DEV, comm=c_e)
    c_e2, s_f = rest(c_e), swap("w1_0", dw1_0)
    dn = _mm_nt("mlp0_dn", dp, w1[0], d, _ep_store, [F32], comm=[c_e2, s_f])[0]
    done(c_e2)
    c_f = exchange(s_f, halves=True)
    grad, grad16, dg_mlp0 = _rms_bwd("mlp0_dn_norm", dn, h1, mlp_g[0], grad)
    d_ev_out = _mm_tn("ev_dw_out", mix, grad16).reshape(N_DEV, D_MODEL // N_DEV, D_MODEL)
    s_g = swap("ev_out", d_ev_out)
    dmix = _mm_nt("ev_dmix", grad16, w_ev_out, A_DIM + B_DIM, _ep_store, [F32], comm=s_g)[0]
    c_g = exchange(s_g)
    da2, dcv, d_caw, d_cab, d_lag, d_lab, d_cbw = _mixer_bwd_local(z, a2, dmix, seq, lag, lab, comm=[c_f, c_g])
    done(c_g)
    c_f2 = rest(c_f)
    dz = _mixer_bwd_input(z, dmix, da2, dcv, seq, caw, cbw, comm=c_f2)
    done(c_f2)
    group_2 = dict(ev_conv_a_w=d_caw, ev_conv_a_b=d_cab, ev_ln_a_g=d_lag, ev_ln_a_b=d_lab, ev_conv_b_w=d_cbw,
                   mlp_norm_g_0=dg_mlp0)
    gather_2 = _gather_comm(list(group_2.values()))
    d_ev_in = _mm_tn("ev_dw_in", n0, dz, comm=gather_2)
    d_ev_in = jnp.transpose(d_ev_in.reshape(D_MODEL, N_DEV, IN_EVEN // N_DEV), (1, 0, 2))
    s_h = swap("ev_in", d_ev_in)
    dn = _mm_nt("ev_dn", dz, w_ev_in, d, _ep_store, [F32], comm=s_h)[0]
    c_h = exchange(s_h)
    grad_x, _, d_ev_g = _rms_bwd("ev_dn_norm", dn, h0, ev_g, grad, comm=c_h)
    done(c_h)

    last_gather = _gather_comm([d_ev_g])
    shard = {"od_w_out": ("od_out",), "mlp_w1": ("w1_0", "w1_1"), "mlp_w2": ("w2_0", "w2_1"), "od_w_in": ("od_in",),
             "ev_w_out": ("ev_out",), "ev_w_in": ("ev_in",)}
    carried = {"od_w_out": last_gather}
    out_g, out_d, out_m, out_v = {}, {}, {}, {}
    for name, keys in shard.items():
        out_g[name], out_d[name], out_m[name], out_v[name] = _sum_adamw(
            f"adamw_{name}", [by_chip[key] for key in keys], W[name], M[name], V[name], comm=carried.get(name))

    gathered = dict(zip(group_1, gather_1.out), **dict(zip(group_2, gather_2.out)), ev_norm_g=last_gather.out[0])
    gathered["mlp_norm_g"] = jnp.concatenate([gathered["mlp_norm_g_0"], gathered["mlp_norm_g_1"]], axis=1)
    mine = []
    for n in SMALL:
        g = gathered[n]
        if n in SMALL_SHARDED:
            width = W[n].shape[-1]
            g = lax.dynamic_slice_in_dim(g, dev * width, width, axis=2)
        mine.append(g)
    res, loss_tile = _small_adamw("adamw_small", mine, [_rows(W[n]) for n in SMALL], [_rows(M[n]) for n in SMALL],
                                  [_rows(V[n]) for n in SMALL], gathered["loss"])
    loss = loss_tile[0, 0]
    for store, values in zip((out_g, out_d, out_m, out_v), res):
        for n, value in zip(SMALL, values):
            store[n] = value.reshape(W[n].shape)

    return (loss, grad_x.reshape(n_seq, seq, d), *[out_g[n] for n in ORDER], *[out_d[n] for n in ORDER],
            *[out_m[n] for n in ORDER], *[out_v[n] for n in ORDER])
```

```python
import math

import jax
import jax.numpy as jnp
from jax import lax
from jax.experimental import pallas as pl
from jax.experimental.pallas import tpu as pltpu

F32 = jnp.float32
BF16 = jnp.bfloat16
MESH = pl.DeviceIdType.MESH

D_MODEL = 1024
A_DIM = 512
B_DIM = 512
IN_EVEN = 2 * A_DIM + 3 * B_DIM
A_TAPS = 31
B_TAPS = 3
CHUNK = 128
C_GROUPS = 8
C_DIM = 1024
D_FF = 4096
RMS_EPS = 1e-6
LN_EPS = 1e-5
N_DEV = 8
N_CHIP = 4

ADAM_LR = 0.001
ADAM_B1 = 0.9
ADAM_B2 = 0.999
ADAM_EPS = 1e-08
ADAM_WD = 0.01
ADAM_STEP = 10

A_HALO = 32
B_HALO = 8
VMEM_LIMIT_BYTES = 56 * 1024 * 1024
TINY = 1.1754944e-38
INV_SQRT2 = 1.0 / math.sqrt(2.0)
INV_SQRT_2PI = 1.0 / math.sqrt(2.0 * math.pi)
HBM_SPEC = pl.BlockSpec(memory_space=pltpu.HBM)


def _tile(n, want):
    t = min(n, want)
    while n % t:
        t //= 2
    return t


def _sigmoid(x):
    return 1.0 / (1.0 + jnp.exp(-x))


def _gelu(x):
    return 0.5 * x * (1.0 + lax.erf(x * INV_SQRT2))


def _gelu_and_grad(x):
    cdf = 0.5 * (1.0 + lax.erf(x * INV_SQRT2))
    return x * cdf, cdf + x * jnp.exp(-0.5 * x * x) * INV_SQRT_2PI


def _colsum(x):
    return jnp.sum(x, axis=0, keepdims=True)


class _Comm:
    def __init__(self, ins, out_shapes, sem_shapes, start, finish, middle=None, into=None, sibling=False, chips=False):
        self.ins, self.out_shapes, self.sem_shapes, self.start, self.finish = ins, out_shapes, sem_shapes, start, finish
        self.middle = middle
        self.sibling, self.chips = sibling, chips
        self.into = list(into) if into is not None else []
        self.out = None


def _piece_rows(rows, piece):
    if piece is None:
        return 0, rows
    i, n = piece
    return i * (rows // n), rows // n


MIDDLE_AT = 0.75
BARRIER_IDS = {(True, True): 0, (True, False): 1, (False, True): 2}
FUSED_ROWS = 512


def _call(body, *, name, grid, in_specs, out_specs, out_shape, args, scratch_shapes=(), parallel=False, comm=None):
    comms = [] if comm is None else (list(comm) if isinstance(comm, (list, tuple)) else [comm])
    if not comms:
        sem = ("parallel" if parallel else "arbitrary",) * len(grid)
        return pl.pallas_call(
            body, name=name, grid=grid, in_specs=list(in_specs), out_specs=list(out_specs), out_shape=list(out_shape),
            scratch_shapes=list(scratch_shapes),
            compiler_params=pltpu.CompilerParams(dimension_semantics=sem, vmem_limit_bytes=VMEM_LIMIT_BYTES),
        )(*args)
    n_in, n_out, n_scr = len(in_specs), len(out_shape), len(scratch_shapes)
    c_ins_all = [a for cm in comms for a in cm.ins]
    c_into_all = [a for cm in comms for a in cm.into]
    c_out_shapes = [s for cm in comms for s in cm.out_shapes]
    c_sem_shapes = [s for cm in comms for s in cm.sem_shapes]
    aliases, in_pos, out_pos = {}, n_in + len(c_ins_all), n_out
    for cm in comms:
        for j in range(len(cm.into)):
            aliases[in_pos + j] = out_pos + j
        in_pos += len(cm.into)
        out_pos += len(cm.out_shapes)
    to_sibling = any(cm.sibling for cm in comms)
    to_chips = any(cm.chips for cm in comms)
    steps = grid
    total = math.prod(steps)
    first_step = (0,) * len(steps)
    last_step = tuple(s - 1 for s in steps)
    middle_step = None
    if 0 < int(MIDDLE_AT * total) < total - 1:
        rest, idx = int(MIDDLE_AT * total), []
        for s in reversed(steps):
            idx.append(rest % s)
            rest //= s
        middle_step = tuple(reversed(idx))

    def carrying(*refs):
        pos = 0
        ins = refs[pos:pos + n_in]; pos += n_in
        c_ins = refs[pos:pos + len(c_ins_all)]; pos += len(c_ins_all) + len(c_into_all)
        outs = refs[pos:pos + n_out]; pos += n_out
        c_outs = refs[pos:pos + len(c_out_shapes)]; pos += len(c_out_shapes)
        scr = refs[pos:pos + n_scr]; pos += n_scr
        c_sems = refs[pos:]
        views, i0, o0, s0 = [], 0, 0, 0
        for cm in comms:
            views.append((c_ins[i0:i0 + len(cm.ins)], c_outs[o0:o0 + len(cm.out_shapes)], c_sems[s0:s0 + len(cm.sem_shapes)]))
            i0, o0, s0 = i0 + len(cm.ins), o0 + len(cm.out_shapes), s0 + len(cm.sem_shapes)

        def at(step):
            hit = pl.program_id(0) == step[0]
            for axis in range(1, len(steps)):
                hit = jnp.logical_and(hit, pl.program_id(axis) == step[axis])
            return hit

        @pl.when(at(first_step))
        def _():
            x, y, c, chips = _place()
            peers = ([(x, y, 1 - c)] if to_sibling else []) + ([(*chip, c) for chip in chips] if to_chips else [])
            barrier = pltpu.get_barrier_semaphore()
            for peer in peers:
                pl.semaphore_signal(barrier, inc=1, device_id=peer, device_id_type=MESH)
            pl.semaphore_wait(barrier, len(peers))
            for cm, view in zip(comms, views):
                cm.start(*view)

        if middle_step is not None:
            @pl.when(at(middle_step))
            def _():
                for cm, view in zip(comms, views):
                    if cm.middle is not None:
                        cm.middle(*view)

        body(*ins, *outs, *scr)

        @pl.when(at(last_step))
        def _():
            for cm, view in zip(comms, views):
                if cm.middle is not None and middle_step is None:
                    cm.middle(*view)
            for cm, view in zip(comms, views):
                cm.finish(*view)

    res = pl.pallas_call(
        carrying, name=name, grid=grid,
        in_specs=[*in_specs, *[HBM_SPEC] * (len(c_ins_all) + len(c_into_all))],
        out_specs=[*out_specs, *[HBM_SPEC] * len(c_out_shapes)],
        out_shape=[*out_shape, *c_out_shapes], scratch_shapes=[*scratch_shapes, *c_sem_shapes],
        input_output_aliases=aliases,
        compiler_params=pltpu.CompilerParams(dimension_semantics=("arbitrary",) * len(grid), vmem_limit_bytes=VMEM_LIMIT_BYTES,
                                             collective_id=BARRIER_IDS[to_sibling, to_chips]),
    )(*args, *c_ins_all, *c_into_all)
    pos = n_out
    for cm in comms:
        cm.out = list(res[pos:pos + len(cm.out_shapes)])
        pos += len(cm.out_shapes)
    return list(res[:n_out])


def _place():
    x, y, c = lax.axis_index("x"), lax.axis_index("y"), lax.axis_index("c")
    return x, y, c, [(1 - x, y), (x, 1 - y), (1 - x, 1 - y)]


def _gather_comm(shards, piece=None, into=None):
    nw = len(shards)
    spans = [_piece_rows(s.shape[0], piece) for s in shards]

    def plan(ins, outs, sems):
        send_sems, recv_sems, local_sems = sems
        x, y, c, chips = _place()
        me, sibling = (x, y, c), (x, y, 1 - c)

        def slot(w, p):
            return outs[w].at[4 * p[0] + 2 * p[1] + p[2], pl.ds(*spans[w])]

        def mine(w):
            return ins[w].at[pl.ds(*spans[w])]

        def copy(w, k, block, to, src=None):
            return pltpu.make_async_remote_copy(
                src_ref=slot(w, block) if src is None else src, dst_ref=slot(w, block),
                send_sem=send_sems.at[w, k], recv_sem=recv_sems.at[w, k], device_id=to, device_id_type=MESH)

        local = [pltpu.make_async_copy(mine(w), slot(w, me), local_sems.at[w]) for w in range(nw)]
        first = [[copy(w, 0, me, sibling, src=mine(w))] + [copy(w, 1 + j, me, (*chip, c), src=mine(w)) for j, chip in enumerate(chips)]
                 for w in range(nw)]
        landed = [[copy(w, 1 + j, (*chip, c), me) for j, chip in enumerate(chips)] for w in range(nw)]
        passed = [[copy(w, 4 + j, (*chip, c), sibling) for j, chip in enumerate(chips)] for w in range(nw)]
        from_sibling = [[copy(w, 0, sibling, me)] + [copy(w, 4 + j, (*chip, 1 - c), me) for j, chip in enumerate(chips)]
                        for w in range(nw)]
        return local, first, landed, passed, from_sibling

    def start(ins, outs, sems):
        local, first, _, _, _ = plan(ins, outs, sems)
        for cp in local:
            cp.start()
        for row in first:
            for cp in row:
                cp.start()

    def middle(ins, outs, sems):
        _, _, landed, passed, _ = plan(ins, outs, sems)
        for w in range(nw):
            for j in range(3):
                landed[w][j].wait_recv()
                passed[w][j].start()

    def finish(ins, outs, sems):
        local, first, landed, passed, from_sibling = plan(ins, outs, sems)
        for w in range(nw):
            for cp in from_sibling[w]:
                cp.wait_recv()
        for w in range(nw):
            for cp in first[w] + passed[w]:
                cp.wait_send()
        for cp in local:
            cp.wait()

    return _Comm(list(shards), [jax.ShapeDtypeStruct((N_DEV, *s.shape), s.dtype) for s in shards],
                 [pltpu.SemaphoreType.DMA((nw, 7)), pltpu.SemaphoreType.DMA((nw, 7)), pltpu.SemaphoreType.DMA((nw,))],
                 start, finish, middle, into=into, sibling=True, chips=True)


def _pair_comm(parts):
    nw = len(parts)

    def plan(ins, outs, sems):
        send_sems, recv_sems = sems
        x, y, c, _ = _place()
        return [pltpu.make_async_remote_copy(
            src_ref=ins[w].at[2 * q + 1 - c], dst_ref=outs[w].at[q], send_sem=send_sems.at[w, q], recv_sem=recv_sems.at[w, q],
            device_id=(x, y, 1 - c), device_id_type=MESH) for w in range(nw) for q in range(N_CHIP)]

    def start(ins, outs, sems):
        for cp in plan(ins, outs, sems):
            cp.start()

    def finish(ins, outs, sems):
        for cp in plan(ins, outs, sems):
            cp.wait()

    return _Comm(list(parts), [jax.ShapeDtypeStruct((N_CHIP, *p.shape[1:]), p.dtype) for p in parts],
                 [pltpu.SemaphoreType.DMA((nw, N_CHIP)), pltpu.SemaphoreType.DMA((nw, N_CHIP))], start, finish, sibling=True)


def _chip_comm(sums, piece=None, into=None):
    nw = len(sums)
    spans = [_piece_rows(s.shape[1], piece) for s in sums]

    def plan(ins, outs, sems):
        send_sems, recv_sems, local_sems = sems
        x, y, c, chips = _place()
        my_chip = 2 * x + y
        local = [pltpu.make_async_copy(ins[w].at[my_chip, pl.ds(*spans[w])], outs[w].at[my_chip, pl.ds(*spans[w])],
                                       local_sems.at[w]) for w in range(nw)]
        remote = [pltpu.make_async_remote_copy(
            src_ref=ins[w].at[2 * chip[0] + chip[1], pl.ds(*spans[w])], dst_ref=outs[w].at[my_chip, pl.ds(*spans[w])],
            send_sem=send_sems.at[w, j], recv_sem=recv_sems.at[w, j], device_id=(*chip, c), device_id_type=MESH)
            for w in range(nw) for j, chip in enumerate(chips)]
        return local, remote

    def start(ins, outs, sems):
        local, remote = plan(ins, outs, sems)
        for cp in local + remote:
            cp.start()

    def finish(ins, outs, sems):
        local, remote = plan(ins, outs, sems)
        for cp in remote + local:
            cp.wait()

    return _Comm(list(sums), [jax.ShapeDtypeStruct(s.shape, s.dtype) for s in sums],
                 [pltpu.SemaphoreType.DMA((nw, 3)), pltpu.SemaphoreType.DMA((nw, 3)), pltpu.SemaphoreType.DMA((nw,))],
                 start, finish, into=into, chips=True)


def _matmul(name, a, b, *, kind, m, n, k, a_spec, b_spec, tm, tn, tk, out_shape, out_specs, epilogue,
            extras=(), extra_specs=(), comm=None):
    dims = {"nn": (((1,), (0,)), ((), ())), "nt": (((1,), (1,)), ((), ())), "tn": (((0,), (0,)), ((), ()))}[kind]
    nk = k // tk
    n_extra = len(extras)
    n_out = len(out_shape)

    def body(a_ref, b_ref, *rest):
        extra_refs = rest[:n_extra]
        out_refs = rest[n_extra:n_extra + n_out]
        part = lax.dot_general(a_ref[...], b_ref[...], dims, preferred_element_type=F32)
        if nk == 1:
            epilogue(part, extra_refs, out_refs)
            return
        acc_ref = rest[n_extra + n_out]
        step = pl.program_id(2)

        @pl.when(step == 0)
        def _():
            acc_ref[...] = part

        @pl.when(jnp.logical_and(step > 0, step < nk - 1))
        def _():
            acc_ref[...] += part

        @pl.when(step == nk - 1)
        def _():
            epilogue(acc_ref[...] + part, extra_refs, out_refs)

    return _call(
        body, name=name, grid=(m // tm, n // tn, nk), in_specs=[a_spec, b_spec, *extra_specs], out_specs=out_specs,
        out_shape=out_shape, scratch_shapes=[pltpu.VMEM((tm, tn), F32)] if nk > 1 else [], args=(a, b, *extras), comm=comm)


def _mm_operands(m, n, tm, tn, out_dtypes, extras, rows, sums):
    tile = pl.BlockSpec((tm, tn), lambda i, j, kk: (i, j))
    row = pl.BlockSpec((1, tn), lambda i, j, kk: (0, j))
    one = pl.BlockSpec((1, 1), lambda i, j, kk: (0, 0))
    out_shape = [jax.ShapeDtypeStruct((m, n), dt) for dt in out_dtypes]
    out_shape += [jax.ShapeDtypeStruct((1, n if wide else 1), F32) for wide in sums]
    out_specs = [tile] * len(out_dtypes) + [row if wide else one for wide in sums]
    return (*extras, *rows), [tile] * len(extras) + [row] * len(rows), out_shape, out_specs


def _row_tile(m, k, tm_want):
    return _tile(m, tm_want or (1024 if k <= 1024 else 512))


def _mm_nn(name, a, b, n, epilogue, out_dtypes, *, extras=(), rows=(), sums=(), tm_want=None, comm=None):
    m, k = a.shape
    blocked = b.ndim == 3
    tm = _row_tile(m, k, tm_want)
    tn = b.shape[-1] if blocked else _tile(n, 1024)
    tk = k
    extras, extra_specs, out_shape, out_specs = _mm_operands(m, n, tm, tn, out_dtypes, extras, rows, sums)
    if blocked:
        b_spec = pl.BlockSpec((None, tk, tn), lambda i, j, kk: (j, kk, 0))
    else:
        b_spec = pl.BlockSpec((tk, tn), lambda i, j, kk: (kk, j))
    return _matmul(
        name, a, b, kind="nn", m=m, n=n, k=k, tm=tm, tn=tn, tk=tk,
        a_spec=pl.BlockSpec((tm, tk), lambda i, j, kk: (i, kk)), b_spec=b_spec, out_shape=out_shape, out_specs=out_specs,
        epilogue=epilogue, extras=extras, extra_specs=extra_specs, comm=comm)


def _mm_nt(name, a, b, n, epilogue, out_dtypes, *, extras=(), rows=(), sums=(), tm_want=None, comm=None):
    m, k = a.shape
    blocked = b.ndim == 3
    tm = _row_tile(m, k, tm_want)
    tn = _tile(n, 1024)
    tk = b.shape[-1] if blocked else k
    extras, extra_specs, out_shape, out_specs = _mm_operands(m, n, tm, tn, out_dtypes, extras, rows, sums)
    if blocked:
        b_spec = pl.BlockSpec((None, tn, tk), lambda i, j, kk: (kk, j, 0))
    else:
        b_spec = pl.BlockSpec((tn, tk), lambda i, j, kk: (j, kk))
    return _matmul(
        name, a, b, kind="nt", m=m, n=n, k=k, tm=tm, tn=tn, tk=tk,
        a_spec=pl.BlockSpec((tm, tk), lambda i, j, kk: (i, kk)), b_spec=b_spec, out_shape=out_shape, out_specs=out_specs,
        epilogue=epilogue, extras=extras, extra_specs=extra_specs, comm=comm)


def _mm_tn(name, a, b, *, col_blocks=0, comm=None):
    t, k = a.shape
    n = b.shape[1]
    tm = _row_tile(k, t, None)
    tn = n // col_blocks if col_blocks else _tile(n, 1024)
    tk = t
    if col_blocks:
        out_shape = [jax.ShapeDtypeStruct((col_blocks, k, tn), F32)]
        out_specs = [pl.BlockSpec((None, tm, tn), lambda i, j, kk: (j, i, 0))]
    else:
        out_shape = [jax.ShapeDtypeStruct((k, n), F32)]
        out_specs = [pl.BlockSpec((tm, tn), lambda i, j, kk: (i, j))]

    def epilogue(acc, extra_refs, out_refs):
        out_refs[0][...] = acc

    return _matmul(
        name, a, b, kind="tn", m=k, n=n, k=t, tm=tm, tn=tn, tk=tk,
        a_spec=pl.BlockSpec((tk, tm), lambda i, j, kk: (kk, i)), b_spec=pl.BlockSpec((tk, tn), lambda i, j, kk: (kk, j)),
        out_shape=out_shape, out_specs=out_specs, epilogue=epilogue, comm=comm)[0]


def _ep_store(acc, extra_refs, out_refs):
    out_refs[0][...] = acc.astype(out_refs[0].dtype)


def _ep_bias(acc, extra_refs, out_refs):
    out_refs[0][...] = acc + extra_refs[0][...]


def _rms(x):
    r = lax.rsqrt(jnp.mean(x * x, axis=-1, keepdims=True) + RMS_EPS)
    return r, x * r


def _rms_grad(dn, r, xr, g):
    dy = dn * g
    return r * (dy - xr * jnp.mean(dy * xr, axis=-1, keepdims=True))


def _ep_residual_norm(acc, extra_refs, out_refs):
    h = extra_refs[0][...] + acc
    out_refs[0][...] = h
    _, xr = _rms(h)
    out_refs[1][...] = (xr * extra_refs[1][...]).astype(BF16)


def _ep_final_loss(acc, extra_refs, out_refs):
    @pl.when(pl.program_id(0) == 0)
    def _():
        out_refs[2][...] = jnp.zeros_like(out_refs[2])
        out_refs[3][...] = jnp.zeros_like(out_refs[3])

    h = extra_refs[0][...] + acc
    g = extra_refs[2][...]
    r, xr = _rms(h)
    err = xr * g - extra_refs[1][...]
    out_refs[3][...] += 0.5 * jnp.sum(jnp.mean(err * err, axis=-1, keepdims=True), axis=0, keepdims=True)
    dout = err * (1.0 / h.shape[-1])
    out_refs[2][...] += _colsum(dout * xr)
    out = _rms_grad(dout, r, xr, g)
    out_refs[0][...] = out
    out_refs[1][...] = out.astype(BF16)


def _ep_relu_sq(acc, extra_refs, out_refs):
    r = jnp.maximum(acc, 0.0)
    out_refs[0][...] = (r * r).astype(BF16)


def _ep_relu_sq_grad(acc, extra_refs, out_refs):
    q = extra_refs[0][...].astype(F32)
    out_refs[0][...] = (acc * (2.0 * q * lax.rsqrt(jnp.maximum(q, TINY)))).astype(BF16)


def _rms_fwd(name, h, g, comm=None):
    t, d = h.shape
    tt = _tile(t, 512)

    def body(h_ref, g_ref, n_ref):
        x = h_ref[...]
        r = lax.rsqrt(jnp.mean(x * x, axis=-1, keepdims=True) + RMS_EPS)
        n_ref[...] = (x * r * g_ref[...]).astype(BF16)

    return _call(
        body, name=name, grid=(t // tt,),
        in_specs=[pl.BlockSpec((tt, d), lambda i: (i, 0)), pl.BlockSpec((1, d), lambda i: (0, 0))],
        out_specs=[pl.BlockSpec((tt, d), lambda i: (i, 0))], out_shape=[jax.ShapeDtypeStruct((t, d), BF16)],
        args=(h, g), parallel=True, comm=comm)[0]


def _rms_bwd(name, dn, h, g, grad_in, comm=None):
    t, d = h.shape
    tt = _tile(t, 512)

    def body(dn_ref, h_ref, g_ref, gin_ref, gout_ref, gout16_ref, dg_ref):
        @pl.when(pl.program_id(0) == 0)
        def _():
            dg_ref[...] = jnp.zeros_like(dg_ref)

        dnv = dn_ref[...]
        r, xr = _rms(h_ref[...])
        dg_ref[...] += _colsum(dnv * xr)
        out = gin_ref[...] + _rms_grad(dnv, r, xr, g_ref[...])
        gout_ref[...] = out
        gout16_ref[...] = out.astype(BF16)

    row = pl.BlockSpec((tt, d), lambda i: (i, 0))
    vec = pl.BlockSpec((1, d), lambda i: (0, 0))
    return _call(
        body, name=name, grid=(t // tt,), in_specs=[row, row, vec, row], out_specs=[row, row, vec],
        out_shape=[jax.ShapeDtypeStruct((t, d), F32), jax.ShapeDtypeStruct((t, d), BF16), jax.ShapeDtypeStruct((1, d), F32)],
        args=(dn, h, g, grad_in), comm=comm)


def _mixer_windows(z_ref, zh_ref, first, a1_s, cb_s, tt):
    sig = _sigmoid(z_ref[:, A_DIM:2 * A_DIM])
    a1_s[A_HALO:A_HALO + tt, :] = z_ref[:, 0:A_DIM] * sig
    a1_h = zh_ref[:, 0:A_DIM] * _sigmoid(zh_ref[:, A_DIM:2 * A_DIM])
    a1_s[0:A_HALO, :] = jnp.where(first, 0.0, a1_h)
    cb_s[B_HALO:B_HALO + tt, :] = z_ref[:, 3 * A_DIM:4 * A_DIM] * z_ref[:, 4 * A_DIM:5 * A_DIM]
    cb_h = zh_ref[A_HALO - B_HALO:A_HALO, 3 * A_DIM:4 * A_DIM] * zh_ref[A_HALO - B_HALO:A_HALO, 4 * A_DIM:5 * A_DIM]
    cb_s[0:B_HALO, :] = jnp.where(first, 0.0, cb_h)
    return sig


def _causal_conv(win_s, w_ref, taps, halo, tt):
    base = halo - (taps - 1)
    acc = w_ref[0:1, :] * win_s[pl.ds(base, tt), :]
    for k in range(1, taps):
        acc = acc + w_ref[k:k + 1, :] * win_s[pl.ds(base + k, tt), :]
    return acc


SUBLANES = 8
LANE_BLOCK = 128
ROW_BLOCK = 128
SHIFT_ROWS = A_HALO - SUBLANES


def _shifted_copies(win_s, sh_s, tt):
    for b in range(1, SUBLANES):
        sh_s[b - 1] = win_s[pl.ds(b, tt + SHIFT_ROWS), :]


def _window_rows(win_s, sh_s, offset, rows, cols):
    b = offset % SUBLANES
    if b == 0:
        return win_s[pl.ds(offset, rows), cols]
    return sh_s[b - 1, pl.ds(offset - b, rows), cols]


def _blocks(tt):
    rb = min(tt, ROW_BLOCK)
    return rb, [(r, slice(lb * LANE_BLOCK, (lb + 1) * LANE_BLOCK))
                for lb in range(A_DIM // LANE_BLOCK) for r in range(0, tt, rb)]


def _conv_taps(win_s, sh_s, w_ref, offsets, out_s, tt, bias_ref=None):
    rb, blocks = _blocks(tt)
    for r, cols in blocks:
        acc = w_ref[0:1, cols] * _window_rows(win_s, sh_s, r + offsets[0], rb, cols)
        for k in range(1, len(offsets)):
            acc = acc + w_ref[k:k + 1, cols] * _window_rows(win_s, sh_s, r + offsets[k], rb, cols)
        out_s[r:r + rb, cols] = acc if bias_ref is None else acc + bias_ref[:, cols]


A_CAUSAL = [A_HALO - (A_TAPS - 1) + k for k in range(A_TAPS)]
A_ANTICAUSAL = [A_TAPS - 1 - k for k in range(A_TAPS)]


def _layer_norm_stats(x):
    mu = jnp.mean(x, axis=-1, keepdims=True)
    xc = x - mu
    rstd = lax.rsqrt(jnp.mean(xc * xc, axis=-1, keepdims=True) + LN_EPS)
    return xc * rstd, rstd


def _mixer_specs(seq, tt):
    tiles_per_seq = seq // tt
    halo_blocks = tt // A_HALO
    z_spec = pl.BlockSpec((tt, IN_EVEN), lambda i: (i, 0))
    zh_spec = pl.BlockSpec((A_HALO, IN_EVEN), lambda i: (jnp.maximum(i * halo_blocks - 1, 0), 0))
    return tiles_per_seq, z_spec, zh_spec


def _vec_spec(rows, cols):
    return pl.BlockSpec((rows, cols), lambda i: (0, 0))


def _mixer_fwd(z, seq, caw, cab, lag, lab, cbw, comm=None):
    t = z.shape[0]
    tt = _tile(seq, 256)
    tiles_per_seq, z_spec, zh_spec = _mixer_specs(seq, tt)

    def body(z_ref, zh_ref, caw_ref, cab_ref, lag_ref, lab_ref, cbw_ref, mix_ref, a2_ref, a1_s, cb_s, sh_s):
        first = (pl.program_id(0) % tiles_per_seq) == 0
        _mixer_windows(z_ref, zh_ref, first, a1_s, cb_s, tt)
        _shifted_copies(a1_s, sh_s, tt)
        _conv_taps(a1_s, sh_s, caw_ref, A_CAUSAL, a2_ref, tt, bias_ref=cab_ref)
        xhat, _ = _layer_norm_stats(a2_ref[...])
        a3 = xhat * lag_ref[...] + lab_ref[...]
        mix_ref[:, 0:A_DIM] = (a3 * _sigmoid(a3)).astype(BF16)
        cv = _causal_conv(cb_s, cbw_ref, B_TAPS, B_HALO, tt)
        mix_ref[:, A_DIM:A_DIM + B_DIM] = (z_ref[:, 2 * A_DIM:3 * A_DIM] * cv).astype(BF16)

    return _call(
        body, name="mixer_fwd", grid=(t // tt,),
        in_specs=[z_spec, zh_spec, _vec_spec(A_TAPS, A_DIM), _vec_spec(1, A_DIM), _vec_spec(1, A_DIM), _vec_spec(1, A_DIM),
                  _vec_spec(B_TAPS, B_DIM)],
        out_specs=[pl.BlockSpec((tt, A_DIM + B_DIM), lambda i: (i, 0)), pl.BlockSpec((tt, A_DIM), lambda i: (i, 0))],
        out_shape=[jax.ShapeDtypeStruct((t, A_DIM + B_DIM), BF16), jax.ShapeDtypeStruct((t, A_DIM), F32)],
        scratch_shapes=[pltpu.VMEM((A_HALO + tt, A_DIM), F32), pltpu.VMEM((B_HALO + tt, B_DIM), F32),
                        pltpu.VMEM((SUBLANES - 1, tt + SHIFT_ROWS, A_DIM), F32)],
        args=(z, z, caw, cab, lag, lab, cbw), parallel=True, comm=comm)


def _mixer_bwd_local(z, a2, dmix, seq, lag, lab, comm=None):
    t = z.shape[0]
    tt = _tile(seq, 256)
    tiles_per_seq, z_spec, zh_spec = _mixer_specs(seq, tt)

    def body(z_ref, zh_ref, a2_ref, dmix_ref, lag_ref, lab_ref,
             da2_ref, dcv_ref, dcaw_ref, dcab_ref, dlag_ref, dlab_ref, dcbw_ref, a1_s, cb_s, sh_s):
        @pl.when(pl.program_id(0) == 0)
        def _():
            for ref in (dcaw_ref, dcab_ref, dlag_ref, dlab_ref, dcbw_ref):
                ref[...] = jnp.zeros_like(ref)

        first = (pl.program_id(0) % tiles_per_seq) == 0
        _mixer_windows(z_ref, zh_ref, first, a1_s, cb_s, tt)
        _shifted_copies(a1_s, sh_s, tt)
        xhat, rstd = _layer_norm_stats(a2_ref[...])
        a3 = xhat * lag_ref[...] + lab_ref[...]
        s3 = _sigmoid(a3)
        da3 = dmix_ref[:, 0:A_DIM] * (s3 * (1.0 + a3 * (1.0 - s3)))
        dlag_ref[...] += _colsum(da3 * xhat)
        dlab_ref[...] += _colsum(da3)
        dxh = da3 * lag_ref[...]
        da2 = rstd * (dxh - jnp.mean(dxh, axis=-1, keepdims=True) - xhat * jnp.mean(dxh * xhat, axis=-1, keepdims=True))
        da2_ref[...] = da2
        dcab_ref[...] += _colsum(da2)
        rb, blocks = _blocks(tt)
        for r, cols in blocks:
            da2_b = da2_ref[r:r + rb, cols]
            for k in range(A_TAPS):
                dcaw_ref[k:k + 1, cols] += _colsum(da2_b * _window_rows(a1_s, sh_s, r + A_CAUSAL[k], rb, cols))
        dcv =dmix_ref[:, A_DIM:A_DIM + B_DIM] * z_ref[:, 2 * A_DIM:3 * A_DIM]
        dcv_ref[...] = dcv
        for k in range(B_TAPS):
            dcbw_ref[k:k + 1, :] += _colsum(dcv * cb_s[pl.ds(B_HALO - (B_TAPS - 1) + k, tt), :])

    half = pl.BlockSpec((tt, A_DIM), lambda i: (i, 0))
    return _call(
        body, name="mixer_bwd_local", grid=(t // tt,),
        in_specs=[z_spec, zh_spec, half, pl.BlockSpec((tt, A_DIM + B_DIM), lambda i: (i, 0)),
                  _vec_spec(1, A_DIM), _vec_spec(1, A_DIM)],
        out_specs=[half, half, _vec_spec(A_TAPS, A_DIM), _vec_spec(1, A_DIM), _vec_spec(1, A_DIM), _vec_spec(1, A_DIM),
                   _vec_spec(B_TAPS, B_DIM)],
        out_shape=[jax.ShapeDtypeStruct((t, A_DIM), F32), jax.ShapeDtypeStruct((t, B_DIM), F32),
                   jax.ShapeDtypeStruct((A_TAPS, A_DIM), F32), jax.ShapeDtypeStruct((1, A_DIM), F32),
                   jax.ShapeDtypeStruct((1, A_DIM), F32), jax.ShapeDtypeStruct((1, A_DIM), F32),
                   jax.ShapeDtypeStruct((B_TAPS, B_DIM), F32)],
        scratch_shapes=[pltpu.VMEM((A_HALO + tt, A_DIM), F32), pltpu.VMEM((B_HALO + tt, B_DIM), F32),
                        pltpu.VMEM((SUBLANES - 1, tt + SHIFT_ROWS, A_DIM), F32)],
        args=(z, z, a2, dmix, lag, lab), comm=comm)


def _mixer_bwd_input(z, dmix, da2, dcv, seq, caw, cbw, comm=None):
    t = z.shape[0]
    tt = _tile(seq, 256)
    tiles_per_seq, z_spec, zh_spec = _mixer_specs(seq, tt)
    a_blocks = tt // A_HALO
    b_blocks = tt // B_HALO
    last_a = t // A_HALO - 1
    last_b = t // B_HALO - 1

    def body(z_ref, zh_ref, dmix_ref, da2_ref, da2n_ref, dcv_ref, dcvn_ref, caw_ref, cbw_ref, dz_ref, a1_s, cb_s, da2_s, dcv_s,
             sh_s, da1_s):
        pos = pl.program_id(0) % tiles_per_seq
        first = pos == 0
        last = pos == tiles_per_seq - 1
        sig = _mixer_windows(z_ref, zh_ref, first, a1_s, cb_s, tt)
        da2_s[0:tt, :] = da2_ref[...]
        da2_s[tt:tt + A_HALO, :] = jnp.where(last, 0.0, da2n_ref[...])
        dcv_s[0:tt, :] = dcv_ref[...]
        dcv_s[tt:tt + B_HALO, :] = jnp.where(last, 0.0, dcvn_ref[...])
        _shifted_copies(da2_s, sh_s, tt)
        _conv_taps(da2_s, sh_s, caw_ref, A_ANTICAUSAL, da1_s, tt)
        da1 = da1_s[...]
        dz_ref[:, 0:A_DIM] = (da1 * sig).astype(BF16)
        dz_ref[:, A_DIM:2 * A_DIM] = (da1 * z_ref[:, 0:A_DIM] * sig * (1.0 - sig)).astype(BF16)
        cv = _causal_conv(cb_s, cbw_ref, B_TAPS, B_HALO, tt)
        dz_ref[:, 2 * A_DIM:3 * A_DIM] = (dmix_ref[:, A_DIM:A_DIM + B_DIM] * cv).astype(BF16)
        dcb = cbw_ref[0:1, :] * dcv_s[pl.ds(B_TAPS - 1, tt), :]
        for k in range(1, B_TAPS):
            dcb = dcb + cbw_ref[k:k + 1, :] * dcv_s[pl.ds(B_TAPS - 1 - k, tt), :]
        dz_ref[:, 3 * A_DIM:4 * A_DIM] = (dcb * z_ref[:, 4 * A_DIM:5 * A_DIM]).astype(BF16)
        dz_ref[:, 4 * A_DIM:5 * A_DIM] = (dcb * z_ref[:, 3 * A_DIM:4 * A_DIM]).astype(BF16)

    half = pl.BlockSpec((tt, A_DIM), lambda i: (i, 0))
    return _call(
        body, name="mixer_bwd_input", grid=(t // tt,),
        in_specs=[z_spec, zh_spec, pl.BlockSpec((tt, A_DIM + B_DIM), lambda i: (i, 0)),
                  half, pl.BlockSpec((A_HALO, A_DIM), lambda i: (jnp.minimum((i + 1) * a_blocks, last_a), 0)),
                  half, pl.BlockSpec((B_HALO, B_DIM), lambda i: (jnp.minimum((i + 1) * b_blocks, last_b), 0)),
                  _vec_spec(A_TAPS, A_DIM), _vec_spec(B_TAPS, B_DIM)],
        out_specs=[pl.BlockSpec((tt, IN_EVEN), lambda i: (i, 0))],
        out_shape=[jax.ShapeDtypeStruct((t, IN_EVEN), BF16)],
        scratch_shapes=[pltpu.VMEM((A_HALO + tt, A_DIM), F32), pltpu.VMEM((B_HALO + tt, B_DIM), F32),
                        pltpu.VMEM((tt + A_HALO, A_DIM), F32), pltpu.VMEM((tt + B_HALO, B_DIM), F32),
                        pltpu.VMEM((SUBLANES - 1, tt + SHIFT_ROWS, A_DIM), F32), pltpu.VMEM((tt, A_DIM), F32)],
        args=(z, z, dmix, da2, da2, dcv, dcv, caw, cbw), parallel=True, comm=comm)[0]


def _tril_ws(ws_ref, g):
    rows = lax.broadcasted_iota(jnp.int32, (CHUNK, CHUNK), 0)
    cols = lax.broadcasted_iota(jnp.int32, (CHUNK, CHUNK), 1)
    return jnp.where(rows >= cols, ws_ref[g], 0.0).astype(BF16), rows >= cols


def _sgu_fwd(pre, lvg, lvb, ws, bs_b, comm=None):
    t = pre.shape[0]
    tt = _tile(t, 256)

    def body(pre_ref, lvg_ref, lvb_ref, ws_ref, bsb_ref, y_ref):
        vhat, _ = _layer_norm_stats(_gelu(pre_ref[:, C_DIM:2 * C_DIM]))
        vl = (vhat * lvg_ref[...] + lvb_ref[...]).astype(BF16)
        for g in range(C_GROUPS):
            w, _ = _tril_ws(ws_ref, g)
            cols = slice(g * CHUNK, (g + 1) * CHUNK)
            for ci in range(tt // CHUNK):
                rows = slice(ci * CHUNK, (ci + 1) * CHUNK)
                sv = jnp.dot(w, vl[rows, cols], preferred_element_type=F32) + bsb_ref[g]
                y_ref[rows, cols] = (_gelu(pre_ref[rows, cols]) * sv).astype(BF16)

    group = pl.BlockSpec((C_GROUPS, CHUNK, CHUNK), lambda i: (0, 0, 0))
    return _call(
        body, name="sgu_fwd", grid=(t // tt,),
        in_specs=[pl.BlockSpec((tt, 2 * C_DIM), lambda i: (i, 0)), _vec_spec(1, C_DIM), _vec_spec(1, C_DIM), group, group],
        out_specs=[pl.BlockSpec((tt, C_DIM), lambda i: (i, 0))], out_shape=[jax.ShapeDtypeStruct((t, C_DIM), BF16)],
        args=(pre, lvg, lvb, ws, bs_b), parallel=True, comm=comm)[0]


def _sgu_bwd(pre, dy, lvg, lvb, ws, bs_b, comm=None):
    t = pre.shape[0]
    tt = _tile(t, 256)

    def body(pre_ref, dy_ref, lvg_ref, lvb_ref, ws_ref, bsb_ref, dpre_ref, dws_ref, dbsb_ref, dlvg_ref, dlvb_ref, dbin_ref,
             dvl_s):
        @pl.when(pl.program_id(0) == 0)
        def _():
            for ref in (dws_ref, dbsb_ref, dlvg_ref, dlvb_ref, dbin_ref):
                ref[...] = jnp.zeros_like(ref)

        v, v_grad = _gelu_and_grad(pre_ref[:, C_DIM:2 * C_DIM])
        vhat, rstd = _layer_norm_stats(v)
        vl = (vhat * lvg_ref[...] + lvb_ref[...]).astype(BF16)
        for g in range(C_GROUPS):
            w, keep = _tril_ws(ws_ref, g)
            cols = slice(g * CHUNK, (g + 1) * CHUNK)
            dws = jnp.zeros((CHUNK, CHUNK), F32)
            dbs = jnp.zeros((CHUNK, 1), F32)
            for ci in range(tt // CHUNK):
                rows = slice(ci * CHUNK, (ci + 1) * CHUNK)
                vl_g = vl[rows, cols]
                sv = jnp.dot(w, vl_g, preferred_element_type=F32) + bsb_ref[g]
                u, u_grad = _gelu_and_grad(pre_ref[rows, cols])
                dyv = dy_ref[rows, cols]
                du = dyv * sv * u_grad
                dpre_ref[rows, cols] = du.astype(BF16)
                dbin_ref[:, cols] += _colsum(du)
                dsv = dyv * u
                dbs = dbs + jnp.sum(dsv, axis=1, keepdims=True)
                dsv16 = dsv.astype(BF16)
                dws = dws + lax.dot_general(dsv16, vl_g, (((1,), (1,)), ((), ())), preferred_element_type=F32)
                dvl_s[rows, cols] = lax.dot_general(w, dsv16, (((0,), (0,)), ((), ())), preferred_element_type=F32)
            dws_ref[g] += jnp.where(keep, dws, 0.0)
            dbsb_ref[g] += dbs
        dvl = dvl_s[...]
        dlvg_ref[...] += _colsum(dvl * vhat)
        dlvb_ref[...] += _colsum(dvl)
        dxh = dvl * lvg_ref[...]
        dv = rstd * (dxh - jnp.mean(dxh, axis=-1, keepdims=True) - vhat * jnp.mean(dxh * vhat, axis=-1, keepdims=True))
        dpv = dv * v_grad
        dpre_ref[:, C_DIM:2 * C_DIM] = dpv.astype(BF16)
        dbin_ref[:, C_DIM:2 * C_DIM] += _colsum(dpv)

    group = pl.BlockSpec((C_GROUPS, CHUNK, CHUNK), lambda i: (0, 0, 0))
    return _call(
        body, name="sgu_bwd", grid=(t // tt,),
        in_specs=[pl.BlockSpec((tt, 2 * C_DIM), lambda i: (i, 0)), pl.BlockSpec((tt, C_DIM), lambda i: (i, 0)),
                  _vec_spec(1, C_DIM), _vec_spec(1, C_DIM), group, group],
        out_specs=[pl.BlockSpec((tt, 2 * C_DIM), lambda i: (i, 0)), group, group,
                   _vec_spec(1, C_DIM), _vec_spec(1, C_DIM), _vec_spec(1, 2 * C_DIM)],
        out_shape=[jax.ShapeDtypeStruct((t, 2 * C_DIM), BF16), jax.ShapeDtypeStruct((C_GROUPS, CHUNK, CHUNK), F32),
                   jax.ShapeDtypeStruct((C_GROUPS, CHUNK, CHUNK), F32), jax.ShapeDtypeStruct((1, C_DIM), F32),
                   jax.ShapeDtypeStruct((1, C_DIM), F32), jax.ShapeDtypeStruct((1, 2 * C_DIM), F32)],
        scratch_shapes=[pltpu.VMEM((tt, C_DIM), F32)],
        args=(pre, dy, lvg, lvb, ws, bs_b), comm=comm)


def _pair_sum(name, part, got, core):
    _, k, n = part.shape
    tk = _tile(k, 1024)

    def body(core_ref, p_ref, s_ref, o_ref):
        o_ref[...] = (p_ref[...] + s_ref[...]).astype(BF16)

    return pl.pallas_call(
        body, name=name,
        grid_spec=pltpu.PrefetchScalarGridSpec(
            num_scalar_prefetch=1, grid=(N_CHIP, k // tk),
            in_specs=[pl.BlockSpec((None, tk, n), lambda q, i, core_ref: (2 * q + core_ref[0], i, 0)),
                      pl.BlockSpec((None, tk, n), lambda q, i, core_ref: (q, i, 0))],
            out_specs=pl.BlockSpec((None, tk, n), lambda q, i, core_ref: (q, i, 0))),
        out_shape=jax.ShapeDtypeStruct((N_CHIP, k, n), BF16),
        compiler_params=pltpu.CompilerParams(dimension_semantics=("parallel", "parallel"), vmem_limit_bytes=VMEM_LIMIT_BYTES),
    )(core, part, got)


def _adamw_math(w, g, m, v):
    m = ADAM_B1 * m + (1.0 - ADAM_B1) * g
    v = ADAM_B2 * v + (1.0 - ADAM_B2) * (g * g)
    m_hat = m / (1.0 - ADAM_B1 ** ADAM_STEP)
    v_hat = v / (1.0 - ADAM_B2 ** ADAM_STEP)
    delta = -ADAM_LR * (m_hat / (jnp.sqrt(v_hat) + ADAM_EPS) + ADAM_WD * w)
    return delta, m, v


def _sum_adamw(name, parts, w, m, v, comm=None):
    layers = len(parts)
    n_parts, k, n = parts[0].shape
    tk = _tile(k, 256)

    def body(*refs):
        p_refs = refs[:layers]
        w_ref, m_ref, v_ref, g_ref, d_ref, nm_ref, nv_ref = refs[layers:]

        def total(p_ref):
            g = p_ref[0].astype(F32)
            for q in range(1, n_parts):
                g = g + p_ref[q].astype(F32)
            return g

        g = total(p_refs[0])
        for l in range(1, layers):
            g = jnp.where(pl.program_id(0) == l, total(p_refs[l]), g)
        g_ref[...] = g
        d_ref[...], nm_ref[...], nv_ref[...] = _adamw_math(w_ref[...], g, m_ref[...], v_ref[...])

    blk = pl.BlockSpec((None, tk, n), lambda l, i: (l, i, 0))
    return _call(
        body, name=name, grid=(layers, k // tk),
        in_specs=[pl.BlockSpec((n_parts, tk, n), lambda l, i: (0, i, 0))] * layers + [blk, blk, blk], out_specs=[blk] * 4,
        out_shape=[jax.ShapeDtypeStruct((layers, k, n), F32)] * 4, args=(*parts, w, m, v), parallel=True, comm=comm)


def _small_adamw(name, parts, w, m, v, losses):
    count = len(parts)

    def in_order(ref):
        total = ref[0]
        for dev in range(1, N_DEV):
            total = total + ref[dev]
        return total

    def body(*refs):
        p_refs, w_refs, m_refs, v_refs = (refs[j * count:(j + 1) * count] for j in range(4))
        losses_ref = refs[4 * count]
        g_refs, d_refs, nm_refs, nv_refs = (refs[4 * count + 1 + j * count:4 * count + 1 + (j + 1) * count] for j in range(4))
        loss_ref = refs[8 * count + 1]
        for i in range(count):
            g = in_order(p_refs[i])
            g_refs[i][...] = g
            d_refs[i][...], nm_refs[i][...], nv_refs[i][...] = _adamw_math(w_refs[i][...], g, m_refs[i][...], v_refs[i][...])
        loss_ref[...] = in_order(losses_ref)

    res = pl.pallas_call(
        body, name=name,
        out_shape=[jax.ShapeDtypeStruct(a.shape, F32) for a in w] * 4 + [jax.ShapeDtypeStruct(losses.shape[1:], F32)],
        compiler_params=pltpu.CompilerParams(vmem_limit_bytes=VMEM_LIMIT_BYTES))(*parts, *w, *m, *v, losses)
    return [res[j * count:(j + 1) * count] for j in range(4)], res[4 * count]


def _rows(a):
    return a.reshape(-1, a.shape[-1])


def _whole(gathered):
    return jnp.transpose(gathered, (1, 0, 2)).reshape(gathered.shape[1], -1)


SMALL =("ev_norm_g", "ev_conv_a_w", "ev_conv_a_b", "ev_ln_a_g", "ev_ln_a_b", "ev_conv_b_w", "od_norm_g", "od_b_in",
         "od_ln_v_g", "od_ln_v_b", "od_w_s", "od_b_s", "mlp_norm_g", "final_norm_g")
SMALL_SHARDED = ("ev_conv_a_w", "ev_conv_b_w", "od_norm_g", "od_b_in", "od_ln_v_g", "od_ln_v_b")
ORDER = ("ev_norm_g", "ev_w_in", "ev_conv_a_w", "ev_conv_a_b", "ev_ln_a_g", "ev_ln_a_b", "ev_conv_b_w", "ev_w_out",
         "od_norm_g", "od_w_in", "od_b_in", "od_ln_v_g", "od_ln_v_b", "od_w_s", "od_b_s", "od_w_out", "mlp_norm_g",
         "mlp_w1", "mlp_w2", "final_norm_g")


def kernel(x, ev_norm_g, ev_w_in, ev_conv_a_w, ev_conv_a_b, ev_ln_a_g, ev_ln_a_b, ev_conv_b_w, ev_w_out, od_norm_g, od_w_in, od_b_in, od_ln_v_g, od_ln_v_b, od_w_s, od_b_s, od_w_out, mlp_norm_g, mlp_w1, mlp_w2, final_norm_g, loss_target, m_ev_norm_g, m_ev_w_in, m_ev_conv_a_w, m_ev_conv_a_b, m_ev_ln_a_g, m_ev_ln_a_b, m_ev_conv_b_w, m_ev_w_out, m_od_norm_g, m_od_w_in, m_od_b_in, m_od_ln_v_g, m_od_ln_v_b, m_od_w_s, m_od_b_s, m_od_w_out, m_mlp_norm_g, m_mlp_w1, m_mlp_w2, m_final_norm_g, v_ev_norm_g, v_ev_w_in, v_ev_conv_a_w, v_ev_conv_a_b, v_ev_ln_a_g, v_ev_ln_a_b, v_ev_conv_b_w, v_ev_w_out, v_od_norm_g, v_od_w_in, v_od_b_in, v_od_ln_v_g, v_od_ln_v_b, v_od_w_s, v_od_b_s, v_od_w_out, v_mlp_norm_g, v_mlp_w1, v_mlp_w2, v_final_norm_g):
    W = dict(ev_norm_g=ev_norm_g, ev_w_in=ev_w_in, ev_conv_a_w=ev_conv_a_w, ev_conv_a_b=ev_conv_a_b, ev_ln_a_g=ev_ln_a_g,
             ev_ln_a_b=ev_ln_a_b, ev_conv_b_w=ev_conv_b_w, ev_w_out=ev_w_out, od_norm_g=od_norm_g, od_w_in=od_w_in,
             od_b_in=od_b_in, od_ln_v_g=od_ln_v_g, od_ln_v_b=od_ln_v_b, od_w_s=od_w_s, od_b_s=od_b_s, od_w_out=od_w_out,
             mlp_norm_g=mlp_norm_g, mlp_w1=mlp_w1, mlp_w2=mlp_w2, final_norm_g=final_norm_g)
    M = dict(ev_norm_g=m_ev_norm_g, ev_w_in=m_ev_w_in, ev_conv_a_w=m_ev_conv_a_w, ev_conv_a_b=m_ev_conv_a_b,
             ev_ln_a_g=m_ev_ln_a_g, ev_ln_a_b=m_ev_ln_a_b, ev_conv_b_w=m_ev_conv_b_w, ev_w_out=m_ev_w_out,
             od_norm_g=m_od_norm_g, od_w_in=m_od_w_in, od_b_in=m_od_b_in, od_ln_v_g=m_od_ln_v_g, od_ln_v_b=m_od_ln_v_b,
             od_w_s=m_od_w_s, od_b_s=m_od_b_s, od_w_out=m_od_w_out, mlp_norm_g=m_mlp_norm_g, mlp_w1=m_mlp_w1,
             mlp_w2=m_mlp_w2, final_norm_g=m_final_norm_g)
    V = dict(ev_norm_g=v_ev_norm_g, ev_w_in=v_ev_w_in, ev_conv_a_w=v_ev_conv_a_w, ev_conv_a_b=v_ev_conv_a_b,
             ev_ln_a_g=v_ev_ln_a_g, ev_ln_a_b=v_ev_ln_a_b, ev_conv_b_w=v_ev_conv_b_w, ev_w_out=v_ev_w_out,
             od_norm_g=v_od_norm_g, od_w_in=v_od_w_in, od_b_in=v_od_b_in, od_ln_v_g=v_od_ln_v_g, od_ln_v_b=v_od_ln_v_b,
             od_w_s=v_od_w_s, od_b_s=v_od_b_s, od_w_out=v_od_w_out, mlp_norm_g=v_mlp_norm_g, mlp_w1=v_mlp_w1,
             mlp_w2=v_mlp_w2, final_norm_g=v_final_norm_g)

    n_seq, seq, d = x.shape
    t = n_seq * seq
    dev = 4 * lax.axis_index("x") + 2 * lax.axis_index("y") + lax.axis_index("c")
    core = lax.axis_index("c").astype(jnp.int32).reshape(1)

    ev_g, cab, lag, lab = W["ev_norm_g"], W["ev_conv_a_b"], W["ev_ln_a_g"], W["ev_ln_a_b"]
    ws = W["od_w_s"][0]
    bs_b = jnp.broadcast_to(W["od_b_s"][0][:, :, None], (C_GROUPS, CHUNK, CHUNK))
    mlp_g = [W["mlp_norm_g"][l:l + 1] for l in range(2)]
    fin_g = W["final_norm_g"].reshape(1, d)

    def w16(name, l=0):
        return W[name][l].astype(BF16)

    h0 = x.reshape(t, d)
    gather = _gather_comm([w16("ev_w_in")])
    n0 = _rms_fwd("ev_norm", h0, ev_g, comm=gather)
    w_ev_in = _whole(gather.out[0])

    first, second = (0, 2), (1, 2)
    w1_0, w2_0, w1_1, w2_1 = w16("mlp_w1", 0), w16("mlp_w2", 0), w16("mlp_w1", 1), w16("mlp_w2", 1)

    g_a, g_b = _gather_comm([w16("ev_w_out")] + [_rows(W[n]) for n in SMALL_SHARDED]), _gather_comm([w1_0], first)
    z = _mm_nn("ev_in", n0, w_ev_in, IN_EVEN, _ep_store, [F32], comm=[g_a, g_b])[0]
    w_ev_out = g_a.out[0].reshape(D_MODEL, D_MODEL)
    caw, cbw, od_g, od_bin, lvg, lvb = [_whole(g) for g in g_a.out[1:]]

    g_c, g_d = _gather_comm([w1_0], second, into=g_b.out), _gather_comm([w2_0], first)
    mix, a2 = _mixer_fwd(z, seq, caw, cab, lag, lab, cbw, comm=[g_c, g_d])
    w1 = [_whole(g_c.out[0]), None]

    w_od_in16 = w16("od_w_in")
    g_e = _gather_comm([w_od_in16], first)
    h1, n1 = _mm_nn("ev_out", mix, w_ev_out, d, _ep_residual_norm, [F32, BF16], extras=(h0,), rows=(mlp_g[0],), comm=g_e)

    g_f, g_f2 = _gather_comm([w2_0], second, into=g_d.out), _gather_comm([w_od_in16], second, into=g_e.out)
    q0 = _mm_nn("mlp0_up", n1, w1[0], D_FF, _ep_relu_sq, [BF16], comm=[g_f, g_f2])[0]
    w2 = [g_f.out[0].reshape(D_FF, D_MODEL), None]
    w_od_in = _whole(g_f2.out[0])

    g_g, g_g2 = _gather_comm([w1_1], first), _gather_comm([w16("od_w_out")])
    h2, n2 = _mm_nn("mlp0_down", q0, w2[0], d, _ep_residual_norm, [F32, BF16], extras=(h1,), rows=(od_g,), comm=[g_g, g_g2])
    w_od_out = g_g2.out[0].reshape(D_MODEL, D_MODEL)
    g_h = _gather_comm([w1_1], second, into=g_g.out)
    pre = _mm_nn("od_in", n2, w_od_in, 2 * C_DIM, _ep_bias, [F32], rows=(od_bin,), comm=g_h)[0]
    w1[1] = _whole(g_h.out[0])
    g_i = _gather_comm([w2_1], (0, 4))
    y = _sgu_fwd(pre, lvg, lvb, ws, bs_b, comm=g_i)
    g_i2 = _gather_comm([w2_1], (1, 4), into=g_i.out)
    h3, n3 = _mm_nn("od_out", y, w_od_out, d, _ep_residual_norm, [F32, BF16], extras=(h2,), rows=(mlp_g[1],), comm=g_i2)
    g_j = _gather_comm([w2_1], second, into=g_i2.out)
    q1 = _mm_nn("mlp1_up", n3, w1[1], D_FF, _ep_relu_sq, [BF16], comm=g_j)[0]
    w2[1] = g_j.out[0].reshape(D_FF, D_MODEL)
    grad, grad16, d_fin_g, loss_part = _mm_nn(
        "mlp1_down", q1, w2[1], d, _ep_final_loss, [F32, BF16], extras=(h3, loss_target.reshape(t, d)), rows=(fin_g,),
        sums=(True, False), tm_want=FUSED_ROWS)

    by_chip = {}

    def swap(name, parts):
        comm = _pair_comm([parts])
        comm.parts, comm.weight = parts, name
        return comm

    def exchange(swapped, halves=False):
        sums = _pair_sum(f"pair_sum_{swapped.weight}", swapped.parts, swapped.out[0], core)
        if not halves:
            comm = _chip_comm([sums])
            comm.weight = swapped.weight
            return comm
        comm = _chip_comm([sums], first)
        comm.sums, comm.weight = sums, swapped.weight
        return comm

    def rest(comm):
        other = _chip_comm([comm.sums], second, into=comm.out)
        other.weight = comm.weight
        return other

    def done(comm):
        by_chip[comm.weight] = comm.out[0]

    dw2_1 = _mm_tn("mlp1_dw2", q1, grad16).reshape(N_DEV, D_FF // N_DEV, D_MODEL)
    s_a = swap("w2_1", dw2_1)
    dp = _mm_nt("mlp1_dq", grad16, w2[1], D_FF, _ep_relu_sq_grad, [BF16], extras=(q1,), comm=s_a)[0]
    c_a = exchange(s_a, halves=True)
    dw1_1 = _mm_tn("mlp1_dw1", n3, dp, col_blocks=N_DEV, comm=c_a)
    c_a2, s_b = rest(c_a), swap("w1_1", dw1_1)
    dn = _mm_nt("mlp1_dn", dp, w1[1], d, _ep_store, [F32], comm=[c_a2, s_b])[0]
    done(c_a2)
    c_b = exchange(s_b, halves=True)
    grad, grad16, dg_mlp1 = _rms_bwd("mlp1_dn_norm", dn, h3, mlp_g[1], grad)
    d_od_out = _mm_tn("od_dw_out", y, grad16).reshape(N_DEV, D_MODEL // N_DEV, D_MODEL)
    s_c = swap("od_out", d_od_out)
    dy = _mm_nt("od_dy", grad16, w_od_out, C_DIM, _ep_store, [F32], comm=s_c)[0]
    c_c = exchange(s_c)
    dpre, d_ws, d_bsb, d_lvg, d_lvb, d_bin = _sgu_bwd(pre, dy, lvg, lvb, ws, bs_b, comm=[c_b, c_c])
    done(c_c)
    c_b2 = rest(c_b)
    d_od_in = _mm_tn("od_dw_in", n2, dpre, col_blocks=N_DEV, comm=c_b2)
    done(c_b2)
    s_d = swap("od_in", d_od_in)
    dn = _mm_nt("od_dn", dpre, w_od_in, d, _ep_store, [F32], comm=s_d)[0]
    c_d = exchange(s_d)
    grad, grad16, d_od_g = _rms_bwd("od_dn_norm", dn, h2, od_g, grad)
    group_1 = dict(od_norm_g=d_od_g, od_b_in=d_bin, od_ln_v_g=d_lvg, od_ln_v_b=d_lvb,
                   od_w_s=d_ws.reshape(C_GROUPS * CHUNK, CHUNK), od_b_s=d_bsb[:, :, 0], final_norm_g=d_fin_g,
                   mlp_norm_g_1=dg_mlp1, loss=jnp.broadcast_to(loss_part, (SUBLANES, LANE_BLOCK)))
    gather_1 = _gather_comm(list(group_1.values()))
    dw2_0 = _mm_tn("mlp0_dw2", q0, grad16, comm=c_d).reshape(N_DEV, D_FF // N_DEV, D_MODEL)
    done(c_d)
    s_e = swap("w2_0", dw2_0)
    dp = _mm_nt("mlp0_dq", grad16, w2[0], D_FF, _ep_relu_sq_grad, [BF16], extras=(q0,), comm=[s_e, gather_1])[0]
    c_e = exchange(s_e, halves=True)
    dw1_0 = _mm_tn("mlp0_dw1", n1, dp, col_blocks=N_DEV, comm=c_e)
    c_e2, s_f = rest(c_e), swap("w1_0", dw1_0)
    dn = _mm_nt("mlp0_dn", dp, w1[0], d, _ep_store, [F32], comm=[c_e2, s_f])[0]
    done(c_e2)
    c_f = exchange(s_f, halves=True)
    grad, grad16, dg_mlp0 = _rms_bwd("mlp0_dn_norm", dn, h1, mlp_g[0], grad)
    d_ev_out = _mm_tn("ev_dw_out", mix, grad16).reshape(N_DEV, D_MODEL // N_DEV, D_MODEL)
    s_g = swap("ev_out", d_ev_out)
    dmix = _mm_nt("ev_dmix", grad16, w_ev_out, A_DIM + B_DIM, _ep_store, [F32], comm=s_g)[0]
    c_g = exchange(s_g)
    da2, dcv, d_caw, d_cab, d_lag, d_lab, d_cbw = _mixer_bwd_local(z, a2, dmix, seq, lag, lab, comm=[c_f, c_g])
    done(c_g)
    c_f2 = rest(c_f)
    dz = _mixer_bwd_input(z, dmix, da2, dcv, seq, caw, cbw, comm=c_f2)
    done(c_f2)
    group_2 = dict(ev_conv_a_w=d_caw, ev_conv_a_b=d_cab, ev_ln_a_g=d_lag, ev_ln_a_b=d_lab, ev_conv_b_w=d_cbw,
                   mlp_norm_g_0=dg_mlp0)
    gather_2 = _gather_comm(list(group_2.values()))
    d_ev_in = _mm_tn("ev_dw_in", n0, dz, comm=gather_2)
    d_ev_in = jnp.transpose(d_ev_in.reshape(D_MODEL, N_DEV, IN_EVEN // N_DEV), (1, 0, 2))
    s_h = swap("ev_in", d_ev_in)
    dn = _mm_nt("ev_dn", dz, w_ev_in, d, _ep_store, [F32], comm=s_h)[0]
    c_h = exchange(s_h)
    grad_x, _, d_ev_g = _rms_bwd("ev_dn_norm", dn, h0, ev_g, grad, comm=c_h)
    done(c_h)

    last_gather = _gather_comm([d_ev_g])
    shard = {"od_w_out": ("od_out",), "mlp_w1": ("w1_0", "w1_1"), "mlp_w2": ("w2_0", "w2_1"), "od_w_in": ("od_in",),
             "ev_w_out": ("ev_out",), "ev_w_in": ("ev_in",)}
    carried = {"od_w_out": last_gather}
    out_g, out_d, out_m, out_v = {}, {}, {}, {}
    for name, keys in shard.items():
        out_g[name], out_d[name], out_m[name], out_v[name] = _sum_adamw(
            f"adamw_{name}", [by_chip[key] for key in keys], W[name], M[name], V[name], comm=carried.get(name))

    gathered = dict(zip(group_1, gather_1.out), **dict(zip(group_2, gather_2.out)), ev_norm_g=last_gather.out[0])
    gathered["mlp_norm_g"] = jnp.concatenate([gathered["mlp_norm_g_0"], gathered["mlp_norm_g_1"]], axis=1)
    mine = []
    for n in SMALL:
        g = gathered[n]
        if n in SMALL_SHARDED:
            width = W[n].shape[-1]
            g = lax.dynamic_slice_in_dim(g, dev * width, width, axis=2)
        mine.append(g)
    res, loss_tile = _small_adamw("adamw_small", mine, [_rows(W[n]) for n in SMALL], [_rows(M[n]) for n in SMALL],
                                  [_rows(V[n]) for n in SMALL], gathered["loss"])
    loss = loss_tile[0, 0]
    for store, values in zip((out_g, out_d, out_m, out_v), res):
        for n, value in zip(SMALL, values):
            store[n] = value.reshape(W[n].shape)

    return (loss, grad_x.reshape(n_seq, seq, d), *[out_g[n] for n in ORDER], *[out_d[n] for n in ORDER],
            *[out_m[n] for n in ORDER], *[out_v[n] for n in ORDER])
```

```python
import math

import jax
import jax.numpy as jnp
from jax import lax
from jax.experimental import pallas as pl
from jax.experimental.pallas import tpu as pltpu

F32 = jnp.float32
BF16 = jnp.bfloat16
MESH = pl.DeviceIdType.MESH

D_MODEL = 1024
A_DIM = 512
B_DIM = 512
IN_EVEN = 2 * A_DIM + 3 * B_DIM
A_TAPS = 31
B_TAPS = 3
CHUNK = 128
C_GROUPS = 8
C_DIM = 1024
D_FF = 4096
RMS_EPS = 1e-6
LN_EPS = 1e-5
N_DEV = 8
N_CHIP = 4

ADAM_LR = 0.001
ADAM_B1 = 0.9
ADAM_B2 = 0.999
ADAM_EPS = 1e-08
ADAM_WD = 0.01
ADAM_STEP = 10

A_HALO = 32
B_HALO = 8
VMEM_LIMIT_BYTES = 56 * 1024 * 1024
TINY = 1.1754944e-38
INV_SQRT2 = 1.0 / math.sqrt(2.0)
INV_SQRT_2PI = 1.0 / math.sqrt(2.0 * math.pi)
HBM_SPEC = pl.BlockSpec(memory_space=pltpu.HBM)


def _tile(n, want):
    t = min(n, want)
    while n % t:
        t //= 2
    return t


def _sigmoid(x):
    return 1.0 / (1.0 + jnp.exp(-x))


def _gelu(x):
    return 0.5 * x * (1.0 + lax.erf(x * INV_SQRT2))


def _gelu_and_grad(x):
    cdf = 0.5 * (1.0 + lax.erf(x * INV_SQRT2))
    return x * cdf, cdf + x * jnp.exp(-0.5 * x * x) * INV_SQRT_2PI


def _colsum(x):
    return jnp.sum(x, axis=0, keepdims=True)


class _Comm:
    def __init__(self, ins, out_shapes, sem_shapes, start, finish, middle=None, into=None, sibling=False, chips=False):
        self.ins, self.out_shapes, self.sem_shapes, self.start, self.finish = ins, out_shapes, sem_shapes, start, finish
        self.middle = middle
        self.sibling, self.chips = sibling, chips
        self.into = list(into) if into is not None else []
        self.out = None


def _piece_rows(rows, piece):
    if piece is None:
        return 0, rows
    i, n = piece
    return i * (rows // n), rows // n


MIDDLE_AT = 0.75
BARRIER_IDS = {(True, True): 0, (True, False): 1, (False, True): 2}
FUSED_ROWS = 512


def _call(body, *, name, grid, in_specs, out_specs, out_shape, args, scratch_shapes=(), parallel=False, comm=None):
    comms = [] if comm is None else (list(comm) if isinstance(comm, (list, tuple)) else [comm])
    if not comms:
        sem = ("parallel" if parallel else "arbitrary",) * len(grid)
        return pl.pallas_call(
            body, name=name, grid=grid, in_specs=list(in_specs), out_specs=list(out_specs), out_shape=list(out_shape),
            scratch_shapes=list(scratch_shapes),
            compiler_params=pltpu.CompilerParams(dimension_semantics=sem, vmem_limit_bytes=VMEM_LIMIT_BYTES),
        )(*args)
    n_in, n_out, n_scr = len(in_specs), len(out_shape), len(scratch_shapes)
    c_ins_all = [a for cm in comms for a in cm.ins]
    c_into_all = [a for cm in comms for a in cm.into]
    c_out_shapes = [s for cm in comms for s in cm.out_shapes]
    c_sem_shapes = [s for cm in comms for s in cm.sem_shapes]
    aliases, in_pos, out_pos = {}, n_in + len(c_ins_all), n_out
    for cm in comms:
        for j in range(len(cm.into)):
            aliases[in_pos + j] = out_pos + j
        in_pos += len(cm.into)
        out_pos += len(cm.out_shapes)
    to_sibling = any(cm.sibling for cm in comms)
    to_chips = any(cm.chips for cm in comms)
    steps = grid
    total = math.prod(steps)
    first_step = (0,) * len(steps)
    last_step = tuple(s - 1 for s in steps)
    middle_step = None
    if 0 < int(MIDDLE_AT * total) < total - 1:
        rest, idx = int(MIDDLE_AT * total), []
        for s in reversed(steps):
            idx.append(rest % s)
            rest //= s
        middle_step = tuple(reversed(idx))

    def carrying(*refs):
        pos = 0
        ins = refs[pos:pos + n_in]; pos += n_in
        c_ins = refs[pos:pos + len(c_ins_all)]; pos += len(c_ins_all) + len(c_into_all)
        outs = refs[pos:pos + n_out]; pos += n_out
        c_outs = refs[pos:pos + len(c_out_shapes)]; pos += len(c_out_shapes)
        scr = refs[pos:pos + n_scr]; pos += n_scr
        c_sems = refs[pos:]
        views, i0, o0, s0 = [], 0, 0, 0
        for cm in comms:
            views.append((c_ins[i0:i0 + len(cm.ins)], c_outs[o0:o0 + len(cm.out_shapes)], c_sems[s0:s0 + len(cm.sem_shapes)]))
            i0, o0, s0 = i0 + len(cm.ins), o0 + len(cm.out_shapes), s0 + len(cm.sem_shapes)

        def at(step):
            hit = pl.program_id(0) == step[0]
            for axis in range(1, len(steps)):
                hit = jnp.logical_and(hit, pl.program_id(axis) == step[axis])
            return hit

        @pl.when(at(first_step))
        def _():
            x, y, c, chips = _place()
            peers = ([(x, y, 1 - c)] if to_sibling else []) + ([(*chip, c) for chip in chips] if to_chips else [])
            barrier = pltpu.get_barrier_semaphore()
            for peer in peers:
                pl.semaphore_signal(barrier, inc=1, device_id=peer, device_id_type=MESH)
            pl.semaphore_wait(barrier, len(peers))
            for cm, view in zip(comms, views):
                cm.start(*view)

        if middle_step is not None:
            @pl.when(at(middle_step))
            def _():
                for cm, view in zip(comms, views):
                    if cm.middle is not None:
                        cm.middle(*view)

        body(*ins, *outs, *scr)

        @pl.when(at(last_step))
        def _():
            for cm, view in zip(comms, views):
                if cm.middle is not None and middle_step is None:
                    cm.middle(*view)
            for cm, view in zip(comms, views):
                cm.finish(*view)

    res = pl.pallas_call(
        carrying, name=name, grid=grid,
        in_specs=[*in_specs, *[HBM_SPEC] * (len(c_ins_all) + len(c_into_all))],
        out_specs=[*out_specs, *[HBM_SPEC] * len(c_out_shapes)],
        out_shape=[*out_shape, *c_out_shapes], scratch_shapes=[*scratch_shapes, *c_sem_shapes],
        input_output_aliases=aliases,
        compiler_params=pltpu.CompilerParams(dimension_semantics=("arbitrary",) * len(grid), vmem_limit_bytes=VMEM_LIMIT_BYTES,
                                             collective_id=BARRIER_IDS[to_sibling, to_chips]),
    )(*args, *c_ins_all, *c_into_all)
    pos = n_out
    for cm in comms:
        cm.out = list(res[pos:pos + len(cm.out_shapes)])
        pos += len(cm.out_shapes)
    return list(res[:n_out])


def _place():
    x, y, c = lax.axis_index("x"), lax.axis_index("y"), lax.axis_index("c")
    return x, y, c, [(1 - x, y), (x, 1 - y), (1 - x, 1 - y)]


def _gather_comm(shards, piece=None, into=None):
    nw = len(shards)
    spans = [_piece_rows(s.shape[0], piece) for s in shards]

    def plan(ins, outs, sems):
        send_sems, recv_sems, local_sems = sems
        x, y, c, chips = _place()
        me, sibling = (x, y, c), (x, y, 1 - c)

        def slot(w, p):
            return outs[w].at[4 * p[0] + 2 * p[1] + p[2], pl.ds(*spans[w])]

        def mine(w):
            return ins[w].at[pl.ds(*spans[w])]

        def copy(w, k, block, to, src=None):
            return pltpu.make_async_remote_copy(
                src_ref=slot(w, block) if src is None else src, dst_ref=slot(w, block),
                send_sem=send_sems.at[w, k], recv_sem=recv_sems.at[w, k], device_id=to, device_id_type=MESH)

        local = [pltpu.make_async_copy(mine(w), slot(w, me), local_sems.at[w]) for w in range(nw)]
        first = [[copy(w, 0, me, sibling, src=mine(w))] + [copy(w, 1 + j, me, (*chip, c), src=mine(w)) for j, chip in enumerate(chips)]
                 for w in range(nw)]
        landed = [[copy(w, 1 + j, (*chip, c), me) for j, chip in enumerate(chips)] for w in range(nw)]
        passed = [[copy(w, 4 + j, (*chip, c), sibling) for j, chip in enumerate(chips)] for w in range(nw)]
        from_sibling = [[copy(w, 0, sibling, me)] + [copy(w, 4 + j, (*chip, 1 - c), me) for j, chip in enumerate(chips)]
                        for w in range(nw)]
        return local, first, landed, passed, from_sibling

    def start(ins, outs, sems):
        local, first, _, _, _ = plan(ins, outs, sems)
        for cp in local:
            cp.start()
        for row in first:
            for cp in row:
                cp.start()

    def middle(ins, outs, sems):
        _, _, landed, passed, _ = plan(ins, outs, sems)
        for w in range(nw):
            for j in range(3):
                landed[w][j].wait_recv()
                passed[w][j].start()

    def finish(ins, outs, sems):
        local, first, landed, passed, from_sibling = plan(ins, outs, sems)
        for w in range(nw):
            for cp in from_sibling[w]:
                cp.wait_recv()
        for w in range(nw):
            for cp in first[w] + passed[w]:
                cp.wait_send()
        for cp in local:
            cp.wait()

    return _Comm(list(shards), [jax.ShapeDtypeStruct((N_DEV, *s.shape), s.dtype) for s in shards],
                 [pltpu.SemaphoreType.DMA((nw, 7)), pltpu.SemaphoreType.DMA((nw, 7)), pltpu.SemaphoreType.DMA((nw,))],
                 start, finish, middle, into=into, sibling=True, chips=True)


def _pair_comm(parts):
    nw = len(parts)

    def plan(ins, outs, sems):
        send_sems, recv_sems = sems
        x, y, c, _ = _place()
        return [pltpu.make_async_remote_copy(
            src_ref=ins[w].at[2 * q + 1 - c], dst_ref=outs[w].at[q], send_sem=send_sems.at[w, q], recv_sem=recv_sems.at[w, q],
            device_id=(x, y, 1 - c), device_id_type=MESH) for w in range(nw) for q in range(N_CHIP)]

    def start(ins, outs, sems):
        for cp in plan(ins, outs, sems):
            cp.start()

    def finish(ins, outs, sems):
        for cp in plan(ins, outs, sems):
            cp.wait()

    return _Comm(list(parts), [jax.ShapeDtypeStruct((N_CHIP, *p.shape[1:]), p.dtype) for p in parts],
                 [pltpu.SemaphoreType.DMA((nw, N_CHIP)), pltpu.SemaphoreType.DMA((nw, N_CHIP))], start, finish, sibling=True)


def _chip_comm(sums, piece=None, into=None):
    nw = len(sums)
    spans = [_piece_rows(s.shape[1], piece) for s in sums]

    def plan(ins, outs, sems):
        send_sems, recv_sems, local_sems = sems
        x, y, c, chips = _place()
        my_chip = 2 * x + y
        local = [pltpu.make_async_copy(ins[w].at[my_chip, pl.ds(*spans[w])], outs[w].at[my_chip, pl.ds(*spans[w])],
                                       local_sems.at[w]) for w in range(nw)]
        remote = [pltpu.make_async_remote_copy(
            src_ref=ins[w].at[2 * chip[0] + chip[1], pl.ds(*spans[w])], dst_ref=outs[w].at[my_chip, pl.ds(*spans[w])],
            send_sem=send_sems.at[w, j], recv_sem=recv_sems.at[w, j], device_id=(*chip, c), device_id_type=MESH)
            for w in range(nw) for j, chip in enumerate(chips)]
        return local, remote

    def start(ins, outs, sems):
        local, remote = plan(ins, outs, sems)
        for cp in local + remote:
            cp.start()

    def finish(ins, outs, sems):
        local, remote = plan(ins, outs, sems)
        for cp in remote + local:
            cp.wait()

    return _Comm(list(sums), [jax.ShapeDtypeStruct(s.shape, s.dtype) for s in sums],
                 [pltpu.SemaphoreType.DMA((nw, 3)), pltpu.SemaphoreType.DMA((nw, 3)), pltpu.SemaphoreType.DMA((nw,))],
                 start, finish, into=into, chips=True)


def _matmul(name, a, b, *, kind, m, n, k, a_spec, b_spec, tm, tn, tk, out_shape, out_specs, epilogue,
            extras=(), extra_specs=(), comm=None):
    dims = {"nn": (((1,), (0,)), ((), ())), "nt": (((1,), (1,)), ((), ())), "tn": (((0,), (0,)), ((), ()))}[kind]
    nk = k // tk
    n_extra = len(extras)
    n_out = len(out_shape)

    def body(a_ref, b_ref, *rest):
        extra_refs = rest[:n_extra]
        out_refs = rest[n_extra:n_extra + n_out]
        part = lax.dot_general(a_ref[...], b_ref[...], dims, preferred_element_type=F32)
        if nk == 1:
            epilogue(part, extra_refs, out_refs)
            return
        acc_ref = rest[n_extra + n_out]
        step = pl.program_id(2)

        @pl.when(step == 0)
        def _():
            acc_ref[...] = part

        @pl.when(jnp.logical_and(step > 0, step < nk - 1))
        def _():
            acc_ref[...] += part

        @pl.when(step == nk - 1)
        def _():
            epilogue(acc_ref[...] + part, extra_refs, out_refs)

    return _call(
        body, name=name, grid=(m // tm, n // tn, nk), in_specs=[a_spec, b_spec, *extra_specs], out_specs=out_specs,
        out_shape=out_shape, scratch_shapes=[pltpu.VMEM((tm, tn), F32)] if nk > 1 else [], args=(a, b, *extras), comm=comm)


def _mm_operands(m, n, tm, tn, out_dtypes, extras, rows, sums):
    tile = pl.BlockSpec((tm, tn), lambda i, j, kk: (i, j))
    row = pl.BlockSpec((1, tn), lambda i, j, kk: (0, j))
    one = pl.BlockSpec((1, 1), lambda i, j, kk: (0, 0))
    out_shape = [jax.ShapeDtypeStruct((m, n), dt) for dt in out_dtypes]
    out_shape += [jax.ShapeDtypeStruct((1, n if wide else 1), F32) for wide in sums]
    out_specs = [tile] * len(out_dtypes) + [row if wide else one for wide in sums]
    return (*extras, *rows), [tile] * len(extras) + [row] * len(rows), out_shape, out_specs


def _row_tile(m, k, tm_want):
    return _tile(m, tm_want or (1024 if k <= 1024 else 512))


def _mm_nn(name, a, b, n, epilogue, out_dtypes, *, extras=(), rows=(), sums=(), tm_want=None, comm=None):
    m, k = a.shape
    blocked = b.ndim == 3
    tm = _row_tile(m, k, tm_want)
    tn = b.shape[-1] if blocked else _tile(n, 1024)
    tk = k
    extras, extra_specs, out_shape, out_specs = _mm_operands(m, n, tm, tn, out_dtypes, extras, rows, sums)
    if blocked:
        b_spec = pl.BlockSpec((None, tk, tn), lambda i, j, kk: (j, kk, 0))
    else:
        b_spec = pl.BlockSpec((tk, tn), lambda i, j, kk: (kk, j))
    return _matmul(
        name, a, b, kind="nn", m=m, n=n, k=k, tm=tm, tn=tn, tk=tk,
        a_spec=pl.BlockSpec((tm, tk), lambda i, j, kk: (i, kk)), b_spec=b_spec, out_shape=out_shape, out_specs=out_specs,
        epilogue=epilogue, extras=extras, extra_specs=extra_specs, comm=comm)


def _mm_nt(name, a, b, n, epilogue, out_dtypes, *, extras=(), rows=(), sums=(), tm_want=None, comm=None):
    m, k = a.shape
    blocked = b.ndim == 3
    tm = _row_tile(m, k, tm_want)
    tn = _tile(n, 1024)
    tk = b.shape[-1] if blocked else k
    extras, extra_specs, out_shape, out_specs = _mm_operands(m, n, tm, tn, out_dtypes, extras, rows, sums)
    if blocked:
        b_spec = pl.BlockSpec((None, tn, tk), lambda i, j, kk: (kk, j, 0))
    else:
        b_spec = pl.BlockSpec((tn, tk), lambda i, j, kk: (j, kk))
    return _matmul(
        name, a, b, kind="nt", m=m, n=n, k=k, tm=tm, tn=tn, tk=tk,
        a_spec=pl.BlockSpec((tm, tk), lambda i, j, kk: (i, kk)), b_spec=b_spec, out_shape=out_shape, out_specs=out_specs,
        epilogue=epilogue, extras=extras, extra_specs=extra_specs, comm=comm)


def _mm_tn(name, a, b, *, col_blocks=0, comm=None):
    t, k = a.shape
    n = b.shape[1]
    tm = _row_tile(k, t, None)
    tn = n // col_blocks if col_blocks else _tile(n, 1024)
    tk = t
    if col_blocks:
        out_shape = [jax.ShapeDtypeStruct((col_blocks, k, tn), F32)]
        out_specs = [pl.BlockSpec((None, tm, tn), lambda i, j, kk: (j, i, 0))]
    else:
        out_shape = [jax.ShapeDtypeStruct((k, n), F32)]
        out_specs = [pl.BlockSpec((tm, tn), lambda i, j, kk: (i, j))]

    def epilogue(acc, extra_refs, out_refs):
        out_refs[0][...] = acc

    return _matmul(
        name, a, b, kind="tn", m=k, n=n, k=t, tm=tm, tn=tn, tk=tk,
        a_spec=pl.BlockSpec((tk, tm), lambda i, j, kk: (kk, i)), b_spec=pl.BlockSpec((tk, tn), lambda i, j, kk: (kk, j)),
        out_shape=out_shape, out_specs=out_specs, epilogue=epilogue, comm=comm)[0]


def _ep_store(acc, extra_refs, out_refs):
    out_refs[0][...] = acc.astype(out_refs[0].dtype)


def _ep_bias(acc, extra_refs, out_refs):
    out_refs[0][...] = acc + extra_refs[0][...]


def _rms(x):
    r = lax.rsqrt(jnp.mean(x * x, axis=-1, keepdims=True) + RMS_EPS)
    return r, x * r


def _rms_grad(dn, r, xr, g):
    dy = dn * g
    return r * (dy - xr * jnp.mean(dy * xr, axis=-1, keepdims=True))


def _ep_residual_norm(acc, extra_refs, out_refs):
    h = extra_refs[0][...] + acc
    out_refs[0][...] = h
    _, xr = _rms(h)
    out_refs[1][...] = (xr * extra_refs[1][...]).astype(BF16)


def _ep_final_loss(acc, extra_refs, out_refs):
    @pl.when(pl.program_id(0) == 0)
    def _():
        out_refs[2][...] = jnp.zeros_like(out_refs[2])
        out_refs[3][...] = jnp.zeros_like(out_refs[3])

    h = extra_refs[0][...] + acc
    g = extra_refs[2][...]
    r, xr = _rms(h)
    err = xr * g - extra_refs[1][...]
    out_refs[3][...] += 0.5 * jnp.sum(jnp.mean(err * err, axis=-1, keepdims=True), axis=0, keepdims=True)
    dout = err * (1.0 / h.shape[-1])
    out_refs[2][...] += _colsum(dout * xr)
    out = _rms_grad(dout, r, xr, g)
    out_refs[0][...] = out
    out_refs[1][...] = out.astype(BF16)


def _ep_relu_sq(acc, extra_refs, out_refs):
    r = jnp.maximum(acc, 0.0)
    out_refs[0][...] = (r * r).astype(BF16)


def _ep_relu_sq_grad(acc, extra_refs, out_refs):
    q = extra_refs[0][...].astype(F32)
    out_refs[0][...] = (acc * (2.0 * q * lax.rsqrt(jnp.maximum(q, TINY)))).astype(BF16)


def _rms_fwd(name, h, g, comm=None, casts=()):
    t, d = h.shape
    tt = _tile(t, 512)
    steps = t // tt
    n_cast = len(casts)

    def body(h_ref, g_ref, *rest):
        n_ref = rest[n_cast]
        x = h_ref[...]
        r = lax.rsqrt(jnp.mean(x * x, axis=-1, keepdims=True) + RMS_EPS)
        n_ref[...] = (x * r * g_ref[...]).astype(BF16)
        for src, dst in zip(rest[:n_cast], rest[n_cast + 1:]):
            dst[...] = src[...].astype(BF16)

    cast_in = [pl.BlockSpec((None, w.shape[1] // steps, w.shape[2]), lambda i, l=l: (l, i, 0)) for w, l in casts]
    cast_out = [pl.BlockSpec((w.shape[1] // steps, w.shape[2]), lambda i: (i, 0)) for w, _ in casts]
    res = _call(
        body, name=name, grid=(steps,),
        in_specs=[pl.BlockSpec((tt, d), lambda i: (i, 0)), pl.BlockSpec((1, d), lambda i: (0, 0)), *cast_in],
        out_specs=[pl.BlockSpec((tt, d), lambda i: (i, 0)), *cast_out],
        out_shape=[jax.ShapeDtypeStruct((t, d), BF16)] + [jax.ShapeDtypeStruct(w.shape[1:], BF16) for w, _ in casts],
        args=(h, g, *[w for w, _ in casts]), parallel=True, comm=comm)
    return res[0], res[1:]


def _rms_bwd(name, dn, h, g, grad_in, comm=None):
    t, d = h.shape
    tt = _tile(t, 512)

    def body(dn_ref, h_ref, g_ref, gin_ref, gout_ref, gout16_ref, dg_ref):
        @pl.when(pl.program_id(0) == 0)
        def _():
            dg_ref[...] = jnp.zeros_like(dg_ref)

        dnv = dn_ref[...]
        r, xr = _rms(h_ref[...])
        dg_ref[...] += _colsum(dnv * xr)
        out = gin_ref[...] + _rms_grad(dnv, r, xr, g_ref[...])
        gout_ref[...] = out
        gout16_ref[...] = out.astype(BF16)

    row = pl.BlockSpec((tt, d), lambda i: (i, 0))
    vec = pl.BlockSpec((1, d), lambda i: (0, 0))
    return _call(
        body, name=name, grid=(t // tt,), in_specs=[row, row, vec, row], out_specs=[row, row, vec],
        out_shape=[jax.ShapeDtypeStruct((t, d), F32), jax.ShapeDtypeStruct((t, d), BF16), jax.ShapeDtypeStruct((1, d), F32)],
        args=(dn, h, g, grad_in), comm=comm)


def _mixer_windows(z_ref, zh_ref, first, a1_s, cb_s, tt):
    sig = _sigmoid(z_ref[:, A_DIM:2 * A_DIM])
    a1_s[A_HALO:A_HALO + tt, :] = z_ref[:, 0:A_DIM] * sig
    a1_h = zh_ref[:, 0:A_DIM] * _sigmoid(zh_ref[:, A_DIM:2 * A_DIM])
    a1_s[0:A_HALO, :] = jnp.where(first, 0.0, a1_h)
    cb_s[B_HALO:B_HALO + tt, :] = z_ref[:, 3 * A_DIM:4 * A_DIM] * z_ref[:, 4 * A_DIM:5 * A_DIM]
    cb_h = zh_ref[A_HALO - B_HALO:A_HALO, 3 * A_DIM:4 * A_DIM] * zh_ref[A_HALO - B_HALO:A_HALO, 4 * A_DIM:5 * A_DIM]
    cb_s[0:B_HALO, :] = jnp.where(first, 0.0, cb_h)
    return sig


def _causal_conv(win_s, w_ref, taps, halo, tt):
    base = halo - (taps - 1)
    acc = w_ref[0:1, :] * win_s[pl.ds(base, tt), :]
    for k in range(1, taps):
        acc = acc + w_ref[k:k + 1, :] * win_s[pl.ds(base + k, tt), :]
    return acc


SUBLANES = 8
LANE_BLOCK = 128
ROW_BLOCK = 128
SHIFT_ROWS = A_HALO - SUBLANES


def _shifted_copies(win_s, sh_s, tt):
    for b in range(1, SUBLANES):
        sh_s[b - 1] = win_s[pl.ds(b, tt + SHIFT_ROWS), :]


def _window_rows(win_s, sh_s, offset, rows, cols):
    b = offset % SUBLANES
    if b == 0:
        return win_s[pl.ds(offset, rows), cols]
    return sh_s[b - 1, pl.ds(offset - b, rows), cols]


def _blocks(tt):
    rb = min(tt, ROW_BLOCK)
    return rb, [(r, slice(lb * LANE_BLOCK, (lb + 1) * LANE_BLOCK))
                for lb in range(A_DIM // LANE_BLOCK) for r in range(0, tt, rb)]


def _conv_taps(win_s, sh_s, w_ref, offsets, out_s, tt, bias_ref=None):
    rb, blocks = _blocks(tt)
    for r, cols in blocks:
        acc = w_ref[0:1, cols] * _window_rows(win_s, sh_s, r + offsets[0], rb, cols)
        for k in range(1, len(offsets)):
            acc = acc + w_ref[k:k + 1, cols] * _window_rows(win_s, sh_s, r + offsets[k], rb, cols)
        out_s[r:r + rb, cols] = acc if bias_ref is None else acc + bias_ref[:, cols]


A_CAUSAL = [A_HALO - (A_TAPS - 1) + k for k in range(A_TAPS)]
A_ANTICAUSAL = [A_TAPS - 1 - k for k in range(A_TAPS)]


def _layer_norm_stats(x):
    mu = jnp.mean(x, axis=-1, keepdims=True)
    xc = x - mu
    rstd = lax.rsqrt(jnp.mean(xc * xc, axis=-1, keepdims=True) + LN_EPS)
    return xc * rstd, rstd


def _mixer_specs(seq, tt):
    tiles_per_seq = seq // tt
    halo_blocks = tt // A_HALO
    z_spec = pl.BlockSpec((tt, IN_EVEN), lambda i: (i, 0))
    zh_spec = pl.BlockSpec((A_HALO, IN_EVEN), lambda i: (jnp.maximum(i * halo_blocks - 1, 0), 0))
    return tiles_per_seq, z_spec, zh_spec


def _vec_spec(rows, cols):
    return pl.BlockSpec((rows, cols), lambda i: (0, 0))


def _mixer_fwd(z, seq, caw, cab, lag, lab, cbw, comm=None):
    t = z.shape[0]
    tt = _tile(seq, 256)
    tiles_per_seq, z_spec, zh_spec = _mixer_specs(seq, tt)

    def body(z_ref, zh_ref, caw_ref, cab_ref, lag_ref, lab_ref, cbw_ref, mix_ref, a2_ref, a1_s, cb_s, sh_s):
        first = (pl.program_id(0) % tiles_per_seq) == 0
        _mixer_windows(z_ref, zh_ref, first, a1_s, cb_s, tt)
        _shifted_copies(a1_s, sh_s, tt)
        _conv_taps(a1_s, sh_s, caw_ref, A_CAUSAL, a2_ref, tt, bias_ref=cab_ref)
        xhat, _ = _layer_norm_stats(a2_ref[...])
        a3 = xhat * lag_ref[...] + lab_ref[...]
        mix_ref[:, 0:A_DIM] = (a3 * _sigmoid(a3)).astype(BF16)
        cv = _causal_conv(cb_s, cbw_ref, B_TAPS, B_HALO, tt)
        mix_ref[:, A_DIM:A_DIM + B_DIM] = (z_ref[:, 2 * A_DIM:3 * A_DIM] * cv).astype(BF16)

    return _call(
        body, name="mixer_fwd", grid=(t // tt,),
        in_specs=[z_spec, zh_spec, _vec_spec(A_TAPS, A_DIM), _vec_spec(1, A_DIM), _vec_spec(1, A_DIM), _vec_spec(1, A_DIM),
                  _vec_spec(B_TAPS, B_DIM)],
        out_specs=[pl.BlockSpec((tt, A_DIM + B_DIM), lambda i: (i, 0)), pl.BlockSpec((tt, A_DIM), lambda i: (i, 0))],
        out_shape=[jax.ShapeDtypeStruct((t, A_DIM + B_DIM), BF16), jax.ShapeDtypeStruct((t, A_DIM), F32)],
        scratch_shapes=[pltpu.VMEM((A_HALO + tt, A_DIM), F32), pltpu.VMEM((B_HALO + tt, B_DIM), F32),
                        pltpu.VMEM((SUBLANES - 1, tt + SHIFT_ROWS, A_DIM), F32)],
        args=(z, z, caw, cab, lag, lab, cbw), parallel=True, comm=comm)


def _mixer_bwd_local(z, a2, dmix, seq, lag, lab, comm=None):
    t = z.shape[0]
    tt = _tile(seq, 256)
    tiles_per_seq, z_spec, zh_spec = _mixer_specs(seq, tt)

    def body(z_ref, zh_ref, a2_ref, dmix_ref, lag_ref, lab_ref,
             da2_ref, dcv_ref, dcaw_ref, dcab_ref, dlag_ref, dlab_ref, dcbw_ref, a1_s, cb_s, sh_s):
        @pl.when(pl.program_id(0) == 0)
        def _():
            for ref in (dcaw_ref, dcab_ref, dlag_ref, dlab_ref, dcbw_ref):
                ref[...] = jnp.zeros_like(ref)

        first = (pl.program_id(0) % tiles_per_seq) == 0
        _mixer_windows(z_ref, zh_ref, first, a1_s, cb_s, tt)
        _shifted_copies(a1_s, sh_s, tt)
        xhat, rstd = _layer_norm_stats(a2_ref[...])
        a3 = xhat * lag_ref[...] + lab_ref[...]
        s3 = _sigmoid(a3)
        da3 = dmix_ref[:, 0:A_DIM] * (s3 * (1.0 + a3 * (1.0 - s3)))
        dlag_ref[...] += _colsum(da3 * xhat)
        dlab_ref[...] += _colsum(da3)
        dxh = da3 * lag_ref[...]
        da2 = rstd * (dxh - jnp.mean(dxh, axis=-1, keepdims=True) - xhat * jnp.mean(dxh * xhat, axis=-1, keepdims=True))
        da2_ref[...] = da2
        dcab_ref[...] += _colsum(da2)
        rb, blocks = _blocks(tt)
        for r, cols in blocks:
            da2_b = da2_ref[r:r + rb, cols]
            for k in range(A_TAPS):
                dcaw_ref[k:k + 1, cols] += _colsum(da2_b * _window_rows(a1_s, sh_s, r + A_CAUSAL[k], rb, cols))
        dcv =dmix_ref[:, A_DIM:A_DIM + B_DIM] * z_ref[:, 2 * A_DIM:3 * A_DIM]
        dcv_ref[...] = dcv
        for k in range(B_TAPS):
            dcbw_ref[k:k + 1, :] += _colsum(dcv * cb_s[pl.ds(B_HALO - (B_TAPS - 1) + k, tt), :])

    half = pl.BlockSpec((tt, A_DIM), lambda i: (i, 0))
    return _call(
        body, name="mixer_bwd_local", grid=(t // tt,),
        in_specs=[z_spec, zh_spec, half, pl.BlockSpec((tt, A_DIM + B_DIM), lambda i: (i, 0)),
                  _vec_spec(1, A_DIM), _vec_spec(1, A_DIM)],
        out_specs=[half, half, _vec_spec(A_TAPS, A_DIM), _vec_spec(1, A_DIM), _vec_spec(1, A_DIM), _vec_spec(1, A_DIM),
                   _vec_spec(B_TAPS, B_DIM)],
        out_shape=[jax.ShapeDtypeStruct((t, A_DIM), F32), jax.ShapeDtypeStruct((t, B_DIM), F32),
                   jax.ShapeDtypeStruct((A_TAPS, A_DIM), F32), jax.ShapeDtypeStruct((1, A_DIM), F32),
                   jax.ShapeDtypeStruct((1, A_DIM), F32), jax.ShapeDtypeStruct((1, A_DIM), F32),
                   jax.ShapeDtypeStruct((B_TAPS, B_DIM), F32)],
        scratch_shapes=[pltpu.VMEM((A_HALO + tt, A_DIM), F32), pltpu.VMEM((B_HALO + tt, B_DIM), F32),
                        pltpu.VMEM((SUBLANES - 1, tt + SHIFT_ROWS, A_DIM), F32)],
        args=(z, z, a2, dmix, lag, lab), comm=comm)


def _mixer_bwd_input(z, dmix, da2, dcv, seq, caw, cbw, comm=None):
    t = z.shape[0]
    tt = _tile(seq, 256)
    tiles_per_seq, z_spec, zh_spec = _mixer_specs(seq, tt)
    a_blocks = tt // A_HALO
    b_blocks = tt // B_HALO
    last_a = t // A_HALO - 1
    last_b = t // B_HALO - 1

    def body(z_ref, zh_ref, dmix_ref, da2_ref, da2n_ref, dcv_ref, dcvn_ref, caw_ref, cbw_ref, dz_ref, a1_s, cb_s, da2_s, dcv_s,
             sh_s, da1_s):
        pos = pl.program_id(0) % tiles_per_seq
        first = pos == 0
        last = pos == tiles_per_seq - 1
        sig = _mixer_windows(z_ref, zh_ref, first, a1_s, cb_s, tt)
        da2_s[0:tt, :] = da2_ref[...]
        da2_s[tt:tt + A_HALO, :] = jnp.where(last, 0.0, da2n_ref[...])
        dcv_s[0:tt, :] = dcv_ref[...]
        dcv_s[tt:tt + B_HALO, :] = jnp.where(last, 0.0, dcvn_ref[...])
        _shifted_copies(da2_s, sh_s, tt)
        _conv_taps(da2_s, sh_s, caw_ref, A_ANTICAUSAL, da1_s, tt)
        da1 = da1_s[...]
        dz_ref[:, 0:A_DIM] = (da1 * sig).astype(BF16)
        dz_ref[:, A_DIM:2 * A_DIM] = (da1 * z_ref[:, 0:A_DIM] * sig * (1.0 - sig)).astype(BF16)
        cv = _causal_conv(cb_s, cbw_ref, B_TAPS, B_HALO, tt)
        dz_ref[:, 2 * A_DIM:3 * A_DIM] = (dmix_ref[:, A_DIM:A_DIM + B_DIM] * cv).astype(BF16)
        dcb = cbw_ref[0:1, :] * dcv_s[pl.ds(B_TAPS - 1, tt), :]
        for k in range(1, B_TAPS):
            dcb = dcb + cbw_ref[k:k + 1, :] * dcv_s[pl.ds(B_TAPS - 1 - k, tt), :]
        dz_ref[:, 3 * A_DIM:4 * A_DIM] = (dcb * z_ref[:, 4 * A_DIM:5 * A_DIM]).astype(BF16)
        dz_ref[:, 4 * A_DIM:5 * A_DIM] = (dcb * z_ref[:, 3 * A_DIM:4 * A_DIM]).astype(BF16)

    half = pl.BlockSpec((tt, A_DIM), lambda i: (i, 0))
    return _call(
        body, name="mixer_bwd_input", grid=(t // tt,),
        in_specs=[z_spec, zh_spec, pl.BlockSpec((tt, A_DIM + B_DIM), lambda i: (i, 0)),
                  half, pl.BlockSpec((A_HALO, A_DIM), lambda i: (jnp.minimum((i + 1) * a_blocks, last_a), 0)),
                  half, pl.BlockSpec((B_HALO, B_DIM), lambda i: (jnp.minimum((i + 1) * b_blocks, last_b), 0)),
                  _vec_spec(A_TAPS, A_DIM), _vec_spec(B_TAPS, B_DIM)],
        out_specs=[pl.BlockSpec((tt, IN_EVEN), lambda i: (i, 0))],
        out_shape=[jax.ShapeDtypeStruct((t, IN_EVEN), BF16)],
        scratch_shapes=[pltpu.VMEM((A_HALO + tt, A_DIM), F32), pltpu.VMEM((B_HALO + tt, B_DIM), F32),
                        pltpu.VMEM((tt + A_HALO, A_DIM), F32), pltpu.VMEM((tt + B_HALO, B_DIM), F32),
                        pltpu.VMEM((SUBLANES - 1, tt + SHIFT_ROWS, A_DIM), F32), pltpu.VMEM((tt, A_DIM), F32)],
        args=(z, z, dmix, da2, da2, dcv, dcv, caw, cbw), parallel=True, comm=comm)[0]


def _tril_ws(ws_ref, g):
    rows = lax.broadcasted_iota(jnp.int32, (CHUNK, CHUNK), 0)
    cols = lax.broadcasted_iota(jnp.int32, (CHUNK, CHUNK), 1)
    return jnp.where(rows >= cols, ws_ref[g], 0.0).astype(BF16), rows >= cols


def _sgu_fwd(pre, lvg, lvb, ws, bs_b, comm=None):
    t = pre.shape[0]
    tt = _tile(t, 256)

    def body(pre_ref, lvg_ref, lvb_ref, ws_ref, bsb_ref, y_ref):
        vhat, _ = _layer_norm_stats(_gelu(pre_ref[:, C_DIM:2 * C_DIM]))
        vl = (vhat * lvg_ref[...] + lvb_ref[...]).astype(BF16)
        for g in range(C_GROUPS):
            w, _ = _tril_ws(ws_ref, g)
            cols = slice(g * CHUNK, (g + 1) * CHUNK)
            for ci in range(tt // CHUNK):
                rows = slice(ci * CHUNK, (ci + 1) * CHUNK)
                sv = jnp.dot(w, vl[rows, cols], preferred_element_type=F32) + bsb_ref[g]
                y_ref[rows, cols] = (_gelu(pre_ref[rows, cols]) * sv).astype(BF16)

    group = pl.BlockSpec((C_GROUPS, CHUNK, CHUNK), lambda i: (0, 0, 0))
    return _call(
        body, name="sgu_fwd", grid=(t // tt,),
        in_specs=[pl.BlockSpec((tt, 2 * C_DIM), lambda i: (i, 0)), _vec_spec(1, C_DIM), _vec_spec(1, C_DIM), group, group],
        out_specs=[pl.BlockSpec((tt, C_DIM), lambda i: (i, 0))], out_shape=[jax.ShapeDtypeStruct((t, C_DIM), BF16)],
        args=(pre, lvg, lvb, ws, bs_b), parallel=True, comm=comm)[0]


def _sgu_bwd(pre, dy, lvg, lvb, ws, bs_b, comm=None):
    t = pre.shape[0]
    tt = _tile(t, 256)

    def body(pre_ref, dy_ref, lvg_ref, lvb_ref, ws_ref, bsb_ref, dpre_ref, dws_ref, dbsb_ref, dlvg_ref, dlvb_ref, dbin_ref,
             dvl_s):
        @pl.when(pl.program_id(0) == 0)
        def _():
            for ref in (dws_ref, dbsb_ref, dlvg_ref, dlvb_ref, dbin_ref):
                ref[...] = jnp.zeros_like(ref)

        v, v_grad = _gelu_and_grad(pre_ref[:, C_DIM:2 * C_DIM])
        vhat, rstd = _layer_norm_stats(v)
        vl = (vhat * lvg_ref[...] + lvb_ref[...]).astype(BF16)
        for g in range(C_GROUPS):
            w, keep = _tril_ws(ws_ref, g)
            cols = slice(g * CHUNK, (g + 1) * CHUNK)
            dws = jnp.zeros((CHUNK, CHUNK), F32)
            dbs = jnp.zeros((CHUNK, 1), F32)
            for ci in range(tt // CHUNK):
                rows = slice(ci * CHUNK, (ci + 1) * CHUNK)
                vl_g = vl[rows, cols]
                sv = jnp.dot(w, vl_g, preferred_element_type=F32) + bsb_ref[g]
                u, u_grad = _gelu_and_grad(pre_ref[rows, cols])
                dyv = dy_ref[rows, cols]
                du = dyv * sv * u_grad
                dpre_ref[rows, cols] = du.astype(BF16)
                dbin_ref[:, cols] += _colsum(du)
                dsv = dyv * u
                dbs = dbs + jnp.sum(dsv, axis=1, keepdims=True)
                dsv16 = dsv.astype(BF16)
                dws = dws + lax.dot_general(dsv16, vl_g, (((1,), (1,)), ((), ())), preferred_element_type=F32)
                dvl_s[rows, cols] = lax.dot_general(w, dsv16, (((0,), (0,)), ((), ())), preferred_element_type=F32)
            dws_ref[g] += jnp.where(keep, dws, 0.0)
            dbsb_ref[g] += dbs
        dvl = dvl_s[...]
        dlvg_ref[...] += _colsum(dvl * vhat)
        dlvb_ref[...] += _colsum(dvl)
        dxh = dvl * lvg_ref[...]
        dv = rstd * (dxh - jnp.mean(dxh, axis=-1, keepdims=True) - vhat * jnp.mean(dxh * vhat, axis=-1, keepdims=True))
        dpv = dv * v_grad
        dpre_ref[:, C_DIM:2 * C_DIM] = dpv.astype(BF16)
        dbin_ref[:, C_DIM:2 * C_DIM] += _colsum(dpv)

    group = pl.BlockSpec((C_GROUPS, CHUNK, CHUNK), lambda i: (0, 0, 0))
    return _call(
        body, name="sgu_bwd", grid=(t // tt,),
        in_specs=[pl.BlockSpec((tt, 2 * C_DIM), lambda i: (i, 0)), pl.BlockSpec((tt, C_DIM), lambda i: (i, 0)),
                  _vec_spec(1, C_DIM), _vec_spec(1, C_DIM), group, group],
        out_specs=[pl.BlockSpec((tt, 2 * C_DIM), lambda i: (i, 0)), group, group,
                   _vec_spec(1, C_DIM), _vec_spec(1, C_DIM), _vec_spec(1, 2 * C_DIM)],
        out_shape=[jax.ShapeDtypeStruct((t, 2 * C_DIM), BF16), jax.ShapeDtypeStruct((C_GROUPS, CHUNK, CHUNK), F32),
                   jax.ShapeDtypeStruct((C_GROUPS, CHUNK, CHUNK), F32), jax.ShapeDtypeStruct((1, C_DIM), F32),
                   jax.ShapeDtypeStruct((1, C_DIM), F32), jax.ShapeDtypeStruct((1, 2 * C_DIM), F32)],
        scratch_shapes=[pltpu.VMEM((tt, C_DIM), F32)],
        args=(pre, dy, lvg, lvb, ws, bs_b), comm=comm)


def _pair_sum(name, part, got, core):
    _, k, n = part.shape
    tk = _tile(k, 1024)

    def body(core_ref, p_ref, s_ref, o_ref):
        o_ref[...] = (p_ref[...] + s_ref[...]).astype(BF16)

    return pl.pallas_call(
        body, name=name,
        grid_spec=pltpu.PrefetchScalarGridSpec(
            num_scalar_prefetch=1, grid=(N_CHIP, k // tk),
            in_specs=[pl.BlockSpec((None, tk, n), lambda q, i, core_ref: (2 * q + core_ref[0], i, 0)),
                      pl.BlockSpec((None, tk, n), lambda q, i, core_ref: (q, i, 0))],
            out_specs=pl.BlockSpec((None, tk, n), lambda q, i, core_ref: (q, i, 0))),
        out_shape=jax.ShapeDtypeStruct((N_CHIP, k, n), BF16),
        compiler_params=pltpu.CompilerParams(dimension_semantics=("parallel", "parallel"), vmem_limit_bytes=VMEM_LIMIT_BYTES),
    )(core, part, got)


def _adamw_math(w, g, m, v):
    m = ADAM_B1 * m + (1.0 - ADAM_B1) * g
    v = ADAM_B2 * v + (1.0 - ADAM_B2) * (g * g)
    m_hat = m / (1.0 - ADAM_B1 ** ADAM_STEP)
    v_hat = v / (1.0 - ADAM_B2 ** ADAM_STEP)
    delta = -ADAM_LR * (m_hat / (jnp.sqrt(v_hat) + ADAM_EPS) + ADAM_WD * w)
    return delta, m, v


def _sum_adamw(name, parts, w, m, v, comm=None):
    layers = len(parts)
    n_parts, k, n = parts[0].shape
    tk = _tile(k, 256)

    def body(*refs):
        p_refs = refs[:layers]
        w_ref, m_ref, v_ref, g_ref, d_ref, nm_ref, nv_ref = refs[layers:]

        def total(p_ref):
            g = p_ref[0].astype(F32)
            for q in range(1, n_parts):
                g = g + p_ref[q].astype(F32)
            return g

        g = total(p_refs[0])
        for l in range(1, layers):
            g = jnp.where(pl.program_id(0) == l, total(p_refs[l]), g)
        g_ref[...] = g
        d_ref[...], nm_ref[...], nv_ref[...] = _adamw_math(w_ref[...], g, m_ref[...], v_ref[...])

    blk = pl.BlockSpec((None, tk, n), lambda l, i: (l, i, 0))
    return _call(
        body, name=name, grid=(layers, k // tk),
        in_specs=[pl.BlockSpec((n_parts, tk, n), lambda l, i: (0, i, 0))] * layers + [blk, blk, blk], out_specs=[blk] * 4,
        out_shape=[jax.ShapeDtypeStruct((layers, k, n), F32)] * 4, args=(*parts, w, m, v), parallel=True, comm=comm)


def _small_adamw(name, parts, w, m, v, losses):
    count = len(parts)

    def in_order(ref):
        total = ref[0]
        for dev in range(1, N_DEV):
            total = total + ref[dev]
        return total

    def body(*refs):
        p_refs, w_refs, m_refs, v_refs = (refs[j * count:(j + 1) * count] for j in range(4))
        losses_ref = refs[4 * count]
        g_refs, d_refs, nm_refs, nv_refs = (refs[4 * count + 1 + j * count:4 * count + 1 + (j + 1) * count] for j in range(4))
        loss_ref = refs[8 * count + 1]
        for i in range(count):
            g = in_order(p_refs[i])
            g_refs[i][...] = g
            d_refs[i][...], nm_refs[i][...], nv_refs[i][...] = _adamw_math(w_refs[i][...], g, m_refs[i][...], v_refs[i][...])
        loss_ref[...] = in_order(losses_ref)

    res = pl.pallas_call(
        body, name=name,
        out_shape=[jax.ShapeDtypeStruct(a.shape, F32) for a in w] * 4 + [jax.ShapeDtypeStruct(losses.shape[1:], F32)],
        compiler_params=pltpu.CompilerParams(vmem_limit_bytes=VMEM_LIMIT_BYTES))(*parts, *w, *m, *v, losses)
    return [res[j * count:(j + 1) * count] for j in range(4)], res[4 * count]


def _rows(a):
    return a.reshape(-1, a.shape[-1])


def _whole(gathered):
    return jnp.transpose(gathered, (1, 0, 2)).reshape(gathered.shape[1], -1)


SMALL =("ev_norm_g", "ev_conv_a_w", "ev_conv_a_b", "ev_ln_a_g", "ev_ln_a_b", "ev_conv_b_w", "od_norm_g", "od_b_in",
         "od_ln_v_g", "od_ln_v_b", "od_w_s", "od_b_s", "mlp_norm_g", "final_norm_g")
SMALL_SHARDED = ("ev_conv_a_w", "ev_conv_b_w", "od_norm_g", "od_b_in", "od_ln_v_g", "od_ln_v_b")
ORDER = ("ev_norm_g", "ev_w_in", "ev_conv_a_w", "ev_conv_a_b", "ev_ln_a_g", "ev_ln_a_b", "ev_conv_b_w", "ev_w_out",
         "od_norm_g", "od_w_in", "od_b_in", "od_ln_v_g", "od_ln_v_b", "od_w_s", "od_b_s", "od_w_out", "mlp_norm_g",
         "mlp_w1", "mlp_w2", "final_norm_g")


def kernel(x, ev_norm_g, ev_w_in, ev_conv_a_w, ev_conv_a_b, ev_ln_a_g, ev_ln_a_b, ev_conv_b_w, ev_w_out, od_norm_g, od_w_in, od_b_in, od_ln_v_g, od_ln_v_b, od_w_s, od_b_s, od_w_out, mlp_norm_g, mlp_w1, mlp_w2, final_norm_g, loss_target, m_ev_norm_g, m_ev_w_in, m_ev_conv_a_w, m_ev_conv_a_b, m_ev_ln_a_g, m_ev_ln_a_b, m_ev_conv_b_w, m_ev_w_out, m_od_norm_g, m_od_w_in, m_od_b_in, m_od_ln_v_g, m_od_ln_v_b, m_od_w_s, m_od_b_s, m_od_w_out, m_mlp_norm_g, m_mlp_w1, m_mlp_w2, m_final_norm_g, v_ev_norm_g, v_ev_w_in, v_ev_conv_a_w, v_ev_conv_a_b, v_ev_ln_a_g, v_ev_ln_a_b, v_ev_conv_b_w, v_ev_w_out, v_od_norm_g, v_od_w_in, v_od_b_in, v_od_ln_v_g, v_od_ln_v_b, v_od_w_s, v_od_b_s, v_od_w_out, v_mlp_norm_g, v_mlp_w1, v_mlp_w2, v_final_norm_g):
    W = dict(ev_norm_g=ev_norm_g, ev_w_in=ev_w_in, ev_conv_a_w=ev_conv_a_w, ev_conv_a_b=ev_conv_a_b, ev_ln_a_g=ev_ln_a_g,
             ev_ln_a_b=ev_ln_a_b, ev_conv_b_w=ev_conv_b_w, ev_w_out=ev_w_out, od_norm_g=od_norm_g, od_w_in=od_w_in,
             od_b_in=od_b_in, od_ln_v_g=od_ln_v_g, od_ln_v_b=od_ln_v_b, od_w_s=od_w_s, od_b_s=od_b_s, od_w_out=od_w_out,
             mlp_norm_g=mlp_norm_g, mlp_w1=mlp_w1, mlp_w2=mlp_w2, final_norm_g=final_norm_g)
    M = dict(ev_norm_g=m_ev_norm_g, ev_w_in=m_ev_w_in, ev_conv_a_w=m_ev_conv_a_w, ev_conv_a_b=m_ev_conv_a_b,
             ev_ln_a_g=m_ev_ln_a_g, ev_ln_a_b=m_ev_ln_a_b, ev_conv_b_w=m_ev_conv_b_w, ev_w_out=m_ev_w_out,
             od_norm_g=m_od_norm_g, od_w_in=m_od_w_in, od_b_in=m_od_b_in, od_ln_v_g=m_od_ln_v_g, od_ln_v_b=m_od_ln_v_b,
             od_w_s=m_od_w_s, od_b_s=m_od_b_s, od_w_out=m_od_w_out, mlp_norm_g=m_mlp_norm_g, mlp_w1=m_mlp_w1,
             mlp_w2=m_mlp_w2, final_norm_g=m_final_norm_g)
    V = dict(ev_norm_g=v_ev_norm_g, ev_w_in=v_ev_w_in, ev_conv_a_w=v_ev_conv_a_w, ev_conv_a_b=v_ev_conv_a_b,
             ev_ln_a_g=v_ev_ln_a_g, ev_ln_a_b=v_ev_ln_a_b, ev_conv_b_w=v_ev_conv_b_w, ev_w_out=v_ev_w_out,
             od_norm_g=v_od_norm_g, od_w_in=v_od_w_in, od_b_in=v_od_b_in, od_ln_v_g=v_od_ln_v_g, od_ln_v_b=v_od_ln_v_b,
             od_w_s=v_od_w_s, od_b_s=v_od_b_s, od_w_out=v_od_w_out, mlp_norm_g=v_mlp_norm_g, mlp_w1=v_mlp_w1,
             mlp_w2=v_mlp_w2, final_norm_g=v_final_norm_g)

    n_seq, seq, d = x.shape
    t = n_seq * seq
    dev = 4 * lax.axis_index("x") + 2 * lax.axis_index("y") + lax.axis_index("c")
    core = lax.axis_index("c").astype(jnp.int32).reshape(1)

    ev_g, cab, lag, lab = W["ev_norm_g"], W["ev_conv_a_b"], W["ev_ln_a_g"], W["ev_ln_a_b"]
    ws = W["od_w_s"][0]
    bs_b = jnp.broadcast_to(W["od_b_s"][0][:, :, None], (C_GROUPS, CHUNK, CHUNK))
    mlp_g = [W["mlp_norm_g"][l:l + 1] for l in range(2)]
    fin_g = W["final_norm_g"].reshape(1, d)

    h0 = x.reshape(t, d)
    gather = _gather_comm([W["ev_w_in"][0].astype(BF16)])
    later = [(W["ev_w_out"], 0), (W["od_w_in"], 0), (W["od_w_out"], 0), (W["mlp_w1"], 0), (W["mlp_w2"], 0),
             (W["mlp_w1"], 1), (W["mlp_w2"], 1)]
    n0, (w_ev_out16, w_od_in16, w_od_out16, w1_0, w2_0, w1_1, w2_1) = _rms_fwd("ev_norm", h0, ev_g, comm=gather, casts=later)
    w_ev_in = _whole(gather.out[0])

    first, second = (0, 2), (1, 2)

    g_a, g_b = _gather_comm([w_ev_out16] + [_rows(W[n]) for n in SMALL_SHARDED]), _gather_comm([w1_0], first)
    z = _mm_nn("ev_in", n0, w_ev_in, IN_EVEN, _ep_store, [F32], comm=[g_a, g_b])[0]
    w_ev_out = g_a.out[0].reshape(D_MODEL, D_MODEL)
    caw, cbw, od_g, od_bin, lvg, lvb = [_whole(g) for g in g_a.out[1:]]

    g_c, g_d = _gather_comm([w1_0], second, into=g_b.out), _gather_comm([w2_0], first)
    mix, a2 = _mixer_fwd(z, seq, caw, cab, lag, lab, cbw, comm=[g_c, g_d])
    w1 = [_whole(g_c.out[0]), None]

    g_e = _gather_comm([w_od_in16], first)
    h1, n1 = _mm_nn("ev_out", mix, w_ev_out, d, _ep_residual_norm, [F32, BF16], extras=(h0,), rows=(mlp_g[0],), comm=g_e)

    g_f, g_f2 = _gather_comm([w2_0], second, into=g_d.out), _gather_comm([w_od_in16], second, into=g_e.out)
    q0 = _mm_nn("mlp0_up", n1, w1[0], D_FF, _ep_relu_sq, [BF16], comm=[g_f, g_f2])[0]
    w2 = [g_f.out[0].reshape(D_FF, D_MODEL), None]
    w_od_in = _whole(g_f2.out[0])

    g_g, g_g2 = _gather_comm([w1_1], first), _gather_comm([w_od_out16])
    h2, n2 = _mm_nn("mlp0_down", q0, w2[0], d, _ep_residual_norm, [F32, BF16], extras=(h1,), rows=(od_g,), comm=[g_g, g_g2])
    w_od_out = g_g2.out[0].reshape(D_MODEL, D_MODEL)
    g_h = _gather_comm([w1_1], second, into=g_g.out)
    pre = _mm_nn("od_in", n2, w_od_in, 2 * C_DIM, _ep_bias, [F32], rows=(od_bin,), comm=g_h)[0]
    w1[1] = _whole(g_h.out[0])
    g_i = _gather_comm([w2_1], (0, 4))
    y = _sgu_fwd(pre, lvg, lvb, ws, bs_b, comm=g_i)
    g_i2 = _gather_comm([w2_1], (1, 4), into=g_i.out)
    h3, n3 = _mm_nn("od_out", y, w_od_out, d, _ep_residual_norm, [F32, BF16], extras=(h2,), rows=(mlp_g[1],), comm=g_i2)
    g_j = _gather_comm([w2_1], second, into=g_i2.out)
    q1 = _mm_nn("mlp1_up", n3, w1[1], D_FF, _ep_relu_sq, [BF16], comm=g_j)[0]
    w2[1] = g_j.out[0].reshape(D_FF, D_MODEL)
    grad, grad16, d_fin_g, loss_part = _mm_nn(
        "mlp1_down", q1, w2[1], d, _ep_final_loss, [F32, BF16], extras=(h3, loss_target.reshape(t, d)), rows=(fin_g,),
        sums=(True, False), tm_want=FUSED_ROWS)

    by_chip = {}

    def swap(name, parts):
        comm = _pair_comm([parts])
        comm.parts, comm.weight = parts, name
        return comm

    def exchange(swapped, halves=False):
        sums = _pair_sum(f"pair_sum_{swapped.weight}", swapped.parts, swapped.out[0], core)
        if not halves:
            comm = _chip_comm([sums])
            comm.weight = swapped.weight
            return comm
        comm = _chip_comm([sums], first)
        comm.sums, comm.weight = sums, swapped.weight
        return comm

    def rest(comm):
        other = _chip_comm([comm.sums], second, into=comm.out)
        other.weight = comm.weight
        return other

    def done(comm):
        by_chip[comm.weight] = comm.out[0]

    dw2_1 = _mm_tn("mlp1_dw2", q1, grad16).reshape(N_DEV, D_FF // N_DEV, D_MODEL)
    s_a = swap("w2_1", dw2_1)
    dp = _mm_nt("mlp1_dq", grad16, w2[1], D_FF, _ep_relu_sq_grad, [BF16], extras=(q1,), comm=s_a)[0]
    c_a = exchange(s_a, halves=True)
    dw1_1 = _mm_tn("mlp1_dw1", n3, dp, col_blocks=N_DEV, comm=c_a)
    c_a2, s_b = rest(c_a), swap("w1_1", dw1_1)
    dn = _mm_nt("mlp1_dn", dp, w1[1], d, _ep_store, [F32], comm=[c_a2, s_b])[0]
    done(c_a2)
    c_b = exchange(s_b, halves=True)
    grad, grad16, dg_mlp1 = _rms_bwd("mlp1_dn_norm", dn, h3, mlp_g[1], grad)
    d_od_out = _mm_tn("od_dw_out", y, grad16).reshape(N_DEV, D_MODEL // N_DEV, D_MODEL)
    s_c = swap("od_out", d_od_out)
    dy = _mm_nt("od_dy", grad16, w_od_out, C_DIM, _ep_store, [F32], comm=s_c)[0]
    c_c = exchange(s_c)
    dpre, d_ws, d_bsb, d_lvg, d_lvb, d_bin = _sgu_bwd(pre, dy, lvg, lvb, ws, bs_b, comm=[c_b, c_c])
    done(c_c)
    c_b2 = rest(c_b)
    d_od_in = _mm_tn("od_dw_in", n2, dpre, col_blocks=N_DEV, comm=c_b2)
    done(c_b2)
    s_d = swap("od_in", d_od_in)
    dn = _mm_nt("od_dn", dpre, w_od_in, d, _ep_store, [F32], comm=s_d)[0]
    c_d = exchange(s_d)
    grad, grad16, d_od_g = _rms_bwd("od_dn_norm", dn, h2, od_g, grad)
    group_1 = dict(od_norm_g=d_od_g, od_b_in=d_bin, od_ln_v_g=d_lvg, od_ln_v_b=d_lvb,
                   od_w_s=d_ws.reshape(C_GROUPS * CHUNK, CHUNK), od_b_s=d_bsb[:, :, 0], final_norm_g=d_fin_g,
                   mlp_norm_g_1=dg_mlp1, loss=jnp.broadcast_to(loss_part, (SUBLANES, LANE_BLOCK)))
    gather_1 = _gather_comm(list(group_1.values()))
    dw2_0 = _mm_tn("mlp0_dw2", q0, grad16, comm=c_d).reshape(N_DEV, D_FF // N_DEV, D_MODEL)
    done(c_d)
    s_e = swap("w2_0", dw2_0)
    dp = _mm_nt("mlp0_dq", grad16, w2[0], D_FF, _ep_relu_sq_grad, [BF16], extras=(q0,), comm=[s_e, gather_1])[0]
    c_e = exchange(s_e, halves=True)
    dw1_0 = _mm_tn("mlp0_dw1", n1, dp, col_blocks=N_DEV, comm=c_e)
    c_e2, s_f = rest(c_e), swap("w1_0", dw1_0)
    dn = _mm_nt("mlp0_dn", dp, w1[0], d, _ep_store, [F32], comm=[c_e2, s_f])[0]
    done(c_e2)
    c_f = exchange(s_f, halves=True)
    grad, grad16, dg_mlp0 = _rms_bwd("mlp0_dn_norm", dn, h1, mlp_g[0], grad)
    d_ev_out = _mm_tn("ev_dw_out", mix, grad16).reshape(N_DEV, D_MODEL // N_DEV, D_MODEL)
    s_g = swap("ev_out", d_ev_out)
    dmix = _mm_nt("ev_dmix", grad16, w_ev_out, A_DIM + B_DIM, _ep_store, [F32], comm=s_g)[0]
    c_g = exchange(s_g)
    da2, dcv, d_caw, d_cab, d_lag, d_lab, d_cbw = _mixer_bwd_local(z, a2, dmix, seq, lag, lab, comm=[c_f, c_g])
    done(c_g)
    c_f2 = rest(c_f)
    dz = _mixer_bwd_input(z, dmix, da2, dcv, seq, caw, cbw, comm=c_f2)
    done(c_f2)
    group_2 = dict(ev_conv_a_w=d_caw, ev_conv_a_b=d_cab, ev_ln_a_g=d_lag, ev_ln_a_b=d_lab, ev_conv_b_w=d_cbw,
                   mlp_norm_g_0=dg_mlp0)
    gather_2 = _gather_comm(list(group_2.values()))
    d_ev_in = _mm_tn("ev_dw_in", n0, dz, comm=gather_2)
    d_ev_in = jnp.transpose(d_ev_in.reshape(D_MODEL, N_DEV, IN_EVEN // N_DEV), (1, 0, 2))
    s_h = swap("ev_in", d_ev_in)
    dn = _mm_nt("ev_dn", dz, w_ev_in, d, _ep_store, [F32], comm=s_h)[0]
    c_h = exchange(s_h)
    grad_x, _, d_ev_g = _rms_bwd("ev_dn_norm", dn, h0, ev_g, grad, comm=c_h)
    done(c_h)

    last_gather = _gather_comm([d_ev_g])
    shard = {"od_w_out": ("od_out",), "mlp_w1": ("w1_0", "w1_1"), "mlp_w2": ("w2_0", "w2_1"), "od_w_in": ("od_in",),
             "ev_w_out": ("ev_out",), "ev_w_in": ("ev_in",)}
    carried = {"od_w_out": last_gather}
    out_g, out_d, out_m, out_v = {}, {}, {}, {}
    for name, keys in shard.items():
        out_g[name], out_d[name], out_m[name], out_v[name] = _sum_adamw(
            f"adamw_{name}", [by_chip[key] for key in keys], W[name], M[name], V[name], comm=carried.get(name))

    gathered = dict(zip(group_1, gather_1.out), **dict(zip(group_2, gather_2.out)), ev_norm_g=last_gather.out[0])
    gathered["mlp_norm_g"] = jnp.concatenate([gathered["mlp_norm_g_0"], gathered["mlp_norm_g_1"]], axis=1)
    mine = []
    for n in SMALL:
        g = gathered[n]
        if n in SMALL_SHARDED:
            width = W[n].shape[-1]
            g = lax.dynamic_slice_in_dim(g, dev * width, width, axis=2)
        mine.append(g)
    res, loss_tile = _small_adamw("adamw_small", mine, [_rows(W[n]) for n in SMALL], [_rows(M[n]) for n in SMALL],
                                  [_rows(V[n]) for n in SMALL], gathered["loss"])
    loss = loss_tile[0, 0]
    for store, values in zip((out_g, out_d, out_m, out_v), res):
        for n, value in zip(SMALL, values):
            store[n] = value.reshape(W[n].shape)

    return (loss, grad_x.reshape(n_seq, seq, d), *[out_g[n] for n in ORDER], *[out_d[n] for n in ORDER],
            *[out_m[n] for n in ORDER], *[out_v[n] for n in ORDER])
```

```python
import math

import jax
import jax.numpy as jnp
from jax import lax
from jax.experimental import pallas as pl
from jax.experimental.pallas import tpu as pltpu

F32 = jnp.float32
BF16 = jnp.bfloat16
MESH = pl.DeviceIdType.MESH

D_MODEL = 1024
A_DIM = 512
B_DIM = 512
IN_EVEN = 2 * A_DIM + 3 * B_DIM
A_TAPS = 31
B_TAPS = 3
CHUNK = 128
C_GROUPS = 8
C_DIM = 1024
D_FF = 4096
RMS_EPS = 1e-6
LN_EPS = 1e-5
N_DEV = 8
N_CHIP = 4

ADAM_LR = 0.001
ADAM_B1 = 0.9
ADAM_B2 = 0.999
ADAM_EPS = 1e-08
ADAM_WD = 0.01
ADAM_STEP = 10

A_HALO = 32
B_HALO = 8
VMEM_LIMIT_BYTES = 56 * 1024 * 1024
TINY = 1.1754944e-38
INV_SQRT2 = 1.0 / math.sqrt(2.0)
INV_SQRT_2PI = 1.0 / math.sqrt(2.0 * math.pi)
HBM_SPEC = pl.BlockSpec(memory_space=pltpu.HBM)


def _tile(n, want):
    t = min(n, want)
    while n % t:
        t //= 2
    return t


def _sigmoid(x):
    return 1.0 / (1.0 + jnp.exp(-x))


def _gelu(x):
    return 0.5 * x * (1.0 + lax.erf(x * INV_SQRT2))


def _gelu_and_grad(x):
    cdf = 0.5 * (1.0 + lax.erf(x * INV_SQRT2))
    return x * cdf, cdf + x * jnp.exp(-0.5 * x * x) * INV_SQRT_2PI


def _colsum(x):
    return jnp.sum(x, axis=0, keepdims=True)


class _Comm:
    def __init__(self, ins, out_shapes, sem_shapes, start, finish, middle=None, into=None, sibling=False, chips=False):
        self.ins, self.out_shapes, self.sem_shapes, self.start, self.finish = ins, out_shapes, sem_shapes, start, finish
        self.middle = middle
        self.sibling, self.chips = sibling, chips
        self.into = list(into) if into is not None else []
        self.out = None


def _piece_rows(rows, piece):
    if piece is None:
        return 0, rows
    i, n = piece
    return i * (rows // n), rows // n


MIDDLE_AT = 0.75
BARRIER_IDS = {(True, True): 0, (True, False): 1, (False, True): 2}
FUSED_ROWS = 512


def _call(body, *, name, grid, in_specs, out_specs, out_shape, args, scratch_shapes=(), parallel=False, comm=None):
    comms = [] if comm is None else (list(comm) if isinstance(comm, (list, tuple)) else [comm])
    if not comms:
        sem = ("parallel" if parallel else "arbitrary",) * len(grid)
        return pl.pallas_call(
            body, name=name, grid=grid, in_specs=list(in_specs), out_specs=list(out_specs), out_shape=list(out_shape),
            scratch_shapes=list(scratch_shapes),
            compiler_params=pltpu.CompilerParams(dimension_semantics=sem, vmem_limit_bytes=VMEM_LIMIT_BYTES),
        )(*args)
    n_in, n_out, n_scr = len(in_specs), len(out_shape), len(scratch_shapes)
    c_ins_all = [a for cm in comms for a in cm.ins]
    c_into_all = [a for cm in comms for a in cm.into]
    c_out_shapes = [s for cm in comms for s in cm.out_shapes]
    c_sem_shapes = [s for cm in comms for s in cm.sem_shapes]
    aliases, in_pos, out_pos = {}, n_in + len(c_ins_all), n_out
    for cm in comms:
        for j in range(len(cm.into)):
            aliases[in_pos + j] = out_pos + j
        in_pos += len(cm.into)
        out_pos += len(cm.out_shapes)
    to_sibling = any(cm.sibling for cm in comms)
    to_chips = any(cm.chips for cm in comms)
    steps = grid
    total = math.prod(steps)
    first_step = (0,) * len(steps)
    last_step = tuple(s - 1 for s in steps)
    middle_step = None
    if 0 < int(MIDDLE_AT * total) < total - 1:
        rest, idx = int(MIDDLE_AT * total), []
        for s in reversed(steps):
            idx.append(rest % s)
            rest //= s
        middle_step = tuple(reversed(idx))

    def carrying(*refs):
        pos = 0
        ins = refs[pos:pos + n_in]; pos += n_in
        c_ins = refs[pos:pos + len(c_ins_all)]; pos += len(c_ins_all) + len(c_into_all)
        outs = refs[pos:pos + n_out]; pos += n_out
        c_outs = refs[pos:pos + len(c_out_shapes)]; pos += len(c_out_shapes)
        scr = refs[pos:pos + n_scr]; pos += n_scr
        c_sems = refs[pos:]
        views, i0, o0, s0 = [], 0, 0, 0
        for cm in comms:
            views.append((c_ins[i0:i0 + len(cm.ins)], c_outs[o0:o0 + len(cm.out_shapes)], c_sems[s0:s0 + len(cm.sem_shapes)]))
            i0, o0, s0 = i0 + len(cm.ins), o0 + len(cm.out_shapes), s0 + len(cm.sem_shapes)

        def at(step):
            hit = pl.program_id(0) == step[0]
            for axis in range(1, len(steps)):
                hit = jnp.logical_and(hit, pl.program_id(axis) == step[axis])
            return hit

        @pl.when(at(first_step))
        def _():
            x, y, c, chips = _place()
            peers = ([(x, y, 1 - c)] if to_sibling else []) + ([(*chip, c) for chip in chips] if to_chips else [])
            barrier = pltpu.get_barrier_semaphore()
            for peer in peers:
                pl.semaphore_signal(barrier, inc=1, device_id=peer, device_id_type=MESH)
            pl.semaphore_wait(barrier, len(peers))
            for cm, view in zip(comms, views):
                cm.start(*view)

        if middle_step is not None:
            @pl.when(at(middle_step))
            def _():
                for cm, view in zip(comms, views):
                    if cm.middle is not None:
                        cm.middle(*view)

        body(*ins, *outs, *scr)

        @pl.when(at(last_step))
        def _():
            for cm, view in zip(comms, views):
                if cm.middle is not None and middle_step is None:
                    cm.middle(*view)
            for cm, view in zip(comms, views):
                cm.finish(*view)

    res = pl.pallas_call(
        carrying, name=name, grid=grid,
        in_specs=[*in_specs, *[HBM_SPEC] * (len(c_ins_all) + len(c_into_all))],
        out_specs=[*out_specs, *[HBM_SPEC] * len(c_out_shapes)],
        out_shape=[*out_shape, *c_out_shapes], scratch_shapes=[*scratch_shapes, *c_sem_shapes],
        input_output_aliases=aliases,
        compiler_params=pltpu.CompilerParams(dimension_semantics=("arbitrary",) * len(grid), vmem_limit_bytes=VMEM_LIMIT_BYTES,
                                             collective_id=BARRIER_IDS[to_sibling, to_chips]),
    )(*args, *c_ins_all, *c_into_all)
    pos = n_out
    for cm in comms:
        cm.out = list(res[pos:pos + len(cm.out_shapes)])
        pos += len(cm.out_shapes)
    return list(res[:n_out])


def _place():
    x, y, c = lax.axis_index("x"), lax.axis_index("y"), lax.axis_index("c")
    return x, y, c, [(1 - x, y), (x, 1 - y), (1 - x, 1 - y)]


def _gather_comm(shards, piece=None, into=None):
    nw = len(shards)
    spans = [_piece_rows(s.shape[0], piece) for s in shards]

    def plan(ins, outs, sems):
        send_sems, recv_sems, local_sems = sems
        x, y, c, chips = _place()
        me, sibling = (x, y, c), (x, y, 1 - c)

        def slot(w, p):
            return outs[w].at[4 * p[0] + 2 * p[1] + p[2], pl.ds(*spans[w])]

        def mine(w):
            return ins[w].at[pl.ds(*spans[w])]

        def copy(w, k, block, to, src=None):
            return pltpu.make_async_remote_copy(
                src_ref=slot(w, block) if src is None else src, dst_ref=slot(w, block),
                send_sem=send_sems.at[w, k], recv_sem=recv_sems.at[w, k], device_id=to, device_id_type=MESH)

        local = [pltpu.make_async_copy(mine(w), slot(w, me), local_sems.at[w]) for w in range(nw)]
        first = [[copy(w, 0, me, sibling, src=mine(w))] + [copy(w, 1 + j, me, (*chip, c), src=mine(w)) for j, chip in enumerate(chips)]
                 for w in range(nw)]
        landed = [[copy(w, 1 + j, (*chip, c), me) for j, chip in enumerate(chips)] for w in range(nw)]
        passed = [[copy(w, 4 + j, (*chip, c), sibling) for j, chip in enumerate(chips)] for w in range(nw)]
        from_sibling = [[copy(w, 0, sibling, me)] + [copy(w, 4 + j, (*chip, 1 - c), me) for j, chip in enumerate(chips)]
                        for w in range(nw)]
        return local, first, landed, passed, from_sibling

    def start(ins, outs, sems):
        local, first, _, _, _ = plan(ins, outs, sems)
        for cp in local:
            cp.start()
        for row in first:
            for cp in row:
                cp.start()

    def middle(ins, outs, sems):
        _, _, landed, passed, _ = plan(ins, outs, sems)
        for w in range(nw):
            for j in range(3):
                landed[w][j].wait_recv()
                passed[w][j].start()

    def finish(ins, outs, sems):
        local, first, landed, passed, from_sibling = plan(ins, outs, sems)
        for w in range(nw):
            for cp in from_sibling[w]:
                cp.wait_recv()
        for w in range(nw):
            for cp in first[w] + passed[w]:
                cp.wait_send()
        for cp in local:
            cp.wait()

    return _Comm(list(shards), [jax.ShapeDtypeStruct((N_DEV, *s.shape), s.dtype) for s in shards],
                 [pltpu.SemaphoreType.DMA((nw, 7)), pltpu.SemaphoreType.DMA((nw, 7)), pltpu.SemaphoreType.DMA((nw,))],
                 start, finish, middle, into=into, sibling=True, chips=True)


def _pair_comm(parts):
    nw = len(parts)

    def plan(ins, outs, sems):
        send_sems, recv_sems = sems
        x, y, c, _ = _place()
        return [pltpu.make_async_remote_copy(
            src_ref=ins[w].at[2 * q + 1 - c], dst_ref=outs[w].at[q], send_sem=send_sems.at[w, q], recv_sem=recv_sems.at[w, q],
            device_id=(x, y, 1 - c), device_id_type=MESH) for w in range(nw) for q in range(N_CHIP)]

    def start(ins, outs, sems):
        for cp in plan(ins, outs, sems):
            cp.start()

    def finish(ins, outs, sems):
        for cp in plan(ins, outs, sems):
            cp.wait()

    return _Comm(list(parts), [jax.ShapeDtypeStruct((N_CHIP, *p.shape[1:]), p.dtype) for p in parts],
                 [pltpu.SemaphoreType.DMA((nw, N_CHIP)), pltpu.SemaphoreType.DMA((nw, N_CHIP))], start, finish, sibling=True)


def _chip_comm(sums, piece=None, into=None):
    nw = len(sums)
    spans = [_piece_rows(s.shape[1], piece) for s in sums]

    def plan(ins, outs, sems):
        send_sems, recv_sems, local_sems = sems
        x, y, c, chips = _place()
        my_chip = 2 * x + y
        local = [pltpu.make_async_copy(ins[w].at[my_chip, pl.ds(*spans[w])], outs[w].at[my_chip, pl.ds(*spans[w])],
                                       local_sems.at[w]) for w in range(nw)]
        remote = [pltpu.make_async_remote_copy(
            src_ref=ins[w].at[2 * chip[0] + chip[1], pl.ds(*spans[w])], dst_ref=outs[w].at[my_chip, pl.ds(*spans[w])],
            send_sem=send_sems.at[w, j], recv_sem=recv_sems.at[w, j], device_id=(*chip, c), device_id_type=MESH)
            for w in range(nw) for j, chip in enumerate(chips)]
        return local, remote

    def start(ins, outs, sems):
        local, remote = plan(ins, outs, sems)
        for cp in local + remote:
            cp.start()

    def finish(ins, outs, sems):
        local, remote = plan(ins, outs, sems)
        for cp in remote + local:
            cp.wait()

    return _Comm(list(sums), [jax.ShapeDtypeStruct(s.shape, s.dtype) for s in sums],
                 [pltpu.SemaphoreType.DMA((nw, 3)), pltpu.SemaphoreType.DMA((nw, 3)), pltpu.SemaphoreType.DMA((nw,))],
                 start, finish, into=into, chips=True)


def _matmul(name, a, b, *, kind, m, n, k, a_spec, b_spec, tm, tn, tk, out_shape, out_specs, epilogue,
            extras=(), extra_specs=(), comm=None):
    dims = {"nn": (((1,), (0,)), ((), ())), "nt": (((1,), (1,)), ((), ())), "tn": (((0,), (0,)), ((), ()))}[kind]
    nk = k // tk
    n_extra = len(extras)
    n_out = len(out_shape)

    def body(a_ref, b_ref, *rest):
        extra_refs = rest[:n_extra]
        out_refs = rest[n_extra:n_extra + n_out]
        part = lax.dot_general(a_ref[...], b_ref[...], dims, preferred_element_type=F32)
        if nk == 1:
            epilogue(part, extra_refs, out_refs)
            return
        acc_ref = rest[n_extra + n_out]
        step = pl.program_id(2)

        @pl.when(step == 0)
        def _():
            acc_ref[...] = part

        @pl.when(jnp.logical_and(step > 0, step < nk - 1))
        def _():
            acc_ref[...] += part

        @pl.when(step == nk - 1)
        def _():
            epilogue(acc_ref[...] + part, extra_refs, out_refs)

    return _call(
        body, name=name, grid=(m // tm, n // tn, nk), in_specs=[a_spec, b_spec, *extra_specs], out_specs=out_specs,
        out_shape=out_shape, scratch_shapes=[pltpu.VMEM((tm, tn), F32)] if nk > 1 else [], args=(a, b, *extras), comm=comm)


def _mm_operands(m, n, tm, tn, out_dtypes, extras, rows, sums):
    tile = pl.BlockSpec((tm, tn), lambda i, j, kk: (i, j))
    row = pl.BlockSpec((1, tn), lambda i, j, kk: (0, j))
    one = pl.BlockSpec((1, 1), lambda i, j, kk: (0, 0))
    out_shape = [jax.ShapeDtypeStruct((m, n), dt) for dt in out_dtypes]
    out_shape += [jax.ShapeDtypeStruct((1, n if wide else 1), F32) for wide in sums]
    out_specs = [tile] * len(out_dtypes) + [row if wide else one for wide in sums]
    return (*extras, *rows), [tile] * len(extras) + [row] * len(rows), out_shape, out_specs


def _row_tile(m, k, tm_want):
    return _tile(m, tm_want or (1024 if k <= 1024 else 512))


def _mm_nn(name, a, b, n, epilogue, out_dtypes, *, extras=(), rows=(), sums=(), tm_want=None, comm=None):
    m, k = a.shape
    blocked = b.ndim == 3
    tm = _row_tile(m, k, tm_want)
    tn = b.shape[-1] if blocked else _tile(n, 1024)
    tk = k
    extras, extra_specs, out_shape, out_specs = _mm_operands(m, n, tm, tn, out_dtypes, extras, rows, sums)
    if blocked:
        b_spec = pl.BlockSpec((None, tk, tn), lambda i, j, kk: (j, kk, 0))
    else:
        b_spec = pl.BlockSpec((tk, tn), lambda i, j, kk: (kk, j))
    return _matmul(
        name, a, b, kind="nn", m=m, n=n, k=k, tm=tm, tn=tn, tk=tk,
        a_spec=pl.BlockSpec((tm, tk), lambda i, j, kk: (i, kk)), b_spec=b_spec, out_shape=out_shape, out_specs=out_specs,
        epilogue=epilogue, extras=extras, extra_specs=extra_specs, comm=comm)


def _mm_nt(name, a, b, n, epilogue, out_dtypes, *, extras=(), rows=(), sums=(), tm_want=None, comm=None):
    m, k = a.shape
    blocked = b.ndim == 3
    tm = _row_tile(m, k, tm_want)
    tn = _tile(n, 1024)
    tk = b.shape[-1] if blocked else k
    extras, extra_specs, out_shape, out_specs = _mm_operands(m, n, tm, tn, out_dtypes, extras, rows, sums)
    if blocked:
        b_spec = pl.BlockSpec((None, tn, tk), lambda i, j, kk: (kk, j, 0))
    else:
        b_spec = pl.BlockSpec((tn, tk), lambda i, j, kk: (j, kk))
    return _matmul(
        name, a, b, kind="nt", m=m, n=n, k=k, tm=tm, tn=tn, tk=tk,
        a_spec=pl.BlockSpec((tm, tk), lambda i, j, kk: (i, kk)), b_spec=b_spec, out_shape=out_shape, out_specs=out_specs,
        epilogue=epilogue, extras=extras, extra_specs=extra_specs, comm=comm)


def _mm_tn(name, a, b, *, col_blocks=0, comm=None):
    t, k = a.shape
    n = b.shape[1]
    tm = _row_tile(k, t, None)
    tn = n // col_blocks if col_blocks else _tile(n, 1024)
    tk = t
    if col_blocks:
        out_shape = [jax.ShapeDtypeStruct((col_blocks, k, tn), F32)]
        out_specs = [pl.BlockSpec((None, tm, tn), lambda i, j, kk: (j, i, 0))]
    else:
        out_shape = [jax.ShapeDtypeStruct((k, n), F32)]
        out_specs = [pl.BlockSpec((tm, tn), lambda i, j, kk: (i, j))]

    def epilogue(acc, extra_refs, out_refs):
        out_refs[0][...] = acc

    return _matmul(
        name, a, b, kind="tn", m=k, n=n, k=t, tm=tm, tn=tn, tk=tk,
        a_spec=pl.BlockSpec((tk, tm), lambda i, j, kk: (kk, i)), b_spec=pl.BlockSpec((tk, tn), lambda i, j, kk: (kk, j)),
        out_shape=out_shape, out_specs=out_specs, epilogue=epilogue, comm=comm)[0]


def _ep_store(acc, extra_refs, out_refs):
    out_refs[0][...] = acc.astype(out_refs[0].dtype)


def _ep_bias(acc, extra_refs, out_refs):
    out_refs[0][...] = acc + extra_refs[0][...]


def _rms(x):
    r = lax.rsqrt(jnp.mean(x * x, axis=-1, keepdims=True) + RMS_EPS)
    return r, x * r


def _rms_grad(dn, r, xr, g):
    dy = dn * g
    return r * (dy - xr * jnp.mean(dy * xr, axis=-1, keepdims=True))


def _ep_residual_norm(acc, extra_refs, out_refs):
    h = extra_refs[0][...] + acc
    out_refs[0][...] = h
    _, xr = _rms(h)
    out_refs[1][...] = (xr * extra_refs[1][...]).astype(BF16)


def _ep_final_loss(acc, extra_refs, out_refs):
    @pl.when(pl.program_id(0) == 0)
    def _():
        out_refs[2][...] = jnp.zeros_like(out_refs[2])
        out_refs[3][...] = jnp.zeros_like(out_refs[3])

    h = extra_refs[0][...] + acc
    g = extra_refs[2][...]
    r, xr = _rms(h)
    err = xr * g - extra_refs[1][...]
    out_refs[3][...] += 0.5 * jnp.sum(jnp.mean(err * err, axis=-1, keepdims=True), axis=0, keepdims=True)
    dout = err * (1.0 / h.shape[-1])
    out_refs[2][...] += _colsum(dout * xr)
    out = _rms_grad(dout, r, xr, g)
    out_refs[0][...] = out
    out_refs[1][...] = out.astype(BF16)


def _ep_relu_sq(acc, extra_refs, out_refs):
    r = jnp.maximum(acc, 0.0)
    out_refs[0][...] = (r * r).astype(BF16)


def _ep_relu_sq_grad(acc, extra_refs, out_refs):
    q = extra_refs[0][...].astype(F32)
    out_refs[0][...] = (acc * (2.0 * q * lax.rsqrt(jnp.maximum(q, TINY)))).astype(BF16)


def _rms_fwd(name, h, g, comm=None, casts=()):
    t, d = h.shape
    tt = _tile(t, 512)
    steps = t // tt
    n_cast = len(casts)

    def body(h_ref, g_ref, *rest):
        n_ref = rest[n_cast]
        x = h_ref[...]
        r = lax.rsqrt(jnp.mean(x * x, axis=-1, keepdims=True) + RMS_EPS)
        n_ref[...] = (x * r * g_ref[...]).astype(BF16)
        for src, dst in zip(rest[:n_cast], rest[n_cast + 1:]):
            dst[...] = src[...].astype(BF16)

    cast_in = [pl.BlockSpec((None, w.shape[1] // steps, w.shape[2]), lambda i, l=l: (l, i, 0)) for w, l in casts]
    cast_out = [pl.BlockSpec((w.shape[1] // steps, w.shape[2]), lambda i: (i, 0)) for w, _ in casts]
    res = _call(
        body, name=name, grid=(steps,),
        in_specs=[pl.BlockSpec((tt, d), lambda i: (i, 0)), pl.BlockSpec((1, d), lambda i: (0, 0)), *cast_in],
        out_specs=[pl.BlockSpec((tt, d), lambda i: (i, 0)), *cast_out],
        out_shape=[jax.ShapeDtypeStruct((t, d), BF16)] + [jax.ShapeDtypeStruct(w.shape[1:], BF16) for w, _ in casts],
        args=(h, g, *[w for w, _ in casts]), parallel=True, comm=comm)
    return res[0], res[1:]


def _rms_bwd(name, dn, h, g, grad_in, comm=None):
    t, d = h.shape
    tt = _tile(t, 512)

    def body(dn_ref, h_ref, g_ref, gin_ref, gout_ref, gout16_ref, dg_ref):
        @pl.when(pl.program_id(0) == 0)
        def _():
            dg_ref[...] = jnp.zeros_like(dg_ref)

        dnv = dn_ref[...]
        r, xr = _rms(h_ref[...])
        dg_ref[...] += _colsum(dnv * xr)
        out = gin_ref[...] + _rms_grad(dnv, r, xr, g_ref[...])
        gout_ref[...] = out
        gout16_ref[...] = out.astype(BF16)

    row = pl.BlockSpec((tt, d), lambda i: (i, 0))
    vec = pl.BlockSpec((1, d), lambda i: (0, 0))
    return _call(
        body, name=name, grid=(t // tt,), in_specs=[row, row, vec, row], out_specs=[row, row, vec],
        out_shape=[jax.ShapeDtypeStruct((t, d), F32), jax.ShapeDtypeStruct((t, d), BF16), jax.ShapeDtypeStruct((1, d), F32)],
        args=(dn, h, g, grad_in), comm=comm)


def _mixer_windows(z_ref, zh_ref, first, a1_s, cb_s, tt):
    sig = _sigmoid(z_ref[:, A_DIM:2 * A_DIM])
    a1_s[A_HALO:A_HALO + tt, :] = z_ref[:, 0:A_DIM] * sig
    a1_h = zh_ref[:, 0:A_DIM] * _sigmoid(zh_ref[:, A_DIM:2 * A_DIM])
    a1_s[0:A_HALO, :] = jnp.where(first, 0.0, a1_h)
    cb_s[B_HALO:B_HALO + tt, :] = z_ref[:, 3 * A_DIM:4 * A_DIM] * z_ref[:, 4 * A_DIM:5 * A_DIM]
    cb_h = zh_ref[A_HALO - B_HALO:A_HALO, 3 * A_DIM:4 * A_DIM] * zh_ref[A_HALO - B_HALO:A_HALO, 4 * A_DIM:5 * A_DIM]
    cb_s[0:B_HALO, :] = jnp.where(first, 0.0, cb_h)
    return sig


def _causal_conv(win_s, w_ref, taps, halo, tt):
    base = halo - (taps - 1)
    acc = w_ref[0:1, :] * win_s[pl.ds(base, tt), :]
    for k in range(1, taps):
        acc = acc + w_ref[k:k + 1, :] * win_s[pl.ds(base + k, tt), :]
    return acc


SUBLANES = 8
LANE_BLOCK = 128
ROW_BLOCK = 128
SHIFT_ROWS = A_HALO - SUBLANES


def _shifted_copies(win_s, sh_s, tt):
    for b in range(1, SUBLANES):
        sh_s[b - 1] = win_s[pl.ds(b, tt + SHIFT_ROWS), :]


def _window_rows(win_s, sh_s, offset, rows, cols):
    b = offset % SUBLANES
    if b == 0:
        return win_s[pl.ds(offset, rows), cols]
    return sh_s[b - 1, pl.ds(offset - b, rows), cols]


def _blocks(tt):
    rb = min(tt, ROW_BLOCK)
    return rb, [(r, slice(lb * LANE_BLOCK, (lb + 1) * LANE_BLOCK))
                for lb in range(A_DIM // LANE_BLOCK) for r in range(0, tt, rb)]


def _conv_taps(win_s, sh_s, w_ref, offsets, out_s, tt, bias_ref=None):
    rb, blocks = _blocks(tt)
    for r, cols in blocks:
        acc = w_ref[0:1, cols] * _window_rows(win_s, sh_s, r + offsets[0], rb, cols)
        for k in range(1, len(offsets)):
            acc = acc + w_ref[k:k + 1, cols] * _window_rows(win_s, sh_s, r + offsets[k], rb, cols)
        out_s[r:r + rb, cols] = acc if bias_ref is None else acc + bias_ref[:, cols]


A_CAUSAL = [A_HALO - (A_TAPS - 1) + k for k in range(A_TAPS)]
A_ANTICAUSAL = [A_TAPS - 1 - k for k in range(A_TAPS)]


def _layer_norm_stats(x):
    mu = jnp.mean(x, axis=-1, keepdims=True)
    xc = x - mu
    rstd = lax.rsqrt(jnp.mean(xc * xc, axis=-1, keepdims=True) + LN_EPS)
    return xc * rstd, rstd


def _mixer_specs(seq, tt):
    tiles_per_seq = seq // tt
    halo_blocks = tt // A_HALO
    z_spec = pl.BlockSpec((tt, IN_EVEN), lambda i: (i, 0))
    zh_spec = pl.BlockSpec((A_HALO, IN_EVEN), lambda i: (jnp.maximum(i * halo_blocks - 1, 0), 0))
    return tiles_per_seq, z_spec, zh_spec


def _vec_spec(rows, cols):
    return pl.BlockSpec((rows, cols), lambda i: (0, 0))


def _mixer_fwd(z, seq, caw, cab, lag, lab, cbw, comm=None):
    t = z.shape[0]
    tt = _tile(seq, 256)
    tiles_per_seq, z_spec, zh_spec = _mixer_specs(seq, tt)

    def body(z_ref, zh_ref, caw_ref, cab_ref, lag_ref, lab_ref, cbw_ref, mix_ref, a2_ref, a1_s, cb_s, sh_s):
        first = (pl.program_id(0) % tiles_per_seq) == 0
        _mixer_windows(z_ref, zh_ref, first, a1_s, cb_s, tt)
        _shifted_copies(a1_s, sh_s, tt)
        _conv_taps(a1_s, sh_s, caw_ref, A_CAUSAL, a2_ref, tt, bias_ref=cab_ref)
        xhat, _ = _layer_norm_stats(a2_ref[...])
        a3 = xhat * lag_ref[...] + lab_ref[...]
        mix_ref[:, 0:A_DIM] = (a3 * _sigmoid(a3)).astype(BF16)
        cv = _causal_conv(cb_s, cbw_ref, B_TAPS, B_HALO, tt)
        mix_ref[:, A_DIM:A_DIM + B_DIM] = (z_ref[:, 2 * A_DIM:3 * A_DIM] * cv).astype(BF16)

    return _call(
        body, name="mixer_fwd", grid=(t // tt,),
        in_specs=[z_spec, zh_spec, _vec_spec(A_TAPS, A_DIM), _vec_spec(1, A_DIM), _vec_spec(1, A_DIM), _vec_spec(1, A_DIM),
                  _vec_spec(B_TAPS, B_DIM)],
        out_specs=[pl.BlockSpec((tt, A_DIM + B_DIM), lambda i: (i, 0)), pl.BlockSpec((tt, A_DIM), lambda i: (i, 0))],
        out_shape=[jax.ShapeDtypeStruct((t, A_DIM + B_DIM), BF16), jax.ShapeDtypeStruct((t, A_DIM), F32)],
        scratch_shapes=[pltpu.VMEM((A_HALO + tt, A_DIM), F32), pltpu.VMEM((B_HALO + tt, B_DIM), F32),
                        pltpu.VMEM((SUBLANES - 1, tt + SHIFT_ROWS, A_DIM), F32)],
        args=(z, z, caw, cab, lag, lab, cbw), parallel=True, comm=comm)


def _mixer_bwd_local(z, a2, dmix, seq, lag, lab, comm=None):
    t = z.shape[0]
    tt = _tile(seq, 256)
    tiles_per_seq, z_spec, zh_spec = _mixer_specs(seq, tt)

    def body(z_ref, zh_ref, a2_ref, dmix_ref, lag_ref, lab_ref,
             da2_ref, dcv_ref, dcaw_ref, dcab_ref, dlag_ref, dlab_ref, dcbw_ref, a1_s, cb_s, sh_s):
        @pl.when(pl.program_id(0) == 0)
        def _():
            for ref in (dcaw_ref, dcab_ref, dlag_ref, dlab_ref, dcbw_ref):
                ref[...] = jnp.zeros_like(ref)

        first = (pl.program_id(0) % tiles_per_seq) == 0
        _mixer_windows(z_ref, zh_ref, first, a1_s, cb_s, tt)
        _shifted_copies(a1_s, sh_s, tt)
        xhat, rstd = _layer_norm_stats(a2_ref[...])
        a3 = xhat * lag_ref[...] + lab_ref[...]
        s3 = _sigmoid(a3)
        da3 = dmix_ref[:, 0:A_DIM] * (s3 * (1.0 + a3 * (1.0 - s3)))
        dlag_ref[...] += _colsum(da3 * xhat)
        dlab_ref[...] += _colsum(da3)
        dxh = da3 * lag_ref[...]
        da2 = rstd * (dxh - jnp.mean(dxh, axis=-1, keepdims=True) - xhat * jnp.mean(dxh * xhat, axis=-1, keepdims=True))
        da2_ref[...] = da2
        dcab_ref[...] += _colsum(da2)
        rb, blocks = _blocks(tt)
        for r, cols in blocks:
            da2_b = da2_ref[r:r + rb, cols]
            for k in range(A_TAPS):
                dcaw_ref[k:k + 1, cols] += _colsum(da2_b * _window_rows(a1_s, sh_s, r + A_CAUSAL[k], rb, cols))
        dcv =dmix_ref[:, A_DIM:A_DIM + B_DIM] * z_ref[:, 2 * A_DIM:3 * A_DIM]
        dcv_ref[...] = dcv
        for k in range(B_TAPS):
            dcbw_ref[k:k + 1, :] += _colsum(dcv * cb_s[pl.ds(B_HALO - (B_TAPS - 1) + k, tt), :])

    half = pl.BlockSpec((tt, A_DIM), lambda i: (i, 0))
    return _call(
        body, name="mixer_bwd_local", grid=(t // tt,),
        in_specs=[z_spec, zh_spec, half, pl.BlockSpec((tt, A_DIM + B_DIM), lambda i: (i, 0)),
                  _vec_spec(1, A_DIM), _vec_spec(1, A_DIM)],
        out_specs=[half, half, _vec_spec(A_TAPS, A_DIM), _vec_spec(1, A_DIM), _vec_spec(1, A_DIM), _vec_spec(1, A_DIM),
                   _vec_spec(B_TAPS, B_DIM)],
        out_shape=[jax.ShapeDtypeStruct((t, A_DIM), F32), jax.ShapeDtypeStruct((t, B_DIM), F32),
                   jax.ShapeDtypeStruct((A_TAPS, A_DIM), F32), jax.ShapeDtypeStruct((1, A_DIM), F32),
                   jax.ShapeDtypeStruct((1, A_DIM), F32), jax.ShapeDtypeStruct((1, A_DIM), F32),
                   jax.ShapeDtypeStruct((B_TAPS, B_DIM), F32)],
        scratch_shapes=[pltpu.VMEM((A_HALO + tt, A_DIM), F32), pltpu.VMEM((B_HALO + tt, B_DIM), F32),
                        pltpu.VMEM((SUBLANES - 1, tt + SHIFT_ROWS, A_DIM), F32)],
        args=(z, z, a2, dmix, lag, lab), comm=comm)


def _mixer_bwd_input(z, dmix, da2, dcv, seq, caw, cbw, comm=None):
    t = z.shape[0]
    tt = _tile(seq, 256)
    tiles_per_seq, z_spec, zh_spec = _mixer_specs(seq, tt)
    a_blocks = tt // A_HALO
    b_blocks = tt // B_HALO
    last_a = t // A_HALO - 1
    last_b = t // B_HALO - 1

    def body(z_ref, zh_ref, dmix_ref, da2_ref, da2n_ref, dcv_ref, dcvn_ref, caw_ref, cbw_ref, dz_ref, a1_s, cb_s, da2_s, dcv_s,
             sh_s, da1_s):
        pos = pl.program_id(0) % tiles_per_seq
        first = pos == 0
        last = pos == tiles_per_seq - 1
        sig = _mixer_windows(z_ref, zh_ref, first, a1_s, cb_s, tt)
        da2_s[0:tt, :] = da2_ref[...]
        da2_s[tt:tt + A_HALO, :] = jnp.where(last, 0.0, da2n_ref[...])
        dcv_s[0:tt, :] = dcv_ref[...]
        dcv_s[tt:tt + B_HALO, :] = jnp.where(last, 0.0, dcvn_ref[...])
        _shifted_copies(da2_s, sh_s, tt)
        _conv_taps(da2_s, sh_s, caw_ref, A_ANTICAUSAL, da1_s, tt)
        da1 = da1_s[...]
        dz_ref[:, 0:A_DIM] = (da1 * sig).astype(BF16)
        dz_ref[:, A_DIM:2 * A_DIM] = (da1 * z_ref[:, 0:A_DIM] * sig * (1.0 - sig)).astype(BF16)
        cv = _causal_conv(cb_s, cbw_ref, B_TAPS, B_HALO, tt)
        dz_ref[:, 2 * A_DIM:3 * A_DIM] = (dmix_ref[:, A_DIM:A_DIM + B_DIM] * cv).astype(BF16)
        dcb = cbw_ref[0:1, :] * dcv_s[pl.ds(B_TAPS - 1, tt), :]
        for k in range(1, B_TAPS):
            dcb = dcb + cbw_ref[k:k + 1, :] * dcv_s[pl.ds(B_TAPS - 1 - k, tt), :]
        dz_ref[:, 3 * A_DIM:4 * A_DIM] = (dcb * z_ref[:, 4 * A_DIM:5 * A_DIM]).astype(BF16)
        dz_ref[:, 4 * A_DIM:5 * A_DIM] = (dcb * z_ref[:, 3 * A_DIM:4 * A_DIM]).astype(BF16)

    half = pl.BlockSpec((tt, A_DIM), lambda i: (i, 0))
    return _call(
        body, name="mixer_bwd_input", grid=(t // tt,),
        in_specs=[z_spec, zh_spec, pl.BlockSpec((tt, A_DIM + B_DIM), lambda i: (i, 0)),
                  half, pl.BlockSpec((A_HALO, A_DIM), lambda i: (jnp.minimum((i + 1) * a_blocks, last_a), 0)),
                  half, pl.BlockSpec((B_HALO, B_DIM), lambda i: (jnp.minimum((i + 1) * b_blocks, last_b), 0)),
                  _vec_spec(A_TAPS, A_DIM), _vec_spec(B_TAPS, B_DIM)],
        out_specs=[pl.BlockSpec((tt, IN_EVEN), lambda i: (i, 0))],
        out_shape=[jax.ShapeDtypeStruct((t, IN_EVEN), BF16)],
        scratch_shapes=[pltpu.VMEM((A_HALO + tt, A_DIM), F32), pltpu.VMEM((B_HALO + tt, B_DIM), F32),
                        pltpu.VMEM((tt + A_HALO, A_DIM), F32), pltpu.VMEM((tt + B_HALO, B_DIM), F32),
                        pltpu.VMEM((SUBLANES - 1, tt + SHIFT_ROWS, A_DIM), F32), pltpu.VMEM((tt, A_DIM), F32)],
        args=(z, z, dmix, da2, da2, dcv, dcv, caw, cbw), parallel=True, comm=comm)[0]


def _tril_ws(ws_ref, g):
    rows = lax.broadcasted_iota(jnp.int32, (CHUNK, CHUNK), 0)
    cols = lax.broadcasted_iota(jnp.int32, (CHUNK, CHUNK), 1)
    return jnp.where(rows >= cols, ws_ref[g], 0.0).astype(BF16), rows >= cols


def _sgu_fwd(pre, lvg, lvb, ws, bs_b, comm=None):
    t = pre.shape[0]
    tt = _tile(t, 256)

    def body(pre_ref, lvg_ref, lvb_ref, ws_ref, bsb_ref, y_ref):
        vhat, _ = _layer_norm_stats(_gelu(pre_ref[:, C_DIM:2 * C_DIM]))
        vl = (vhat * lvg_ref[...] + lvb_ref[...]).astype(BF16)
        for g in range(C_GROUPS):
            w, _ = _tril_ws(ws_ref, g)
            cols = slice(g * CHUNK, (g + 1) * CHUNK)
            for ci in range(tt // CHUNK):
                rows = slice(ci * CHUNK, (ci + 1) * CHUNK)
                sv = jnp.dot(w, vl[rows, cols], preferred_element_type=F32) + bsb_ref[g]
                y_ref[rows, cols] = (_gelu(pre_ref[rows, cols]) * sv).astype(BF16)

    group = pl.BlockSpec((C_GROUPS, CHUNK, CHUNK), lambda i: (0, 0, 0))
    return _call(
        body, name="sgu_fwd", grid=(t // tt,),
        in_specs=[pl.BlockSpec((tt, 2 * C_DIM), lambda i: (i, 0)), _vec_spec(1, C_DIM), _vec_spec(1, C_DIM), group, group],
        out_specs=[pl.BlockSpec((tt, C_DIM), lambda i: (i, 0))], out_shape=[jax.ShapeDtypeStruct((t, C_DIM), BF16)],
        args=(pre, lvg, lvb, ws, bs_b), parallel=True, comm=comm)[0]


def _sgu_bwd(pre, dy, lvg, lvb, ws, bs_b, comm=None):
    t = pre.shape[0]
    tt = _tile(t, 256)

    def body(pre_ref, dy_ref, lvg_ref, lvb_ref, ws_ref, bsb_ref, dpre_ref, dws_ref, dbsb_ref, dlvg_ref, dlvb_ref, dbin_ref,
             dvl_s):
        @pl.when(pl.program_id(0) == 0)
        def _():
            for ref in (dws_ref, dbsb_ref, dlvg_ref, dlvb_ref, dbin_ref):
                ref[...] = jnp.zeros_like(ref)

        v, v_grad = _gelu_and_grad(pre_ref[:, C_DIM:2 * C_DIM])
        vhat, rstd = _layer_norm_stats(v)
        vl = (vhat * lvg_ref[...] + lvb_ref[...]).astype(BF16)
        for g in range(C_GROUPS):
            w, keep = _tril_ws(ws_ref, g)
            cols = slice(g * CHUNK, (g + 1) * CHUNK)
            dws = jnp.zeros((CHUNK, CHUNK), F32)
            dbs = jnp.zeros((CHUNK, 1), F32)
            for ci in range(tt // CHUNK):
                rows = slice(ci * CHUNK, (ci + 1) * CHUNK)
                vl_g = vl[rows, cols]
                sv = jnp.dot(w, vl_g, preferred_element_type=F32) + bsb_ref[g]
                u, u_grad = _gelu_and_grad(pre_ref[rows, cols])
                dyv = dy_ref[rows, cols]
                du = dyv * sv * u_grad
                dpre_ref[rows, cols] = du.astype(BF16)
                dbin_ref[:, cols] += _colsum(du)
                dsv = dyv * u
                dbs = dbs + jnp.sum(dsv, axis=1, keepdims=True)
                dsv16 = dsv.astype(BF16)
                dws = dws + lax.dot_general(dsv16, vl_g, (((1,), (1,)), ((), ())), preferred_element_type=F32)
                dvl_s[rows, cols] = lax.dot_general(w, dsv16, (((0,), (0,)), ((), ())), preferred_element_type=F32)
            dws_ref[g] += jnp.where(keep, dws, 0.0)
            dbsb_ref[g] += dbs
        dvl = dvl_s[...]
        dlvg_ref[...] += _colsum(dvl * vhat)
        dlvb_ref[...] += _colsum(dvl)
        dxh = dvl * lvg_ref[...]
        dv = rstd * (dxh - jnp.mean(dxh, axis=-1, keepdims=True) - vhat * jnp.mean(dxh * vhat, axis=-1, keepdims=True))
        dpv = dv * v_grad
        dpre_ref[:, C_DIM:2 * C_DIM] = dpv.astype(BF16)
        dbin_ref[:, C_DIM:2 * C_DIM] += _colsum(dpv)

    group = pl.BlockSpec((C_GROUPS, CHUNK, CHUNK), lambda i: (0, 0, 0))
    return _call(
        body, name="sgu_bwd", grid=(t // tt,),
        in_specs=[pl.BlockSpec((tt, 2 * C_DIM), lambda i: (i, 0)), pl.BlockSpec((tt, C_DIM), lambda i: (i, 0)),
                  _vec_spec(1, C_DIM), _vec_spec(1, C_DIM), group, group],
        out_specs=[pl.BlockSpec((tt, 2 * C_DIM), lambda i: (i, 0)), group, group,
                   _vec_spec(1, C_DIM), _vec_spec(1, C_DIM), _vec_spec(1, 2 * C_DIM)],
        out_shape=[jax.ShapeDtypeStruct((t, 2 * C_DIM), BF16), jax.ShapeDtypeStruct((C_GROUPS, CHUNK, CHUNK), F32),
                   jax.ShapeDtypeStruct((C_GROUPS, CHUNK, CHUNK), F32), jax.ShapeDtypeStruct((1, C_DIM), F32),
                   jax.ShapeDtypeStruct((1, C_DIM), F32), jax.ShapeDtypeStruct((1, 2 * C_DIM), F32)],
        scratch_shapes=[pltpu.VMEM((tt, C_DIM), F32)],
        args=(pre, dy, lvg, lvb, ws, bs_b), comm=comm)


def _pair_sum(name, part, got, core):
    _, k, n = part.shape
    tk = _tile(k, 1024)

    def body(core_ref, p_ref, s_ref, o_ref):
        o_ref[...] = (p_ref[...] + s_ref[...]).astype(BF16)

    return pl.pallas_call(
        body, name=name,
        grid_spec=pltpu.PrefetchScalarGridSpec(
            num_scalar_prefetch=1, grid=(N_CHIP, k // tk),
            in_specs=[pl.BlockSpec((None, tk, n), lambda q, i, core_ref: (2 * q + core_ref[0], i, 0)),
                      pl.BlockSpec((None, tk, n), lambda q, i, core_ref: (q, i, 0))],
            out_specs=pl.BlockSpec((None, tk, n), lambda q, i, core_ref: (q, i, 0))),
        out_shape=jax.ShapeDtypeStruct((N_CHIP, k, n), BF16),
        compiler_params=pltpu.CompilerParams(dimension_semantics=("parallel", "parallel"), vmem_limit_bytes=VMEM_LIMIT_BYTES),
    )(core, part, got)


def _adamw_math(w, g, m, v):
    m = ADAM_B1 * m + (1.0 - ADAM_B1) * g
    v = ADAM_B2 * v + (1.0 - ADAM_B2) * (g * g)
    m_hat = m / (1.0 - ADAM_B1 ** ADAM_STEP)
    v_hat = v / (1.0 - ADAM_B2 ** ADAM_STEP)
    delta = -ADAM_LR * (m_hat / (jnp.sqrt(v_hat) + ADAM_EPS) + ADAM_WD * w)
    return delta, m, v


def _sum_adamw(name, parts, w, m, v, comm=None):
    layers = len(parts)
    n_parts, k, n = parts[0].shape
    tk = _tile(k, 256)

    def body(*refs):
        p_refs = refs[:layers]
        w_ref, m_ref, v_ref, g_ref, d_ref, nm_ref, nv_ref = refs[layers:]

        def total(p_ref):
            g = p_ref[0].astype(F32)
            for q in range(1, n_parts):
                g = g + p_ref[q].astype(F32)
            return g

        g = total(p_refs[0])
        for l in range(1, layers):
            g = jnp.where(pl.program_id(0) == l, total(p_refs[l]), g)
        g_ref[...] = g
        d_ref[...], nm_ref[...], nv_ref[...] = _adamw_math(w_ref[...], g, m_ref[...], v_ref[...])

    blk = pl.BlockSpec((None, tk, n), lambda l, i: (l, i, 0))
    return _call(
        body, name=name, grid=(layers, k // tk),
        in_specs=[pl.BlockSpec((n_parts, tk, n), lambda l, i: (0, i, 0))] * layers + [blk, blk, blk], out_specs=[blk] * 4,
        out_shape=[jax.ShapeDtypeStruct((layers, k, n), F32)] * 4, args=(*parts, w, m, v), parallel=True, comm=comm)


def _small_adamw(name, parts, w, m, v, losses):
    count = len(parts)

    def in_order(ref):
        total = ref[0]
        for dev in range(1, N_DEV):
            total = total + ref[dev]
        return total

    def body(*refs):
        p_refs, w_refs, m_refs, v_refs = (refs[j * count:(j + 1) * count] for j in range(4))
        losses_ref = refs[4 * count]
        g_refs, d_refs, nm_refs, nv_refs = (refs[4 * count + 1 + j * count:4 * count + 1 + (j + 1) * count] for j in range(4))
        loss_ref = refs[8 * count + 1]
        for i in range(count):
            g = in_order(p_refs[i])
            g_refs[i][...] = g
            d_refs[i][...], nm_refs[i][...], nv_refs[i][...] = _adamw_math(w_refs[i][...], g, m_refs[i][...], v_refs[i][...])
        loss_ref[...] = in_order(losses_ref)

    res = pl.pallas_call(
        body, name=name,
        out_shape=[jax.ShapeDtypeStruct(a.shape, F32) for a in w] * 4 + [jax.ShapeDtypeStruct(losses.shape[1:], F32)],
        compiler_params=pltpu.CompilerParams(vmem_limit_bytes=VMEM_LIMIT_BYTES))(*parts, *w, *m, *v, losses)
    return [res[j * count:(j + 1) * count] for j in range(4)], res[4 * count]


def _rows(a):
    return a.reshape(-1, a.shape[-1])


def _whole(gathered):
    return jnp.transpose(gathered, (1, 0, 2)).reshape(gathered.shape[1], -1)


SMALL =("ev_norm_g", "ev_conv_a_w", "ev_conv_a_b", "ev_ln_a_g", "ev_ln_a_b", "ev_conv_b_w", "od_norm_g", "od_b_in",
         "od_ln_v_g", "od_ln_v_b", "od_w_s", "od_b_s", "mlp_norm_g", "final_norm_g")
SMALL_SHARDED = ("ev_conv_a_w", "ev_conv_b_w", "od_norm_g", "od_b_in", "od_ln_v_g", "od_ln_v_b")
ORDER = ("ev_norm_g", "ev_w_in", "ev_conv_a_w", "ev_conv_a_b", "ev_ln_a_g", "ev_ln_a_b", "ev_conv_b_w", "ev_w_out",
         "od_norm_g", "od_w_in", "od_b_in", "od_ln_v_g", "od_ln_v_b", "od_w_s", "od_b_s", "od_w_out", "mlp_norm_g",
         "mlp_w1", "mlp_w2", "final_norm_g")


def kernel(x, ev_norm_g, ev_w_in, ev_conv_a_w, ev_conv_a_b, ev_ln_a_g, ev_ln_a_b, ev_conv_b_w, ev_w_out, od_norm_g, od_w_in, od_b_in, od_ln_v_g, od_ln_v_b, od_w_s, od_b_s, od_w_out, mlp_norm_g, mlp_w1, mlp_w2, final_norm_g, loss_target, m_ev_norm_g, m_ev_w_in, m_ev_conv_a_w, m_ev_conv_a_b, m_ev_ln_a_g, m_ev_ln_a_b, m_ev_conv_b_w, m_ev_w_out, m_od_norm_g, m_od_w_in, m_od_b_in, m_od_ln_v_g, m_od_ln_v_b, m_od_w_s, m_od_b_s, m_od_w_out, m_mlp_norm_g, m_mlp_w1, m_mlp_w2, m_final_norm_g, v_ev_norm_g, v_ev_w_in, v_ev_conv_a_w, v_ev_conv_a_b, v_ev_ln_a_g, v_ev_ln_a_b, v_ev_conv_b_w, v_ev_w_out, v_od_norm_g, v_od_w_in, v_od_b_in, v_od_ln_v_g, v_od_ln_v_b, v_od_w_s, v_od_b_s, v_od_w_out, v_mlp_norm_g, v_mlp_w1, v_mlp_w2, v_final_norm_g):
    W = dict(ev_norm_g=ev_norm_g, ev_w_in=ev_w_in, ev_conv_a_w=ev_conv_a_w, ev_conv_a_b=ev_conv_a_b, ev_ln_a_g=ev_ln_a_g,
             ev_ln_a_b=ev_ln_a_b, ev_conv_b_w=ev_conv_b_w, ev_w_out=ev_w_out, od_norm_g=od_norm_g, od_w_in=od_w_in,
             od_b_in=od_b_in, od_ln_v_g=od_ln_v_g, od_ln_v_b=od_ln_v_b, od_w_s=od_w_s, od_b_s=od_b_s, od_w_out=od_w_out,
             mlp_norm_g=mlp_norm_g, mlp_w1=mlp_w1, mlp_w2=mlp_w2, final_norm_g=final_norm_g)
    M = dict(ev_norm_g=m_ev_norm_g, ev_w_in=m_ev_w_in, ev_conv_a_w=m_ev_conv_a_w, ev_conv_a_b=m_ev_conv_a_b,
             ev_ln_a_g=m_ev_ln_a_g, ev_ln_a_b=m_ev_ln_a_b, ev_conv_b_w=m_ev_conv_b_w, ev_w_out=m_ev_w_out,
             od_norm_g=m_od_norm_g, od_w_in=m_od_w_in, od_b_in=m_od_b_in, od_ln_v_g=m_od_ln_v_g, od_ln_v_b=m_od_ln_v_b,
             od_w_s=m_od_w_s, od_b_s=m_od_b_s, od_w_out=m_od_w_out, mlp_norm_g=m_mlp_norm_g, mlp_w1=m_mlp_w1,
             mlp_w2=m_mlp_w2, final_norm_g=m_final_norm_g)
    V = dict(ev_norm_g=v_ev_norm_g, ev_w_in=v_ev_w_in, ev_conv_a_w=v_ev_conv_a_w, ev_conv_a_b=v_ev_conv_a_b,
             ev_ln_a_g=v_ev_ln_a_g, ev_ln_a_b=v_ev_ln_a_b, ev_conv_b_w=v_ev_conv_b_w, ev_w_out=v_ev_w_out,
             od_norm_g=v_od_norm_g, od_w_in=v_od_w_in, od_b_in=v_od_b_in, od_ln_v_g=v_od_ln_v_g, od_ln_v_b=v_od_ln_v_b,
             od_w_s=v_od_w_s, od_b_s=v_od_b_s, od_w_out=v_od_w_out, mlp_norm_g=v_mlp_norm_g, mlp_w1=v_mlp_w1,
             mlp_w2=v_mlp_w2, final_norm_g=v_final_norm_g)

    n_seq, seq, d = x.shape
    t = n_seq * seq
    dev = 4 * lax.axis_index("x") + 2 * lax.axis_index("y") + lax.axis_index("c")
    core = lax.axis_index("c").astype(jnp.int32).reshape(1)

    ev_g, cab, lag, lab = W["ev_norm_g"], W["ev_conv_a_b"], W["ev_ln_a_g"], W["ev_ln_a_b"]
    ws = W["od_w_s"][0]
    bs_b = jnp.broadcast_to(W["od_b_s"][0][:, :, None], (C_GROUPS, CHUNK, CHUNK))
    mlp_g = [W["mlp_norm_g"][l:l + 1] for l in range(2)]
    fin_g = W["final_norm_g"].reshape(1, d)

    h0 = x.reshape(t, d)
    for store in (W, M, V):
        store["ev_w_in"] = jnp.transpose(store["ev_w_in"], (0, 2, 1))
    gather = _gather_comm([W["ev_w_in"][0].astype(BF16)])
    later = [(W["ev_w_out"], 0), (W["od_w_in"], 0), (W["od_w_out"], 0), (W["mlp_w1"], 0), (W["mlp_w2"], 0),
             (W["mlp_w1"], 1), (W["mlp_w2"], 1)]
    n0, (w_ev_out16, w_od_in16, w_od_out16, w1_0, w2_0, w1_1, w2_1) = _rms_fwd("ev_norm", h0, ev_g, comm=gather, casts=later)
    w_ev_in_t = gather.out[0].reshape(IN_EVEN, D_MODEL)

    first, second = (0, 2), (1, 2)

    g_a, g_b = _gather_comm([w_ev_out16] + [_rows(W[n]) for n in SMALL_SHARDED]), _gather_comm([w1_0], first)
    z = _mm_nt("ev_in", n0, w_ev_in_t, IN_EVEN, _ep_store, [F32], comm=[g_a, g_b])[0]
    w_ev_out = g_a.out[0].reshape(D_MODEL, D_MODEL)
    caw, cbw, od_g, od_bin, lvg, lvb = [_whole(g) for g in g_a.out[1:]]

    g_c, g_d = _gather_comm([w1_0], second, into=g_b.out), _gather_comm([w2_0], first)
    mix, a2 = _mixer_fwd(z, seq, caw, cab, lag, lab, cbw, comm=[g_c, g_d])
    w1 = [_whole(g_c.out[0]), None]

    g_e = _gather_comm([w_od_in16], first)
    h1, n1 = _mm_nn("ev_out", mix, w_ev_out, d, _ep_residual_norm, [F32, BF16], extras=(h0,), rows=(mlp_g[0],), comm=g_e)

    g_f, g_f2 = _gather_comm([w2_0], second, into=g_d.out), _gather_comm([w_od_in16], second, into=g_e.out)
    q0 = _mm_nn("mlp0_up", n1, w1[0], D_FF, _ep_relu_sq, [BF16], comm=[g_f, g_f2])[0]
    w2 = [g_f.out[0].reshape(D_FF, D_MODEL), None]
    w_od_in = _whole(g_f2.out[0])

    g_g, g_g2 = _gather_comm([w1_1], first), _gather_comm([w_od_out16])
    h2, n2 = _mm_nn("mlp0_down", q0, w2[0], d, _ep_residual_norm, [F32, BF16], extras=(h1,), rows=(od_g,), comm=[g_g, g_g2])
    w_od_out = g_g2.out[0].reshape(D_MODEL, D_MODEL)
    g_h = _gather_comm([w1_1], second, into=g_g.out)
    pre = _mm_nn("od_in", n2, w_od_in, 2 * C_DIM, _ep_bias, [F32], rows=(od_bin,), comm=g_h)[0]
    w1[1] = _whole(g_h.out[0])
    g_i = _gather_comm([w2_1], (0, 4))
    y = _sgu_fwd(pre, lvg, lvb, ws, bs_b, comm=g_i)
    g_i2 = _gather_comm([w2_1], (1, 4), into=g_i.out)
    h3, n3 = _mm_nn("od_out", y, w_od_out, d, _ep_residual_norm, [F32, BF16], extras=(h2,), rows=(mlp_g[1],), comm=g_i2)
    g_j = _gather_comm([w2_1], second, into=g_i2.out)
    q1 = _mm_nn("mlp1_up", n3, w1[1], D_FF, _ep_relu_sq, [BF16], comm=g_j)[0]
    w2[1] = g_j.out[0].reshape(D_FF, D_MODEL)
    grad, grad16, d_fin_g, loss_part = _mm_nn(
        "mlp1_down", q1, w2[1], d, _ep_final_loss, [F32, BF16], extras=(h3, loss_target.reshape(t, d)), rows=(fin_g,),
        sums=(True, False), tm_want=FUSED_ROWS)

    by_chip = {}

    def swap(name, parts):
        comm = _pair_comm([parts])
        comm.parts, comm.weight = parts, name
        return comm

    def exchange(swapped, halves=False):
        sums = _pair_sum(f"pair_sum_{swapped.weight}", swapped.parts, swapped.out[0], core)
        if not halves:
            comm = _chip_comm([sums])
            comm.weight = swapped.weight
            return comm
        comm = _chip_comm([sums], first)
        comm.sums, comm.weight = sums, swapped.weight
        return comm

    def rest(comm):
        other = _chip_comm([comm.sums], second, into=comm.out)
        other.weight = comm.weight
        return other

    def done(comm):
        by_chip[comm.weight] = comm.out[0]

    dw2_1 = _mm_tn("mlp1_dw2", q1, grad16).reshape(N_DEV, D_FF // N_DEV, D_MODEL)
    s_a = swap("w2_1", dw2_1)
    dp = _mm_nt("mlp1_dq", grad16, w2[1], D_FF, _ep_relu_sq_grad, [BF16], extras=(q1,), comm=s_a)[0]
    c_a = exchange(s_a, halves=True)
    dw1_1 = _mm_tn("mlp1_dw1", n3, dp, col_blocks=N_DEV, comm=c_a)
    c_a2, s_b = rest(c_a), swap("w1_1", dw1_1)
    dn = _mm_nt("mlp1_dn", dp, w1[1], d, _ep_store, [F32], comm=[c_a2, s_b])[0]
    done(c_a2)
    c_b = exchange(s_b, halves=True)
    grad, grad16, dg_mlp1 = _rms_bwd("mlp1_dn_norm", dn, h3, mlp_g[1], grad)
    d_od_out = _mm_tn("od_dw_out", y, grad16).reshape(N_DEV, D_MODEL // N_DEV, D_MODEL)
    s_c = swap("od_out", d_od_out)
    dy = _mm_nt("od_dy", grad16, w_od_out, C_DIM, _ep_store, [F32], comm=s_c)[0]
    c_c = exchange(s_c)
    dpre, d_ws, d_bsb, d_lvg, d_lvb, d_bin = _sgu_bwd(pre, dy, lvg, lvb, ws, bs_b, comm=[c_b, c_c])
    done(c_c)
    c_b2 = rest(c_b)
    d_od_in = _mm_tn("od_dw_in", n2, dpre, col_blocks=N_DEV, comm=c_b2)
    done(c_b2)
    s_d = swap("od_in", d_od_in)
    dn = _mm_nt("od_dn", dpre, w_od_in, d, _ep_store, [F32], comm=s_d)[0]
    c_d = exchange(s_d)
    grad, grad16, d_od_g = _rms_bwd("od_dn_norm", dn, h2, od_g, grad)
    group_1 = dict(od_norm_g=d_od_g, od_b_in=d_bin, od_ln_v_g=d_lvg, od_ln_v_b=d_lvb,
                   od_w_s=d_ws.reshape(C_GROUPS * CHUNK, CHUNK), od_b_s=d_bsb[:, :, 0], final_norm_g=d_fin_g,
                   mlp_norm_g_1=dg_mlp1, loss=jnp.broadcast_to(loss_part, (SUBLANES, LANE_BLOCK)))
    gather_1 = _gather_comm(list(group_1.values()))
    dw2_0 = _mm_tn("mlp0_dw2", q0, grad16, comm=c_d).reshape(N_DEV, D_FF // N_DEV, D_MODEL)
    done(c_d)
    s_e = swap("w2_0", dw2_0)
    dp = _mm_nt("mlp0_dq", grad16, w2[0], D_FF, _ep_relu_sq_grad, [BF16], extras=(q0,), comm=[s_e, gather_1])[0]
    c_e = exchange(s_e, halves=True)
    dw1_0 = _mm_tn("mlp0_dw1", n1, dp, col_blocks=N_DEV, comm=c_e)
    c_e2, s_f = rest(c_e), swap("w1_0", dw1_0)
    dn = _mm_nt("mlp0_dn", dp, w1[0], d, _ep_store, [F32], comm=[c_e2, s_f])[0]
    done(c_e2)
    c_f = exchange(s_f, halves=True)
    grad, grad16, dg_mlp0 = _rms_bwd("mlp0_dn_norm", dn, h1, mlp_g[0], grad)
    d_ev_out = _mm_tn("ev_dw_out", mix, grad16).reshape(N_DEV, D_MODEL // N_DEV, D_MODEL)
    s_g = swap("ev_out", d_ev_out)
    dmix = _mm_nt("ev_dmix", grad16, w_ev_out, A_DIM + B_DIM, _ep_store, [F32], comm=s_g)[0]
    c_g = exchange(s_g)
    da2, dcv, d_caw, d_cab, d_lag, d_lab, d_cbw = _mixer_bwd_local(z, a2, dmix, seq, lag, lab, comm=[c_f, c_g])
    done(c_g)
    c_f2 = rest(c_f)
    dz = _mixer_bwd_input(z, dmix, da2, dcv, seq, caw, cbw, comm=c_f2)
    done(c_f2)
    group_2 = dict(ev_conv_a_w=d_caw, ev_conv_a_b=d_cab, ev_ln_a_g=d_lag, ev_ln_a_b=d_lab, ev_conv_b_w=d_cbw,
                   mlp_norm_g_0=dg_mlp0)
    gather_2 = _gather_comm(list(group_2.values()))
    d_ev_in_t = _mm_tn("ev_dw_in", dz, n0, comm=gather_2).reshape(N_DEV, IN_EVEN // N_DEV, D_MODEL)
    s_h = swap("ev_in", d_ev_in_t)
    dn = _mm_nn("ev_dn", dz, w_ev_in_t, d, _ep_store, [F32], comm=s_h)[0]
    c_h = exchange(s_h)
    grad_x, _, d_ev_g = _rms_bwd("ev_dn_norm", dn, h0, ev_g, grad, comm=c_h)
    done(c_h)

    last_gather = _gather_comm([d_ev_g])
    shard = {"od_w_out": ("od_out",), "mlp_w1": ("w1_0", "w1_1"), "mlp_w2": ("w2_0", "w2_1"), "od_w_in": ("od_in",),
             "ev_w_out": ("ev_out",), "ev_w_in": ("ev_in",)}
    carried = {"od_w_out": last_gather}
    out_g, out_d, out_m, out_v = {}, {}, {}, {}
    for name, keys in shard.items():
        out_g[name], out_d[name], out_m[name], out_v[name] = _sum_adamw(
            f"adamw_{name}", [by_chip[key] for key in keys], W[name], M[name], V[name], comm=carried.get(name))
    for store in (out_g, out_d, out_m, out_v):
        store["ev_w_in"] = jnp.transpose(store["ev_w_in"], (0, 2, 1))

    gathered = dict(zip(group_1, gather_1.out), **dict(zip(group_2, gather_2.out)), ev_norm_g=last_gather.out[0])
    gathered["mlp_norm_g"] = jnp.concatenate([gathered["mlp_norm_g_0"], gathered["mlp_norm_g_1"]], axis=1)
    mine = []
    for n in SMALL:
        g = gathered[n]
        if n in SMALL_SHARDED:
            width = W[n].shape[-1]
            g = lax.dynamic_slice_in_dim(g, dev * width, width, axis=2)
        mine.append(g)
    res, loss_tile = _small_adamw("adamw_small", mine, [_rows(W[n]) for n in SMALL], [_rows(M[n]) for n in SMALL],
                                  [_rows(V[n]) for n in SMALL], gathered["loss"])
    loss = loss_tile[0, 0]
    for store, values in zip((out_g, out_d, out_m, out_v), res):
        for n, value in zip(SMALL, values):
            store[n] = value.reshape(W[n].shape)

    return (loss, grad_x.reshape(n_seq, seq, d), *[out_g[n] for n in ORDER], *[out_d[n] for n in ORDER],
            *[out_m[n] for n in ORDER], *[out_v[n] for n in ORDER])
```

```python
import math

import jax
import jax.numpy as jnp
from jax import lax
from jax.experimental import pallas as pl
from jax.experimental.pallas import tpu as pltpu

F32 = jnp.float32
BF16 = jnp.bfloat16
MESH = pl.DeviceIdType.MESH

D_MODEL = 1024
A_DIM = 512
B_DIM = 512
IN_EVEN = 2 * A_DIM + 3 * B_DIM
A_TAPS = 31
B_TAPS = 3
CHUNK = 128
C_GROUPS = 8
C_DIM = 1024
D_FF = 4096
RMS_EPS = 1e-6
LN_EPS = 1e-5
N_DEV = 8
N_CHIP = 4

ADAM_LR = 0.001
ADAM_B1 = 0.9
ADAM_B2 = 0.999
ADAM_EPS = 1e-08
ADAM_WD = 0.01
ADAM_STEP = 10

A_HALO = 32
B_HALO = 8
VMEM_LIMIT_BYTES = 56 * 1024 * 1024
TINY = 1.1754944e-38
INV_SQRT2 = 1.0 / math.sqrt(2.0)
INV_SQRT_2PI = 1.0 / math.sqrt(2.0 * math.pi)
HBM_SPEC = pl.BlockSpec(memory_space=pltpu.HBM)


def _tile(n, want):
    t = min(n, want)
    while n % t:
        t //= 2
    return t


def _sigmoid(x):
    return 1.0 / (1.0 + jnp.exp(-x))


def _gelu(x):
    return 0.5 * x * (1.0 + lax.erf(x * INV_SQRT2))


def _gelu_and_grad(x):
    cdf = 0.5 * (1.0 + lax.erf(x * INV_SQRT2))
    return x * cdf, cdf + x * jnp.exp(-0.5 * x * x) * INV_SQRT_2PI


def _colsum(x):
    return jnp.sum(x, axis=0, keepdims=True)


class _Comm:
    def __init__(self, ins, out_shapes, sem_shapes, start, finish, middle=None, into=None, sibling=False, chips=False):
        self.ins, self.out_shapes, self.sem_shapes, self.start, self.finish = ins, out_shapes, sem_shapes, start, finish
        self.middle = middle
        self.sibling, self.chips = sibling, chips
        self.into = list(into) if into is not None else []
        self.out = None


def _piece_rows(rows, piece):
    if piece is None:
        return 0, rows
    i, n = piece
    return i * (rows // n), rows // n


MIDDLE_AT = 0.75
BARRIER_IDS = {(True, True): 0, (True, False): 1, (False, True): 2}
FUSED_ROWS = 512


def _call(body, *, name, grid, in_specs, out_specs, out_shape, args, scratch_shapes=(), parallel=False, comm=None):
    comms = [] if comm is None else (list(comm) if isinstance(comm, (list, tuple)) else [comm])
    if not comms:
        sem = ("parallel" if parallel else "arbitrary",) * len(grid)
        return pl.pallas_call(
            body, name=name, grid=grid, in_specs=list(in_specs), out_specs=list(out_specs), out_shape=list(out_shape),
            scratch_shapes=list(scratch_shapes),
            compiler_params=pltpu.CompilerParams(dimension_semantics=sem, vmem_limit_bytes=VMEM_LIMIT_BYTES),
        )(*args)
    n_in, n_out, n_scr = len(in_specs), len(out_shape), len(scratch_shapes)
    c_ins_all = [a for cm in comms for a in cm.ins]
    c_into_all = [a for cm in comms for a in cm.into]
    c_out_shapes = [s for cm in comms for s in cm.out_shapes]
    c_sem_shapes = [s for cm in comms for s in cm.sem_shapes]
    aliases, in_pos, out_pos = {}, n_in + len(c_ins_all), n_out
    for cm in comms:
        for j in range(len(cm.into)):
            aliases[in_pos + j] = out_pos + j
        in_pos += len(cm.into)
        out_pos += len(cm.out_shapes)
    to_sibling = any(cm.sibling for cm in comms)
    to_chips = any(cm.chips for cm in comms)
    steps = grid
    total = math.prod(steps)
    first_step = (0,) * len(steps)
    last_step = tuple(s - 1 for s in steps)
    middle_step = None
    if 0 < int(MIDDLE_AT * total) < total - 1:
        rest, idx = int(MIDDLE_AT * total), []
        for s in reversed(steps):
            idx.append(rest % s)
            rest //= s
        middle_step = tuple(reversed(idx))

    def carrying(*refs):
        pos = 0
        ins = refs[pos:pos + n_in]; pos += n_in
        c_ins = refs[pos:pos + len(c_ins_all)]; pos += len(c_ins_all) + len(c_into_all)
        outs = refs[pos:pos + n_out]; pos += n_out
        c_outs = refs[pos:pos + len(c_out_shapes)]; pos += len(c_out_shapes)
        scr = refs[pos:pos + n_scr]; pos += n_scr
        c_sems = refs[pos:]
        views, i0, o0, s0 = [], 0, 0, 0
        for cm in comms:
            views.append((c_ins[i0:i0 + len(cm.ins)], c_outs[o0:o0 + len(cm.out_shapes)], c_sems[s0:s0 + len(cm.sem_shapes)]))
            i0, o0, s0 = i0 + len(cm.ins), o0 + len(cm.out_shapes), s0 + len(cm.sem_shapes)

        def at(step):
            hit = pl.program_id(0) == step[0]
            for axis in range(1, len(steps)):
                hit = jnp.logical_and(hit, pl.program_id(axis) == step[axis])
            return hit

        @pl.when(at(first_step))
        def _():
            x, y, c, chips = _place()
            peers = ([(x, y, 1 - c)] if to_sibling else []) + ([(*chip, c) for chip in chips] if to_chips else [])
            barrier = pltpu.get_barrier_semaphore()
            for peer in peers:
                pl.semaphore_signal(barrier, inc=1, device_id=peer, device_id_type=MESH)
            pl.semaphore_wait(barrier, len(peers))
            for cm, view in zip(comms, views):
                cm.start(*view)

        if middle_step is not None:
            @pl.when(at(middle_step))
            def _():
                for cm, view in zip(comms, views):
                    if cm.middle is not None:
                        cm.middle(*view)

        body(*ins, *outs, *scr)

        @pl.when(at(last_step))
        def _():
            for cm, view in zip(comms, views):
                if cm.middle is not None and middle_step is None:
                    cm.middle(*view)
            for cm, view in zip(comms, views):
                cm.finish(*view)

    res = pl.pallas_call(
        carrying, name=name, grid=grid,
        in_specs=[*in_specs, *[HBM_SPEC] * (len(c_ins_all) + len(c_into_all))],
        out_specs=[*out_specs, *[HBM_SPEC] * len(c_out_shapes)],
        out_shape=[*out_shape, *c_out_shapes], scratch_shapes=[*scratch_shapes, *c_sem_shapes],
        input_output_aliases=aliases,
        compiler_params=pltpu.CompilerParams(dimension_semantics=("arbitrary",) * len(grid), vmem_limit_bytes=VMEM_LIMIT_BYTES,
                                             collective_id=BARRIER_IDS[to_sibling, to_chips]),
    )(*args, *c_ins_all, *c_into_all)
    pos = n_out
    for cm in comms:
        cm.out = list(res[pos:pos + len(cm.out_shapes)])
        pos += len(cm.out_shapes)
    return list(res[:n_out])


def _place():
    x, y, c = lax.axis_index("x"), lax.axis_index("y"), lax.axis_index("c")
    return x, y, c, [(1 - x, y), (x, 1 - y), (1 - x, 1 - y)]


def _gather_comm(shards, piece=None, into=None):
    nw = len(shards)
    spans = [_piece_rows(s.shape[0], piece) for s in shards]

    def plan(ins, outs, sems):
        send_sems, recv_sems, local_sems = sems
        x, y, c, chips = _place()
        me, sibling = (x, y, c), (x, y, 1 - c)

        def slot(w, p):
            return outs[w].at[4 * p[0] + 2 * p[1] + p[2], pl.ds(*spans[w])]

        def mine(w):
            return ins[w].at[pl.ds(*spans[w])]

        def copy(w, k, block, to, src=None):
            return pltpu.make_async_remote_copy(
                src_ref=slot(w, block) if src is None else src, dst_ref=slot(w, block),
                send_sem=send_sems.at[w, k], recv_sem=recv_sems.at[w, k], device_id=to, device_id_type=MESH)

        local = [pltpu.make_async_copy(mine(w), slot(w, me), local_sems.at[w]) for w in range(nw)]
        first = [[copy(w, 0, me, sibling, src=mine(w))] + [copy(w, 1 + j, me, (*chip, c), src=mine(w)) for j, chip in enumerate(chips)]
                 for w in range(nw)]
        landed = [[copy(w, 1 + j, (*chip, c), me) for j, chip in enumerate(chips)] for w in range(nw)]
        passed = [[copy(w, 4 + j, (*chip, c), sibling) for j, chip in enumerate(chips)] for w in range(nw)]
        from_sibling = [[copy(w, 0, sibling, me)] + [copy(w, 4 + j, (*chip, 1 - c), me) for j, chip in enumerate(chips)]
                        for w in range(nw)]
        return local, first, landed, passed, from_sibling

    def start(ins, outs, sems):
        local, first, _, _, _ = plan(ins, outs, sems)
        for cp in local:
            cp.start()
        for row in first:
            for cp in row:
                cp.start()

    def middle(ins, outs, sems):
        _, _, landed, passed, _ = plan(ins, outs, sems)
        for w in range(nw):
            for j in range(3):
                landed[w][j].wait_recv()
                passed[w][j].start()

    def finish(ins, outs, sems):
        local, first, landed, passed, from_sibling = plan(ins, outs, sems)
        for w in range(nw):
            for cp in from_sibling[w]:
                cp.wait_recv()
        for w in range(nw):
            for cp in first[w] + passed[w]:
                cp.wait_send()
        for cp in local:
            cp.wait()

    return _Comm(list(shards), [jax.ShapeDtypeStruct((N_DEV, *s.shape), s.dtype) for s in shards],
                 [pltpu.SemaphoreType.DMA((nw, 7)), pltpu.SemaphoreType.DMA((nw, 7)), pltpu.SemaphoreType.DMA((nw,))],
                 start, finish, middle, into=into, sibling=True, chips=True)


def _pair_comm(parts):
    nw = len(parts)

    def plan(ins, outs, sems):
        send_sems, recv_sems = sems
        x, y, c, _ = _place()
        return [pltpu.make_async_remote_copy(
            src_ref=ins[w].at[2 * q + 1 - c], dst_ref=outs[w].at[q], send_sem=send_sems.at[w, q], recv_sem=recv_sems.at[w, q],
            device_id=(x, y, 1 - c), device_id_type=MESH) for w in range(nw) for q in range(N_CHIP)]

    def start(ins, outs, sems):
        for cp in plan(ins, outs, sems):
            cp.start()

    def finish(ins, outs, sems):
        for cp in plan(ins, outs, sems):
            cp.wait()

    return _Comm(list(parts), [jax.ShapeDtypeStruct((N_CHIP, *p.shape[1:]), p.dtype) for p in parts],
                 [pltpu.SemaphoreType.DMA((nw, N_CHIP)), pltpu.SemaphoreType.DMA((nw, N_CHIP))], start, finish, sibling=True)


def _chip_comm(sums, piece=None, into=None):
    nw = len(sums)
    spans = [_piece_rows(s.shape[1], piece) for s in sums]

    def plan(ins, outs, sems):
        send_sems, recv_sems, local_sems = sems
        x, y, c, chips = _place()
        my_chip = 2 * x + y
        local = [pltpu.make_async_copy(ins[w].at[my_chip, pl.ds(*spans[w])], outs[w].at[my_chip, pl.ds(*spans[w])],
                                       local_sems.at[w]) for w in range(nw)]
        remote = [pltpu.make_async_remote_copy(
            src_ref=ins[w].at[2 * chip[0] + chip[1], pl.ds(*spans[w])], dst_ref=outs[w].at[my_chip, pl.ds(*spans[w])],
            send_sem=send_sems.at[w, j], recv_sem=recv_sems.at[w, j], device_id=(*chip, c), device_id_type=MESH)
            for w in range(nw) for j, chip in enumerate(chips)]
        return local, remote

    def start(ins, outs, sems):
        local, remote = plan(ins, outs, sems)
        for cp in local + remote:
            cp.start()

    def finish(ins, outs, sems):
        local, remote = plan(ins, outs, sems)
        for cp in remote + local:
            cp.wait()

    return _Comm(list(sums), [jax.ShapeDtypeStruct(s.shape, s.dtype) for s in sums],
                 [pltpu.SemaphoreType.DMA((nw, 3)), pltpu.SemaphoreType.DMA((nw, 3)), pltpu.SemaphoreType.DMA((nw,))],
                 start, finish, into=into, chips=True)


def _matmul(name, a, b, *, kind, m, n, k, a_spec, b_spec, tm, tn, tk, out_shape, out_specs, epilogue,
            extras=(), extra_specs=(), comm=None):
    dims = {"nn": (((1,), (0,)), ((), ())), "nt": (((1,), (1,)), ((), ())), "tn": (((0,), (0,)), ((), ()))}[kind]
    nk = k // tk
    n_extra = len(extras)
    n_out = len(out_shape)

    def body(a_ref, b_ref, *rest):
        extra_refs = rest[:n_extra]
        out_refs = rest[n_extra:n_extra + n_out]
        part = lax.dot_general(a_ref[...], b_ref[...], dims, preferred_element_type=F32)
        if nk == 1:
            epilogue(part, extra_refs, out_refs)
            return
        acc_ref = rest[n_extra + n_out]
        step = pl.program_id(2)

        @pl.when(step == 0)
        def _():
            acc_ref[...] = part

        @pl.when(jnp.logical_and(step > 0, step < nk - 1))
        def _():
            acc_ref[...] += part

        @pl.when(step == nk - 1)
        def _():
            epilogue(acc_ref[...] + part, extra_refs, out_refs)

    return _call(
        body, name=name, grid=(m // tm, n // tn, nk), in_specs=[a_spec, b_spec, *extra_specs], out_specs=out_specs,
        out_shape=out_shape, scratch_shapes=[pltpu.VMEM((tm, tn), F32)] if nk > 1 else [], args=(a, b, *extras), comm=comm)


def _mm_operands(m, n, tm, tn, out_dtypes, extras, rows, sums):
    tile = pl.BlockSpec((tm, tn), lambda i, j, kk: (i, j))
    row = pl.BlockSpec((1, tn), lambda i, j, kk: (0, j))
    one = pl.BlockSpec((1, 1), lambda i, j, kk: (0, 0))
    out_shape = [jax.ShapeDtypeStruct((m, n), dt) for dt in out_dtypes]
    out_shape += [jax.ShapeDtypeStruct((1, n if wide else 1), F32) for wide in sums]
    out_specs = [tile] * len(out_dtypes) + [row if wide else one for wide in sums]
    return (*extras, *rows), [tile] * len(extras) + [row] * len(rows), out_shape, out_specs


def _row_tile(m, k, tm_want):
    return _tile(m, tm_want or (1024 if k <= 1024 else 512))


def _mm_nn(name, a, b, n, epilogue, out_dtypes, *, extras=(), rows=(), sums=(), tm_want=None, comm=None):
    m, k = a.shape
    blocked = b.ndim == 3
    tm = _row_tile(m, k, tm_want)
    tn = b.shape[-1] if blocked else _tile(n, 1024)
    tk = k
    extras, extra_specs, out_shape, out_specs = _mm_operands(m, n, tm, tn, out_dtypes, extras, rows, sums)
    if blocked:
        b_spec = pl.BlockSpec((None, tk, tn), lambda i, j, kk: (j, kk, 0))
    else:
        b_spec = pl.BlockSpec((tk, tn), lambda i, j, kk: (kk, j))
    return _matmul(
        name, a, b, kind="nn", m=m, n=n, k=k, tm=tm, tn=tn, tk=tk,
        a_spec=pl.BlockSpec((tm, tk), lambda i, j, kk: (i, kk)), b_spec=b_spec, out_shape=out_shape, out_specs=out_specs,
        epilogue=epilogue, extras=extras, extra_specs=extra_specs, comm=comm)


def _mm_nt(name, a, b, n, epilogue, out_dtypes, *, extras=(), rows=(), sums=(), tm_want=None, comm=None):
    m, k = a.shape
    blocked = b.ndim == 3
    tm = _row_tile(m, k, tm_want)
    tn = _tile(n, 1024)
    tk = b.shape[-1] if blocked else k
    extras, extra_specs, out_shape, out_specs = _mm_operands(m, n, tm, tn, out_dtypes, extras, rows, sums)
    if blocked:
        b_spec = pl.BlockSpec((None, tn, tk), lambda i, j, kk: (kk, j, 0))
    else:
        b_spec = pl.BlockSpec((tn, tk), lambda i, j, kk: (j, kk))
    return _matmul(
        name, a, b, kind="nt", m=m, n=n, k=k, tm=tm, tn=tn, tk=tk,
        a_spec=pl.BlockSpec((tm, tk), lambda i, j, kk: (i, kk)), b_spec=b_spec, out_shape=out_shape, out_specs=out_specs,
        epilogue=epilogue, extras=extras, extra_specs=extra_specs, comm=comm)


def _mm_out_proj_grads(name, grad16, w, act):
    t, n = grad16.shape
    k = w.shape[0]
    tm = _tile(t, 1024)

    def body(g_ref, w_ref, a_ref, din_ref, dw_ref):
        g = g_ref[...]
        din_ref[...] = lax.dot_general(g, w_ref[...], (((1,), (1,)), ((), ())), preferred_element_type=F32)
        part = lax.dot_general(a_ref[...], g, (((0,), (0,)), ((), ())), preferred_element_type=F32)

        @pl.when(pl.program_id(0) == 0)
        def _():
            dw_ref[...] = part

        @pl.when(pl.program_id(0) > 0)
        def _():
            dw_ref[...] += part

    return _call(
        body, name=name, grid=(t // tm,),
        in_specs=[pl.BlockSpec((tm, n), lambda i: (i, 0)), pl.BlockSpec((k, n), lambda i: (0, 0)),
                  pl.BlockSpec((tm, k), lambda i: (i, 0))],
        out_specs=[pl.BlockSpec((tm, k), lambda i: (i, 0)), pl.BlockSpec((k, n), lambda i: (0, 0))],
        out_shape=[jax.ShapeDtypeStruct((t, k), F32), jax.ShapeDtypeStruct((k, n), F32)], args=(grad16, w, act))


def _mm_tn(name, a, b, *, col_blocks=0, comm=None):
    t, k = a.shape
    n = b.shape[1]
    tm = _row_tile(k, t, None)
    tn = n // col_blocks if col_blocks else _tile(n, 1024)
    tk = t
    if col_blocks:
        out_shape = [jax.ShapeDtypeStruct((col_blocks, k, tn), F32)]
        out_specs = [pl.BlockSpec((None, tm, tn), lambda i, j, kk: (j, i, 0))]
    else:
        out_shape = [jax.ShapeDtypeStruct((k, n), F32)]
        out_specs = [pl.BlockSpec((tm, tn), lambda i, j, kk: (i, j))]

    def epilogue(acc, extra_refs, out_refs):
        out_refs[0][...] = acc

    return _matmul(
        name, a, b, kind="tn", m=k, n=n, k=t, tm=tm, tn=tn, tk=tk,
        a_spec=pl.BlockSpec((tk, tm), lambda i, j, kk: (kk, i)), b_spec=pl.BlockSpec((tk, tn), lambda i, j, kk: (kk, j)),
        out_shape=out_shape, out_specs=out_specs, epilogue=epilogue, comm=comm)[0]


def _ep_store(acc, extra_refs, out_refs):
    out_refs[0][...] = acc.astype(out_refs[0].dtype)


def _ep_bias(acc, extra_refs, out_refs):
    out_refs[0][...] = acc + extra_refs[0][...]


def _rms(x):
    r = lax.rsqrt(jnp.mean(x * x, axis=-1, keepdims=True) + RMS_EPS)
    return r, x * r


def _rms_grad(dn, r, xr, g):
    dy = dn * g
    return r * (dy - xr * jnp.mean(dy * xr, axis=-1, keepdims=True))


def _ep_residual_norm(acc, extra_refs, out_refs):
    h = extra_refs[0][...] + acc
    out_refs[0][...] = h
    _, xr = _rms(h)
    out_refs[1][...] = (xr * extra_refs[1][...]).astype(BF16)


def _ep_final_loss(acc, extra_refs, out_refs):
    @pl.when(pl.program_id(0) == 0)
    def _():
        out_refs[2][...] = jnp.zeros_like(out_refs[2])
        out_refs[3][...] = jnp.zeros_like(out_refs[3])

    h = extra_refs[0][...] + acc
    g = extra_refs[2][...]
    r, xr = _rms(h)
    err = xr * g - extra_refs[1][...]
    out_refs[3][...] += 0.5 * jnp.sum(jnp.mean(err * err, axis=-1, keepdims=True), axis=0, keepdims=True)
    dout = err * (1.0 / h.shape[-1])
    out_refs[2][...] += _colsum(dout * xr)
    out = _rms_grad(dout, r, xr, g)
    out_refs[0][...] = out
    out_refs[1][...] = out.astype(BF16)


def _ep_relu_sq(acc, extra_refs, out_refs):
    r = jnp.maximum(acc, 0.0)
    out_refs[0][...] = (r * r).astype(BF16)


def _ep_relu_sq_grad(acc, extra_refs, out_refs):
    q = extra_refs[0][...].astype(F32)
    out_refs[0][...] = (acc * (2.0 * q * lax.rsqrt(jnp.maximum(q, TINY)))).astype(BF16)


def _rms_fwd(name, h, g, comm=None, casts=()):
    t, d = h.shape
    tt = _tile(t, 512)
    steps = t // tt
    n_cast = len(casts)

    def body(h_ref, g_ref, *rest):
        n_ref = rest[n_cast]
        x = h_ref[...]
        r = lax.rsqrt(jnp.mean(x * x, axis=-1, keepdims=True) + RMS_EPS)
        n_ref[...] = (x * r * g_ref[...]).astype(BF16)
        for src, dst in zip(rest[:n_cast], rest[n_cast + 1:]):
            dst[...] = src[...].astype(BF16)

    cast_in = [pl.BlockSpec((None, w.shape[1] // steps, w.shape[2]), lambda i, l=l: (l, i, 0)) for w, l in casts]
    cast_out = [pl.BlockSpec((w.shape[1] // steps, w.shape[2]), lambda i: (i, 0)) for w, _ in casts]
    res = _call(
        body, name=name, grid=(steps,),
        in_specs=[pl.BlockSpec((tt, d), lambda i: (i, 0)), pl.BlockSpec((1, d), lambda i: (0, 0)), *cast_in],
        out_specs=[pl.BlockSpec((tt, d), lambda i: (i, 0)), *cast_out],
        out_shape=[jax.ShapeDtypeStruct((t, d), BF16)] + [jax.ShapeDtypeStruct(w.shape[1:], BF16) for w, _ in casts],
        args=(h, g, *[w for w, _ in casts]), parallel=True, comm=comm)
    return res[0], res[1:]


def _rms_bwd(name, dn, h, g, grad_in, comm=None):
    t, d = h.shape
    tt = _tile(t, 512)

    def body(dn_ref, h_ref, g_ref, gin_ref, gout_ref, gout16_ref, dg_ref):
        @pl.when(pl.program_id(0) == 0)
        def _():
            dg_ref[...] = jnp.zeros_like(dg_ref)

        dnv = dn_ref[...]
        r, xr = _rms(h_ref[...])
        dg_ref[...] += _colsum(dnv * xr)
        out = gin_ref[...] + _rms_grad(dnv, r, xr, g_ref[...])
        gout_ref[...] = out
        gout16_ref[...] = out.astype(BF16)

    row = pl.BlockSpec((tt, d), lambda i: (i, 0))
    vec = pl.BlockSpec((1, d), lambda i: (0, 0))
    return _call(
        body, name=name, grid=(t // tt,), in_specs=[row, row, vec, row], out_specs=[row, row, vec],
        out_shape=[jax.ShapeDtypeStruct((t, d), F32), jax.ShapeDtypeStruct((t, d), BF16), jax.ShapeDtypeStruct((1, d), F32)],
        args=(dn, h, g, grad_in), comm=comm)


def _mixer_windows(z_ref, zh_ref, first, a1_s, cb_s, tt):
    sig = _sigmoid(z_ref[:, A_DIM:2 * A_DIM])
    if a1_s is not None:
        a1_s[A_HALO:A_HALO + tt, :] = z_ref[:, 0:A_DIM] * sig
        a1_h = zh_ref[:, 0:A_DIM] * _sigmoid(zh_ref[:, A_DIM:2 * A_DIM])
        a1_s[0:A_HALO, :] = jnp.where(first, 0.0, a1_h)
    cb_s[B_HALO:B_HALO + tt, :] = z_ref[:, 3 * A_DIM:4 * A_DIM] * z_ref[:, 4 * A_DIM:5 * A_DIM]
    cb_h = zh_ref[A_HALO - B_HALO:A_HALO, 3 * A_DIM:4 * A_DIM] * zh_ref[A_HALO - B_HALO:A_HALO, 4 * A_DIM:5 * A_DIM]
    cb_s[0:B_HALO, :] = jnp.where(first, 0.0, cb_h)
    return sig


def _causal_conv(win_s, w_ref, taps, halo, tt):
    base = halo - (taps - 1)
    acc = w_ref[0:1, :] * win_s[pl.ds(base, tt), :]
    for k in range(1, taps):
        acc = acc + w_ref[k:k + 1, :] * win_s[pl.ds(base + k, tt), :]
    return acc


SUBLANES = 8
LANE_BLOCK = 128
ROW_BLOCK = 128
SHIFT_ROWS = A_HALO - SUBLANES


def _shifted_copies(win_s, sh_s, tt):
    for b in range(1, SUBLANES):
        sh_s[b - 1] = win_s[pl.ds(b, tt + SHIFT_ROWS), :]


def _window_rows(win_s, sh_s, offset, rows, cols):
    b = offset % SUBLANES
    if b == 0:
        return win_s[pl.ds(offset, rows), cols]
    return sh_s[b - 1, pl.ds(offset - b, rows), cols]


def _blocks(tt):
    rb = min(tt, ROW_BLOCK)
    return rb, [(r, slice(lb * LANE_BLOCK, (lb + 1) * LANE_BLOCK))
                for lb in range(A_DIM // LANE_BLOCK) for r in range(0, tt, rb)]


def _conv_taps(win_s, sh_s, w_ref, offsets, out_s, tt, bias_ref=None):
    rb, blocks = _blocks(tt)
    for r, cols in blocks:
        acc = w_ref[0:1, cols] * _window_rows(win_s, sh_s, r + offsets[0], rb, cols)
        for k in range(1, len(offsets)):
            acc = acc + w_ref[k:k + 1, cols] * _window_rows(win_s, sh_s, r + offsets[k], rb, cols)
        out_s[r:r + rb, cols] = acc if bias_ref is None else acc + bias_ref[:, cols]


A_CAUSAL = [A_HALO - (A_TAPS - 1) + k for k in range(A_TAPS)]
A_ANTICAUSAL = [A_TAPS - 1 - k for k in range(A_TAPS)]


def _layer_norm_stats(x):
    mu = jnp.mean(x, axis=-1, keepdims=True)
    xc = x - mu
    rstd = lax.rsqrt(jnp.mean(xc * xc, axis=-1, keepdims=True) + LN_EPS)
    return xc * rstd, rstd


def _mixer_specs(seq, tt):
    tiles_per_seq = seq // tt
    halo_blocks = tt // A_HALO
    z_spec = pl.BlockSpec((tt, IN_EVEN), lambda i: (i, 0))
    zh_spec = pl.BlockSpec((A_HALO, IN_EVEN), lambda i: (jnp.maximum(i * halo_blocks - 1, 0), 0))
    return tiles_per_seq, z_spec, zh_spec


def _vec_spec(rows, cols):
    return pl.BlockSpec((rows, cols), lambda i: (0, 0))


def _mixer_fwd(z, seq, caw, cab, lag, lab, cbw, comm=None):
    t = z.shape[0]
    tt = _tile(seq, 256)
    tiles_per_seq, z_spec, zh_spec = _mixer_specs(seq, tt)

    def body(z_ref, zh_ref, caw_ref, cab_ref, lag_ref, lab_ref, cbw_ref, mix_ref, a2_ref, a1_s, cb_s, sh_s):
        first = (pl.program_id(0) % tiles_per_seq) == 0
        _mixer_windows(z_ref, zh_ref, first, a1_s, cb_s, tt)
        _shifted_copies(a1_s, sh_s, tt)
        _conv_taps(a1_s, sh_s, caw_ref, A_CAUSAL, a2_ref, tt, bias_ref=cab_ref)
        xhat, _ = _layer_norm_stats(a2_ref[...])
        a3 = xhat * lag_ref[...] + lab_ref[...]
        mix_ref[:, 0:A_DIM] = (a3 * _sigmoid(a3)).astype(BF16)
        cv = _causal_conv(cb_s, cbw_ref, B_TAPS, B_HALO, tt)
        mix_ref[:, A_DIM:A_DIM + B_DIM] = (z_ref[:, 2 * A_DIM:3 * A_DIM] * cv).astype(BF16)

    return _call(
        body, name="mixer_fwd", grid=(t // tt,),
        in_specs=[z_spec, zh_spec, _vec_spec(A_TAPS, A_DIM), _vec_spec(1, A_DIM), _vec_spec(1, A_DIM), _vec_spec(1, A_DIM),
                  _vec_spec(B_TAPS, B_DIM)],
        out_specs=[pl.BlockSpec((tt, A_DIM + B_DIM), lambda i: (i, 0)), pl.BlockSpec((tt, A_DIM), lambda i: (i, 0))],
        out_shape=[jax.ShapeDtypeStruct((t, A_DIM + B_DIM), BF16), jax.ShapeDtypeStruct((t, A_DIM), F32)],
        scratch_shapes=[pltpu.VMEM((A_HALO + tt, A_DIM), F32), pltpu.VMEM((B_HALO + tt, B_DIM), F32),
                        pltpu.VMEM((SUBLANES - 1, tt + SHIFT_ROWS, A_DIM), F32)],
        args=(z, z, caw, cab, lag, lab, cbw), parallel=True, comm=comm)


def _mixer_bwd_local(z, a2, dmix, seq, lag, lab, comm=None):
    t = z.shape[0]
    tt = _tile(seq, 256)
    tiles_per_seq, z_spec, zh_spec = _mixer_specs(seq, tt)

    def body(z_ref, zh_ref, a2_ref, dmix_ref, lag_ref, lab_ref,
             da2_ref, dcv_ref, dcaw_ref, dcab_ref, dlag_ref, dlab_ref, dcbw_ref, a1_s, cb_s, sh_s):
        @pl.when(pl.program_id(0) == 0)
        def _():
            for ref in (dcaw_ref, dcab_ref, dlag_ref, dlab_ref, dcbw_ref):
                ref[...] = jnp.zeros_like(ref)

        first = (pl.program_id(0) % tiles_per_seq) == 0
        _mixer_windows(z_ref, zh_ref, first, a1_s, cb_s, tt)
        _shifted_copies(a1_s, sh_s, tt)
        xhat, rstd = _layer_norm_stats(a2_ref[...])
        a3 = xhat * lag_ref[...] + lab_ref[...]
        s3 = _sigmoid(a3)
        da3 = dmix_ref[:, 0:A_DIM] * (s3 * (1.0 + a3 * (1.0 - s3)))
        dlag_ref[...] += _colsum(da3 * xhat)
        dlab_ref[...] += _colsum(da3)
        dxh = da3 * lag_ref[...]
        da2 = rstd * (dxh - jnp.mean(dxh, axis=-1, keepdims=True) - xhat * jnp.mean(dxh * xhat, axis=-1, keepdims=True))
        da2_ref[...] = da2
        dcab_ref[...] += _colsum(da2)
        rb, blocks = _blocks(tt)
        for r, cols in blocks:
            da2_b = da2_ref[r:r + rb, cols]
            for k in range(A_TAPS):
                dcaw_ref[k:k + 1, cols] += _colsum(da2_b * _window_rows(a1_s, sh_s, r + A_CAUSAL[k], rb, cols))
        dcv =dmix_ref[:, A_DIM:A_DIM + B_DIM] * z_ref[:, 2 * A_DIM:3 * A_DIM]
        dcv_ref[...] = dcv
        for k in range(B_TAPS):
            dcbw_ref[k:k + 1, :] += _colsum(dcv * cb_s[pl.ds(B_HALO - (B_TAPS - 1) + k, tt), :])

    half = pl.BlockSpec((tt, A_DIM), lambda i: (i, 0))
    return _call(
        body, name="mixer_bwd_local", grid=(t // tt,),
        in_specs=[z_spec, zh_spec, half, pl.BlockSpec((tt, A_DIM + B_DIM), lambda i: (i, 0)),
                  _vec_spec(1, A_DIM), _vec_spec(1, A_DIM)],
        out_specs=[half, half, _vec_spec(A_TAPS, A_DIM), _vec_spec(1, A_DIM), _vec_spec(1, A_DIM), _vec_spec(1, A_DIM),
                   _vec_spec(B_TAPS, B_DIM)],
        out_shape=[jax.ShapeDtypeStruct((t, A_DIM), F32), jax.ShapeDtypeStruct((t, B_DIM), F32),
                   jax.ShapeDtypeStruct((A_TAPS, A_DIM), F32), jax.ShapeDtypeStruct((1, A_DIM), F32),
                   jax.ShapeDtypeStruct((1, A_DIM), F32), jax.ShapeDtypeStruct((1, A_DIM), F32),
                   jax.ShapeDtypeStruct((B_TAPS, B_DIM), F32)],
        scratch_shapes=[pltpu.VMEM((A_HALO + tt, A_DIM), F32), pltpu.VMEM((B_HALO + tt, B_DIM), F32),
                        pltpu.VMEM((SUBLANES - 1, tt + SHIFT_ROWS, A_DIM), F32)],
        args=(z, z, a2, dmix, lag, lab), comm=comm)


def _mixer_bwd_input(z, dmix, da2, dcv, seq, caw, cbw, comm=None):
    t = z.shape[0]
    tt = _tile(seq, 256)
    tiles_per_seq, z_spec, zh_spec = _mixer_specs(seq, tt)
    a_blocks = tt // A_HALO
    b_blocks = tt // B_HALO
    last_a = t // A_HALO - 1
    last_b = t // B_HALO - 1

    def body(z_ref, zh_ref, dmix_ref, da2_ref, da2n_ref, dcv_ref, dcvn_ref, caw_ref, cbw_ref, dz_ref, cb_s, da2_s, dcv_s,
             sh_s, da1_s):
        pos = pl.program_id(0) % tiles_per_seq
        first = pos == 0
        last = pos == tiles_per_seq - 1
        sig = _mixer_windows(z_ref, zh_ref, first, None, cb_s, tt)
        da2_s[0:tt, :] = da2_ref[...]
        da2_s[tt:tt + A_HALO, :] = jnp.where(last, 0.0, da2n_ref[...])
        dcv_s[0:tt, :] = dcv_ref[...]
        dcv_s[tt:tt + B_HALO, :] = jnp.where(last, 0.0, dcvn_ref[...])
        _shifted_copies(da2_s, sh_s, tt)
        _conv_taps(da2_s, sh_s, caw_ref, A_ANTICAUSAL, da1_s, tt)
        da1 = da1_s[...]
        dz_ref[:, 0:A_DIM] = (da1 * sig).astype(BF16)
        dz_ref[:, A_DIM:2 * A_DIM] = (da1 * z_ref[:, 0:A_DIM] * sig * (1.0 - sig)).astype(BF16)
        cv = _causal_conv(cb_s, cbw_ref, B_TAPS, B_HALO, tt)
        dz_ref[:, 2 * A_DIM:3 * A_DIM] = (dmix_ref[:, A_DIM:A_DIM + B_DIM] * cv).astype(BF16)
        dcb = cbw_ref[0:1, :] * dcv_s[pl.ds(B_TAPS - 1, tt), :]
        for k in range(1, B_TAPS):
            dcb = dcb + cbw_ref[k:k + 1, :] * dcv_s[pl.ds(B_TAPS - 1 - k, tt), :]
        dz_ref[:, 3 * A_DIM:4 * A_DIM] = (dcb * z_ref[:, 4 * A_DIM:5 * A_DIM]).astype(BF16)
        dz_ref[:, 4 * A_DIM:5 * A_DIM] = (dcb * z_ref[:, 3 * A_DIM:4 * A_DIM]).astype(BF16)

    half = pl.BlockSpec((tt, A_DIM), lambda i: (i, 0))
    return _call(
        body, name="mixer_bwd_input", grid=(t // tt,),
        in_specs=[z_spec, zh_spec, pl.BlockSpec((tt, A_DIM + B_DIM), lambda i: (i, 0)),
                  half, pl.BlockSpec((A_HALO, A_DIM), lambda i: (jnp.minimum((i + 1) * a_blocks, last_a), 0)),
                  half, pl.BlockSpec((B_HALO, B_DIM), lambda i: (jnp.minimum((i + 1) * b_blocks, last_b), 0)),
                  _vec_spec(A_TAPS, A_DIM), _vec_spec(B_TAPS, B_DIM)],
        out_specs=[pl.BlockSpec((tt, IN_EVEN), lambda i: (i, 0))],
        out_shape=[jax.ShapeDtypeStruct((t, IN_EVEN), BF16)],
        scratch_shapes=[pltpu.VMEM((B_HALO + tt, B_DIM), F32),
                        pltpu.VMEM((tt + A_HALO, A_DIM), F32), pltpu.VMEM((tt + B_HALO, B_DIM), F32),
                        pltpu.VMEM((SUBLANES - 1, tt + SHIFT_ROWS, A_DIM), F32), pltpu.VMEM((tt, A_DIM), F32)],
        args=(z, z, dmix, da2, da2, dcv, dcv, caw, cbw), parallel=True, comm=comm)[0]


def _tril_ws(ws_ref, g):
    rows = lax.broadcasted_iota(jnp.int32, (CHUNK, CHUNK), 0)
    cols = lax.broadcasted_iota(jnp.int32, (CHUNK, CHUNK), 1)
    return jnp.where(rows >= cols, ws_ref[g], 0.0).astype(BF16), rows >= cols


def _sgu_fwd(pre, lvg, lvb, ws, bs_b, comm=None):
    t = pre.shape[0]
    tt = _tile(t, 256)

    def body(pre_ref, lvg_ref, lvb_ref, ws_ref, bsb_ref, y_ref):
        vhat, _ = _layer_norm_stats(_gelu(pre_ref[:, C_DIM:2 * C_DIM]))
        vl = (vhat * lvg_ref[...] + lvb_ref[...]).astype(BF16)
        for g in range(C_GROUPS):
            w, _ = _tril_ws(ws_ref, g)
            cols = slice(g * CHUNK, (g + 1) * CHUNK)
            for ci in range(tt // CHUNK):
                rows = slice(ci * CHUNK, (ci + 1) * CHUNK)
                sv = jnp.dot(w, vl[rows, cols], preferred_element_type=F32) + bsb_ref[g]
                y_ref[rows, cols] = (_gelu(pre_ref[rows, cols]) * sv).astype(BF16)

    group = pl.BlockSpec((C_GROUPS, CHUNK, CHUNK), lambda i: (0, 0, 0))
    return _call(
        body, name="sgu_fwd", grid=(t // tt,),
        in_specs=[pl.BlockSpec((tt, 2 * C_DIM), lambda i: (i, 0)), _vec_spec(1, C_DIM), _vec_spec(1, C_DIM), group, group],
        out_specs=[pl.BlockSpec((tt, C_DIM), lambda i: (i, 0))], out_shape=[jax.ShapeDtypeStruct((t, C_DIM), BF16)],
        args=(pre, lvg, lvb, ws, bs_b), parallel=True, comm=comm)[0]


def _sgu_bwd(pre, dy, lvg, lvb, ws, bs_b, comm=None):
    t = pre.shape[0]
    tt = _tile(t, 256)

    def body(pre_ref, dy_ref, lvg_ref, lvb_ref, ws_ref, bsb_ref, dpre_ref, dws_ref, dbsb_ref, dlvg_ref, dlvb_ref, dbin_ref,
             dvl_s):
        @pl.when(pl.program_id(0) == 0)
        def _():
            for ref in (dws_ref, dbsb_ref, dlvg_ref, dlvb_ref, dbin_ref):
                ref[...] = jnp.zeros_like(ref)

        v, v_grad = _gelu_and_grad(pre_ref[:, C_DIM:2 * C_DIM])
        vhat, rstd = _layer_norm_stats(v)
        vl = (vhat * lvg_ref[...] + lvb_ref[...]).astype(BF16)
        for g in range(C_GROUPS):
            w, keep = _tril_ws(ws_ref, g)
            cols = slice(g * CHUNK, (g + 1) * CHUNK)
            dws = jnp.zeros((CHUNK, CHUNK), F32)
            dbs = jnp.zeros((CHUNK, 1), F32)
            for ci in range(tt // CHUNK):
                rows = slice(ci * CHUNK, (ci + 1) * CHUNK)
                vl_g = vl[rows, cols]
                sv = jnp.dot(w, vl_g, preferred_element_type=F32) + bsb_ref[g]
                u, u_grad = _gelu_and_grad(pre_ref[rows, cols])
                dyv = dy_ref[rows, cols]
                du = dyv * sv * u_grad
                dpre_ref[rows, cols] = du.astype(BF16)
                dbin_ref[:, cols] += _colsum(du)
                dsv = dyv * u
                dbs = dbs + jnp.sum(dsv, axis=1, keepdims=True)
                dsv16 = dsv.astype(BF16)
                dws = dws + lax.dot_general(dsv16, vl_g, (((1,), (1,)), ((), ())), preferred_element_type=F32)
                dvl_s[rows, cols] = lax.dot_general(w, dsv16, (((0,), (0,)), ((), ())), preferred_element_type=F32)
            dws_ref[g] += jnp.where(keep, dws, 0.0)
            dbsb_ref[g] += dbs
        dvl = dvl_s[...]
        dlvg_ref[...] += _colsum(dvl * vhat)
        dlvb_ref[...] += _colsum(dvl)
        dxh = dvl * lvg_ref[...]
        dv = rstd * (dxh - jnp.mean(dxh, axis=-1, keepdims=True) - vhat * jnp.mean(dxh * vhat, axis=-1, keepdims=True))
        dpv = dv * v_grad
        dpre_ref[:, C_DIM:2 * C_DIM] = dpv.astype(BF16)
        dbin_ref[:, C_DIM:2 * C_DIM] += _colsum(dpv)

    group = pl.BlockSpec((C_GROUPS, CHUNK, CHUNK), lambda i: (0, 0, 0))
    return _call(
        body, name="sgu_bwd", grid=(t // tt,),
        in_specs=[pl.BlockSpec((tt, 2 * C_DIM), lambda i: (i, 0)), pl.BlockSpec((tt, C_DIM), lambda i: (i, 0)),
                  _vec_spec(1, C_DIM), _vec_spec(1, C_DIM), group, group],
        out_specs=[pl.BlockSpec((tt, 2 * C_DIM), lambda i: (i, 0)), group, group,
                   _vec_spec(1, C_DIM), _vec_spec(1, C_DIM), _vec_spec(1, 2 * C_DIM)],
        out_shape=[jax.ShapeDtypeStruct((t, 2 * C_DIM), BF16), jax.ShapeDtypeStruct((C_GROUPS, CHUNK, CHUNK), F32),
                   jax.ShapeDtypeStruct((C_GROUPS, CHUNK, CHUNK), F32), jax.ShapeDtypeStruct((1, C_DIM), F32),
                   jax.ShapeDtypeStruct((1, C_DIM), F32), jax.ShapeDtypeStruct((1, 2 * C_DIM), F32)],
        scratch_shapes=[pltpu.VMEM((tt, C_DIM), F32)],
        args=(pre, dy, lvg, lvb, ws, bs_b), comm=comm)


def _pair_sum(name, part, got, core):
    _, k, n = part.shape
    tk = _tile(k, 1024)

    def body(core_ref, p_ref, s_ref, o_ref):
        o_ref[...] = (p_ref[...] + s_ref[...]).astype(BF16)

    return pl.pallas_call(
        body, name=name,
        grid_spec=pltpu.PrefetchScalarGridSpec(
            num_scalar_prefetch=1, grid=(N_CHIP, k // tk),
            in_specs=[pl.BlockSpec((None, tk, n), lambda q, i, core_ref: (2 * q + core_ref[0], i, 0)),
                      pl.BlockSpec((None, tk, n), lambda q, i, core_ref: (q, i, 0))],
            out_specs=pl.BlockSpec((None, tk, n), lambda q, i, core_ref: (q, i, 0))),
        out_shape=jax.ShapeDtypeStruct((N_CHIP, k, n), BF16),
        compiler_params=pltpu.CompilerParams(dimension_semantics=("parallel", "parallel"), vmem_limit_bytes=VMEM_LIMIT_BYTES),
    )(core, part, got)


def _adamw_math(w, g, m, v):
    m = ADAM_B1 * m + (1.0 - ADAM_B1) * g
    v = ADAM_B2 * v + (1.0 - ADAM_B2) * (g * g)
    m_hat = m / (1.0 - ADAM_B1 ** ADAM_STEP)
    v_hat = v / (1.0 - ADAM_B2 ** ADAM_STEP)
    delta = -ADAM_LR * (m_hat / (jnp.sqrt(v_hat) + ADAM_EPS) + ADAM_WD * w)
    return delta, m, v


def _sum_adamw(name, parts, w, m, v, comm=None):
    layers = len(parts)
    n_parts, k, n = parts[0].shape
    tk = _tile(k, 256)

    def body(*refs):
        p_refs = refs[:layers]
        w_ref, m_ref, v_ref, g_ref, d_ref, nm_ref, nv_ref = refs[layers:]

        def total(p_ref):
            g = p_ref[0].astype(F32)
            for q in range(1, n_parts):
                g = g + p_ref[q].astype(F32)
            return g

        g = total(p_refs[0])
        for l in range(1, layers):
            g = jnp.where(pl.program_id(0) == l, total(p_refs[l]), g)
        g_ref[...] = g
        d_ref[...], nm_ref[...], nv_ref[...] = _adamw_math(w_ref[...], g, m_ref[...], v_ref[...])

    blk = pl.BlockSpec((None, tk, n), lambda l, i: (l, i, 0))
    return _call(
        body, name=name, grid=(layers, k // tk),
        in_specs=[pl.BlockSpec((n_parts, tk, n), lambda l, i: (0, i, 0))] * layers + [blk, blk, blk], out_specs=[blk] * 4,
        out_shape=[jax.ShapeDtypeStruct((layers, k, n), F32)] * 4, args=(*parts, w, m, v), parallel=True, comm=comm)


def _small_adamw(name, parts, w, m, v, losses):
    count = len(parts)

    def in_order(ref):
        total = ref[0]
        for dev in range(1, N_DEV):
            total = total + ref[dev]
        return total

    def body(*refs):
        p_refs, w_refs, m_refs, v_refs = (refs[j * count:(j + 1) * count] for j in range(4))
        losses_ref = refs[4 * count]
        g_refs, d_refs, nm_refs, nv_refs = (refs[4 * count + 1 + j * count:4 * count + 1 + (j + 1) * count] for j in range(4))
        loss_ref = refs[8 * count + 1]
        for i in range(count):
            g = in_order(p_refs[i])
            g_refs[i][...] = g
            d_refs[i][...], nm_refs[i][...], nv_refs[i][...] = _adamw_math(w_refs[i][...], g, m_refs[i][...], v_refs[i][...])
        loss_ref[...] = in_order(losses_ref)

    res = pl.pallas_call(
        body, name=name,
        out_shape=[jax.ShapeDtypeStruct(a.shape, F32) for a in w] * 4 + [jax.ShapeDtypeStruct(losses.shape[1:], F32)],
        compiler_params=pltpu.CompilerParams(vmem_limit_bytes=VMEM_LIMIT_BYTES))(*parts, *w, *m, *v, losses)
    return [res[j * count:(j + 1) * count] for j in range(4)], res[4 * count]


def _rows(a):
    return a.reshape(-1, a.shape[-1])


def _whole(gathered):
    return jnp.transpose(gathered, (1, 0, 2)).reshape(gathered.shape[1], -1)


SMALL =("ev_norm_g", "ev_conv_a_w", "ev_conv_a_b", "ev_ln_a_g", "ev_ln_a_b", "ev_conv_b_w", "od_norm_g", "od_b_in",
         "od_ln_v_g", "od_ln_v_b", "od_w_s", "od_b_s", "mlp_norm_g", "final_norm_g")
SMALL_SHARDED = ("ev_conv_a_w", "ev_conv_b_w", "od_norm_g", "od_b_in", "od_ln_v_g", "od_ln_v_b")
ORDER = ("ev_norm_g", "ev_w_in", "ev_conv_a_w", "ev_conv_a_b", "ev_ln_a_g", "ev_ln_a_b", "ev_conv_b_w", "ev_w_out",
         "od_norm_g", "od_w_in", "od_b_in", "od_ln_v_g", "od_ln_v_b", "od_w_s", "od_b_s", "od_w_out", "mlp_norm_g",
         "mlp_w1", "mlp_w2", "final_norm_g")


def kernel(x, ev_norm_g, ev_w_in, ev_conv_a_w, ev_conv_a_b, ev_ln_a_g, ev_ln_a_b, ev_conv_b_w, ev_w_out, od_norm_g, od_w_in, od_b_in, od_ln_v_g, od_ln_v_b, od_w_s, od_b_s, od_w_out, mlp_norm_g, mlp_w1, mlp_w2, final_norm_g, loss_target, m_ev_norm_g, m_ev_w_in, m_ev_conv_a_w, m_ev_conv_a_b, m_ev_ln_a_g, m_ev_ln_a_b, m_ev_conv_b_w, m_ev_w_out, m_od_norm_g, m_od_w_in, m_od_b_in, m_od_ln_v_g, m_od_ln_v_b, m_od_w_s, m_od_b_s, m_od_w_out, m_mlp_norm_g, m_mlp_w1, m_mlp_w2, m_final_norm_g, v_ev_norm_g, v_ev_w_in, v_ev_conv_a_w, v_ev_conv_a_b, v_ev_ln_a_g, v_ev_ln_a_b, v_ev_conv_b_w, v_ev_w_out, v_od_norm_g, v_od_w_in, v_od_b_in, v_od_ln_v_g, v_od_ln_v_b, v_od_w_s, v_od_b_s, v_od_w_out, v_mlp_norm_g, v_mlp_w1, v_mlp_w2, v_final_norm_g):
    W = dict(ev_norm_g=ev_norm_g, ev_w_in=ev_w_in, ev_conv_a_w=ev_conv_a_w, ev_conv_a_b=ev_conv_a_b, ev_ln_a_g=ev_ln_a_g,
             ev_ln_a_b=ev_ln_a_b, ev_conv_b_w=ev_conv_b_w, ev_w_out=ev_w_out, od_norm_g=od_norm_g, od_w_in=od_w_in,
             od_b_in=od_b_in, od_ln_v_g=od_ln_v_g, od_ln_v_b=od_ln_v_b, od_w_s=od_w_s, od_b_s=od_b_s, od_w_out=od_w_out,
             mlp_norm_g=mlp_norm_g, mlp_w1=mlp_w1, mlp_w2=mlp_w2, final_norm_g=final_norm_g)
    M = dict(ev_norm_g=m_ev_norm_g, ev_w_in=m_ev_w_in, ev_conv_a_w=m_ev_conv_a_w, ev_conv_a_b=m_ev_conv_a_b,
             ev_ln_a_g=m_ev_ln_a_g, ev_ln_a_b=m_ev_ln_a_b, ev_conv_b_w=m_ev_conv_b_w, ev_w_out=m_ev_w_out,
             od_norm_g=m_od_norm_g, od_w_in=m_od_w_in, od_b_in=m_od_b_in, od_ln_v_g=m_od_ln_v_g, od_ln_v_b=m_od_ln_v_b,
             od_w_s=m_od_w_s, od_b_s=m_od_b_s, od_w_out=m_od_w_out, mlp_norm_g=m_mlp_norm_g, mlp_w1=m_mlp_w1,
             mlp_w2=m_mlp_w2, final_norm_g=m_final_norm_g)
    V = dict(ev_norm_g=v_ev_norm_g, ev_w_in=v_ev_w_in, ev_conv_a_w=v_ev_conv_a_w, ev_conv_a_b=v_ev_conv_a_b,
             ev_ln_a_g=v_ev_ln_a_g, ev_ln_a_b=v_ev_ln_a_b, ev_conv_b_w=v_ev_conv_b_w, ev_w_out=v_ev_w_out,
             od_norm_g=v_od_norm_g, od_w_in=v_od_w_in, od_b_in=v_od_b_in, od_ln_v_g=v_od_ln_v_g, od_ln_v_b=v_od_ln_v_b,
             od_w_s=v_od_w_s, od_b_s=v_od_b_s, od_w_out=v_od_w_out, mlp_norm_g=v_mlp_norm_g, mlp_w1=v_mlp_w1,
             mlp_w2=v_mlp_w2, final_norm_g=v_final_norm_g)

    n_seq, seq, d = x.shape
    t = n_seq * seq
    dev = 4 * lax.axis_index("x") + 2 * lax.axis_index("y") + lax.axis_index("c")
    core = lax.axis_index("c").astype(jnp.int32).reshape(1)

    ev_g, cab, lag, lab = W["ev_norm_g"], W["ev_conv_a_b"], W["ev_ln_a_g"], W["ev_ln_a_b"]
    ws = W["od_w_s"][0]
    bs_b = jnp.broadcast_to(W["od_b_s"][0][:, :, None], (C_GROUPS, CHUNK, CHUNK))
    mlp_g = [W["mlp_norm_g"][l:l + 1] for l in range(2)]
    fin_g = W["final_norm_g"].reshape(1, d)

    h0 = x.reshape(t, d)
    for store in (W, M, V):
        store["ev_w_in"] = jnp.transpose(store["ev_w_in"], (0, 2, 1))
    gather = _gather_comm([W["ev_w_in"][0].astype(BF16)])
    later = [(W["ev_w_out"], 0), (W["od_w_in"], 0), (W["od_w_out"], 0), (W["mlp_w1"], 0), (W["mlp_w2"], 0),
             (W["mlp_w1"], 1), (W["mlp_w2"], 1)]
    n0, (w_ev_out16, w_od_in16, w_od_out16, w1_0, w2_0, w1_1, w2_1) = _rms_fwd("ev_norm", h0, ev_g, comm=gather, casts=later)
    w_ev_in_t = gather.out[0].reshape(IN_EVEN, D_MODEL)

    first, second = (0, 2), (1, 2)

    g_a, g_b = _gather_comm([w_ev_out16] + [_rows(W[n]) for n in SMALL_SHARDED]), _gather_comm([w1_0], first)
    z = _mm_nt("ev_in", n0, w_ev_in_t, IN_EVEN, _ep_store, [F32], comm=[g_a, g_b])[0]
    w_ev_out = g_a.out[0].reshape(D_MODEL, D_MODEL)
    caw, cbw, od_g, od_bin, lvg, lvb = [_whole(g) for g in g_a.out[1:]]

    g_c, g_d = _gather_comm([w1_0], second, into=g_b.out), _gather_comm([w2_0], first)
    mix, a2 = _mixer_fwd(z, seq, caw, cab, lag, lab, cbw, comm=[g_c, g_d])
    w1 = [_whole(g_c.out[0]), None]

    g_e = _gather_comm([w_od_in16], first)
    h1, n1 = _mm_nn("ev_out", mix, w_ev_out, d, _ep_residual_norm, [F32, BF16], extras=(h0,), rows=(mlp_g[0],), comm=g_e)

    g_f, g_f2 = _gather_comm([w2_0], second, into=g_d.out), _gather_comm([w_od_in16], second, into=g_e.out)
    q0 = _mm_nn("mlp0_up", n1, w1[0], D_FF, _ep_relu_sq, [BF16], comm=[g_f, g_f2])[0]
    w2 = [g_f.out[0].reshape(D_FF, D_MODEL), None]
    w_od_in = _whole(g_f2.out[0])

    g_g, g_g2 = _gather_comm([w1_1], first), _gather_comm([w_od_out16])
    h2, n2 = _mm_nn("mlp0_down", q0, w2[0], d, _ep_residual_norm, [F32, BF16], extras=(h1,), rows=(od_g,), comm=[g_g, g_g2])
    w_od_out = g_g2.out[0].reshape(D_MODEL, D_MODEL)
    g_h = _gather_comm([w1_1], second, into=g_g.out)
    pre = _mm_nn("od_in", n2, w_od_in, 2 * C_DIM, _ep_bias, [F32], rows=(od_bin,), comm=g_h)[0]
    w1[1] = _whole(g_h.out[0])
    g_i = _gather_comm([w2_1], (0, 4))
    y = _sgu_fwd(pre, lvg, lvb, ws, bs_b, comm=g_i)
    g_i2 = _gather_comm([w2_1], (1, 4), into=g_i.out)
    h3, n3 = _mm_nn("od_out", y, w_od_out, d, _ep_residual_norm, [F32, BF16], extras=(h2,), rows=(mlp_g[1],), comm=g_i2)
    g_j = _gather_comm([w2_1], second, into=g_i2.out)
    q1 = _mm_nn("mlp1_up", n3, w1[1], D_FF, _ep_relu_sq, [BF16], comm=g_j)[0]
    w2[1] = g_j.out[0].reshape(D_FF, D_MODEL)
    grad, grad16, d_fin_g, loss_part = _mm_nn(
        "mlp1_down", q1, w2[1], d, _ep_final_loss, [F32, BF16], extras=(h3, loss_target.reshape(t, d)), rows=(fin_g,),
        sums=(True, False), tm_want=FUSED_ROWS)

    by_chip = {}

    def swap(name, parts):
        comm = _pair_comm([parts])
        comm.parts, comm.weight = parts, name
        return comm

    def exchange(swapped, halves=False):
        sums = _pair_sum(f"pair_sum_{swapped.weight}", swapped.parts, swapped.out[0], core)
        if not halves:
            comm = _chip_comm([sums])
            comm.weight = swapped.weight
            return comm
        comm = _chip_comm([sums], first)
        comm.sums, comm.weight = sums, swapped.weight
        return comm

    def rest(comm):
        other = _chip_comm([comm.sums], second, into=comm.out)
        other.weight = comm.weight
        return other

    def done(comm):
        by_chip[comm.weight] = comm.out[0]

    dw2_1 = _mm_tn("mlp1_dw2", q1, grad16).reshape(N_DEV, D_FF // N_DEV, D_MODEL)
    s_a = swap("w2_1", dw2_1)
    dp = _mm_nt("mlp1_dq", grad16, w2[1], D_FF, _ep_relu_sq_grad, [BF16], extras=(q1,), comm=s_a)[0]
    c_a = exchange(s_a, halves=True)
    dw1_1 = _mm_tn("mlp1_dw1", n3, dp, col_blocks=N_DEV, comm=c_a)
    c_a2, s_b = rest(c_a), swap("w1_1", dw1_1)
    dn = _mm_nt("mlp1_dn", dp, w1[1], d, _ep_store, [F32], comm=[c_a2, s_b])[0]
    done(c_a2)
    c_b = exchange(s_b, halves=True)
    grad, grad16, dg_mlp1 = _rms_bwd("mlp1_dn_norm", dn, h3, mlp_g[1], grad)
    dy, d_od_out = _mm_out_proj_grads("od_out_grads", grad16, w_od_out, y)
    s_c = swap("od_out", d_od_out.reshape(N_DEV, D_MODEL // N_DEV, D_MODEL))
    dpre, d_ws, d_bsb, d_lvg, d_lvb, d_bin = _sgu_bwd(pre, dy, lvg, lvb, ws, bs_b, comm=[c_b, s_c])
    c_c = exchange(s_c)
    c_b2 = rest(c_b)
    d_od_in = _mm_tn("od_dw_in", n2, dpre, col_blocks=N_DEV, comm=[c_b2, c_c])
    done(c_b2)
    done(c_c)
    s_d = swap("od_in", d_od_in)
    dn = _mm_nt("od_dn", dpre, w_od_in, d, _ep_store, [F32], comm=s_d)[0]
    c_d = exchange(s_d)
    grad, grad16, d_od_g = _rms_bwd("od_dn_norm", dn, h2, od_g, grad)
    group_1 = dict(od_norm_g=d_od_g, od_b_in=d_bin, od_ln_v_g=d_lvg, od_ln_v_b=d_lvb,
                   od_w_s=d_ws.reshape(C_GROUPS * CHUNK, CHUNK), od_b_s=d_bsb[:, :, 0], final_norm_g=d_fin_g,
                   mlp_norm_g_1=dg_mlp1, loss=jnp.broadcast_to(loss_part, (SUBLANES, LANE_BLOCK)))
    gather_1 = _gather_comm(list(group_1.values()))
    dw2_0 = _mm_tn("mlp0_dw2", q0, grad16, comm=c_d).reshape(N_DEV, D_FF // N_DEV, D_MODEL)
    done(c_d)
    s_e = swap("w2_0", dw2_0)
    dp = _mm_nt("mlp0_dq", grad16, w2[0], D_FF, _ep_relu_sq_grad, [BF16], extras=(q0,), comm=[s_e, gather_1])[0]
    c_e = exchange(s_e, halves=True)
    dw1_0 = _mm_tn("mlp0_dw1", n1, dp, col_blocks=N_DEV, comm=c_e)
    c_e2, s_f = rest(c_e), swap("w1_0", dw1_0)
    dn = _mm_nt("mlp0_dn", dp, w1[0], d, _ep_store, [F32], comm=[c_e2, s_f])[0]
    done(c_e2)
    c_f = exchange(s_f, halves=True)
    grad, grad16, dg_mlp0 = _rms_bwd("mlp0_dn_norm", dn, h1, mlp_g[0], grad)
    dmix, d_ev_out = _mm_out_proj_grads("ev_out_grads", grad16, w_ev_out, mix)
    s_g = swap("ev_out", d_ev_out.reshape(N_DEV, D_MODEL // N_DEV, D_MODEL))
    da2, dcv, d_caw, d_cab, d_lag, d_lab, d_cbw = _mixer_bwd_local(z, a2, dmix, seq, lag, lab, comm=[c_f, s_g])
    c_g = exchange(s_g)
    c_f2 = rest(c_f)
    dz = _mixer_bwd_input(z, dmix, da2, dcv, seq, caw, cbw, comm=[c_f2, c_g])
    done(c_f2)
    done(c_g)
    group_2 = dict(ev_conv_a_w=d_caw, ev_conv_a_b=d_cab, ev_ln_a_g=d_lag, ev_ln_a_b=d_lab, ev_conv_b_w=d_cbw,
                   mlp_norm_g_0=dg_mlp0)
    gather_2 = _gather_comm(list(group_2.values()))
    d_ev_in_t = _mm_tn("ev_dw_in", dz, n0, comm=gather_2).reshape(N_DEV, IN_EVEN // N_DEV, D_MODEL)
    s_h = swap("ev_in", d_ev_in_t)
    dn = _mm_nn("ev_dn", dz, w_ev_in_t, d, _ep_store, [F32], comm=s_h)[0]
    c_h = exchange(s_h)
    grad_x, _, d_ev_g = _rms_bwd("ev_dn_norm", dn, h0, ev_g, grad, comm=c_h)
    done(c_h)

    last_gather = _gather_comm([d_ev_g])
    shard = {"od_w_out": ("od_out",), "mlp_w1": ("w1_0", "w1_1"), "mlp_w2": ("w2_0", "w2_1"), "od_w_in": ("od_in",),
             "ev_w_out": ("ev_out",), "ev_w_in": ("ev_in",)}
    carried = {"od_w_out": last_gather}
    out_g, out_d, out_m, out_v = {}, {}, {}, {}
    for name, keys in shard.items():
        out_g[name], out_d[name], out_m[name], out_v[name] = _sum_adamw(
            f"adamw_{name}", [by_chip[key] for key in keys], W[name], M[name], V[name], comm=carried.get(name))
    for store in (out_g, out_d, out_m, out_v):
        store["ev_w_in"] = jnp.transpose(store["ev_w_in"], (0, 2, 1))

    gathered = dict(zip(group_1, gather_1.out), **dict(zip(group_2, gather_2.out)), ev_norm_g=last_gather.out[0])
    gathered["mlp_norm_g"] = jnp.concatenate([gathered["mlp_norm_g_0"], gathered["mlp_norm_g_1"]], axis=1)
    mine = []
    for n in SMALL:
        g = gathered[n]
        if n in SMALL_SHARDED:
            width = W[n].shape[-1]
            g = lax.dynamic_slice_in_dim(g, dev * width, width, axis=2)
        mine.append(g)
    res, loss_tile = _small_adamw("adamw_small", mine, [_rows(W[n]) for n in SMALL], [_rows(M[n]) for n in SMALL],
                                  [_rows(V[n]) for n in SMALL], gathered["loss"])
    loss = loss_tile[0, 0]
    for store, values in zip((out_g, out_d, out_m, out_v), res):
        for n, value in zip(SMALL, values):
            store[n] = value.reshape(W[n].shape)

    return (loss, grad_x.reshape(n_seq, seq, d), *[out_g[n] for n in ORDER], *[out_d[n] for n in ORDER],
            *[out_m[n] for n in ORDER], *[out_v[n] for n in ORDER])
```

```python
import math

import jax
import jax.numpy as jnp
from jax import lax
from jax.experimental import pallas as pl
from jax.experimental.pallas import tpu as pltpu

F32 = jnp.float32
BF16 = jnp.bfloat16
MESH = pl.DeviceIdType.MESH

D_MODEL = 1024
A_DIM = 512
B_DIM = 512
IN_EVEN = 2 * A_DIM + 3 * B_DIM
A_TAPS = 31
B_TAPS = 3
CHUNK = 128
C_GROUPS = 8
C_DIM = 1024
D_FF = 4096
RMS_EPS = 1e-6
LN_EPS = 1e-5
N_DEV = 8
N_CHIP = 4

ADAM_LR = 0.001
ADAM_B1 = 0.9
ADAM_B2 = 0.999
ADAM_EPS = 1e-08
ADAM_WD = 0.01
ADAM_STEP = 10

A_HALO = 32
B_HALO = 8
VMEM_LIMIT_BYTES = 56 * 1024 * 1024
TINY = 1.1754944e-38
INV_SQRT2 = 1.0 / math.sqrt(2.0)
INV_SQRT_2PI = 1.0 / math.sqrt(2.0 * math.pi)
HBM_SPEC = pl.BlockSpec(memory_space=pltpu.HBM)


def _tile(n, want):
    t = min(n, want)
    while n % t:
        t //= 2
    return t


def _sigmoid(x):
    return 1.0 / (1.0 + jnp.exp(-x))


def _gelu(x):
    return 0.5 * x * (1.0 + lax.erf(x * INV_SQRT2))


def _gelu_and_grad(x):
    cdf = 0.5 * (1.0 + lax.erf(x * INV_SQRT2))
    return x * cdf, cdf + x * jnp.exp(-0.5 * x * x) * INV_SQRT_2PI


def _colsum(x):
    return jnp.sum(x, axis=0, keepdims=True)


class _Comm:
    def __init__(self, ins, out_shapes, sem_shapes, start, finish, middle=None, into=None, sibling=False, chips=False):
        self.ins, self.out_shapes, self.sem_shapes, self.start, self.finish = ins, out_shapes, sem_shapes, start, finish
        self.middle = middle
        self.sibling, self.chips = sibling, chips
        self.into = list(into) if into is not None else []
        self.out = None


def _piece_rows(rows, piece):
    if piece is None:
        return 0, rows
    i, n = piece
    return i * (rows // n), rows // n


MIDDLE_AT = 0.75
BARRIER_IDS = {(True, True): 0, (True, False): 1, (False, True): 2}
FUSED_ROWS = 512


def _call(body, *, name, grid, in_specs, out_specs, out_shape, args, scratch_shapes=(), parallel=False, comm=None):
    comms = [] if comm is None else (list(comm) if isinstance(comm, (list, tuple)) else [comm])
    if not comms:
        sem = ("parallel" if parallel else "arbitrary",) * len(grid)
        return pl.pallas_call(
            body, name=name, grid=grid, in_specs=list(in_specs), out_specs=list(out_specs), out_shape=list(out_shape),
            scratch_shapes=list(scratch_shapes),
            compiler_params=pltpu.CompilerParams(dimension_semantics=sem, vmem_limit_bytes=VMEM_LIMIT_BYTES),
        )(*args)
    n_in, n_out, n_scr = len(in_specs), len(out_shape), len(scratch_shapes)
    c_ins_all = [a for cm in comms for a in cm.ins]
    c_into_all = [a for cm in comms for a in cm.into]
    c_out_shapes = [s for cm in comms for s in cm.out_shapes]
    c_sem_shapes = [s for cm in comms for s in cm.sem_shapes]
    aliases, in_pos, out_pos = {}, n_in + len(c_ins_all), n_out
    for cm in comms:
        for j in range(len(cm.into)):
            aliases[in_pos + j] = out_pos + j
        in_pos += len(cm.into)
        out_pos += len(cm.out_shapes)
    to_sibling = any(cm.sibling for cm in comms)
    to_chips = any(cm.chips for cm in comms)
    steps = grid
    total = math.prod(steps)
    first_step = (0,) * len(steps)
    last_step = tuple(s - 1 for s in steps)
    middle_step = None
    if 0 < int(MIDDLE_AT * total) < total - 1:
        rest, idx = int(MIDDLE_AT * total), []
        for s in reversed(steps):
            idx.append(rest % s)
            rest //= s
        middle_step = tuple(reversed(idx))

    def carrying(*refs):
        pos = 0
        ins = refs[pos:pos + n_in]; pos += n_in
        c_ins = refs[pos:pos + len(c_ins_all)]; pos += len(c_ins_all) + len(c_into_all)
        outs = refs[pos:pos + n_out]; pos += n_out
        c_outs = refs[pos:pos + len(c_out_shapes)]; pos += len(c_out_shapes)
        scr = refs[pos:pos + n_scr]; pos += n_scr
        c_sems = refs[pos:]
        views, i0, o0, s0 = [], 0, 0, 0
        for cm in comms:
            views.append((c_ins[i0:i0 + len(cm.ins)], c_outs[o0:o0 + len(cm.out_shapes)], c_sems[s0:s0 + len(cm.sem_shapes)]))
            i0, o0, s0 = i0 + len(cm.ins), o0 + len(cm.out_shapes), s0 + len(cm.sem_shapes)

        def at(step):
            hit = pl.program_id(0) == step[0]
            for axis in range(1, len(steps)):
                hit = jnp.logical_and(hit, pl.program_id(axis) == step[axis])
            return hit

        @pl.when(at(first_step))
        def _():
            x, y, c, chips = _place()
            peers = ([(x, y, 1 - c)] if to_sibling else []) + ([(*chip, c) for chip in chips] if to_chips else [])
            barrier = pltpu.get_barrier_semaphore()
            for peer in peers:
                pl.semaphore_signal(barrier, inc=1, device_id=peer, device_id_type=MESH)
            pl.semaphore_wait(barrier, len(peers))
            for cm, view in zip(comms, views):
                cm.start(*view)

        if middle_step is not None:
            @pl.when(at(middle_step))
            def _():
                for cm, view in zip(comms, views):
                    if cm.middle is not None:
                        cm.middle(*view)

        body(*ins, *outs, *scr)

        @pl.when(at(last_step))
        def _():
            for cm, view in zip(comms, views):
                if cm.middle is not None and middle_step is None:
                    cm.middle(*view)
            for cm, view in zip(comms, views):
                cm.finish(*view)

    res = pl.pallas_call(
        carrying, name=name, grid=grid,
        in_specs=[*in_specs, *[HBM_SPEC] * (len(c_ins_all) + len(c_into_all))],
        out_specs=[*out_specs, *[HBM_SPEC] * len(c_out_shapes)],
        out_shape=[*out_shape, *c_out_shapes], scratch_shapes=[*scratch_shapes, *c_sem_shapes],
        input_output_aliases=aliases,
        compiler_params=pltpu.CompilerParams(dimension_semantics=("arbitrary",) * len(grid), vmem_limit_bytes=VMEM_LIMIT_BYTES,
                                             collective_id=BARRIER_IDS[to_sibling, to_chips]),
    )(*args, *c_ins_all, *c_into_all)
    pos = n_out
    for cm in comms:
        cm.out = list(res[pos:pos + len(cm.out_shapes)])
        pos += len(cm.out_shapes)
    return list(res[:n_out])


def _place():
    x, y, c = lax.axis_index("x"), lax.axis_index("y"), lax.axis_index("c")
    return x, y, c, [(1 - x, y), (x, 1 - y), (1 - x, 1 - y)]


def _gather_comm(shards, piece=None, into=None):
    nw = len(shards)
    spans = [_piece_rows(s.shape[0], piece) for s in shards]

    def plan(ins, outs, sems):
        send_sems, recv_sems, local_sems = sems
        x, y, c, chips = _place()
        me, sibling = (x, y, c), (x, y, 1 - c)

        def slot(w, p):
            return outs[w].at[4 * p[0] + 2 * p[1] + p[2], pl.ds(*spans[w])]

        def mine(w):
            return ins[w].at[pl.ds(*spans[w])]

        def copy(w, k, block, to, src=None):
            return pltpu.make_async_remote_copy(
                src_ref=slot(w, block) if src is None else src, dst_ref=slot(w, block),
                send_sem=send_sems.at[w, k], recv_sem=recv_sems.at[w, k], device_id=to, device_id_type=MESH)

        local = [pltpu.make_async_copy(mine(w), slot(w, me), local_sems.at[w]) for w in range(nw)]
        first = [[copy(w, 0, me, sibling, src=mine(w))] + [copy(w, 1 + j, me, (*chip, c), src=mine(w)) for j, chip in enumerate(chips)]
                 for w in range(nw)]
        landed = [[copy(w, 1 + j, (*chip, c), me) for j, chip in enumerate(chips)] for w in range(nw)]
        passed = [[copy(w, 4 + j, (*chip, c), sibling) for j, chip in enumerate(chips)] for w in range(nw)]
        from_sibling = [[copy(w, 0, sibling, me)] + [copy(w, 4 + j, (*chip, 1 - c), me) for j, chip in enumerate(chips)]
                        for w in range(nw)]
        return local, first, landed, passed, from_sibling

    def start(ins, outs, sems):
        local, first, _, _, _ = plan(ins, outs, sems)
        for cp in local:
            cp.start()
        for row in first:
            for cp in row:
                cp.start()

    def middle(ins, outs, sems):
        _, _, landed, passed, _ = plan(ins, outs, sems)
        for w in range(nw):
            for j in range(3):
                landed[w][j].wait_recv()
                passed[w][j].start()

    def finish(ins, outs, sems):
        local, first, landed, passed, from_sibling = plan(ins, outs, sems)
        for w in range(nw):
            for cp in from_sibling[w]:
                cp.wait_recv()
        for w in range(nw):
            for cp in first[w] + passed[w]:
                cp.wait_send()
        for cp in local:
            cp.wait()

    return _Comm(list(shards), [jax.ShapeDtypeStruct((N_DEV, *s.shape), s.dtype) for s in shards],
                 [pltpu.SemaphoreType.DMA((nw, 7)), pltpu.SemaphoreType.DMA((nw, 7)), pltpu.SemaphoreType.DMA((nw,))],
                 start, finish, middle, into=into, sibling=True, chips=True)


def _pair_comm(parts):
    nw = len(parts)

    def plan(ins, outs, sems):
        send_sems, recv_sems = sems
        x, y, c, _ = _place()
        return [pltpu.make_async_remote_copy(
            src_ref=ins[w].at[2 * q + 1 - c], dst_ref=outs[w].at[q], send_sem=send_sems.at[w, q], recv_sem=recv_sems.at[w, q],
            device_id=(x, y, 1 - c), device_id_type=MESH) for w in range(nw) for q in range(N_CHIP)]

    def start(ins, outs, sems):
        for cp in plan(ins, outs, sems):
            cp.start()

    def finish(ins, outs, sems):
        for cp in plan(ins, outs, sems):
            cp.wait()

    return _Comm(list(parts), [jax.ShapeDtypeStruct((N_CHIP, *p.shape[1:]), p.dtype) for p in parts],
                 [pltpu.SemaphoreType.DMA((nw, N_CHIP)), pltpu.SemaphoreType.DMA((nw, N_CHIP))], start, finish, sibling=True)


def _chip_comm(sums, piece=None, into=None):
    nw = len(sums)
    spans = [_piece_rows(s.shape[1], piece) for s in sums]

    def plan(ins, outs, sems):
        send_sems, recv_sems, local_sems = sems
        x, y, c, chips = _place()
        my_chip = 2 * x + y
        local = [pltpu.make_async_copy(ins[w].at[my_chip, pl.ds(*spans[w])], outs[w].at[my_chip, pl.ds(*spans[w])],
                                       local_sems.at[w]) for w in range(nw)]
        remote = [pltpu.make_async_remote_copy(
            src_ref=ins[w].at[2 * chip[0] + chip[1], pl.ds(*spans[w])], dst_ref=outs[w].at[my_chip, pl.ds(*spans[w])],
            send_sem=send_sems.at[w, j], recv_sem=recv_sems.at[w, j], device_id=(*chip, c), device_id_type=MESH)
            for w in range(nw) for j, chip in enumerate(chips)]
        return local, remote

    def start(ins, outs, sems):
        local, remote = plan(ins, outs, sems)
        for cp in local + remote:
            cp.start()

    def finish(ins, outs, sems):
        local, remote = plan(ins, outs, sems)
        for cp in remote + local:
            cp.wait()

    return _Comm(list(sums), [jax.ShapeDtypeStruct(s.shape, s.dtype) for s in sums],
                 [pltpu.SemaphoreType.DMA((nw, 3)), pltpu.SemaphoreType.DMA((nw, 3)), pltpu.SemaphoreType.DMA((nw,))],
                 start, finish, into=into, chips=True)


def _matmul(name, a, b, *, kind, m, n, k, a_spec, b_spec, tm, tn, tk, out_shape, out_specs, epilogue,
            extras=(), extra_specs=(), comm=None):
    dims = {"nn": (((1,), (0,)), ((), ())), "nt": (((1,), (1,)), ((), ())), "tn": (((0,), (0,)), ((), ()))}[kind]
    nk = k // tk
    n_extra = len(extras)
    n_out = len(out_shape)

    def body(a_ref, b_ref, *rest):
        extra_refs = rest[:n_extra]
        out_refs = rest[n_extra:n_extra + n_out]
        part = lax.dot_general(a_ref[...], b_ref[...], dims, preferred_element_type=F32)
        if nk == 1:
            epilogue(part, extra_refs, out_refs)
            return
        acc_ref = rest[n_extra + n_out]
        step = pl.program_id(2)

        @pl.when(step == 0)
        def _():
            acc_ref[...] = part

        @pl.when(jnp.logical_and(step > 0, step < nk - 1))
        def _():
            acc_ref[...] += part

        @pl.when(step == nk - 1)
        def _():
            epilogue(acc_ref[...] + part, extra_refs, out_refs)

    return _call(
        body, name=name, grid=(m // tm, n // tn, nk), in_specs=[a_spec, b_spec, *extra_specs], out_specs=out_specs,
        out_shape=out_shape, scratch_shapes=[pltpu.VMEM((tm, tn), F32)] if nk > 1 else [], args=(a, b, *extras), comm=comm)


def _mm_operands(m, n, tm, tn, out_dtypes, extras, rows, sums):
    tile = pl.BlockSpec((tm, tn), lambda i, j, kk: (i, j))
    row = pl.BlockSpec((1, tn), lambda i, j, kk: (0, j))
    one = pl.BlockSpec((1, 1), lambda i, j, kk: (0, 0))
    out_shape = [jax.ShapeDtypeStruct((m, n), dt) for dt in out_dtypes]
    out_shape += [jax.ShapeDtypeStruct((1, n if wide else 1), F32) for wide in sums]
    out_specs = [tile] * len(out_dtypes) + [row if wide else one for wide in sums]
    return (*extras, *rows), [tile] * len(extras) + [row] * len(rows), out_shape, out_specs


def _row_tile(m, k, tm_want):
    return _tile(m, tm_want or (1024 if k <= 1024 else 512))


def _mm_nn(name, a, b, n, epilogue, out_dtypes, *, extras=(), rows=(), sums=(), tm_want=None, comm=None):
    m, k = a.shape
    blocked = b.ndim == 3
    tm = _row_tile(m, k, tm_want)
    tn = b.shape[-1] if blocked else _tile(n, 1024)
    tk = k
    extras, extra_specs, out_shape, out_specs = _mm_operands(m, n, tm, tn, out_dtypes, extras, rows, sums)
    if blocked:
        b_spec = pl.BlockSpec((None, tk, tn), lambda i, j, kk: (j, kk, 0))
    else:
        b_spec = pl.BlockSpec((tk, tn), lambda i, j, kk: (kk, j))
    return _matmul(
        name, a, b, kind="nn", m=m, n=n, k=k, tm=tm, tn=tn, tk=tk,
        a_spec=pl.BlockSpec((tm, tk), lambda i, j, kk: (i, kk)), b_spec=b_spec, out_shape=out_shape, out_specs=out_specs,
        epilogue=epilogue, extras=extras, extra_specs=extra_specs, comm=comm)


def _mm_nt(name, a, b, n, epilogue, out_dtypes, *, extras=(), rows=(), sums=(), tm_want=None, comm=None):
    m, k = a.shape
    blocked = b.ndim == 3
    tm = _row_tile(m, k, tm_want)
    tn = _tile(n, 1024)
    tk = b.shape[-1] if blocked else k
    extras, extra_specs, out_shape, out_specs = _mm_operands(m, n, tm, tn, out_dtypes, extras, rows, sums)
    if blocked:
        b_spec = pl.BlockSpec((None, tn, tk), lambda i, j, kk: (kk, j, 0))
    else:
        b_spec = pl.BlockSpec((tn, tk), lambda i, j, kk: (j, kk))
    return _matmul(
        name, a, b, kind="nt", m=m, n=n, k=k, tm=tm, tn=tn, tk=tk,
        a_spec=pl.BlockSpec((tm, tk), lambda i, j, kk: (i, kk)), b_spec=b_spec, out_shape=out_shape, out_specs=out_specs,
        epilogue=epilogue, extras=extras, extra_specs=extra_specs, comm=comm)


def _mm_out_proj_grads(name, grad16, w, act):
    t, n = grad16.shape
    k = w.shape[0]
    tm = _tile(t, 1024)

    def body(g_ref, w_ref, a_ref, din_ref, dw_ref):
        g = g_ref[...]
        din_ref[...] = lax.dot_general(g, w_ref[...], (((1,), (1,)), ((), ())), preferred_element_type=F32)
        part = lax.dot_general(a_ref[...], g, (((0,), (0,)), ((), ())), preferred_element_type=F32)

        @pl.when(pl.program_id(0) == 0)
        def _():
            dw_ref[...] = part

        @pl.when(pl.program_id(0) > 0)
        def _():
            dw_ref[...] += part

    return _call(
        body, name=name, grid=(t // tm,),
        in_specs=[pl.BlockSpec((tm, n), lambda i: (i, 0)), pl.BlockSpec((k, n), lambda i: (0, 0)),
                  pl.BlockSpec((tm, k), lambda i: (i, 0))],
        out_specs=[pl.BlockSpec((tm, k), lambda i: (i, 0)), pl.BlockSpec((k, n), lambda i: (0, 0))],
        out_shape=[jax.ShapeDtypeStruct((t, k), F32), jax.ShapeDtypeStruct((k, n), F32)], args=(grad16, w, act))


def _mm_tn(name, a, b, *, col_blocks=0, comm=None):
    t, k = a.shape
    n = b.shape[1]
    tm = _row_tile(k, t, None)
    tn = n // col_blocks if col_blocks else _tile(n, 1024)
    tk = t
    if col_blocks:
        out_shape = [jax.ShapeDtypeStruct((col_blocks, k, tn), F32)]
        out_specs = [pl.BlockSpec((None, tm, tn), lambda i, j, kk: (j, i, 0))]
    else:
        out_shape = [jax.ShapeDtypeStruct((k, n), F32)]
        out_specs = [pl.BlockSpec((tm, tn), lambda i, j, kk: (i, j))]

    def epilogue(acc, extra_refs, out_refs):
        out_refs[0][...] = acc

    return _matmul(
        name, a, b, kind="tn", m=k, n=n, k=t, tm=tm, tn=tn, tk=tk,
        a_spec=pl.BlockSpec((tk, tm), lambda i, j, kk: (kk, i)), b_spec=pl.BlockSpec((tk, tn), lambda i, j, kk: (kk, j)),
        out_shape=out_shape, out_specs=out_specs, epilogue=epilogue, comm=comm)[0]


def _ep_store(acc, extra_refs, out_refs):
    out_refs[0][...] = acc.astype(out_refs[0].dtype)


def _ep_bias(acc, extra_refs, out_refs):
    out_refs[0][...] = acc + extra_refs[0][...]


def _rms(x):
    r = lax.rsqrt(jnp.mean(x * x, axis=-1, keepdims=True) + RMS_EPS)
    return r, x * r


def _rms_grad(dn, r, xr, g):
    dy = dn * g
    return r * (dy - xr * jnp.mean(dy * xr, axis=-1, keepdims=True))


def _ep_residual_norm(acc, extra_refs, out_refs):
    h = extra_refs[0][...] + acc
    out_refs[0][...] = h
    _, xr = _rms(h)
    out_refs[1][...] = (xr * extra_refs[1][...]).astype(BF16)


def _ep_final_loss(acc, extra_refs, out_refs):
    @pl.when(pl.program_id(0) == 0)
    def _():
        out_refs[2][...] = jnp.zeros_like(out_refs[2])
        out_refs[3][...] = jnp.zeros_like(out_refs[3])

    h = extra_refs[0][...] + acc
    g = extra_refs[2][...]
    r, xr = _rms(h)
    err = xr * g - extra_refs[1][...]
    out_refs[3][...] += 0.5 * jnp.sum(jnp.mean(err * err, axis=-1, keepdims=True), axis=0, keepdims=True)
    dout = err * (1.0 / h.shape[-1])
    out_refs[2][...] += _colsum(dout * xr)
    out = _rms_grad(dout, r, xr, g)
    out_refs[0][...] = out
    out_refs[1][...] = out.astype(BF16)


def _ep_relu_sq(acc, extra_refs, out_refs):
    r = jnp.maximum(acc, 0.0)
    out_refs[0][...] = (r * r).astype(BF16)


def _ep_relu_sq_grad(acc, extra_refs, out_refs):
    q = extra_refs[0][...].astype(F32)
    out_refs[0][...] = (acc * (2.0 * q * lax.rsqrt(jnp.maximum(q, TINY)))).astype(BF16)


def _rms_fwd(name, h, g, comm=None, casts=()):
    t, d = h.shape
    tt = _tile(t, 512)
    steps = t // tt
    n_cast = len(casts)

    def body(h_ref, g_ref, *rest):
        n_ref = rest[n_cast]
        x = h_ref[...]
        r = lax.rsqrt(jnp.mean(x * x, axis=-1, keepdims=True) + RMS_EPS)
        n_ref[...] = (x * r * g_ref[...]).astype(BF16)
        for src, dst in zip(rest[:n_cast], rest[n_cast + 1:]):
            dst[...] = src[...].astype(BF16)

    cast_in = [pl.BlockSpec((None, w.shape[1] // steps, w.shape[2]), lambda i, l=l: (l, i, 0)) for w, l in casts]
    cast_out = [pl.BlockSpec((w.shape[1] // steps, w.shape[2]), lambda i: (i, 0)) for w, _ in casts]
    res = _call(
        body, name=name, grid=(steps,),
        in_specs=[pl.BlockSpec((tt, d), lambda i: (i, 0)), pl.BlockSpec((1, d), lambda i: (0, 0)), *cast_in],
        out_specs=[pl.BlockSpec((tt, d), lambda i: (i, 0)), *cast_out],
        out_shape=[jax.ShapeDtypeStruct((t, d), BF16)] + [jax.ShapeDtypeStruct(w.shape[1:], BF16) for w, _ in casts],
        args=(h, g, *[w for w, _ in casts]), parallel=True, comm=comm)
    return res[0], res[1:]


def _rms_bwd(name, dn, h, g, grad_in, comm=None):
    t, d = h.shape
    tt = _tile(t, 512)

    def body(dn_ref, h_ref, g_ref, gin_ref, gout_ref, gout16_ref, dg_ref):
        @pl.when(pl.program_id(0) == 0)
        def _():
            dg_ref[...] = jnp.zeros_like(dg_ref)

        dnv = dn_ref[...].astype(F32)
        r, xr = _rms(h_ref[...])
        dg_ref[...] += _colsum(dnv * xr)
        out = gin_ref[...] + _rms_grad(dnv, r, xr, g_ref[...])
        gout_ref[...] = out
        gout16_ref[...] = out.astype(BF16)

    row = pl.BlockSpec((tt, d), lambda i: (i, 0))
    vec = pl.BlockSpec((1, d), lambda i: (0, 0))
    return _call(
        body, name=name, grid=(t // tt,), in_specs=[row, row, vec, row], out_specs=[row, row, vec],
        out_shape=[jax.ShapeDtypeStruct((t, d), F32), jax.ShapeDtypeStruct((t, d), BF16), jax.ShapeDtypeStruct((1, d), F32)],
        args=(dn, h, g, grad_in), comm=comm)


def _mixer_windows(z_ref, zh_ref, first, a1_s, cb_s, tt):
    sig = _sigmoid(z_ref[:, A_DIM:2 * A_DIM])
    if a1_s is not None:
        a1_s[A_HALO:A_HALO + tt, :] = z_ref[:, 0:A_DIM] * sig
        a1_h = zh_ref[:, 0:A_DIM] * _sigmoid(zh_ref[:, A_DIM:2 * A_DIM])
        a1_s[0:A_HALO, :] = jnp.where(first, 0.0, a1_h)
    cb_s[B_HALO:B_HALO + tt, :] = z_ref[:, 3 * A_DIM:4 * A_DIM] * z_ref[:, 4 * A_DIM:5 * A_DIM]
    cb_h = zh_ref[A_HALO - B_HALO:A_HALO, 3 * A_DIM:4 * A_DIM] * zh_ref[A_HALO - B_HALO:A_HALO, 4 * A_DIM:5 * A_DIM]
    cb_s[0:B_HALO, :] = jnp.where(first, 0.0, cb_h)
    return sig


def _causal_conv(win_s, w_ref, taps, halo, tt):
    base = halo - (taps - 1)
    acc = w_ref[0:1, :] * win_s[pl.ds(base, tt), :]
    for k in range(1, taps):
        acc = acc + w_ref[k:k + 1, :] * win_s[pl.ds(base + k, tt), :]
    return acc


SUBLANES = 8
LANE_BLOCK = 128
ROW_BLOCK = 128
SHIFT_ROWS = A_HALO - SUBLANES


def _shifted_copies(win_s, sh_s, tt):
    for b in range(1, SUBLANES):
        sh_s[b - 1] = win_s[pl.ds(b, tt + SHIFT_ROWS), :]


def _window_rows(win_s, sh_s, offset, rows, cols):
    b = offset % SUBLANES
    if b == 0:
        return win_s[pl.ds(offset, rows), cols]
    return sh_s[b - 1, pl.ds(offset - b, rows), cols]


def _blocks(tt):
    rb = min(tt, ROW_BLOCK)
    return rb, [(r, slice(lb * LANE_BLOCK, (lb + 1) * LANE_BLOCK))
                for lb in range(A_DIM // LANE_BLOCK) for r in range(0, tt, rb)]


def _conv_taps(win_s, sh_s, w_ref, offsets, out_s, tt, bias_ref=None):
    rb, blocks = _blocks(tt)
    for r, cols in blocks:
        acc = w_ref[0:1, cols] * _window_rows(win_s, sh_s, r + offsets[0], rb, cols)
        for k in range(1, len(offsets)):
            acc = acc + w_ref[k:k + 1, cols] * _window_rows(win_s, sh_s, r + offsets[k], rb, cols)
        out_s[r:r + rb, cols] = acc if bias_ref is None else acc + bias_ref[:, cols]


A_CAUSAL = [A_HALO - (A_TAPS - 1) + k for k in range(A_TAPS)]
A_ANTICAUSAL = [A_TAPS - 1 - k for k in range(A_TAPS)]


def _layer_norm_stats(x):
    mu = jnp.mean(x, axis=-1, keepdims=True)
    xc = x - mu
    rstd = lax.rsqrt(jnp.mean(xc * xc, axis=-1, keepdims=True) + LN_EPS)
    return xc * rstd, rstd


def _mixer_specs(seq, tt):
    tiles_per_seq = seq // tt
    halo_blocks = tt // A_HALO
    z_spec = pl.BlockSpec((tt, IN_EVEN), lambda i: (i, 0))
    zh_spec = pl.BlockSpec((A_HALO, IN_EVEN), lambda i: (jnp.maximum(i * halo_blocks - 1, 0), 0))
    return tiles_per_seq, z_spec, zh_spec


def _vec_spec(rows, cols):
    return pl.BlockSpec((rows, cols), lambda i: (0, 0))


def _mixer_fwd(z, seq, caw, cab, lag, lab, cbw, comm=None):
    t = z.shape[0]
    tt = _tile(seq, 256)
    tiles_per_seq, z_spec, zh_spec = _mixer_specs(seq, tt)

    def body(z_ref, zh_ref, caw_ref, cab_ref, lag_ref, lab_ref, cbw_ref, mix_ref, a2_ref, a1_s, cb_s, sh_s):
        first = (pl.program_id(0) % tiles_per_seq) == 0
        _mixer_windows(z_ref, zh_ref, first, a1_s, cb_s, tt)
        _shifted_copies(a1_s, sh_s, tt)
        _conv_taps(a1_s, sh_s, caw_ref, A_CAUSAL, a2_ref, tt, bias_ref=cab_ref)
        xhat, _ = _layer_norm_stats(a2_ref[...])
        a3 = xhat * lag_ref[...] + lab_ref[...]
        mix_ref[:, 0:A_DIM] = (a3 * _sigmoid(a3)).astype(BF16)
        cv = _causal_conv(cb_s, cbw_ref, B_TAPS, B_HALO, tt)
        mix_ref[:, A_DIM:A_DIM + B_DIM] = (z_ref[:, 2 * A_DIM:3 * A_DIM] * cv).astype(BF16)

    return _call(
        body, name="mixer_fwd", grid=(t // tt,),
        in_specs=[z_spec, zh_spec, _vec_spec(A_TAPS, A_DIM), _vec_spec(1, A_DIM), _vec_spec(1, A_DIM), _vec_spec(1, A_DIM),
                  _vec_spec(B_TAPS, B_DIM)],
        out_specs=[pl.BlockSpec((tt, A_DIM + B_DIM), lambda i: (i, 0)), pl.BlockSpec((tt, A_DIM), lambda i: (i, 0))],
        out_shape=[jax.ShapeDtypeStruct((t, A_DIM + B_DIM), BF16), jax.ShapeDtypeStruct((t, A_DIM), F32)],
        scratch_shapes=[pltpu.VMEM((A_HALO + tt, A_DIM), F32), pltpu.VMEM((B_HALO + tt, B_DIM), F32),
                        pltpu.VMEM((SUBLANES - 1, tt + SHIFT_ROWS, A_DIM), F32)],
        args=(z, z, caw, cab, lag, lab, cbw), parallel=True, comm=comm)


def _mixer_bwd_local(z, a2, dmix, seq, lag, lab, comm=None):
    t = z.shape[0]
    tt = _tile(seq, 256)
    tiles_per_seq, z_spec, zh_spec = _mixer_specs(seq, tt)

    def body(z_ref, zh_ref, a2_ref, dmix_ref, lag_ref, lab_ref,
             da2_ref, dcv_ref, dcaw_ref, dcab_ref, dlag_ref, dlab_ref, dcbw_ref, a1_s, cb_s, sh_s):
        @pl.when(pl.program_id(0) == 0)
        def _():
            for ref in (dcaw_ref, dcab_ref, dlag_ref, dlab_ref, dcbw_ref):
                ref[...] = jnp.zeros_like(ref)

        first = (pl.program_id(0) % tiles_per_seq) == 0
        _mixer_windows(z_ref, zh_ref, first, a1_s, cb_s, tt)
        _shifted_copies(a1_s, sh_s, tt)
        xhat, rstd = _layer_norm_stats(a2_ref[...])
        a3 = xhat * lag_ref[...] + lab_ref[...]
        s3 = _sigmoid(a3)
        da3 = dmix_ref[:, 0:A_DIM] * (s3 * (1.0 + a3 * (1.0 - s3)))
        dlag_ref[...] += _colsum(da3 * xhat)
        dlab_ref[...] += _colsum(da3)
        dxh = da3 * lag_ref[...]
        da2 = rstd * (dxh - jnp.mean(dxh, axis=-1, keepdims=True) - xhat * jnp.mean(dxh * xhat, axis=-1, keepdims=True))
        da2_ref[...] = da2
        dcab_ref[...] += _colsum(da2)
        rb, blocks = _blocks(tt)
        for r, cols in blocks:
            da2_b = da2_ref[r:r + rb, cols]
            for k in range(A_TAPS):
                dcaw_ref[k:k + 1, cols] += _colsum(da2_b * _window_rows(a1_s, sh_s, r + A_CAUSAL[k], rb, cols))
        dcv =dmix_ref[:, A_DIM:A_DIM + B_DIM] * z_ref[:, 2 * A_DIM:3 * A_DIM]
        dcv_ref[...] = dcv
        for k in range(B_TAPS):
            dcbw_ref[k:k + 1, :] += _colsum(dcv * cb_s[pl.ds(B_HALO - (B_TAPS - 1) + k, tt), :])

    half = pl.BlockSpec((tt, A_DIM), lambda i: (i, 0))
    return _call(
        body, name="mixer_bwd_local", grid=(t // tt,),
        in_specs=[z_spec, zh_spec, half, pl.BlockSpec((tt, A_DIM + B_DIM), lambda i: (i, 0)),
                  _vec_spec(1, A_DIM), _vec_spec(1, A_DIM)],
        out_specs=[half, half, _vec_spec(A_TAPS, A_DIM), _vec_spec(1, A_DIM), _vec_spec(1, A_DIM), _vec_spec(1, A_DIM),
                   _vec_spec(B_TAPS, B_DIM)],
        out_shape=[jax.ShapeDtypeStruct((t, A_DIM), F32), jax.ShapeDtypeStruct((t, B_DIM), F32),
                   jax.ShapeDtypeStruct((A_TAPS, A_DIM), F32), jax.ShapeDtypeStruct((1, A_DIM), F32),
                   jax.ShapeDtypeStruct((1, A_DIM), F32), jax.ShapeDtypeStruct((1, A_DIM), F32),
                   jax.ShapeDtypeStruct((B_TAPS, B_DIM), F32)],
        scratch_shapes=[pltpu.VMEM((A_HALO + tt, A_DIM), F32), pltpu.VMEM((B_HALO + tt, B_DIM), F32),
                        pltpu.VMEM((SUBLANES - 1, tt + SHIFT_ROWS, A_DIM), F32)],
        args=(z, z, a2, dmix, lag, lab), comm=comm)


def _mixer_bwd_input(z, dmix, da2, dcv, seq, caw, cbw, comm=None):
    t = z.shape[0]
    tt = _tile(seq, 256)
    tiles_per_seq, z_spec, zh_spec = _mixer_specs(seq, tt)
    a_blocks = tt // A_HALO
    b_blocks = tt // B_HALO
    last_a = t // A_HALO - 1
    last_b = t // B_HALO - 1

    def body(z_ref, zh_ref, dmix_ref, da2_ref, da2n_ref, dcv_ref, dcvn_ref, caw_ref, cbw_ref, dz_ref, cb_s, da2_s, dcv_s,
             sh_s, da1_s):
        pos = pl.program_id(0) % tiles_per_seq
        first = pos == 0
        last = pos == tiles_per_seq - 1
        sig = _mixer_windows(z_ref, zh_ref, first, None, cb_s, tt)
        da2_s[0:tt, :] = da2_ref[...]
        da2_s[tt:tt + A_HALO, :] = jnp.where(last, 0.0, da2n_ref[...])
        dcv_s[0:tt, :] = dcv_ref[...]
        dcv_s[tt:tt + B_HALO, :] = jnp.where(last, 0.0, dcvn_ref[...])
        _shifted_copies(da2_s, sh_s, tt)
        _conv_taps(da2_s, sh_s, caw_ref, A_ANTICAUSAL, da1_s, tt)
        da1 = da1_s[...]
        dz_ref[:, 0:A_DIM] = (da1 * sig).astype(BF16)
        dz_ref[:, A_DIM:2 * A_DIM] = (da1 * z_ref[:, 0:A_DIM] * sig * (1.0 - sig)).astype(BF16)
        cv = _causal_conv(cb_s, cbw_ref, B_TAPS, B_HALO, tt)
        dz_ref[:, 2 * A_DIM:3 * A_DIM] = (dmix_ref[:, A_DIM:A_DIM + B_DIM] * cv).astype(BF16)
        dcb = cbw_ref[0:1, :] * dcv_s[pl.ds(B_TAPS - 1, tt), :]
        for k in range(1, B_TAPS):
            dcb = dcb + cbw_ref[k:k + 1, :] * dcv_s[pl.ds(B_TAPS - 1 - k, tt), :]
        dz_ref[:, 3 * A_DIM:4 * A_DIM] = (dcb * z_ref[:, 4 * A_DIM:5 * A_DIM]).astype(BF16)
        dz_ref[:, 4 * A_DIM:5 * A_DIM] = (dcb * z_ref[:, 3 * A_DIM:4 * A_DIM]).astype(BF16)

    half = pl.BlockSpec((tt, A_DIM), lambda i: (i, 0))
    return _call(
        body, name="mixer_bwd_input", grid=(t // tt,),
        in_specs=[z_spec, zh_spec, pl.BlockSpec((tt, A_DIM + B_DIM), lambda i: (i, 0)),
                  half, pl.BlockSpec((A_HALO, A_DIM), lambda i: (jnp.minimum((i + 1) * a_blocks, last_a), 0)),
                  half, pl.BlockSpec((B_HALO, B_DIM), lambda i: (jnp.minimum((i + 1) * b_blocks, last_b), 0)),
                  _vec_spec(A_TAPS, A_DIM), _vec_spec(B_TAPS, B_DIM)],
        out_specs=[pl.BlockSpec((tt, IN_EVEN), lambda i: (i, 0))],
        out_shape=[jax.ShapeDtypeStruct((t, IN_EVEN), BF16)],
        scratch_shapes=[pltpu.VMEM((B_HALO + tt, B_DIM), F32),
                        pltpu.VMEM((tt + A_HALO, A_DIM), F32), pltpu.VMEM((tt + B_HALO, B_DIM), F32),
                        pltpu.VMEM((SUBLANES - 1, tt + SHIFT_ROWS, A_DIM), F32), pltpu.VMEM((tt, A_DIM), F32)],
        args=(z, z, dmix, da2, da2, dcv, dcv, caw, cbw), parallel=True, comm=comm)[0]


def _tril_ws(ws_ref, g):
    rows = lax.broadcasted_iota(jnp.int32, (CHUNK, CHUNK), 0)
    cols = lax.broadcasted_iota(jnp.int32, (CHUNK, CHUNK), 1)
    return jnp.where(rows >= cols, ws_ref[g], 0.0).astype(BF16), rows >= cols


def _sgu_fwd(pre, lvg, lvb, ws, bs_b, comm=None):
    t = pre.shape[0]
    tt = _tile(t, 256)

    def body(pre_ref, lvg_ref, lvb_ref, ws_ref, bsb_ref, y_ref):
        vhat, _ = _layer_norm_stats(_gelu(pre_ref[:, C_DIM:2 * C_DIM]))
        vl = (vhat * lvg_ref[...] + lvb_ref[...]).astype(BF16)
        for g in range(C_GROUPS):
            w, _ = _tril_ws(ws_ref, g)
            cols = slice(g * CHUNK, (g + 1) * CHUNK)
            for ci in range(tt // CHUNK):
                rows = slice(ci * CHUNK, (ci + 1) * CHUNK)
                sv = jnp.dot(w, vl[rows, cols], preferred_element_type=F32) + bsb_ref[g]
                y_ref[rows, cols] = (_gelu(pre_ref[rows, cols]) * sv).astype(BF16)

    group = pl.BlockSpec((C_GROUPS, CHUNK, CHUNK), lambda i: (0, 0, 0))
    return _call(
        body, name="sgu_fwd", grid=(t // tt,),
        in_specs=[pl.BlockSpec((tt, 2 * C_DIM), lambda i: (i, 0)), _vec_spec(1, C_DIM), _vec_spec(1, C_DIM), group, group],
        out_specs=[pl.BlockSpec((tt, C_DIM), lambda i: (i, 0))], out_shape=[jax.ShapeDtypeStruct((t, C_DIM), BF16)],
        args=(pre, lvg, lvb, ws, bs_b), parallel=True, comm=comm)[0]


def _sgu_bwd(pre, dy, lvg, lvb, ws, bs_b, comm=None):
    t = pre.shape[0]
    tt = _tile(t, 256)

    def body(pre_ref, dy_ref, lvg_ref, lvb_ref, ws_ref, bsb_ref, dpre_ref, dws_ref, dbsb_ref, dlvg_ref, dlvb_ref, dbin_ref,
             dvl_s):
        @pl.when(pl.program_id(0) == 0)
        def _():
            for ref in (dws_ref, dbsb_ref, dlvg_ref, dlvb_ref, dbin_ref):
                ref[...] = jnp.zeros_like(ref)

        v, v_grad = _gelu_and_grad(pre_ref[:, C_DIM:2 * C_DIM])
        vhat, rstd = _layer_norm_stats(v)
        vl = (vhat * lvg_ref[...] + lvb_ref[...]).astype(BF16)
        for g in range(C_GROUPS):
            w, keep = _tril_ws(ws_ref, g)
            cols = slice(g * CHUNK, (g + 1) * CHUNK)
            dws = jnp.zeros((CHUNK, CHUNK), F32)
            dbs = jnp.zeros((CHUNK, 1), F32)
            for ci in range(tt // CHUNK):
                rows = slice(ci * CHUNK, (ci + 1) * CHUNK)
                vl_g = vl[rows, cols]
                sv = jnp.dot(w, vl_g, preferred_element_type=F32) + bsb_ref[g]
                u, u_grad = _gelu_and_grad(pre_ref[rows, cols])
                dyv = dy_ref[rows, cols]
                du = dyv * sv * u_grad
                dpre_ref[rows, cols] = du.astype(BF16)
                dbin_ref[:, cols] += _colsum(du)
                dsv = dyv * u
                dbs = dbs + jnp.sum(dsv, axis=1, keepdims=True)
                dsv16 = dsv.astype(BF16)
                dws = dws + lax.dot_general(dsv16, vl_g, (((1,), (1,)), ((), ())), preferred_element_type=F32)
                dvl_s[rows, cols] = lax.dot_general(w, dsv16, (((0,), (0,)), ((), ())), preferred_element_type=F32)
            dws_ref[g] += jnp.where(keep, dws, 0.0)
            dbsb_ref[g] += dbs
        dvl = dvl_s[...]
        dlvg_ref[...] += _colsum(dvl * vhat)
        dlvb_ref[...] += _colsum(dvl)
        dxh = dvl * lvg_ref[...]
        dv = rstd * (dxh - jnp.mean(dxh, axis=-1, keepdims=True) - vhat * jnp.mean(dxh * vhat, axis=-1, keepdims=True))
        dpv = dv * v_grad
        dpre_ref[:, C_DIM:2 * C_DIM] = dpv.astype(BF16)
        dbin_ref[:, C_DIM:2 * C_DIM] += _colsum(dpv)

    group = pl.BlockSpec((C_GROUPS, CHUNK, CHUNK), lambda i: (0, 0, 0))
    return _call(
        body, name="sgu_bwd", grid=(t // tt,),
        in_specs=[pl.BlockSpec((tt, 2 * C_DIM), lambda i: (i, 0)), pl.BlockSpec((tt, C_DIM), lambda i: (i, 0)),
                  _vec_spec(1, C_DIM), _vec_spec(1, C_DIM), group, group],
        out_specs=[pl.BlockSpec((tt, 2 * C_DIM), lambda i: (i, 0)), group, group,
                   _vec_spec(1, C_DIM), _vec_spec(1, C_DIM), _vec_spec(1, 2 * C_DIM)],
        out_shape=[jax.ShapeDtypeStruct((t, 2 * C_DIM), BF16), jax.ShapeDtypeStruct((C_GROUPS, CHUNK, CHUNK), F32),
                   jax.ShapeDtypeStruct((C_GROUPS, CHUNK, CHUNK), F32), jax.ShapeDtypeStruct((1, C_DIM), F32),
                   jax.ShapeDtypeStruct((1, C_DIM), F32), jax.ShapeDtypeStruct((1, 2 * C_DIM), F32)],
        scratch_shapes=[pltpu.VMEM((tt, C_DIM), F32)],
        args=(pre, dy, lvg, lvb, ws, bs_b), comm=comm)


def _pair_sum(name, part, got, core):
    _, k, n = part.shape
    tk = _tile(k, 1024)

    def body(core_ref, p_ref, s_ref, o_ref):
        o_ref[...] = (p_ref[...] + s_ref[...]).astype(BF16)

    return pl.pallas_call(
        body, name=name,
        grid_spec=pltpu.PrefetchScalarGridSpec(
            num_scalar_prefetch=1, grid=(N_CHIP, k // tk),
            in_specs=[pl.BlockSpec((None, tk, n), lambda q, i, core_ref: (2 * q + core_ref[0], i, 0)),
                      pl.BlockSpec((None, tk, n), lambda q, i, core_ref: (q, i, 0))],
            out_specs=pl.BlockSpec((None, tk, n), lambda q, i, core_ref: (q, i, 0))),
        out_shape=jax.ShapeDtypeStruct((N_CHIP, k, n), BF16),
        compiler_params=pltpu.CompilerParams(dimension_semantics=("parallel", "parallel"), vmem_limit_bytes=VMEM_LIMIT_BYTES),
    )(core, part, got)


def _adamw_math(w, g, m, v):
    m = ADAM_B1 * m + (1.0 - ADAM_B1) * g
    v = ADAM_B2 * v + (1.0 - ADAM_B2) * (g * g)
    m_hat = m / (1.0 - ADAM_B1 ** ADAM_STEP)
    v_hat = v / (1.0 - ADAM_B2 ** ADAM_STEP)
    delta = -ADAM_LR * (m_hat / (jnp.sqrt(v_hat) + ADAM_EPS) + ADAM_WD * w)
    return delta, m, v


def _sum_adamw(name, parts, w, m, v, comm=None):
    layers = len(parts)
    n_parts, k, n = parts[0].shape
    tk = _tile(k, 256)

    def body(*refs):
        p_refs = refs[:layers]
        w_ref, m_ref, v_ref, g_ref, d_ref, nm_ref, nv_ref = refs[layers:]

        def total(p_ref):
            g = p_ref[0].astype(F32)
            for q in range(1, n_parts):
                g = g + p_ref[q].astype(F32)
            return g

        g = total(p_refs[0])
        for l in range(1, layers):
            g = jnp.where(pl.program_id(0) == l, total(p_refs[l]), g)
        g_ref[...] = g
        d_ref[...], nm_ref[...], nv_ref[...] = _adamw_math(w_ref[...], g, m_ref[...], v_ref[...])

    blk = pl.BlockSpec((None, tk, n), lambda l, i: (l, i, 0))
    return _call(
        body, name=name, grid=(layers, k // tk),
        in_specs=[pl.BlockSpec((n_parts, tk, n), lambda l, i: (0, i, 0))] * layers + [blk, blk, blk], out_specs=[blk] * 4,
        out_shape=[jax.ShapeDtypeStruct((layers, k, n), F32)] * 4, args=(*parts, w, m, v), parallel=True, comm=comm)


def _small_adamw(name, parts, w, m, v, losses):
    count = len(parts)

    def in_order(ref):
        total = ref[0]
        for dev in range(1, N_DEV):
            total = total + ref[dev]
        return total

    def body(*refs):
        p_refs, w_refs, m_refs, v_refs = (refs[j * count:(j + 1) * count] for j in range(4))
        losses_ref = refs[4 * count]
        g_refs, d_refs, nm_refs, nv_refs = (refs[4 * count + 1 + j * count:4 * count + 1 + (j + 1) * count] for j in range(4))
        loss_ref = refs[8 * count + 1]
        for i in range(count):
            g = in_order(p_refs[i])
            g_refs[i][...] = g
            d_refs[i][...], nm_refs[i][...], nv_refs[i][...] = _adamw_math(w_refs[i][...], g, m_refs[i][...], v_refs[i][...])
        loss_ref[...] = in_order(losses_ref)

    res = pl.pallas_call(
        body, name=name,
        out_shape=[jax.ShapeDtypeStruct(a.shape, F32) for a in w] * 4 + [jax.ShapeDtypeStruct(losses.shape[1:], F32)],
        compiler_params=pltpu.CompilerParams(vmem_limit_bytes=VMEM_LIMIT_BYTES))(*parts, *w, *m, *v, losses)
    return [res[j * count:(j + 1) * count] for j in range(4)], res[4 * count]


def _rows(a):
    return a.reshape(-1, a.shape[-1])


def _whole(gathered):
    return jnp.transpose(gathered, (1, 0, 2)).reshape(gathered.shape[1], -1)


SMALL =("ev_norm_g", "ev_conv_a_w", "ev_conv_a_b", "ev_ln_a_g", "ev_ln_a_b", "ev_conv_b_w", "od_norm_g", "od_b_in",
         "od_ln_v_g", "od_ln_v_b", "od_w_s", "od_b_s", "mlp_norm_g", "final_norm_g")
SMALL_SHARDED = ("ev_conv_a_w", "ev_conv_b_w", "od_norm_g", "od_b_in", "od_ln_v_g", "od_ln_v_b")
ORDER = ("ev_norm_g", "ev_w_in", "ev_conv_a_w", "ev_conv_a_b", "ev_ln_a_g", "ev_ln_a_b", "ev_conv_b_w", "ev_w_out",
         "od_norm_g", "od_w_in", "od_b_in", "od_ln_v_g", "od_ln_v_b", "od_w_s", "od_b_s", "od_w_out", "mlp_norm_g",
         "mlp_w1", "mlp_w2", "final_norm_g")


def kernel(x, ev_norm_g, ev_w_in, ev_conv_a_w, ev_conv_a_b, ev_ln_a_g, ev_ln_a_b, ev_conv_b_w, ev_w_out, od_norm_g, od_w_in, od_b_in, od_ln_v_g, od_ln_v_b, od_w_s, od_b_s, od_w_out, mlp_norm_g, mlp_w1, mlp_w2, final_norm_g, loss_target, m_ev_norm_g, m_ev_w_in, m_ev_conv_a_w, m_ev_conv_a_b, m_ev_ln_a_g, m_ev_ln_a_b, m_ev_conv_b_w, m_ev_w_out, m_od_norm_g, m_od_w_in, m_od_b_in, m_od_ln_v_g, m_od_ln_v_b, m_od_w_s, m_od_b_s, m_od_w_out, m_mlp_norm_g, m_mlp_w1, m_mlp_w2, m_final_norm_g, v_ev_norm_g, v_ev_w_in, v_ev_conv_a_w, v_ev_conv_a_b, v_ev_ln_a_g, v_ev_ln_a_b, v_ev_conv_b_w, v_ev_w_out, v_od_norm_g, v_od_w_in, v_od_b_in, v_od_ln_v_g, v_od_ln_v_b, v_od_w_s, v_od_b_s, v_od_w_out, v_mlp_norm_g, v_mlp_w1, v_mlp_w2, v_final_norm_g):
    W = dict(ev_norm_g=ev_norm_g, ev_w_in=ev_w_in, ev_conv_a_w=ev_conv_a_w, ev_conv_a_b=ev_conv_a_b, ev_ln_a_g=ev_ln_a_g,
             ev_ln_a_b=ev_ln_a_b, ev_conv_b_w=ev_conv_b_w, ev_w_out=ev_w_out, od_norm_g=od_norm_g, od_w_in=od_w_in,
             od_b_in=od_b_in, od_ln_v_g=od_ln_v_g, od_ln_v_b=od_ln_v_b, od_w_s=od_w_s, od_b_s=od_b_s, od_w_out=od_w_out,
             mlp_norm_g=mlp_norm_g, mlp_w1=mlp_w1, mlp_w2=mlp_w2, final_norm_g=final_norm_g)
    M = dict(ev_norm_g=m_ev_norm_g, ev_w_in=m_ev_w_in, ev_conv_a_w=m_ev_conv_a_w, ev_conv_a_b=m_ev_conv_a_b,
             ev_ln_a_g=m_ev_ln_a_g, ev_ln_a_b=m_ev_ln_a_b, ev_conv_b_w=m_ev_conv_b_w, ev_w_out=m_ev_w_out,
             od_norm_g=m_od_norm_g, od_w_in=m_od_w_in, od_b_in=m_od_b_in, od_ln_v_g=m_od_ln_v_g, od_ln_v_b=m_od_ln_v_b,
             od_w_s=m_od_w_s, od_b_s=m_od_b_s, od_w_out=m_od_w_out, mlp_norm_g=m_mlp_norm_g, mlp_w1=m_mlp_w1,
             mlp_w2=m_mlp_w2, final_norm_g=m_final_norm_g)
    V = dict(ev_norm_g=v_ev_norm_g, ev_w_in=v_ev_w_in, ev_conv_a_w=v_ev_conv_a_w, ev_conv_a_b=v_ev_conv_a_b,
             ev_ln_a_g=v_ev_ln_a_g, ev_ln_a_b=v_ev_ln_a_b, ev_conv_b_w=v_ev_conv_b_w, ev_w_out=v_ev_w_out,
             od_norm_g=v_od_norm_g, od_w_in=v_od_w_in, od_b_in=v_od_b_in, od_ln_v_g=v_od_ln_v_g, od_ln_v_b=v_od_ln_v_b,
             od_w_s=v_od_w_s, od_b_s=v_od_b_s, od_w_out=v_od_w_out, mlp_norm_g=v_mlp_norm_g, mlp_w1=v_mlp_w1,
             mlp_w2=v_mlp_w2, final_norm_g=v_final_norm_g)

    n_seq, seq, d = x.shape
    t = n_seq * seq
    dev = 4 * lax.axis_index("x") + 2 * lax.axis_index("y") + lax.axis_index("c")
    core = lax.axis_index("c").astype(jnp.int32).reshape(1)

    ev_g, cab, lag, lab = W["ev_norm_g"], W["ev_conv_a_b"], W["ev_ln_a_g"], W["ev_ln_a_b"]
    ws = W["od_w_s"][0]
    bs_b = jnp.broadcast_to(W["od_b_s"][0][:, :, None], (C_GROUPS, CHUNK, CHUNK))
    mlp_g = [W["mlp_norm_g"][l:l + 1] for l in range(2)]
    fin_g = W["final_norm_g"].reshape(1, d)

    h0 = x.reshape(t, d)
    for store in (W, M, V):
        store["ev_w_in"] = jnp.transpose(store["ev_w_in"], (0, 2, 1))
    gather = _gather_comm([W["ev_w_in"][0].astype(BF16)])
    later = [(W["ev_w_out"], 0), (W["od_w_in"], 0), (W["od_w_out"], 0), (W["mlp_w1"], 0), (W["mlp_w2"], 0),
             (W["mlp_w1"], 1), (W["mlp_w2"], 1)]
    n0, (w_ev_out16, w_od_in16, w_od_out16, w1_0, w2_0, w1_1, w2_1) = _rms_fwd("ev_norm", h0, ev_g, comm=gather, casts=later)
    w_ev_in_t = gather.out[0].reshape(IN_EVEN, D_MODEL)

    first, second = (0, 2), (1, 2)

    g_a, g_b = _gather_comm([w_ev_out16] + [_rows(W[n]) for n in SMALL_SHARDED]), _gather_comm([w1_0], first)
    z = _mm_nt("ev_in", n0, w_ev_in_t, IN_EVEN, _ep_store, [F32], comm=[g_a, g_b])[0]
    w_ev_out = g_a.out[0].reshape(D_MODEL, D_MODEL)
    caw, cbw, od_g, od_bin, lvg, lvb = [_whole(g) for g in g_a.out[1:]]

    g_c, g_d = _gather_comm([w1_0], second, into=g_b.out), _gather_comm([w2_0], first)
    mix, a2 = _mixer_fwd(z, seq, caw, cab, lag, lab, cbw, comm=[g_c, g_d])
    w1 = [_whole(g_c.out[0]), None]

    g_e = _gather_comm([w_od_in16], first)
    h1, n1 = _mm_nn("ev_out", mix, w_ev_out, d, _ep_residual_norm, [F32, BF16], extras=(h0,), rows=(mlp_g[0],), comm=g_e)

    g_f, g_f2 = _gather_comm([w2_0], second, into=g_d.out), _gather_comm([w_od_in16], second, into=g_e.out)
    q0 = _mm_nn("mlp0_up", n1, w1[0], D_FF, _ep_relu_sq, [BF16], comm=[g_f, g_f2])[0]
    w2 = [g_f.out[0].reshape(D_FF, D_MODEL), None]
    w_od_in = _whole(g_f2.out[0])

    g_g, g_g2 = _gather_comm([w1_1], first), _gather_comm([w_od_out16])
    h2, n2 = _mm_nn("mlp0_down", q0, w2[0], d, _ep_residual_norm, [F32, BF16], extras=(h1,), rows=(od_g,), comm=[g_g, g_g2])
    w_od_out = g_g2.out[0].reshape(D_MODEL, D_MODEL)
    g_h = _gather_comm([w1_1], second, into=g_g.out)
    pre = _mm_nn("od_in", n2, w_od_in, 2 * C_DIM, _ep_bias, [F32], rows=(od_bin,), comm=g_h)[0]
    w1[1] = _whole(g_h.out[0])
    g_i = _gather_comm([w2_1], (0, 4))
    y = _sgu_fwd(pre, lvg, lvb, ws, bs_b, comm=g_i)
    g_i2 = _gather_comm([w2_1], (1, 4), into=g_i.out)
    h3, n3 = _mm_nn("od_out", y, w_od_out, d, _ep_residual_norm, [F32, BF16], extras=(h2,), rows=(mlp_g[1],), comm=g_i2)
    g_j = _gather_comm([w2_1], second, into=g_i2.out)
    q1 = _mm_nn("mlp1_up", n3, w1[1], D_FF, _ep_relu_sq, [BF16], comm=g_j)[0]
    w2[1] = g_j.out[0].reshape(D_FF, D_MODEL)
    grad, grad16, d_fin_g, loss_part = _mm_nn(
        "mlp1_down", q1, w2[1], d, _ep_final_loss, [F32, BF16], extras=(h3, loss_target.reshape(t, d)), rows=(fin_g,),
        sums=(True, False), tm_want=FUSED_ROWS)

    by_chip = {}

    def swap(name, parts):
        comm = _pair_comm([parts])
        comm.parts, comm.weight = parts, name
        return comm

    def exchange(swapped, halves=False):
        sums = _pair_sum(f"pair_sum_{swapped.weight}", swapped.parts, swapped.out[0], core)
        if not halves:
            comm = _chip_comm([sums])
            comm.weight = swapped.weight
            return comm
        comm = _chip_comm([sums], first)
        comm.sums, comm.weight = sums, swapped.weight
        return comm

    def rest(comm):
        other = _chip_comm([comm.sums], second, into=comm.out)
        other.weight = comm.weight
        return other

    def done(comm):
        by_chip[comm.weight] = comm.out[0]

    dw2_1 = _mm_tn("mlp1_dw2", q1, grad16).reshape(N_DEV, D_FF // N_DEV, D_MODEL)
    s_a = swap("w2_1", dw2_1)
    dp = _mm_nt("mlp1_dq", grad16, w2[1], D_FF, _ep_relu_sq_grad, [BF16], extras=(q1,), comm=s_a)[0]
    c_a = exchange(s_a, halves=True)
    dw1_1 = _mm_tn("mlp1_dw1", n3, dp, col_blocks=N_DEV, comm=c_a)
    c_a2, s_b = rest(c_a), swap("w1_1", dw1_1)
    dn = _mm_nt("mlp1_dn", dp, w1[1], d, _ep_store, [BF16], comm=[c_a2, s_b])[0]
    done(c_a2)
    c_b = exchange(s_b, halves=True)
    grad, grad16, dg_mlp1 = _rms_bwd("mlp1_dn_norm", dn, h3, mlp_g[1], grad)
    dy, d_od_out = _mm_out_proj_grads("od_out_grads", grad16, w_od_out, y)
    s_c = swap("od_out", d_od_out.reshape(N_DEV, D_MODEL // N_DEV, D_MODEL))
    dpre, d_ws, d_bsb, d_lvg, d_lvb, d_bin = _sgu_bwd(pre, dy, lvg, lvb, ws, bs_b, comm=[c_b, s_c])
    c_c = exchange(s_c)
    c_b2 = rest(c_b)
    d_od_in = _mm_tn("od_dw_in", n2, dpre, col_blocks=N_DEV, comm=c_b2)
    done(c_b2)
    s_d = swap("od_in", d_od_in)
    dn = _mm_nt("od_dn", dpre, w_od_in, d, _ep_store, [BF16], comm=[s_d, c_c])[0]
    done(c_c)
    c_d = exchange(s_d)
    grad, grad16, d_od_g = _rms_bwd("od_dn_norm", dn, h2, od_g, grad)
    group_1 = dict(od_norm_g=d_od_g, od_b_in=d_bin, od_ln_v_g=d_lvg, od_ln_v_b=d_lvb,
                   od_w_s=d_ws.reshape(C_GROUPS * CHUNK, CHUNK), od_b_s=d_bsb[:, :, 0], final_norm_g=d_fin_g,
                   mlp_norm_g_1=dg_mlp1, loss=jnp.broadcast_to(loss_part, (SUBLANES, LANE_BLOCK)))
    gather_1 = _gather_comm(list(group_1.values()))
    dw2_0 = _mm_tn("mlp0_dw2", q0, grad16, comm=c_d).reshape(N_DEV, D_FF // N_DEV, D_MODEL)
    done(c_d)
    s_e = swap("w2_0", dw2_0)
    dp = _mm_nt("mlp0_dq", grad16, w2[0], D_FF, _ep_relu_sq_grad, [BF16], extras=(q0,), comm=[s_e, gather_1])[0]
    c_e = exchange(s_e, halves=True)
    dw1_0 = _mm_tn("mlp0_dw1", n1, dp, col_blocks=N_DEV, comm=c_e)
    c_e2, s_f = rest(c_e), swap("w1_0", dw1_0)
    dn = _mm_nt("mlp0_dn", dp, w1[0], d, _ep_store, [BF16], comm=[c_e2, s_f])[0]
    done(c_e2)
    c_f = exchange(s_f, halves=True)
    grad, grad16, dg_mlp0 = _rms_bwd("mlp0_dn_norm", dn, h1, mlp_g[0], grad)
    dmix, d_ev_out = _mm_out_proj_grads("ev_out_grads", grad16, w_ev_out, mix)
    s_g = swap("ev_out", d_ev_out.reshape(N_DEV, D_MODEL // N_DEV, D_MODEL))
    da2, dcv, d_caw, d_cab, d_lag, d_lab, d_cbw = _mixer_bwd_local(z, a2, dmix, seq, lag, lab, comm=[c_f, s_g])
    c_g = exchange(s_g)
    c_f2 = rest(c_f)
    dz = _mixer_bwd_input(z, dmix, da2, dcv, seq, caw, cbw, comm=[c_f2, c_g])
    done(c_f2)
    done(c_g)
    group_2 = dict(ev_conv_a_w=d_caw, ev_conv_a_b=d_cab, ev_ln_a_g=d_lag, ev_ln_a_b=d_lab, ev_conv_b_w=d_cbw,
                   mlp_norm_g_0=dg_mlp0)
    gather_2 = _gather_comm(list(group_2.values()))
    d_ev_in_t = _mm_tn("ev_dw_in", dz, n0, comm=gather_2).reshape(N_DEV, IN_EVEN // N_DEV, D_MODEL)
    s_h = swap("ev_in", d_ev_in_t)
    dn = _mm_nn("ev_dn", dz, w_ev_in_t, d, _ep_store, [BF16], comm=s_h)[0]
    c_h = exchange(s_h)
    grad_x, _, d_ev_g = _rms_bwd("ev_dn_norm", dn, h0, ev_g, grad, comm=c_h)
    done(c_h)

    last_gather = _gather_comm([d_ev_g])
    shard = {"od_w_out": ("od_out",), "mlp_w1": ("w1_0", "w1_1"), "mlp_w2": ("w2_0", "w2_1"), "od_w_in": ("od_in",),
             "ev_w_out": ("ev_out",), "ev_w_in": ("ev_in",)}
    carried = {"od_w_out": last_gather}
    out_g, out_d, out_m, out_v = {}, {}, {}, {}
    for name, keys in shard.items():
        out_g[name], out_d[name], out_m[name], out_v[name] = _sum_adamw(
            f"adamw_{name}", [by_chip[key] for key in keys], W[name], M[name], V[name], comm=carried.get(name))
    for store in (out_g, out_d, out_m, out_v):
        store["ev_w_in"] = jnp.transpose(store["ev_w_in"], (0, 2, 1))

    gathered = dict(zip(group_1, gather_1.out), **dict(zip(group_2, gather_2.out)), ev_norm_g=last_gather.out[0])
    gathered["mlp_norm_g"] = jnp.concatenate([gathered["mlp_norm_g_0"], gathered["mlp_norm_g_1"]], axis=1)
    mine = []
    for n in SMALL:
        g = gathered[n]
        if n in SMALL_SHARDED:
            width = W[n].shape[-1]
            g = lax.dynamic_slice_in_dim(g, dev * width, width, axis=2)
        mine.append(g)
    res, loss_tile = _small_adamw("adamw_small", mine, [_rows(W[n]) for n in SMALL], [_rows(M[n]) for n in SMALL],
                                  [_rows(V[n]) for n in SMALL], gathered["loss"])
    loss = loss_tile[0, 0]
    for store, values in zip((out_g, out_d, out_m, out_v), res):
        for n, value in zip(SMALL, values):
            store[n] = value.reshape(W[n].shape)

    return (loss, grad_x.reshape(n_seq, seq, d), *[out_g[n] for n in ORDER], *[out_d[n] for n in ORDER],
            *[out_m[n] for n in ORDER], *[out_v[n] for n in ORDER])
```

```python
import math

import jax
import jax.numpy as jnp
from jax import lax
from jax.experimental import pallas as pl
from jax.experimental.pallas import tpu as pltpu

F32 = jnp.float32
BF16 = jnp.bfloat16
MESH = pl.DeviceIdType.MESH

D_MODEL = 1024
A_DIM = 512
B_DIM = 512
IN_EVEN = 2 * A_DIM + 3 * B_DIM
A_TAPS = 31
B_TAPS = 3
CHUNK = 128
C_GROUPS = 8
C_DIM = 1024
D_FF = 4096
RMS_EPS = 1e-6
LN_EPS = 1e-5
N_DEV = 8
N_CHIP = 4

ADAM_LR = 0.001
ADAM_B1 = 0.9
ADAM_B2 = 0.999
ADAM_EPS = 1e-08
ADAM_WD = 0.01
ADAM_STEP = 10

A_HALO = 32
B_HALO = 8
VMEM_LIMIT_BYTES = 56 * 1024 * 1024
TINY = 1.1754944e-38
INV_SQRT2 = 1.0 / math.sqrt(2.0)
INV_SQRT_2PI = 1.0 / math.sqrt(2.0 * math.pi)
HBM_SPEC = pl.BlockSpec(memory_space=pltpu.HBM)


def _tile(n, want):
    t = min(n, want)
    while n % t:
        t //= 2
    return t


def _sigmoid(x):
    return 1.0 / (1.0 + jnp.exp(-x))


def _gelu(x):
    return 0.5 * x * (1.0 + lax.erf(x * INV_SQRT2))


def _gelu_and_grad(x):
    cdf = 0.5 * (1.0 + lax.erf(x * INV_SQRT2))
    return x * cdf, cdf + x * jnp.exp(-0.5 * x * x) * INV_SQRT_2PI


def _colsum(x):
    return jnp.sum(x, axis=0, keepdims=True)


class _Comm:
    def __init__(self, ins, out_shapes, sem_shapes, start, finish, middle=None, into=None, sibling=False, chips=False):
        self.ins, self.out_shapes, self.sem_shapes, self.start, self.finish = ins, out_shapes, sem_shapes, start, finish
        self.middle = middle
        self.sibling, self.chips = sibling, chips
        self.into = list(into) if into is not None else []
        self.out = None


def _piece_rows(rows, piece):
    if piece is None:
        return 0, rows
    i, n = piece
    return i * (rows // n), rows // n


MIDDLE_AT = 0.75
BARRIER_IDS = {(True, True): 0, (True, False): 1, (False, True): 2}
FUSED_ROWS = 512


def _call(body, *, name, grid, in_specs, out_specs, out_shape, args, scratch_shapes=(), parallel=False, comm=None):
    comms = [] if comm is None else (list(comm) if isinstance(comm, (list, tuple)) else [comm])
    if not comms:
        sem = ("parallel" if parallel else "arbitrary",) * len(grid)
        return pl.pallas_call(
            body, name=name, grid=grid, in_specs=list(in_specs), out_specs=list(out_specs), out_shape=list(out_shape),
            scratch_shapes=list(scratch_shapes),
            compiler_params=pltpu.CompilerParams(dimension_semantics=sem, vmem_limit_bytes=VMEM_LIMIT_BYTES),
        )(*args)
    n_in, n_out, n_scr = len(in_specs), len(out_shape), len(scratch_shapes)
    c_ins_all = [a for cm in comms for a in cm.ins]
    c_into_all = [a for cm in comms for a in cm.into]
    c_out_shapes = [s for cm in comms for s in cm.out_shapes]
    c_sem_shapes = [s for cm in comms for s in cm.sem_shapes]
    aliases, in_pos, out_pos = {}, n_in + len(c_ins_all), n_out
    for cm in comms:
        for j in range(len(cm.into)):
            aliases[in_pos + j] = out_pos + j
        in_pos += len(cm.into)
        out_pos += len(cm.out_shapes)
    to_sibling = any(cm.sibling for cm in comms)
    to_chips = any(cm.chips for cm in comms)
    steps = grid
    total = math.prod(steps)
    first_step = (0,) * len(steps)
    last_step = tuple(s - 1 for s in steps)
    middle_step = None
    if 0 < int(MIDDLE_AT * total) < total - 1:
        rest, idx = int(MIDDLE_AT * total), []
        for s in reversed(steps):
            idx.append(rest % s)
            rest //= s
        middle_step = tuple(reversed(idx))

    def carrying(*refs):
        pos = 0
        ins = refs[pos:pos + n_in]; pos += n_in
        c_ins = refs[pos:pos + len(c_ins_all)]; pos += len(c_ins_all) + len(c_into_all)
        outs = refs[pos:pos + n_out]; pos += n_out
        c_outs = refs[pos:pos + len(c_out_shapes)]; pos += len(c_out_shapes)
        scr = refs[pos:pos + n_scr]; pos += n_scr
        c_sems = refs[pos:]
        views, i0, o0, s0 = [], 0, 0, 0
        for cm in comms:
            views.append((c_ins[i0:i0 + len(cm.ins)], c_outs[o0:o0 + len(cm.out_shapes)], c_sems[s0:s0 + len(cm.sem_shapes)]))
            i0, o0, s0 = i0 + len(cm.ins), o0 + len(cm.out_shapes), s0 + len(cm.sem_shapes)

        def at(step):
            hit = pl.program_id(0) == step[0]
            for axis in range(1, len(steps)):
                hit = jnp.logical_and(hit, pl.program_id(axis) == step[axis])
            return hit

        @pl.when(at(first_step))
        def _():
            x, y, c, chips = _place()
            peers = ([(x, y, 1 - c)] if to_sibling else []) + ([(*chip, c) for chip in chips] if to_chips else [])
            barrier = pltpu.get_barrier_semaphore()
            for peer in peers:
                pl.semaphore_signal(barrier, inc=1, device_id=peer, device_id_type=MESH)
            pl.semaphore_wait(barrier, len(peers))
            for cm, view in zip(comms, views):
                cm.start(*view)

        if middle_step is not None:
            @pl.when(at(middle_step))
            def _():
                for cm, view in zip(comms, views):
                    if cm.middle is not None:
                        cm.middle(*view)

        body(*ins, *outs, *scr)

        @pl.when(at(last_step))
        def _():
            for cm, view in zip(comms, views):
                if cm.middle is not None and middle_step is None:
                    cm.middle(*view)
            for cm, view in zip(comms, views):
                cm.finish(*view)

    res = pl.pallas_call(
        carrying, name=name, grid=grid,
        in_specs=[*in_specs, *[HBM_SPEC] * (len(c_ins_all) + len(c_into_all))],
        out_specs=[*out_specs, *[HBM_SPEC] * len(c_out_shapes)],
        out_shape=[*out_shape, *c_out_shapes], scratch_shapes=[*scratch_shapes, *c_sem_shapes],
        input_output_aliases=aliases,
        compiler_params=pltpu.CompilerParams(dimension_semantics=("arbitrary",) * len(grid), vmem_limit_bytes=VMEM_LIMIT_BYTES,
                                             collective_id=BARRIER_IDS[to_sibling, to_chips]),
    )(*args, *c_ins_all, *c_into_all)
    pos = n_out
    for cm in comms:
        cm.out = list(res[pos:pos + len(cm.out_shapes)])
        pos += len(cm.out_shapes)
    return list(res[:n_out])


def _place():
    x, y, c = lax.axis_index("x"), lax.axis_index("y"), lax.axis_index("c")
    return x, y, c, [(1 - x, y), (x, 1 - y), (1 - x, 1 - y)]


def _gather_comm(shards, piece=None, into=None):
    nw = len(shards)
    spans = [_piece_rows(s.shape[0], piece) for s in shards]

    def plan(ins, outs, sems):
        send_sems, recv_sems, local_sems = sems
        x, y, c, chips = _place()
        me, sibling = (x, y, c), (x, y, 1 - c)

        def slot(w, p):
            return outs[w].at[4 * p[0] + 2 * p[1] + p[2], pl.ds(*spans[w])]

        def mine(w):
            return ins[w].at[pl.ds(*spans[w])]

        def copy(w, k, block, to, src=None):
            return pltpu.make_async_remote_copy(
                src_ref=slot(w, block) if src is None else src, dst_ref=slot(w, block),
                send_sem=send_sems.at[w, k], recv_sem=recv_sems.at[w, k], device_id=to, device_id_type=MESH)

        local = [pltpu.make_async_copy(mine(w), slot(w, me), local_sems.at[w]) for w in range(nw)]
        first = [[copy(w, 0, me, sibling, src=mine(w))] + [copy(w, 1 + j, me, (*chip, c), src=mine(w)) for j, chip in enumerate(chips)]
                 for w in range(nw)]
        landed = [[copy(w, 1 + j, (*chip, c), me) for j, chip in enumerate(chips)] for w in range(nw)]
        passed = [[copy(w, 4 + j, (*chip, c), sibling) for j, chip in enumerate(chips)] for w in range(nw)]
        from_sibling = [[copy(w, 0, sibling, me)] + [copy(w, 4 + j, (*chip, 1 - c), me) for j, chip in enumerate(chips)]
                        for w in range(nw)]
        return local, first, landed, passed, from_sibling

    def start(ins, outs, sems):
        local, first, _, _, _ = plan(ins, outs, sems)
        for cp in local:
            cp.start()
        for row in first:
            for cp in row:
                cp.start()

    def middle(ins, outs, sems):
        _, _, landed, passed, _ = plan(ins, outs, sems)
        for w in range(nw):
            for j in range(3):
                landed[w][j].wait_recv()
                passed[w][j].start()

    def finish(ins, outs, sems):
        local, first, landed, passed, from_sibling = plan(ins, outs, sems)
        for w in range(nw):
            for cp in from_sibling[w]:
                cp.wait_recv()
        for w in range(nw):
            for cp in first[w] + passed[w]:
                cp.wait_send()
        for cp in local:
            cp.wait()

    return _Comm(list(shards), [jax.ShapeDtypeStruct((N_DEV, *s.shape), s.dtype) for s in shards],
                 [pltpu.SemaphoreType.DMA((nw, 7)), pltpu.SemaphoreType.DMA((nw, 7)), pltpu.SemaphoreType.DMA((nw,))],
                 start, finish, middle, into=into, sibling=True, chips=True)


def _pair_comm(parts):
    nw = len(parts)

    def plan(ins, outs, sems):
        send_sems, recv_sems = sems
        x, y, c, _ = _place()
        return [pltpu.make_async_remote_copy(
            src_ref=ins[w].at[2 * q + 1 - c], dst_ref=outs[w].at[q], send_sem=send_sems.at[w, q], recv_sem=recv_sems.at[w, q],
            device_id=(x, y, 1 - c), device_id_type=MESH) for w in range(nw) for q in range(N_CHIP)]

    def start(ins, outs, sems):
        for cp in plan(ins, outs, sems):
            cp.start()

    def finish(ins, outs, sems):
        for cp in plan(ins, outs, sems):
            cp.wait()

    return _Comm(list(parts), [jax.ShapeDtypeStruct((N_CHIP, *p.shape[1:]), p.dtype) for p in parts],
                 [pltpu.SemaphoreType.DMA((nw, N_CHIP)), pltpu.SemaphoreType.DMA((nw, N_CHIP))], start, finish, sibling=True)


def _chip_comm(sums, piece=None, into=None):
    nw = len(sums)
    spans = [_piece_rows(s.shape[1], piece) for s in sums]

    def plan(ins, outs, sems):
        send_sems, recv_sems, local_sems = sems
        x, y, c, chips = _place()
        my_chip = 2 * x + y
        local = [pltpu.make_async_copy(ins[w].at[my_chip, pl.ds(*spans[w])], outs[w].at[my_chip, pl.ds(*spans[w])],
                                       local_sems.at[w]) for w in range(nw)]
        remote = [pltpu.make_async_remote_copy(
            src_ref=ins[w].at[2 * chip[0] + chip[1], pl.ds(*spans[w])], dst_ref=outs[w].at[my_chip, pl.ds(*spans[w])],
            send_sem=send_sems.at[w, j], recv_sem=recv_sems.at[w, j], device_id=(*chip, c), device_id_type=MESH)
            for w in range(nw) for j, chip in enumerate(chips)]
        return local, remote

    def start(ins, outs, sems):
        local, remote = plan(ins, outs, sems)
        for cp in local + remote:
            cp.start()

    def finish(ins, outs, sems):
        local, remote = plan(ins, outs, sems)
        for cp in remote + local:
            cp.wait()

    return _Comm(list(sums), [jax.ShapeDtypeStruct(s.shape, s.dtype) for s in sums],
                 [pltpu.SemaphoreType.DMA((nw, 3)), pltpu.SemaphoreType.DMA((nw, 3)), pltpu.SemaphoreType.DMA((nw,))],
                 start, finish, into=into, chips=True)


def _matmul(name, a, b, *, kind, m, n, k, a_spec, b_spec, tm, tn, tk, out_shape, out_specs, epilogue,
            extras=(), extra_specs=(), comm=None):
    dims = {"nn": (((1,), (0,)), ((), ())), "nt": (((1,), (1,)), ((), ())), "tn": (((0,), (0,)), ((), ()))}[kind]
    nk = k // tk
    n_extra = len(extras)
    n_out = len(out_shape)

    def body(a_ref, b_ref, *rest):
        extra_refs = rest[:n_extra]
        out_refs = rest[n_extra:n_extra + n_out]
        part = lax.dot_general(a_ref[...], b_ref[...], dims, preferred_element_type=F32)
        if nk == 1:
            epilogue(part, extra_refs, out_refs)
            return
        acc_ref = rest[n_extra + n_out]
        step = pl.program_id(2)

        @pl.when(step == 0)
        def _():
            acc_ref[...] = part

        @pl.when(jnp.logical_and(step > 0, step < nk - 1))
        def _():
            acc_ref[...] += part

        @pl.when(step == nk - 1)
        def _():
            epilogue(acc_ref[...] + part, extra_refs, out_refs)

    return _call(
        body, name=name, grid=(m // tm, n // tn, nk), in_specs=[a_spec, b_spec, *extra_specs], out_specs=out_specs,
        out_shape=out_shape, scratch_shapes=[pltpu.VMEM((tm, tn), F32)] if nk > 1 else [], args=(a, b, *extras), comm=comm)


def _mm_operands(m, n, tm, tn, out_dtypes, extras, rows, sums):
    tile = pl.BlockSpec((tm, tn), lambda i, j, kk: (i, j))
    row = pl.BlockSpec((1, tn), lambda i, j, kk: (0, j))
    one = pl.BlockSpec((1, 1), lambda i, j, kk: (0, 0))
    out_shape = [jax.ShapeDtypeStruct((m, n), dt) for dt in out_dtypes]
    out_shape += [jax.ShapeDtypeStruct((1, n if wide else 1), F32) for wide in sums]
    out_specs = [tile] * len(out_dtypes) + [row if wide else one for wide in sums]
    return (*extras, *rows), [tile] * len(extras) + [row] * len(rows), out_shape, out_specs


def _row_tile(m, k, tm_want):
    return _tile(m, tm_want or (1024 if k <= 1024 else 512))


def _mm_nn(name, a, b, n, epilogue, out_dtypes, *, extras=(), rows=(), sums=(), tm_want=None, comm=None):
    m, k = a.shape
    blocked = b.ndim == 3
    tm = _row_tile(m, k, tm_want)
    tn = b.shape[-1] if blocked else _tile(n, 1024)
    tk = k
    extras, extra_specs, out_shape, out_specs = _mm_operands(m, n, tm, tn, out_dtypes, extras, rows, sums)
    if blocked:
        b_spec = pl.BlockSpec((None, tk, tn), lambda i, j, kk: (j, kk, 0))
    else:
        b_spec = pl.BlockSpec((tk, tn), lambda i, j, kk: (kk, j))
    return _matmul(
        name, a, b, kind="nn", m=m, n=n, k=k, tm=tm, tn=tn, tk=tk,
        a_spec=pl.BlockSpec((tm, tk), lambda i, j, kk: (i, kk)), b_spec=b_spec, out_shape=out_shape, out_specs=out_specs,
        epilogue=epilogue, extras=extras, extra_specs=extra_specs, comm=comm)


def _mm_nt(name, a, b, n, epilogue, out_dtypes, *, extras=(), rows=(), sums=(), tm_want=None, comm=None):
    m, k = a.shape
    blocked = b.ndim == 3
    tm = _row_tile(m, k, tm_want)
    tn = _tile(n, 1024)
    tk = b.shape[-1] if blocked else k
    extras, extra_specs, out_shape, out_specs = _mm_operands(m, n, tm, tn, out_dtypes, extras, rows, sums)
    if blocked:
        b_spec = pl.BlockSpec((None, tn, tk), lambda i, j, kk: (kk, j, 0))
    else:
        b_spec = pl.BlockSpec((tn, tk), lambda i, j, kk: (j, kk))
    return _matmul(
        name, a, b, kind="nt", m=m, n=n, k=k, tm=tm, tn=tn, tk=tk,
        a_spec=pl.BlockSpec((tm, tk), lambda i, j, kk: (i, kk)), b_spec=b_spec, out_shape=out_shape, out_specs=out_specs,
        epilogue=epilogue, extras=extras, extra_specs=extra_specs, comm=comm)


def _mm_out_proj_grads(name, grad16, w, act):
    t, n = grad16.shape
    k = w.shape[0]
    tm = _tile(t, 1024)

    def body(g_ref, w_ref, a_ref, din_ref, dw_ref):
        g = g_ref[...]
        din_ref[...] = lax.dot_general(g, w_ref[...], (((1,), (1,)), ((), ())), preferred_element_type=F32)
        part = lax.dot_general(a_ref[...], g, (((0,), (0,)), ((), ())), preferred_element_type=F32)

        @pl.when(pl.program_id(0) == 0)
        def _():
            dw_ref[...] = part

        @pl.when(pl.program_id(0) > 0)
        def _():
            dw_ref[...] += part

    return _call(
        body, name=name, grid=(t // tm,),
        in_specs=[pl.BlockSpec((tm, n), lambda i: (i, 0)), pl.BlockSpec((k, n), lambda i: (0, 0)),
                  pl.BlockSpec((tm, k), lambda i: (i, 0))],
        out_specs=[pl.BlockSpec((tm, k), lambda i: (i, 0)), pl.BlockSpec((k, n), lambda i: (0, 0))],
        out_shape=[jax.ShapeDtypeStruct((t, k), F32), jax.ShapeDtypeStruct((k, n), F32)], args=(grad16, w, act))


def _mm_tn(name, a, b, *, col_blocks=0, comm=None):
    t, k = a.shape
    n = b.shape[1]
    tm = _row_tile(k, t, None)
    tn = n // col_blocks if col_blocks else _tile(n, 1024)
    tk = t
    if col_blocks:
        out_shape = [jax.ShapeDtypeStruct((col_blocks, k, tn), F32)]
        out_specs = [pl.BlockSpec((None, tm, tn), lambda i, j, kk: (j, i, 0))]
    else:
        out_shape = [jax.ShapeDtypeStruct((k, n), F32)]
        out_specs = [pl.BlockSpec((tm, tn), lambda i, j, kk: (i, j))]

    def epilogue(acc, extra_refs, out_refs):
        out_refs[0][...] = acc

    return _matmul(
        name, a, b, kind="tn", m=k, n=n, k=t, tm=tm, tn=tn, tk=tk,
        a_spec=pl.BlockSpec((tk, tm), lambda i, j, kk: (kk, i)), b_spec=pl.BlockSpec((tk, tn), lambda i, j, kk: (kk, j)),
        out_shape=out_shape, out_specs=out_specs, epilogue=epilogue, comm=comm)[0]


def _ep_store(acc, extra_refs, out_refs):
    out_refs[0][...] = acc.astype(out_refs[0].dtype)


def _ep_bias(acc, extra_refs, out_refs):
    out_refs[0][...] = acc + extra_refs[0][...]


def _rms(x):
    r = lax.rsqrt(jnp.mean(x * x, axis=-1, keepdims=True) + RMS_EPS)
    return r, x * r


def _rms_grad(dn, r, xr, g):
    dy = dn * g
    return r * (dy - xr * jnp.mean(dy * xr, axis=-1, keepdims=True))


def _ep_residual_norm(acc, extra_refs, out_refs):
    h = extra_refs[0][...] + acc
    out_refs[0][...] = h
    _, xr = _rms(h)
    out_refs[1][...] = (xr * extra_refs[1][...]).astype(BF16)


def _ep_final_loss(acc, extra_refs, out_refs):
    @pl.when(pl.program_id(0) == 0)
    def _():
        out_refs[2][...] = jnp.zeros_like(out_refs[2])
        out_refs[3][...] = jnp.zeros_like(out_refs[3])

    h = extra_refs[0][...] + acc
    g = extra_refs[2][...]
    r, xr = _rms(h)
    err = xr * g - extra_refs[1][...]
    out_refs[3][...] += 0.5 * jnp.sum(jnp.mean(err * err, axis=-1, keepdims=True), axis=0, keepdims=True)
    dout = err * (1.0 / h.shape[-1])
    out_refs[2][...] += _colsum(dout * xr)
    out = _rms_grad(dout, r, xr, g)
    out_refs[0][...] = out
    out_refs[1][...] = out.astype(BF16)


def _ep_relu_sq(acc, extra_refs, out_refs):
    r = jnp.maximum(acc, 0.0)
    out_refs[0][...] = (r * r).astype(BF16)


def _ep_relu_sq_grad(acc, extra_refs, out_refs):
    q = extra_refs[0][...].astype(F32)
    out_refs[0][...] = (acc * (2.0 * q * lax.rsqrt(jnp.maximum(q, TINY)))).astype(BF16)


def _rms_fwd(name, h, g, comm=None, casts=()):
    t, d = h.shape
    tt = _tile(t, 512)
    steps = t // tt
    n_cast = len(casts)

    def body(h_ref, g_ref, *rest):
        n_ref = rest[n_cast]
        x = h_ref[...]
        r = lax.rsqrt(jnp.mean(x * x, axis=-1, keepdims=True) + RMS_EPS)
        n_ref[...] = (x * r * g_ref[...]).astype(BF16)
        for src, dst in zip(rest[:n_cast], rest[n_cast + 1:]):
            dst[...] = src[...].astype(BF16)

    cast_in = [pl.BlockSpec((None, w.shape[1] // steps, w.shape[2]), lambda i, l=l: (l, i, 0)) for w, l in casts]
    cast_out = [pl.BlockSpec((w.shape[1] // steps, w.shape[2]), lambda i: (i, 0)) for w, _ in casts]
    res = _call(
        body, name=name, grid=(steps,),
        in_specs=[pl.BlockSpec((tt, d), lambda i: (i, 0)), pl.BlockSpec((1, d), lambda i: (0, 0)), *cast_in],
        out_specs=[pl.BlockSpec((tt, d), lambda i: (i, 0)), *cast_out],
        out_shape=[jax.ShapeDtypeStruct((t, d), BF16)] + [jax.ShapeDtypeStruct(w.shape[1:], BF16) for w, _ in casts],
        args=(h, g, *[w for w, _ in casts]), parallel=True, comm=comm)
    return res[0], res[1:]


def _rms_bwd(name, dn, h, g, grad_in, comm=None):
    t, d = h.shape
    tt = _tile(t, 512)

    def body(dn_ref, h_ref, g_ref, gin_ref, gout_ref, gout16_ref, dg_ref):
        @pl.when(pl.program_id(0) == 0)
        def _():
            dg_ref[...] = jnp.zeros_like(dg_ref)

        dnv = dn_ref[...].astype(F32)
        r, xr = _rms(h_ref[...])
        dg_ref[...] += _colsum(dnv * xr)
        out = gin_ref[...] + _rms_grad(dnv, r, xr, g_ref[...])
        gout_ref[...] = out
        gout16_ref[...] = out.astype(BF16)

    row = pl.BlockSpec((tt, d), lambda i: (i, 0))
    vec = pl.BlockSpec((1, d), lambda i: (0, 0))
    return _call(
        body, name=name, grid=(t // tt,), in_specs=[row, row, vec, row], out_specs=[row, row, vec],
        out_shape=[jax.ShapeDtypeStruct((t, d), F32), jax.ShapeDtypeStruct((t, d), BF16), jax.ShapeDtypeStruct((1, d), F32)],
        args=(dn, h, g, grad_in), comm=comm)


def _mixer_windows(z_ref, zh_ref, first, a1_s, cb_s, tt):
    sig = _sigmoid(z_ref[:, A_DIM:2 * A_DIM])
    if a1_s is not None:
        a1_s[A_HALO:A_HALO + tt, :] = z_ref[:, 0:A_DIM] * sig
        a1_h = zh_ref[:, 0:A_DIM] * _sigmoid(zh_ref[:, A_DIM:2 * A_DIM])
        a1_s[0:A_HALO, :] = jnp.where(first, 0.0, a1_h)
    cb_s[B_HALO:B_HALO + tt, :] = z_ref[:, 3 * A_DIM:4 * A_DIM] * z_ref[:, 4 * A_DIM:5 * A_DIM]
    cb_h = zh_ref[A_HALO - B_HALO:A_HALO, 3 * A_DIM:4 * A_DIM] * zh_ref[A_HALO - B_HALO:A_HALO, 4 * A_DIM:5 * A_DIM]
    cb_s[0:B_HALO, :] = jnp.where(first, 0.0, cb_h)
    return sig


def _causal_conv(win_s, w_ref, taps, halo, tt):
    base = halo - (taps - 1)
    acc = w_ref[0:1, :] * win_s[pl.ds(base, tt), :]
    for k in range(1, taps):
        acc = acc + w_ref[k:k + 1, :] * win_s[pl.ds(base + k, tt), :]
    return acc


SUBLANES = 8
LANE_BLOCK = 128
CONV_ROWS = 32
TAP_GRAD_ROWS = 64
NORM_ROWS = 32
SHIFT_ROWS = A_HALO - SUBLANES


def _shifted_copies(win_s, sh_s, tt):
    for b in range(1, SUBLANES):
        sh_s[b - 1] = win_s[pl.ds(b, tt + SHIFT_ROWS), :]


def _window_rows(win_s, sh_s, offset, rows, cols):
    b = offset % SUBLANES
    if b == 0:
        return win_s[pl.ds(offset, rows), cols]
    return sh_s[b - 1, pl.ds(offset - b, rows), cols]


def _blocks(tt, rows):
    rb = min(tt, rows)
    return rb, [(r, slice(lb * LANE_BLOCK, (lb + 1) * LANE_BLOCK))
                for lb in range(A_DIM // LANE_BLOCK) for r in range(0, tt, rb)]


def _conv_taps(win_s, sh_s, w_ref, offsets, out_s, tt, bias_ref=None):
    rb, blocks = _blocks(tt, CONV_ROWS)
    for r, cols in blocks:
        acc = w_ref[0:1, cols] * _window_rows(win_s, sh_s, r + offsets[0], rb, cols)
        for k in range(1, len(offsets)):
            acc = acc + w_ref[k:k + 1, cols] * _window_rows(win_s, sh_s, r + offsets[k], rb, cols)
        out_s[r:r + rb, cols] = acc if bias_ref is None else acc + bias_ref[:, cols]


A_CAUSAL = [A_HALO - (A_TAPS - 1) + k for k in range(A_TAPS)]
A_ANTICAUSAL = [A_TAPS - 1 - k for k in range(A_TAPS)]


def _layer_norm_stats(x):
    mu = jnp.mean(x, axis=-1, keepdims=True)
    xc = x - mu
    rstd = lax.rsqrt(jnp.mean(xc * xc, axis=-1, keepdims=True) + LN_EPS)
    return xc * rstd, rstd


def _mixer_specs(seq, tt):
    tiles_per_seq = seq // tt
    halo_blocks = tt // A_HALO
    z_spec = pl.BlockSpec((tt, IN_EVEN), lambda i: (i, 0))
    zh_spec = pl.BlockSpec((A_HALO, IN_EVEN), lambda i: (jnp.maximum(i * halo_blocks - 1, 0), 0))
    return tiles_per_seq, z_spec, zh_spec


def _vec_spec(rows, cols):
    return pl.BlockSpec((rows, cols), lambda i: (0, 0))


def _mixer_fwd(z, seq, caw, cab, lag, lab, cbw, comm=None):
    t = z.shape[0]
    tt = _tile(seq, 256)
    tiles_per_seq, z_spec, zh_spec = _mixer_specs(seq, tt)

    def body(z_ref, zh_ref, caw_ref, cab_ref, lag_ref, lab_ref, cbw_ref, mix_ref, a2_ref, a1_s, cb_s, sh_s):
        first = (pl.program_id(0) % tiles_per_seq) == 0
        _mixer_windows(z_ref, zh_ref, first, a1_s, cb_s, tt)
        _shifted_copies(a1_s, sh_s, tt)
        _conv_taps(a1_s, sh_s, caw_ref, A_CAUSAL, a2_ref, tt, bias_ref=cab_ref)
        for r in range(0, tt, min(tt, NORM_ROWS)):
            rows = slice(r, r + min(tt, NORM_ROWS))
            xhat, _ = _layer_norm_stats(a2_ref[rows, :])
            a3 = xhat * lag_ref[...] + lab_ref[...]
            mix_ref[rows, 0:A_DIM] = (a3 * _sigmoid(a3)).astype(BF16)
        cv = _causal_conv(cb_s, cbw_ref, B_TAPS, B_HALO, tt)
        mix_ref[:, A_DIM:A_DIM + B_DIM] = (z_ref[:, 2 * A_DIM:3 * A_DIM] * cv).astype(BF16)

    return _call(
        body, name="mixer_fwd", grid=(t // tt,),
        in_specs=[z_spec, zh_spec, _vec_spec(A_TAPS, A_DIM), _vec_spec(1, A_DIM), _vec_spec(1, A_DIM), _vec_spec(1, A_DIM),
                  _vec_spec(B_TAPS, B_DIM)],
        out_specs=[pl.BlockSpec((tt, A_DIM + B_DIM), lambda i: (i, 0)), pl.BlockSpec((tt, A_DIM), lambda i: (i, 0))],
        out_shape=[jax.ShapeDtypeStruct((t, A_DIM + B_DIM), BF16), jax.ShapeDtypeStruct((t, A_DIM), F32)],
        scratch_shapes=[pltpu.VMEM((A_HALO + tt, A_DIM), F32), pltpu.VMEM((B_HALO + tt, B_DIM), F32),
                        pltpu.VMEM((SUBLANES - 1, tt + SHIFT_ROWS, A_DIM), F32)],
        args=(z, z, caw, cab, lag, lab, cbw), parallel=True, comm=comm)


def _mixer_bwd_local(z, a2, dmix, seq, lag, lab, comm=None):
    t = z.shape[0]
    tt = _tile(seq, 256)
    tiles_per_seq, z_spec, zh_spec = _mixer_specs(seq, tt)

    def body(z_ref, zh_ref, a2_ref, dmix_ref, lag_ref, lab_ref,
             da2_ref, dcv_ref, dcaw_ref, dcab_ref, dlag_ref, dlab_ref, dcbw_ref, a1_s, cb_s, sh_s):
        @pl.when(pl.program_id(0) == 0)
        def _():
            for ref in (dcaw_ref, dcab_ref, dlag_ref, dlab_ref, dcbw_ref):
                ref[...] = jnp.zeros_like(ref)

        first = (pl.program_id(0) % tiles_per_seq) == 0
        _mixer_windows(z_ref, zh_ref, first, a1_s, cb_s, tt)
        _shifted_copies(a1_s, sh_s, tt)
        for r in range(0, tt, min(tt, NORM_ROWS)):
            rows = slice(r, r + min(tt, NORM_ROWS))
            xhat, rstd = _layer_norm_stats(a2_ref[rows, :])
            a3 = xhat * lag_ref[...] + lab_ref[...]
            s3 = _sigmoid(a3)
            da3 = dmix_ref[rows, 0:A_DIM] * (s3 * (1.0 + a3 * (1.0 - s3)))
            dlag_ref[...] += _colsum(da3 * xhat)
            dlab_ref[...] += _colsum(da3)
            dxh = da3 * lag_ref[...]
            da2 = rstd * (dxh - jnp.mean(dxh, axis=-1, keepdims=True) - xhat * jnp.mean(dxh * xhat, axis=-1, keepdims=True))
            da2_ref[rows, :] = da2
            dcab_ref[...] += _colsum(da2)
        rb, blocks = _blocks(tt, TAP_GRAD_ROWS)
        for r, cols in blocks:
            da2_b = da2_ref[r:r + rb, cols]
            for k in range(A_TAPS):
                dcaw_ref[k:k + 1, cols] += _colsum(da2_b * _window_rows(a1_s, sh_s, r + A_CAUSAL[k], rb, cols))
        dcv =dmix_ref[:, A_DIM:A_DIM + B_DIM] * z_ref[:, 2 * A_DIM:3 * A_DIM]
        dcv_ref[...] = dcv
        for k in range(B_TAPS):
            dcbw_ref[k:k + 1, :] += _colsum(dcv * cb_s[pl.ds(B_HALO - (B_TAPS - 1) + k, tt), :])

    half = pl.BlockSpec((tt, A_DIM), lambda i: (i, 0))
    return _call(
        body, name="mixer_bwd_local", grid=(t // tt,),
        in_specs=[z_spec, zh_spec, half, pl.BlockSpec((tt, A_DIM + B_DIM), lambda i: (i, 0)),
                  _vec_spec(1, A_DIM), _vec_spec(1, A_DIM)],
        out_specs=[half, half, _vec_spec(A_TAPS, A_DIM), _vec_spec(1, A_DIM), _vec_spec(1, A_DIM), _vec_spec(1, A_DIM),
                   _vec_spec(B_TAPS, B_DIM)],
        out_shape=[jax.ShapeDtypeStruct((t, A_DIM), F32), jax.ShapeDtypeStruct((t, B_DIM), F32),
                   jax.ShapeDtypeStruct((A_TAPS, A_DIM), F32), jax.ShapeDtypeStruct((1, A_DIM), F32),
                   jax.ShapeDtypeStruct((1, A_DIM), F32), jax.ShapeDtypeStruct((1, A_DIM), F32),
                   jax.ShapeDtypeStruct((B_TAPS, B_DIM), F32)],
        scratch_shapes=[pltpu.VMEM((A_HALO + tt, A_DIM), F32), pltpu.VMEM((B_HALO + tt, B_DIM), F32),
                        pltpu.VMEM((SUBLANES - 1, tt + SHIFT_ROWS, A_DIM), F32)],
        args=(z, z, a2, dmix, lag, lab), comm=comm)


def _mixer_bwd_input(z, dmix, da2, dcv, seq, caw, cbw, comm=None):
    t = z.shape[0]
    tt = _tile(seq, 256)
    tiles_per_seq, z_spec, zh_spec = _mixer_specs(seq, tt)
    a_blocks = tt // A_HALO
    b_blocks = tt // B_HALO
    last_a = t // A_HALO - 1
    last_b = t // B_HALO - 1

    def body(z_ref, zh_ref, dmix_ref, da2_ref, da2n_ref, dcv_ref, dcvn_ref, caw_ref, cbw_ref, dz_ref, cb_s, da2_s, dcv_s,
             sh_s, da1_s):
        pos = pl.program_id(0) % tiles_per_seq
        first = pos == 0
        last = pos == tiles_per_seq - 1
        sig = _mixer_windows(z_ref, zh_ref, first, None, cb_s, tt)
        da2_s[0:tt, :] = da2_ref[...]
        da2_s[tt:tt + A_HALO, :] = jnp.where(last, 0.0, da2n_ref[...])
        dcv_s[0:tt, :] = dcv_ref[...]
        dcv_s[tt:tt + B_HALO, :] = jnp.where(last, 0.0, dcvn_ref[...])
        _shifted_copies(da2_s, sh_s, tt)
        _conv_taps(da2_s, sh_s, caw_ref, A_ANTICAUSAL, da1_s, tt)
        da1 = da1_s[...]
        dz_ref[:, 0:A_DIM] = (da1 * sig).astype(BF16)
        dz_ref[:, A_DIM:2 * A_DIM] = (da1 * z_ref[:, 0:A_DIM] * sig * (1.0 - sig)).astype(BF16)
        cv = _causal_conv(cb_s, cbw_ref, B_TAPS, B_HALO, tt)
        dz_ref[:, 2 * A_DIM:3 * A_DIM] = (dmix_ref[:, A_DIM:A_DIM + B_DIM] * cv).astype(BF16)
        dcb = cbw_ref[0:1, :] * dcv_s[pl.ds(B_TAPS - 1, tt), :]
        for k in range(1, B_TAPS):
            dcb = dcb + cbw_ref[k:k + 1, :] * dcv_s[pl.ds(B_TAPS - 1 - k, tt), :]
        dz_ref[:, 3 * A_DIM:4 * A_DIM] = (dcb * z_ref[:, 4 * A_DIM:5 * A_DIM]).astype(BF16)
        dz_ref[:, 4 * A_DIM:5 * A_DIM] = (dcb * z_ref[:, 3 * A_DIM:4 * A_DIM]).astype(BF16)

    half = pl.BlockSpec((tt, A_DIM), lambda i: (i, 0))
    return _call(
        body, name="mixer_bwd_input", grid=(t // tt,),
        in_specs=[z_spec, zh_spec, pl.BlockSpec((tt, A_DIM + B_DIM), lambda i: (i, 0)),
                  half, pl.BlockSpec((A_HALO, A_DIM), lambda i: (jnp.minimum((i + 1) * a_blocks, last_a), 0)),
                  half, pl.BlockSpec((B_HALO, B_DIM), lambda i: (jnp.minimum((i + 1) * b_blocks, last_b), 0)),
                  _vec_spec(A_TAPS, A_DIM), _vec_spec(B_TAPS, B_DIM)],
        out_specs=[pl.BlockSpec((tt, IN_EVEN), lambda i: (i, 0))],
        out_shape=[jax.ShapeDtypeStruct((t, IN_EVEN), BF16)],
        scratch_shapes=[pltpu.VMEM((B_HALO + tt, B_DIM), F32),
                        pltpu.VMEM((tt + A_HALO, A_DIM), F32), pltpu.VMEM((tt + B_HALO, B_DIM), F32),
                        pltpu.VMEM((SUBLANES - 1, tt + SHIFT_ROWS, A_DIM), F32), pltpu.VMEM((tt, A_DIM), F32)],
        args=(z, z, dmix, da2, da2, dcv, dcv, caw, cbw), parallel=True, comm=comm)[0]


def _tril_ws(ws_ref, g):
    rows = lax.broadcasted_iota(jnp.int32, (CHUNK, CHUNK), 0)
    cols = lax.broadcasted_iota(jnp.int32, (CHUNK, CHUNK), 1)
    return jnp.where(rows >= cols, ws_ref[g], 0.0).astype(BF16), rows >= cols


def _sgu_fwd(pre, lvg, lvb, ws, bs_b, comm=None):
    t = pre.shape[0]
    tt = _tile(t, 256)

    def body(pre_ref, lvg_ref, lvb_ref, ws_ref, bsb_ref, y_ref):
        vhat, _ = _layer_norm_stats(_gelu(pre_ref[:, C_DIM:2 * C_DIM]))
        vl = (vhat * lvg_ref[...] + lvb_ref[...]).astype(BF16)
        for g in range(C_GROUPS):
            w, _ = _tril_ws(ws_ref, g)
            cols = slice(g * CHUNK, (g + 1) * CHUNK)
            for ci in range(tt // CHUNK):
                rows = slice(ci * CHUNK, (ci + 1) * CHUNK)
                sv = jnp.dot(w, vl[rows, cols], preferred_element_type=F32) + bsb_ref[g]
                y_ref[rows, cols] = (_gelu(pre_ref[rows, cols]) * sv).astype(BF16)

    group = pl.BlockSpec((C_GROUPS, CHUNK, CHUNK), lambda i: (0, 0, 0))
    return _call(
        body, name="sgu_fwd", grid=(t // tt,),
        in_specs=[pl.BlockSpec((tt, 2 * C_DIM), lambda i: (i, 0)), _vec_spec(1, C_DIM), _vec_spec(1, C_DIM), group, group],
        out_specs=[pl.BlockSpec((tt, C_DIM), lambda i: (i, 0))], out_shape=[jax.ShapeDtypeStruct((t, C_DIM), BF16)],
        args=(pre, lvg, lvb, ws, bs_b), parallel=True, comm=comm)[0]


def _sgu_bwd(pre, dy, lvg, lvb, ws, bs_b, comm=None):
    t = pre.shape[0]
    tt = _tile(t, 256)

    def body(pre_ref, dy_ref, lvg_ref, lvb_ref, ws_ref, bsb_ref, dpre_ref, dws_ref, dbsb_ref, dlvg_ref, dlvb_ref, dbin_ref,
             dvl_s):
        @pl.when(pl.program_id(0) == 0)
        def _():
            for ref in (dws_ref, dbsb_ref, dlvg_ref, dlvb_ref, dbin_ref):
                ref[...] = jnp.zeros_like(ref)

        v, v_grad = _gelu_and_grad(pre_ref[:, C_DIM:2 * C_DIM])
        vhat, rstd = _layer_norm_stats(v)
        vl = (vhat * lvg_ref[...] + lvb_ref[...]).astype(BF16)
        for g in range(C_GROUPS):
            w, keep = _tril_ws(ws_ref, g)
            cols = slice(g * CHUNK, (g + 1) * CHUNK)
            dws = jnp.zeros((CHUNK, CHUNK), F32)
            dbs = jnp.zeros((CHUNK, 1), F32)
            for ci in range(tt // CHUNK):
                rows = slice(ci * CHUNK, (ci + 1) * CHUNK)
                vl_g = vl[rows, cols]
                sv = jnp.dot(w, vl_g, preferred_element_type=F32) + bsb_ref[g]
                u, u_grad = _gelu_and_grad(pre_ref[rows, cols])
                dyv = dy_ref[rows, cols]
                du = dyv * sv * u_grad
                dpre_ref[rows, cols] = du.astype(BF16)
                dbin_ref[:, cols] += _colsum(du)
                dsv = dyv * u
                dbs = dbs + jnp.sum(dsv, axis=1, keepdims=True)
                dsv16 = dsv.astype(BF16)
                dws = dws + lax.dot_general(dsv16, vl_g, (((1,), (1,)), ((), ())), preferred_element_type=F32)
                dvl_s[rows, cols] = lax.dot_general(w, dsv16, (((0,), (0,)), ((), ())), preferred_element_type=F32)
            dws_ref[g] += jnp.where(keep, dws, 0.0)
            dbsb_ref[g] += dbs
        dvl = dvl_s[...]
        dlvg_ref[...] += _colsum(dvl * vhat)
        dlvb_ref[...] += _colsum(dvl)
        dxh = dvl * lvg_ref[...]
        dv = rstd * (dxh - jnp.mean(dxh, axis=-1, keepdims=True) - vhat * jnp.mean(dxh * vhat, axis=-1, keepdims=True))
        dpv = dv * v_grad
        dpre_ref[:, C_DIM:2 * C_DIM] = dpv.astype(BF16)
        dbin_ref[:, C_DIM:2 * C_DIM] += _colsum(dpv)

    group = pl.BlockSpec((C_GROUPS, CHUNK, CHUNK), lambda i: (0, 0, 0))
    return _call(
        body, name="sgu_bwd", grid=(t // tt,),
        in_specs=[pl.BlockSpec((tt, 2 * C_DIM), lambda i: (i, 0)), pl.BlockSpec((tt, C_DIM), lambda i: (i, 0)),
                  _vec_spec(1, C_DIM), _vec_spec(1, C_DIM), group, group],
        out_specs=[pl.BlockSpec((tt, 2 * C_DIM), lambda i: (i, 0)), group, group,
                   _vec_spec(1, C_DIM), _vec_spec(1, C_DIM), _vec_spec(1, 2 * C_DIM)],
        out_shape=[jax.ShapeDtypeStruct((t, 2 * C_DIM), BF16), jax.ShapeDtypeStruct((C_GROUPS, CHUNK, CHUNK), F32),
                   jax.ShapeDtypeStruct((C_GROUPS, CHUNK, CHUNK), F32), jax.ShapeDtypeStruct((1, C_DIM), F32),
                   jax.ShapeDtypeStruct((1, C_DIM), F32), jax.ShapeDtypeStruct((1, 2 * C_DIM), F32)],
        scratch_shapes=[pltpu.VMEM((tt, C_DIM), F32)],
        args=(pre, dy, lvg, lvb, ws, bs_b), comm=comm)


def _pair_sum(name, part, got, core):
    _, k, n = part.shape
    tk = _tile(k, 1024)

    def body(core_ref, p_ref, s_ref, o_ref):
        o_ref[...] = (p_ref[...] + s_ref[...]).astype(BF16)

    return pl.pallas_call(
        body, name=name,
        grid_spec=pltpu.PrefetchScalarGridSpec(
            num_scalar_prefetch=1, grid=(N_CHIP, k // tk),
            in_specs=[pl.BlockSpec((None, tk, n), lambda q, i, core_ref: (2 * q + core_ref[0], i, 0)),
                      pl.BlockSpec((None, tk, n), lambda q, i, core_ref: (q, i, 0))],
            out_specs=pl.BlockSpec((None, tk, n), lambda q, i, core_ref: (q, i, 0))),
        out_shape=jax.ShapeDtypeStruct((N_CHIP, k, n), BF16),
        compiler_params=pltpu.CompilerParams(dimension_semantics=("parallel", "parallel"), vmem_limit_bytes=VMEM_LIMIT_BYTES),
    )(core, part, got)


def _adamw_math(w, g, m, v):
    m = ADAM_B1 * m + (1.0 - ADAM_B1) * g
    v = ADAM_B2 * v + (1.0 - ADAM_B2) * (g * g)
    m_hat = m / (1.0 - ADAM_B1 ** ADAM_STEP)
    v_hat = v / (1.0 - ADAM_B2 ** ADAM_STEP)
    delta = -ADAM_LR * (m_hat / (jnp.sqrt(v_hat) + ADAM_EPS) + ADAM_WD * w)
    return delta, m, v


def _sum_adamw(name, parts, w, m, v, comm=None):
    layers = len(parts)
    n_parts, k, n = parts[0].shape
    tk = _tile(k, 256)

    def body(*refs):
        p_refs = refs[:layers]
        w_ref, m_ref, v_ref, g_ref, d_ref, nm_ref, nv_ref = refs[layers:]

        def total(p_ref):
            g = p_ref[0].astype(F32)
            for q in range(1, n_parts):
                g = g + p_ref[q].astype(F32)
            return g

        g = total(p_refs[0])
        for l in range(1, layers):
            g = jnp.where(pl.program_id(0) == l, total(p_refs[l]), g)
        g_ref[...] = g
        d_ref[...], nm_ref[...], nv_ref[...] = _adamw_math(w_ref[...], g, m_ref[...], v_ref[...])

    blk = pl.BlockSpec((None, tk, n), lambda l, i: (l, i, 0))
    return _call(
        body, name=name, grid=(layers, k // tk),
        in_specs=[pl.BlockSpec((n_parts, tk, n), lambda l, i: (0, i, 0))] * layers + [blk, blk, blk], out_specs=[blk] * 4,
        out_shape=[jax.ShapeDtypeStruct((layers, k, n), F32)] * 4, args=(*parts, w, m, v), parallel=True, comm=comm)


def _small_adamw(name, parts, w, m, v, losses):
    count = len(parts)

    def in_order(ref):
        total = ref[0]
        for dev in range(1, N_DEV):
            total = total + ref[dev]
        return total

    def body(*refs):
        p_refs, w_refs, m_refs, v_refs = (refs[j * count:(j + 1) * count] for j in range(4))
        losses_ref = refs[4 * count]
        g_refs, d_refs, nm_refs, nv_refs = (refs[4 * count + 1 + j * count:4 * count + 1 + (j + 1) * count] for j in range(4))
        loss_ref = refs[8 * count + 1]
        for i in range(count):
            g = in_order(p_refs[i])
            g_refs[i][...] = g
            d_refs[i][...], nm_refs[i][...], nv_refs[i][...] = _adamw_math(w_refs[i][...], g, m_refs[i][...], v_refs[i][...])
        loss_ref[...] = in_order(losses_ref)

    res = pl.pallas_call(
        body, name=name,
        out_shape=[jax.ShapeDtypeStruct(a.shape, F32) for a in w] * 4 + [jax.ShapeDtypeStruct(losses.shape[1:], F32)],
        compiler_params=pltpu.CompilerParams(vmem_limit_bytes=VMEM_LIMIT_BYTES))(*parts, *w, *m, *v, losses)
    return [res[j * count:(j + 1) * count] for j in range(4)], res[4 * count]


def _rows(a):
    return a.reshape(-1, a.shape[-1])


def _whole(gathered):
    return jnp.transpose(gathered, (1, 0, 2)).reshape(gathered.shape[1], -1)


SMALL =("ev_norm_g", "ev_conv_a_w", "ev_conv_a_b", "ev_ln_a_g", "ev_ln_a_b", "ev_conv_b_w", "od_norm_g", "od_b_in",
         "od_ln_v_g", "od_ln_v_b", "od_w_s", "od_b_s", "mlp_norm_g", "final_norm_g")
SMALL_SHARDED = ("ev_conv_a_w", "ev_conv_b_w", "od_norm_g", "od_b_in", "od_ln_v_g", "od_ln_v_b")
ORDER = ("ev_norm_g", "ev_w_in", "ev_conv_a_w", "ev_conv_a_b", "ev_ln_a_g", "ev_ln_a_b", "ev_conv_b_w", "ev_w_out",
         "od_norm_g", "od_w_in", "od_b_in", "od_ln_v_g", "od_ln_v_b", "od_w_s", "od_b_s", "od_w_out", "mlp_norm_g",
         "mlp_w1", "mlp_w2", "final_norm_g")


def kernel(x, ev_norm_g, ev_w_in, ev_conv_a_w, ev_conv_a_b, ev_ln_a_g, ev_ln_a_b, ev_conv_b_w, ev_w_out, od_norm_g, od_w_in, od_b_in, od_ln_v_g, od_ln_v_b, od_w_s, od_b_s, od_w_out, mlp_norm_g, mlp_w1, mlp_w2, final_norm_g, loss_target, m_ev_norm_g, m_ev_w_in, m_ev_conv_a_w, m_ev_conv_a_b, m_ev_ln_a_g, m_ev_ln_a_b, m_ev_conv_b_w, m_ev_w_out, m_od_norm_g, m_od_w_in, m_od_b_in, m_od_ln_v_g, m_od_ln_v_b, m_od_w_s, m_od_b_s, m_od_w_out, m_mlp_norm_g, m_mlp_w1, m_mlp_w2, m_final_norm_g, v_ev_norm_g, v_ev_w_in, v_ev_conv_a_w, v_ev_conv_a_b, v_ev_ln_a_g, v_ev_ln_a_b, v_ev_conv_b_w, v_ev_w_out, v_od_norm_g, v_od_w_in, v_od_b_in, v_od_ln_v_g, v_od_ln_v_b, v_od_w_s, v_od_b_s, v_od_w_out, v_mlp_norm_g, v_mlp_w1, v_mlp_w2, v_final_norm_g):
    W = dict(ev_norm_g=ev_norm_g, ev_w_in=ev_w_in, ev_conv_a_w=ev_conv_a_w, ev_conv_a_b=ev_conv_a_b, ev_ln_a_g=ev_ln_a_g,
             ev_ln_a_b=ev_ln_a_b, ev_conv_b_w=ev_conv_b_w, ev_w_out=ev_w_out, od_norm_g=od_norm_g, od_w_in=od_w_in,
             od_b_in=od_b_in, od_ln_v_g=od_ln_v_g, od_ln_v_b=od_ln_v_b, od_w_s=od_w_s, od_b_s=od_b_s, od_w_out=od_w_out,
             mlp_norm_g=mlp_norm_g, mlp_w1=mlp_w1, mlp_w2=mlp_w2, final_norm_g=final_norm_g)
    M = dict(ev_norm_g=m_ev_norm_g, ev_w_in=m_ev_w_in, ev_conv_a_w=m_ev_conv_a_w, ev_conv_a_b=m_ev_conv_a_b,
             ev_ln_a_g=m_ev_ln_a_g, ev_ln_a_b=m_ev_ln_a_b, ev_conv_b_w=m_ev_conv_b_w, ev_w_out=m_ev_w_out,
             od_norm_g=m_od_norm_g, od_w_in=m_od_w_in, od_b_in=m_od_b_in, od_ln_v_g=m_od_ln_v_g, od_ln_v_b=m_od_ln_v_b,
             od_w_s=m_od_w_s, od_b_s=m_od_b_s, od_w_out=m_od_w_out, mlp_norm_g=m_mlp_norm_g, mlp_w1=m_mlp_w1,
             mlp_w2=m_mlp_w2, final_norm_g=m_final_norm_g)
    V = dict(ev_norm_g=v_ev_norm_g, ev_w_in=v_ev_w_in, ev_conv_a_w=v_ev_conv_a_w, ev_conv_a_b=v_ev_conv_a_b,
             ev_ln_a_g=v_ev_ln_a_g, ev_ln_a_b=v_ev_ln_a_b, ev_conv_b_w=v_ev_conv_b_w, ev_w_out=v_ev_w_out,
             od_norm_g=v_od_norm_g, od_w_in=v_od_w_in, od_b_in=v_od_b_in, od_ln_v_g=v_od_ln_v_g, od_ln_v_b=v_od_ln_v_b,
             od_w_s=v_od_w_s, od_b_s=v_od_b_s, od_w_out=v_od_w_out, mlp_norm_g=v_mlp_norm_g, mlp_w1=v_mlp_w1,
             mlp_w2=v_mlp_w2, final_norm_g=v_final_norm_g)

    n_seq, seq, d = x.shape
    t = n_seq * seq
    dev = 4 * lax.axis_index("x") + 2 * lax.axis_index("y") + lax.axis_index("c")
    core = lax.axis_index("c").astype(jnp.int32).reshape(1)

    ev_g, cab, lag, lab = W["ev_norm_g"], W["ev_conv_a_b"], W["ev_ln_a_g"], W["ev_ln_a_b"]
    ws = W["od_w_s"][0]
    bs_b = jnp.broadcast_to(W["od_b_s"][0][:, :, None], (C_GROUPS, CHUNK, CHUNK))
    mlp_g = [W["mlp_norm_g"][l:l + 1] for l in range(2)]
    fin_g = W["final_norm_g"].reshape(1, d)

    h0 = x.reshape(t, d)
    for store in (W, M, V):
        store["ev_w_in"] = jnp.transpose(store["ev_w_in"], (0, 2, 1))
    gather = _gather_comm([W["ev_w_in"][0].astype(BF16)])
    later = [(W["ev_w_out"], 0), (W["od_w_in"], 0), (W["od_w_out"], 0), (W["mlp_w1"], 0), (W["mlp_w2"], 0),
             (W["mlp_w1"], 1), (W["mlp_w2"], 1)]
    n0, (w_ev_out16, w_od_in16, w_od_out16, w1_0, w2_0, w1_1, w2_1) = _rms_fwd("ev_norm", h0, ev_g, comm=gather, casts=later)
    w_ev_in_t = gather.out[0].reshape(IN_EVEN, D_MODEL)

    first, second = (0, 2), (1, 2)

    g_a, g_b = _gather_comm([w_ev_out16] + [_rows(W[n]) for n in SMALL_SHARDED]), _gather_comm([w1_0], first)
    z = _mm_nt("ev_in", n0, w_ev_in_t, IN_EVEN, _ep_store, [F32], comm=[g_a, g_b])[0]
    w_ev_out = g_a.out[0].reshape(D_MODEL, D_MODEL)
    caw, cbw, od_g, od_bin, lvg, lvb = [_whole(g) for g in g_a.out[1:]]

    g_c, g_d = _gather_comm([w1_0], second, into=g_b.out), _gather_comm([w2_0], first)
    mix, a2 = _mixer_fwd(z, seq, caw, cab, lag, lab, cbw, comm=[g_c, g_d])
    w1 = [_whole(g_c.out[0]), None]

    g_e = _gather_comm([w_od_in16], first)
    h1, n1 = _mm_nn("ev_out", mix, w_ev_out, d, _ep_residual_norm, [F32, BF16], extras=(h0,), rows=(mlp_g[0],), comm=g_e)

    g_f, g_f2 = _gather_comm([w2_0], second, into=g_d.out), _gather_comm([w_od_in16], second, into=g_e.out)
    q0 = _mm_nn("mlp0_up", n1, w1[0], D_FF, _ep_relu_sq, [BF16], comm=[g_f, g_f2])[0]
    w2 = [g_f.out[0].reshape(D_FF, D_MODEL), None]
    w_od_in = _whole(g_f2.out[0])

    g_g, g_g2 = _gather_comm([w1_1], first), _gather_comm([w_od_out16])
    h2, n2 = _mm_nn("mlp0_down", q0, w2[0], d, _ep_residual_norm, [F32, BF16], extras=(h1,), rows=(od_g,), comm=[g_g, g_g2])
    w_od_out = g_g2.out[0].reshape(D_MODEL, D_MODEL)
    g_h = _gather_comm([w1_1], second, into=g_g.out)
    pre = _mm_nn("od_in", n2, w_od_in, 2 * C_DIM, _ep_bias, [F32], rows=(od_bin,), comm=g_h)[0]
    w1[1] = _whole(g_h.out[0])
    g_i = _gather_comm([w2_1], (0, 4))
    y = _sgu_fwd(pre, lvg, lvb, ws, bs_b, comm=g_i)
    g_i2 = _gather_comm([w2_1], (1, 4), into=g_i.out)
    h3, n3 = _mm_nn("od_out", y, w_od_out, d, _ep_residual_norm, [F32, BF16], extras=(h2,), rows=(mlp_g[1],), comm=g_i2)
    g_j = _gather_comm([w2_1], second, into=g_i2.out)
    q1 = _mm_nn("mlp1_up", n3, w1[1], D_FF, _ep_relu_sq, [BF16], comm=g_j)[0]
    w2[1] = g_j.out[0].reshape(D_FF, D_MODEL)
    grad, grad16, d_fin_g, loss_part = _mm_nn(
        "mlp1_down", q1, w2[1], d, _ep_final_loss, [F32, BF16], extras=(h3, loss_target.reshape(t, d)), rows=(fin_g,),
        sums=(True, False), tm_want=FUSED_ROWS)

    by_chip = {}

    def swap(name, parts):
        comm = _pair_comm([parts])
        comm.parts, comm.weight = parts, name
        return comm

    def exchange(swapped, halves=False):
        sums = _pair_sum(f"pair_sum_{swapped.weight}", swapped.parts, swapped.out[0], core)
        if not halves:
            comm = _chip_comm([sums])
            comm.weight = swapped.weight
            return comm
        comm = _chip_comm([sums], first)
        comm.sums, comm.weight = sums, swapped.weight
        return comm

    def rest(comm):
        other = _chip_comm([comm.sums], second, into=comm.out)
        other.weight = comm.weight
        return other

    def done(comm):
        by_chip[comm.weight] = comm.out[0]

    dw2_1 = _mm_tn("mlp1_dw2", q1, grad16).reshape(N_DEV, D_FF // N_DEV, D_MODEL)
    s_a = swap("w2_1", dw2_1)
    dp = _mm_nt("mlp1_dq", grad16, w2[1], D_FF, _ep_relu_sq_grad, [BF16], extras=(q1,), comm=s_a)[0]
    c_a = exchange(s_a, halves=True)
    dw1_1 = _mm_tn("mlp1_dw1", n3, dp, col_blocks=N_DEV, comm=c_a)
    c_a2, s_b = rest(c_a), swap("w1_1", dw1_1)
    dn = _mm_nt("mlp1_dn", dp, w1[1], d, _ep_store, [BF16], comm=[c_a2, s_b])[0]
    done(c_a2)
    c_b = exchange(s_b, halves=True)
    grad, grad16, dg_mlp1 = _rms_bwd("mlp1_dn_norm", dn, h3, mlp_g[1], grad)
    dy, d_od_out = _mm_out_proj_grads("od_out_grads", grad16, w_od_out, y)
    s_c = swap("od_out", d_od_out.reshape(N_DEV, D_MODEL // N_DEV, D_MODEL))
    dpre, d_ws, d_bsb, d_lvg, d_lvb, d_bin = _sgu_bwd(pre, dy, lvg, lvb, ws, bs_b, comm=[c_b, s_c])
    c_c = exchange(s_c)
    c_b2 = rest(c_b)
    d_od_in = _mm_tn("od_dw_in", n2, dpre, col_blocks=N_DEV, comm=c_b2)
    done(c_b2)
    s_d = swap("od_in", d_od_in)
    dn = _mm_nt("od_dn", dpre, w_od_in, d, _ep_store, [BF16], comm=[s_d, c_c])[0]
    done(c_c)
    c_d = exchange(s_d)
    grad, grad16, d_od_g = _rms_bwd("od_dn_norm", dn, h2, od_g, grad)
    group_1 = dict(od_norm_g=d_od_g, od_b_in=d_bin, od_ln_v_g=d_lvg, od_ln_v_b=d_lvb,
                   od_w_s=d_ws.reshape(C_GROUPS * CHUNK, CHUNK), od_b_s=d_bsb[:, :, 0], final_norm_g=d_fin_g,
                   mlp_norm_g_1=dg_mlp1, loss=jnp.broadcast_to(loss_part, (SUBLANES, LANE_BLOCK)))
    gather_1 = _gather_comm(list(group_1.values()))
    dw2_0 = _mm_tn("mlp0_dw2", q0, grad16, comm=c_d).reshape(N_DEV, D_FF // N_DEV, D_MODEL)
    done(c_d)
    s_e = swap("w2_0", dw2_0)
    dp = _mm_nt("mlp0_dq", grad16, w2[0], D_FF, _ep_relu_sq_grad, [BF16], extras=(q0,), comm=[s_e, gather_1])[0]
    c_e = exchange(s_e, halves=True)
    dw1_0 = _mm_tn("mlp0_dw1", n1, dp, col_blocks=N_DEV, comm=c_e)
    c_e2, s_f = rest(c_e), swap("w1_0", dw1_0)
    dn = _mm_nt("mlp0_dn", dp, w1[0], d, _ep_store, [BF16], comm=[c_e2, s_f])[0]
    done(c_e2)
    c_f = exchange(s_f, halves=True)
    grad, grad16, dg_mlp0 = _rms_bwd("mlp0_dn_norm", dn, h1, mlp_g[0], grad)
    dmix, d_ev_out = _mm_out_proj_grads("ev_out_grads", grad16, w_ev_out, mix)
    s_g = swap("ev_out", d_ev_out.reshape(N_DEV, D_MODEL // N_DEV, D_MODEL))
    da2, dcv, d_caw, d_cab, d_lag, d_lab, d_cbw = _mixer_bwd_local(z, a2, dmix, seq, lag, lab, comm=[c_f, s_g])
    c_g = exchange(s_g)
    c_f2 = rest(c_f)
    dz = _mixer_bwd_input(z, dmix, da2, dcv, seq, caw, cbw, comm=[c_f2, c_g])
    done(c_f2)
    done(c_g)
    group_2 = dict(ev_conv_a_w=d_caw, ev_conv_a_b=d_cab, ev_ln_a_g=d_lag, ev_ln_a_b=d_lab, ev_conv_b_w=d_cbw,
                   mlp_norm_g_0=dg_mlp0)
    gather_2 = _gather_comm(list(group_2.values()))
    d_ev_in_t = _mm_tn("ev_dw_in", dz, n0, comm=gather_2).reshape(N_DEV, IN_EVEN // N_DEV, D_MODEL)
    s_h = swap("ev_in", d_ev_in_t)
    dn = _mm_nn("ev_dn", dz, w_ev_in_t, d, _ep_store, [BF16], comm=s_h)[0]
    c_h = exchange(s_h)
    grad_x, _, d_ev_g = _rms_bwd("ev_dn_norm", dn, h0, ev_g, grad, comm=c_h)
    done(c_h)

    last_gather = _gather_comm([d_ev_g])
    shard = {"od_w_out": ("od_out",), "mlp_w1": ("w1_0", "w1_1"), "mlp_w2": ("w2_0", "w2_1"), "od_w_in": ("od_in",),
             "ev_w_out": ("ev_out",), "ev_w_in": ("ev_in",)}
    carried = {"od_w_out": last_gather}
    out_g, out_d, out_m, out_v = {}, {}, {}, {}
    for name, keys in shard.items():
        out_g[name], out_d[name], out_m[name], out_v[name] = _sum_adamw(
            f"adamw_{name}", [by_chip[key] for key in keys], W[name], M[name], V[name], comm=carried.get(name))
    for store in (out_g, out_d, out_m, out_v):
        store["ev_w_in"] = jnp.transpose(store["ev_w_in"], (0, 2, 1))

    gathered = dict(zip(group_1, gather_1.out), **dict(zip(group_2, gather_2.out)), ev_norm_g=last_gather.out[0])
    gathered["mlp_norm_g"] = jnp.concatenate([gathered["mlp_norm_g_0"], gathered["mlp_norm_g_1"]], axis=1)
    mine = []
    for n in SMALL:
        g = gathered[n]
        if n in SMALL_SHARDED:
            width = W[n].shape[-1]
            g = lax.dynamic_slice_in_dim(g, dev * width, width, axis=2)
        mine.append(g)
    res, loss_tile = _small_adamw("adamw_small", mine, [_rows(W[n]) for n in SMALL], [_rows(M[n]) for n in SMALL],
                                  [_rows(V[n]) for n in SMALL], gathered["loss"])
    loss = loss_tile[0, 0]
    for store, values in zip((out_g, out_d, out_m, out_v), res):
        for n, value in zip(SMALL, values):
            store[n] = value.reshape(W[n].shape)

    return (loss, grad_x.reshape(n_seq, seq, d), *[out_g[n] for n in ORDER], *[out_d[n] for n in ORDER],
            *[out_m[n] for n in ORDER], *[out_v[n] for n in ORDER])
```

```python
import math

import jax
import jax.numpy as jnp
from jax import lax
from jax.experimental import pallas as pl
from jax.experimental.pallas import tpu as pltpu

F32 = jnp.float32
BF16 = jnp.bfloat16
MESH = pl.DeviceIdType.MESH

D_MODEL = 1024
A_DIM = 512
B_DIM = 512
IN_EVEN = 2 * A_DIM + 3 * B_DIM
A_TAPS = 31
B_TAPS = 3
CHUNK = 128
C_GROUPS = 8
C_DIM = 1024
D_FF = 4096
RMS_EPS = 1e-6
LN_EPS = 1e-5
N_DEV = 8
N_CHIP = 4

ADAM_LR = 0.001
ADAM_B1 = 0.9
ADAM_B2 = 0.999
ADAM_EPS = 1e-08
ADAM_WD = 0.01
ADAM_STEP = 10

A_HALO = 32
B_HALO = 8
VMEM_LIMIT_BYTES = 56 * 1024 * 1024
TINY = 1.1754944e-38
INV_SQRT2 = 1.0 / math.sqrt(2.0)
INV_SQRT_2PI = 1.0 / math.sqrt(2.0 * math.pi)
HBM_SPEC = pl.BlockSpec(memory_space=pltpu.HBM)


def _tile(n, want):
    t = min(n, want)
    while n % t:
        t //= 2
    return t


def _sigmoid(x):
    return 1.0 / (1.0 + jnp.exp(-x))


def _gelu(x):
    return 0.5 * x * (1.0 + lax.erf(x * INV_SQRT2))


def _gelu_and_grad(x):
    cdf = 0.5 * (1.0 + lax.erf(x * INV_SQRT2))
    return x * cdf, cdf + x * jnp.exp(-0.5 * x * x) * INV_SQRT_2PI


def _colsum(x):
    return jnp.sum(x, axis=0, keepdims=True)


class _Comm:
    def __init__(self, ins, out_shapes, sem_shapes, start, finish, middle=None, into=None, sibling=False, chips=False):
        self.ins, self.out_shapes, self.sem_shapes, self.start, self.finish = ins, out_shapes, sem_shapes, start, finish
        self.middle = middle
        self.sibling, self.chips = sibling, chips
        self.into = list(into) if into is not None else []
        self.out = None


def _piece_rows(rows, piece):
    if piece is None:
        return 0, rows
    i, n, count = (*piece, 1)[:3]
    return i * (rows // n), count * (rows // n)


MIDDLE_AT = 0.75
BARRIER_IDS = {(True, True): 0, (True, False): 1, (False, True): 2}
FUSED_ROWS = 512


def _call(body, *, name, grid, in_specs, out_specs, out_shape, args, scratch_shapes=(), parallel=False, comm=None):
    comms = [] if comm is None else (list(comm) if isinstance(comm, (list, tuple)) else [comm])
    if not comms:
        sem = ("parallel" if parallel else "arbitrary",) * len(grid)
        return pl.pallas_call(
            body, name=name, grid=grid, in_specs=list(in_specs), out_specs=list(out_specs), out_shape=list(out_shape),
            scratch_shapes=list(scratch_shapes),
            compiler_params=pltpu.CompilerParams(dimension_semantics=sem, vmem_limit_bytes=VMEM_LIMIT_BYTES),
        )(*args)
    n_in, n_out, n_scr = len(in_specs), len(out_shape), len(scratch_shapes)
    c_ins_all = [a for cm in comms for a in cm.ins]
    c_into_all = [a for cm in comms for a in cm.into]
    c_out_shapes = [s for cm in comms for s in cm.out_shapes]
    c_sem_shapes = [s for cm in comms for s in cm.sem_shapes]
    aliases, in_pos, out_pos = {}, n_in + len(c_ins_all), n_out
    for cm in comms:
        for j in range(len(cm.into)):
            aliases[in_pos + j] = out_pos + j
        in_pos += len(cm.into)
        out_pos += len(cm.out_shapes)
    to_sibling = any(cm.sibling for cm in comms)
    to_chips = any(cm.chips for cm in comms)
    steps = grid
    total = math.prod(steps)
    first_step = (0,) * len(steps)
    last_step = tuple(s - 1 for s in steps)
    middle_step = None
    if 0 < int(MIDDLE_AT * total) < total - 1:
        rest, idx = int(MIDDLE_AT * total), []
        for s in reversed(steps):
            idx.append(rest % s)
            rest //= s
        middle_step = tuple(reversed(idx))

    def carrying(*refs):
        pos = 0
        ins = refs[pos:pos + n_in]; pos += n_in
        c_ins = refs[pos:pos + len(c_ins_all)]; pos += len(c_ins_all) + len(c_into_all)
        outs = refs[pos:pos + n_out]; pos += n_out
        c_outs = refs[pos:pos + len(c_out_shapes)]; pos += len(c_out_shapes)
        scr = refs[pos:pos + n_scr]; pos += n_scr
        c_sems = refs[pos:]
        views, i0, o0, s0 = [], 0, 0, 0
        for cm in comms:
            views.append((c_ins[i0:i0 + len(cm.ins)], c_outs[o0:o0 + len(cm.out_shapes)], c_sems[s0:s0 + len(cm.sem_shapes)]))
            i0, o0, s0 = i0 + len(cm.ins), o0 + len(cm.out_shapes), s0 + len(cm.sem_shapes)

        def at(step):
            hit = pl.program_id(0) == step[0]
            for axis in range(1, len(steps)):
                hit = jnp.logical_and(hit, pl.program_id(axis) == step[axis])
            return hit

        @pl.when(at(first_step))
        def _():
            x, y, c, chips = _place()
            peers = ([(x, y, 1 - c)] if to_sibling else []) + ([(*chip, c) for chip in chips] if to_chips else [])
            barrier = pltpu.get_barrier_semaphore()
            for peer in peers:
                pl.semaphore_signal(barrier, inc=1, device_id=peer, device_id_type=MESH)
            pl.semaphore_wait(barrier, len(peers))
            for cm, view in zip(comms, views):
                cm.start(*view)

        if middle_step is not None:
            @pl.when(at(middle_step))
            def _():
                for cm, view in zip(comms, views):
                    if cm.middle is not None:
                        cm.middle(*view)

        body(*ins, *outs, *scr)

        @pl.when(at(last_step))
        def _():
            for cm, view in zip(comms, views):
                if cm.middle is not None and middle_step is None:
                    cm.middle(*view)
            for cm, view in zip(comms, views):
                cm.finish(*view)

    res = pl.pallas_call(
        carrying, name=name, grid=grid,
        in_specs=[*in_specs, *[HBM_SPEC] * (len(c_ins_all) + len(c_into_all))],
        out_specs=[*out_specs, *[HBM_SPEC] * len(c_out_shapes)],
        out_shape=[*out_shape, *c_out_shapes], scratch_shapes=[*scratch_shapes, *c_sem_shapes],
        input_output_aliases=aliases,
        compiler_params=pltpu.CompilerParams(dimension_semantics=("arbitrary",) * len(grid), vmem_limit_bytes=VMEM_LIMIT_BYTES,
                                             collective_id=BARRIER_IDS[to_sibling, to_chips]),
    )(*args, *c_ins_all, *c_into_all)
    pos = n_out
    for cm in comms:
        cm.out = list(res[pos:pos + len(cm.out_shapes)])
        pos += len(cm.out_shapes)
    return list(res[:n_out])


def _place():
    x, y, c = lax.axis_index("x"), lax.axis_index("y"), lax.axis_index("c")
    return x, y, c, [(1 - x, y), (x, 1 - y), (1 - x, 1 - y)]


def _gather_comm(shards, piece=None, into=None):
    nw = len(shards)
    spans = [_piece_rows(s.shape[0], piece) for s in shards]

    def plan(ins, outs, sems):
        send_sems, recv_sems, local_sems = sems
        x, y, c, chips = _place()
        me, sibling = (x, y, c), (x, y, 1 - c)

        def slot(w, p):
            return outs[w].at[4 * p[0] + 2 * p[1] + p[2], pl.ds(*spans[w])]

        def mine(w):
            return ins[w].at[pl.ds(*spans[w])]

        def copy(w, k, block, to, src=None):
            return pltpu.make_async_remote_copy(
                src_ref=slot(w, block) if src is None else src, dst_ref=slot(w, block),
                send_sem=send_sems.at[w, k], recv_sem=recv_sems.at[w, k], device_id=to, device_id_type=MESH)

        local = [pltpu.make_async_copy(mine(w), slot(w, me), local_sems.at[w]) for w in range(nw)]
        first = [[copy(w, 0, me, sibling, src=mine(w))] + [copy(w, 1 + j, me, (*chip, c), src=mine(w)) for j, chip in enumerate(chips)]
                 for w in range(nw)]
        landed = [[copy(w, 1 + j, (*chip, c), me) for j, chip in enumerate(chips)] for w in range(nw)]
        passed = [[copy(w, 4 + j, (*chip, c), sibling) for j, chip in enumerate(chips)] for w in range(nw)]
        from_sibling = [[copy(w, 0, sibling, me)] + [copy(w, 4 + j, (*chip, 1 - c), me) for j, chip in enumerate(chips)]
                        for w in range(nw)]
        return local, first, landed, passed, from_sibling

    def start(ins, outs, sems):
        local, first, _, _, _ = plan(ins, outs, sems)
        for cp in local:
            cp.start()
        for row in first:
            for cp in row:
                cp.start()

    def middle(ins, outs, sems):
        _, _, landed, passed, _ = plan(ins, outs, sems)
        for w in range(nw):
            for j in range(3):
                landed[w][j].wait_recv()
                passed[w][j].start()

    def finish(ins, outs, sems):
        local, first, landed, passed, from_sibling = plan(ins, outs, sems)
        for w in range(nw):
            for cp in from_sibling[w]:
                cp.wait_recv()
        for w in range(nw):
            for cp in first[w] + passed[w]:
                cp.wait_send()
        for cp in local:
            cp.wait()

    return _Comm(list(shards), [jax.ShapeDtypeStruct((N_DEV, *s.shape), s.dtype) for s in shards],
                 [pltpu.SemaphoreType.DMA((nw, 7)), pltpu.SemaphoreType.DMA((nw, 7)), pltpu.SemaphoreType.DMA((nw,))],
                 start, finish, middle, into=into, sibling=True, chips=True)


def _pair_comm(parts):
    nw = len(parts)

    def plan(ins, outs, sems):
        send_sems, recv_sems = sems
        x, y, c, _ = _place()
        return [pltpu.make_async_remote_copy(
            src_ref=ins[w].at[2 * q + 1 - c], dst_ref=outs[w].at[q], send_sem=send_sems.at[w, q], recv_sem=recv_sems.at[w, q],
            device_id=(x, y, 1 - c), device_id_type=MESH) for w in range(nw) for q in range(N_CHIP)]

    def start(ins, outs, sems):
        for cp in plan(ins, outs, sems):
            cp.start()

    def finish(ins, outs, sems):
        for cp in plan(ins, outs, sems):
            cp.wait()

    return _Comm(list(parts), [jax.ShapeDtypeStruct((N_CHIP, *p.shape[1:]), p.dtype) for p in parts],
                 [pltpu.SemaphoreType.DMA((nw, N_CHIP)), pltpu.SemaphoreType.DMA((nw, N_CHIP))], start, finish, sibling=True)


def _chip_comm(sums, piece=None, into=None):
    nw = len(sums)
    spans = [_piece_rows(s.shape[1], piece) for s in sums]

    def plan(ins, outs, sems):
        send_sems, recv_sems, local_sems = sems
        x, y, c, chips = _place()
        my_chip = 2 * x + y
        local = [pltpu.make_async_copy(ins[w].at[my_chip, pl.ds(*spans[w])], outs[w].at[my_chip, pl.ds(*spans[w])],
                                       local_sems.at[w]) for w in range(nw)]
        remote = [pltpu.make_async_remote_copy(
            src_ref=ins[w].at[2 * chip[0] + chip[1], pl.ds(*spans[w])], dst_ref=outs[w].at[my_chip, pl.ds(*spans[w])],
            send_sem=send_sems.at[w, j], recv_sem=recv_sems.at[w, j], device_id=(*chip, c), device_id_type=MESH)
            for w in range(nw) for j, chip in enumerate(chips)]
        return local, remote

    def start(ins, outs, sems):
        local, remote = plan(ins, outs, sems)
        for cp in local + remote:
            cp.start()

    def finish(ins, outs, sems):
        local, remote = plan(ins, outs, sems)
        for cp in remote + local:
            cp.wait()

    return _Comm(list(sums), [jax.ShapeDtypeStruct(s.shape, s.dtype) for s in sums],
                 [pltpu.SemaphoreType.DMA((nw, 3)), pltpu.SemaphoreType.DMA((nw, 3)), pltpu.SemaphoreType.DMA((nw,))],
                 start, finish, into=into, chips=True)


def _matmul(name, a, b, *, kind, m, n, k, a_spec, b_spec, tm, tn, tk, out_shape, out_specs, epilogue,
            extras=(), extra_specs=(), comm=None):
    assert tk == k, "the whole contraction is one grid step"
    dims = {"nn": (((1,), (0,)), ((), ())), "nt": (((1,), (1,)), ((), ())), "tn": (((0,), (0,)), ((), ()))}[kind]
    n_extra = len(extras)
    n_out = len(out_shape)

    def body(a_ref, b_ref, *rest):
        extra_refs = rest[:n_extra]
        out_refs = rest[n_extra:n_extra + n_out]
        epilogue(lax.dot_general(a_ref[...], b_ref[...], dims, preferred_element_type=F32), extra_refs, out_refs)

    return _call(
        body, name=name, grid=(m // tm, n // tn, 1), in_specs=[a_spec, b_spec, *extra_specs], out_specs=out_specs,
        out_shape=out_shape, args=(a, b, *extras), comm=comm)


def _mm_operands(m, n, tm, tn, out_dtypes, extras, rows, sums):
    tile = pl.BlockSpec((tm, tn), lambda i, j, kk: (i, j))
    row = pl.BlockSpec((1, tn), lambda i, j, kk: (0, j))
    one = pl.BlockSpec((1, 1), lambda i, j, kk: (0, 0))
    out_shape = [jax.ShapeDtypeStruct((m, n), dt) for dt in out_dtypes]
    out_shape += [jax.ShapeDtypeStruct((1, n if wide else 1), F32) for wide in sums]
    out_specs = [tile] * len(out_dtypes) + [row if wide else one for wide in sums]
    return (*extras, *rows), [tile] * len(extras) + [row] * len(rows), out_shape, out_specs


def _row_tile(m, k, tm_want):
    return _tile(m, tm_want or (1024 if k <= 1024 else 512))


def _mm_nn(name, a, b, n, epilogue, out_dtypes, *, extras=(), rows=(), sums=(), tm_want=None, comm=None):
    m, k = a.shape
    tm = _row_tile(m, k, tm_want)
    tn = _tile(n, 1024)
    extras, extra_specs, out_shape, out_specs = _mm_operands(m, n, tm, tn, out_dtypes, extras, rows, sums)
    return _matmul(
        name, a, b, kind="nn", m=m, n=n, k=k, tm=tm, tn=tn, tk=k,
        a_spec=pl.BlockSpec((tm, k), lambda i, j, kk: (i, kk)), b_spec=pl.BlockSpec((k, tn), lambda i, j, kk: (kk, j)),
        out_shape=out_shape, out_specs=out_specs, epilogue=epilogue, extras=extras, extra_specs=extra_specs, comm=comm)


def _mm_nt(name, a, b, n, epilogue, out_dtypes, *, extras=(), rows=(), sums=(), tm_want=None, comm=None):
    m, k = a.shape
    tm = _row_tile(m, k, tm_want)
    tn = _tile(n, 1024)
    extras, extra_specs, out_shape, out_specs = _mm_operands(m, n, tm, tn, out_dtypes, extras, rows, sums)
    return _matmul(
        name, a, b, kind="nt", m=m, n=n, k=k, tm=tm, tn=tn, tk=k,
        a_spec=pl.BlockSpec((tm, k), lambda i, j, kk: (i, kk)), b_spec=pl.BlockSpec((tn, k), lambda i, j, kk: (j, kk)),
        out_shape=out_shape, out_specs=out_specs, epilogue=epilogue, extras=extras, extra_specs=extra_specs, comm=comm)


def _mm_out_proj_grads(name, grad16, w, act):
    t, n = grad16.shape
    k = w.shape[0]
    tm = _tile(t, 1024)

    def body(g_ref, w_ref, a_ref, din_ref, dw_ref):
        g = g_ref[...]
        din_ref[...] = lax.dot_general(g, w_ref[...], (((1,), (1,)), ((), ())), preferred_element_type=F32)
        part = lax.dot_general(a_ref[...], g, (((0,), (0,)), ((), ())), preferred_element_type=F32)

        @pl.when(pl.program_id(0) == 0)
        def _():
            dw_ref[...] = part

        @pl.when(pl.program_id(0) > 0)
        def _():
            dw_ref[...] += part

    return _call(
        body, name=name, grid=(t // tm,),
        in_specs=[pl.BlockSpec((tm, n), lambda i: (i, 0)), pl.BlockSpec((k, n), lambda i: (0, 0)),
                  pl.BlockSpec((tm, k), lambda i: (i, 0))],
        out_specs=[pl.BlockSpec((tm, k), lambda i: (i, 0)), pl.BlockSpec((k, n), lambda i: (0, 0))],
        out_shape=[jax.ShapeDtypeStruct((t, k), F32), jax.ShapeDtypeStruct((k, n), F32)], args=(grad16, w, act))


def _mm_tn(name, a, b, *, col_blocks=0, comm=None):
    t, k = a.shape
    n = b.shape[1]
    tm = _row_tile(k, t, None)
    tn = n // col_blocks if col_blocks else _tile(n, 1024)
    tk = t
    if col_blocks:
        out_shape = [jax.ShapeDtypeStruct((col_blocks, k, tn), F32)]
        out_specs = [pl.BlockSpec((None, tm, tn), lambda i, j, kk: (j, i, 0))]
    else:
        out_shape = [jax.ShapeDtypeStruct((k, n), F32)]
        out_specs = [pl.BlockSpec((tm, tn), lambda i, j, kk: (i, j))]

    def epilogue(acc, extra_refs, out_refs):
        out_refs[0][...] = acc

    return _matmul(
        name, a, b, kind="tn", m=k, n=n, k=t, tm=tm, tn=tn, tk=tk,
        a_spec=pl.BlockSpec((tk, tm), lambda i, j, kk: (kk, i)), b_spec=pl.BlockSpec((tk, tn), lambda i, j, kk: (kk, j)),
        out_shape=out_shape, out_specs=out_specs, epilogue=epilogue, comm=comm)[0]


def _ep_store(acc, extra_refs, out_refs):
    out_refs[0][...] = acc.astype(out_refs[0].dtype)


def _ep_bias(acc, extra_refs, out_refs):
    out_refs[0][...] = acc + extra_refs[0][...]


def _rms(x):
    r = lax.rsqrt(jnp.mean(x * x, axis=-1, keepdims=True) + RMS_EPS)
    return r, x * r


def _rms_grad(dn, r, xr, g):
    dy = dn * g
    return r * (dy - xr * jnp.mean(dy * xr, axis=-1, keepdims=True))


def _ep_residual_norm(acc, extra_refs, out_refs):
    h = extra_refs[0][...] + acc
    out_refs[0][...] = h
    _, xr = _rms(h)
    out_refs[1][...] = (xr * extra_refs[1][...]).astype(BF16)


def _ep_final_loss(acc, extra_refs, out_refs):
    @pl.when(pl.program_id(0) == 0)
    def _():
        out_refs[2][...] = jnp.zeros_like(out_refs[2])
        out_refs[3][...] = jnp.zeros_like(out_refs[3])

    h = extra_refs[0][...] + acc
    g = extra_refs[2][...]
    r, xr = _rms(h)
    err = xr * g - extra_refs[1][...]
    out_refs[3][...] += 0.5 * jnp.sum(jnp.mean(err * err, axis=-1, keepdims=True), axis=0, keepdims=True)
    dout = err * (1.0 / h.shape[-1])
    out_refs[2][...] += _colsum(dout * xr)
    out = _rms_grad(dout, r, xr, g)
    out_refs[0][...] = out
    out_refs[1][...] = out.astype(BF16)


def _ep_relu_sq(acc, extra_refs, out_refs):
    r = jnp.maximum(acc, 0.0)
    out_refs[0][...] = (r * r).astype(BF16)


def _ep_relu_sq_grad(acc, extra_refs, out_refs):
    q = extra_refs[0][...].astype(F32)
    out_refs[0][...] = (acc * (2.0 * q * lax.rsqrt(jnp.maximum(q, TINY)))).astype(BF16)


def _rms_fwd(name, h, g, comm=None, casts=()):
    t, d = h.shape
    tt = _tile(t, 512)
    steps = t // tt
    n_cast = len(casts)

    def body(h_ref, g_ref, *rest):
        n_ref = rest[n_cast]
        x = h_ref[...]
        r = lax.rsqrt(jnp.mean(x * x, axis=-1, keepdims=True) + RMS_EPS)
        n_ref[...] = (x * r * g_ref[...]).astype(BF16)
        for src, dst in zip(rest[:n_cast], rest[n_cast + 1:]):
            dst[...] = src[...].astype(BF16)

    cast_in = [pl.BlockSpec((None, w.shape[1] // steps, w.shape[2]), lambda i, l=l: (l, i, 0)) for w, l in casts]
    cast_out = [pl.BlockSpec((w.shape[1] // steps, w.shape[2]), lambda i: (i, 0)) for w, _ in casts]
    res = _call(
        body, name=name, grid=(steps,),
        in_specs=[pl.BlockSpec((tt, d), lambda i: (i, 0)), pl.BlockSpec((1, d), lambda i: (0, 0)), *cast_in],
        out_specs=[pl.BlockSpec((tt, d), lambda i: (i, 0)), *cast_out],
        out_shape=[jax.ShapeDtypeStruct((t, d), BF16)] + [jax.ShapeDtypeStruct(w.shape[1:], BF16) for w, _ in casts],
        args=(h, g, *[w for w, _ in casts]), parallel=True, comm=comm)
    return res[0], res[1:]


def _rms_bwd(name, dn, h, g, grad_in, comm=None):
    t, d = h.shape
    tt = _tile(t, 512)

    def body(dn_ref, h_ref, g_ref, gin_ref, gout_ref, gout16_ref, dg_ref):
        @pl.when(pl.program_id(0) == 0)
        def _():
            dg_ref[...] = jnp.zeros_like(dg_ref)

        dnv = dn_ref[...].astype(F32)
        r, xr = _rms(h_ref[...])
        dg_ref[...] += _colsum(dnv * xr)
        out = gin_ref[...] + _rms_grad(dnv, r, xr, g_ref[...])
        gout_ref[...] = out
        gout16_ref[...] = out.astype(BF16)

    row = pl.BlockSpec((tt, d), lambda i: (i, 0))
    vec = pl.BlockSpec((1, d), lambda i: (0, 0))
    return _call(
        body, name=name, grid=(t // tt,), in_specs=[row, row, vec, row], out_specs=[row, row, vec],
        out_shape=[jax.ShapeDtypeStruct((t, d), F32), jax.ShapeDtypeStruct((t, d), BF16), jax.ShapeDtypeStruct((1, d), F32)],
        args=(dn, h, g, grad_in), comm=comm)


def _mixer_windows(z_ref, zh_ref, first, a1_s, cb_s, tt):
    sig = _sigmoid(z_ref[:, A_DIM:2 * A_DIM])
    if a1_s is not None:
        a1_s[A_HALO:A_HALO + tt, :] = z_ref[:, 0:A_DIM] * sig
        a1_h = zh_ref[:, 0:A_DIM] * _sigmoid(zh_ref[:, A_DIM:2 * A_DIM])
        a1_s[0:A_HALO, :] = jnp.where(first, 0.0, a1_h)
    cb_s[B_HALO:B_HALO + tt, :] = z_ref[:, 3 * A_DIM:4 * A_DIM] * z_ref[:, 4 * A_DIM:5 * A_DIM]
    cb_h = zh_ref[A_HALO - B_HALO:A_HALO, 3 * A_DIM:4 * A_DIM] * zh_ref[A_HALO - B_HALO:A_HALO, 4 * A_DIM:5 * A_DIM]
    cb_s[0:B_HALO, :] = jnp.where(first, 0.0, cb_h)
    return sig


def _causal_conv(win_s, w_ref, taps, halo, tt):
    base = halo - (taps - 1)
    acc = w_ref[0:1, :] * win_s[pl.ds(base, tt), :]
    for k in range(1, taps):
        acc = acc + w_ref[k:k + 1, :] * win_s[pl.ds(base + k, tt), :]
    return acc


SUBLANES = 8
LANE_BLOCK = 128
CONV_ROWS = 32
TAP_GRAD_ROWS = 64
NORM_ROWS = 32
SHIFT_ROWS = A_HALO - SUBLANES


def _shifted_copies(win_s, sh_s, tt):
    for b in range(1, SUBLANES):
        sh_s[b - 1] = win_s[pl.ds(b, tt + SHIFT_ROWS), :]


def _window_rows(win_s, sh_s, offset, rows, cols):
    b = offset % SUBLANES
    if b == 0:
        return win_s[pl.ds(offset, rows), cols]
    return sh_s[b - 1, pl.ds(offset - b, rows), cols]


def _blocks(tt, rows):
    rb = min(tt, rows)
    return rb, [(r, slice(lb * LANE_BLOCK, (lb + 1) * LANE_BLOCK))
                for lb in range(A_DIM // LANE_BLOCK) for r in range(0, tt, rb)]


def _conv_taps(win_s, sh_s, w_ref, offsets, out_s, tt, bias_ref=None):
    rb, blocks = _blocks(tt, CONV_ROWS)
    for r, cols in blocks:
        acc = w_ref[0:1, cols] * _window_rows(win_s, sh_s, r + offsets[0], rb, cols)
        for k in range(1, len(offsets)):
            acc = acc + w_ref[k:k + 1, cols] * _window_rows(win_s, sh_s, r + offsets[k], rb, cols)
        out_s[r:r + rb, cols] = acc if bias_ref is None else acc + bias_ref[:, cols]


A_CAUSAL = [A_HALO - (A_TAPS - 1) + k for k in range(A_TAPS)]
A_ANTICAUSAL = [A_TAPS - 1 - k for k in range(A_TAPS)]


def _layer_norm_stats(x):
    mu = jnp.mean(x, axis=-1, keepdims=True)
    xc = x - mu
    rstd = lax.rsqrt(jnp.mean(xc * xc, axis=-1, keepdims=True) + LN_EPS)
    return xc * rstd, rstd


def _mixer_specs(seq, tt):
    tiles_per_seq = seq // tt
    halo_blocks = tt // A_HALO
    z_spec = pl.BlockSpec((tt, IN_EVEN), lambda i: (i, 0))
    zh_spec = pl.BlockSpec((A_HALO, IN_EVEN), lambda i: (jnp.maximum(i * halo_blocks - 1, 0), 0))
    return tiles_per_seq, z_spec, zh_spec


def _vec_spec(rows, cols):
    return pl.BlockSpec((rows, cols), lambda i: (0, 0))


def _mixer_fwd(z, seq, caw, cab, lag, lab, cbw, comm=None):
    t = z.shape[0]
    tt = _tile(seq, 256)
    tiles_per_seq, z_spec, zh_spec = _mixer_specs(seq, tt)

    def body(z_ref, zh_ref, caw_ref, cab_ref, lag_ref, lab_ref, cbw_ref, mix_ref, a2_ref, a1_s, cb_s, sh_s):
        first = (pl.program_id(0) % tiles_per_seq) == 0
        _mixer_windows(z_ref, zh_ref, first, a1_s, cb_s, tt)
        _shifted_copies(a1_s, sh_s, tt)
        _conv_taps(a1_s, sh_s, caw_ref, A_CAUSAL, a2_ref, tt, bias_ref=cab_ref)
        for r in range(0, tt, min(tt, NORM_ROWS)):
            rows = slice(r, r + min(tt, NORM_ROWS))
            xhat, _ = _layer_norm_stats(a2_ref[rows, :])
            a3 = xhat * lag_ref[...] + lab_ref[...]
            mix_ref[rows, 0:A_DIM] = (a3 * _sigmoid(a3)).astype(BF16)
        cv = _causal_conv(cb_s, cbw_ref, B_TAPS, B_HALO, tt)
        mix_ref[:, A_DIM:A_DIM + B_DIM] = (z_ref[:, 2 * A_DIM:3 * A_DIM] * cv).astype(BF16)

    return _call(
        body, name="mixer_fwd", grid=(t // tt,),
        in_specs=[z_spec, zh_spec, _vec_spec(A_TAPS, A_DIM), _vec_spec(1, A_DIM), _vec_spec(1, A_DIM), _vec_spec(1, A_DIM),
                  _vec_spec(B_TAPS, B_DIM)],
        out_specs=[pl.BlockSpec((tt, A_DIM + B_DIM), lambda i: (i, 0)), pl.BlockSpec((tt, A_DIM), lambda i: (i, 0))],
        out_shape=[jax.ShapeDtypeStruct((t, A_DIM + B_DIM), BF16), jax.ShapeDtypeStruct((t, A_DIM), F32)],
        scratch_shapes=[pltpu.VMEM((A_HALO + tt, A_DIM), F32), pltpu.VMEM((B_HALO + tt, B_DIM), F32),
                        pltpu.VMEM((SUBLANES - 1, tt + SHIFT_ROWS, A_DIM), F32)],
        args=(z, z, caw, cab, lag, lab, cbw), parallel=True, comm=comm)


def _mixer_bwd_local(z, a2, dmix, seq, lag, lab, comm=None):
    t = z.shape[0]
    tt = _tile(seq, 256)
    tiles_per_seq, z_spec, zh_spec = _mixer_specs(seq, tt)

    def body(z_ref, zh_ref, a2_ref, dmix_ref, lag_ref, lab_ref,
             da2_ref, dcv_ref, dcaw_ref, dcab_ref, dlag_ref, dlab_ref, dcbw_ref, a1_s, cb_s, sh_s):
        @pl.when(pl.program_id(0) == 0)
        def _():
            for ref in (dcaw_ref, dcab_ref, dlag_ref, dlab_ref, dcbw_ref):
                ref[...] = jnp.zeros_like(ref)

        first = (pl.program_id(0) % tiles_per_seq) == 0
        _mixer_windows(z_ref, zh_ref, first, a1_s, cb_s, tt)
        _shifted_copies(a1_s, sh_s, tt)
        for r in range(0, tt, min(tt, NORM_ROWS)):
            rows = slice(r, r + min(tt, NORM_ROWS))
            xhat, rstd = _layer_norm_stats(a2_ref[rows, :])
            a3 = xhat * lag_ref[...] + lab_ref[...]
            s3 = _sigmoid(a3)
            da3 = dmix_ref[rows, 0:A_DIM] * (s3 * (1.0 + a3 * (1.0 - s3)))
            dlag_ref[...] += _colsum(da3 * xhat)
            dlab_ref[...] += _colsum(da3)
            dxh = da3 * lag_ref[...]
            da2 = rstd * (dxh - jnp.mean(dxh, axis=-1, keepdims=True) - xhat * jnp.mean(dxh * xhat, axis=-1, keepdims=True))
            da2_ref[rows, :] = da2
            dcab_ref[...] += _colsum(da2)
        rb, blocks = _blocks(tt, TAP_GRAD_ROWS)
        for r, cols in blocks:
            da2_b = da2_ref[r:r + rb, cols]
            for k in range(A_TAPS):
                dcaw_ref[k:k + 1, cols] += _colsum(da2_b * _window_rows(a1_s, sh_s, r + A_CAUSAL[k], rb, cols))
        dcv =dmix_ref[:, A_DIM:A_DIM + B_DIM] * z_ref[:, 2 * A_DIM:3 * A_DIM]
        dcv_ref[...] = dcv
        for k in range(B_TAPS):
            dcbw_ref[k:k + 1, :] += _colsum(dcv * cb_s[pl.ds(B_HALO - (B_TAPS - 1) + k, tt), :])

    half = pl.BlockSpec((tt, A_DIM), lambda i: (i, 0))
    return _call(
        body, name="mixer_bwd_local", grid=(t // tt,),
        in_specs=[z_spec, zh_spec, half, pl.BlockSpec((tt, A_DIM + B_DIM), lambda i: (i, 0)),
                  _vec_spec(1, A_DIM), _vec_spec(1, A_DIM)],
        out_specs=[half, half, _vec_spec(A_TAPS, A_DIM), _vec_spec(1, A_DIM), _vec_spec(1, A_DIM), _vec_spec(1, A_DIM),
                   _vec_spec(B_TAPS, B_DIM)],
        out_shape=[jax.ShapeDtypeStruct((t, A_DIM), F32), jax.ShapeDtypeStruct((t, B_DIM), F32),
                   jax.ShapeDtypeStruct((A_TAPS, A_DIM), F32), jax.ShapeDtypeStruct((1, A_DIM), F32),
                   jax.ShapeDtypeStruct((1, A_DIM), F32), jax.ShapeDtypeStruct((1, A_DIM), F32),
                   jax.ShapeDtypeStruct((B_TAPS, B_DIM), F32)],
        scratch_shapes=[pltpu.VMEM((A_HALO + tt, A_DIM), F32), pltpu.VMEM((B_HALO + tt, B_DIM), F32),
                        pltpu.VMEM((SUBLANES - 1, tt + SHIFT_ROWS, A_DIM), F32)],
        args=(z, z, a2, dmix, lag, lab), comm=comm)


def _mixer_bwd_input(z, dmix, da2, dcv, seq, caw, cbw, comm=None):
    t = z.shape[0]
    tt = _tile(seq, 256)
    tiles_per_seq, z_spec, zh_spec = _mixer_specs(seq, tt)
    a_blocks = tt // A_HALO
    b_blocks = tt // B_HALO
    last_a = t // A_HALO - 1
    last_b = t // B_HALO - 1

    def body(z_ref, zh_ref, dmix_ref, da2_ref, da2n_ref, dcv_ref, dcvn_ref, caw_ref, cbw_ref, dz_ref, cb_s, da2_s, dcv_s,
             sh_s, da1_s):
        pos = pl.program_id(0) % tiles_per_seq
        first = pos == 0
        last = pos == tiles_per_seq - 1
        sig = _mixer_windows(z_ref, zh_ref, first, None, cb_s, tt)
        da2_s[0:tt, :] = da2_ref[...]
        da2_s[tt:tt + A_HALO, :] = jnp.where(last, 0.0, da2n_ref[...])
        dcv_s[0:tt, :] = dcv_ref[...]
        dcv_s[tt:tt + B_HALO, :] = jnp.where(last, 0.0, dcvn_ref[...])
        _shifted_copies(da2_s, sh_s, tt)
        _conv_taps(da2_s, sh_s, caw_ref, A_ANTICAUSAL, da1_s, tt)
        da1 = da1_s[...]
        dz_ref[:, 0:A_DIM] = (da1 * sig).astype(BF16)
        dz_ref[:, A_DIM:2 * A_DIM] = (da1 * z_ref[:, 0:A_DIM] * sig * (1.0 - sig)).astype(BF16)
        cv = _causal_conv(cb_s, cbw_ref, B_TAPS, B_HALO, tt)
        dz_ref[:, 2 * A_DIM:3 * A_DIM] = (dmix_ref[:, A_DIM:A_DIM + B_DIM] * cv).astype(BF16)
        dcb = cbw_ref[0:1, :] * dcv_s[pl.ds(B_TAPS - 1, tt), :]
        for k in range(1, B_TAPS):
            dcb = dcb + cbw_ref[k:k + 1, :] * dcv_s[pl.ds(B_TAPS - 1 - k, tt), :]
        dz_ref[:, 3 * A_DIM:4 * A_DIM] = (dcb * z_ref[:, 4 * A_DIM:5 * A_DIM]).astype(BF16)
        dz_ref[:, 4 * A_DIM:5 * A_DIM] = (dcb * z_ref[:, 3 * A_DIM:4 * A_DIM]).astype(BF16)

    half = pl.BlockSpec((tt, A_DIM), lambda i: (i, 0))
    return _call(
        body, name="mixer_bwd_input", grid=(t // tt,),
        in_specs=[z_spec, zh_spec, pl.BlockSpec((tt, A_DIM + B_DIM), lambda i: (i, 0)),
                  half, pl.BlockSpec((A_HALO, A_DIM), lambda i: (jnp.minimum((i + 1) * a_blocks, last_a), 0)),
                  half, pl.BlockSpec((B_HALO, B_DIM), lambda i: (jnp.minimum((i + 1) * b_blocks, last_b), 0)),
                  _vec_spec(A_TAPS, A_DIM), _vec_spec(B_TAPS, B_DIM)],
        out_specs=[pl.BlockSpec((tt, IN_EVEN), lambda i: (i, 0))],
        out_shape=[jax.ShapeDtypeStruct((t, IN_EVEN), BF16)],
        scratch_shapes=[pltpu.VMEM((B_HALO + tt, B_DIM), F32),
                        pltpu.VMEM((tt + A_HALO, A_DIM), F32), pltpu.VMEM((tt + B_HALO, B_DIM), F32),
                        pltpu.VMEM((SUBLANES - 1, tt + SHIFT_ROWS, A_DIM), F32), pltpu.VMEM((tt, A_DIM), F32)],
        args=(z, z, dmix, da2, da2, dcv, dcv, caw, cbw), parallel=True, comm=comm)[0]


def _tril_ws(ws_ref, g):
    rows = lax.broadcasted_iota(jnp.int32, (CHUNK, CHUNK), 0)
    cols = lax.broadcasted_iota(jnp.int32, (CHUNK, CHUNK), 1)
    return jnp.where(rows >= cols, ws_ref[g], 0.0).astype(BF16), rows >= cols


def _sgu_fwd(pre, lvg, lvb, ws, bs_b, comm=None):
    t = pre.shape[0]
    tt = _tile(t, 256)

    def body(pre_ref, lvg_ref, lvb_ref, ws_ref, bsb_ref, y_ref):
        vhat, _ = _layer_norm_stats(_gelu(pre_ref[:, C_DIM:2 * C_DIM]))
        vl = (vhat * lvg_ref[...] + lvb_ref[...]).astype(BF16)
        for g in range(C_GROUPS):
            w, _ = _tril_ws(ws_ref, g)
            cols = slice(g * CHUNK, (g + 1) * CHUNK)
            for ci in range(tt // CHUNK):
                rows = slice(ci * CHUNK, (ci + 1) * CHUNK)
                sv = jnp.dot(w, vl[rows, cols], preferred_element_type=F32) + bsb_ref[g]
                y_ref[rows, cols] = (_gelu(pre_ref[rows, cols]) * sv).astype(BF16)

    group = pl.BlockSpec((C_GROUPS, CHUNK, CHUNK), lambda i: (0, 0, 0))
    return _call(
        body, name="sgu_fwd", grid=(t // tt,),
        in_specs=[pl.BlockSpec((tt, 2 * C_DIM), lambda i: (i, 0)), _vec_spec(1, C_DIM), _vec_spec(1, C_DIM), group, group],
        out_specs=[pl.BlockSpec((tt, C_DIM), lambda i: (i, 0))], out_shape=[jax.ShapeDtypeStruct((t, C_DIM), BF16)],
        args=(pre, lvg, lvb, ws, bs_b), parallel=True, comm=comm)[0]


def _sgu_bwd(pre, dy, lvg, lvb, ws, bs_b, comm=None):
    t = pre.shape[0]
    tt = _tile(t, 256)

    def body(pre_ref, dy_ref, lvg_ref, lvb_ref, ws_ref, bsb_ref, dpre_ref, dws_ref, dbsb_ref, dlvg_ref, dlvb_ref, dbin_ref,
             dvl_s):
        @pl.when(pl.program_id(0) == 0)
        def _():
            for ref in (dws_ref, dbsb_ref, dlvg_ref, dlvb_ref, dbin_ref):
                ref[...] = jnp.zeros_like(ref)

        v, v_grad = _gelu_and_grad(pre_ref[:, C_DIM:2 * C_DIM])
        vhat, rstd = _layer_norm_stats(v)
        vl = (vhat * lvg_ref[...] + lvb_ref[...]).astype(BF16)
        for g in range(C_GROUPS):
            w, keep = _tril_ws(ws_ref, g)
            cols = slice(g * CHUNK, (g + 1) * CHUNK)
            dws = jnp.zeros((CHUNK, CHUNK), F32)
            dbs = jnp.zeros((CHUNK, 1), F32)
            for ci in range(tt // CHUNK):
                rows = slice(ci * CHUNK, (ci + 1) * CHUNK)
                vl_g = vl[rows, cols]
                sv = jnp.dot(w, vl_g, preferred_element_type=F32) + bsb_ref[g]
                u, u_grad = _gelu_and_grad(pre_ref[rows, cols])
                dyv = dy_ref[rows, cols]
                du = dyv * sv * u_grad
                dpre_ref[rows, cols] = du.astype(BF16)
                dbin_ref[:, cols] += _colsum(du)
                dsv = dyv * u
                dbs = dbs + jnp.sum(dsv, axis=1, keepdims=True)
                dsv16 = dsv.astype(BF16)
                dws = dws + lax.dot_general(dsv16, vl_g, (((1,), (1,)), ((), ())), preferred_element_type=F32)
                dvl_s[rows, cols] = lax.dot_general(w, dsv16, (((0,), (0,)), ((), ())), preferred_element_type=F32)
            dws_ref[g] += jnp.where(keep, dws, 0.0)
            dbsb_ref[g] += dbs
        dvl = dvl_s[...]
        dlvg_ref[...] += _colsum(dvl * vhat)
        dlvb_ref[...] += _colsum(dvl)
        dxh = dvl * lvg_ref[...]
        dv = rstd * (dxh - jnp.mean(dxh, axis=-1, keepdims=True) - vhat * jnp.mean(dxh * vhat, axis=-1, keepdims=True))
        dpv = dv * v_grad
        dpre_ref[:, C_DIM:2 * C_DIM] = dpv.astype(BF16)
        dbin_ref[:, C_DIM:2 * C_DIM] += _colsum(dpv)

    group = pl.BlockSpec((C_GROUPS, CHUNK, CHUNK), lambda i: (0, 0, 0))
    return _call(
        body, name="sgu_bwd", grid=(t // tt,),
        in_specs=[pl.BlockSpec((tt, 2 * C_DIM), lambda i: (i, 0)), pl.BlockSpec((tt, C_DIM), lambda i: (i, 0)),
                  _vec_spec(1, C_DIM), _vec_spec(1, C_DIM), group, group],
        out_specs=[pl.BlockSpec((tt, 2 * C_DIM), lambda i: (i, 0)), group, group,
                   _vec_spec(1, C_DIM), _vec_spec(1, C_DIM), _vec_spec(1, 2 * C_DIM)],
        out_shape=[jax.ShapeDtypeStruct((t, 2 * C_DIM), BF16), jax.ShapeDtypeStruct((C_GROUPS, CHUNK, CHUNK), F32),
                   jax.ShapeDtypeStruct((C_GROUPS, CHUNK, CHUNK), F32), jax.ShapeDtypeStruct((1, C_DIM), F32),
                   jax.ShapeDtypeStruct((1, C_DIM), F32), jax.ShapeDtypeStruct((1, 2 * C_DIM), F32)],
        scratch_shapes=[pltpu.VMEM((tt, C_DIM), F32)],
        args=(pre, dy, lvg, lvb, ws, bs_b), comm=comm)


def _pair_sum(name, part, got, core):
    _, k, n = part.shape
    tk = _tile(k, 1024)

    def body(core_ref, p_ref, s_ref, o_ref):
        o_ref[...] = (p_ref[...] + s_ref[...]).astype(BF16)

    return pl.pallas_call(
        body, name=name,
        grid_spec=pltpu.PrefetchScalarGridSpec(
            num_scalar_prefetch=1, grid=(N_CHIP, k // tk),
            in_specs=[pl.BlockSpec((None, tk, n), lambda q, i, core_ref: (2 * q + core_ref[0], i, 0)),
                      pl.BlockSpec((None, tk, n), lambda q, i, core_ref: (q, i, 0))],
            out_specs=pl.BlockSpec((None, tk, n), lambda q, i, core_ref: (q, i, 0))),
        out_shape=jax.ShapeDtypeStruct((N_CHIP, k, n), BF16),
        compiler_params=pltpu.CompilerParams(dimension_semantics=("parallel", "parallel"), vmem_limit_bytes=VMEM_LIMIT_BYTES),
    )(core, part, got)


def _adamw_math(w, g, m, v):
    m = ADAM_B1 * m + (1.0 - ADAM_B1) * g
    v = ADAM_B2 * v + (1.0 - ADAM_B2) * (g * g)
    m_hat = m / (1.0 - ADAM_B1 ** ADAM_STEP)
    v_hat = v / (1.0 - ADAM_B2 ** ADAM_STEP)
    delta = -ADAM_LR * (m_hat / (jnp.sqrt(v_hat) + ADAM_EPS) + ADAM_WD * w)
    return delta, m, v


def _sum_adamw(name, parts, w, m, v, comm=None):
    layers = len(parts)
    n_parts, k, n = parts[0].shape
    tk = _tile(k, 256)

    def body(*refs):
        p_refs = refs[:layers]
        w_ref, m_ref, v_ref, g_ref, d_ref, nm_ref, nv_ref = refs[layers:]

        def total(p_ref):
            g = p_ref[0].astype(F32)
            for q in range(1, n_parts):
                g = g + p_ref[q].astype(F32)
            return g

        g = total(p_refs[0])
        for l in range(1, layers):
            g = jnp.where(pl.program_id(0) == l, total(p_refs[l]), g)
        g_ref[...] = g
        d_ref[...], nm_ref[...], nv_ref[...] = _adamw_math(w_ref[...], g, m_ref[...], v_ref[...])

    blk = pl.BlockSpec((None, tk, n), lambda l, i: (l, i, 0))
    return _call(
        body, name=name, grid=(layers, k // tk),
        in_specs=[pl.BlockSpec((n_parts, tk, n), lambda l, i: (0, i, 0))] * layers + [blk, blk, blk], out_specs=[blk] * 4,
        out_shape=[jax.ShapeDtypeStruct((layers, k, n), F32)] * 4, args=(*parts, w, m, v), parallel=True, comm=comm)


def _small_adamw(name, parts, w, m, v, losses):
    count = len(parts)

    def in_order(ref):
        total = ref[0]
        for dev in range(1, N_DEV):
            total = total + ref[dev]
        return total

    def body(*refs):
        p_refs, w_refs, m_refs, v_refs = (refs[j * count:(j + 1) * count] for j in range(4))
        losses_ref = refs[4 * count]
        g_refs, d_refs, nm_refs, nv_refs = (refs[4 * count + 1 + j * count:4 * count + 1 + (j + 1) * count] for j in range(4))
        loss_ref = refs[8 * count + 1]
        for i in range(count):
            g = in_order(p_refs[i])
            g_refs[i][...] = g
            d_refs[i][...], nm_refs[i][...], nv_refs[i][...] = _adamw_math(w_refs[i][...], g, m_refs[i][...], v_refs[i][...])
        loss_ref[...] = in_order(losses_ref)

    res = pl.pallas_call(
        body, name=name,
        out_shape=[jax.ShapeDtypeStruct(a.shape, F32) for a in w] * 4 + [jax.ShapeDtypeStruct(losses.shape[1:], F32)],
        compiler_params=pltpu.CompilerParams(vmem_limit_bytes=VMEM_LIMIT_BYTES))(*parts, *w, *m, *v, losses)
    return [res[j * count:(j + 1) * count] for j in range(4)], res[4 * count]


def _rows(a):
    return a.reshape(-1, a.shape[-1])


def _whole(gathered):
    return jnp.transpose(gathered, (1, 0, 2)).reshape(gathered.shape[1], -1)


SMALL =("ev_norm_g", "ev_conv_a_w", "ev_conv_a_b", "ev_ln_a_g", "ev_ln_a_b", "ev_conv_b_w", "od_norm_g", "od_b_in",
         "od_ln_v_g", "od_ln_v_b", "od_w_s", "od_b_s", "mlp_norm_g", "final_norm_g")
SMALL_SHARDED = ("ev_conv_a_w", "ev_conv_b_w", "od_norm_g", "od_b_in", "od_ln_v_g", "od_ln_v_b")
ORDER = ("ev_norm_g", "ev_w_in", "ev_conv_a_w", "ev_conv_a_b", "ev_ln_a_g", "ev_ln_a_b", "ev_conv_b_w", "ev_w_out",
         "od_norm_g", "od_w_in", "od_b_in", "od_ln_v_g", "od_ln_v_b", "od_w_s", "od_b_s", "od_w_out", "mlp_norm_g",
         "mlp_w1", "mlp_w2", "final_norm_g")


def kernel(x, ev_norm_g, ev_w_in, ev_conv_a_w, ev_conv_a_b, ev_ln_a_g, ev_ln_a_b, ev_conv_b_w, ev_w_out, od_norm_g, od_w_in, od_b_in, od_ln_v_g, od_ln_v_b, od_w_s, od_b_s, od_w_out, mlp_norm_g, mlp_w1, mlp_w2, final_norm_g, loss_target, m_ev_norm_g, m_ev_w_in, m_ev_conv_a_w, m_ev_conv_a_b, m_ev_ln_a_g, m_ev_ln_a_b, m_ev_conv_b_w, m_ev_w_out, m_od_norm_g, m_od_w_in, m_od_b_in, m_od_ln_v_g, m_od_ln_v_b, m_od_w_s, m_od_b_s, m_od_w_out, m_mlp_norm_g, m_mlp_w1, m_mlp_w2, m_final_norm_g, v_ev_norm_g, v_ev_w_in, v_ev_conv_a_w, v_ev_conv_a_b, v_ev_ln_a_g, v_ev_ln_a_b, v_ev_conv_b_w, v_ev_w_out, v_od_norm_g, v_od_w_in, v_od_b_in, v_od_ln_v_g, v_od_ln_v_b, v_od_w_s, v_od_b_s, v_od_w_out, v_mlp_norm_g, v_mlp_w1, v_mlp_w2, v_final_norm_g):
    W = dict(ev_norm_g=ev_norm_g, ev_w_in=ev_w_in, ev_conv_a_w=ev_conv_a_w, ev_conv_a_b=ev_conv_a_b, ev_ln_a_g=ev_ln_a_g,
             ev_ln_a_b=ev_ln_a_b, ev_conv_b_w=ev_conv_b_w, ev_w_out=ev_w_out, od_norm_g=od_norm_g, od_w_in=od_w_in,
             od_b_in=od_b_in, od_ln_v_g=od_ln_v_g, od_ln_v_b=od_ln_v_b, od_w_s=od_w_s, od_b_s=od_b_s, od_w_out=od_w_out,
             mlp_norm_g=mlp_norm_g, mlp_w1=mlp_w1, mlp_w2=mlp_w2, final_norm_g=final_norm_g)
    M = dict(ev_norm_g=m_ev_norm_g, ev_w_in=m_ev_w_in, ev_conv_a_w=m_ev_conv_a_w, ev_conv_a_b=m_ev_conv_a_b,
             ev_ln_a_g=m_ev_ln_a_g, ev_ln_a_b=m_ev_ln_a_b, ev_conv_b_w=m_ev_conv_b_w, ev_w_out=m_ev_w_out,
             od_norm_g=m_od_norm_g, od_w_in=m_od_w_in, od_b_in=m_od_b_in, od_ln_v_g=m_od_ln_v_g, od_ln_v_b=m_od_ln_v_b,
             od_w_s=m_od_w_s, od_b_s=m_od_b_s, od_w_out=m_od_w_out, mlp_norm_g=m_mlp_norm_g, mlp_w1=m_mlp_w1,
             mlp_w2=m_mlp_w2, final_norm_g=m_final_norm_g)
    V = dict(ev_norm_g=v_ev_norm_g, ev_w_in=v_ev_w_in, ev_conv_a_w=v_ev_conv_a_w, ev_conv_a_b=v_ev_conv_a_b,
             ev_ln_a_g=v_ev_ln_a_g, ev_ln_a_b=v_ev_ln_a_b, ev_conv_b_w=v_ev_conv_b_w, ev_w_out=v_ev_w_out,
             od_norm_g=v_od_norm_g, od_w_in=v_od_w_in, od_b_in=v_od_b_in, od_ln_v_g=v_od_ln_v_g, od_ln_v_b=v_od_ln_v_b,
             od_w_s=v_od_w_s, od_b_s=v_od_b_s, od_w_out=v_od_w_out, mlp_norm_g=v_mlp_norm_g, mlp_w1=v_mlp_w1,
             mlp_w2=v_mlp_w2, final_norm_g=v_final_norm_g)

    n_seq, seq, d = x.shape
    t = n_seq * seq
    dev = 4 * lax.axis_index("x") + 2 * lax.axis_index("y") + lax.axis_index("c")
    core = lax.axis_index("c").astype(jnp.int32).reshape(1)

    ev_g, cab, lag, lab = W["ev_norm_g"], W["ev_conv_a_b"], W["ev_ln_a_g"], W["ev_ln_a_b"]
    ws = W["od_w_s"][0]
    bs_b = jnp.broadcast_to(W["od_b_s"][0][:, :, None], (C_GROUPS, CHUNK, CHUNK))
    mlp_g = [W["mlp_norm_g"][l:l + 1] for l in range(2)]
    fin_g = W["final_norm_g"].reshape(1, d)

    h0 = x.reshape(t, d)
    for store in (W, M, V):
        store["ev_w_in"] = jnp.transpose(store["ev_w_in"], (0, 2, 1))
    gather = _gather_comm([W["ev_w_in"][0].astype(BF16)])
    later = [(W["ev_w_out"], 0), (W["od_w_in"], 0), (W["od_w_out"], 0), (W["mlp_w1"], 0), (W["mlp_w2"], 0),
             (W["mlp_w1"], 1), (W["mlp_w2"], 1)]
    n0, (w_ev_out16, w_od_in16, w_od_out16, w1_0, w2_0, w1_1, w2_1) = _rms_fwd("ev_norm", h0, ev_g, comm=gather, casts=later)
    w_ev_in_t = gather.out[0].reshape(IN_EVEN, D_MODEL)

    first, second = (0, 2), (1, 2)

    g_a, g_b = _gather_comm([w_ev_out16] + [_rows(W[n]) for n in SMALL_SHARDED]), _gather_comm([w1_0], first)
    z = _mm_nt("ev_in", n0, w_ev_in_t, IN_EVEN, _ep_store, [F32], comm=[g_a, g_b])[0]
    w_ev_out = g_a.out[0].reshape(D_MODEL, D_MODEL)
    caw, cbw, od_g, od_bin, lvg, lvb = [_whole(g) for g in g_a.out[1:]]

    g_c, g_d = _gather_comm([w1_0], second, into=g_b.out), _gather_comm([w2_0], first)
    mix, a2 = _mixer_fwd(z, seq, caw, cab, lag, lab, cbw, comm=[g_c, g_d])
    w1 = [_whole(g_c.out[0]), None]

    g_e = _gather_comm([w_od_in16], first)
    h1, n1 = _mm_nn("ev_out", mix, w_ev_out, d, _ep_residual_norm, [F32, BF16], extras=(h0,), rows=(mlp_g[0],), comm=g_e)

    g_f, g_f2 = _gather_comm([w2_0], second, into=g_d.out), _gather_comm([w_od_in16], second, into=g_e.out)
    q0 = _mm_nn("mlp0_up", n1, w1[0], D_FF, _ep_relu_sq, [BF16], comm=[g_f, g_f2])[0]
    w2 = [g_f.out[0].reshape(D_FF, D_MODEL), None]
    w_od_in = _whole(g_f2.out[0])

    g_g, g_g2 = _gather_comm([w1_1], first), _gather_comm([w_od_out16])
    h2, n2 = _mm_nn("mlp0_down", q0, w2[0], d, _ep_residual_norm, [F32, BF16], extras=(h1,), rows=(od_g,), comm=[g_g, g_g2])
    w_od_out = g_g2.out[0].reshape(D_MODEL, D_MODEL)
    g_h = _gather_comm([w1_1], second, into=g_g.out)
    pre = _mm_nn("od_in", n2, w_od_in, 2 * C_DIM, _ep_bias, [F32], rows=(od_bin,), comm=g_h)[0]
    w1[1] = _whole(g_h.out[0])
    y = _sgu_fwd(pre, lvg, lvb, ws, bs_b)
    g_i = _gather_comm([w2_1], (0, 4))
    h3, n3 = _mm_nn("od_out", y, w_od_out, d, _ep_residual_norm, [F32, BF16], extras=(h2,), rows=(mlp_g[1],), comm=g_i)
    g_j = _gather_comm([w2_1], (1, 4, 3), into=g_i.out)
    q1 = _mm_nn("mlp1_up", n3, w1[1], D_FF, _ep_relu_sq, [BF16], comm=g_j)[0]
    w2[1] = g_j.out[0].reshape(D_FF, D_MODEL)
    grad, grad16, d_fin_g, loss_part = _mm_nn(
        "mlp1_down", q1, w2[1], d, _ep_final_loss, [F32, BF16], extras=(h3, loss_target.reshape(t, d)), rows=(fin_g,),
        sums=(True, False), tm_want=FUSED_ROWS)

    by_chip = {}

    def swap(name, parts):
        comm = _pair_comm([parts])
        comm.parts, comm.weight = parts, name
        return comm

    def exchange(swapped, halves=False):
        sums = _pair_sum(f"pair_sum_{swapped.weight}", swapped.parts, swapped.out[0], core)
        if not halves:
            comm = _chip_comm([sums])
            comm.weight = swapped.weight
            return comm
        comm = _chip_comm([sums], first)
        comm.sums, comm.weight = sums, swapped.weight
        return comm

    def rest(comm):
        other = _chip_comm([comm.sums], second, into=comm.out)
        other.weight = comm.weight
        return other

    def done(comm):
        by_chip[comm.weight] = comm.out[0]

    dw2_1 = _mm_tn("mlp1_dw2", q1, grad16).reshape(N_DEV, D_FF // N_DEV, D_MODEL)
    s_a = swap("w2_1", dw2_1)
    dp = _mm_nt("mlp1_dq", grad16, w2[1], D_FF, _ep_relu_sq_grad, [BF16], extras=(q1,), comm=s_a)[0]
    c_a = exchange(s_a, halves=True)
    dw1_1 = _mm_tn("mlp1_dw1", n3, dp, col_blocks=N_DEV, comm=c_a)
    c_a2, s_b = rest(c_a), swap("w1_1", dw1_1)
    dn = _mm_nt("mlp1_dn", dp, w1[1], d, _ep_store, [BF16], comm=[c_a2, s_b])[0]
    done(c_a2)
    c_b = exchange(s_b, halves=True)
    grad, grad16, dg_mlp1 = _rms_bwd("mlp1_dn_norm", dn, h3, mlp_g[1], grad)
    dy, d_od_out = _mm_out_proj_grads("od_out_grads", grad16, w_od_out, y)
    s_c = swap("od_out", d_od_out.reshape(N_DEV, D_MODEL // N_DEV, D_MODEL))
    dpre, d_ws, d_bsb, d_lvg, d_lvb, d_bin = _sgu_bwd(pre, dy, lvg, lvb, ws, bs_b, comm=[c_b, s_c])
    c_c = exchange(s_c)
    c_b2 = rest(c_b)
    d_od_in = _mm_tn("od_dw_in", n2, dpre, col_blocks=N_DEV, comm=c_b2)
    done(c_b2)
    s_d = swap("od_in", d_od_in)
    dn = _mm_nt("od_dn", dpre, w_od_in, d, _ep_store, [BF16], comm=[s_d, c_c])[0]
    done(c_c)
    c_d = exchange(s_d)
    grad, grad16, d_od_g = _rms_bwd("od_dn_norm", dn, h2, od_g, grad)
    group_1 = dict(od_norm_g=d_od_g, od_b_in=d_bin, od_ln_v_g=d_lvg, od_ln_v_b=d_lvb,
                   od_w_s=d_ws.reshape(C_GROUPS * CHUNK, CHUNK), od_b_s=d_bsb[:, :, 0], final_norm_g=d_fin_g,
                   mlp_norm_g_1=dg_mlp1, loss=jnp.broadcast_to(loss_part, (SUBLANES, LANE_BLOCK)))
    gather_1 = _gather_comm(list(group_1.values()))
    dw2_0 = _mm_tn("mlp0_dw2", q0, grad16, comm=c_d).reshape(N_DEV, D_FF // N_DEV, D_MODEL)
    done(c_d)
    s_e = swap("w2_0", dw2_0)
    dp = _mm_nt("mlp0_dq", grad16, w2[0], D_FF, _ep_relu_sq_grad, [BF16], extras=(q0,), comm=[s_e, gather_1])[0]
    c_e = exchange(s_e, halves=True)
    dw1_0 = _mm_tn("mlp0_dw1", n1, dp, col_blocks=N_DEV, comm=c_e)
    c_e2, s_f = rest(c_e), swap("w1_0", dw1_0)
    dn = _mm_nt("mlp0_dn", dp, w1[0], d, _ep_store, [BF16], comm=[c_e2, s_f])[0]
    done(c_e2)
    c_f = exchange(s_f, halves=True)
    grad, grad16, dg_mlp0 = _rms_bwd("mlp0_dn_norm", dn, h1, mlp_g[0], grad)
    dmix, d_ev_out = _mm_out_proj_grads("ev_out_grads", grad16, w_ev_out, mix)
    s_g = swap("ev_out", d_ev_out.reshape(N_DEV, D_MODEL // N_DEV, D_MODEL))
    da2, dcv, d_caw, d_cab, d_lag, d_lab, d_cbw = _mixer_bwd_local(z, a2, dmix, seq, lag, lab, comm=[c_f, s_g])
    c_g = exchange(s_g)
    c_f2 = rest(c_f)
    dz = _mixer_bwd_input(z, dmix, da2, dcv, seq, caw, cbw, comm=[c_f2, c_g])
    done(c_f2)
    done(c_g)
    group_2 = dict(ev_conv_a_w=d_caw, ev_conv_a_b=d_cab, ev_ln_a_g=d_lag, ev_ln_a_b=d_lab, ev_conv_b_w=d_cbw,
                   mlp_norm_g_0=dg_mlp0)
    gather_2 = _gather_comm(list(group_2.values()))
    d_ev_in_t = _mm_tn("ev_dw_in", dz, n0, comm=gather_2).reshape(N_DEV, IN_EVEN // N_DEV, D_MODEL)
    s_h = swap("ev_in", d_ev_in_t)
    dn = _mm_nn("ev_dn", dz, w_ev_in_t, d, _ep_store, [BF16], comm=s_h)[0]
    c_h = exchange(s_h)
    grad_x, _, d_ev_g = _rms_bwd("ev_dn_norm", dn, h0, ev_g, grad, comm=c_h)
    done(c_h)

    last_gather = _gather_comm([d_ev_g])
    shard = {"od_w_out": ("od_out",), "mlp_w1": ("w1_0", "w1_1"), "mlp_w2": ("w2_0", "w2_1"), "od_w_in": ("od_in",),
             "ev_w_out": ("ev_out",), "ev_w_in": ("ev_in",)}
    carried = {"od_w_out": last_gather}
    out_g, out_d, out_m, out_v = {}, {}, {}, {}
    for name, keys in shard.items():
        out_g[name], out_d[name], out_m[name], out_v[name] = _sum_adamw(
            f"adamw_{name}", [by_chip[key] for key in keys], W[name], M[name], V[name], comm=carried.get(name))
    for store in (out_g, out_d, out_m, out_v):
        store["ev_w_in"] = jnp.transpose(store["ev_w_in"], (0, 2, 1))

    gathered = dict(zip(group_1, gather_1.out), **dict(zip(group_2, gather_2.out)), ev_norm_g=last_gather.out[0])
    gathered["mlp_norm_g"] = jnp.concatenate([gathered["mlp_norm_g_0"], gathered["mlp_norm_g_1"]], axis=1)
    mine = []
    for n in SMALL:
        g = gathered[n]
        if n in SMALL_SHARDED:
            width = W[n].shape[-1]
            g = lax.dynamic_slice_in_dim(g, dev * width, width, axis=2)
        mine.append(g)
    res, loss_tile = _small_adamw("adamw_small", mine, [_rows(W[n]) for n in SMALL], [_rows(M[n]) for n in SMALL],
                                  [_rows(V[n]) for n in SMALL], gathered["loss"])
    loss = loss_tile[0, 0]
    for store, values in zip((out_g, out_d, out_m, out_v), res):
        for n, value in zip(SMALL, values):
            store[n] = value.reshape(W[n].shape)

    return (loss, grad_x.reshape(n_seq, seq, d), *[out_g[n] for n in ORDER], *[out_d[n] for n in ORDER],
            *[out_m[n] for n in ORDER], *[out_v[n] for n in ORDER])
```

```python
import math

import jax
import jax.numpy as jnp
from jax import lax
from jax.experimental import pallas as pl
from jax.experimental.pallas import tpu as pltpu

F32 = jnp.float32
BF16 = jnp.bfloat16
MESH = pl.DeviceIdType.MESH

D_MODEL = 1024
A_DIM = 512
B_DIM = 512
IN_EVEN = 2 * A_DIM + 3 * B_DIM
A_TAPS = 31
B_TAPS = 3
CHUNK = 128
C_GROUPS = 8
C_DIM = 1024
D_FF = 4096
RMS_EPS = 1e-6
LN_EPS = 1e-5
N_DEV = 8
N_CHIP = 4

ADAM_LR = 0.001
ADAM_B1 = 0.9
ADAM_B2 = 0.999
ADAM_EPS = 1e-08
ADAM_WD = 0.01
ADAM_STEP = 10

A_HALO = 32
B_HALO = 8
VMEM_LIMIT_BYTES = 56 * 1024 * 1024
TINY = 1.1754944e-38
INV_SQRT2 = 1.0 / math.sqrt(2.0)
INV_SQRT_2PI = 1.0 / math.sqrt(2.0 * math.pi)
HBM_SPEC = pl.BlockSpec(memory_space=pltpu.HBM)


def _tile(n, want):
    t = min(n, want)
    while n % t:
        t //= 2
    return t


def _sigmoid(x):
    return 1.0 / (1.0 + jnp.exp(-x))


def _gelu(x):
    return 0.5 * x * (1.0 + lax.erf(x * INV_SQRT2))


def _gelu_and_grad(x):
    cdf = 0.5 * (1.0 + lax.erf(x * INV_SQRT2))
    return x * cdf, cdf + x * jnp.exp(-0.5 * x * x) * INV_SQRT_2PI


def _colsum(x):
    return jnp.sum(x, axis=0, keepdims=True)


class _Comm:
    def __init__(self, ins, out_shapes, sem_shapes, start, finish, middle=None, into=None, sibling=False, chips=False):
        self.ins, self.out_shapes, self.sem_shapes, self.start, self.finish = ins, out_shapes, sem_shapes, start, finish
        self.middle = middle
        self.sibling, self.chips = sibling, chips
        self.into = list(into) if into is not None else []
        self.out = None


def _piece_rows(rows, piece):
    if piece is None:
        return 0, rows
    i, n = piece
    return i * (rows // n), rows // n


MIDDLE_AT = 0.75
BARRIER_IDS = {(True, True): 0, (True, False): 1, (False, True): 2}
FUSED_ROWS = 512


def _call(body, *, name, grid, in_specs, out_specs, out_shape, args, scratch_shapes=(), parallel=False, comm=None):
    comms = [] if comm is None else (list(comm) if isinstance(comm, (list, tuple)) else [comm])
    if not comms:
        sem = ("parallel" if parallel else "arbitrary",) * len(grid)
        return pl.pallas_call(
            body, name=name, grid=grid, in_specs=list(in_specs), out_specs=list(out_specs), out_shape=list(out_shape),
            scratch_shapes=list(scratch_shapes),
            compiler_params=pltpu.CompilerParams(dimension_semantics=sem, vmem_limit_bytes=VMEM_LIMIT_BYTES),
        )(*args)
    n_in, n_out, n_scr = len(in_specs), len(out_shape), len(scratch_shapes)
    c_ins_all = [a for cm in comms for a in cm.ins]
    c_into_all = [a for cm in comms for a in cm.into]
    c_out_shapes = [s for cm in comms for s in cm.out_shapes]
    c_sem_shapes = [s for cm in comms for s in cm.sem_shapes]
    aliases, in_pos, out_pos = {}, n_in + len(c_ins_all), n_out
    for cm in comms:
        for j in range(len(cm.into)):
            aliases[in_pos + j] = out_pos + j
        in_pos += len(cm.into)
        out_pos += len(cm.out_shapes)
    to_sibling = any(cm.sibling for cm in comms)
    to_chips = any(cm.chips for cm in comms)
    steps = grid
    total = math.prod(steps)
    first_step = (0,) * len(steps)
    last_step = tuple(s - 1 for s in steps)
    middle_step = None
    if 0 < int(MIDDLE_AT * total) < total - 1:
        rest, idx = int(MIDDLE_AT * total), []
        for s in reversed(steps):
            idx.append(rest % s)
            rest //= s
        middle_step = tuple(reversed(idx))

    def carrying(*refs):
        pos = 0
        ins = refs[pos:pos + n_in]; pos += n_in
        c_ins = refs[pos:pos + len(c_ins_all)]; pos += len(c_ins_all) + len(c_into_all)
        outs = refs[pos:pos + n_out]; pos += n_out
        c_outs = refs[pos:pos + len(c_out_shapes)]; pos += len(c_out_shapes)
        scr = refs[pos:pos + n_scr]; pos += n_scr
        c_sems = refs[pos:]
        views, i0, o0, s0 = [], 0, 0, 0
        for cm in comms:
            views.append((c_ins[i0:i0 + len(cm.ins)], c_outs[o0:o0 + len(cm.out_shapes)], c_sems[s0:s0 + len(cm.sem_shapes)]))
            i0, o0, s0 = i0 + len(cm.ins), o0 + len(cm.out_shapes), s0 + len(cm.sem_shapes)

        def at(step):
            hit = pl.program_id(0) == step[0]
            for axis in range(1, len(steps)):
                hit = jnp.logical_and(hit, pl.program_id(axis) == step[axis])
            return hit

        @pl.when(at(first_step))
        def _():
            x, y, c, chips = _place()
            peers = ([(x, y, 1 - c)] if to_sibling else []) + ([(*chip, c) for chip in chips] if to_chips else [])
            barrier = pltpu.get_barrier_semaphore()
            for peer in peers:
                pl.semaphore_signal(barrier, inc=1, device_id=peer, device_id_type=MESH)
            pl.semaphore_wait(barrier, len(peers))
            for cm, view in zip(comms, views):
                cm.start(*view)

        if middle_step is not None:
            @pl.when(at(middle_step))
            def _():
                for cm, view in zip(comms, views):
                    if cm.middle is not None:
                        cm.middle(*view)

        body(*ins, *outs, *scr)

        @pl.when(at(last_step))
        def _():
            for cm, view in zip(comms, views):
                if cm.middle is not None and middle_step is None:
                    cm.middle(*view)
            for cm, view in zip(comms, views):
                cm.finish(*view)

    res = pl.pallas_call(
        carrying, name=name, grid=grid,
        in_specs=[*in_specs, *[HBM_SPEC] * (len(c_ins_all) + len(c_into_all))],
        out_specs=[*out_specs, *[HBM_SPEC] * len(c_out_shapes)],
        out_shape=[*out_shape, *c_out_shapes], scratch_shapes=[*scratch_shapes, *c_sem_shapes],
        input_output_aliases=aliases,
        compiler_params=pltpu.CompilerParams(dimension_semantics=("arbitrary",) * len(grid), vmem_limit_bytes=VMEM_LIMIT_BYTES,
                                             collective_id=BARRIER_IDS[to_sibling, to_chips]),
    )(*args, *c_ins_all, *c_into_all)
    pos = n_out
    for cm in comms:
        cm.out = list(res[pos:pos + len(cm.out_shapes)])
        pos += len(cm.out_shapes)
    return list(res[:n_out])


def _place():
    x, y, c = lax.axis_index("x"), lax.axis_index("y"), lax.axis_index("c")
    return x, y, c, [(1 - x, y), (x, 1 - y), (1 - x, 1 - y)]


def _gather_comm(shards, piece=None, into=None):
    nw = len(shards)
    spans = [_piece_rows(s.shape[0], piece) for s in shards]

    def plan(ins, outs, sems):
        send_sems, recv_sems, local_sems = sems
        x, y, c, chips = _place()
        me, sibling = (x, y, c), (x, y, 1 - c)

        def slot(w, p):
            return outs[w].at[4 * p[0] + 2 * p[1] + p[2], pl.ds(*spans[w])]

        def mine(w):
            return ins[w].at[pl.ds(*spans[w])]

        def copy(w, k, block, to, src=None):
            return pltpu.make_async_remote_copy(
                src_ref=slot(w, block) if src is None else src, dst_ref=slot(w, block),
                send_sem=send_sems.at[w, k], recv_sem=recv_sems.at[w, k], device_id=to, device_id_type=MESH)

        local = [pltpu.make_async_copy(mine(w), slot(w, me), local_sems.at[w]) for w in range(nw)]
        first = [[copy(w, 0, me, sibling, src=mine(w))] + [copy(w, 1 + j, me, (*chip, c), src=mine(w)) for j, chip in enumerate(chips)]
                 for w in range(nw)]
        landed = [[copy(w, 1 + j, (*chip, c), me) for j, chip in enumerate(chips)] for w in range(nw)]
        passed = [[copy(w, 4 + j, (*chip, c), sibling) for j, chip in enumerate(chips)] for w in range(nw)]
        from_sibling = [[copy(w, 0, sibling, me)] + [copy(w, 4 + j, (*chip, 1 - c), me) for j, chip in enumerate(chips)]
                        for w in range(nw)]
        return local, first, landed, passed, from_sibling

    def start(ins, outs, sems):
        local, first, _, _, _ = plan(ins, outs, sems)
        for cp in local:
            cp.start()
        for row in first:
            for cp in row:
                cp.start()

    def middle(ins, outs, sems):
        _, _, landed, passed, _ = plan(ins, outs, sems)
        for w in range(nw):
            for j in range(3):
                landed[w][j].wait_recv()
                passed[w][j].start()

    def finish(ins, outs, sems):
        local, first, landed, passed, from_sibling = plan(ins, outs, sems)
        for w in range(nw):
            for cp in from_sibling[w]:
                cp.wait_recv()
        for w in range(nw):
            for cp in first[w] + passed[w]:
                cp.wait_send()
        for cp in local:
            cp.wait()

    return _Comm(list(shards), [jax.ShapeDtypeStruct((N_DEV, *s.shape), s.dtype) for s in shards],
                 [pltpu.SemaphoreType.DMA((nw, 7)), pltpu.SemaphoreType.DMA((nw, 7)), pltpu.SemaphoreType.DMA((nw,))],
                 start, finish, middle, into=into, sibling=True, chips=True)


def _pair_comm(parts):
    nw = len(parts)

    def plan(ins, outs, sems):
        send_sems, recv_sems = sems
        x, y, c, _ = _place()
        return [pltpu.make_async_remote_copy(
            src_ref=ins[w].at[2 * q + 1 - c], dst_ref=outs[w].at[q], send_sem=send_sems.at[w, q], recv_sem=recv_sems.at[w, q],
            device_id=(x, y, 1 - c), device_id_type=MESH) for w in range(nw) for q in range(N_CHIP)]

    def start(ins, outs, sems):
        for cp in plan(ins, outs, sems):
            cp.start()

    def finish(ins, outs, sems):
        for cp in plan(ins, outs, sems):
            cp.wait()

    return _Comm(list(parts), [jax.ShapeDtypeStruct((N_CHIP, *p.shape[1:]), p.dtype) for p in parts],
                 [pltpu.SemaphoreType.DMA((nw, N_CHIP)), pltpu.SemaphoreType.DMA((nw, N_CHIP))], start, finish, sibling=True)


def _chip_comm(sums, piece=None, into=None):
    nw = len(sums)
    spans = [_piece_rows(s.shape[1], piece) for s in sums]

    def plan(ins, outs, sems):
        send_sems, recv_sems, local_sems = sems
        x, y, c, chips = _place()
        my_chip = 2 * x + y
        local = [pltpu.make_async_copy(ins[w].at[my_chip, pl.ds(*spans[w])], outs[w].at[my_chip, pl.ds(*spans[w])],
                                       local_sems.at[w]) for w in range(nw)]
        remote = [pltpu.make_async_remote_copy(
            src_ref=ins[w].at[2 * chip[0] + chip[1], pl.ds(*spans[w])], dst_ref=outs[w].at[my_chip, pl.ds(*spans[w])],
            send_sem=send_sems.at[w, j], recv_sem=recv_sems.at[w, j], device_id=(*chip, c), device_id_type=MESH)
            for w in range(nw) for j, chip in enumerate(chips)]
        return local, remote

    def start(ins, outs, sems):
        local, remote = plan(ins, outs, sems)
        for cp in local + remote:
            cp.start()

    def finish(ins, outs, sems):
        local, remote = plan(ins, outs, sems)
        for cp in remote + local:
            cp.wait()

    return _Comm(list(sums), [jax.ShapeDtypeStruct(s.shape, s.dtype) for s in sums],
                 [pltpu.SemaphoreType.DMA((nw, 3)), pltpu.SemaphoreType.DMA((nw, 3)), pltpu.SemaphoreType.DMA((nw,))],
                 start, finish, into=into, chips=True)


def _matmul(name, a, b, *, kind, m, n, k, a_spec, b_spec, tm, tn, tk, out_shape, out_specs, epilogue,
            extras=(), extra_specs=(), comm=None):
    assert tk == k, "the whole contraction is one grid step"
    dims = {"nn": (((1,), (0,)), ((), ())), "nt": (((1,), (1,)), ((), ())), "tn": (((0,), (0,)), ((), ()))}[kind]
    n_extra = len(extras)
    n_out = len(out_shape)

    def body(a_ref, b_ref, *rest):
        extra_refs = rest[:n_extra]
        out_refs = rest[n_extra:n_extra + n_out]
        epilogue(lax.dot_general(a_ref[...], b_ref[...], dims, preferred_element_type=F32), extra_refs, out_refs)

    return _call(
        body, name=name, grid=(m // tm, n // tn, 1), in_specs=[a_spec, b_spec, *extra_specs], out_specs=out_specs,
        out_shape=out_shape, args=(a, b, *extras), comm=comm)


def _mm_operands(m, n, tm, tn, out_dtypes, extras, rows, sums):
    tile = pl.BlockSpec((tm, tn), lambda i, j, kk: (i, j))
    row = pl.BlockSpec((1, tn), lambda i, j, kk: (0, j))
    one = pl.BlockSpec((1, 1), lambda i, j, kk: (0, 0))
    out_shape = [jax.ShapeDtypeStruct((m, n), dt) for dt in out_dtypes]
    out_shape += [jax.ShapeDtypeStruct((1, n if wide else 1), F32) for wide in sums]
    out_specs = [tile] * len(out_dtypes) + [row if wide else one for wide in sums]
    return (*extras, *rows), [tile] * len(extras) + [row] * len(rows), out_shape, out_specs


def _row_tile(m, k, tm_want):
    return _tile(m, tm_want or (1024 if k <= 1024 else 512))


def _mm_nn(name, a, b, n, epilogue, out_dtypes, *, extras=(), rows=(), sums=(), tm_want=None, comm=None):
    m, k = a.shape
    tm = _row_tile(m, k, tm_want)
    tn = _tile(n, 1024)
    extras, extra_specs, out_shape, out_specs = _mm_operands(m, n, tm, tn, out_dtypes, extras, rows, sums)
    return _matmul(
        name, a, b, kind="nn", m=m, n=n, k=k, tm=tm, tn=tn, tk=k,
        a_spec=pl.BlockSpec((tm, k), lambda i, j, kk: (i, kk)), b_spec=pl.BlockSpec((k, tn), lambda i, j, kk: (kk, j)),
        out_shape=out_shape, out_specs=out_specs, epilogue=epilogue, extras=extras, extra_specs=extra_specs, comm=comm)


def _mm_nt(name, a, b, n, epilogue, out_dtypes, *, extras=(), rows=(), sums=(), tm_want=None, comm=None):
    m, k = a.shape
    tm = _row_tile(m, k, tm_want)
    tn = _tile(n, 1024)
    extras, extra_specs, out_shape, out_specs = _mm_operands(m, n, tm, tn, out_dtypes, extras, rows, sums)
    return _matmul(
        name, a, b, kind="nt", m=m, n=n, k=k, tm=tm, tn=tn, tk=k,
        a_spec=pl.BlockSpec((tm, k), lambda i, j, kk: (i, kk)), b_spec=pl.BlockSpec((tn, k), lambda i, j, kk: (j, kk)),
        out_shape=out_shape, out_specs=out_specs, epilogue=epilogue, extras=extras, extra_specs=extra_specs, comm=comm)


def _mm_out_proj_grads(name, grad16, w, act):
    t, n = grad16.shape
    k = w.shape[0]
    tm = _tile(t, 1024)

    def body(g_ref, w_ref, a_ref, din_ref, dw_ref):
        g = g_ref[...]
        din_ref[...] = lax.dot_general(g, w_ref[...], (((1,), (1,)), ((), ())), preferred_element_type=F32)
        part = lax.dot_general(a_ref[...], g, (((0,), (0,)), ((), ())), preferred_element_type=F32)

        @pl.when(pl.program_id(0) == 0)
        def _():
            dw_ref[...] = part

        @pl.when(pl.program_id(0) > 0)
        def _():
            dw_ref[...] += part

    return _call(
        body, name=name, grid=(t // tm,),
        in_specs=[pl.BlockSpec((tm, n), lambda i: (i, 0)), pl.BlockSpec((k, n), lambda i: (0, 0)),
                  pl.BlockSpec((tm, k), lambda i: (i, 0))],
        out_specs=[pl.BlockSpec((tm, k), lambda i: (i, 0)), pl.BlockSpec((k, n), lambda i: (0, 0))],
        out_shape=[jax.ShapeDtypeStruct((t, k), F32), jax.ShapeDtypeStruct((k, n), F32)], args=(grad16, w, act))


def _mm_tn(name, a, b, *, col_blocks=0, comm=None):
    t, k = a.shape
    n = b.shape[1]
    tm = _row_tile(k, t, None)
    tn = n // col_blocks if col_blocks else _tile(n, 1024)
    tk = t
    if col_blocks:
        out_shape = [jax.ShapeDtypeStruct((col_blocks, k, tn), F32)]
        out_specs = [pl.BlockSpec((None, tm, tn), lambda i, j, kk: (j, i, 0))]
    else:
        out_shape = [jax.ShapeDtypeStruct((k, n), F32)]
        out_specs = [pl.BlockSpec((tm, tn), lambda i, j, kk: (i, j))]

    def epilogue(acc, extra_refs, out_refs):
        out_refs[0][...] = acc

    return _matmul(
        name, a, b, kind="tn", m=k, n=n, k=t, tm=tm, tn=tn, tk=tk,
        a_spec=pl.BlockSpec((tk, tm), lambda i, j, kk: (kk, i)), b_spec=pl.BlockSpec((tk, tn), lambda i, j, kk: (kk, j)),
        out_shape=out_shape, out_specs=out_specs, epilogue=epilogue, comm=comm)[0]


def _ep_store(acc, extra_refs, out_refs):
    out_refs[0][...] = acc.astype(out_refs[0].dtype)


def _ep_bias(acc, extra_refs, out_refs):
    out_refs[0][...] = acc + extra_refs[0][...]


def _rms(x):
    r = lax.rsqrt(jnp.mean(x * x, axis=-1, keepdims=True) + RMS_EPS)
    return r, x * r


def _rms_grad(dn, r, xr, g):
    dy = dn * g
    return r * (dy - xr * jnp.mean(dy * xr, axis=-1, keepdims=True))


def _ep_residual_norm(acc, extra_refs, out_refs):
    h = extra_refs[0][...] + acc
    out_refs[0][...] = h
    _, xr = _rms(h)
    out_refs[1][...] = (xr * extra_refs[1][...]).astype(BF16)


def _ep_final_loss(acc, extra_refs, out_refs):
    @pl.when(pl.program_id(0) == 0)
    def _():
        out_refs[2][...] = jnp.zeros_like(out_refs[2])
        out_refs[3][...] = jnp.zeros_like(out_refs[3])

    h = extra_refs[0][...] + acc
    g = extra_refs[2][...]
    r, xr = _rms(h)
    err = xr * g - extra_refs[1][...]
    out_refs[3][...] += 0.5 * jnp.sum(jnp.mean(err * err, axis=-1, keepdims=True), axis=0, keepdims=True)
    dout = err * (1.0 / h.shape[-1])
    out_refs[2][...] += _colsum(dout * xr)
    out = _rms_grad(dout, r, xr, g)
    out_refs[0][...] = out
    out_refs[1][...] = out.astype(BF16)


def _ep_relu_sq(acc, extra_refs, out_refs):
    r = jnp.maximum(acc, 0.0)
    out_refs[0][...] = (r * r).astype(BF16)


def _ep_relu_sq_grad(acc, extra_refs, out_refs):
    q = extra_refs[0][...].astype(F32)
    out_refs[0][...] = (acc * (2.0 * q * lax.rsqrt(jnp.maximum(q, TINY)))).astype(BF16)


def _rms_fwd(name, h, g, comm=None, casts=()):
    t, d = h.shape
    tt = _tile(t, 512)
    steps = t // tt
    n_cast = len(casts)

    def body(h_ref, g_ref, *rest):
        n_ref = rest[n_cast]
        x = h_ref[...]
        r = lax.rsqrt(jnp.mean(x * x, axis=-1, keepdims=True) + RMS_EPS)
        n_ref[...] = (x * r * g_ref[...]).astype(BF16)
        for src, dst in zip(rest[:n_cast], rest[n_cast + 1:]):
            dst[...] = src[...].astype(BF16)

    cast_in = [pl.BlockSpec((None, w.shape[1] // steps, w.shape[2]), lambda i, l=l: (l, i, 0)) for w, l in casts]
    cast_out = [pl.BlockSpec((w.shape[1] // steps, w.shape[2]), lambda i: (i, 0)) for w, _ in casts]
    res = _call(
        body, name=name, grid=(steps,),
        in_specs=[pl.BlockSpec((tt, d), lambda i: (i, 0)), pl.BlockSpec((1, d), lambda i: (0, 0)), *cast_in],
        out_specs=[pl.BlockSpec((tt, d), lambda i: (i, 0)), *cast_out],
        out_shape=[jax.ShapeDtypeStruct((t, d), BF16)] + [jax.ShapeDtypeStruct(w.shape[1:], BF16) for w, _ in casts],
        args=(h, g, *[w for w, _ in casts]), parallel=True, comm=comm)
    return res[0], res[1:]


def _rms_bwd(name, dn, h, g, grad_in, comm=None):
    t, d = h.shape
    tt = _tile(t, 512)

    def body(dn_ref, h_ref, g_ref, gin_ref, gout_ref, gout16_ref, dg_ref):
        @pl.when(pl.program_id(0) == 0)
        def _():
            dg_ref[...] = jnp.zeros_like(dg_ref)

        dnv = dn_ref[...].astype(F32)
        r, xr = _rms(h_ref[...])
        dg_ref[...] += _colsum(dnv * xr)
        out = gin_ref[...] + _rms_grad(dnv, r, xr, g_ref[...])
        gout_ref[...] = out
        gout16_ref[...] = out.astype(BF16)

    row = pl.BlockSpec((tt, d), lambda i: (i, 0))
    vec = pl.BlockSpec((1, d), lambda i: (0, 0))
    return _call(
        body, name=name, grid=(t // tt,), in_specs=[row, row, vec, row], out_specs=[row, row, vec],
        out_shape=[jax.ShapeDtypeStruct((t, d), F32), jax.ShapeDtypeStruct((t, d), BF16), jax.ShapeDtypeStruct((1, d), F32)],
        args=(dn, h, g, grad_in), comm=comm)


def _mixer_windows(z_ref, zh_ref, first, a1_s, cb_s, tt):
    sig = None
    if a1_s is None:
        sig = _sigmoid(z_ref[:, A_DIM:2 * A_DIM])
    else:
        for r in range(0, tt, min(tt, CHUNK_ROWS)):
            rows = slice(r, r + min(tt, CHUNK_ROWS))
            a1_s[A_HALO + r:A_HALO + r + min(tt, CHUNK_ROWS), :] = z_ref[rows, 0:A_DIM] * _sigmoid(z_ref[rows, A_DIM:2 * A_DIM])
        a1_h = zh_ref[:, 0:A_DIM] * _sigmoid(zh_ref[:, A_DIM:2 * A_DIM])
        a1_s[0:A_HALO, :] = jnp.where(first, 0.0, a1_h)
    cb_s[B_HALO:B_HALO + tt, :] = z_ref[:, 3 * A_DIM:4 * A_DIM] * z_ref[:, 4 * A_DIM:5 * A_DIM]
    cb_h = zh_ref[A_HALO - B_HALO:A_HALO, 3 * A_DIM:4 * A_DIM] * zh_ref[A_HALO - B_HALO:A_HALO, 4 * A_DIM:5 * A_DIM]
    cb_s[0:B_HALO, :] = jnp.where(first, 0.0, cb_h)
    return sig


def _causal_conv(win_s, w_ref, taps, halo, tt):
    base = halo - (taps - 1)
    acc = w_ref[0:1, :] * win_s[pl.ds(base, tt), :]
    for k in range(1, taps):
        acc = acc + w_ref[k:k + 1, :] * win_s[pl.ds(base + k, tt), :]
    return acc


SUBLANES = 8
LANE_BLOCK = 128
CONV_ROWS = 32
TAP_GRAD_ROWS = 64
MIXER_ROWS = 512
CHUNK_ROWS = 64
NORM_ROWS = 32
SHIFT_ROWS = A_HALO - SUBLANES


def _shifted_copies(win_s, sh_s, tt):
    rows = tt + SHIFT_ROWS
    for b in range(1, SUBLANES):
        for r in range(0, rows, CHUNK_ROWS):
            n = min(CHUNK_ROWS, rows - r)
            sh_s[b - 1, r:r + n, :] = win_s[pl.ds(r + b, n), :]


def _window_rows(win_s, sh_s, offset, rows, cols):
    b = offset % SUBLANES
    if b == 0:
        return win_s[pl.ds(offset, rows), cols]
    return sh_s[b - 1, pl.ds(offset - b, rows), cols]


def _blocks(tt, rows):
    rb = min(tt, rows)
    return rb, [(r, slice(lb * LANE_BLOCK, (lb + 1) * LANE_BLOCK))
                for lb in range(A_DIM // LANE_BLOCK) for r in range(0, tt, rb)]


def _conv_taps(win_s, sh_s, w_ref, offsets, out_s, tt, bias_ref=None):
    rb, blocks = _blocks(tt, CONV_ROWS)
    for r, cols in blocks:
        acc = w_ref[0:1, cols] * _window_rows(win_s, sh_s, r + offsets[0], rb, cols)
        for k in range(1, len(offsets)):
            acc = acc + w_ref[k:k + 1, cols] * _window_rows(win_s, sh_s, r + offsets[k], rb, cols)
        out_s[r:r + rb, cols] = acc if bias_ref is None else acc + bias_ref[:, cols]


A_CAUSAL = [A_HALO - (A_TAPS - 1) + k for k in range(A_TAPS)]
A_ANTICAUSAL = [A_TAPS - 1 - k for k in range(A_TAPS)]


def _layer_norm_stats(x):
    mu = jnp.mean(x, axis=-1, keepdims=True)
    xc = x - mu
    rstd = lax.rsqrt(jnp.mean(xc * xc, axis=-1, keepdims=True) + LN_EPS)
    return xc * rstd, rstd


def _mixer_specs(seq, tt):
    tiles_per_seq = seq // tt
    halo_blocks = tt // A_HALO
    z_spec = pl.BlockSpec((tt, IN_EVEN), lambda i: (i, 0))
    zh_spec = pl.BlockSpec((A_HALO, IN_EVEN), lambda i: (jnp.maximum(i * halo_blocks - 1, 0), 0))
    return tiles_per_seq, z_spec, zh_spec


def _vec_spec(rows, cols):
    return pl.BlockSpec((rows, cols), lambda i: (0, 0))


def _mixer_fwd(z, seq, caw, cab, lag, lab, cbw, comm=None):
    t = z.shape[0]
    tt = _tile(seq, MIXER_ROWS)
    tiles_per_seq, z_spec, zh_spec = _mixer_specs(seq, tt)

    def body(z_ref, zh_ref, caw_ref, cab_ref, lag_ref, lab_ref, cbw_ref, mix_ref, a2_ref, a1_s, cb_s, sh_s):
        first = (pl.program_id(0) % tiles_per_seq) == 0
        _mixer_windows(z_ref, zh_ref, first, a1_s, cb_s, tt)
        _shifted_copies(a1_s, sh_s, tt)
        _conv_taps(a1_s, sh_s, caw_ref, A_CAUSAL, a2_ref, tt, bias_ref=cab_ref)
        for r in range(0, tt, min(tt, NORM_ROWS)):
            rows = slice(r, r + min(tt, NORM_ROWS))
            xhat, _ = _layer_norm_stats(a2_ref[rows, :])
            a3 = xhat * lag_ref[...] + lab_ref[...]
            mix_ref[rows, 0:A_DIM] = (a3 * _sigmoid(a3)).astype(BF16)
        cv = _causal_conv(cb_s, cbw_ref, B_TAPS, B_HALO, tt)
        mix_ref[:, A_DIM:A_DIM + B_DIM] = (z_ref[:, 2 * A_DIM:3 * A_DIM] * cv).astype(BF16)

    return _call(
        body, name="mixer_fwd", grid=(t // tt,),
        in_specs=[z_spec, zh_spec, _vec_spec(A_TAPS, A_DIM), _vec_spec(1, A_DIM), _vec_spec(1, A_DIM), _vec_spec(1, A_DIM),
                  _vec_spec(B_TAPS, B_DIM)],
        out_specs=[pl.BlockSpec((tt, A_DIM + B_DIM), lambda i: (i, 0)), pl.BlockSpec((tt, A_DIM), lambda i: (i, 0))],
        out_shape=[jax.ShapeDtypeStruct((t, A_DIM + B_DIM), BF16), jax.ShapeDtypeStruct((t, A_DIM), F32)],
        scratch_shapes=[pltpu.VMEM((A_HALO + tt, A_DIM), F32), pltpu.VMEM((B_HALO + tt, B_DIM), F32),
                        pltpu.VMEM((SUBLANES - 1, tt + SHIFT_ROWS, A_DIM), F32)],
        args=(z, z, caw, cab, lag, lab, cbw), parallel=True, comm=comm)


def _mixer_bwd_local(z, a2, dmix, seq, lag, lab, comm=None):
    t = z.shape[0]
    tt = _tile(seq, MIXER_ROWS)
    tiles_per_seq, z_spec, zh_spec = _mixer_specs(seq, tt)

    def body(z_ref, zh_ref, a2_ref, dmix_ref, lag_ref, lab_ref,
             da2_ref, dcv_ref, dcaw_ref, dcab_ref, dlag_ref, dlab_ref, dcbw_ref, a1_s, cb_s, sh_s):
        @pl.when(pl.program_id(0) == 0)
        def _():
            for ref in (dcaw_ref, dcab_ref, dlag_ref, dlab_ref, dcbw_ref):
                ref[...] = jnp.zeros_like(ref)

        first = (pl.program_id(0) % tiles_per_seq) == 0
        _mixer_windows(z_ref, zh_ref, first, a1_s, cb_s, tt)
        _shifted_copies(a1_s, sh_s, tt)
        for r in range(0, tt, min(tt, NORM_ROWS)):
            rows = slice(r, r + min(tt, NORM_ROWS))
            xhat, rstd = _layer_norm_stats(a2_ref[rows, :])
            a3 = xhat * lag_ref[...] + lab_ref[...]
            s3 = _sigmoid(a3)
            da3 = dmix_ref[rows, 0:A_DIM] * (s3 * (1.0 + a3 * (1.0 - s3)))
            dlag_ref[...] += _colsum(da3 * xhat)
            dlab_ref[...] += _colsum(da3)
            dxh = da3 * lag_ref[...]
            da2 = rstd * (dxh - jnp.mean(dxh, axis=-1, keepdims=True) - xhat * jnp.mean(dxh * xhat, axis=-1, keepdims=True))
            da2_ref[rows, :] = da2
            dcab_ref[...] += _colsum(da2)
        rb, blocks = _blocks(tt, TAP_GRAD_ROWS)
        for r, cols in blocks:
            da2_b = da2_ref[r:r + rb, cols]
            for k in range(A_TAPS):
                dcaw_ref[k:k + 1, cols] += _colsum(da2_b * _window_rows(a1_s, sh_s, r + A_CAUSAL[k], rb, cols))
        dcv =dmix_ref[:, A_DIM:A_DIM + B_DIM] * z_ref[:, 2 * A_DIM:3 * A_DIM]
        dcv_ref[...] = dcv
        for k in range(B_TAPS):
            dcbw_ref[k:k + 1, :] += _colsum(dcv * cb_s[pl.ds(B_HALO - (B_TAPS - 1) + k, tt), :])

    half = pl.BlockSpec((tt, A_DIM), lambda i: (i, 0))
    return _call(
        body, name="mixer_bwd_local", grid=(t // tt,),
        in_specs=[z_spec, zh_spec, half, pl.BlockSpec((tt, A_DIM + B_DIM), lambda i: (i, 0)),
                  _vec_spec(1, A_DIM), _vec_spec(1, A_DIM)],
        out_specs=[half, half, _vec_spec(A_TAPS, A_DIM), _vec_spec(1, A_DIM), _vec_spec(1, A_DIM), _vec_spec(1, A_DIM),
                   _vec_spec(B_TAPS, B_DIM)],
        out_shape=[jax.ShapeDtypeStruct((t, A_DIM), F32), jax.ShapeDtypeStruct((t, B_DIM), F32),
                   jax.ShapeDtypeStruct((A_TAPS, A_DIM), F32), jax.ShapeDtypeStruct((1, A_DIM), F32),
                   jax.ShapeDtypeStruct((1, A_DIM), F32), jax.ShapeDtypeStruct((1, A_DIM), F32),
                   jax.ShapeDtypeStruct((B_TAPS, B_DIM), F32)],
        scratch_shapes=[pltpu.VMEM((A_HALO + tt, A_DIM), F32), pltpu.VMEM((B_HALO + tt, B_DIM), F32),
                        pltpu.VMEM((SUBLANES - 1, tt + SHIFT_ROWS, A_DIM), F32)],
        args=(z, z, a2, dmix, lag, lab), comm=comm)


def _mixer_bwd_input(z, dmix, da2, dcv, seq, caw, cbw, comm=None):
    t = z.shape[0]
    tt = _tile(seq, MIXER_ROWS)
    tiles_per_seq, z_spec, zh_spec = _mixer_specs(seq, tt)
    a_blocks = tt // A_HALO
    b_blocks = tt // B_HALO
    last_a = t // A_HALO - 1
    last_b = t // B_HALO - 1

    def body(z_ref, zh_ref, dmix_ref, da2_ref, da2n_ref, dcv_ref, dcvn_ref, caw_ref, cbw_ref, dz_ref, cb_s, da2_s, dcv_s,
             sh_s, da1_s):
        pos = pl.program_id(0) % tiles_per_seq
        first = pos == 0
        last = pos == tiles_per_seq - 1
        sig = _mixer_windows(z_ref, zh_ref, first, None, cb_s, tt)
        da2_s[0:tt, :] = da2_ref[...]
        da2_s[tt:tt + A_HALO, :] = jnp.where(last, 0.0, da2n_ref[...])
        dcv_s[0:tt, :] = dcv_ref[...]
        dcv_s[tt:tt + B_HALO, :] = jnp.where(last, 0.0, dcvn_ref[...])
        _shifted_copies(da2_s, sh_s, tt)
        _conv_taps(da2_s, sh_s, caw_ref, A_ANTICAUSAL, da1_s, tt)
        da1 = da1_s[...]
        dz_ref[:, 0:A_DIM] = (da1 * sig).astype(BF16)
        dz_ref[:, A_DIM:2 * A_DIM] = (da1 * z_ref[:, 0:A_DIM] * sig * (1.0 - sig)).astype(BF16)
        cv = _causal_conv(cb_s, cbw_ref, B_TAPS, B_HALO, tt)
        dz_ref[:, 2 * A_DIM:3 * A_DIM] = (dmix_ref[:, A_DIM:A_DIM + B_DIM] * cv).astype(BF16)
        dcb = cbw_ref[0:1, :] * dcv_s[pl.ds(B_TAPS - 1, tt), :]
        for k in range(1, B_TAPS):
            dcb = dcb + cbw_ref[k:k + 1, :] * dcv_s[pl.ds(B_TAPS - 1 - k, tt), :]
        dz_ref[:, 3 * A_DIM:4 * A_DIM] = (dcb * z_ref[:, 4 * A_DIM:5 * A_DIM]).astype(BF16)
        dz_ref[:, 4 * A_DIM:5 * A_DIM] = (dcb * z_ref[:, 3 * A_DIM:4 * A_DIM]).astype(BF16)

    half = pl.BlockSpec((tt, A_DIM), lambda i: (i, 0))
    return _call(
        body, name="mixer_bwd_input", grid=(t // tt,),
        in_specs=[z_spec, zh_spec, pl.BlockSpec((tt, A_DIM + B_DIM), lambda i: (i, 0)),
                  half, pl.BlockSpec((A_HALO, A_DIM), lambda i: (jnp.minimum((i + 1) * a_blocks, last_a), 0)),
                  half, pl.BlockSpec((B_HALO, B_DIM), lambda i: (jnp.minimum((i + 1) * b_blocks, last_b), 0)),
                  _vec_spec(A_TAPS, A_DIM), _vec_spec(B_TAPS, B_DIM)],
        out_specs=[pl.BlockSpec((tt, IN_EVEN), lambda i: (i, 0))],
        out_shape=[jax.ShapeDtypeStruct((t, IN_EVEN), BF16)],
        scratch_shapes=[pltpu.VMEM((B_HALO + tt, B_DIM), F32),
                        pltpu.VMEM((tt + A_HALO, A_DIM), F32), pltpu.VMEM((tt + B_HALO, B_DIM), F32),
                        pltpu.VMEM((SUBLANES - 1, tt + SHIFT_ROWS, A_DIM), F32), pltpu.VMEM((tt, A_DIM), F32)],
        args=(z, z, dmix, da2, da2, dcv, dcv, caw, cbw), parallel=True, comm=comm)[0]


def _tril_ws(ws_ref, g):
    rows = lax.broadcasted_iota(jnp.int32, (CHUNK, CHUNK), 0)
    cols = lax.broadcasted_iota(jnp.int32, (CHUNK, CHUNK), 1)
    return jnp.where(rows >= cols, ws_ref[g], 0.0).astype(BF16), rows >= cols


def _sgu_fwd(pre, lvg, lvb, ws, bs_b, comm=None):
    t = pre.shape[0]
    tt = _tile(t, 256)

    def body(pre_ref, lvg_ref, lvb_ref, ws_ref, bsb_ref, y_ref):
        vhat, _ = _layer_norm_stats(_gelu(pre_ref[:, C_DIM:2 * C_DIM]))
        vl = (vhat * lvg_ref[...] + lvb_ref[...]).astype(BF16)
        for g in range(C_GROUPS):
            w, _ = _tril_ws(ws_ref, g)
            cols = slice(g * CHUNK, (g + 1) * CHUNK)
            for ci in range(tt // CHUNK):
                rows = slice(ci * CHUNK, (ci + 1) * CHUNK)
                sv = jnp.dot(w, vl[rows, cols], preferred_element_type=F32) + bsb_ref[g]
                y_ref[rows, cols] = (_gelu(pre_ref[rows, cols]) * sv).astype(BF16)

    group = pl.BlockSpec((C_GROUPS, CHUNK, CHUNK), lambda i: (0, 0, 0))
    return _call(
        body, name="sgu_fwd", grid=(t // tt,),
        in_specs=[pl.BlockSpec((tt, 2 * C_DIM), lambda i: (i, 0)), _vec_spec(1, C_DIM), _vec_spec(1, C_DIM), group, group],
        out_specs=[pl.BlockSpec((tt, C_DIM), lambda i: (i, 0))], out_shape=[jax.ShapeDtypeStruct((t, C_DIM), BF16)],
        args=(pre, lvg, lvb, ws, bs_b), parallel=True, comm=comm)[0]


def _sgu_bwd(pre, dy, lvg, lvb, ws, bs_b, comm=None):
    t = pre.shape[0]
    tt = _tile(t, 256)

    def body(pre_ref, dy_ref, lvg_ref, lvb_ref, ws_ref, bsb_ref, dpre_ref, dws_ref, dbsb_ref, dlvg_ref, dlvb_ref, dbin_ref,
             dvl_s):
        @pl.when(pl.program_id(0) == 0)
        def _():
            for ref in (dws_ref, dbsb_ref, dlvg_ref, dlvb_ref, dbin_ref):
                ref[...] = jnp.zeros_like(ref)

        v, v_grad = _gelu_and_grad(pre_ref[:, C_DIM:2 * C_DIM])
        vhat, rstd = _layer_norm_stats(v)
        vl = (vhat * lvg_ref[...] + lvb_ref[...]).astype(BF16)
        for g in range(C_GROUPS):
            w, keep = _tril_ws(ws_ref, g)
            cols = slice(g * CHUNK, (g + 1) * CHUNK)
            dws = jnp.zeros((CHUNK, CHUNK), F32)
            dbs = jnp.zeros((CHUNK, 1), F32)
            for ci in range(tt // CHUNK):
                rows = slice(ci * CHUNK, (ci + 1) * CHUNK)
                vl_g = vl[rows, cols]
                sv = jnp.dot(w, vl_g, preferred_element_type=F32) + bsb_ref[g]
                u, u_grad = _gelu_and_grad(pre_ref[rows, cols])
                dyv = dy_ref[rows, cols]
                du = dyv * sv * u_grad
                dpre_ref[rows, cols] = du.astype(BF16)
                dbin_ref[:, cols] += _colsum(du)
                dsv = dyv * u
                dbs = dbs + jnp.sum(dsv, axis=1, keepdims=True)
                dsv16 = dsv.astype(BF16)
                dws = dws + lax.dot_general(dsv16, vl_g, (((1,), (1,)), ((), ())), preferred_element_type=F32)
                dvl_s[rows, cols] = lax.dot_general(w, dsv16, (((0,), (0,)), ((), ())), preferred_element_type=F32)
            dws_ref[g] += jnp.where(keep, dws, 0.0)
            dbsb_ref[g] += dbs
        dvl = dvl_s[...]
        dlvg_ref[...] += _colsum(dvl * vhat)
        dlvb_ref[...] += _colsum(dvl)
        dxh = dvl * lvg_ref[...]
        dv = rstd * (dxh - jnp.mean(dxh, axis=-1, keepdims=True) - vhat * jnp.mean(dxh * vhat, axis=-1, keepdims=True))
        dpv = dv * v_grad
        dpre_ref[:, C_DIM:2 * C_DIM] = dpv.astype(BF16)
        dbin_ref[:, C_DIM:2 * C_DIM] += _colsum(dpv)

    group = pl.BlockSpec((C_GROUPS, CHUNK, CHUNK), lambda i: (0, 0, 0))
    return _call(
        body, name="sgu_bwd", grid=(t // tt,),
        in_specs=[pl.BlockSpec((tt, 2 * C_DIM), lambda i: (i, 0)), pl.BlockSpec((tt, C_DIM), lambda i: (i, 0)),
                  _vec_spec(1, C_DIM), _vec_spec(1, C_DIM), group, group],
        out_specs=[pl.BlockSpec((tt, 2 * C_DIM), lambda i: (i, 0)), group, group,
                   _vec_spec(1, C_DIM), _vec_spec(1, C_DIM), _vec_spec(1, 2 * C_DIM)],
        out_shape=[jax.ShapeDtypeStruct((t, 2 * C_DIM), BF16), jax.ShapeDtypeStruct((C_GROUPS, CHUNK, CHUNK), F32),
                   jax.ShapeDtypeStruct((C_GROUPS, CHUNK, CHUNK), F32), jax.ShapeDtypeStruct((1, C_DIM), F32),
                   jax.ShapeDtypeStruct((1, C_DIM), F32), jax.ShapeDtypeStruct((1, 2 * C_DIM), F32)],
        scratch_shapes=[pltpu.VMEM((tt, C_DIM), F32)],
        args=(pre, dy, lvg, lvb, ws, bs_b), comm=comm)


def _pair_sum(name, part, got, core):
    _, k, n = part.shape
    tk = _tile(k, 1024)

    def body(core_ref, p_ref, s_ref, o_ref):
        o_ref[...] = (p_ref[...] + s_ref[...]).astype(BF16)

    return pl.pallas_call(
        body, name=name,
        grid_spec=pltpu.PrefetchScalarGridSpec(
            num_scalar_prefetch=1, grid=(N_CHIP, k // tk),
            in_specs=[pl.BlockSpec((None, tk, n), lambda q, i, core_ref: (2 * q + core_ref[0], i, 0)),
                      pl.BlockSpec((None, tk, n), lambda q, i, core_ref: (q, i, 0))],
            out_specs=pl.BlockSpec((None, tk, n), lambda q, i, core_ref: (q, i, 0))),
        out_shape=jax.ShapeDtypeStruct((N_CHIP, k, n), BF16),
        compiler_params=pltpu.CompilerParams(dimension_semantics=("parallel", "parallel"), vmem_limit_bytes=VMEM_LIMIT_BYTES),
    )(core, part, got)


def _adamw_math(w, g, m, v):
    m = ADAM_B1 * m + (1.0 - ADAM_B1) * g
    v = ADAM_B2 * v + (1.0 - ADAM_B2) * (g * g)
    m_hat = m / (1.0 - ADAM_B1 ** ADAM_STEP)
    v_hat = v / (1.0 - ADAM_B2 ** ADAM_STEP)
    delta = -ADAM_LR * (m_hat / (jnp.sqrt(v_hat) + ADAM_EPS) + ADAM_WD * w)
    return delta, m, v


def _sum_adamw(name, parts, w, m, v, comm=None):
    layers = len(parts)
    n_parts, k, n = parts[0].shape
    tk = _tile(k, 256)

    def body(*refs):
        p_refs = refs[:layers]
        w_ref, m_ref, v_ref, g_ref, d_ref, nm_ref, nv_ref = refs[layers:]

        def total(p_ref):
            g = p_ref[0].astype(F32)
            for q in range(1, n_parts):
                g = g + p_ref[q].astype(F32)
            return g

        g = total(p_refs[0])
        for l in range(1, layers):
            g = jnp.where(pl.program_id(0) == l, total(p_refs[l]), g)
        g_ref[...] = g
        d_ref[...], nm_ref[...], nv_ref[...] = _adamw_math(w_ref[...], g, m_ref[...], v_ref[...])

    blk = pl.BlockSpec((None, tk, n), lambda l, i: (l, i, 0))
    return _call(
        body, name=name, grid=(layers, k // tk),
        in_specs=[pl.BlockSpec((n_parts, tk, n), lambda l, i: (0, i, 0))] * layers + [blk, blk, blk], out_specs=[blk] * 4,
        out_shape=[jax.ShapeDtypeStruct((layers, k, n), F32)] * 4, args=(*parts, w, m, v), parallel=True, comm=comm)


def _small_adamw(name, parts, w, m, v, losses):
    count = len(parts)

    def in_order(ref):
        total = ref[0]
        for dev in range(1, N_DEV):
            total = total + ref[dev]
        return total

    def body(*refs):
        p_refs, w_refs, m_refs, v_refs = (refs[j * count:(j + 1) * count] for j in range(4))
        losses_ref = refs[4 * count]
        g_refs, d_refs, nm_refs, nv_refs = (refs[4 * count + 1 + j * count:4 * count + 1 + (j + 1) * count] for j in range(4))
        loss_ref = refs[8 * count + 1]
        for i in range(count):
            g = in_order(p_refs[i])
            g_refs[i][...] = g
            d_refs[i][...], nm_refs[i][...], nv_refs[i][...] = _adamw_math(w_refs[i][...], g, m_refs[i][...], v_refs[i][...])
        loss_ref[...] = in_order(losses_ref)

    res = pl.pallas_call(
        body, name=name,
        out_shape=[jax.ShapeDtypeStruct(a.shape, F32) for a in w] * 4 + [jax.ShapeDtypeStruct(losses.shape[1:], F32)],
        compiler_params=pltpu.CompilerParams(vmem_limit_bytes=VMEM_LIMIT_BYTES))(*parts, *w, *m, *v, losses)
    return [res[j * count:(j + 1) * count] for j in range(4)], res[4 * count]


def _rows(a):
    return a.reshape(-1, a.shape[-1])


def _whole(gathered):
    return jnp.transpose(gathered, (1, 0, 2)).reshape(gathered.shape[1], -1)


SMALL =("ev_norm_g", "ev_conv_a_w", "ev_conv_a_b", "ev_ln_a_g", "ev_ln_a_b", "ev_conv_b_w", "od_norm_g", "od_b_in",
         "od_ln_v_g", "od_ln_v_b", "od_w_s", "od_b_s", "mlp_norm_g", "final_norm_g")
SMALL_SHARDED = ("ev_conv_a_w", "ev_conv_b_w", "od_norm_g", "od_b_in", "od_ln_v_g", "od_ln_v_b")
ORDER = ("ev_norm_g", "ev_w_in", "ev_conv_a_w", "ev_conv_a_b", "ev_ln_a_g", "ev_ln_a_b", "ev_conv_b_w", "ev_w_out",
         "od_norm_g", "od_w_in", "od_b_in", "od_ln_v_g", "od_ln_v_b", "od_w_s", "od_b_s", "od_w_out", "mlp_norm_g",
         "mlp_w1", "mlp_w2", "final_norm_g")


def kernel(x, ev_norm_g, ev_w_in, ev_conv_a_w, ev_conv_a_b, ev_ln_a_g, ev_ln_a_b, ev_conv_b_w, ev_w_out, od_norm_g, od_w_in, od_b_in, od_ln_v_g, od_ln_v_b, od_w_s, od_b_s, od_w_out, mlp_norm_g, mlp_w1, mlp_w2, final_norm_g, loss_target, m_ev_norm_g, m_ev_w_in, m_ev_conv_a_w, m_ev_conv_a_b, m_ev_ln_a_g, m_ev_ln_a_b, m_ev_conv_b_w, m_ev_w_out, m_od_norm_g, m_od_w_in, m_od_b_in, m_od_ln_v_g, m_od_ln_v_b, m_od_w_s, m_od_b_s, m_od_w_out, m_mlp_norm_g, m_mlp_w1, m_mlp_w2, m_final_norm_g, v_ev_norm_g, v_ev_w_in, v_ev_conv_a_w, v_ev_conv_a_b, v_ev_ln_a_g, v_ev_ln_a_b, v_ev_conv_b_w, v_ev_w_out, v_od_norm_g, v_od_w_in, v_od_b_in, v_od_ln_v_g, v_od_ln_v_b, v_od_w_s, v_od_b_s, v_od_w_out, v_mlp_norm_g, v_mlp_w1, v_mlp_w2, v_final_norm_g):
    W = dict(ev_norm_g=ev_norm_g, ev_w_in=ev_w_in, ev_conv_a_w=ev_conv_a_w, ev_conv_a_b=ev_conv_a_b, ev_ln_a_g=ev_ln_a_g,
             ev_ln_a_b=ev_ln_a_b, ev_conv_b_w=ev_conv_b_w, ev_w_out=ev_w_out, od_norm_g=od_norm_g, od_w_in=od_w_in,
             od_b_in=od_b_in, od_ln_v_g=od_ln_v_g, od_ln_v_b=od_ln_v_b, od_w_s=od_w_s, od_b_s=od_b_s, od_w_out=od_w_out,
             mlp_norm_g=mlp_norm_g, mlp_w1=mlp_w1, mlp_w2=mlp_w2, final_norm_g=final_norm_g)
    M = dict(ev_norm_g=m_ev_norm_g, ev_w_in=m_ev_w_in, ev_conv_a_w=m_ev_conv_a_w, ev_conv_a_b=m_ev_conv_a_b,
             ev_ln_a_g=m_ev_ln_a_g, ev_ln_a_b=m_ev_ln_a_b, ev_conv_b_w=m_ev_conv_b_w, ev_w_out=m_ev_w_out,
             od_norm_g=m_od_norm_g, od_w_in=m_od_w_in, od_b_in=m_od_b_in, od_ln_v_g=m_od_ln_v_g, od_ln_v_b=m_od_ln_v_b,
             od_w_s=m_od_w_s, od_b_s=m_od_b_s, od_w_out=m_od_w_out, mlp_norm_g=m_mlp_norm_g, mlp_w1=m_mlp_w1,
             mlp_w2=m_mlp_w2, final_norm_g=m_final_norm_g)
    V = dict(ev_norm_g=v_ev_norm_g, ev_w_in=v_ev_w_in, ev_conv_a_w=v_ev_conv_a_w, ev_conv_a_b=v_ev_conv_a_b,
             ev_ln_a_g=v_ev_ln_a_g, ev_ln_a_b=v_ev_ln_a_b, ev_conv_b_w=v_ev_conv_b_w, ev_w_out=v_ev_w_out,
             od_norm_g=v_od_norm_g, od_w_in=v_od_w_in, od_b_in=v_od_b_in, od_ln_v_g=v_od_ln_v_g, od_ln_v_b=v_od_ln_v_b,
             od_w_s=v_od_w_s, od_b_s=v_od_b_s, od_w_out=v_od_w_out, mlp_norm_g=v_mlp_norm_g, mlp_w1=v_mlp_w1,
             mlp_w2=v_mlp_w2, final_norm_g=v_final_norm_g)

    n_seq, seq, d = x.shape
    t = n_seq * seq
    dev = 4 * lax.axis_index("x") + 2 * lax.axis_index("y") + lax.axis_index("c")
    core = lax.axis_index("c").astype(jnp.int32).reshape(1)

    ev_g, cab, lag, lab = W["ev_norm_g"], W["ev_conv_a_b"], W["ev_ln_a_g"], W["ev_ln_a_b"]
    ws = W["od_w_s"][0]
    bs_b = jnp.broadcast_to(W["od_b_s"][0][:, :, None], (C_GROUPS, CHUNK, CHUNK))
    mlp_g = [W["mlp_norm_g"][l:l + 1] for l in range(2)]
    fin_g = W["final_norm_g"].reshape(1, d)

    h0 = x.reshape(t, d)
    for store in (W, M, V):
        store["ev_w_in"] = jnp.transpose(store["ev_w_in"], (0, 2, 1))
    gather = _gather_comm([W["ev_w_in"][0].astype(BF16)])
    later = [(W["ev_w_out"], 0), (W["od_w_in"], 0), (W["od_w_out"], 0), (W["mlp_w1"], 0), (W["mlp_w2"], 0),
             (W["mlp_w1"], 1), (W["mlp_w2"], 1)]
    n0, (w_ev_out16, w_od_in16, w_od_out16, w1_0, w2_0, w1_1, w2_1) = _rms_fwd("ev_norm", h0, ev_g, comm=gather, casts=later)
    w_ev_in_t = gather.out[0].reshape(IN_EVEN, D_MODEL)

    first, second = (0, 2), (1, 2)

    g_a, g_b = _gather_comm([w_ev_out16] + [_rows(W[n]) for n in SMALL_SHARDED]), _gather_comm([w1_0], first)
    z = _mm_nt("ev_in", n0, w_ev_in_t, IN_EVEN, _ep_store, [F32], comm=[g_a, g_b])[0]
    w_ev_out = g_a.out[0].reshape(D_MODEL, D_MODEL)
    caw, cbw, od_g, od_bin, lvg, lvb = [_whole(g) for g in g_a.out[1:]]

    g_c, g_d = _gather_comm([w1_0], second, into=g_b.out), _gather_comm([w2_0], first)
    mix, a2 = _mixer_fwd(z, seq, caw, cab, lag, lab, cbw, comm=[g_c, g_d])
    w1 = [_whole(g_c.out[0]), None]

    g_e = _gather_comm([w_od_in16], first)
    h1, n1 = _mm_nn("ev_out", mix, w_ev_out, d, _ep_residual_norm, [F32, BF16], extras=(h0,), rows=(mlp_g[0],), comm=g_e)

    g_f, g_f2 = _gather_comm([w2_0], second, into=g_d.out), _gather_comm([w_od_in16], second, into=g_e.out)
    q0 = _mm_nn("mlp0_up", n1, w1[0], D_FF, _ep_relu_sq, [BF16], comm=[g_f, g_f2])[0]
    w2 = [g_f.out[0].reshape(D_FF, D_MODEL), None]
    w_od_in = _whole(g_f2.out[0])

    g_g, g_g2 = _gather_comm([w1_1], first), _gather_comm([w_od_out16])
    h2, n2 = _mm_nn("mlp0_down", q0, w2[0], d, _ep_residual_norm, [F32, BF16], extras=(h1,), rows=(od_g,), comm=[g_g, g_g2])
    w_od_out = g_g2.out[0].reshape(D_MODEL, D_MODEL)
    g_h = _gather_comm([w1_1], second, into=g_g.out)
    pre = _mm_nn("od_in", n2, w_od_in, 2 * C_DIM, _ep_bias, [F32], rows=(od_bin,), comm=g_h)[0]
    w1[1] = _whole(g_h.out[0])
    g_i = _gather_comm([w2_1], (0, 4))
    y = _sgu_fwd(pre, lvg, lvb, ws, bs_b, comm=g_i)
    g_i2 = _gather_comm([w2_1], (1, 4), into=g_i.out)
    h3, n3 = _mm_nn("od_out", y, w_od_out, d, _ep_residual_norm, [F32, BF16], extras=(h2,), rows=(mlp_g[1],), comm=g_i2)
    g_j = _gather_comm([w2_1], second, into=g_i2.out)
    q1 = _mm_nn("mlp1_up", n3, w1[1], D_FF, _ep_relu_sq, [BF16], comm=g_j)[0]
    w2[1] = g_j.out[0].reshape(D_FF, D_MODEL)
    grad, grad16, d_fin_g, loss_part = _mm_nn(
        "mlp1_down", q1, w2[1], d, _ep_final_loss, [F32, BF16], extras=(h3, loss_target.reshape(t, d)), rows=(fin_g,),
        sums=(True, False), tm_want=FUSED_ROWS)

    by_chip = {}

    def swap(name, parts):
        comm = _pair_comm([parts])
        comm.parts, comm.weight = parts, name
        return comm

    def exchange(swapped, halves=False):
        sums = _pair_sum(f"pair_sum_{swapped.weight}", swapped.parts, swapped.out[0], core)
        if not halves:
            comm = _chip_comm([sums])
            comm.weight = swapped.weight
            return comm
        comm = _chip_comm([sums], first)
        comm.sums, comm.weight = sums, swapped.weight
        return comm

    def rest(comm):
        other = _chip_comm([comm.sums], second, into=comm.out)
        other.weight = comm.weight
        return other

    def done(comm):
        by_chip[comm.weight] = comm.out[0]

    dw2_1 = _mm_tn("mlp1_dw2", q1, grad16).reshape(N_DEV, D_FF // N_DEV, D_MODEL)
    s_a = swap("w2_1", dw2_1)
    dp = _mm_nt("mlp1_dq", grad16, w2[1], D_FF, _ep_relu_sq_grad, [BF16], extras=(q1,), comm=s_a)[0]
    c_a = exchange(s_a, halves=True)
    dw1_1 = _mm_tn("mlp1_dw1", n3, dp, col_blocks=N_DEV, comm=c_a)
    c_a2, s_b = rest(c_a), swap("w1_1", dw1_1)
    dn = _mm_nt("mlp1_dn", dp, w1[1], d, _ep_store, [BF16], comm=[c_a2, s_b])[0]
    done(c_a2)
    c_b = exchange(s_b, halves=True)
    grad, grad16, dg_mlp1 = _rms_bwd("mlp1_dn_norm", dn, h3, mlp_g[1], grad)
    dy, d_od_out = _mm_out_proj_grads("od_out_grads", grad16, w_od_out, y)
    s_c = swap("od_out", d_od_out.reshape(N_DEV, D_MODEL // N_DEV, D_MODEL))
    dpre, d_ws, d_bsb, d_lvg, d_lvb, d_bin = _sgu_bwd(pre, dy, lvg, lvb, ws, bs_b, comm=[c_b, s_c])
    c_c = exchange(s_c)
    c_b2 = rest(c_b)
    d_od_in = _mm_tn("od_dw_in", n2, dpre, col_blocks=N_DEV, comm=c_b2)
    done(c_b2)
    s_d = swap("od_in", d_od_in)
    dn = _mm_nt("od_dn", dpre, w_od_in, d, _ep_store, [BF16], comm=[s_d, c_c])[0]
    done(c_c)
    c_d = exchange(s_d)
    grad, grad16, d_od_g = _rms_bwd("od_dn_norm", dn, h2, od_g, grad)
    group_1 = dict(od_norm_g=d_od_g, od_b_in=d_bin, od_ln_v_g=d_lvg, od_ln_v_b=d_lvb,
                   od_w_s=d_ws.reshape(C_GROUPS * CHUNK, CHUNK), od_b_s=d_bsb[:, :, 0], final_norm_g=d_fin_g,
                   mlp_norm_g_1=dg_mlp1, loss=jnp.broadcast_to(loss_part, (SUBLANES, LANE_BLOCK)))
    gather_1 = _gather_comm(list(group_1.values()))
    dw2_0 = _mm_tn("mlp0_dw2", q0, grad16, comm=c_d).reshape(N_DEV, D_FF // N_DEV, D_MODEL)
    done(c_d)
    s_e = swap("w2_0", dw2_0)
    dp = _mm_nt("mlp0_dq", grad16, w2[0], D_FF, _ep_relu_sq_grad, [BF16], extras=(q0,), comm=[s_e, gather_1])[0]
    c_e = exchange(s_e, halves=True)
    dw1_0 = _mm_tn("mlp0_dw1", n1, dp, col_blocks=N_DEV, comm=c_e)
    c_e2, s_f = rest(c_e), swap("w1_0", dw1_0)
    dn = _mm_nt("mlp0_dn", dp, w1[0], d, _ep_store, [BF16], comm=[c_e2, s_f])[0]
    done(c_e2)
    c_f = exchange(s_f, halves=True)
    grad, grad16, dg_mlp0 = _rms_bwd("mlp0_dn_norm", dn, h1, mlp_g[0], grad)
    dmix, d_ev_out = _mm_out_proj_grads("ev_out_grads", grad16, w_ev_out, mix)
    s_g = swap("ev_out", d_ev_out.reshape(N_DEV, D_MODEL // N_DEV, D_MODEL))
    da2, dcv, d_caw, d_cab, d_lag, d_lab, d_cbw = _mixer_bwd_local(z, a2, dmix, seq, lag, lab, comm=[c_f, s_g])
    c_g = exchange(s_g)
    c_f2 = rest(c_f)
    dz = _mixer_bwd_input(z, dmix, da2, dcv, seq, caw, cbw, comm=[c_f2, c_g])
    done(c_f2)
    done(c_g)
    group_2 = dict(ev_conv_a_w=d_caw, ev_conv_a_b=d_cab, ev_ln_a_g=d_lag, ev_ln_a_b=d_lab, ev_conv_b_w=d_cbw,
                   mlp_norm_g_0=dg_mlp0)
    gather_2 = _gather_comm(list(group_2.values()))
    d_ev_in_t = _mm_tn("ev_dw_in", dz, n0, comm=gather_2).reshape(N_DEV, IN_EVEN // N_DEV, D_MODEL)
    s_h = swap("ev_in", d_ev_in_t)
    dn = _mm_nn("ev_dn", dz, w_ev_in_t, d, _ep_store, [BF16], comm=s_h)[0]
    c_h = exchange(s_h)
    grad_x, _, d_ev_g = _rms_bwd("ev_dn_norm", dn, h0, ev_g, grad, comm=c_h)
    done(c_h)

    last_gather = _gather_comm([d_ev_g])
    shard = {"od_w_out": ("od_out",), "mlp_w1": ("w1_0", "w1_1"), "mlp_w2": ("w2_0", "w2_1"), "od_w_in": ("od_in",),
             "ev_w_out": ("ev_out",), "ev_w_in": ("ev_in",)}
    carried = {"od_w_out": last_gather}
    out_g, out_d, out_m, out_v = {}, {}, {}, {}
    for name, keys in shard.items():
        out_g[name], out_d[name], out_m[name], out_v[name] = _sum_adamw(
            f"adamw_{name}", [by_chip[key] for key in keys], W[name], M[name], V[name], comm=carried.get(name))
    for store in (out_g, out_d, out_m, out_v):
        store["ev_w_in"] = jnp.transpose(store["ev_w_in"], (0, 2, 1))

    gathered = dict(zip(group_1, gather_1.out), **dict(zip(group_2, gather_2.out)), ev_norm_g=last_gather.out[0])
    gathered["mlp_norm_g"] = jnp.concatenate([gathered["mlp_norm_g_0"], gathered["mlp_norm_g_1"]], axis=1)
    mine = []
    for n in SMALL:
        g = gathered[n]
        if n in SMALL_SHARDED:
            width = W[n].shape[-1]
            g = lax.dynamic_slice_in_dim(g, dev * width, width, axis=2)
        mine.append(g)
    res, loss_tile = _small_adamw("adamw_small", mine, [_rows(W[n]) for n in SMALL], [_rows(M[n]) for n in SMALL],
                                  [_rows(V[n]) for n in SMALL], gathered["loss"])
    loss = loss_tile[0, 0]
    for store, values in zip((out_g, out_d, out_m, out_v), res):
        for n, value in zip(SMALL, values):
            store[n] = value.reshape(W[n].shape)

    return (loss, grad_x.reshape(n_seq, seq, d), *[out_g[n] for n in ORDER], *[out_d[n] for n in ORDER],
            *[out_m[n] for n in ORDER], *[out_v[n] for n in ORDER])
```
